```python
import jax, jax.numpy as jnp
from jax import lax
import numpy as np

D_MODEL = 2048
BATCH = 8
SEQ = 2048
DEPTH = 2

N_META = 16
MLSTM_W = D_MODEL // 2
CONV_W = D_MODEL - MLSTM_W
MLSTM_HEADS = 4
DV = MLSTM_W // MLSTM_HEADS
DQK = DV // 2
QK_W = MLSTM_HEADS * DQK
CHUNK = 64
CONV_K = 3
D_FF = -(-8 * D_MODEL // (3 * 256)) * 256
GATE_CAP = 15.0
EPS = 1e-6
SPLIT_SIZES = (QK_W, QK_W, MLSTM_W, MLSTM_W, MLSTM_HEADS, MLSTM_HEADS, CONV_W, CONV_W, CONV_W)
D_IN = sum(SPLIT_SIZES)

kernel_name = "hymba_mlstm_shortconv_swiglu"


def rmsnorm(x, w):
    xf = x.astype(jnp.float32)
    y = xf * lax.rsqrt(jnp.mean(xf * xf, axis=-1, keepdims=True) + EPS)
    return (y * w.astype(jnp.float32)).astype(x.dtype)


def mlstm_chunkwise(q, k, v, log_i, log_f):
    b_, h_, t_, _ = q.shape
    nc = t_ // CHUNK

    def to_chunks(a):
        return jnp.moveaxis(a.reshape(a.shape[:2] + (nc, CHUNK) + a.shape[3:]), 2, 0)

    causal = jnp.tril(jnp.ones((CHUNK, CHUNK), dtype=bool))

    def step(carry, inp):
        c_st, n_st, m_st = carry
        qb, kb, vb, li, lf = inp
        b = jnp.cumsum(lf, axis=-1)
        dmat = jnp.where(causal, b[..., :, None] - b[..., None, :] + li[..., None, :], -jnp.inf)
        inter = b + m_st[..., None]
        m_t = jnp.maximum(inter, jnp.max(dmat, axis=-1))
        w_inter = jnp.exp(inter - m_t)
        s_w = jnp.einsum('bhtd,bhsd->bhts', qb, kb) * jnp.exp(dmat - m_t[..., None])
        num = (w_inter[..., None] * jnp.einsum('bhtd,bhde->bhte', qb, c_st)
               + jnp.einsum('bhts,bhse->bhte', s_w, vb))
        den = w_inter * jnp.einsum('bhtd,bhd->bht', qb, n_st) + jnp.sum(s_w, axis=-1)
        h = num / jnp.maximum(jnp.abs(den), jnp.exp(-m_t))[..., None]
        b_end = b[..., -1]
        decay = b_end[..., None] - b + li
        m_new = jnp.maximum(b_end + m_st, jnp.max(decay, axis=-1))
        w_old = jnp.exp(b_end + m_st - m_new)
        w_in = jnp.exp(decay - m_new[..., None])
        c_new = w_old[..., None, None] * c_st + jnp.einsum('bhs,bhsd,bhse->bhde', w_in, kb, vb)
        n_new = w_old[..., None] * n_st + jnp.einsum('bhs,bhsd->bhd', w_in, kb)
        return (c_new, n_new, m_new), h

    init = (jnp.zeros((b_, h_, q.shape[-1], v.shape[-1]), jnp.float32),
            jnp.zeros((b_, h_, q.shape[-1]), jnp.float32),
            jnp.zeros((b_, h_), jnp.float32))
    _, hs = lax.scan(step, init, tuple(map(to_chunks, (q, k, v, log_i, log_f))))
    return jnp.moveaxis(hs, 0, 2).reshape(b_, h_, t_, v.shape[-1])


def mlstm_group(q, k, v, i_raw, f_raw):
    seq_len = q.shape[1]
    pad_front = (-N_META) % CHUNK
    pad_back = (-(pad_front + seq_len)) % CHUNK
    tr = lambda a: jnp.moveaxis(a.astype(jnp.float32), 1, 2)
    q, k, v = tr(q) * (DQK ** -0.5), tr(k), tr(v)
    log_i = tr(GATE_CAP * jnp.tanh(i_raw.astype(jnp.float32) / GATE_CAP))
    log_f = jax.nn.log_sigmoid(tr(GATE_CAP * jnp.tanh(f_raw.astype(jnp.float32) / GATE_CAP)))
    pad4 = ((0, 0), (0, 0), (pad_front, pad_back), (0, 0))
    pad3 = ((0, 0), (0, 0), (pad_front, pad_back))
    q, k, v = jnp.pad(q, pad4), jnp.pad(k, pad4), jnp.pad(v, pad4)
    log_i = jnp.pad(log_i, pad3, constant_values=-jnp.inf)
    log_f = jnp.pad(log_f, pad3)
    h = mlstm_chunkwise(q, k, v, log_i, log_f)
    return h[:, :, pad_front:pad_front + seq_len]


def short_conv_group(u, gate_b, gate_c, conv_w):
    a = gate_c * u
    seq_len = a.shape[1]
    ap = jnp.pad(a, ((0, 0), (CONV_K - 1, 0), (0, 0)))
    conv = sum(ap[:, j:j + seq_len] * conv_w[j] for j in range(CONV_K))
    return gate_b * conv


def _fwd_setup_inputs(seed: int = 0) -> dict:
    key = jax.random.key(seed)
    ks = jax.random.split(key, 14)
    nrm = lambda k, shape, s: jax.random.normal(k, shape, jnp.float32) * s
    gain = lambda k, shape: 1.0 + 0.02 * jax.random.normal(k, shape, jnp.float32)
    b_i = nrm(ks[4], (DEPTH, MLSTM_HEADS), 0.1)
    b_f = 3.0 + nrm(ks[5], (DEPTH, MLSTM_HEADS), 0.5)
    return {
        "x": nrm(ks[0], (BATCH, SEQ, D_MODEL), 1.0),
        "meta_tokens": nrm(ks[1], (N_META, D_MODEL), 1.0),
        "norm_mix_w": gain(ks[2], (DEPTH, D_MODEL)),
        "w_in": nrm(ks[3], (DEPTH, D_MODEL, D_IN), D_MODEL ** -0.5),
        "b_gates": jnp.concatenate([b_i, b_f], axis=-1),
        "conv_w": nrm(ks[6], (DEPTH, CONV_K, CONV_W), CONV_K ** -0.5),
        "mlstm_norm_w": gain(ks[7], (DEPTH, MLSTM_W)),
        "w_out": nrm(ks[8], (DEPTH, D_MODEL, D_MODEL), D_MODEL ** -0.5),
        "norm_ffn_w": gain(ks[9], (DEPTH, D_MODEL)),
        "w_gate": nrm(ks[10], (DEPTH, D_MODEL, D_FF), D_MODEL ** -0.5),
        "w_up": nrm(ks[11], (DEPTH, D_MODEL, D_FF), D_MODEL ** -0.5),
        "w_down": nrm(ks[12], (DEPTH, D_FF, D_MODEL), D_FF ** -0.5),
        "norm_final_w": gain(ks[13], (D_MODEL,)),
    }


def _fwd_reference(x, meta_tokens, norm_mix_w, w_in, b_gates, conv_w, mlstm_norm_w, w_out,
              norm_ffn_w, w_gate, w_up, w_down, norm_final_w):
    bsz = x.shape[0]
    meta = jnp.broadcast_to(meta_tokens.astype(x.dtype)[None], (bsz, N_META, D_MODEL))
    h = jnp.concatenate([meta, x], axis=1)
    seq_len = h.shape[1]
    split_points = np.cumsum(SPLIT_SIZES)[:-1].tolist()
    for l in range(DEPTH):
        hn = rmsnorm(h, norm_mix_w[l])
        proj = hn @ w_in[l]
        q, k, v, og, ig, fg, u, gb, gc = jnp.split(proj, split_points, axis=-1)
        ig = ig + b_gates[l, :MLSTM_HEADS]
        fg = fg + b_gates[l, MLSTM_HEADS:]
        hm = mlstm_group(q.reshape(bsz, seq_len, MLSTM_HEADS, DQK),
                         k.reshape(bsz, seq_len, MLSTM_HEADS, DQK),
                         v.reshape(bsz, seq_len, MLSTM_HEADS, DV), ig, fg)
        hm = rmsnorm(hm, mlstm_norm_w[l].reshape(MLSTM_HEADS, 1, DV))
        hm = jnp.moveaxis(hm, 1, 2).reshape(bsz, seq_len, MLSTM_W).astype(h.dtype)
        hm = jax.nn.sigmoid(og) * hm
        hc = short_conv_group(u, gb, gc, conv_w[l])
        h = h + jnp.concatenate([hm, hc], axis=-1) @ w_out[l]
        hf = rmsnorm(h, norm_ffn_w[l])
        h = h + (jax.nn.silu(hf @ w_gate[l]) * (hf @ w_up[l])) @ w_down[l]
    out = rmsnorm(h, norm_final_w)
    return out[:, N_META:]


import jax as _jax
import jax.numpy as _jnp

TWIN_FORMAT = 'train_step'
FWD_PARAMS = ['x', 'meta_tokens', 'norm_mix_w', 'w_in', 'b_gates', 'conv_w', 'mlstm_norm_w', 'w_out', 'norm_ffn_w', 'w_gate', 'w_up', 'w_down', 'norm_final_w']
TWIN_WEIGHTS = ['meta_tokens', 'norm_mix_w', 'w_in', 'b_gates', 'conv_w', 'mlstm_norm_w', 'w_out', 'norm_ffn_w', 'w_gate', 'w_up', 'w_down', 'norm_final_w']
TWIN_DIFF_INPUT = 'x'
TWIN_INPUTS = ['x', 'meta_tokens', 'norm_mix_w', 'w_in', 'b_gates', 'conv_w', 'mlstm_norm_w', 'w_out', 'norm_ffn_w', 'w_gate', 'w_up', 'w_down', 'norm_final_w', 'loss_target', 'm_meta_tokens', 'm_norm_mix_w', 'm_w_in', 'm_b_gates', 'm_conv_w', 'm_mlstm_norm_w', 'm_w_out', 'm_norm_ffn_w', 'm_w_gate', 'm_w_up', 'm_w_down', 'm_norm_final_w', 'v_meta_tokens', 'v_norm_mix_w', 'v_w_in', 'v_b_gates', 'v_conv_w', 'v_mlstm_norm_w', 'v_w_out', 'v_norm_ffn_w', 'v_w_gate', 'v_w_up', 'v_w_down', 'v_norm_final_w']
TWIN_OUTPUTS = ['loss', 'grad_x', 'grad_meta_tokens', 'grad_norm_mix_w', 'grad_w_in', 'grad_b_gates', 'grad_conv_w', 'grad_mlstm_norm_w', 'grad_w_out', 'grad_norm_ffn_w', 'grad_w_gate', 'grad_w_up', 'grad_w_down', 'grad_norm_final_w', 'delta_meta_tokens', 'delta_norm_mix_w', 'delta_w_in', 'delta_b_gates', 'delta_conv_w', 'delta_mlstm_norm_w', 'delta_w_out', 'delta_norm_ffn_w', 'delta_w_gate', 'delta_w_up', 'delta_w_down', 'delta_norm_final_w', 'new_m_meta_tokens', 'new_m_norm_mix_w', 'new_m_w_in', 'new_m_b_gates', 'new_m_conv_w', 'new_m_mlstm_norm_w', 'new_m_w_out', 'new_m_norm_ffn_w', 'new_m_w_gate', 'new_m_w_up', 'new_m_w_down', 'new_m_norm_final_w', 'new_v_meta_tokens', 'new_v_norm_mix_w', 'new_v_w_in', 'new_v_b_gates', 'new_v_conv_w', 'new_v_mlstm_norm_w', 'new_v_w_out', 'new_v_norm_ffn_w', 'new_v_w_gate', 'new_v_w_up', 'new_v_w_down', 'new_v_norm_final_w']
TWIN_LEAF_KINDS = {'loss': 'loss', 'grad_x': 'grad_x', 'grad_meta_tokens': 'grad_w', 'grad_norm_mix_w': 'grad_w', 'grad_w_in': 'grad_w', 'grad_b_gates': 'grad_w', 'grad_conv_w': 'grad_w', 'grad_mlstm_norm_w': 'grad_w', 'grad_w_out': 'grad_w', 'grad_norm_ffn_w': 'grad_w', 'grad_w_gate': 'grad_w', 'grad_w_up': 'grad_w', 'grad_w_down': 'grad_w', 'grad_norm_final_w': 'grad_w', 'delta_meta_tokens': 'delta_w', 'delta_norm_mix_w': 'delta_w', 'delta_w_in': 'delta_w', 'delta_b_gates': 'delta_w', 'delta_conv_w': 'delta_w', 'delta_mlstm_norm_w': 'delta_w', 'delta_w_out': 'delta_w', 'delta_norm_ffn_w': 'delta_w', 'delta_w_gate': 'delta_w', 'delta_w_up': 'delta_w', 'delta_w_down': 'delta_w', 'delta_norm_final_w': 'delta_w', 'new_m_meta_tokens': 'new_m', 'new_m_norm_mix_w': 'new_m', 'new_m_w_in': 'new_m', 'new_m_b_gates': 'new_m', 'new_m_conv_w': 'new_m', 'new_m_mlstm_norm_w': 'new_m', 'new_m_w_out': 'new_m', 'new_m_norm_ffn_w': 'new_m', 'new_m_w_gate': 'new_m', 'new_m_w_up': 'new_m', 'new_m_w_down': 'new_m', 'new_m_norm_final_w': 'new_m', 'new_v_meta_tokens': 'new_v', 'new_v_norm_mix_w': 'new_v', 'new_v_w_in': 'new_v', 'new_v_b_gates': 'new_v', 'new_v_conv_w': 'new_v', 'new_v_mlstm_norm_w': 'new_v', 'new_v_w_out': 'new_v', 'new_v_norm_ffn_w': 'new_v', 'new_v_w_gate': 'new_v', 'new_v_w_up': 'new_v', 'new_v_w_down': 'new_v', 'new_v_norm_final_w': 'new_v'}


def _forward(args):
    return _fwd_reference(*[args[k] for k in FWD_PARAMS])


def _output_shape():
    out = _jax.eval_shape(lambda: _forward(_fwd_setup_inputs(0)))
    return out.shape, out.dtype

N_MICROBATCH = 1
ADAM_LR = 0.001
ADAM_B1 = 0.9
ADAM_B2 = 0.999
ADAM_EPS = 1e-08
ADAM_WD = 0.01
ADAM_STEP = 10
PER_EXAMPLE_BATCH_AXIS = {'x': 0, 'loss_target': 0}
SHARED_INPUTS = []
_WEIGHT_DTYPES = {'meta_tokens': _jnp.float32, 'norm_mix_w': _jnp.float32, 'w_in': _jnp.float32, 'b_gates': _jnp.float32, 'conv_w': _jnp.float32, 'mlstm_norm_w': _jnp.float32, 'w_out': _jnp.float32, 'norm_ffn_w': _jnp.float32, 'w_gate': _jnp.float32, 'w_up': _jnp.float32, 'w_down': _jnp.float32, 'norm_final_w': _jnp.float32}
MOMENT_SCALE = {'meta_tokens': 3.228903e-03, 'norm_mix_w': 8.853901e-02, 'w_in': 4.906772e-02, 'b_gates': 1.610501e-01, 'conv_w': 5.970595e-02, 'mlstm_norm_w': 3.157768e-02, 'w_out': 4.680375e-02, 'norm_ffn_w': 4.178777e-02, 'w_gate': 1.828683e-02, 'w_up': 1.768889e-02, 'w_down': 2.934674e-02, 'norm_final_w': 8.006354e+00}


def _to_microbatches(a, axis):
    t = _jnp.moveaxis(a, axis, 0)
    t = t.reshape((N_MICROBATCH, t.shape[0] // N_MICROBATCH) + t.shape[1:])
    return _jnp.moveaxis(t, 1, axis + 1)


def setup_inputs(seed: int = 0) -> dict:
    inp = _fwd_setup_inputs(seed)
    key = _jax.random.fold_in(_jax.random.key(seed), 7919)
    shape, _ = _output_shape()
    out = dict(inp)
    out["loss_target"] = _jax.random.normal(_jax.random.fold_in(key, 0), shape, _jnp.float32)
    for i, name in enumerate(TWIN_WEIGHTS):
        w = inp[name].astype(_jnp.float32)
        if MOMENT_SCALE is None:
            s = _jnp.sqrt(_jnp.mean(_jnp.square(w)) + 1e-30)
        else:
            s = MOMENT_SCALE[name]
        km, kv = _jax.random.split(_jax.random.fold_in(key, i + 1))
        out[name] = w
        out["m_" + name] = s * _jax.random.normal(km, w.shape, _jnp.float32)
        out["v_" + name] = (s * s) * _jax.random.uniform(kv, w.shape, _jnp.float32, 0.5, 1.5)
    if N_MICROBATCH > 1:
        for name, axis in PER_EXAMPLE_BATCH_AXIS.items():
            out[name] = _to_microbatches(out[name], axis)
    return {'x': out['x'], 'meta_tokens': out['meta_tokens'], 'norm_mix_w': out['norm_mix_w'], 'w_in': out['w_in'], 'b_gates': out['b_gates'], 'conv_w': out['conv_w'], 'mlstm_norm_w': out['mlstm_norm_w'], 'w_out': out['w_out'], 'norm_ffn_w': out['norm_ffn_w'], 'w_gate': out['w_gate'], 'w_up': out['w_up'], 'w_down': out['w_down'], 'norm_final_w': out['norm_final_w'], 'loss_target': out['loss_target'], 'm_meta_tokens': out['m_meta_tokens'], 'm_norm_mix_w': out['m_norm_mix_w'], 'm_w_in': out['m_w_in'], 'm_b_gates': out['m_b_gates'], 'm_conv_w': out['m_conv_w'], 'm_mlstm_norm_w': out['m_mlstm_norm_w'], 'm_w_out': out['m_w_out'], 'm_norm_ffn_w': out['m_norm_ffn_w'], 'm_w_gate': out['m_w_gate'], 'm_w_up': out['m_w_up'], 'm_w_down': out['m_w_down'], 'm_norm_final_w': out['m_norm_final_w'], 'v_meta_tokens': out['v_meta_tokens'], 'v_norm_mix_w': out['v_norm_mix_w'], 'v_w_in': out['v_w_in'], 'v_b_gates': out['v_b_gates'], 'v_conv_w': out['v_conv_w'], 'v_mlstm_norm_w': out['v_mlstm_norm_w'], 'v_w_out': out['v_w_out'], 'v_norm_ffn_w': out['v_norm_ffn_w'], 'v_w_gate': out['v_w_gate'], 'v_w_up': out['v_w_up'], 'v_w_down': out['v_w_down'], 'v_norm_final_w': out['v_norm_final_w']}


def _loss(weights, diff, rest, loss_target):
    with _jax.named_scope("forward"):
        args = {**rest, TWIN_DIFF_INPUT: diff, **{k: w.astype(_WEIGHT_DTYPES[k]) for k, w in weights.items()}}
        y = _forward(args)
    with _jax.named_scope("loss_head"):
        err = _jnp.square(y.astype(_jnp.float32) - loss_target)
        return 0.5 * _jnp.sum(_jnp.mean(err, axis=-1)) if err.ndim else 0.5 * err


def _adamw(w, g, m, v):
    m = ADAM_B1 * m + (1.0 - ADAM_B1) * g
    v = ADAM_B2 * v + (1.0 - ADAM_B2) * _jnp.square(g)
    m_hat = m / (1.0 - ADAM_B1 ** ADAM_STEP)
    v_hat = v / (1.0 - ADAM_B2 ** ADAM_STEP)
    delta = -ADAM_LR * (m_hat / (_jnp.sqrt(v_hat) + ADAM_EPS) + ADAM_WD * w)
    return delta, m, v


def reference(x, meta_tokens, norm_mix_w, w_in, b_gates, conv_w, mlstm_norm_w, w_out, norm_ffn_w, w_gate, w_up, w_down, norm_final_w, loss_target, m_meta_tokens, m_norm_mix_w, m_w_in, m_b_gates, m_conv_w, m_mlstm_norm_w, m_w_out, m_norm_ffn_w, m_w_gate, m_w_up, m_w_down, m_norm_final_w, v_meta_tokens, v_norm_mix_w, v_w_in, v_b_gates, v_conv_w, v_mlstm_norm_w, v_w_out, v_norm_ffn_w, v_w_gate, v_w_up, v_w_down, v_norm_final_w):
    given = dict(x=x, meta_tokens=meta_tokens, norm_mix_w=norm_mix_w, w_in=w_in, b_gates=b_gates, conv_w=conv_w, mlstm_norm_w=mlstm_norm_w, w_out=w_out, norm_ffn_w=norm_ffn_w, w_gate=w_gate, w_up=w_up, w_down=w_down, norm_final_w=norm_final_w, loss_target=loss_target, m_meta_tokens=m_meta_tokens, m_norm_mix_w=m_norm_mix_w, m_w_in=m_w_in, m_b_gates=m_b_gates, m_conv_w=m_conv_w, m_mlstm_norm_w=m_mlstm_norm_w, m_w_out=m_w_out, m_norm_ffn_w=m_norm_ffn_w, m_w_gate=m_w_gate, m_w_up=m_w_up, m_w_down=m_w_down, m_norm_final_w=m_norm_final_w, v_meta_tokens=v_meta_tokens, v_norm_mix_w=v_norm_mix_w, v_w_in=v_w_in, v_b_gates=v_b_gates, v_conv_w=v_conv_w, v_mlstm_norm_w=v_mlstm_norm_w, v_w_out=v_w_out, v_norm_ffn_w=v_norm_ffn_w, v_w_gate=v_w_gate, v_w_up=v_w_up, v_w_down=v_w_down, v_norm_final_w=v_norm_final_w)
    weights = {n: given[n] for n in TWIN_WEIGHTS}
    shared = {n: given[n] for n in SHARED_INPUTS}
    per_example = {n: given[n] for n in ['x']}
    grad_fn = _jax.value_and_grad(_loss, argnums=(0, 1))

    def one_microbatch(ex, loss_target):
        ex = dict(ex)
        diff = ex.pop(TWIN_DIFF_INPUT)
        return grad_fn(weights, diff, {**shared, **ex}, loss_target)

    if N_MICROBATCH == 1:
        loss, (grad_w, grad_x) = one_microbatch(per_example, given["loss_target"])
    else:
        def body(carry, xs):
            loss_sum, grad_sum = carry
            l_k, (gw_k, gx_k) = one_microbatch(xs[0], xs[1])
            with _jax.named_scope("update"):
                return (loss_sum + l_k, _jax.tree.map(_jnp.add, grad_sum, gw_k)), gx_k

        init = (_jnp.zeros((), _jnp.float32), _jax.tree.map(_jnp.zeros_like, weights))
        (loss, grad_w), grad_x = _jax.lax.scan(body, init, (per_example, given["loss_target"]))
    with _jax.named_scope("update"):
        delta_w, new_m, new_v = {}, {}, {}
        for n in TWIN_WEIGHTS:
            delta_w[n], new_m[n], new_v[n] = _adamw(weights[n], grad_w[n], given["m_" + n], given["v_" + n])
    return (loss, grad_x, *[grad_w[n] for n in TWIN_WEIGHTS], *[delta_w[n] for n in TWIN_WEIGHTS],
            *[new_m[n] for n in TWIN_WEIGHTS], *[new_v[n] for n in TWIN_WEIGHTS])
```

```python
import functools

import numpy as np
import jax
import jax.numpy as jnp
from jax import lax
from jax.experimental import pallas as pl
from jax.experimental.pallas import tpu as pltpu

F32 = jnp.float32
BF16 = jnp.bfloat16
MESH = pl.DeviceIdType.MESH

D_MODEL = 2048
DEPTH = 2
N_META = 16
MLSTM_W = 1024
CONV_W = 1024
HEADS = 4
DV = 256
DQK = 128
QK_W = 512
CHUNK = 64
PAD_FRONT = 48
TOK0 = PAD_FRONT + N_META
D_FF = 5632
N_DEV = 8
FF_SH = D_FF // N_DEV
D_IN = 6152
IN_SH = D_IN // N_DEV
OUT_SH = D_MODEL // N_DEV
GATE_COL = 3072
PM_W = GATE_COL + 128
GATE_CAP = 15.0
EPS = 1e-6
QSCALE = DQK ** -0.5

ADAM_LR = 0.001
ADAM_B1 = 0.9
ADAM_B2 = 0.999
ADAM_EPS = 1e-08
ADAM_WD = 0.01
ADAM_STEP = 10

V7X_VMEM_LIMIT = 50 * 1024 * 1024


def _params(**kw):
    return pltpu.CompilerParams(vmem_limit_bytes=V7X_VMEM_LIMIT, **kw)


def _tile(n, target, mult):
    best = None
    for t in range(mult, min(n, target) + 1, mult):
        if n % t == 0:
            best = t
    return best if best is not None else n


def _sigmoid(x):
    return 1.0 / (1.0 + jnp.exp(-x))


NN = ((1,), (0,))
NT = ((1,), (1,))
TN = ((0,), (0,))


def _matmul(name, a, b, out_shape, out_dtype, grid, a_bs, b_bs, o_bs, dims, nk, acc_shape=None,
            res=None, res_bs=None):
    has_res = res is not None

    def body(*refs):
        a_ref, b_ref = refs[0], refs[1]
        r_ref = refs[2] if has_res else None
        o_ref = refs[2 + has_res]
        x = lax.dot_general(a_ref[...], b_ref[...], (dims, ((), ())), preferred_element_type=F32)
        if nk == 1:
            if has_res:
                x = x + r_ref[...]
            o_ref[...] = x.astype(o_ref.dtype)
            return
        acc = refs[3 + has_res]
        k = pl.program_id(len(grid) - 1)

        @pl.when(k == 0)
        def _():
            acc[...] = (x + r_ref[...]) if has_res else x

        @pl.when(k > 0)
        def _():
            acc[...] += x

        @pl.when(k == nk - 1)
        def _():
            o_ref[...] = acc[...].astype(o_ref.dtype)

    ins = [a, b] + ([res] if has_res else [])
    specs = [a_bs, b_bs] + ([res_bs] if has_res else [])
    scratch = [pltpu.VMEM(acc_shape, F32)] if nk > 1 else []
    return pl.pallas_call(
        body, name=name, grid=grid, in_specs=specs, out_specs=o_bs,
        out_shape=jax.ShapeDtypeStruct(out_shape, out_dtype), scratch_shapes=scratch,
        compiler_params=_params(),
    )(*ins)


def _mm_nn(name, a, b, out_dtype, res=None, tm=1056, tn=512):
    r, k = a.shape
    n = b.shape[1]
    tm, tn = _tile(r, tm, 8), _tile(n, tn, 128)
    return _matmul(name, a, b, (r, n), out_dtype, (r // tm, n // tn, 1),
                   pl.BlockSpec((tm, k), lambda i, j, s: (i, 0)),
                   pl.BlockSpec((k, tn), lambda i, j, s: (0, j)),
                   pl.BlockSpec((tm, tn), lambda i, j, s: (i, j)), NN, 1,
                   res=res, res_bs=pl.BlockSpec((tm, tn), lambda i, j, s: (i, j)))


def _mm_nn_bcols(name, a, b3, out_dtype, tm=1056, tn=1024):
    r, k = a.shape
    e, _, n = b3.shape
    tm, tn = _tile(r, tm, 8), _tile(n, tn, 128)
    return _matmul(name, a, b3, (e, r, n), out_dtype, (r // tm, e, n // tn, 1),
                   pl.BlockSpec((tm, k), lambda i, g, j, s: (i, 0)),
                   pl.BlockSpec((None, k, tn), lambda i, g, j, s: (g, 0, j)),
                   pl.BlockSpec((None, tm, tn), lambda i, g, j, s: (g, i, j)), NN, 1)


def _mm_nn_ksum(name, a3, b3, res, out_dtype, tm=1056, tn=1024):
    e, r, kb = a3.shape
    n = b3.shape[2]
    tm, tn = _tile(r, tm, 8), _tile(n, tn, 128)
    return _matmul(name, a3, b3, (r, n), out_dtype, (r // tm, n // tn, e),
                   pl.BlockSpec((None, tm, kb), lambda i, j, s: (s, i, 0)),
                   pl.BlockSpec((None, kb, tn), lambda i, j, s: (s, 0, j)),
                   pl.BlockSpec((tm, tn), lambda i, j, s: (i, j)), NN, e, acc_shape=(tm, tn),
                   res=res, res_bs=pl.BlockSpec((tm, tn), lambda i, j, s: (i, j)))


def _mm_nt(name, a, b, out_dtype, res=None, tm=1056, tn=512, tk=640):
    r, k = a.shape
    n = b.shape[0]
    tm, tn, tk = _tile(r, tm, 8), _tile(n, tn, 128), _tile(k, tk, 128)
    nk = k // tk
    return _matmul(name, a, b, (r, n), out_dtype, (r // tm, n // tn, nk),
                   pl.BlockSpec((tm, tk), lambda i, j, s: (i, s)),
                   pl.BlockSpec((tn, tk), lambda i, j, s: (j, s)),
                   pl.BlockSpec((tm, tn), lambda i, j, s: (i, j)), NT, nk, acc_shape=(tm, tn),
                   res=res, res_bs=pl.BlockSpec((tm, tn), lambda i, j, s: (i, j)))


def _mm_nt_bcols(name, a, b3, out_dtype, tm=1056):
    r, k = a.shape
    e, n, _ = b3.shape
    tm = _tile(r, tm, 8)
    return _matmul(name, a, b3, (e, r, n), out_dtype, (r // tm, e, 1),
                   pl.BlockSpec((tm, k), lambda i, g, s: (i, 0)),
                   pl.BlockSpec((None, n, k), lambda i, g, s: (g, 0, 0)),
                   pl.BlockSpec((None, tm, n), lambda i, g, s: (g, i, 0)), NT, 1)


def _mm_nt_ksum(name, a3, b3, out_dtype, res=None, tm=1056, tn=1024):
    e, r, kb = a3.shape
    n = b3.shape[1]
    tm, tn = _tile(r, tm, 8), _tile(n, tn, 128)
    return _matmul(name, a3, b3, (r, n), out_dtype, (r // tm, n // tn, e),
                   pl.BlockSpec((None, tm, kb), lambda i, j, s: (s, i, 0)),
                   pl.BlockSpec((None, tn, kb), lambda i, j, s: (s, j, 0)),
                   pl.BlockSpec((tm, tn), lambda i, j, s: (i, j)), NT, e, acc_shape=(tm, tn),
                   res=res, res_bs=pl.BlockSpec((tm, tn), lambda i, j, s: (i, j)))


def _mm_tn(name, a, b, out_dtype, tm=1024, tn=640):
    r, m = a.shape
    n = b.shape[1]
    tm, tn = _tile(m, tm, 128), _tile(n, tn, 128)
    return _matmul(name, a, b, (m, n), out_dtype, (m // tm, n // tn, 1),
                   pl.BlockSpec((r, tm), lambda i, j, s: (0, i)),
                   pl.BlockSpec((r, tn), lambda i, j, s: (0, j)),
                   pl.BlockSpec((tm, tn), lambda i, j, s: (i, j)), TN, 1)


def _mm_tn_bcols(name, a, b3, out_dtype, tm=1024, tn=1024):
    r, m = a.shape
    e, _, n = b3.shape
    tm, tn = _tile(m, tm, 128), _tile(n, tn, 128)
    return _matmul(name, a, b3, (e, m, n), out_dtype, (m // tm, e, n // tn, 1),
                   pl.BlockSpec((r, tm), lambda i, g, j, s: (0, i)),
                   pl.BlockSpec((None, r, tn), lambda i, g, j, s: (g, 0, j)),
                   pl.BlockSpec((None, tm, tn), lambda i, g, j, s: (g, i, j)), TN, 1)


def _mm_tn_acols(name, a3, b, out_dtype, tn=1024):
    e, r, m = a3.shape
    n = b.shape[1]
    tn = _tile(n, tn, 128)
    return _matmul(name, a3, b, (e, m, n), out_dtype, (n // tn, e, 1),
                   pl.BlockSpec((None, r, m), lambda j, g, s: (g, 0, 0)),
                   pl.BlockSpec((r, tn), lambda j, g, s: (0, j)),
                   pl.BlockSpec((None, m, tn), lambda j, g, s: (g, 0, j)), TN, 1)


def _rms_fwd(name, h, w):
    r, d = h.shape
    tr = _tile(r, 264, 8)

    def body(h_ref, w_ref, o_ref):
        x = h_ref[...]
        rs = lax.rsqrt(jnp.mean(x * x, axis=1, keepdims=True) + EPS)
        o_ref[...] = (x * rs * w_ref[...]).astype(BF16)

    return pl.pallas_call(
        body, name=name, grid=(r // tr,),
        in_specs=[pl.BlockSpec((tr, d), lambda i: (i, 0)), pl.BlockSpec((1, d), lambda i: (0, 0))],
        out_specs=pl.BlockSpec((tr, d), lambda i: (i, 0)),
        out_shape=jax.ShapeDtypeStruct((r, d), BF16), compiler_params=_params(),
    )(h, w)


def _rms_bwd(name, x, w, dy, dres):
    r, d = x.shape
    tr = _tile(r, 264, 8)

    def body(x_ref, w_ref, dy_ref, dr_ref, dx_ref, dxb_ref, dw_ref):
        xv = x_ref[...]
        g = dy_ref[...]
        rs = lax.rsqrt(jnp.mean(xv * xv, axis=1, keepdims=True) + EPS)
        wg = g * w_ref[...]
        dx = rs * wg - xv * (rs * rs * rs) * jnp.mean(xv * wg, axis=1, keepdims=True) + dr_ref[...]
        dx_ref[...] = dx
        dxb_ref[...] = dx.astype(BF16)
        part = jnp.sum(g * xv * rs, axis=0, keepdims=True)

        @pl.when(pl.program_id(0) == 0)
        def _():
            dw_ref[...] = part

        @pl.when(pl.program_id(0) > 0)
        def _():
            dw_ref[...] += part

    row = pl.BlockSpec((tr, d), lambda i: (i, 0))
    vec = pl.BlockSpec((1, d), lambda i: (0, 0))
    return pl.pallas_call(
        body, name=name, grid=(r // tr,), in_specs=[row, vec, row, row], out_specs=[row, row, vec],
        out_shape=[jax.ShapeDtypeStruct((r, d), F32), jax.ShapeDtypeStruct((r, d), BF16),
                   jax.ShapeDtypeStruct((1, d), F32)],
        compiler_params=_params(),
    )(x, w, dy, dres)


def _final_loss(name, h, w, target):
    r, d = h.shape
    nb = r // CHUNK

    def body(h_ref, w_ref, t_ref, dh_ref, dhb_ref, dw_ref, ls_ref):
        i = pl.program_id(0)

        @pl.when(i == 0)
        def _():
            dh_ref[...] = jnp.zeros_like(dh_ref)
            dhb_ref[...] = jnp.zeros_like(dhb_ref)
            dw_ref[...] = jnp.zeros_like(dw_ref)
            ls_ref[...] = jnp.zeros_like(ls_ref)

        @pl.when(i > 0)
        def _():
            xv = h_ref[...]
            wv = w_ref[...]
            rs = lax.rsqrt(jnp.mean(xv * xv, axis=1, keepdims=True) + EPS)
            err = xv * rs * wv - t_ref[...]
            sq = jnp.sum(jnp.sum(err * err, axis=1, keepdims=True), axis=0, keepdims=True)
            ls_ref[...] += jnp.broadcast_to(sq * (0.5 / d), ls_ref.shape)
            g = err * (1.0 / d)
            wg = g * wv
            dx = rs * wg - xv * (rs * rs * rs) * jnp.mean(xv * wg, axis=1, keepdims=True)
            dh_ref[...] = dx
            dhb_ref[...] = dx.astype(BF16)
            dw_ref[...] += jnp.sum(g * xv * rs, axis=0, keepdims=True)

    row = pl.BlockSpec((CHUNK, d), lambda i: (i, 0))
    vec = pl.BlockSpec((1, d), lambda i: (0, 0))
    return pl.pallas_call(
        body, name=name, grid=(nb,),
        in_specs=[row, vec, pl.BlockSpec((CHUNK, d), lambda i: (jnp.maximum(i - 1, 0), 0))],
        out_specs=[row, row, vec, pl.BlockSpec((1, 128), lambda i: (0, 0))],
        out_shape=[jax.ShapeDtypeStruct((r, d), F32), jax.ShapeDtypeStruct((r, d), BF16),
                   jax.ShapeDtypeStruct((1, d), F32), jax.ShapeDtypeStruct((1, 128), F32)],
        compiler_params=_params(),
    )(h, w, target)


def _swiglu_fwd(name, gu):
    e, _, r, f = gu.shape
    tr = _tile(r, 528, 8)

    def body(gu_ref, a_ref):
        g = gu_ref[0]
        a_ref[...] = (g * _sigmoid(g) * gu_ref[1]).astype(BF16)

    return pl.pallas_call(
        body, name=name, grid=(e, r // tr),
        in_specs=[pl.BlockSpec((None, 2, tr, f), lambda s, i: (s, 0, i, 0))],
        out_specs=pl.BlockSpec((None, tr, f), lambda s, i: (s, i, 0)),
        out_shape=jax.ShapeDtypeStruct((e, r, f), BF16), compiler_params=_params(),
    )(gu)


def _swiglu_bwd(name, da, gu):
    e, _, r, f = gu.shape
    tr = _tile(r, 528, 8)

    def body(da_ref, gu_ref, o_ref):
        g = gu_ref[0]
        s = _sigmoid(g)
        d = da_ref[...]
        o_ref[0] = (d * gu_ref[1] * (s + g * s * (1.0 - s))).astype(BF16)
        o_ref[1] = (d * g * s).astype(BF16)

    pair = pl.BlockSpec((None, 2, tr, f), lambda s, i: (s, 0, i, 0))
    return pl.pallas_call(
        body, name=name, grid=(e, r // tr),
        in_specs=[pl.BlockSpec((None, tr, f), lambda s, i: (s, i, 0)), pair], out_specs=pair,
        out_shape=jax.ShapeDtypeStruct((e, 2, r, f), BF16), compiler_params=_params(),
    )(da, gu)


def _shift_down(a, k):
    row = lax.broadcasted_iota(jnp.int32, a.shape, 0)
    return jnp.where(row >= k, pltpu.roll(a, k, 0), 0.0)


def _shift_up(a, k):
    n = a.shape[0]
    row = lax.broadcasted_iota(jnp.int32, a.shape, 0)
    return jnp.where(row < n - k, pltpu.roll(a, n - k, 0), 0.0)


def _conv_fwd(name, pc, cw):
    _, r, w = pc.shape

    def body(pc_ref, cw_ref, o_ref):
        a = pc_ref[2] * pc_ref[0]
        cwv = cw_ref[...]
        conv = _shift_down(a, 2) * cwv[0:1] + _shift_down(a, 1) * cwv[1:2] + a * cwv[2:3]
        o_ref[...] = (pc_ref[1] * conv).astype(BF16)

    return pl.pallas_call(
        body, name=name, grid=(w // 128,),
        in_specs=[pl.BlockSpec((3, r, 128), lambda j: (0, 0, j)), pl.BlockSpec((8, 128), lambda j: (0, j))],
        out_specs=pl.BlockSpec((r, 128), lambda j: (0, j)),
        out_shape=jax.ShapeDtypeStruct((r, w), BF16), compiler_params=_params(),
    )(pc, cw)


def _conv_bwd(name, dcat, pc, cw):
    _, r, w = pc.shape
    nblk = w // 128

    def body(dy_ref, pc_ref, cw_ref, dpc_ref, dcw_ref):
        u, gb, gc = pc_ref[0], pc_ref[1], pc_ref[2]
        cwv = cw_ref[...]
        dy = dy_ref[...]
        a = gc * u
        a1, a2 = _shift_down(a, 1), _shift_down(a, 2)
        conv = a2 * cwv[0:1] + a1 * cwv[1:2] + a * cwv[2:3]
        dconv = dy * gb
        da = dconv * cwv[2:3] + _shift_up(dconv, 1) * cwv[1:2] + _shift_up(dconv, 2) * cwv[0:1]
        dpc_ref[0] = (da * gc).astype(BF16)
        dpc_ref[1] = (dy * conv).astype(BF16)
        dpc_ref[2] = (da * u).astype(BF16)
        row = lax.broadcasted_iota(jnp.int32, (8, 128), 0)
        dw0 = jnp.sum(dconv * a2, axis=0, keepdims=True)
        dw1 = jnp.sum(dconv * a1, axis=0, keepdims=True)
        dw2 = jnp.sum(dconv * a, axis=0, keepdims=True)
        dcw_ref[...] = jnp.where(row == 0, dw0, jnp.where(row == 1, dw1, jnp.where(row == 2, dw2, 0.0)))

    return pl.pallas_call(
        body, name=name, grid=(nblk,),
        in_specs=[pl.BlockSpec((r, 128), lambda j: (0, nblk + j)),
                  pl.BlockSpec((3, r, 128), lambda j: (0, 0, j)), pl.BlockSpec((8, 128), lambda j: (0, j))],
        out_specs=[pl.BlockSpec((3, r, 128), lambda j: (0, 0, j)), pl.BlockSpec((8, 128), lambda j: (0, j))],
        out_shape=[jax.ShapeDtypeStruct((3, r, w), BF16), jax.ShapeDtypeStruct((8, w), F32)],
        compiler_params=_params(),
    )(dcat, pc, cw)


def _dot(a, b, dims):
    return lax.dot_general(a, b, (dims, ((), ())), preferred_element_type=F32)


def _col_to_row(xc, eye):
    return jnp.sum(jnp.where(eye, xc, 0.0), axis=0, keepdims=True)


def _row_to_col(xr, eye):
    return jnp.sum(jnp.where(eye, xr, 0.0), axis=1, keepdims=True)


def _gate_tiles(graw, bias, row0):
    th = jnp.tanh((graw + bias) / GATE_CAP)
    z = GATE_CAP * th
    row = lax.broadcasted_iota(jnp.int32, graw.shape, 0) + row0
    real = row >= PAD_FRONT
    li = jnp.where(real, z, -jnp.inf)
    lf = jnp.where(real, jnp.minimum(z, 0.0) - jnp.log(1.0 + jnp.exp(-jnp.abs(z))), 0.0)
    return th, z, li, lf, real


def _chunk_common(pm, h, li, lf, cst, nst, mst, tril, eye):
    kraw = pm[:, QK_W + h * DQK:QK_W + (h + 1) * DQK]
    q = (pm[:, h * DQK:(h + 1) * DQK] * QSCALE).astype(BF16)
    k = kraw.astype(BF16)
    v = pm[:, 2 * QK_W + h * DV:2 * QK_W + (h + 1) * DV].astype(BF16)
    li_c = li[:, h:h + 1]
    lf_c = lf[:, HEADS + h:HEADS + h + 1]
    li_r = _col_to_row(li_c, eye)
    lf_r = _col_to_row(lf_c, eye)
    b_c = jnp.sum(jnp.where(tril, lf_r, 0.0), axis=1, keepdims=True)
    b_r = _col_to_row(b_c, eye)
    dmat = jnp.where(tril, b_c - b_r + li_r, -jnp.inf)
    inter = b_c + mst
    mt = jnp.maximum(inter, jnp.max(dmat, axis=1, keepdims=True))
    w_inter = jnp.exp(inter - mt)
    p = jnp.exp(dmat - mt)
    s = _dot(q, k, NT) * p
    cb = cst.astype(BF16)
    nb = nst.astype(BF16).astype(F32)
    qc = _dot(q, cb, NN)
    qn = jnp.sum(q.astype(F32) * nb, axis=1, keepdims=True)
    den = w_inter * qn + jnp.sum(s, axis=1, keepdims=True)
    dn = jnp.maximum(jnp.abs(den), jnp.exp(-mt))
    b_end = b_c[CHUNK - 1:CHUNK, :]
    decay = b_end - b_c + li_c
    m_new = jnp.maximum(b_end + mst, jnp.max(decay, axis=0, keepdims=True))
    w_old = jnp.exp(b_end + mst - m_new)
    w_in = jnp.exp(decay - m_new)
    kw = (w_in * kraw).astype(BF16)
    return dict(q=q, k=k, v=v, kraw=kraw, mt=mt, w_inter=w_inter, p=p, s=s, cb=cb, nb=nb, qc=qc, qn=qn,
                den=den, dn=dn, m_new=m_new, w_old=w_old, w_in=w_in, kw=kw)


def _mlstm_fwd(name, pm, bias, nw):
    r = pm.shape[0]
    nc = r // CHUNK

    def body(pm_ref, b_ref, nw_ref, hm_ref, ht_ref, cs_ref, ns_ref, ms_ref, c_scr, n_scr, m_scr):
        ci = pl.program_id(0)

        @pl.when(ci == 0)
        def _():
            c_scr[...] = jnp.zeros_like(c_scr)
            n_scr[...] = jnp.zeros_like(n_scr)
            m_scr[...] = jnp.zeros_like(m_scr)

        pmv = pm_ref[...]
        rr = lax.broadcasted_iota(jnp.int32, (CHUNK, CHUNK), 0)
        cc = lax.broadcasted_iota(jnp.int32, (CHUNK, CHUNK), 1)
        tril, eye = cc <= rr, cc == rr
        _, _, li, lf, _ = _gate_tiles(pmv[:, GATE_COL:GATE_COL + 128], b_ref[...], ci * CHUNK)
        nwv = nw_ref[...]
        for h in range(HEADS):
            cst, nst, mst = c_scr[h], n_scr[h], m_scr[h]
            cs_ref[h] = cst
            ns_ref[h] = nst
            ms_ref[h] = mst
            f = _chunk_common(pmv, h, li, lf, cst, nst, mst, tril, eye)
            num = f["w_inter"] * f["qc"] + _dot(f["s"].astype(BF16), f["v"], NN)
            hh = num / f["dn"]
            c_scr[h] = f["w_old"] * cst + _dot(f["kw"], f["v"], TN)
            n_scr[h] = f["w_old"] * nst + jnp.sum(
                f["w_in"].astype(BF16).astype(F32) * f["k"].astype(F32), axis=0, keepdims=True)
            m_scr[h] = f["m_new"]
            sl = slice(h * DV, (h + 1) * DV)
            rs = lax.rsqrt(jnp.mean(hh * hh, axis=1, keepdims=True) + EPS)
            og = pmv[:, 2 * QK_W + MLSTM_W + h * DV:2 * QK_W + MLSTM_W + (h + 1) * DV]
            ht_ref[:, sl] = hh
            hm_ref[:, sl] = (_sigmoid(og) * (hh * rs * nwv[:, sl])).astype(BF16)

    return pl.pallas_call(
        body, name=name, grid=(nc,),
        in_specs=[pl.BlockSpec((CHUNK, PM_W), lambda i: (i, 0)), pl.BlockSpec((1, 128), lambda i: (0, 0)),
                  pl.BlockSpec((1, MLSTM_W), lambda i: (0, 0))],
        out_specs=[pl.BlockSpec((CHUNK, MLSTM_W), lambda i: (i, 0)),
                   pl.BlockSpec((CHUNK, MLSTM_W), lambda i: (i, 0)),
                   pl.BlockSpec((None, HEADS, DQK, DV), lambda i: (i, 0, 0, 0)),
                   pl.BlockSpec((None, HEADS, 1, DQK), lambda i: (i, 0, 0, 0)),
                   pl.BlockSpec((None, HEADS, 1, 1), lambda i: (i, 0, 0, 0))],
        out_shape=[jax.ShapeDtypeStruct((r, MLSTM_W), BF16), jax.ShapeDtypeStruct((r, MLSTM_W), F32),
                   jax.ShapeDtypeStruct((nc, HEADS, DQK, DV), F32),
                   jax.ShapeDtypeStruct((nc, HEADS, 1, DQK), F32),
                   jax.ShapeDtypeStruct((nc, HEADS, 1, 1), F32)],
        scratch_shapes=[pltpu.VMEM((HEADS, DQK, DV), F32), pltpu.VMEM((HEADS, 1, DQK), F32),
                        pltpu.VMEM((HEADS, 1, 1), F32)],
        compiler_params=_params(),
    )(pm, bias, nw)


def _mlstm_bwd(name, dcat, pm, ht, cs, ns, ms, bias, nw):
    r = pm.shape[0]
    nc = r // CHUNK

    def body(dy_ref, pm_ref, ht_ref, cs_ref, ns_ref, ms_ref, b_ref, nw_ref, dpm_ref, dnw_ref, db_ref,
             dc_scr, dn_scr):
        step = pl.program_id(0)
        ci = nc - 1 - step

        @pl.when(step == 0)
        def _():
            dc_scr[...] = jnp.zeros_like(dc_scr)
            dn_scr[...] = jnp.zeros_like(dn_scr)
            dnw_ref[...] = jnp.zeros_like(dnw_ref)
            db_ref[...] = jnp.zeros_like(db_ref)

        pmv = pm_ref[...]
        rr = lax.broadcasted_iota(jnp.int32, (CHUNK, CHUNK), 0)
        cc = lax.broadcasted_iota(jnp.int32, (CHUNK, CHUNK), 1)
        tril, eye, triu = cc <= rr, cc == rr, cc >= rr
        th, z, li, lf, real = _gate_tiles(pmv[:, GATE_COL:GATE_COL + 128], b_ref[...], ci * CHUNK)
        lane = lax.broadcasted_iota(jnp.int32, (CHUNK, 128), 1)
        rowid = lax.broadcasted_iota(jnp.int32, (CHUNK, 1), 0)
        nwv = nw_ref[...]
        dgt = jnp.zeros((CHUNK, 128), F32)
        for h in range(HEADS):
            cst, nst, mst = cs_ref[h], ns_ref[h], ms_ref[h]
            f = _chunk_common(pmv, h, li, lf, cst, nst, mst, tril, eye)
            q, k, v, s, p = f["q"], f["k"], f["v"], f["s"], f["p"]
            w_inter, w_in, w_old, dn = f["w_inter"], f["w_in"], f["w_old"], f["dn"]
            sl = slice(h * DV, (h + 1) * DV)
            osl = slice(2 * QK_W + MLSTM_W + h * DV, 2 * QK_W + MLSTM_W + (h + 1) * DV)
            hh = ht_ref[:, sl]
            y = dy_ref[:, sl]
            sg = _sigmoid(pmv[:, osl])
            rs = lax.rsqrt(jnp.mean(hh * hh, axis=1, keepdims=True) + EPS)
            nwh = nwv[:, sl]
            dpm_ref[:, osl] = (y * (hh * rs * nwh) * sg * (1.0 - sg)).astype(BF16)
            dhn = y * sg
            dnw_ref[:, sl] += jnp.sum(dhn * hh * rs, axis=0, keepdims=True)
            wd = dhn * nwh
            dhh = rs * wd - hh * (rs * rs * rs) * jnp.mean(hh * wd, axis=1, keepdims=True)
            dnum = dhh / dn
            dd = -jnp.sum(dhh * hh, axis=1, keepdims=True) / dn
            dden = jnp.where(jnp.abs(f["den"]) > jnp.exp(-f["mt"]), dd * jnp.sign(f["den"]), 0.0)
            dnum_b = dnum.astype(BF16)
            wdn = (w_inter * dnum).astype(BF16)
            wid = (w_inter * dden).astype(BF16).astype(F32)
            ds = _dot(dnum_b, v, NT) + dden
            dsp = (ds * p).astype(BF16)
            dq = _dot(dsp, k, NN) + _dot(wdn, f["cb"], NT) + wid * f["nb"]
            dk = _dot(dsp, q, TN)
            dv = _dot(s.astype(BF16), dnum_b, TN)
            g = ds * s
            g_col = _row_to_col(jnp.sum(g, axis=0, keepdims=True), eye)
            db = jnp.sum(g, axis=1, keepdims=True) - g_col
            dli = g_col
            db = db + (jnp.sum(dnum * f["qc"], axis=1, keepdims=True) + dden * f["qn"]) * w_inter
            dcn, dnn = dc_scr[h], dn_scr[h]
            dcnb = dcn.astype(BF16)
            dnnb = dnn.astype(BF16).astype(F32)
            dkw = _dot(v, dcnb, NT) + dnnb
            dk = dk + w_in * dkw
            dv = dv + _dot(f["kw"], dcnb, NN)
            ddecay = jnp.sum(dkw * f["kraw"], axis=1, keepdims=True) * w_in
            dw_old = (jnp.sum(jnp.sum(dcn * cst, axis=1, keepdims=True), axis=0, keepdims=True)
                      + jnp.sum(dnn * nst, axis=1, keepdims=True))
            db_end = dw_old * w_old + jnp.sum(ddecay, axis=0, keepdims=True)
            db = db - ddecay + jnp.where(rowid == CHUNK - 1, db_end, 0.0)
            dli = dli + ddecay
            dc_scr[h] = w_old * dcn + _dot(q, wdn, TN)
            dn_scr[h] = w_old * dnn + jnp.sum(wid * q.astype(F32), axis=0, keepdims=True)
            dlf = jnp.sum(jnp.where(triu, _col_to_row(db, eye), 0.0), axis=1, keepdims=True)
            dgt = dgt + jnp.where(lane == h, dli, 0.0) + jnp.where(lane == HEADS + h, dlf, 0.0)
            dpm_ref[:, h * DQK:(h + 1) * DQK] = (dq * QSCALE).astype(BF16)
            dpm_ref[:, QK_W + h * DQK:QK_W + (h + 1) * DQK] = dk.astype(BF16)
            dpm_ref[:, 2 * QK_W + h * DV:2 * QK_W + (h + 1) * DV] = dv.astype(BF16)
        dact = jnp.where(lane < HEADS, 1.0, 1.0 - _sigmoid(z)) * (1.0 - th * th)
        dgraw = jnp.where(real & (lane < 2 * HEADS), dgt * dact, 0.0)
        dpm_ref[:, GATE_COL:GATE_COL + 128] = dgraw.astype(BF16)
        db_ref[...] += jnp.sum(dgraw, axis=0, keepdims=True)

    rev = lambda i: (nc - 1 - i, 0)
    rev4 = lambda i: (nc - 1 - i, 0, 0, 0)
    return pl.pallas_call(
        body, name=name, grid=(nc,),
        in_specs=[pl.BlockSpec((CHUNK, MLSTM_W), rev), pl.BlockSpec((CHUNK, PM_W), rev),
                  pl.BlockSpec((CHUNK, MLSTM_W), rev),
                  pl.BlockSpec((None, HEADS, DQK, DV), rev4), pl.BlockSpec((None, HEADS, 1, DQK), rev4),
                  pl.BlockSpec((None, HEADS, 1, 1), rev4),
                  pl.BlockSpec((1, 128), lambda i: (0, 0)), pl.BlockSpec((1, MLSTM_W), lambda i: (0, 0))],
        out_specs=[pl.BlockSpec((CHUNK, PM_W), rev), pl.BlockSpec((1, MLSTM_W), lambda i: (0, 0)),
                   pl.BlockSpec((1, 128), lambda i: (0, 0))],
        out_shape=[jax.ShapeDtypeStruct((r, PM_W), BF16), jax.ShapeDtypeStruct((1, MLSTM_W), F32),
                   jax.ShapeDtypeStruct((1, 128), F32)],
        scratch_shapes=[pltpu.VMEM((HEADS, DQK, DV), F32), pltpu.VMEM((HEADS, 1, DQK), F32)],
        compiler_params=_params(),
    )(dcat, pm, ht, cs, ns, ms, bias, nw)


def _my_place():
    return lax.axis_index("x"), lax.axis_index("y"), lax.axis_index("c")


def _flip(v, bit):
    return 1 - v if bit else v


def _exchange_small(name, blk, reduce):
    r, c = blk.shape

    def body(x_ref, o_ref, *rest):
        slots = rest[0] if reduce else o_ref
        send_sems, recv_sems = rest[-2], rest[-1]
        x, y, cc = _my_place()
        me = 4 * x + 2 * y + cc
        slots[me] = x_ref[...]
        copies = []
        for k in range(1, N_DEV):
            peer = (_flip(x, k & 4), _flip(y, k & 2), _flip(cc, k & 1))
            cp = pltpu.make_async_remote_copy(
                src_ref=x_ref, dst_ref=slots.at[me], send_sem=send_sems.at[k - 1],
                recv_sem=recv_sems.at[k - 1], device_id=peer, device_id_type=MESH)
            cp.start()
            copies.append(cp)
        for cp in copies:
            cp.wait()
        if reduce:
            acc = slots[0]
            for d in range(1, N_DEV):
                acc = acc + slots[d]
            o_ref[...] = acc

    scratch = ([pltpu.VMEM((N_DEV, r, c), F32)] if reduce else []) + [
        pltpu.SemaphoreType.DMA((N_DEV - 1,)), pltpu.SemaphoreType.DMA((N_DEV - 1,))]
    return pl.pallas_call(
        body, name=name,
        out_shape=jax.ShapeDtypeStruct((r, c) if reduce else (N_DEV, r, c), F32),
        in_specs=[pl.BlockSpec(memory_space=pltpu.VMEM)], out_specs=pl.BlockSpec(memory_space=pltpu.VMEM),
        scratch_shapes=scratch, compiler_params=_params(),
    )(blk)


def _all_gather(name, shards):
    n = len(shards)

    def body(*refs):
        ins, outs = refs[:n], refs[n:2 * n]
        send_sems, recv_sems, local_sems = refs[2 * n:]
        x, y, c = _my_place()
        sibling = (x, y, 1 - c)
        chips = [(1 - x, y), (x, 1 - y), (1 - x, 1 - y)]

        def slot(px, py, pc):
            return 4 * px + 2 * py + pc

        def copy(a, k, block, to, src=None):
            dst = outs[a].at[slot(*block)]
            return pltpu.make_async_remote_copy(
                src_ref=dst if src is None else src, dst_ref=dst, send_sem=send_sems.at[a, k],
                recv_sem=recv_sems.at[a, k], device_id=to, device_id_type=MESH)

        mine, first, passed = [], [], []
        for a in range(n):
            cp = pltpu.make_async_copy(ins[a], outs[a].at[slot(x, y, c)], local_sems.at[a])
            cp.start()
            mine.append(cp)
            for j, chip in enumerate(chips):
                first.append(copy(a, 1 + j, (x, y, c), (*chip, c), src=ins[a]))
                first[-1].start()
        for a in range(n):
            first.append(copy(a, 0, (x, y, c), sibling, src=ins[a]))
            first[-1].start()
        for j, chip in enumerate(chips):
            for a in range(n):
                copy(a, 1 + j, (*chip, c), (x, y, c)).wait_recv()
                passed.append(copy(a, 4 + j, (*chip, c), sibling))
                passed[-1].start()
        for a in range(n):
            copy(a, 0, (x, y, 1 - c), (x, y, c)).wait_recv()
            for j, chip in enumerate(chips):
                copy(a, 4 + j, (*chip, 1 - c), (x, y, c)).wait_recv()
        for cp in first + passed:
            cp.wait_send()
        for cp in mine:
            cp.wait()

    anyspec = pl.BlockSpec(memory_space=pl.ANY)
    return pl.pallas_call(
        body, name=name,
        out_shape=[jax.ShapeDtypeStruct((N_DEV,) + s.shape, s.dtype) for s in shards],
        in_specs=[anyspec] * n, out_specs=[anyspec] * n,
        scratch_shapes=[pltpu.SemaphoreType.DMA((n, 7)), pltpu.SemaphoreType.DMA((n, 7)),
                        pltpu.SemaphoreType.DMA((n,))],
        compiler_params=_params(),
    )(*shards)


def _sibling_exchange(name, grads):
    n = len(grads)

    def body(*refs):
        ins, outs = refs[:n], refs[n:2 * n]
        send_sems, recv_sems = refs[2 * n:]
        x, y, c = _my_place()
        copies = []
        for a in range(n):
            for j in range(4):
                cp = pltpu.make_async_remote_copy(
                    src_ref=ins[a].at[2 * j + 1 - c], dst_ref=outs[a].at[j], send_sem=send_sems.at[a, j],
                    recv_sem=recv_sems.at[a, j], device_id=(x, y, 1 - c), device_id_type=MESH)
                cp.start()
                copies.append(cp)
        for cp in copies:
            cp.wait()

    anyspec = pl.BlockSpec(memory_space=pl.ANY)
    return pl.pallas_call(
        body, name=name,
        out_shape=[jax.ShapeDtypeStruct((4,) + g.shape[1:], g.dtype) for g in grads],
        in_specs=[anyspec] * n, out_specs=[anyspec] * n,
        scratch_shapes=[pltpu.SemaphoreType.DMA((n, 4)), pltpu.SemaphoreType.DMA((n, 4))],
        compiler_params=_params(),
    )(*grads)


def _pair_sum(name, core, g, t):
    _, r, c = g.shape
    tr = _tile(r, 512, 8)
    g4 = g.reshape(4, 2, r, c)

    def body(core_ref, g_ref, t_ref, o_ref):
        o_ref[...] = (g_ref[...].astype(F32) + t_ref[...].astype(F32)).astype(BF16)

    return pl.pallas_call(
        body, name=name,
        grid_spec=pltpu.PrefetchScalarGridSpec(
            num_scalar_prefetch=1, grid=(4, r // tr),
            in_specs=[pl.BlockSpec((None, None, tr, c), lambda j, i, core_ref: (j, core_ref[0], i, 0)),
                      pl.BlockSpec((None, tr, c), lambda j, i, core_ref: (j, i, 0))],
            out_specs=pl.BlockSpec((None, tr, c), lambda j, i, core_ref: (j, i, 0))),
        out_shape=jax.ShapeDtypeStruct((4, r, c), BF16), compiler_params=_params(),
    )(core, g4, t)


def _chip_exchange(name, parts):
    n = len(parts)

    def body(*refs):
        ins, outs = refs[:n], refs[n:2 * n]
        send_sems, recv_sems, local_sems = refs[2 * n:]
        x, y, c = _my_place()
        chip = 2 * x + y
        copies = []
        for a in range(n):
            cp = pltpu.make_async_copy(ins[a].at[chip], outs[a].at[chip], local_sems.at[a])
            cp.start()
            copies.append(cp)
            for k in range(1, 4):
                px, py = _flip(x, k & 2), _flip(y, k & 1)
                cp = pltpu.make_async_remote_copy(
                    src_ref=ins[a].at[2 * px + py], dst_ref=outs[a].at[chip], send_sem=send_sems.at[a, k - 1],
                    recv_sem=recv_sems.at[a, k - 1], device_id=(px, py, c), device_id_type=MESH)
                cp.start()
                copies.append(cp)
        for cp in copies:
            cp.wait()

    anyspec = pl.BlockSpec(memory_space=pl.ANY)
    return pl.pallas_call(
        body, name=name,
        out_shape=[jax.ShapeDtypeStruct(p.shape, p.dtype) for p in parts],
        in_specs=[anyspec] * n, out_specs=[anyspec] * n,
        scratch_shapes=[pltpu.SemaphoreType.DMA((n, 3)), pltpu.SemaphoreType.DMA((n, 3)),
                        pltpu.SemaphoreType.DMA((n,))],
        compiler_params=_params(),
    )(*parts)


def _adam_math(w, g, m, v):
    m2 = ADAM_B1 * m + (1.0 - ADAM_B1) * g
    v2 = ADAM_B2 * v + (1.0 - ADAM_B2) * (g * g)
    m_hat = m2 / (1.0 - ADAM_B1 ** ADAM_STEP)
    v_hat = v2 / (1.0 - ADAM_B2 ** ADAM_STEP)
    delta = -ADAM_LR * (m_hat / (jnp.sqrt(v_hat) + ADAM_EPS) + ADAM_WD * w)
    return delta, m2, v2


def _adam_sharded(name, w, m, v, q0, q1, row_off=0):
    _, r, c = w.shape
    tr = _tile(r, 256, 8)
    boff = row_off // tr

    def body(w_ref, m_ref, v_ref, q0_ref, q1_ref, g_ref, d_ref, nm_ref, nv_ref):
        def total(q_ref):
            acc = q_ref[0].astype(F32)
            for j in range(1, 4):
                acc = acc + q_ref[j].astype(F32)
            return acc

        g = jnp.where(pl.program_id(0) == 0, total(q0_ref), total(q1_ref))
        delta, m2, v2 = _adam_math(w_ref[...], g, m_ref[...], v_ref[...])
        g_ref[...] = g
        d_ref[...] = delta
        nm_ref[...] = m2
        nv_ref[...] = v2

    wspec = pl.BlockSpec((None, tr, c), lambda l, i: (l, i, 0))
    qspec = pl.BlockSpec((4, tr, c), lambda l, i: (0, boff + i, 0))
    sds = jax.ShapeDtypeStruct(w.shape, F32)
    return pl.pallas_call(
        body, name=name, grid=(2, r // tr), in_specs=[wspec, wspec, wspec, qspec, qspec],
        out_specs=[wspec] * 4, out_shape=[sds] * 4, compiler_params=_params(),
    )(w, m, v, q0, q1)


def _adam_small(name, w, m, v, g):
    def body(w_ref, m_ref, v_ref, g_ref, d_ref, nm_ref, nv_ref):
        delta, m2, v2 = _adam_math(w_ref[...], g_ref[...], m_ref[...], v_ref[...])
        d_ref[...] = delta
        nm_ref[...] = m2
        nv_ref[...] = v2

    sds = jax.ShapeDtypeStruct(w.shape, F32)
    vm = pl.BlockSpec(memory_space=pltpu.VMEM)
    return pl.pallas_call(body, name=name, in_specs=[vm] * 4, out_specs=[vm] * 3, out_shape=[sds] * 3,
                          compiler_params=_params())(w, m, v, g)


def _split_w_in(gathered):
    full = jnp.transpose(gathered, (1, 0, 2)).reshape(D_MODEL, D_IN)
    gate_end = GATE_COL + 2 * HEADS
    wm = jnp.concatenate([full[:, :gate_end], jnp.zeros((D_MODEL, PM_W - gate_end), full.dtype)], axis=1)
    wc = jnp.stack([full[:, gate_end + s * CONV_W:gate_end + (s + 1) * CONV_W] for s in range(3)])
    return wm, wc


def _merge_dw_in(dwm, dwc):
    gate_end = GATE_COL + 2 * HEADS
    full = jnp.concatenate([dwm[:, :gate_end], dwc[0], dwc[1], dwc[2]], axis=1)
    return jnp.transpose(full.reshape(D_MODEL, N_DEV, IN_SH), (1, 0, 2))


def _pack128(parts):
    flat = jnp.concatenate([p.reshape(-1) for p in parts])
    n = flat.shape[0]
    rows = -(-n // 1024) * 8
    return jnp.pad(flat, (0, rows * 128 - n)).reshape(rows, 128)


def _unpack128(packed, shapes):
    flat = packed.reshape(-1)
    out, at = [], 0
    for s in shapes:
        n = int(np.prod(s))
        out.append(flat[at:at + n].reshape(s))
        at += n
    return out


def kernel(x, meta_tokens, norm_mix_w, w_in, b_gates, conv_w, mlstm_norm_w, w_out, norm_ffn_w, w_gate, w_up, w_down, norm_final_w, loss_target, m_meta_tokens, m_norm_mix_w, m_w_in, m_b_gates, m_conv_w, m_mlstm_norm_w, m_w_out, m_norm_ffn_w, m_w_gate, m_w_up, m_w_down, m_norm_final_w, v_meta_tokens, v_norm_mix_w, v_w_in, v_b_gates, v_conv_w, v_mlstm_norm_w, v_w_out, v_norm_ffn_w, v_w_gate, v_w_up, v_w_down, v_norm_final_w):
    seq = x.shape[1]
    rows = TOK0 + seq
    me = 4 * lax.axis_index("x") + 2 * lax.axis_index("y") + lax.axis_index("c")
    meta_sh = meta_tokens.shape[1]
    conv_sh = conv_w.shape[2]

    small = jnp.concatenate(
        [meta_tokens, jnp.pad(conv_w.reshape(DEPTH * 3, conv_sh), ((0, 2), (0, meta_sh - conv_sh)))], axis=0)
    slots = _exchange_small("gather_small", small, reduce=False)
    meta_full = jnp.transpose(slots[:, :N_META, :], (1, 0, 2)).reshape(N_META, D_MODEL)
    conv_full = jnp.transpose(slots[:, N_META:N_META + DEPTH * 3, :conv_sh], (1, 0, 2)).reshape(DEPTH, 3, CONV_W)
    conv_rows = [jnp.pad(conv_full[l], ((0, 5), (0, 0))) for l in range(DEPTH)]

    shards = []
    for l in range(DEPTH):
        shards += [w_in[l].astype(BF16), w_out[l].astype(BF16),
                   jnp.concatenate([w_gate[l], w_up[l]], axis=0).astype(BF16), w_down[l].astype(BF16)]
    gathered = _all_gather("gather_weights", shards)
    weights = []
    for l in range(DEPTH):
        g_in, g_out, g_gu, g_down = gathered[4 * l:4 * l + 4]
        wm, wc = _split_w_in(g_in)
        weights.append(dict(wm=wm, wc=wc, wo=g_out.reshape(D_MODEL, D_MODEL),
                            wgu=g_gu.reshape(2 * N_DEV, D_MODEL, FF_SH), wd=g_down))

    bias = [jnp.pad(b_gates[l].reshape(1, 2 * HEADS), ((0, 0), (0, 128 - 2 * HEADS))) for l in range(DEPTH)]
    nmix = [norm_mix_w[l].reshape(1, D_MODEL) for l in range(DEPTH)]
    nffn = [norm_ffn_w[l].reshape(1, D_MODEL) for l in range(DEPTH)]
    nmls = [mlstm_norm_w[l].reshape(1, MLSTM_W) for l in range(DEPTH)]

    h = jnp.concatenate([jnp.zeros((PAD_FRONT, D_MODEL), F32), meta_full, x[0]], axis=0)
    saved = []
    for l in range(DEPTH):
        w = weights[l]
        hn = _rms_fwd(f"norm_mix_{l}", h, nmix[l])
        pm = _mm_nn(f"proj_mlstm_{l}", hn, w["wm"], F32, tn=640)
        pc = _mm_nn_bcols(f"proj_conv_{l}", hn, w["wc"], F32, tn=512)
        hm, ht, cs, ns, ms = _mlstm_fwd(f"mlstm_fwd_{l}", pm, bias[l], nmls[l])
        hc = _conv_fwd(f"conv_fwd_{l}", pc, conv_rows[l])
        cat = jnp.concatenate([hm, hc], axis=1)
        h1 = _mm_nn(f"out_proj_{l}", cat, w["wo"], F32, res=h)
        hf = _rms_fwd(f"norm_ffn_{l}", h1, nffn[l])
        gu = _mm_nn_bcols(f"ffn_in_{l}", hf, w["wgu"], F32).reshape(N_DEV, 2, rows, FF_SH)
        act = _swiglu_fwd(f"swiglu_{l}", gu)
        h2 = _mm_nn_ksum(f"ffn_out_{l}", act, w["wd"], h1, F32)
        saved.append(dict(h0=h, hn=hn, pm=pm, pc=pc, ht=ht, cs=cs, ns=ns, ms=ms, cat=cat, h1=h1, hf=hf,
                          gu=gu, act=act))
        h = h2

    dh, dh_b, d_final, loss_part = _final_loss("final_loss", h, norm_final_w.reshape(1, D_MODEL), loss_target[0])

    grads = [None] * DEPTH
    d_mix, d_ffn, d_mls, d_bias, d_conv = ([None] * DEPTH for _ in range(5))
    for l in reversed(range(DEPTH)):
        w, s = weights[l], saved[l]
        da = _mm_nt_bcols(f"d_act_{l}", dh_b, w["wd"], F32)
        dw_down = _mm_tn_acols(f"dw_down_{l}", s["act"], dh_b, BF16)
        dgu = _swiglu_bwd(f"swiglu_bwd_{l}", da, s["gu"]).reshape(2 * N_DEV, rows, FF_SH)
        dhf = _mm_nt_ksum(f"d_ffn_in_{l}", dgu, w["wgu"], F32)
        dw_gu = _mm_tn_bcols(f"dw_ffn_in_{l}", s["hf"], dgu, BF16, tn=FF_SH)
        dh1, dh1_b, d_ffn[l] = _rms_bwd(f"norm_ffn_bwd_{l}", s["h1"], nffn[l], dhf, dh)
        dcat = _mm_nt(f"d_cat_{l}", dh1_b, w["wo"], F32, tk=D_MODEL)
        dw_out = _mm_tn(f"dw_out_{l}", s["cat"], dh1_b, BF16, tn=1024)
        dpm, d_mls[l], d_bias[l] = _mlstm_bwd(f"mlstm_bwd_{l}", dcat, s["pm"], s["ht"], s["cs"], s["ns"],
                                               s["ms"], bias[l], nmls[l])
        dpc, d_conv[l] = _conv_bwd(f"conv_bwd_{l}", dcat, s["pc"], conv_rows[l])
        dhn = _mm_nt(f"d_norm_mlstm_{l}", dpm, w["wm"], F32, tk=640)
        dhn = _mm_nt_ksum(f"d_norm_conv_{l}", dpc, w["wc"], F32, res=dhn)
        dwm = _mm_tn(f"dw_mlstm_{l}", s["hn"], dpm, BF16)
        dwc = _mm_tn_bcols(f"dw_conv_{l}", s["hn"], dpc, BF16)
        dh, dh_b, d_mix[l] = _rms_bwd(f"norm_mix_bwd_{l}", s["h0"], nmix[l], dhn, dh1)
        grads[l] = [_merge_dw_in(dwm, dwc), dw_out.reshape(N_DEV, OUT_SH, D_MODEL),
                    dw_gu.reshape(N_DEV, 2 * D_MODEL, FF_SH), dw_down]

    flat = grads[0] + grads[1]
    from_sibling = _sibling_exchange("grad_sibling_exchange", flat)
    core = lax.axis_index("c").astype(jnp.int32).reshape(1)
    names = ["w_in", "w_out", "w_ffn_in", "w_down"]
    parts = [_pair_sum(f"grad_pair_sum_{names[i % 4]}_{i // 4}", core, flat[i], from_sibling[i])
             for i in range(len(flat))]
    q = _chip_exchange("grad_chip_exchange", parts)
    g_in, d_in, nm_in, nv_in = _adam_sharded("adam_w_in", w_in, m_w_in, v_w_in, q[0], q[4])
    g_out, d_out, nm_out, nv_out = _adam_sharded("adam_w_out", w_out, m_w_out, v_w_out, q[1], q[5])
    g_gate, d_gate, nm_gate, nv_gate = _adam_sharded("adam_w_gate", w_gate, m_w_gate, v_w_gate, q[2], q[6])
    g_up, d_up, nm_up, nv_up = _adam_sharded("adam_w_up", w_up, m_w_up, v_w_up, q[2], q[6], row_off=D_MODEL)
    g_down, d_down, nm_down, nv_down = _adam_sharded("adam_w_down", w_down, m_w_down, v_w_down, q[3], q[7])

    bg = jnp.concatenate([d_bias[l][0, :2 * HEADS] for l in range(DEPTH)])
    red_in = jnp.concatenate([
        dh[PAD_FRONT:TOK0], d_mix[0], d_mix[1], d_ffn[0], d_ffn[1], d_final,
        jnp.concatenate([d_mls[0], d_mls[1]], axis=1),
        jnp.stack([d_conv[l][:3] for l in range(DEPTH)]).reshape(3, 2 * CONV_W),
        jnp.pad(bg, (0, D_MODEL - bg.shape[0])).reshape(1, D_MODEL),
        jnp.pad(loss_part[:, :1], ((0, 0), (0, D_MODEL - 1))),
        jnp.zeros((5, D_MODEL), F32)], axis=0)
    red = _exchange_small("reduce_small", red_in, reduce=True)
    loss = red[26, 0]
    g_meta = lax.dynamic_slice_in_dim(red[:N_META], me * meta_sh, meta_sh, axis=1)
    g_mix, g_ffn, g_final = red[16:18], red[18:20], red[20]
    g_mls = red[21].reshape(DEPTH, MLSTM_W)
    g_conv = lax.dynamic_slice_in_dim(red[22:25].reshape(DEPTH, 3, CONV_W), me * conv_sh, conv_sh, axis=2)
    g_bias = red[25, :DEPTH * 2 * HEADS].reshape(DEPTH, 2 * HEADS)

    small_w = [meta_tokens, norm_mix_w, b_gates, conv_w, mlstm_norm_w, norm_ffn_w, norm_final_w]
    small_m = [m_meta_tokens, m_norm_mix_w, m_b_gates, m_conv_w, m_mlstm_norm_w, m_norm_ffn_w, m_norm_final_w]
    small_v = [v_meta_tokens, v_norm_mix_w, v_b_gates, v_conv_w, v_mlstm_norm_w, v_norm_ffn_w, v_norm_final_w]
    small_g = [g_meta, g_mix, g_bias, g_conv, g_mls, g_ffn, g_final]
    shapes = [a.shape for a in small_w]
    packed = _adam_small("adam_small", _pack128(small_w), _pack128(small_m), _pack128(small_v), _pack128(small_g))
    (d_meta, d_nmix, d_bg, d_cw, d_nmls, d_nffn, d_nfin), (nm_meta, nm_nmix, nm_bg, nm_cw, nm_nmls, nm_nffn, nm_nfin), \
        (nv_meta, nv_nmix, nv_bg, nv_cw, nv_nmls, nv_nffn, nv_nfin) = (_unpack128(p, shapes) for p in packed)

    grad_x = dh[TOK0:].reshape(1, seq, D_MODEL)
    return (loss, grad_x,
            g_meta, g_mix, g_in, g_bias, g_conv, g_mls, g_out, g_ffn, g_gate, g_up, g_down, g_final,
            d_meta, d_nmix, d_in, d_bg, d_cw, d_nmls, d_out, d_nffn, d_gate, d_up, d_down, d_nfin,
            nm_meta, nm_nmix, nm_in, nm_bg, nm_cw, nm_nmls, nm_out, nm_nffn, nm_gate, nm_up, nm_down, nm_nfin,
            nv_meta, nv_nmix, nv_in, nv_bg, nv_cw, nv_nmls, nv_out, nv_nffn, nv_gate, nv_up, nv_down, nv_nfin)
```

```python
import functools

import numpy as np
import jax
import jax.numpy as jnp
from jax import lax
from jax.experimental import pallas as pl
from jax.experimental.pallas import tpu as pltpu

F32 = jnp.float32
BF16 = jnp.bfloat16
MESH = pl.DeviceIdType.MESH

D_MODEL = 2048
DEPTH = 2
N_META = 16
MLSTM_W = 1024
CONV_W = 1024
HEADS = 4
DV = 256
DQK = 128
QK_W = 512
CHUNK = 64
PAD_FRONT = 48
TOK0 = PAD_FRONT + N_META
D_FF = 5632
N_DEV = 8
FF_SH = D_FF // N_DEV
D_IN = 6152
IN_SH = D_IN // N_DEV
OUT_SH = D_MODEL // N_DEV
GATE_COL = 3072
PM_W = GATE_COL + 128
GATE_CAP = 15.0
EPS = 1e-6
QSCALE = DQK ** -0.5

ADAM_LR = 0.001
ADAM_B1 = 0.9
ADAM_B2 = 0.999
ADAM_EPS = 1e-08
ADAM_WD = 0.01
ADAM_STEP = 10

V7X_VMEM_LIMIT = 50 * 1024 * 1024


def _params(**kw):
    return pltpu.CompilerParams(vmem_limit_bytes=V7X_VMEM_LIMIT, **kw)


def _tile(n, target, mult):
    best = None
    for t in range(mult, min(n, target) + 1, mult):
        if n % t == 0:
            best = t
    return best if best is not None else n


def _sigmoid(x):
    return 1.0 / (1.0 + jnp.exp(-x))


NN = ((1,), (0,))
NT = ((1,), (1,))
TN = ((0,), (0,))


def _matmul(name, a, b, out_shape, out_dtype, grid, a_bs, b_bs, o_bs, dims, nk, acc_shape=None,
            res=None, res_bs=None):
    has_res = res is not None

    def body(*refs):
        a_ref, b_ref = refs[0], refs[1]
        r_ref = refs[2] if has_res else None
        o_ref = refs[2 + has_res]
        x = lax.dot_general(a_ref[...], b_ref[...], (dims, ((), ())), preferred_element_type=F32)
        if nk == 1:
            if has_res:
                x = x + r_ref[...]
            o_ref[...] = x.astype(o_ref.dtype)
            return
        acc = refs[3 + has_res]
        k = pl.program_id(len(grid) - 1)

        @pl.when(k == 0)
        def _():
            acc[...] = (x + r_ref[...]) if has_res else x

        @pl.when(k > 0)
        def _():
            acc[...] += x

        @pl.when(k == nk - 1)
        def _():
            o_ref[...] = acc[...].astype(o_ref.dtype)

    ins = [a, b] + ([res] if has_res else [])
    specs = [a_bs, b_bs] + ([res_bs] if has_res else [])
    scratch = [pltpu.VMEM(acc_shape, F32)] if nk > 1 else []
    return pl.pallas_call(
        body, name=name, grid=grid, in_specs=specs, out_specs=o_bs,
        out_shape=jax.ShapeDtypeStruct(out_shape, out_dtype), scratch_shapes=scratch,
        compiler_params=_params(),
    )(*ins)


def _mm_nn(name, a, b, out_dtype, res=None, tm=1056, tn=512):
    r, k = a.shape
    n = b.shape[1]
    tm, tn = _tile(r, tm, 8), _tile(n, tn, 128)
    return _matmul(name, a, b, (r, n), out_dtype, (r // tm, n // tn, 1),
                   pl.BlockSpec((tm, k), lambda i, j, s: (i, 0)),
                   pl.BlockSpec((k, tn), lambda i, j, s: (0, j)),
                   pl.BlockSpec((tm, tn), lambda i, j, s: (i, j)), NN, 1,
                   res=res, res_bs=pl.BlockSpec((tm, tn), lambda i, j, s: (i, j)))


def _mm_nn_kt(name, a, b, out_dtype, tm=1056, tn=1024, tk=640):
    r, k = a.shape
    n = b.shape[1]
    tm, tn, tk = _tile(r, tm, 8), _tile(n, tn, 128), _tile(k, tk, 128)
    nk = k // tk
    return _matmul(name, a, b, (r, n), out_dtype, (r // tm, n // tn, nk),
                   pl.BlockSpec((tm, tk), lambda i, j, s: (i, s)),
                   pl.BlockSpec((tk, tn), lambda i, j, s: (s, j)),
                   pl.BlockSpec((tm, tn), lambda i, j, s: (i, j)), NN, nk, acc_shape=(tm, tn))


def _mm_nn_bcols(name, a, b3, out_dtype, tm=1056, tn=1024):
    r, k = a.shape
    e, _, n = b3.shape
    tm, tn = _tile(r, tm, 8), _tile(n, tn, 128)
    return _matmul(name, a, b3, (e, r, n), out_dtype, (r // tm, e, n // tn, 1),
                   pl.BlockSpec((tm, k), lambda i, g, j, s: (i, 0)),
                   pl.BlockSpec((None, k, tn), lambda i, g, j, s: (g, 0, j)),
                   pl.BlockSpec((None, tm, tn), lambda i, g, j, s: (g, i, j)), NN, 1)


def _mm_nn_ksum(name, a3, b3, out_dtype, res=None, tm=1056, tn=1024):
    e, r, kb = a3.shape
    n = b3.shape[2]
    tm, tn = _tile(r, tm, 8), _tile(n, tn, 128)
    return _matmul(name, a3, b3, (r, n), out_dtype, (r // tm, n // tn, e),
                   pl.BlockSpec((None, tm, kb), lambda i, j, s: (s, i, 0)),
                   pl.BlockSpec((None, kb, tn), lambda i, j, s: (s, 0, j)),
                   pl.BlockSpec((tm, tn), lambda i, j, s: (i, j)), NN, e, acc_shape=(tm, tn),
                   res=res, res_bs=pl.BlockSpec((tm, tn), lambda i, j, s: (i, j)))


def _mm_nt(name, a, b, out_dtype, res=None, tm=1056, tn=512, tk=640, n=None):
    r, k = a.shape
    n = b.shape[0] if n is None else n
    tm, tn, tk = _tile(r, tm, 8), _tile(n, tn, 128), _tile(k, tk, 128)
    nk = k // tk
    return _matmul(name, a, b, (r, n), out_dtype, (r // tm, n // tn, nk),
                   pl.BlockSpec((tm, tk), lambda i, j, s: (i, s)),
                   pl.BlockSpec((tn, tk), lambda i, j, s: (j, s)),
                   pl.BlockSpec((tm, tn), lambda i, j, s: (i, j)), NT, nk, acc_shape=(tm, tn),
                   res=res, res_bs=pl.BlockSpec((tm, tn), lambda i, j, s: (i, j)))


def _mm_nt_bcols(name, a, b3, out_dtype, tm=1056):
    r, k = a.shape
    e, n, _ = b3.shape
    tm = _tile(r, tm, 8)
    return _matmul(name, a, b3, (e, r, n), out_dtype, (r // tm, e, 1),
                   pl.BlockSpec((tm, k), lambda i, g, s: (i, 0)),
                   pl.BlockSpec((None, n, k), lambda i, g, s: (g, 0, 0)),
                   pl.BlockSpec((None, tm, n), lambda i, g, s: (g, i, 0)), NT, 1)


def _mm_nt_ksum(name, a3, b3, out_dtype, res=None, tm=1056, tn=1024):
    e, r, kb = a3.shape
    n = b3.shape[1]
    tm, tn = _tile(r, tm, 8), _tile(n, tn, 128)
    return _matmul(name, a3, b3, (r, n), out_dtype, (r // tm, n // tn, e),
                   pl.BlockSpec((None, tm, kb), lambda i, j, s: (s, i, 0)),
                   pl.BlockSpec((None, tn, kb), lambda i, j, s: (s, j, 0)),
                   pl.BlockSpec((tm, tn), lambda i, j, s: (i, j)), NT, e, acc_shape=(tm, tn),
                   res=res, res_bs=pl.BlockSpec((tm, tn), lambda i, j, s: (i, j)))


def _mm_tn(name, a, b, out_dtype, tm=1024, tn=640):
    r, m = a.shape
    n = b.shape[1]
    tm, tn = _tile(m, tm, 128), _tile(n, tn, 128)
    return _matmul(name, a, b, (m, n), out_dtype, (m // tm, n // tn, 1),
                   pl.BlockSpec((r, tm), lambda i, j, s: (0, i)),
                   pl.BlockSpec((r, tn), lambda i, j, s: (0, j)),
                   pl.BlockSpec((tm, tn), lambda i, j, s: (i, j)), TN, 1)


def _mm_tn_bcols(name, a, b3, out_dtype, tm=1024, tn=1024):
    r, m = a.shape
    e, _, n = b3.shape
    tm, tn = _tile(m, tm, 128), _tile(n, tn, 128)
    return _matmul(name, a, b3, (e, m, n), out_dtype, (m // tm, e, n // tn, 1),
                   pl.BlockSpec((r, tm), lambda i, g, j, s: (0, i)),
                   pl.BlockSpec((None, r, tn), lambda i, g, j, s: (g, 0, j)),
                   pl.BlockSpec((None, tm, tn), lambda i, g, j, s: (g, i, j)), TN, 1)


def _mm_tn_acols(name, a3, b, out_dtype, tn=1024):
    e, r, m = a3.shape
    n = b.shape[1]
    tn = _tile(n, tn, 128)
    return _matmul(name, a3, b, (e, m, n), out_dtype, (n // tn, e, 1),
                   pl.BlockSpec((None, r, m), lambda j, g, s: (g, 0, 0)),
                   pl.BlockSpec((r, tn), lambda j, g, s: (0, j)),
                   pl.BlockSpec((None, m, tn), lambda j, g, s: (g, 0, j)), TN, 1)


def _rms_fwd(name, h, w):
    r, d = h.shape
    tr = _tile(r, 264, 8)

    def body(h_ref, w_ref, o_ref):
        x = h_ref[...]
        rs = lax.rsqrt(jnp.mean(x * x, axis=1, keepdims=True) + EPS)
        o_ref[...] = (x * rs * w_ref[...]).astype(BF16)

    return pl.pallas_call(
        body, name=name, grid=(r // tr,),
        in_specs=[pl.BlockSpec((tr, d), lambda i: (i, 0)), pl.BlockSpec((1, d), lambda i: (0, 0))],
        out_specs=pl.BlockSpec((tr, d), lambda i: (i, 0)),
        out_shape=jax.ShapeDtypeStruct((r, d), BF16), compiler_params=_params(),
    )(h, w)


def _rms_bwd(name, x, w, dy, dres):
    r, d = x.shape
    tr = _tile(r, 264, 8)

    def body(x_ref, w_ref, dy_ref, dr_ref, dx_ref, dxb_ref, dw_ref):
        xv = x_ref[...]
        g = dy_ref[...]
        rs = lax.rsqrt(jnp.mean(xv * xv, axis=1, keepdims=True) + EPS)
        wg = g * w_ref[...]
        dx = rs * wg - xv * (rs * rs * rs) * jnp.mean(xv * wg, axis=1, keepdims=True) + dr_ref[...]
        dx_ref[...] = dx
        dxb_ref[...] = dx.astype(BF16)
        part = jnp.sum(g * xv * rs, axis=0, keepdims=True)

        @pl.when(pl.program_id(0) == 0)
        def _():
            dw_ref[...] = part

        @pl.when(pl.program_id(0) > 0)
        def _():
            dw_ref[...] += part

    row = pl.BlockSpec((tr, d), lambda i: (i, 0))
    vec = pl.BlockSpec((1, d), lambda i: (0, 0))
    return pl.pallas_call(
        body, name=name, grid=(r // tr,), in_specs=[row, vec, row, row], out_specs=[row, row, vec],
        out_shape=[jax.ShapeDtypeStruct((r, d), F32), jax.ShapeDtypeStruct((r, d), BF16),
                   jax.ShapeDtypeStruct((1, d), F32)],
        compiler_params=_params(),
    )(x, w, dy, dres)


def _final_loss(name, h, w, target):
    r, d = h.shape
    nb = r // CHUNK

    def body(h_ref, w_ref, t_ref, dh_ref, dhb_ref, dw_ref, ls_ref):
        i = pl.program_id(0)

        @pl.when(i == 0)
        def _():
            dh_ref[...] = jnp.zeros_like(dh_ref)
            dhb_ref[...] = jnp.zeros_like(dhb_ref)
            dw_ref[...] = jnp.zeros_like(dw_ref)
            ls_ref[...] = jnp.zeros_like(ls_ref)

        @pl.when(i > 0)
        def _():
            xv = h_ref[...]
            wv = w_ref[...]
            rs = lax.rsqrt(jnp.mean(xv * xv, axis=1, keepdims=True) + EPS)
            err = xv * rs * wv - t_ref[...]
            sq = jnp.sum(jnp.sum(err * err, axis=1, keepdims=True), axis=0, keepdims=True)
            ls_ref[...] += jnp.broadcast_to(sq * (0.5 / d), ls_ref.shape)
            g = err * (1.0 / d)
            wg = g * wv
            dx = rs * wg - xv * (rs * rs * rs) * jnp.mean(xv * wg, axis=1, keepdims=True)
            dh_ref[...] = dx
            dhb_ref[...] = dx.astype(BF16)
            dw_ref[...] += jnp.sum(g * xv * rs, axis=0, keepdims=True)

    row = pl.BlockSpec((CHUNK, d), lambda i: (i, 0))
    vec = pl.BlockSpec((1, d), lambda i: (0, 0))
    return pl.pallas_call(
        body, name=name, grid=(nb,),
        in_specs=[row, vec, pl.BlockSpec((CHUNK, d), lambda i: (jnp.maximum(i - 1, 0), 0))],
        out_specs=[row, row, vec, pl.BlockSpec((1, 128), lambda i: (0, 0))],
        out_shape=[jax.ShapeDtypeStruct((r, d), F32), jax.ShapeDtypeStruct((r, d), BF16),
                   jax.ShapeDtypeStruct((1, d), F32), jax.ShapeDtypeStruct((1, 128), F32)],
        compiler_params=_params(),
    )(h, w, target)


def _swiglu_fwd(name, gu):
    e, _, r, f = gu.shape
    tr = _tile(r, 528, 8)

    def body(gu_ref, a_ref):
        g = gu_ref[0]
        a_ref[...] = (g * _sigmoid(g) * gu_ref[1]).astype(BF16)

    return pl.pallas_call(
        body, name=name, grid=(e, r // tr),
        in_specs=[pl.BlockSpec((None, 2, tr, f), lambda s, i: (s, 0, i, 0))],
        out_specs=pl.BlockSpec((None, tr, f), lambda s, i: (s, i, 0)),
        out_shape=jax.ShapeDtypeStruct((e, r, f), BF16), compiler_params=_params(),
    )(gu)


def _swiglu_bwd(name, da, gu):
    e, _, r, f = gu.shape
    tr = _tile(r, 528, 8)

    def body(da_ref, gu_ref, o_ref):
        g = gu_ref[0]
        s = _sigmoid(g)
        d = da_ref[...]
        o_ref[0] = (d * gu_ref[1] * (s + g * s * (1.0 - s))).astype(BF16)
        o_ref[1] = (d * g * s).astype(BF16)

    pair = pl.BlockSpec((None, 2, tr, f), lambda s, i: (s, 0, i, 0))
    return pl.pallas_call(
        body, name=name, grid=(e, r // tr),
        in_specs=[pl.BlockSpec((None, tr, f), lambda s, i: (s, i, 0)), pair], out_specs=pair,
        out_shape=jax.ShapeDtypeStruct((e, 2, r, f), BF16), compiler_params=_params(),
    )(da, gu)


def _shift_down(a, k):
    row = lax.broadcasted_iota(jnp.int32, a.shape, 0)
    return jnp.where(row >= k, pltpu.roll(a, k, 0), 0.0)


def _shift_up(a, k):
    n = a.shape[0]
    row = lax.broadcasted_iota(jnp.int32, a.shape, 0)
    return jnp.where(row < n - k, pltpu.roll(a, n - k, 0), 0.0)


def _conv_fwd(name, pc, cw):
    _, r, w = pc.shape

    def body(pc_ref, cw_ref, o_ref):
        a = pc_ref[2] * pc_ref[0]
        cwv = cw_ref[...]
        conv = _shift_down(a, 2) * cwv[0:1] + _shift_down(a, 1) * cwv[1:2] + a * cwv[2:3]
        o_ref[...] = (pc_ref[1] * conv).astype(BF16)

    return pl.pallas_call(
        body, name=name, grid=(w // 128,),
        in_specs=[pl.BlockSpec((3, r, 128), lambda j: (0, 0, j)), pl.BlockSpec((8, 128), lambda j: (0, j))],
        out_specs=pl.BlockSpec((r, 128), lambda j: (0, j)),
        out_shape=jax.ShapeDtypeStruct((r, w), BF16), compiler_params=_params(),
    )(pc, cw)


def _conv_bwd(name, dcat, pc, cw):
    _, r, w = pc.shape
    nblk = w // 128

    def body(dy_ref, pc_ref, cw_ref, dpc_ref, dcw_ref):
        u, gb, gc = pc_ref[0], pc_ref[1], pc_ref[2]
        cwv = cw_ref[...]
        dy = dy_ref[...]
        a = gc * u
        a1, a2 = _shift_down(a, 1), _shift_down(a, 2)
        conv = a2 * cwv[0:1] + a1 * cwv[1:2] + a * cwv[2:3]
        dconv = dy * gb
        da = dconv * cwv[2:3] + _shift_up(dconv, 1) * cwv[1:2] + _shift_up(dconv, 2) * cwv[0:1]
        dpc_ref[0] = (da * gc).astype(BF16)
        dpc_ref[1] = (dy * conv).astype(BF16)
        dpc_ref[2] = (da * u).astype(BF16)
        row = lax.broadcasted_iota(jnp.int32, (8, 128), 0)
        dw0 = jnp.sum(dconv * a2, axis=0, keepdims=True)
        dw1 = jnp.sum(dconv * a1, axis=0, keepdims=True)
        dw2 = jnp.sum(dconv * a, axis=0, keepdims=True)
        dcw_ref[...] = jnp.where(row == 0, dw0, jnp.where(row == 1, dw1, jnp.where(row == 2, dw2, 0.0)))

    return pl.pallas_call(
        body, name=name, grid=(nblk,),
        in_specs=[pl.BlockSpec((r, 128), lambda j: (0, nblk + j)),
                  pl.BlockSpec((3, r, 128), lambda j: (0, 0, j)), pl.BlockSpec((8, 128), lambda j: (0, j))],
        out_specs=[pl.BlockSpec((3, r, 128), lambda j: (0, 0, j)), pl.BlockSpec((8, 128), lambda j: (0, j))],
        out_shape=[jax.ShapeDtypeStruct((3, r, w), BF16), jax.ShapeDtypeStruct((8, w), F32)],
        compiler_params=_params(),
    )(dcat, pc, cw)


def _dot(a, b, dims):
    return lax.dot_general(a, b, (dims, ((), ())), preferred_element_type=F32)


def _col_to_row(xc, eye):
    return jnp.sum(jnp.where(eye, xc, 0.0), axis=0, keepdims=True)


def _row_to_col(xr, eye):
    return jnp.sum(jnp.where(eye, xr, 0.0), axis=1, keepdims=True)


def _gate_tiles(graw, bias, row0):
    th = jnp.tanh((graw + bias) / GATE_CAP)
    z = GATE_CAP * th
    row = lax.broadcasted_iota(jnp.int32, graw.shape, 0) + row0
    real = row >= PAD_FRONT
    li = jnp.where(real, z, -jnp.inf)
    lf = jnp.where(real, jnp.minimum(z, 0.0) - jnp.log(1.0 + jnp.exp(-jnp.abs(z))), 0.0)
    return th, z, li, lf, real


def _chunk_common(pm, h, li, lf, cst, nst, mst, tril, eye):
    kraw = pm[:, QK_W + h * DQK:QK_W + (h + 1) * DQK]
    q = (pm[:, h * DQK:(h + 1) * DQK] * QSCALE).astype(BF16)
    k = kraw.astype(BF16)
    v = pm[:, 2 * QK_W + h * DV:2 * QK_W + (h + 1) * DV].astype(BF16)
    li_c = li[:, h:h + 1]
    lf_c = lf[:, HEADS + h:HEADS + h + 1]
    li_r = _col_to_row(li_c, eye)
    lf_r = _col_to_row(lf_c, eye)
    b_c = jnp.sum(jnp.where(tril, lf_r, 0.0), axis=1, keepdims=True)
    b_r = _col_to_row(b_c, eye)
    dmat = jnp.where(tril, b_c - b_r + li_r, -jnp.inf)
    inter = b_c + mst
    mt = jnp.maximum(inter, jnp.max(dmat, axis=1, keepdims=True))
    w_inter = jnp.exp(inter - mt)
    p = jnp.exp(dmat - mt)
    s = _dot(q, k, NT) * p
    cb = cst.astype(BF16)
    nb = nst.astype(BF16).astype(F32)
    qc = _dot(q, cb, NN)
    qn = jnp.sum(q.astype(F32) * nb, axis=1, keepdims=True)
    den = w_inter * qn + jnp.sum(s, axis=1, keepdims=True)
    dn = jnp.maximum(jnp.abs(den), jnp.exp(-mt))
    b_end = b_c[CHUNK - 1:CHUNK, :]
    decay = b_end - b_c + li_c
    m_new = jnp.maximum(b_end + mst, jnp.max(decay, axis=0, keepdims=True))
    w_old = jnp.exp(b_end + mst - m_new)
    w_in = jnp.exp(decay - m_new)
    kw = (w_in * kraw).astype(BF16)
    return dict(q=q, k=k, v=v, kraw=kraw, mt=mt, w_inter=w_inter, p=p, s=s, cb=cb, nb=nb, qc=qc, qn=qn,
                den=den, dn=dn, m_new=m_new, w_old=w_old, w_in=w_in, kw=kw)


def _mlstm_fwd(name, pm, bias, nw):
    r = pm.shape[0]
    nc = r // CHUNK

    def body(pm_ref, b_ref, nw_ref, hm_ref, ht_ref, cs_ref, ns_ref, ms_ref, c_scr, n_scr, m_scr):
        ci = pl.program_id(0)

        @pl.when(ci == 0)
        def _():
            c_scr[...] = jnp.zeros_like(c_scr)
            n_scr[...] = jnp.zeros_like(n_scr)
            m_scr[...] = jnp.zeros_like(m_scr)

        pmv = pm_ref[...]
        rr = lax.broadcasted_iota(jnp.int32, (CHUNK, CHUNK), 0)
        cc = lax.broadcasted_iota(jnp.int32, (CHUNK, CHUNK), 1)
        tril, eye = cc <= rr, cc == rr
        _, _, li, lf, _ = _gate_tiles(pmv[:, GATE_COL:GATE_COL + 128], b_ref[...], ci * CHUNK)
        nwv = nw_ref[...]
        for h in range(HEADS):
            cst, nst, mst = c_scr[h], n_scr[h], m_scr[h]
            cs_ref[h] = cst
            ns_ref[h] = nst
            ms_ref[h] = mst
            f = _chunk_common(pmv, h, li, lf, cst, nst, mst, tril, eye)
            num = f["w_inter"] * f["qc"] + _dot(f["s"].astype(BF16), f["v"], NN)
            hh = num / f["dn"]
            c_scr[h] = f["w_old"] * cst + _dot(f["kw"], f["v"], TN)
            n_scr[h] = f["w_old"] * nst + jnp.sum(
                f["w_in"].astype(BF16).astype(F32) * f["k"].astype(F32), axis=0, keepdims=True)
            m_scr[h] = f["m_new"]
            sl = slice(h * DV, (h + 1) * DV)
            rs = lax.rsqrt(jnp.mean(hh * hh, axis=1, keepdims=True) + EPS)
            og = pmv[:, 2 * QK_W + MLSTM_W + h * DV:2 * QK_W + MLSTM_W + (h + 1) * DV]
            ht_ref[:, sl] = hh
            hm_ref[:, sl] = (_sigmoid(og) * (hh * rs * nwv[:, sl])).astype(BF16)

    return pl.pallas_call(
        body, name=name, grid=(nc,),
        in_specs=[pl.BlockSpec((CHUNK, PM_W), lambda i: (i, 0)), pl.BlockSpec((1, 128), lambda i: (0, 0)),
                  pl.BlockSpec((1, MLSTM_W), lambda i: (0, 0))],
        out_specs=[pl.BlockSpec((CHUNK, MLSTM_W), lambda i: (i, 0)),
                   pl.BlockSpec((CHUNK, MLSTM_W), lambda i: (i, 0)),
                   pl.BlockSpec((None, HEADS, DQK, DV), lambda i: (i, 0, 0, 0)),
                   pl.BlockSpec((None, HEADS, 1, DQK), lambda i: (i, 0, 0, 0)),
                   pl.BlockSpec((None, HEADS, 1, 1), lambda i: (i, 0, 0, 0))],
        out_shape=[jax.ShapeDtypeStruct((r, MLSTM_W), BF16), jax.ShapeDtypeStruct((r, MLSTM_W), F32),
                   jax.ShapeDtypeStruct((nc, HEADS, DQK, DV), F32),
                   jax.ShapeDtypeStruct((nc, HEADS, 1, DQK), F32),
                   jax.ShapeDtypeStruct((nc, HEADS, 1, 1), F32)],
        scratch_shapes=[pltpu.VMEM((HEADS, DQK, DV), F32), pltpu.VMEM((HEADS, 1, DQK), F32),
                        pltpu.VMEM((HEADS, 1, 1), F32)],
        compiler_params=_params(),
    )(pm, bias, nw)


def _mlstm_bwd(name, dcat, pm, ht, cs, ns, ms, bias, nw):
    r = pm.shape[0]
    nc = r // CHUNK

    def body(dy_ref, pm_ref, ht_ref, cs_ref, ns_ref, ms_ref, b_ref, nw_ref, dpm_ref, dnw_ref, db_ref,
             dc_scr, dn_scr):
        step = pl.program_id(0)
        ci = nc - 1 - step

        @pl.when(step == 0)
        def _():
            dc_scr[...] = jnp.zeros_like(dc_scr)
            dn_scr[...] = jnp.zeros_like(dn_scr)
            dnw_ref[...] = jnp.zeros_like(dnw_ref)
            db_ref[...] = jnp.zeros_like(db_ref)

        pmv = pm_ref[...]
        rr = lax.broadcasted_iota(jnp.int32, (CHUNK, CHUNK), 0)
        cc = lax.broadcasted_iota(jnp.int32, (CHUNK, CHUNK), 1)
        tril, eye, triu = cc <= rr, cc == rr, cc >= rr
        th, z, li, lf, real = _gate_tiles(pmv[:, GATE_COL:GATE_COL + 128], b_ref[...], ci * CHUNK)
        lane = lax.broadcasted_iota(jnp.int32, (CHUNK, 128), 1)
        rowid = lax.broadcasted_iota(jnp.int32, (CHUNK, 1), 0)
        nwv = nw_ref[...]
        dgt = jnp.zeros((CHUNK, 128), F32)
        for h in range(HEADS):
            cst, nst, mst = cs_ref[h], ns_ref[h], ms_ref[h]
            f = _chunk_common(pmv, h, li, lf, cst, nst, mst, tril, eye)
            q, k, v, s, p = f["q"], f["k"], f["v"], f["s"], f["p"]
            w_inter, w_in, w_old, dn = f["w_inter"], f["w_in"], f["w_old"], f["dn"]
            sl = slice(h * DV, (h + 1) * DV)
            osl = slice(2 * QK_W + MLSTM_W + h * DV, 2 * QK_W + MLSTM_W + (h + 1) * DV)
            hh = ht_ref[:, sl]
            y = dy_ref[:, sl]
            sg = _sigmoid(pmv[:, osl])
            rs = lax.rsqrt(jnp.mean(hh * hh, axis=1, keepdims=True) + EPS)
            nwh = nwv[:, sl]
            dpm_ref[:, osl] = (y * (hh * rs * nwh) * sg * (1.0 - sg)).astype(BF16)
            dhn = y * sg
            dnw_ref[:, sl] += jnp.sum(dhn * hh * rs, axis=0, keepdims=True)
            wd = dhn * nwh
            dhh = rs * wd - hh * (rs * rs * rs) * jnp.mean(hh * wd, axis=1, keepdims=True)
            dnum = dhh / dn
            dd = -jnp.sum(dhh * hh, axis=1, keepdims=True) / dn
            dden = jnp.where(jnp.abs(f["den"]) > jnp.exp(-f["mt"]), dd * jnp.sign(f["den"]), 0.0)
            dnum_b = dnum.astype(BF16)
            wdn = (w_inter * dnum).astype(BF16)
            wid = (w_inter * dden).astype(BF16).astype(F32)
            ds = _dot(dnum_b, v, NT) + dden
            dsp = (ds * p).astype(BF16)
            dq = _dot(dsp, k, NN) + _dot(wdn, f["cb"], NT) + wid * f["nb"]
            dk = _dot(dsp, q, TN)
            dv = _dot(s.astype(BF16), dnum_b, TN)
            g = ds * s
            g_col = _row_to_col(jnp.sum(g, axis=0, keepdims=True), eye)
            db = jnp.sum(g, axis=1, keepdims=True) - g_col
            dli = g_col
            db = db + (jnp.sum(dnum * f["qc"], axis=1, keepdims=True) + dden * f["qn"]) * w_inter
            dcn, dnn = dc_scr[h], dn_scr[h]
            dcnb = dcn.astype(BF16)
            dnnb = dnn.astype(BF16).astype(F32)
            dkw = _dot(v, dcnb, NT) + dnnb
            dk = dk + w_in * dkw
            dv = dv + _dot(f["kw"], dcnb, NN)
            ddecay = jnp.sum(dkw * f["kraw"], axis=1, keepdims=True) * w_in
            dw_old = (jnp.sum(jnp.sum(dcn * cst, axis=1, keepdims=True), axis=0, keepdims=True)
                      + jnp.sum(dnn * nst, axis=1, keepdims=True))
            db_end = dw_old * w_old + jnp.sum(ddecay, axis=0, keepdims=True)
            db = db - ddecay + jnp.where(rowid == CHUNK - 1, db_end, 0.0)
            dli = dli + ddecay
            dc_scr[h] = w_old * dcn + _dot(q, wdn, TN)
            dn_scr[h] = w_old * dnn + jnp.sum(wid * q.astype(F32), axis=0, keepdims=True)
            dlf = jnp.sum(jnp.where(triu, _col_to_row(db, eye), 0.0), axis=1, keepdims=True)
            dgt = dgt + jnp.where(lane == h, dli, 0.0) + jnp.where(lane == HEADS + h, dlf, 0.0)
            dpm_ref[:, h * DQK:(h + 1) * DQK] = (dq * QSCALE).astype(BF16)
            dpm_ref[:, QK_W + h * DQK:QK_W + (h + 1) * DQK] = dk.astype(BF16)
            dpm_ref[:, 2 * QK_W + h * DV:2 * QK_W + (h + 1) * DV] = dv.astype(BF16)
        dact = jnp.where(lane < HEADS, 1.0, 1.0 - _sigmoid(z)) * (1.0 - th * th)
        dgraw = jnp.where(real & (lane < 2 * HEADS), dgt * dact, 0.0)
        dpm_ref[:, GATE_COL:GATE_COL + 128] = dgraw.astype(BF16)
        db_ref[...] += jnp.sum(dgraw, axis=0, keepdims=True)

    rev = lambda i: (nc - 1 - i, 0)
    rev4 = lambda i: (nc - 1 - i, 0, 0, 0)
    return pl.pallas_call(
        body, name=name, grid=(nc,),
        in_specs=[pl.BlockSpec((CHUNK, MLSTM_W), rev), pl.BlockSpec((CHUNK, PM_W), rev),
                  pl.BlockSpec((CHUNK, MLSTM_W), rev),
                  pl.BlockSpec((None, HEADS, DQK, DV), rev4), pl.BlockSpec((None, HEADS, 1, DQK), rev4),
                  pl.BlockSpec((None, HEADS, 1, 1), rev4),
                  pl.BlockSpec((1, 128), lambda i: (0, 0)), pl.BlockSpec((1, MLSTM_W), lambda i: (0, 0))],
        out_specs=[pl.BlockSpec((CHUNK, PM_W), rev), pl.BlockSpec((1, MLSTM_W), lambda i: (0, 0)),
                   pl.BlockSpec((1, 128), lambda i: (0, 0))],
        out_shape=[jax.ShapeDtypeStruct((r, PM_W), BF16), jax.ShapeDtypeStruct((1, MLSTM_W), F32),
                   jax.ShapeDtypeStruct((1, 128), F32)],
        scratch_shapes=[pltpu.VMEM((HEADS, DQK, DV), F32), pltpu.VMEM((HEADS, 1, DQK), F32)],
        compiler_params=_params(),
    )(dcat, pm, ht, cs, ns, ms, bias, nw)


def _my_place():
    return lax.axis_index("x"), lax.axis_index("y"), lax.axis_index("c")


def _flip(v, bit):
    return 1 - v if bit else v


def _exchange_small(name, blk, reduce):
    r, c = blk.shape

    def body(x_ref, o_ref, *rest):
        slots = rest[0] if reduce else o_ref
        send_sems, recv_sems = rest[-2], rest[-1]
        x, y, cc = _my_place()
        me = 4 * x + 2 * y + cc
        slots[me] = x_ref[...]
        copies = []
        for k in range(1, N_DEV):
            peer = (_flip(x, k & 4), _flip(y, k & 2), _flip(cc, k & 1))
            cp = pltpu.make_async_remote_copy(
                src_ref=x_ref, dst_ref=slots.at[me], send_sem=send_sems.at[k - 1],
                recv_sem=recv_sems.at[k - 1], device_id=peer, device_id_type=MESH)
            cp.start()
            copies.append(cp)
        for cp in copies:
            cp.wait()
        if reduce:
            acc = slots[0]
            for d in range(1, N_DEV):
                acc = acc + slots[d]
            o_ref[...] = acc

    scratch = ([pltpu.VMEM((N_DEV, r, c), F32)] if reduce else []) + [
        pltpu.SemaphoreType.DMA((N_DEV - 1,)), pltpu.SemaphoreType.DMA((N_DEV - 1,))]
    return pl.pallas_call(
        body, name=name,
        out_shape=jax.ShapeDtypeStruct((r, c) if reduce else (N_DEV, r, c), F32),
        in_specs=[pl.BlockSpec(memory_space=pltpu.VMEM)], out_specs=pl.BlockSpec(memory_space=pltpu.VMEM),
        scratch_shapes=scratch, compiler_params=_params(),
    )(blk)


def _all_gather(name, shards):
    n = len(shards)

    def body(*refs):
        ins, outs = refs[:n], refs[n:2 * n]
        send_sems, recv_sems, local_sems = refs[2 * n:]
        x, y, c = _my_place()
        sibling = (x, y, 1 - c)
        chips = [(1 - x, y), (x, 1 - y), (1 - x, 1 - y)]

        def slot(px, py, pc):
            return 4 * px + 2 * py + pc

        def copy(a, k, block, to, src=None):
            dst = outs[a].at[slot(*block)]
            return pltpu.make_async_remote_copy(
                src_ref=dst if src is None else src, dst_ref=dst, send_sem=send_sems.at[a, k],
                recv_sem=recv_sems.at[a, k], device_id=to, device_id_type=MESH)

        mine, first, passed = [], [], []
        for a in range(n):
            cp = pltpu.make_async_copy(ins[a], outs[a].at[slot(x, y, c)], local_sems.at[a])
            cp.start()
            mine.append(cp)
            for j, chip in enumerate(chips):
                first.append(copy(a, 1 + j, (x, y, c), (*chip, c), src=ins[a]))
                first[-1].start()
        for a in range(n):
            first.append(copy(a, 0, (x, y, c), sibling, src=ins[a]))
            first[-1].start()
        for j, chip in enumerate(chips):
            for a in range(n):
                copy(a, 1 + j, (*chip, c), (x, y, c)).wait_recv()
                passed.append(copy(a, 4 + j, (*chip, c), sibling))
                passed[-1].start()
        for a in range(n):
            copy(a, 0, (x, y, 1 - c), (x, y, c)).wait_recv()
            for j, chip in enumerate(chips):
                copy(a, 4 + j, (*chip, 1 - c), (x, y, c)).wait_recv()
        for cp in first + passed:
            cp.wait_send()
        for cp in mine:
            cp.wait()

    anyspec = pl.BlockSpec(memory_space=pl.ANY)
    return pl.pallas_call(
        body, name=name,
        out_shape=[jax.ShapeDtypeStruct((N_DEV,) + s.shape, s.dtype) for s in shards],
        in_specs=[anyspec] * n, out_specs=[anyspec] * n,
        scratch_shapes=[pltpu.SemaphoreType.DMA((n, 7)), pltpu.SemaphoreType.DMA((n, 7)),
                        pltpu.SemaphoreType.DMA((n,))],
        compiler_params=_params(),
    )(*shards)


def _sibling_exchange(name, grads):
    n = len(grads)

    def body(*refs):
        ins, outs = refs[:n], refs[n:2 * n]
        send_sems, recv_sems = refs[2 * n:]
        x, y, c = _my_place()
        copies = []
        for a in range(n):
            for j in range(4):
                cp = pltpu.make_async_remote_copy(
                    src_ref=ins[a].at[2 * j + 1 - c], dst_ref=outs[a].at[j], send_sem=send_sems.at[a, j],
                    recv_sem=recv_sems.at[a, j], device_id=(x, y, 1 - c), device_id_type=MESH)
                cp.start()
                copies.append(cp)
        for cp in copies:
            cp.wait()

    anyspec = pl.BlockSpec(memory_space=pl.ANY)
    return pl.pallas_call(
        body, name=name,
        out_shape=[jax.ShapeDtypeStruct((4,) + g.shape[1:], g.dtype) for g in grads],
        in_specs=[anyspec] * n, out_specs=[anyspec] * n,
        scratch_shapes=[pltpu.SemaphoreType.DMA((n, 4)), pltpu.SemaphoreType.DMA((n, 4))],
        compiler_params=_params(),
    )(*grads)


def _pair_sum(name, core, g, t):
    _, r, c = g.shape
    tr = _tile(r, 512, 8)
    g4 = g.reshape(4, 2, r, c)

    def body(core_ref, g_ref, t_ref, o_ref):
        o_ref[...] = (g_ref[...].astype(F32) + t_ref[...].astype(F32)).astype(BF16)

    return pl.pallas_call(
        body, name=name,
        grid_spec=pltpu.PrefetchScalarGridSpec(
            num_scalar_prefetch=1, grid=(4, r // tr),
            in_specs=[pl.BlockSpec((None, None, tr, c), lambda j, i, core_ref: (j, core_ref[0], i, 0)),
                      pl.BlockSpec((None, tr, c), lambda j, i, core_ref: (j, i, 0))],
            out_specs=pl.BlockSpec((None, tr, c), lambda j, i, core_ref: (j, i, 0))),
        out_shape=jax.ShapeDtypeStruct((4, r, c), BF16), compiler_params=_params(),
    )(core, g4, t)


def _chip_exchange(name, parts):
    n = len(parts)

    def body(*refs):
        ins, outs = refs[:n], refs[n:2 * n]
        send_sems, recv_sems, local_sems = refs[2 * n:]
        x, y, c = _my_place()
        chip = 2 * x + y
        copies = []
        for a in range(n):
            cp = pltpu.make_async_copy(ins[a].at[chip], outs[a].at[chip], local_sems.at[a])
            cp.start()
            copies.append(cp)
            for k in range(1, 4):
                px, py = _flip(x, k & 2), _flip(y, k & 1)
                cp = pltpu.make_async_remote_copy(
                    src_ref=ins[a].at[2 * px + py], dst_ref=outs[a].at[chip], send_sem=send_sems.at[a, k - 1],
                    recv_sem=recv_sems.at[a, k - 1], device_id=(px, py, c), device_id_type=MESH)
                cp.start()
                copies.append(cp)
        for cp in copies:
            cp.wait()

    anyspec = pl.BlockSpec(memory_space=pl.ANY)
    return pl.pallas_call(
        body, name=name,
        out_shape=[jax.ShapeDtypeStruct(p.shape, p.dtype) for p in parts],
        in_specs=[anyspec] * n, out_specs=[anyspec] * n,
        scratch_shapes=[pltpu.SemaphoreType.DMA((n, 3)), pltpu.SemaphoreType.DMA((n, 3)),
                        pltpu.SemaphoreType.DMA((n,))],
        compiler_params=_params(),
    )(*parts)


def _adam_math(w, g, m, v):
    m2 = ADAM_B1 * m + (1.0 - ADAM_B1) * g
    v2 = ADAM_B2 * v + (1.0 - ADAM_B2) * (g * g)
    m_hat = m2 / (1.0 - ADAM_B1 ** ADAM_STEP)
    v_hat = v2 / (1.0 - ADAM_B2 ** ADAM_STEP)
    delta = -ADAM_LR * (m_hat / (jnp.sqrt(v_hat) + ADAM_EPS) + ADAM_WD * w)
    return delta, m2, v2


def _adam_sharded(name, w, m, v, q0, q1, row_off=0):
    _, r, c = w.shape
    tr = _tile(r, 256, 8)
    tc = c if tr < r else _tile(c, 256, 128)
    boff = row_off // tr

    def body(w_ref, m_ref, v_ref, q0_ref, q1_ref, g_ref, d_ref, nm_ref, nv_ref):
        def total(q_ref):
            acc = q_ref[0].astype(F32)
            for j in range(1, 4):
                acc = acc + q_ref[j].astype(F32)
            return acc

        g = jnp.where(pl.program_id(0) == 0, total(q0_ref), total(q1_ref))
        delta, m2, v2 = _adam_math(w_ref[...], g, m_ref[...], v_ref[...])
        g_ref[...] = g
        d_ref[...] = delta
        nm_ref[...] = m2
        nv_ref[...] = v2

    wspec = pl.BlockSpec((None, tr, tc), lambda l, i, j: (l, i, j))
    qspec = pl.BlockSpec((4, tr, tc), lambda l, i, j: (0, boff + i, j))
    sds = jax.ShapeDtypeStruct(w.shape, F32)
    return pl.pallas_call(
        body, name=name, grid=(2, r // tr, c // tc), in_specs=[wspec, wspec, wspec, qspec, qspec],
        out_specs=[wspec] * 4, out_shape=[sds] * 4, compiler_params=_params(),
    )(w, m, v, q0, q1)


def _adam_small(name, w, m, v, g):
    def body(w_ref, m_ref, v_ref, g_ref, d_ref, nm_ref, nv_ref):
        delta, m2, v2 = _adam_math(w_ref[...], g_ref[...], m_ref[...], v_ref[...])
        d_ref[...] = delta
        nm_ref[...] = m2
        nv_ref[...] = v2

    sds = jax.ShapeDtypeStruct(w.shape, F32)
    vm = pl.BlockSpec(memory_space=pltpu.VMEM)
    return pl.pallas_call(body, name=name, in_specs=[vm] * 4, out_specs=[vm] * 3, out_shape=[sds] * 3,
                          compiler_params=_params())(w, m, v, g)


GATE_END = GATE_COL + 2 * HEADS


def _merge_dw_in(dwm_t, dwc_t):
    full = jnp.concatenate([dwm_t[:GATE_END], dwc_t.reshape(3 * CONV_W, D_MODEL)], axis=0)
    return full.reshape(N_DEV, IN_SH, D_MODEL)


def _pack128(parts):
    flat = jnp.concatenate([p.reshape(-1) for p in parts])
    n = flat.shape[0]
    rows = -(-n // 1024) * 8
    return jnp.pad(flat, (0, rows * 128 - n)).reshape(rows, 128)


def _unpack128(packed, shapes):
    flat = packed.reshape(-1)
    out, at = [], 0
    for s in shapes:
        n = int(np.prod(s))
        out.append(flat[at:at + n].reshape(s))
        at += n
    return out


def kernel(x, meta_tokens, norm_mix_w, w_in, b_gates, conv_w, mlstm_norm_w, w_out, norm_ffn_w, w_gate, w_up, w_down, norm_final_w, loss_target, m_meta_tokens, m_norm_mix_w, m_w_in, m_b_gates, m_conv_w, m_mlstm_norm_w, m_w_out, m_norm_ffn_w, m_w_gate, m_w_up, m_w_down, m_norm_final_w, v_meta_tokens, v_norm_mix_w, v_w_in, v_b_gates, v_conv_w, v_mlstm_norm_w, v_w_out, v_norm_ffn_w, v_w_gate, v_w_up, v_w_down, v_norm_final_w):
    seq = x.shape[1]
    rows = TOK0 + seq
    me = 4 * lax.axis_index("x") + 2 * lax.axis_index("y") + lax.axis_index("c")
    meta_sh = meta_tokens.shape[1]
    conv_sh = conv_w.shape[2]

    small = jnp.concatenate(
        [meta_tokens, jnp.pad(conv_w.reshape(DEPTH * 3, conv_sh), ((0, 2), (0, meta_sh - conv_sh)))], axis=0)
    slots = _exchange_small("gather_small", small, reduce=False)
    meta_full = jnp.transpose(slots[:, :N_META, :], (1, 0, 2)).reshape(N_META, D_MODEL)
    conv_full = jnp.transpose(slots[:, N_META:N_META + DEPTH * 3, :conv_sh], (1, 0, 2)).reshape(DEPTH, 3, CONV_W)
    conv_rows = [jnp.pad(conv_full[l], ((0, 5), (0, 0))) for l in range(DEPTH)]

    w_in_t, m_w_in_t, v_w_in_t = (jnp.transpose(a, (0, 2, 1)) for a in (w_in, m_w_in, v_w_in))
    w_gate_t, m_w_gate_t, v_w_gate_t = (jnp.transpose(a, (0, 2, 1)) for a in (w_gate, m_w_gate, v_w_gate))
    w_up_t, m_w_up_t, v_w_up_t = (jnp.transpose(a, (0, 2, 1)) for a in (w_up, m_w_up, v_w_up))
    shards = []
    for l in range(DEPTH):
        shards += [w_in_t[l].astype(BF16), w_out[l].astype(BF16),
                   jnp.concatenate([w_gate_t[l], w_up_t[l]], axis=0).astype(BF16), w_down[l].astype(BF16)]
    gathered = _all_gather("gather_weights", shards)
    weights = []
    for l in range(DEPTH):
        g_in, g_out, g_gu, g_down = gathered[4 * l:4 * l + 4]
        win_t = g_in.reshape(D_IN, D_MODEL)
        weights.append(dict(win_t=win_t, wc_t=win_t[GATE_END:].reshape(3, CONV_W, D_MODEL),
                            wo=g_out.reshape(D_MODEL, D_MODEL),
                            wgu_t=g_gu.reshape(2 * N_DEV, FF_SH, D_MODEL), wd=g_down))

    bias = [jnp.pad(b_gates[l].reshape(1, 2 * HEADS), ((0, 0), (0, 128 - 2 * HEADS))) for l in range(DEPTH)]
    nmix = [norm_mix_w[l].reshape(1, D_MODEL) for l in range(DEPTH)]
    nffn = [norm_ffn_w[l].reshape(1, D_MODEL) for l in range(DEPTH)]
    nmls = [mlstm_norm_w[l].reshape(1, MLSTM_W) for l in range(DEPTH)]

    h = jnp.concatenate([jnp.zeros((PAD_FRONT, D_MODEL), F32), meta_full, x[0]], axis=0)
    saved = []
    for l in range(DEPTH):
        w = weights[l]
        hn = _rms_fwd(f"norm_mix_{l}", h, nmix[l])
        pm = _mm_nt(f"proj_mlstm_{l}", hn, w["win_t"], F32, tn=640, tk=D_MODEL, n=PM_W)
        pc = _mm_nt_bcols(f"proj_conv_{l}", hn, w["wc_t"], F32)
        hm, ht, cs, ns, ms = _mlstm_fwd(f"mlstm_fwd_{l}", pm, bias[l], nmls[l])
        hc = _conv_fwd(f"conv_fwd_{l}", pc, conv_rows[l])
        cat = jnp.concatenate([hm, hc], axis=1)
        h1 = _mm_nn(f"out_proj_{l}", cat, w["wo"], F32, res=h)
        hf = _rms_fwd(f"norm_ffn_{l}", h1, nffn[l])
        gu = _mm_nt_bcols(f"ffn_in_{l}", hf, w["wgu_t"], F32).reshape(N_DEV, 2, rows, FF_SH)
        act = _swiglu_fwd(f"swiglu_{l}", gu)
        h2 = _mm_nn_ksum(f"ffn_out_{l}", act, w["wd"], F32, res=h1)
        saved.append(dict(h0=h, hn=hn, pm=pm, pc=pc, ht=ht, cs=cs, ns=ns, ms=ms, cat=cat, h1=h1, hf=hf,
                          gu=gu, act=act))
        h = h2

    dh, dh_b, d_final, loss_part = _final_loss("final_loss", h, norm_final_w.reshape(1, D_MODEL), loss_target[0])

    grads = [None] * DEPTH
    d_mix, d_ffn, d_mls, d_bias, d_conv = ([None] * DEPTH for _ in range(5))
    for l in reversed(range(DEPTH)):
        w, s = weights[l], saved[l]
        da = _mm_nt_bcols(f"d_act_{l}", dh_b, w["wd"], F32)
        dw_down = _mm_tn_acols(f"dw_down_{l}", s["act"], dh_b, BF16)
        dgu = _swiglu_bwd(f"swiglu_bwd_{l}", da, s["gu"]).reshape(2 * N_DEV, rows, FF_SH)
        dhf = _mm_nn_ksum(f"d_ffn_in_{l}", dgu, w["wgu_t"], F32)
        dw_gu = _mm_tn_acols(f"dw_ffn_in_{l}", dgu, s["hf"], BF16)
        dh1, dh1_b, d_ffn[l] = _rms_bwd(f"norm_ffn_bwd_{l}", s["h1"], nffn[l], dhf, dh)
        dcat = _mm_nt(f"d_cat_{l}", dh1_b, w["wo"], F32, tk=D_MODEL)
        dw_out = _mm_tn(f"dw_out_{l}", s["cat"], dh1_b, BF16, tn=1024)
        dpm, d_mls[l], d_bias[l] = _mlstm_bwd(f"mlstm_bwd_{l}", dcat, s["pm"], s["ht"], s["cs"], s["ns"],
                                               s["ms"], bias[l], nmls[l])
        dpc, d_conv[l] = _conv_bwd(f"conv_bwd_{l}", dcat, s["pc"], conv_rows[l])
        dhn = _mm_nn_kt(f"d_norm_mlstm_{l}", dpm, w["win_t"], F32)
        dhn = _mm_nn_ksum(f"d_norm_conv_{l}", dpc, w["wc_t"], F32, res=dhn)
        dwm_t = _mm_tn(f"dw_mlstm_{l}", dpm, s["hn"], BF16, tm=640, tn=1024)
        dwc_t = _mm_tn_acols(f"dw_conv_{l}", dpc, s["hn"], BF16)
        dh, dh_b, d_mix[l] = _rms_bwd(f"norm_mix_bwd_{l}", s["h0"], nmix[l], dhn, dh1)
        grads[l] = [_merge_dw_in(dwm_t, dwc_t), dw_out.reshape(N_DEV, OUT_SH, D_MODEL),
                    dw_gu.reshape(N_DEV, 2 * FF_SH, D_MODEL), dw_down]

    flat = grads[0] + grads[1]
    from_sibling = _sibling_exchange("grad_sibling_exchange", flat)
    core = lax.axis_index("c").astype(jnp.int32).reshape(1)
    names = ["w_in", "w_out", "w_ffn_in", "w_down"]
    parts = [_pair_sum(f"grad_pair_sum_{names[i % 4]}_{i // 4}", core, flat[i], from_sibling[i])
             for i in range(len(flat))]
    q = _chip_exchange("grad_chip_exchange", parts)
    untransposed = lambda outs: [jnp.transpose(o, (0, 2, 1)) for o in outs]
    g_in, d_in, nm_in, nv_in = untransposed(
        _adam_sharded("adam_w_in", w_in_t, m_w_in_t, v_w_in_t, q[0], q[4]))
    g_out, d_out, nm_out, nv_out = _adam_sharded("adam_w_out", w_out, m_w_out, v_w_out, q[1], q[5])
    g_gate, d_gate, nm_gate, nv_gate = untransposed(
        _adam_sharded("adam_w_gate", w_gate_t, m_w_gate_t, v_w_gate_t, q[2], q[6]))
    g_up, d_up, nm_up, nv_up = untransposed(
        _adam_sharded("adam_w_up", w_up_t, m_w_up_t, v_w_up_t, q[2], q[6], row_off=FF_SH))
    g_down, d_down, nm_down, nv_down = _adam_sharded("adam_w_down", w_down, m_w_down, v_w_down, q[3], q[7])

    bg = jnp.concatenate([d_bias[l][0, :2 * HEADS] for l in range(DEPTH)])
    red_in = jnp.concatenate([
        dh[PAD_FRONT:TOK0], d_mix[0], d_mix[1], d_ffn[0], d_ffn[1], d_final,
        jnp.concatenate([d_mls[0], d_mls[1]], axis=1),
        jnp.stack([d_conv[l][:3] for l in range(DEPTH)]).reshape(3, 2 * CONV_W),
        jnp.pad(bg, (0, D_MODEL - bg.shape[0])).reshape(1, D_MODEL),
        jnp.pad(loss_part[:, :1], ((0, 0), (0, D_MODEL - 1))),
        jnp.zeros((5, D_MODEL), F32)], axis=0)
    red = _exchange_small("reduce_small", red_in, reduce=True)
    loss = red[26, 0]
    g_meta = lax.dynamic_slice_in_dim(red[:N_META], me * meta_sh, meta_sh, axis=1)
    g_mix, g_ffn, g_final = red[16:18], red[18:20], red[20]
    g_mls = red[21].reshape(DEPTH, MLSTM_W)
    g_conv = lax.dynamic_slice_in_dim(red[22:25].reshape(DEPTH, 3, CONV_W), me * conv_sh, conv_sh, axis=2)
    g_bias = red[25, :DEPTH * 2 * HEADS].reshape(DEPTH, 2 * HEADS)

    small_w = [meta_tokens, norm_mix_w, b_gates, conv_w, mlstm_norm_w, norm_ffn_w, norm_final_w]
    small_m = [m_meta_tokens, m_norm_mix_w, m_b_gates, m_conv_w, m_mlstm_norm_w, m_norm_ffn_w, m_norm_final_w]
    small_v = [v_meta_tokens, v_norm_mix_w, v_b_gates, v_conv_w, v_mlstm_norm_w, v_norm_ffn_w, v_norm_final_w]
    small_g = [g_meta, g_mix, g_bias, g_conv, g_mls, g_ffn, g_final]
    shapes = [a.shape for a in small_w]
    packed = _adam_small("adam_small", _pack128(small_w), _pack128(small_m), _pack128(small_v), _pack128(small_g))
    (d_meta, d_nmix, d_bg, d_cw, d_nmls, d_nffn, d_nfin), (nm_meta, nm_nmix, nm_bg, nm_cw, nm_nmls, nm_nffn, nm_nfin), \
        (nv_meta, nv_nmix, nv_bg, nv_cw, nv_nmls, nv_nffn, nv_nfin) = (_unpack128(p, shapes) for p in packed)

    grad_x = dh[TOK0:].reshape(1, seq, D_MODEL)
    return (loss, grad_x,
            g_meta, g_mix, g_in, g_bias, g_conv, g_mls, g_out, g_ffn, g_gate, g_up, g_down, g_final,
            d_meta, d_nmix, d_in, d_bg, d_cw, d_nmls, d_out, d_nffn, d_gate, d_up, d_down, d_nfin,
            nm_meta, nm_nmix, nm_in, nm_bg, nm_cw, nm_nmls, nm_out, nm_nffn, nm_gate, nm_up, nm_down, nm_nfin,
            nv_meta, nv_nmix, nv_in, nv_bg, nv_cw, nv_nmls, nv_out, nv_nffn, nv_gate, nv_up, nv_down, nv_nfin)
```

```python
import functools

import numpy as np
import jax
import jax.numpy as jnp
from jax import lax
from jax.experimental import pallas as pl
from jax.experimental.pallas import tpu as pltpu

F32 = jnp.float32
BF16 = jnp.bfloat16
MESH = pl.DeviceIdType.MESH

D_MODEL = 2048
DEPTH = 2
N_META = 16
MLSTM_W = 1024
CONV_W = 1024
HEADS = 4
DV = 256
DQK = 128
QK_W = 512
CHUNK = 64
PAD_FRONT = 48
TOK0 = PAD_FRONT + N_META
D_FF = 5632
N_DEV = 8
FF_SH = D_FF // N_DEV
D_IN = 6152
IN_SH = D_IN // N_DEV
OUT_SH = D_MODEL // N_DEV
GATE_COL = 3072
PM_W = GATE_COL + 128
GATE_CAP = 15.0
EPS = 1e-6
QSCALE = DQK ** -0.5

ADAM_LR = 0.001
ADAM_B1 = 0.9
ADAM_B2 = 0.999
ADAM_EPS = 1e-08
ADAM_WD = 0.01
ADAM_STEP = 10

V7X_VMEM_LIMIT = 50 * 1024 * 1024


def _params(**kw):
    return pltpu.CompilerParams(vmem_limit_bytes=V7X_VMEM_LIMIT, **kw)


def _tile(n, target, mult):
    best = None
    for t in range(mult, min(n, target) + 1, mult):
        if n % t == 0:
            best = t
    return best if best is not None else n


def _sigmoid(x):
    return 1.0 / (1.0 + jnp.exp(-x))


NN = ((1,), (0,))
NT = ((1,), (1,))
TN = ((0,), (0,))


def _matmul(name, a, b, out_shape, out_dtype, grid, a_bs, b_bs, o_bs, dims, nk, acc_shape=None,
            res=None, res_bs=None, dep=None):
    has_res = res is not None
    n_in = 2 + has_res + (dep is not None)

    def body(*refs):
        a_ref, b_ref = refs[0], refs[1]
        r_ref = refs[2] if has_res else None
        o_ref = refs[n_in]
        x = lax.dot_general(a_ref[...], b_ref[...], (dims, ((), ())), preferred_element_type=F32)
        if nk == 1:
            if has_res:
                x = x + r_ref[...]
            o_ref[...] = x.astype(o_ref.dtype)
            return
        acc = refs[n_in + 1]
        k = pl.program_id(len(grid) - 1)

        @pl.when(k == 0)
        def _():
            acc[...] = (x + r_ref[...]) if has_res else x

        @pl.when(k > 0)
        def _():
            acc[...] += x

        @pl.when(k == nk - 1)
        def _():
            o_ref[...] = acc[...].astype(o_ref.dtype)

    ins = [a, b] + ([res] if has_res else [])
    specs = [a_bs, b_bs] + ([res_bs] if has_res else [])
    if dep is not None:
        ins.append(dep)
        specs.append(pl.BlockSpec((8, 128), lambda *_: (0, 0)))
    scratch = [pltpu.VMEM(acc_shape, F32)] if nk > 1 else []
    return pl.pallas_call(
        body, name=name, grid=grid, in_specs=specs, out_specs=o_bs,
        out_shape=jax.ShapeDtypeStruct(out_shape, out_dtype), scratch_shapes=scratch,
        compiler_params=_params(),
    )(*ins)


def _mm_nn(name, a, b, out_dtype, res=None, tm=1056, tn=512):
    r, k = a.shape
    n = b.shape[1]
    tm, tn = _tile(r, tm, 8), _tile(n, tn, 128)
    return _matmul(name, a, b, (r, n), out_dtype, (r // tm, n // tn, 1),
                   pl.BlockSpec((tm, k), lambda i, j, s: (i, 0)),
                   pl.BlockSpec((k, tn), lambda i, j, s: (0, j)),
                   pl.BlockSpec((tm, tn), lambda i, j, s: (i, j)), NN, 1,
                   res=res, res_bs=pl.BlockSpec((tm, tn), lambda i, j, s: (i, j)))


def _mm_nn_kt(name, a, b, out_dtype, tm=1056, tn=1024, tk=640, dep=None):
    r, k = a.shape
    n = b.shape[1]
    tm, tn, tk = _tile(r, tm, 8), _tile(n, tn, 128), _tile(k, tk, 128)
    nk = k // tk
    return _matmul(name, a, b, (r, n), out_dtype, (r // tm, n // tn, nk),
                   pl.BlockSpec((tm, tk), lambda i, j, s: (i, s)),
                   pl.BlockSpec((tk, tn), lambda i, j, s: (s, j)),
                   pl.BlockSpec((tm, tn), lambda i, j, s: (i, j)), NN, nk, acc_shape=(tm, tn), dep=dep)


def _mm_nn_ksum(name, a3, b3, out_dtype, res=None, tm=1056, tn=1024, dep=None):
    e, r, kb = a3.shape
    n = b3.shape[2]
    tm, tn = _tile(r, tm, 8), _tile(n, tn, 128)
    return _matmul(name, a3, b3, (r, n), out_dtype, (r // tm, n // tn, e),
                   pl.BlockSpec((None, tm, kb), lambda i, j, s: (s, i, 0)),
                   pl.BlockSpec((None, kb, tn), lambda i, j, s: (s, 0, j)),
                   pl.BlockSpec((tm, tn), lambda i, j, s: (i, j)), NN, e, acc_shape=(tm, tn),
                   res=res, res_bs=pl.BlockSpec((tm, tn), lambda i, j, s: (i, j)), dep=dep)


def _mm_nt(name, a, b, out_dtype, res=None, tm=1056, tn=512, tk=640, n=None):
    r, k = a.shape
    n = b.shape[0] if n is None else n
    tm, tn, tk = _tile(r, tm, 8), _tile(n, tn, 128), _tile(k, tk, 128)
    nk = k // tk
    return _matmul(name, a, b, (r, n), out_dtype, (r // tm, n // tn, nk),
                   pl.BlockSpec((tm, tk), lambda i, j, s: (i, s)),
                   pl.BlockSpec((tn, tk), lambda i, j, s: (j, s)),
                   pl.BlockSpec((tm, tn), lambda i, j, s: (i, j)), NT, nk, acc_shape=(tm, tn),
                   res=res, res_bs=pl.BlockSpec((tm, tn), lambda i, j, s: (i, j)))


def _mm_nt_bcols(name, a, b3, out_dtype, tm=1056, dep=None):
    r, k = a.shape
    e, n, _ = b3.shape
    tm = _tile(r, tm, 8)
    return _matmul(name, a, b3, (e, r, n), out_dtype, (r // tm, e, 1),
                   pl.BlockSpec((tm, k), lambda i, g, s: (i, 0)),
                   pl.BlockSpec((None, n, k), lambda i, g, s: (g, 0, 0)),
                   pl.BlockSpec((None, tm, n), lambda i, g, s: (g, i, 0)), NT, 1, dep=dep)


def _mm_tn(name, a, b, out_dtype, tm=1024, tn=640):
    r, m = a.shape
    n = b.shape[1]
    tm, tn = _tile(m, tm, 128), _tile(n, tn, 128)
    return _matmul(name, a, b, (m, n), out_dtype, (m // tm, n // tn, 1),
                   pl.BlockSpec((r, tm), lambda i, j, s: (0, i)),
                   pl.BlockSpec((r, tn), lambda i, j, s: (0, j)),
                   pl.BlockSpec((tm, tn), lambda i, j, s: (i, j)), TN, 1)


def _mm_tn_acols(name, a3, b, out_dtype, tn=1024):
    e, r, m = a3.shape
    n = b.shape[1]
    tn = _tile(n, tn, 128)
    return _matmul(name, a3, b, (e, m, n), out_dtype, (n // tn, e, 1),
                   pl.BlockSpec((None, r, m), lambda j, g, s: (g, 0, 0)),
                   pl.BlockSpec((r, tn), lambda j, g, s: (0, j)),
                   pl.BlockSpec((None, m, tn), lambda j, g, s: (g, 0, j)), TN, 1)


def _rms_fwd(name, h, w):
    r, d = h.shape
    tr = _tile(r, 264, 8)

    def body(h_ref, w_ref, o_ref):
        x = h_ref[...]
        rs = lax.rsqrt(jnp.mean(x * x, axis=1, keepdims=True) + EPS)
        o_ref[...] = (x * rs * w_ref[...]).astype(BF16)

    return pl.pallas_call(
        body, name=name, grid=(r // tr,),
        in_specs=[pl.BlockSpec((tr, d), lambda i: (i, 0)), pl.BlockSpec((1, d), lambda i: (0, 0))],
        out_specs=pl.BlockSpec((tr, d), lambda i: (i, 0)),
        out_shape=jax.ShapeDtypeStruct((r, d), BF16), compiler_params=_params(),
    )(h, w)


def _rms_bwd(name, x, w, dy, dres):
    r, d = x.shape
    tr = _tile(r, 264, 8)

    def body(x_ref, w_ref, dy_ref, dr_ref, dx_ref, dxb_ref, dw_ref):
        xv = x_ref[...]
        g = dy_ref[...]
        rs = lax.rsqrt(jnp.mean(xv * xv, axis=1, keepdims=True) + EPS)
        wg = g * w_ref[...]
        dx = rs * wg - xv * (rs * rs * rs) * jnp.mean(xv * wg, axis=1, keepdims=True) + dr_ref[...]
        dx_ref[...] = dx
        dxb_ref[...] = dx.astype(BF16)
        part = jnp.sum(g * xv * rs, axis=0, keepdims=True)

        @pl.when(pl.program_id(0) == 0)
        def _():
            dw_ref[...] = part

        @pl.when(pl.program_id(0) > 0)
        def _():
            dw_ref[...] += part

    row = pl.BlockSpec((tr, d), lambda i: (i, 0))
    vec = pl.BlockSpec((1, d), lambda i: (0, 0))
    return pl.pallas_call(
        body, name=name, grid=(r // tr,), in_specs=[row, vec, row, row], out_specs=[row, row, vec],
        out_shape=[jax.ShapeDtypeStruct((r, d), F32), jax.ShapeDtypeStruct((r, d), BF16),
                   jax.ShapeDtypeStruct((1, d), F32)],
        compiler_params=_params(),
    )(x, w, dy, dres)


def _final_loss(name, h, w, target):
    r, d = h.shape
    nb = r // CHUNK

    def body(h_ref, w_ref, t_ref, dh_ref, dhb_ref, dw_ref, ls_ref):
        i = pl.program_id(0)

        @pl.when(i == 0)
        def _():
            dh_ref[...] = jnp.zeros_like(dh_ref)
            dhb_ref[...] = jnp.zeros_like(dhb_ref)
            dw_ref[...] = jnp.zeros_like(dw_ref)
            ls_ref[...] = jnp.zeros_like(ls_ref)

        @pl.when(i > 0)
        def _():
            xv = h_ref[...]
            wv = w_ref[...]
            rs = lax.rsqrt(jnp.mean(xv * xv, axis=1, keepdims=True) + EPS)
            err = xv * rs * wv - t_ref[...]
            sq = jnp.sum(jnp.sum(err * err, axis=1, keepdims=True), axis=0, keepdims=True)
            ls_ref[...] += jnp.broadcast_to(sq * (0.5 / d), ls_ref.shape)
            g = err * (1.0 / d)
            wg = g * wv
            dx = rs * wg - xv * (rs * rs * rs) * jnp.mean(xv * wg, axis=1, keepdims=True)
            dh_ref[...] = dx
            dhb_ref[...] = dx.astype(BF16)
            dw_ref[...] += jnp.sum(g * xv * rs, axis=0, keepdims=True)

    row = pl.BlockSpec((CHUNK, d), lambda i: (i, 0))
    vec = pl.BlockSpec((1, d), lambda i: (0, 0))
    return pl.pallas_call(
        body, name=name, grid=(nb,),
        in_specs=[row, vec, pl.BlockSpec((CHUNK, d), lambda i: (jnp.maximum(i - 1, 0), 0))],
        out_specs=[row, row, vec, pl.BlockSpec((1, 128), lambda i: (0, 0))],
        out_shape=[jax.ShapeDtypeStruct((r, d), F32), jax.ShapeDtypeStruct((r, d), BF16),
                   jax.ShapeDtypeStruct((1, d), F32), jax.ShapeDtypeStruct((1, 128), F32)],
        compiler_params=_params(),
    )(h, w, target)


def _swiglu_fwd(name, gu):
    e, _, r, f = gu.shape
    tr = _tile(r, 528, 8)

    def body(gu_ref, a_ref):
        g = gu_ref[0]
        a_ref[...] = (g * _sigmoid(g) * gu_ref[1]).astype(BF16)

    return pl.pallas_call(
        body, name=name, grid=(e, r // tr),
        in_specs=[pl.BlockSpec((None, 2, tr, f), lambda s, i: (s, 0, i, 0))],
        out_specs=pl.BlockSpec((None, tr, f), lambda s, i: (s, i, 0)),
        out_shape=jax.ShapeDtypeStruct((e, r, f), BF16), compiler_params=_params(),
    )(gu)


def _swiglu_bwd(name, da, gu):
    e, _, r, f = gu.shape
    tr = _tile(r, 528, 8)

    def body(da_ref, gu_ref, o_ref):
        g = gu_ref[0]
        s = _sigmoid(g)
        d = da_ref[...]
        o_ref[0] = (d * gu_ref[1] * (s + g * s * (1.0 - s))).astype(BF16)
        o_ref[1] = (d * g * s).astype(BF16)

    pair = pl.BlockSpec((None, 2, tr, f), lambda s, i: (s, 0, i, 0))
    return pl.pallas_call(
        body, name=name, grid=(e, r // tr),
        in_specs=[pl.BlockSpec((None, tr, f), lambda s, i: (s, i, 0)), pair], out_specs=pair,
        out_shape=jax.ShapeDtypeStruct((e, 2, r, f), BF16), compiler_params=_params(),
    )(da, gu)


def _shift_down(a, k):
    row = lax.broadcasted_iota(jnp.int32, a.shape, 0)
    return jnp.where(row >= k, pltpu.roll(a, k, 0), 0.0)


def _shift_up(a, k):
    n = a.shape[0]
    row = lax.broadcasted_iota(jnp.int32, a.shape, 0)
    return jnp.where(row < n - k, pltpu.roll(a, n - k, 0), 0.0)


def _conv_fwd(name, pc, cw):
    _, r, w = pc.shape

    def body(pc_ref, cw_ref, o_ref):
        a = pc_ref[2] * pc_ref[0]
        cwv = cw_ref[...]
        conv = _shift_down(a, 2) * cwv[0:1] + _shift_down(a, 1) * cwv[1:2] + a * cwv[2:3]
        o_ref[...] = (pc_ref[1] * conv).astype(BF16)

    return pl.pallas_call(
        body, name=name, grid=(w // 128,),
        in_specs=[pl.BlockSpec((3, r, 128), lambda j: (0, 0, j)), pl.BlockSpec((8, 128), lambda j: (0, j))],
        out_specs=pl.BlockSpec((r, 128), lambda j: (0, j)),
        out_shape=jax.ShapeDtypeStruct((r, w), BF16), compiler_params=_params(),
    )(pc, cw)


def _conv_bwd(name, dcat, pc, cw):
    _, r, w = pc.shape
    nblk = w // 128

    def body(dy_ref, pc_ref, cw_ref, dpc_ref, dcw_ref):
        u, gb, gc = pc_ref[0], pc_ref[1], pc_ref[2]
        cwv = cw_ref[...]
        dy = dy_ref[...]
        a = gc * u
        a1, a2 = _shift_down(a, 1), _shift_down(a, 2)
        conv = a2 * cwv[0:1] + a1 * cwv[1:2] + a * cwv[2:3]
        dconv = dy * gb
        da = dconv * cwv[2:3] + _shift_up(dconv, 1) * cwv[1:2] + _shift_up(dconv, 2) * cwv[0:1]
        dpc_ref[0] = (da * gc).astype(BF16)
        dpc_ref[1] = (dy * conv).astype(BF16)
        dpc_ref[2] = (da * u).astype(BF16)
        row = lax.broadcasted_iota(jnp.int32, (8, 128), 0)
        dw0 = jnp.sum(dconv * a2, axis=0, keepdims=True)
        dw1 = jnp.sum(dconv * a1, axis=0, keepdims=True)
        dw2 = jnp.sum(dconv * a, axis=0, keepdims=True)
        dcw_ref[...] = jnp.where(row == 0, dw0, jnp.where(row == 1, dw1, jnp.where(row == 2, dw2, 0.0)))

    return pl.pallas_call(
        body, name=name, grid=(nblk,),
        in_specs=[pl.BlockSpec((r, 128), lambda j: (0, nblk + j)),
                  pl.BlockSpec((3, r, 128), lambda j: (0, 0, j)), pl.BlockSpec((8, 128), lambda j: (0, j))],
        out_specs=[pl.BlockSpec((3, r, 128), lambda j: (0, 0, j)), pl.BlockSpec((8, 128), lambda j: (0, j))],
        out_shape=[jax.ShapeDtypeStruct((3, r, w), BF16), jax.ShapeDtypeStruct((8, w), F32)],
        compiler_params=_params(),
    )(dcat, pc, cw)


def _dot(a, b, dims):
    return lax.dot_general(a, b, (dims, ((), ())), preferred_element_type=F32)


def _col_to_row(xc, eye):
    return jnp.sum(jnp.where(eye, xc, 0.0), axis=0, keepdims=True)


def _row_to_col(xr, eye):
    return jnp.sum(jnp.where(eye, xr, 0.0), axis=1, keepdims=True)


def _gate_tiles(graw, bias, row0):
    th = jnp.tanh((graw + bias) / GATE_CAP)
    z = GATE_CAP * th
    row = lax.broadcasted_iota(jnp.int32, graw.shape, 0) + row0
    real = row >= PAD_FRONT
    li = jnp.where(real, z, -jnp.inf)
    lf = jnp.where(real, jnp.minimum(z, 0.0) - jnp.log(1.0 + jnp.exp(-jnp.abs(z))), 0.0)
    return th, z, li, lf, real


def _chunk_common(pm, h, li, lf, cst, nst, mst, tril, eye):
    kraw = pm[:, QK_W + h * DQK:QK_W + (h + 1) * DQK]
    q = (pm[:, h * DQK:(h + 1) * DQK] * QSCALE).astype(BF16)
    k = kraw.astype(BF16)
    v = pm[:, 2 * QK_W + h * DV:2 * QK_W + (h + 1) * DV].astype(BF16)
    li_c = li[:, h:h + 1]
    lf_c = lf[:, HEADS + h:HEADS + h + 1]
    li_r = _col_to_row(li_c, eye)
    lf_r = _col_to_row(lf_c, eye)
    b_c = jnp.sum(jnp.where(tril, lf_r, 0.0), axis=1, keepdims=True)
    b_r = _col_to_row(b_c, eye)
    dmat = jnp.where(tril, b_c - b_r + li_r, -jnp.inf)
    inter = b_c + mst
    mt = jnp.maximum(inter, jnp.max(dmat, axis=1, keepdims=True))
    w_inter = jnp.exp(inter - mt)
    p = jnp.exp(dmat - mt)
    s = _dot(q, k, NT) * p
    cb = cst.astype(BF16)
    nb = nst.astype(BF16).astype(F32)
    qc = _dot(q, cb, NN)
    qn = jnp.sum(q.astype(F32) * nb, axis=1, keepdims=True)
    den = w_inter * qn + jnp.sum(s, axis=1, keepdims=True)
    dn = jnp.maximum(jnp.abs(den), jnp.exp(-mt))
    b_end = b_c[CHUNK - 1:CHUNK, :]
    decay = b_end - b_c + li_c
    m_new = jnp.maximum(b_end + mst, jnp.max(decay, axis=0, keepdims=True))
    w_old = jnp.exp(b_end + mst - m_new)
    w_in = jnp.exp(decay - m_new)
    kw = (w_in * kraw).astype(BF16)
    return dict(q=q, k=k, v=v, kraw=kraw, mt=mt, w_inter=w_inter, p=p, s=s, cb=cb, nb=nb, qc=qc, qn=qn,
                den=den, dn=dn, m_new=m_new, w_old=w_old, w_in=w_in, kw=kw)


def _mlstm_fwd(name, pm, bias, nw):
    r = pm.shape[0]
    nc = r // CHUNK

    def body(pm_ref, b_ref, nw_ref, hm_ref, ht_ref, cs_ref, ns_ref, ms_ref, c_scr, n_scr, m_scr):
        ci = pl.program_id(0)

        @pl.when(ci == 0)
        def _():
            c_scr[...] = jnp.zeros_like(c_scr)
            n_scr[...] = jnp.zeros_like(n_scr)
            m_scr[...] = jnp.zeros_like(m_scr)

        pmv = pm_ref[...]
        rr = lax.broadcasted_iota(jnp.int32, (CHUNK, CHUNK), 0)
        cc = lax.broadcasted_iota(jnp.int32, (CHUNK, CHUNK), 1)
        tril, eye = cc <= rr, cc == rr
        _, _, li, lf, _ = _gate_tiles(pmv[:, GATE_COL:GATE_COL + 128], b_ref[...], ci * CHUNK)
        nwv = nw_ref[...]
        for h in range(HEADS):
            cst, nst, mst = c_scr[h], n_scr[h], m_scr[h]
            cs_ref[h] = cst
            ns_ref[h] = nst
            ms_ref[h] = mst
            f = _chunk_common(pmv, h, li, lf, cst, nst, mst, tril, eye)
            num = f["w_inter"] * f["qc"] + _dot(f["s"].astype(BF16), f["v"], NN)
            hh = num / f["dn"]
            c_scr[h] = f["w_old"] * cst + _dot(f["kw"], f["v"], TN)
            n_scr[h] = f["w_old"] * nst + jnp.sum(
                f["w_in"].astype(BF16).astype(F32) * f["k"].astype(F32), axis=0, keepdims=True)
            m_scr[h] = f["m_new"]
            sl = slice(h * DV, (h + 1) * DV)
            rs = lax.rsqrt(jnp.mean(hh * hh, axis=1, keepdims=True) + EPS)
            og = pmv[:, 2 * QK_W + MLSTM_W + h * DV:2 * QK_W + MLSTM_W + (h + 1) * DV]
            ht_ref[:, sl] = hh
            hm_ref[:, sl] = (_sigmoid(og) * (hh * rs * nwv[:, sl])).astype(BF16)

    return pl.pallas_call(
        body, name=name, grid=(nc,),
        in_specs=[pl.BlockSpec((CHUNK, PM_W), lambda i: (i, 0)), pl.BlockSpec((1, 128), lambda i: (0, 0)),
                  pl.BlockSpec((1, MLSTM_W), lambda i: (0, 0))],
        out_specs=[pl.BlockSpec((CHUNK, MLSTM_W), lambda i: (i, 0)),
                   pl.BlockSpec((CHUNK, MLSTM_W), lambda i: (i, 0)),
                   pl.BlockSpec((None, HEADS, DQK, DV), lambda i: (i, 0, 0, 0)),
                   pl.BlockSpec((None, HEADS, 1, DQK), lambda i: (i, 0, 0, 0)),
                   pl.BlockSpec((None, HEADS, 1, 1), lambda i: (i, 0, 0, 0))],
        out_shape=[jax.ShapeDtypeStruct((r, MLSTM_W), BF16), jax.ShapeDtypeStruct((r, MLSTM_W), F32),
                   jax.ShapeDtypeStruct((nc, HEADS, DQK, DV), F32),
                   jax.ShapeDtypeStruct((nc, HEADS, 1, DQK), F32),
                   jax.ShapeDtypeStruct((nc, HEADS, 1, 1), F32)],
        scratch_shapes=[pltpu.VMEM((HEADS, DQK, DV), F32), pltpu.VMEM((HEADS, 1, DQK), F32),
                        pltpu.VMEM((HEADS, 1, 1), F32)],
        compiler_params=_params(),
    )(pm, bias, nw)


def _mlstm_bwd(name, dcat, pm, ht, cs, ns, ms, bias, nw):
    r = pm.shape[0]
    nc = r // CHUNK

    def body(dy_ref, pm_ref, ht_ref, cs_ref, ns_ref, ms_ref, b_ref, nw_ref, dpm_ref, dnw_ref, db_ref,
             dc_scr, dn_scr):
        step = pl.program_id(0)
        ci = nc - 1 - step

        @pl.when(step == 0)
        def _():
            dc_scr[...] = jnp.zeros_like(dc_scr)
            dn_scr[...] = jnp.zeros_like(dn_scr)
            dnw_ref[...] = jnp.zeros_like(dnw_ref)
            db_ref[...] = jnp.zeros_like(db_ref)

        pmv = pm_ref[...]
        rr = lax.broadcasted_iota(jnp.int32, (CHUNK, CHUNK), 0)
        cc = lax.broadcasted_iota(jnp.int32, (CHUNK, CHUNK), 1)
        tril, eye, triu = cc <= rr, cc == rr, cc >= rr
        th, z, li, lf, real = _gate_tiles(pmv[:, GATE_COL:GATE_COL + 128], b_ref[...], ci * CHUNK)
        lane = lax.broadcasted_iota(jnp.int32, (CHUNK, 128), 1)
        rowid = lax.broadcasted_iota(jnp.int32, (CHUNK, 1), 0)
        nwv = nw_ref[...]
        dgt = jnp.zeros((CHUNK, 128), F32)
        for h in range(HEADS):
            cst, nst, mst = cs_ref[h], ns_ref[h], ms_ref[h]
            f = _chunk_common(pmv, h, li, lf, cst, nst, mst, tril, eye)
            q, k, v, s, p = f["q"], f["k"], f["v"], f["s"], f["p"]
            w_inter, w_in, w_old, dn = f["w_inter"], f["w_in"], f["w_old"], f["dn"]
            sl = slice(h * DV, (h + 1) * DV)
            osl = slice(2 * QK_W + MLSTM_W + h * DV, 2 * QK_W + MLSTM_W + (h + 1) * DV)
            hh = ht_ref[:, sl]
            y = dy_ref[:, sl]
            sg = _sigmoid(pmv[:, osl])
            rs = lax.rsqrt(jnp.mean(hh * hh, axis=1, keepdims=True) + EPS)
            nwh = nwv[:, sl]
            dpm_ref[:, osl] = (y * (hh * rs * nwh) * sg * (1.0 - sg)).astype(BF16)
            dhn = y * sg
            dnw_ref[:, sl] += jnp.sum(dhn * hh * rs, axis=0, keepdims=True)
            wd = dhn * nwh
            dhh = rs * wd - hh * (rs * rs * rs) * jnp.mean(hh * wd, axis=1, keepdims=True)
            dnum = dhh / dn
            dd = -jnp.sum(dhh * hh, axis=1, keepdims=True) / dn
            dden = jnp.where(jnp.abs(f["den"]) > jnp.exp(-f["mt"]), dd * jnp.sign(f["den"]), 0.0)
            dnum_b = dnum.astype(BF16)
            wdn = (w_inter * dnum).astype(BF16)
            wid = (w_inter * dden).astype(BF16).astype(F32)
            ds = _dot(dnum_b, v, NT) + dden
            dsp = (ds * p).astype(BF16)
            dq = _dot(dsp, k, NN) + _dot(wdn, f["cb"], NT) + wid * f["nb"]
            dk = _dot(dsp, q, TN)
            dv = _dot(s.astype(BF16), dnum_b, TN)
            g = ds * s
            g_col = _row_to_col(jnp.sum(g, axis=0, keepdims=True), eye)
            db = jnp.sum(g, axis=1, keepdims=True) - g_col
            dli = g_col
            db = db + (jnp.sum(dnum * f["qc"], axis=1, keepdims=True) + dden * f["qn"]) * w_inter
            dcn, dnn = dc_scr[h], dn_scr[h]
            dcnb = dcn.astype(BF16)
            dnnb = dnn.astype(BF16).astype(F32)
            dkw = _dot(v, dcnb, NT) + dnnb
            dk = dk + w_in * dkw
            dv = dv + _dot(f["kw"], dcnb, NN)
            ddecay = jnp.sum(dkw * f["kraw"], axis=1, keepdims=True) * w_in
            dw_old = (jnp.sum(jnp.sum(dcn * cst, axis=1, keepdims=True), axis=0, keepdims=True)
                      + jnp.sum(dnn * nst, axis=1, keepdims=True))
            db_end = dw_old * w_old + jnp.sum(ddecay, axis=0, keepdims=True)
            db = db - ddecay + jnp.where(rowid == CHUNK - 1, db_end, 0.0)
            dli = dli + ddecay
            dc_scr[h] = w_old * dcn + _dot(q, wdn, TN)
            dn_scr[h] = w_old * dnn + jnp.sum(wid * q.astype(F32), axis=0, keepdims=True)
            dlf = jnp.sum(jnp.where(triu, _col_to_row(db, eye), 0.0), axis=1, keepdims=True)
            dgt = dgt + jnp.where(lane == h, dli, 0.0) + jnp.where(lane == HEADS + h, dlf, 0.0)
            dpm_ref[:, h * DQK:(h + 1) * DQK] = (dq * QSCALE).astype(BF16)
            dpm_ref[:, QK_W + h * DQK:QK_W + (h + 1) * DQK] = dk.astype(BF16)
            dpm_ref[:, 2 * QK_W + h * DV:2 * QK_W + (h + 1) * DV] = dv.astype(BF16)
        dact = jnp.where(lane < HEADS, 1.0, 1.0 - _sigmoid(z)) * (1.0 - th * th)
        dgraw = jnp.where(real & (lane < 2 * HEADS), dgt * dact, 0.0)
        dpm_ref[:, GATE_COL:GATE_COL + 128] = dgraw.astype(BF16)
        db_ref[...] += jnp.sum(dgraw, axis=0, keepdims=True)

    rev = lambda i: (nc - 1 - i, 0)
    rev4 = lambda i: (nc - 1 - i, 0, 0, 0)
    return pl.pallas_call(
        body, name=name, grid=(nc,),
        in_specs=[pl.BlockSpec((CHUNK, MLSTM_W), rev), pl.BlockSpec((CHUNK, PM_W), rev),
                  pl.BlockSpec((CHUNK, MLSTM_W), rev),
                  pl.BlockSpec((None, HEADS, DQK, DV), rev4), pl.BlockSpec((None, HEADS, 1, DQK), rev4),
                  pl.BlockSpec((None, HEADS, 1, 1), rev4),
                  pl.BlockSpec((1, 128), lambda i: (0, 0)), pl.BlockSpec((1, MLSTM_W), lambda i: (0, 0))],
        out_specs=[pl.BlockSpec((CHUNK, PM_W), rev), pl.BlockSpec((1, MLSTM_W), lambda i: (0, 0)),
                   pl.BlockSpec((1, 128), lambda i: (0, 0))],
        out_shape=[jax.ShapeDtypeStruct((r, PM_W), BF16), jax.ShapeDtypeStruct((1, MLSTM_W), F32),
                   jax.ShapeDtypeStruct((1, 128), F32)],
        scratch_shapes=[pltpu.VMEM((HEADS, DQK, DV), F32), pltpu.VMEM((HEADS, 1, DQK), F32)],
        compiler_params=_params(),
    )(dcat, pm, ht, cs, ns, ms, bias, nw)


def _my_place():
    return lax.axis_index("x"), lax.axis_index("y"), lax.axis_index("c")


def _flip(v, bit):
    return 1 - v if bit else v


def _exchange_small(name, blk, reduce):
    r, c = blk.shape

    def body(x_ref, o_ref, *rest):
        slots = rest[0] if reduce else o_ref
        send_sems, recv_sems = rest[-2], rest[-1]
        x, y, cc = _my_place()
        me = 4 * x + 2 * y + cc
        slots[me] = x_ref[...]
        copies = []
        for k in range(1, N_DEV):
            peer = (_flip(x, k & 4), _flip(y, k & 2), _flip(cc, k & 1))
            cp = pltpu.make_async_remote_copy(
                src_ref=x_ref, dst_ref=slots.at[me], send_sem=send_sems.at[k - 1],
                recv_sem=recv_sems.at[k - 1], device_id=peer, device_id_type=MESH)
            cp.start()
            copies.append(cp)
        for cp in copies:
            cp.wait()
        if reduce:
            acc = slots[0]
            for d in range(1, N_DEV):
                acc = acc + slots[d]
            o_ref[...] = acc

    scratch = ([pltpu.VMEM((N_DEV, r, c), F32)] if reduce else []) + [
        pltpu.SemaphoreType.DMA((N_DEV - 1,)), pltpu.SemaphoreType.DMA((N_DEV - 1,))]
    return pl.pallas_call(
        body, name=name,
        out_shape=jax.ShapeDtypeStruct((r, c) if reduce else (N_DEV, r, c), F32),
        in_specs=[pl.BlockSpec(memory_space=pltpu.VMEM)], out_specs=pl.BlockSpec(memory_space=pltpu.VMEM),
        scratch_shapes=scratch, compiler_params=_params(),
    )(blk)


HBM_SPEC = pl.BlockSpec(memory_space=pltpu.HBM)
SEM_SPEC = pl.BlockSpec(memory_space=pltpu.SEMAPHORE)
ANY_SPEC = pl.BlockSpec(memory_space=pl.ANY)
DATAFLOW = pltpu.SideEffectType.DATAFLOW_SIDE_EFFECTING


def _split_copy(name, arrays, start=None, wait=None, after=None):
    n = len(arrays)
    n_wait = 2 if wait else 0
    n_after = 0 if after is None else 1
    n_new = 2 if start else 0

    def body(*refs):
        ins = refs[:n]
        if wait:
            for cp in wait[0](ins, refs[n], refs[n + 1]):
                cp.wait_send()
                cp.wait_recv()
        if start:
            at = n + n_wait + n_after
            for cp in start[0](ins, refs[at], refs[at + 1]):
                cp.start()
            token = refs[at + 2 + n]
            token[...] = jnp.zeros_like(token)

    operands = [pltpu.with_memory_space_constraint(a, pltpu.HBM) for a in arrays]
    in_specs = [HBM_SPEC] * n
    if wait:
        operands += list(wait[1])
        in_specs += [SEM_SPEC, SEM_SPEC]
    if after is not None:
        operands.append(after)
        in_specs.append(ANY_SPEC)
    out_shape, out_specs = [], []
    if start:
        out_shape += [pltpu.SemaphoreType.DMA((start[1],)), pltpu.SemaphoreType.DMA((start[1],))]
        out_specs += [SEM_SPEC, SEM_SPEC]
    out_shape += [pltpu.HBM(a.shape, a.dtype) for a in arrays]
    out_specs += [HBM_SPEC] * n
    if start:
        out_shape.append(jax.ShapeDtypeStruct((8, 128), F32))
        out_specs.append(pl.BlockSpec(memory_space=pltpu.VMEM))
    outs = pl.pallas_call(
        body, name=name, in_specs=in_specs, out_specs=out_specs, out_shape=out_shape,
        input_output_aliases={i: n_new + i for i in range(n)},
        compiler_params=pltpu.CompilerParams(has_side_effects=DATAFLOW),
    )(*operands)
    thru = list(outs[n_new:n_new + n])
    return thru, (tuple(outs[:2]) if start else None), (outs[n_new + n] if start else None)


def _place_own(name, shards):
    n = len(shards)

    def body(*refs):
        ins, outs, sems = refs[:n], refs[n:2 * n], refs[2 * n]
        x, y, c = _my_place()
        copies = [pltpu.make_async_copy(ins[a], outs[a].at[4 * x + 2 * y + c], sems.at[a]) for a in range(n)]
        for cp in copies:
            cp.start()
        for cp in copies:
            cp.wait()

    return pl.pallas_call(
        body, name=name, in_specs=[ANY_SPEC] * n, out_specs=[ANY_SPEC] * n,
        out_shape=[jax.ShapeDtypeStruct((N_DEV,) + s.shape, s.dtype) for s in shards],
        scratch_shapes=[pltpu.SemaphoreType.DMA((n,))], compiler_params=_params(),
    )(*shards)


def _remote(src, dst, send_sems, recv_sems, k, to):
    return pltpu.make_async_remote_copy(src_ref=src, dst_ref=dst, send_sem=send_sems.at[k],
                                        recv_sem=recv_sems.at[k], device_id=to, device_id_type=MESH)


def _gather_first(n):
    def copies(refs, send_sems, recv_sems):
        x, y, c = _my_place()
        targets = [(x, y, 1 - c), (1 - x, y, c), (x, 1 - y, c), (1 - x, 1 - y, c)]
        return [_remote(refs[a], refs[n + a].at[4 * x + 2 * y + c], send_sems, recv_sems, 4 * a + k, to)
                for a in range(n) for k, to in enumerate(targets)]
    return copies


def _gather_pass(n):
    def copies(refs, send_sems, recv_sems):
        x, y, c = _my_place()
        out = []
        for a in range(n):
            for j, (px, py) in enumerate([(1 - x, y), (x, 1 - y), (1 - x, 1 - y)]):
                blk = refs[n + a].at[4 * px + 2 * py + c]
                out.append(_remote(blk, blk, send_sems, recv_sems, 3 * a + j, (x, y, 1 - c)))
        return out
    return copies


def _scatter_sibling(n):
    def copies(refs, send_sems, recv_sems):
        x, y, c = _my_place()
        return [_remote(refs[a].at[2 * j + 1 - c], refs[n + a].at[j], send_sems, recv_sems, 4 * a + j, (x, y, 1 - c))
                for a in range(n) for j in range(4)]
    return copies


def _scatter_chips(n):
    def copies(refs, send_sems, recv_sems):
        x, y, c = _my_place()
        out = []
        for a in range(n):
            for k in range(1, 4):
                px, py = _flip(x, k & 2), _flip(y, k & 1)
                out.append(_remote(refs[a].at[2 * px + py], refs[n + a].at[2 * x + y], send_sems, recv_sems,
                                   3 * a + k - 1, (px, py, c)))
        return out
    return copies


def _pair_sum(name, core, g, t):
    _, r, c = g.shape
    tr = _tile(r, 512, 8)
    g4 = g.reshape(4, 2, r, c)

    def body(core_ref, g_ref, t_ref, o_ref):
        o_ref[...] = (g_ref[...].astype(F32) + t_ref[...].astype(F32)).astype(BF16)

    return pl.pallas_call(
        body, name=name,
        grid_spec=pltpu.PrefetchScalarGridSpec(
            num_scalar_prefetch=1, grid=(4, r // tr),
            in_specs=[pl.BlockSpec((None, None, tr, c), lambda j, i, core_ref: (j, core_ref[0], i, 0)),
                      pl.BlockSpec((None, tr, c), lambda j, i, core_ref: (j, i, 0))],
            out_specs=pl.BlockSpec((None, tr, c), lambda j, i, core_ref: (j, i, 0))),
        out_shape=jax.ShapeDtypeStruct((4, r, c), BF16), compiler_params=_params(),
    )(core, g4, t)


def _adam_math(w, g, m, v):
    m2 = ADAM_B1 * m + (1.0 - ADAM_B1) * g
    v2 = ADAM_B2 * v + (1.0 - ADAM_B2) * (g * g)
    m_hat = m2 / (1.0 - ADAM_B1 ** ADAM_STEP)
    v_hat = v2 / (1.0 - ADAM_B2 ** ADAM_STEP)
    delta = -ADAM_LR * (m_hat / (jnp.sqrt(v_hat) + ADAM_EPS) + ADAM_WD * w)
    return delta, m2, v2


def _adam_sharded(name, chip, w, m, v, grads, row_off=0, dep=None):
    _, r, c = w.shape
    tr = _tile(r, 256, 8)
    tc = c if tr < r else _tile(c, 256, 128)
    boff = row_off // tr

    def body(chip_ref, w_ref, m_ref, v_ref, p0_ref, q0_ref, p1_ref, q1_ref, *rest):
        g_ref, d_ref, nm_ref, nv_ref = rest[-4:]
        mine = chip_ref[0]

        def total(p_ref, q_ref):
            acc = None
            for j in range(4):
                part = jnp.where(mine == j, p_ref[j], q_ref[j]).astype(F32)
                acc = part if acc is None else acc + part
            return acc

        g = jnp.where(pl.program_id(0) == 0, total(p0_ref, q0_ref), total(p1_ref, q1_ref))
        delta, m2, v2 = _adam_math(w_ref[...], g, m_ref[...], v_ref[...])
        g_ref[...] = g
        d_ref[...] = delta
        nm_ref[...] = m2
        nv_ref[...] = v2

    wspec = pl.BlockSpec((None, tr, tc), lambda l, i, j, chip_ref: (l, i, j))
    qspec = pl.BlockSpec((4, tr, tc), lambda l, i, j, chip_ref: (0, boff + i, j))
    sds = jax.ShapeDtypeStruct(w.shape, F32)
    ins = [chip, w, m, v, grads[0][0], grads[0][1], grads[1][0], grads[1][1]]
    in_specs = [wspec, wspec, wspec, qspec, qspec, qspec, qspec]
    if dep is not None:
        ins.append(dep)
        in_specs.append(pl.BlockSpec((8, 128), lambda *_: (0, 0)))
    return pl.pallas_call(
        body, name=name,
        grid_spec=pltpu.PrefetchScalarGridSpec(
            num_scalar_prefetch=1, grid=(2, r // tr, c // tc), in_specs=in_specs, out_specs=[wspec] * 4),
        out_shape=[sds] * 4, compiler_params=_params(),
    )(*ins)


def _adam_small(name, w, m, v, g):
    def body(w_ref, m_ref, v_ref, g_ref, d_ref, nm_ref, nv_ref):
        delta, m2, v2 = _adam_math(w_ref[...], g_ref[...], m_ref[...], v_ref[...])
        d_ref[...] = delta
        nm_ref[...] = m2
        nv_ref[...] = v2

    sds = jax.ShapeDtypeStruct(w.shape, F32)
    vm = pl.BlockSpec(memory_space=pltpu.VMEM)
    return pl.pallas_call(body, name=name, in_specs=[vm] * 4, out_specs=[vm] * 3, out_shape=[sds] * 3,
                          compiler_params=_params())(w, m, v, g)


GATE_END = GATE_COL + 2 * HEADS


def _merge_dw_in(dwm_t, dwc_t):
    full = jnp.concatenate([dwm_t[:GATE_END], dwc_t.reshape(3 * CONV_W, D_MODEL)], axis=0)
    return full.reshape(N_DEV, IN_SH, D_MODEL)


def _pack128(parts):
    flat = jnp.concatenate([p.reshape(-1) for p in parts])
    n = flat.shape[0]
    rows = -(-n // 1024) * 8
    return jnp.pad(flat, (0, rows * 128 - n)).reshape(rows, 128)


def _unpack128(packed, shapes):
    flat = packed.reshape(-1)
    out, at = [], 0
    for s in shapes:
        n = int(np.prod(s))
        out.append(flat[at:at + n].reshape(s))
        at += n
    return out


def kernel(x, meta_tokens, norm_mix_w, w_in, b_gates, conv_w, mlstm_norm_w, w_out, norm_ffn_w, w_gate, w_up, w_down, norm_final_w, loss_target, m_meta_tokens, m_norm_mix_w, m_w_in, m_b_gates, m_conv_w, m_mlstm_norm_w, m_w_out, m_norm_ffn_w, m_w_gate, m_w_up, m_w_down, m_norm_final_w, v_meta_tokens, v_norm_mix_w, v_w_in, v_b_gates, v_conv_w, v_mlstm_norm_w, v_w_out, v_norm_ffn_w, v_w_gate, v_w_up, v_w_down, v_norm_final_w):
    seq = x.shape[1]
    rows = TOK0 + seq
    me = 4 * lax.axis_index("x") + 2 * lax.axis_index("y") + lax.axis_index("c")
    meta_sh = meta_tokens.shape[1]
    conv_sh = conv_w.shape[2]

    small = jnp.concatenate(
        [meta_tokens, jnp.pad(conv_w.reshape(DEPTH * 3, conv_sh), ((0, 2), (0, meta_sh - conv_sh)))], axis=0)
    slots = _exchange_small("gather_small", small, reduce=False)
    meta_full = jnp.transpose(slots[:, :N_META, :], (1, 0, 2)).reshape(N_META, D_MODEL)
    conv_full = jnp.transpose(slots[:, N_META:N_META + DEPTH * 3, :conv_sh], (1, 0, 2)).reshape(DEPTH, 3, CONV_W)
    conv_rows = [jnp.pad(conv_full[l], ((0, 5), (0, 0))) for l in range(DEPTH)]

    w_in_t, m_w_in_t, v_w_in_t = (jnp.transpose(a, (0, 2, 1)) for a in (w_in, m_w_in, v_w_in))
    w_gate_t, m_w_gate_t, v_w_gate_t = (jnp.transpose(a, (0, 2, 1)) for a in (w_gate, m_w_gate, v_w_gate))
    w_up_t, m_w_up_t, v_w_up_t = (jnp.transpose(a, (0, 2, 1)) for a in (w_up, m_w_up, v_w_up))
    shards = []
    for l in range(DEPTH):
        shards += [w_in_t[l].astype(BF16), w_out[l].astype(BF16),
                   jnp.concatenate([w_gate_t[l], w_up_t[l]], axis=0).astype(BF16), w_down[l].astype(BF16)]
    bufs = _place_own("gather_place_own", shards)
    gather_groups = [[0], [1, 2, 3], [4], [5, 6, 7]]
    gather_state = {}

    def gather_start(g, after=None):
        idx = gather_groups[g]
        arrs, sems, tok = _split_copy(f"gather_start_{g}", [shards[i] for i in idx] + [bufs[i] for i in idx],
                                      start=(_gather_first(len(idx)), 4 * len(idx)), after=after)
        gather_state[g] = (arrs, sems)
        return tok

    def gather_pass(g, after=None):
        n = len(gather_groups[g])
        arrs, sems = gather_state[g]
        arrs, sems, tok = _split_copy(f"gather_pass_{g}", arrs, start=(_gather_pass(n), 3 * n),
                                      wait=(_gather_first(n), sems), after=after)
        gather_state[g] = (arrs, sems)
        return tok

    def gather_done(g, after=None):
        n = len(gather_groups[g])
        arrs, sems = gather_state[g]
        arrs, _, _ = _split_copy(f"gather_done_{g}", arrs, wait=(_gather_pass(n), sems), after=after)
        return arrs[n:]

    bias = [jnp.pad(b_gates[l].reshape(1, 2 * HEADS), ((0, 0), (0, 128 - 2 * HEADS))) for l in range(DEPTH)]
    nmix = [norm_mix_w[l].reshape(1, D_MODEL) for l in range(DEPTH)]
    nffn = [norm_ffn_w[l].reshape(1, D_MODEL) for l in range(DEPTH)]
    nmls = [mlstm_norm_w[l].reshape(1, MLSTM_W) for l in range(DEPTH)]
    weights = [dict() for _ in range(DEPTH)]
    saved = [dict() for _ in range(DEPTH)]

    def mixer_fwd(l, h, g_in):
        w, s = weights[l], saved[l]
        w["win_t"] = g_in.reshape(D_IN, D_MODEL)
        w["wc_t"] = w["win_t"][GATE_END:].reshape(3, CONV_W, D_MODEL)
        s["h0"] = h
        s["hn"] = _rms_fwd(f"norm_mix_{l}", h, nmix[l])
        s["pm"] = _mm_nt(f"proj_mlstm_{l}", s["hn"], w["win_t"], F32, tn=640, tk=D_MODEL, n=PM_W)
        s["pc"] = _mm_nt_bcols(f"proj_conv_{l}", s["hn"], w["wc_t"], F32)
        hm, s["ht"], s["cs"], s["ns"], s["ms"] = _mlstm_fwd(f"mlstm_fwd_{l}", s["pm"], bias[l], nmls[l])
        hc = _conv_fwd(f"conv_fwd_{l}", s["pc"], conv_rows[l])
        s["cat"] = jnp.concatenate([hm, hc], axis=1)

    def ffn_fwd(l, g_out, g_gu, g_down, between=None):
        w, s = weights[l], saved[l]
        w["wo"] = g_out.reshape(D_MODEL, D_MODEL)
        w["wgu_t"] = g_gu.reshape(2 * N_DEV, FF_SH, D_MODEL)
        w["wd"] = g_down
        s["h1"] = _mm_nn(f"out_proj_{l}", s["cat"], w["wo"], F32, res=s["h0"])
        s["hf"] = _rms_fwd(f"norm_ffn_{l}", s["h1"], nffn[l])
        s["gu"] = _mm_nt_bcols(f"ffn_in_{l}", s["hf"], w["wgu_t"], F32).reshape(N_DEV, 2, rows, FF_SH)
        tok = between(s["gu"]) if between else None
        s["act"] = _swiglu_fwd(f"swiglu_{l}", s["gu"])
        return _mm_nn_ksum(f"ffn_out_{l}", s["act"], w["wd"], F32, res=s["h1"], dep=tok)

    tok = gather_start(0)
    tok = gather_pass(0, after=tok)
    tok = gather_start(1, after=tok)
    (g_in,) = gather_done(0, after=tok)
    h = jnp.concatenate([jnp.zeros((PAD_FRONT, D_MODEL), F32), meta_full, x[0]], axis=0)
    mixer_fwd(0, h, g_in)
    tok = gather_pass(1, after=saved[0]["cat"])
    tok = gather_start(2, after=tok)
    g_out, g_gu, g_down = gather_done(1, after=tok)

    def third_group(gu):
        return gather_start(3, after=gather_pass(2, after=gu))

    h = ffn_fwd(0, g_out, g_gu, g_down, between=third_group)
    (g_in,) = gather_done(2, after=h)
    mixer_fwd(1, h, g_in)
    tok = gather_pass(3, after=saved[1]["cat"])
    g_out, g_gu, g_down = gather_done(3, after=tok)
    h = ffn_fwd(1, g_out, g_gu, g_down)

    dh, dh_b, d_final, loss_part = _final_loss("final_loss", h, norm_final_w.reshape(1, D_MODEL), loss_target[0])

    core = lax.axis_index("c").astype(jnp.int32).reshape(1)
    chip = (2 * lax.axis_index("x") + lax.axis_index("y")).astype(jnp.int32).reshape(1)

    def sibling_start(tag, grads_):
        n = len(grads_)
        land = [lax.empty((4,) + g.shape[1:], BF16) for g in grads_]
        arrs, sems, tok = _split_copy(f"grad_sibling_start_{tag}", list(grads_) + land,
                                      start=(_scatter_sibling(n), 4 * n))
        return (arrs, sems, n), tok

    def sibling_done(tag, state, names, after):
        arrs, sems, n = state
        arrs, _, _ = _split_copy(f"grad_sibling_done_{tag}", arrs, wait=(_scatter_sibling(n), sems), after=after)
        return [_pair_sum(f"grad_pair_sum_{nm}", core, arrs[a], arrs[n + a]) for a, nm in enumerate(names)]

    def chips_start(tag, parts):
        n = len(parts)
        land = [lax.empty(p.shape, BF16) for p in parts]
        arrs, sems, tok = _split_copy(f"grad_chips_start_{tag}", list(parts) + land,
                                      start=(_scatter_chips(n), 3 * n))
        return (arrs, sems, n), tok

    def chips_done(tag, state, after):
        arrs, sems, n = state
        arrs, _, _ = _split_copy(f"grad_chips_done_{tag}", arrs, wait=(_scatter_chips(n), sems), after=after)
        return list(zip(arrs[:n], arrs[n:]))

    d_mix, d_ffn, d_mls, d_bias, d_conv = ([None] * DEPTH for _ in range(5))

    def layer_bwd(l, dh, dh_b, hooks):
        w, s = weights[l], saved[l]
        hook = lambda name, value=None: hooks[name](value) if name in hooks else None
        da = _mm_nt_bcols(f"d_act_{l}", dh_b, w["wd"], F32, dep=hook("begin"))
        dw_down = _mm_tn_acols(f"dw_down_{l}", s["act"], dh_b, BF16)
        tok = hook("after_dw_down", dw_down)
        dgu = _swiglu_bwd(f"swiglu_bwd_{l}", da, s["gu"]).reshape(2 * N_DEV, rows, FF_SH)
        dhf = _mm_nn_ksum(f"d_ffn_in_{l}", dgu, w["wgu_t"], F32, dep=tok)
        dw_gu = _mm_tn_acols(f"dw_ffn_in_{l}", dgu, s["hf"], BF16).reshape(N_DEV, 2 * FF_SH, D_MODEL)
        dh1, dh1_b, d_ffn[l] = _rms_bwd(f"norm_ffn_bwd_{l}", s["h1"], nffn[l], dhf, dh)
        dcat = _mm_nt(f"d_cat_{l}", dh1_b, w["wo"], F32, tk=D_MODEL)
        dw_out = _mm_tn(f"dw_out_{l}", s["cat"], dh1_b, BF16, tn=1024).reshape(N_DEV, OUT_SH, D_MODEL)
        tok = hook("after_dw_out", [dw_out, dw_gu, dw_down])
        bias_l = bias[l] if tok is None else bias[l] + tok[:1]
        dpm, d_mls[l], d_bias[l] = _mlstm_bwd(f"mlstm_bwd_{l}", dcat, s["pm"], s["ht"], s["cs"], s["ns"],
                                               s["ms"], bias_l, nmls[l])
        dpc, d_conv[l] = _conv_bwd(f"conv_bwd_{l}", dcat, s["pc"], conv_rows[l])
        tok = hook("after_conv_bwd", dpc)
        dhn = _mm_nn_kt(f"d_norm_mlstm_{l}", dpm, w["win_t"], F32, dep=tok)
        dhn = _mm_nn_ksum(f"d_norm_conv_{l}", dpc, w["wc_t"], F32, res=dhn)
        dwm_t = _mm_tn(f"dw_mlstm_{l}", dpm, s["hn"], BF16, tm=640, tn=1024)
        dwc_t = _mm_tn_acols(f"dw_conv_{l}", dpc, s["hn"], BF16)
        dh, dh_b, d_mix[l] = _rms_bwd(f"norm_mix_bwd_{l}", s["h0"], nmix[l], dhn, dh1)
        return dh, dh_b, [_merge_dw_in(dwm_t, dwc_t), dw_out, dw_gu, dw_down]

    dh, dh_b, grads1 = layer_bwd(1, dh, dh_b, {})
    state = {}

    def begin0(_):
        state["sib1"], tok = sibling_start("l1", grads1)
        return tok

    def after_dw_down0(dw_down):
        parts = sibling_done("l1", state["sib1"], ["w_in_1", "w_out_1", "w_ffn_in_1", "w_down_1"], after=dw_down)
        state["chips1"], tok = chips_start("l1", parts)
        return tok

    def after_dw_out0(grads0_rest):
        state["sib0"], tok = sibling_start("l0_rest", grads0_rest)
        return tok

    def after_conv_bwd0(dpc):
        parts = sibling_done("l0_rest", state["sib0"], ["w_out_0", "w_ffn_in_0", "w_down_0"], after=dpc)
        state["pq1"] = chips_done("l1", state["chips1"], after=parts[0])
        state["chips0"], tok = chips_start("l0_rest", parts)
        return tok

    dh, dh_b, grads0 = layer_bwd(0, dh, dh_b, {"begin": begin0, "after_dw_down": after_dw_down0,
                                                  "after_dw_out": after_dw_out0, "after_conv_bwd": after_conv_bwd0})
    sib_in, _ = sibling_start("l0_w_in", grads0[:1])
    parts = sibling_done("l0_w_in", sib_in, ["w_in_0"], after=dh)
    pq0_rest = chips_done("l0_rest", state["chips0"], after=parts[0])
    chips_in, tok_tail = chips_start("l0_w_in", parts)
    pq1 = state["pq1"]

    untransposed = lambda outs: [jnp.transpose(o, (0, 2, 1)) for o in outs]
    g_out, d_out, nm_out, nv_out = _adam_sharded(
        "adam_w_out", chip, w_out, m_w_out, v_w_out, [pq0_rest[0], pq1[1]], dep=tok_tail)
    g_gate, d_gate, nm_gate, nv_gate = untransposed(_adam_sharded(
        "adam_w_gate", chip, w_gate_t, m_w_gate_t, v_w_gate_t, [pq0_rest[1], pq1[2]], dep=tok_tail))
    g_up, d_up, nm_up, nv_up = untransposed(_adam_sharded(
        "adam_w_up", chip, w_up_t, m_w_up_t, v_w_up_t, [pq0_rest[1], pq1[2]], row_off=FF_SH, dep=tok_tail))
    g_down, d_down, nm_down, nv_down = _adam_sharded(
        "adam_w_down", chip, w_down, m_w_down, v_w_down, [pq0_rest[2], pq1[3]], dep=tok_tail)
    (pq0_in,) = chips_done("l0_w_in", chips_in, after=nv_down)
    g_in, d_in, nm_in, nv_in = untransposed(_adam_sharded(
        "adam_w_in", chip, w_in_t, m_w_in_t, v_w_in_t, [pq0_in, pq1[0]]))

    bg = jnp.concatenate([d_bias[l][0, :2 * HEADS] for l in range(DEPTH)])
    red_in = jnp.concatenate([
        dh[PAD_FRONT:TOK0], d_mix[0], d_mix[1], d_ffn[0], d_ffn[1], d_final,
        jnp.concatenate([d_mls[0], d_mls[1]], axis=1),
        jnp.stack([d_conv[l][:3] for l in range(DEPTH)]).reshape(3, 2 * CONV_W),
        jnp.pad(bg, (0, D_MODEL - bg.shape[0])).reshape(1, D_MODEL),
        jnp.pad(loss_part[:, :1], ((0, 0), (0, D_MODEL - 1))),
        jnp.zeros((5, D_MODEL), F32) + tok_tail[0, 0]], axis=0)
    red = _exchange_small("reduce_small", red_in, reduce=True)
    loss = red[26, 0]
    g_meta = lax.dynamic_slice_in_dim(red[:N_META], me * meta_sh, meta_sh, axis=1)
    g_mix, g_ffn, g_final = red[16:18], red[18:20], red[20]
    g_mls = red[21].reshape(DEPTH, MLSTM_W)
    g_conv = lax.dynamic_slice_in_dim(red[22:25].reshape(DEPTH, 3, CONV_W), me * conv_sh, conv_sh, axis=2)
    g_bias = red[25, :DEPTH * 2 * HEADS].reshape(DEPTH, 2 * HEADS)

    small_w = [meta_tokens, norm_mix_w, b_gates, conv_w, mlstm_norm_w, norm_ffn_w, norm_final_w]
    small_m = [m_meta_tokens, m_norm_mix_w, m_b_gates, m_conv_w, m_mlstm_norm_w, m_norm_ffn_w, m_norm_final_w]
    small_v = [v_meta_tokens, v_norm_mix_w, v_b_gates, v_conv_w, v_mlstm_norm_w, v_norm_ffn_w, v_norm_final_w]
    small_g = [g_meta, g_mix, g_bias, g_conv, g_mls, g_ffn, g_final]
    shapes = [a.shape for a in small_w]
    packed = _adam_small("adam_small", _pack128(small_w), _pack128(small_m), _pack128(small_v), _pack128(small_g))
    (d_meta, d_nmix, d_bg, d_cw, d_nmls, d_nffn, d_nfin), (nm_meta, nm_nmix, nm_bg, nm_cw, nm_nmls, nm_nffn, nm_nfin), \
        (nv_meta, nv_nmix, nv_bg, nv_cw, nv_nmls, nv_nffn, nv_nfin) = (_unpack128(p, shapes) for p in packed)

    grad_x = dh[TOK0:].reshape(1, seq, D_MODEL)
    return (loss, grad_x,
            g_meta, g_mix, g_in, g_bias, g_conv, g_mls, g_out, g_ffn, g_gate, g_up, g_down, g_final,
            d_meta, d_nmix, d_in, d_bg, d_cw, d_nmls, d_out, d_nffn, d_gate, d_up, d_down, d_nfin,
            nm_meta, nm_nmix, nm_in, nm_bg, nm_cw, nm_nmls, nm_out, nm_nffn, nm_gate, nm_up, nm_down, nm_nfin,
            nv_meta, nv_nmix, nv_in, nv_bg, nv_cw, nv_nmls, nv_out, nv_nffn, nv_gate, nv_up, nv_down, nv_nfin)
```

```python
import functools

import numpy as np
import jax
import jax.numpy as jnp
from jax import lax
from jax.experimental import pallas as pl
from jax.experimental.pallas import tpu as pltpu

F32 = jnp.float32
BF16 = jnp.bfloat16
MESH = pl.DeviceIdType.MESH

D_MODEL = 2048
DEPTH = 2
N_META = 16
MLSTM_W = 1024
CONV_W = 1024
HEADS = 4
DV = 256
DQK = 128
QK_W = 512
CHUNK = 64
PAD_FRONT = 48
TOK0 = PAD_FRONT + N_META
D_FF = 5632
N_DEV = 8
FF_SH = D_FF // N_DEV
D_IN = 6152
IN_SH = D_IN // N_DEV
OUT_SH = D_MODEL // N_DEV
GATE_COL = 3072
PM_W = GATE_COL + 128
GATE_CAP = 15.0
EPS = 1e-6
QSCALE = DQK ** -0.5

ADAM_LR = 0.001
ADAM_B1 = 0.9
ADAM_B2 = 0.999
ADAM_EPS = 1e-08
ADAM_WD = 0.01
ADAM_STEP = 10

V7X_VMEM_LIMIT = 50 * 1024 * 1024


def _params(**kw):
    return pltpu.CompilerParams(vmem_limit_bytes=V7X_VMEM_LIMIT, **kw)


def _tile(n, target, mult):
    best = None
    for t in range(mult, min(n, target) + 1, mult):
        if n % t == 0:
            best = t
    return best if best is not None else n


def _sigmoid(x):
    return 1.0 / (1.0 + jnp.exp(-x))


NN = ((1,), (0,))
NT = ((1,), (1,))
TN = ((0,), (0,))


def _matmul(name, a, b, out_shape, out_dtype, grid, a_bs, b_bs, o_bs, dims, nk, acc_shape=None,
            res=None, res_bs=None, dep=None):
    has_res = res is not None
    n_in = 2 + has_res + (dep is not None)

    def body(*refs):
        a_ref, b_ref = refs[0], refs[1]
        r_ref = refs[2] if has_res else None
        o_ref = refs[n_in]
        x = lax.dot_general(a_ref[...], b_ref[...], (dims, ((), ())), preferred_element_type=F32)
        if nk == 1:
            if has_res:
                x = x + r_ref[...]
            o_ref[...] = x.astype(o_ref.dtype)
            return
        acc = refs[n_in + 1]
        k = pl.program_id(len(grid) - 1)

        @pl.when(k == 0)
        def _():
            acc[...] = (x + r_ref[...]) if has_res else x

        @pl.when(k > 0)
        def _():
            acc[...] += x

        @pl.when(k == nk - 1)
        def _():
            o_ref[...] = acc[...].astype(o_ref.dtype)

    ins = [a, b] + ([res] if has_res else [])
    specs = [a_bs, b_bs] + ([res_bs] if has_res else [])
    if dep is not None:
        ins.append(dep)
        specs.append(pl.BlockSpec((8, 128), lambda *_: (0, 0)))
    scratch = [pltpu.VMEM(acc_shape, F32)] if nk > 1 else []
    return pl.pallas_call(
        body, name=name, grid=grid, in_specs=specs, out_specs=o_bs,
        out_shape=jax.ShapeDtypeStruct(out_shape, out_dtype), scratch_shapes=scratch,
        compiler_params=_params(),
    )(*ins)


def _mm_nn(name, a, b, out_dtype, res=None, tm=1056, tn=512):
    r, k = a.shape
    n = b.shape[1]
    tm, tn = _tile(r, tm, 8), _tile(n, tn, 128)
    return _matmul(name, a, b, (r, n), out_dtype, (r // tm, n // tn, 1),
                   pl.BlockSpec((tm, k), lambda i, j, s: (i, 0)),
                   pl.BlockSpec((k, tn), lambda i, j, s: (0, j)),
                   pl.BlockSpec((tm, tn), lambda i, j, s: (i, j)), NN, 1,
                   res=res, res_bs=pl.BlockSpec((tm, tn), lambda i, j, s: (i, j)))


def _mm_nn_kt(name, a, b, out_dtype, tm=1056, tn=1024, tk=640, dep=None):
    r, k = a.shape
    n = b.shape[1]
    tm, tn, tk = _tile(r, tm, 8), _tile(n, tn, 128), _tile(k, tk, 128)
    nk = k // tk
    return _matmul(name, a, b, (r, n), out_dtype, (r // tm, n // tn, nk),
                   pl.BlockSpec((tm, tk), lambda i, j, s: (i, s)),
                   pl.BlockSpec((tk, tn), lambda i, j, s: (s, j)),
                   pl.BlockSpec((tm, tn), lambda i, j, s: (i, j)), NN, nk, acc_shape=(tm, tn), dep=dep)


def _mm_nn_ksum(name, a3, b3, out_dtype, res=None, tm=1056, tn=1024, dep=None):
    e, r, kb = a3.shape
    n = b3.shape[2]
    tm, tn = _tile(r, tm, 8), _tile(n, tn, 128)
    return _matmul(name, a3, b3, (r, n), out_dtype, (r // tm, n // tn, e),
                   pl.BlockSpec((None, tm, kb), lambda i, j, s: (s, i, 0)),
                   pl.BlockSpec((None, kb, tn), lambda i, j, s: (s, 0, j)),
                   pl.BlockSpec((tm, tn), lambda i, j, s: (i, j)), NN, e, acc_shape=(tm, tn),
                   res=res, res_bs=pl.BlockSpec((tm, tn), lambda i, j, s: (i, j)), dep=dep)


def _mm_nt(name, a, b, out_dtype, res=None, tm=1056, tn=512, tk=640, n=None, dep=None):
    r, k = a.shape
    n = b.shape[0] if n is None else n
    tm, tn, tk = _tile(r, tm, 8), _tile(n, tn, 128), _tile(k, tk, 128)
    nk = k // tk
    return _matmul(name, a, b, (r, n), out_dtype, (r // tm, n // tn, nk),
                   pl.BlockSpec((tm, tk), lambda i, j, s: (i, s)),
                   pl.BlockSpec((tn, tk), lambda i, j, s: (j, s)),
                   pl.BlockSpec((tm, tn), lambda i, j, s: (i, j)), NT, nk, acc_shape=(tm, tn),
                   res=res, res_bs=pl.BlockSpec((tm, tn), lambda i, j, s: (i, j)), dep=dep)


def _mm_nt_bcols(name, a, b3, out_dtype, tm=1056, dep=None):
    r, k = a.shape
    e, n, _ = b3.shape
    tm = _tile(r, tm, 8)
    return _matmul(name, a, b3, (e, r, n), out_dtype, (r // tm, e, 1),
                   pl.BlockSpec((tm, k), lambda i, g, s: (i, 0)),
                   pl.BlockSpec((None, n, k), lambda i, g, s: (g, 0, 0)),
                   pl.BlockSpec((None, tm, n), lambda i, g, s: (g, i, 0)), NT, 1, dep=dep)


def _mm_tn(name, a, b, out_dtype, tm=1024, tn=640, dep=None):
    r, m = a.shape
    n = b.shape[1]
    tm, tn = _tile(m, tm, 128), _tile(n, tn, 128)
    return _matmul(name, a, b, (m, n), out_dtype, (m // tm, n // tn, 1),
                   pl.BlockSpec((r, tm), lambda i, j, s: (0, i)),
                   pl.BlockSpec((r, tn), lambda i, j, s: (0, j)),
                   pl.BlockSpec((tm, tn), lambda i, j, s: (i, j)), TN, 1, dep=dep)


def _mm_tn_acols(name, a3, b, out_dtype, tn=1024, dep=None):
    e, r, m = a3.shape
    n = b.shape[1]
    tn = _tile(n, tn, 128)
    return _matmul(name, a3, b, (e, m, n), out_dtype, (n // tn, e, 1),
                   pl.BlockSpec((None, r, m), lambda j, g, s: (g, 0, 0)),
                   pl.BlockSpec((r, tn), lambda j, g, s: (0, j)),
                   pl.BlockSpec((None, m, tn), lambda j, g, s: (g, 0, j)), TN, 1, dep=dep)


def _rms_fwd(name, h, w):
    r, d = h.shape
    tr = _tile(r, 264, 8)

    def body(h_ref, w_ref, o_ref):
        x = h_ref[...]
        rs = lax.rsqrt(jnp.mean(x * x, axis=1, keepdims=True) + EPS)
        o_ref[...] = (x * rs * w_ref[...]).astype(BF16)

    return pl.pallas_call(
        body, name=name, grid=(r // tr,),
        in_specs=[pl.BlockSpec((tr, d), lambda i: (i, 0)), pl.BlockSpec((1, d), lambda i: (0, 0))],
        out_specs=pl.BlockSpec((tr, d), lambda i: (i, 0)),
        out_shape=jax.ShapeDtypeStruct((r, d), BF16), compiler_params=_params(),
    )(h, w)


def _rms_bwd(name, x, w, dy, dres):
    r, d = x.shape
    tr = _tile(r, 264, 8)

    def body(x_ref, w_ref, dy_ref, dr_ref, dx_ref, dxb_ref, dw_ref):
        xv = x_ref[...]
        g = dy_ref[...]
        rs = lax.rsqrt(jnp.mean(xv * xv, axis=1, keepdims=True) + EPS)
        wg = g * w_ref[...]
        dx = rs * wg - xv * (rs * rs * rs) * jnp.mean(xv * wg, axis=1, keepdims=True) + dr_ref[...]
        dx_ref[...] = dx
        dxb_ref[...] = dx.astype(BF16)
        part = jnp.sum(g * xv * rs, axis=0, keepdims=True)

        @pl.when(pl.program_id(0) == 0)
        def _():
            dw_ref[...] = part

        @pl.when(pl.program_id(0) > 0)
        def _():
            dw_ref[...] += part

    row = pl.BlockSpec((tr, d), lambda i: (i, 0))
    vec = pl.BlockSpec((1, d), lambda i: (0, 0))
    return pl.pallas_call(
        body, name=name, grid=(r // tr,), in_specs=[row, vec, row, row], out_specs=[row, row, vec],
        out_shape=[jax.ShapeDtypeStruct((r, d), F32), jax.ShapeDtypeStruct((r, d), BF16),
                   jax.ShapeDtypeStruct((1, d), F32)],
        compiler_params=_params(),
    )(x, w, dy, dres)


def _final_loss(name, h, w, target):
    r, d = h.shape
    nb = r // CHUNK

    def body(h_ref, w_ref, t_ref, dh_ref, dhb_ref, dw_ref, ls_ref):
        i = pl.program_id(0)

        @pl.when(i == 0)
        def _():
            dh_ref[...] = jnp.zeros_like(dh_ref)
            dhb_ref[...] = jnp.zeros_like(dhb_ref)
            dw_ref[...] = jnp.zeros_like(dw_ref)
            ls_ref[...] = jnp.zeros_like(ls_ref)

        @pl.when(i > 0)
        def _():
            xv = h_ref[...]
            wv = w_ref[...]
            rs = lax.rsqrt(jnp.mean(xv * xv, axis=1, keepdims=True) + EPS)
            err = xv * rs * wv - t_ref[...]
            sq = jnp.sum(jnp.sum(err * err, axis=1, keepdims=True), axis=0, keepdims=True)
            ls_ref[...] += jnp.broadcast_to(sq * (0.5 / d), ls_ref.shape)
            g = err * (1.0 / d)
            wg = g * wv
            dx = rs * wg - xv * (rs * rs * rs) * jnp.mean(xv * wg, axis=1, keepdims=True)
            dh_ref[...] = dx
            dhb_ref[...] = dx.astype(BF16)
            dw_ref[...] += jnp.sum(g * xv * rs, axis=0, keepdims=True)

    row = pl.BlockSpec((CHUNK, d), lambda i: (i, 0))
    vec = pl.BlockSpec((1, d), lambda i: (0, 0))
    return pl.pallas_call(
        body, name=name, grid=(nb,),
        in_specs=[row, vec, pl.BlockSpec((CHUNK, d), lambda i: (jnp.maximum(i - 1, 0), 0))],
        out_specs=[row, row, vec, pl.BlockSpec((1, 128), lambda i: (0, 0))],
        out_shape=[jax.ShapeDtypeStruct((r, d), F32), jax.ShapeDtypeStruct((r, d), BF16),
                   jax.ShapeDtypeStruct((1, d), F32), jax.ShapeDtypeStruct((1, 128), F32)],
        compiler_params=_params(),
    )(h, w, target)


def _swiglu_fwd(name, gu):
    e, _, r, f = gu.shape
    tr = _tile(r, 528, 8)

    def body(gu_ref, a_ref):
        g = gu_ref[0]
        a_ref[...] = (g * _sigmoid(g) * gu_ref[1]).astype(BF16)

    return pl.pallas_call(
        body, name=name, grid=(e, r // tr),
        in_specs=[pl.BlockSpec((None, 2, tr, f), lambda s, i: (s, 0, i, 0))],
        out_specs=pl.BlockSpec((None, tr, f), lambda s, i: (s, i, 0)),
        out_shape=jax.ShapeDtypeStruct((e, r, f), BF16), compiler_params=_params(),
    )(gu)


def _swiglu_bwd(name, da, gu):
    e, _, r, f = gu.shape
    tr = _tile(r, 528, 8)

    def body(da_ref, gu_ref, o_ref):
        g = gu_ref[0]
        s = _sigmoid(g)
        d = da_ref[...]
        o_ref[0] = (d * gu_ref[1] * (s + g * s * (1.0 - s))).astype(BF16)
        o_ref[1] = (d * g * s).astype(BF16)

    pair = pl.BlockSpec((None, 2, tr, f), lambda s, i: (s, 0, i, 0))
    return pl.pallas_call(
        body, name=name, grid=(e, r // tr),
        in_specs=[pl.BlockSpec((None, tr, f), lambda s, i: (s, i, 0)), pair], out_specs=pair,
        out_shape=jax.ShapeDtypeStruct((e, 2, r, f), BF16), compiler_params=_params(),
    )(da, gu)


def _shift_down(a, k):
    row = lax.broadcasted_iota(jnp.int32, a.shape, 0)
    return jnp.where(row >= k, pltpu.roll(a, k, 0), 0.0)


def _shift_up(a, k):
    n = a.shape[0]
    row = lax.broadcasted_iota(jnp.int32, a.shape, 0)
    return jnp.where(row < n - k, pltpu.roll(a, n - k, 0), 0.0)


def _conv_fwd(name, pc, cw):
    _, r, w = pc.shape

    def body(pc_ref, cw_ref, o_ref):
        a = pc_ref[2] * pc_ref[0]
        cwv = cw_ref[...]
        conv = _shift_down(a, 2) * cwv[0:1] + _shift_down(a, 1) * cwv[1:2] + a * cwv[2:3]
        o_ref[...] = (pc_ref[1] * conv).astype(BF16)

    return pl.pallas_call(
        body, name=name, grid=(w // 128,),
        in_specs=[pl.BlockSpec((3, r, 128), lambda j: (0, 0, j)), pl.BlockSpec((8, 128), lambda j: (0, j))],
        out_specs=pl.BlockSpec((r, 128), lambda j: (0, j)),
        out_shape=jax.ShapeDtypeStruct((r, w), BF16), compiler_params=_params(),
    )(pc, cw)


def _conv_bwd(name, dcat, pc, cw):
    _, r, w = pc.shape
    nblk = w // 128

    def body(dy_ref, pc_ref, cw_ref, dpc_ref, dcw_ref):
        u, gb, gc = pc_ref[0], pc_ref[1], pc_ref[2]
        cwv = cw_ref[...]
        dy = dy_ref[...]
        a = gc * u
        a1, a2 = _shift_down(a, 1), _shift_down(a, 2)
        conv = a2 * cwv[0:1] + a1 * cwv[1:2] + a * cwv[2:3]
        dconv = dy * gb
        da = dconv * cwv[2:3] + _shift_up(dconv, 1) * cwv[1:2] + _shift_up(dconv, 2) * cwv[0:1]
        dpc_ref[0] = (da * gc).astype(BF16)
        dpc_ref[1] = (dy * conv).astype(BF16)
        dpc_ref[2] = (da * u).astype(BF16)
        row = lax.broadcasted_iota(jnp.int32, (8, 128), 0)
        dw0 = jnp.sum(dconv * a2, axis=0, keepdims=True)
        dw1 = jnp.sum(dconv * a1, axis=0, keepdims=True)
        dw2 = jnp.sum(dconv * a, axis=0, keepdims=True)
        dcw_ref[...] = jnp.where(row == 0, dw0, jnp.where(row == 1, dw1, jnp.where(row == 2, dw2, 0.0)))

    return pl.pallas_call(
        body, name=name, grid=(nblk,),
        in_specs=[pl.BlockSpec((r, 128), lambda j: (0, nblk + j)),
                  pl.BlockSpec((3, r, 128), lambda j: (0, 0, j)), pl.BlockSpec((8, 128), lambda j: (0, j))],
        out_specs=[pl.BlockSpec((3, r, 128), lambda j: (0, 0, j)), pl.BlockSpec((8, 128), lambda j: (0, j))],
        out_shape=[jax.ShapeDtypeStruct((3, r, w), BF16), jax.ShapeDtypeStruct((8, w), F32)],
        compiler_params=_params(),
    )(dcat, pc, cw)


def _dot(a, b, dims):
    return lax.dot_general(a, b, (dims, ((), ())), preferred_element_type=F32)


def _col_to_row(xc, eye):
    return jnp.sum(jnp.where(eye, xc, 0.0), axis=0, keepdims=True)


def _row_to_col(xr, eye):
    return jnp.sum(jnp.where(eye, xr, 0.0), axis=1, keepdims=True)


def _gate_tiles(graw, bias, row0):
    th = jnp.tanh((graw + bias) / GATE_CAP)
    z = GATE_CAP * th
    row = lax.broadcasted_iota(jnp.int32, graw.shape, 0) + row0
    real = row >= PAD_FRONT
    li = jnp.where(real, z, -jnp.inf)
    lf = jnp.where(real, jnp.minimum(z, 0.0) - jnp.log(1.0 + jnp.exp(-jnp.abs(z))), 0.0)
    return th, z, li, lf, real


def _chunk_common(pm, h, li, lf, cst, nst, mst, tril, eye):
    kraw = pm[:, QK_W + h * DQK:QK_W + (h + 1) * DQK]
    q = (pm[:, h * DQK:(h + 1) * DQK] * QSCALE).astype(BF16)
    k = kraw.astype(BF16)
    v = pm[:, 2 * QK_W + h * DV:2 * QK_W + (h + 1) * DV].astype(BF16)
    li_c = li[:, h:h + 1]
    lf_c = lf[:, HEADS + h:HEADS + h + 1]
    li_r = _col_to_row(li_c, eye)
    lf_r = _col_to_row(lf_c, eye)
    b_c = jnp.sum(jnp.where(tril, lf_r, 0.0), axis=1, keepdims=True)
    b_r = _col_to_row(b_c, eye)
    dmat = jnp.where(tril, b_c - b_r + li_r, -jnp.inf)
    inter = b_c + mst
    mt = jnp.maximum(inter, jnp.max(dmat, axis=1, keepdims=True))
    w_inter = jnp.exp(inter - mt)
    p = jnp.exp(dmat - mt)
    s = _dot(q, k, NT) * p
    cb = cst.astype(BF16)
    nb = nst.astype(BF16).astype(F32)
    qc = _dot(q, cb, NN)
    qn = jnp.sum(q.astype(F32) * nb, axis=1, keepdims=True)
    den = w_inter * qn + jnp.sum(s, axis=1, keepdims=True)
    dn = jnp.maximum(jnp.abs(den), jnp.exp(-mt))
    b_end = b_c[CHUNK - 1:CHUNK, :]
    decay = b_end - b_c + li_c
    m_new = jnp.maximum(b_end + mst, jnp.max(decay, axis=0, keepdims=True))
    w_old = jnp.exp(b_end + mst - m_new)
    w_in = jnp.exp(decay - m_new)
    kw = (w_in * kraw).astype(BF16)
    return dict(q=q, k=k, v=v, kraw=kraw, mt=mt, w_inter=w_inter, p=p, s=s, cb=cb, nb=nb, qc=qc, qn=qn,
                den=den, dn=dn, m_new=m_new, w_old=w_old, w_in=w_in, kw=kw)


def _mlstm_fwd(name, pm, bias, nw):
    r = pm.shape[0]
    nc = r // CHUNK

    def body(pm_ref, b_ref, nw_ref, hm_ref, ht_ref, cs_ref, ns_ref, ms_ref, c_scr, n_scr, m_scr):
        ci = pl.program_id(0)

        @pl.when(ci == 0)
        def _():
            c_scr[...] = jnp.zeros_like(c_scr)
            n_scr[...] = jnp.zeros_like(n_scr)
            m_scr[...] = jnp.zeros_like(m_scr)

        pmv = pm_ref[...]
        rr = lax.broadcasted_iota(jnp.int32, (CHUNK, CHUNK), 0)
        cc = lax.broadcasted_iota(jnp.int32, (CHUNK, CHUNK), 1)
        tril, eye = cc <= rr, cc == rr
        _, _, li, lf, _ = _gate_tiles(pmv[:, GATE_COL:GATE_COL + 128], b_ref[...], ci * CHUNK)
        nwv = nw_ref[...]
        for h in range(HEADS):
            cst, nst, mst = c_scr[h], n_scr[h], m_scr[h]
            cs_ref[h] = cst
            ns_ref[h] = nst
            ms_ref[h] = mst
            f = _chunk_common(pmv, h, li, lf, cst, nst, mst, tril, eye)
            num = f["w_inter"] * f["qc"] + _dot(f["s"].astype(BF16), f["v"], NN)
            hh = num / f["dn"]
            c_scr[h] = f["w_old"] * cst + _dot(f["kw"], f["v"], TN)
            n_scr[h] = f["w_old"] * nst + jnp.sum(
                f["w_in"].astype(BF16).astype(F32) * f["k"].astype(F32), axis=0, keepdims=True)
            m_scr[h] = f["m_new"]
            sl = slice(h * DV, (h + 1) * DV)
            rs = lax.rsqrt(jnp.mean(hh * hh, axis=1, keepdims=True) + EPS)
            og = pmv[:, 2 * QK_W + MLSTM_W + h * DV:2 * QK_W + MLSTM_W + (h + 1) * DV]
            ht_ref[:, sl] = hh
            hm_ref[:, sl] = (_sigmoid(og) * (hh * rs * nwv[:, sl])).astype(BF16)

    return pl.pallas_call(
        body, name=name, grid=(nc,),
        in_specs=[pl.BlockSpec((CHUNK, PM_W), lambda i: (i, 0)), pl.BlockSpec((1, 128), lambda i: (0, 0)),
                  pl.BlockSpec((1, MLSTM_W), lambda i: (0, 0))],
        out_specs=[pl.BlockSpec((CHUNK, MLSTM_W), lambda i: (i, 0)),
                   pl.BlockSpec((CHUNK, MLSTM_W), lambda i: (i, 0)),
                   pl.BlockSpec((None, HEADS, DQK, DV), lambda i: (i, 0, 0, 0)),
                   pl.BlockSpec((None, HEADS, 1, DQK), lambda i: (i, 0, 0, 0)),
                   pl.BlockSpec((None, HEADS, 1, 1), lambda i: (i, 0, 0, 0))],
        out_shape=[jax.ShapeDtypeStruct((r, MLSTM_W), BF16), jax.ShapeDtypeStruct((r, MLSTM_W), F32),
                   jax.ShapeDtypeStruct((nc, HEADS, DQK, DV), F32),
                   jax.ShapeDtypeStruct((nc, HEADS, 1, DQK), F32),
                   jax.ShapeDtypeStruct((nc, HEADS, 1, 1), F32)],
        scratch_shapes=[pltpu.VMEM((HEADS, DQK, DV), F32), pltpu.VMEM((HEADS, 1, DQK), F32),
                        pltpu.VMEM((HEADS, 1, 1), F32)],
        compiler_params=_params(),
    )(pm, bias, nw)


def _mlstm_bwd(name, dcat, pm, ht, cs, ns, ms, bias, nw):
    r = pm.shape[0]
    nc = r // CHUNK

    def body(dy_ref, pm_ref, ht_ref, cs_ref, ns_ref, ms_ref, b_ref, nw_ref, dpm_ref, dnw_ref, db_ref,
             dc_scr, dn_scr):
        step = pl.program_id(0)
        ci = nc - 1 - step

        @pl.when(step == 0)
        def _():
            dc_scr[...] = jnp.zeros_like(dc_scr)
            dn_scr[...] = jnp.zeros_like(dn_scr)
            dnw_ref[...] = jnp.zeros_like(dnw_ref)
            db_ref[...] = jnp.zeros_like(db_ref)

        pmv = pm_ref[...]
        rr = lax.broadcasted_iota(jnp.int32, (CHUNK, CHUNK), 0)
        cc = lax.broadcasted_iota(jnp.int32, (CHUNK, CHUNK), 1)
        tril, eye, triu = cc <= rr, cc == rr, cc >= rr
        th, z, li, lf, real = _gate_tiles(pmv[:, GATE_COL:GATE_COL + 128], b_ref[...], ci * CHUNK)
        lane = lax.broadcasted_iota(jnp.int32, (CHUNK, 128), 1)
        rowid = lax.broadcasted_iota(jnp.int32, (CHUNK, 1), 0)
        nwv = nw_ref[...]
        dgt = jnp.zeros((CHUNK, 128), F32)
        for h in range(HEADS):
            cst, nst, mst = cs_ref[h], ns_ref[h], ms_ref[h]
            f = _chunk_common(pmv, h, li, lf, cst, nst, mst, tril, eye)
            q, k, v, s, p = f["q"], f["k"], f["v"], f["s"], f["p"]
            w_inter, w_in, w_old, dn = f["w_inter"], f["w_in"], f["w_old"], f["dn"]
            sl = slice(h * DV, (h + 1) * DV)
            osl = slice(2 * QK_W + MLSTM_W + h * DV, 2 * QK_W + MLSTM_W + (h + 1) * DV)
            hh = ht_ref[:, sl]
            y = dy_ref[:, sl]
            sg = _sigmoid(pmv[:, osl])
            rs = lax.rsqrt(jnp.mean(hh * hh, axis=1, keepdims=True) + EPS)
            nwh = nwv[:, sl]
            dpm_ref[:, osl] = (y * (hh * rs * nwh) * sg * (1.0 - sg)).astype(BF16)
            dhn = y * sg
            dnw_ref[:, sl] += jnp.sum(dhn * hh * rs, axis=0, keepdims=True)
            wd = dhn * nwh
            dhh = rs * wd - hh * (rs * rs * rs) * jnp.mean(hh * wd, axis=1, keepdims=True)
            dnum = dhh / dn
            dd = -jnp.sum(dhh * hh, axis=1, keepdims=True) / dn
            dden = jnp.where(jnp.abs(f["den"]) > jnp.exp(-f["mt"]), dd * jnp.sign(f["den"]), 0.0)
            dnum_b = dnum.astype(BF16)
            wdn = (w_inter * dnum).astype(BF16)
            wid = (w_inter * dden).astype(BF16).astype(F32)
            ds = _dot(dnum_b, v, NT) + dden
            dsp = (ds * p).astype(BF16)
            dq = _dot(dsp, k, NN) + _dot(wdn, f["cb"], NT) + wid * f["nb"]
            dk = _dot(dsp, q, TN)
            dv = _dot(s.astype(BF16), dnum_b, TN)
            g = ds * s
            g_col = _row_to_col(jnp.sum(g, axis=0, keepdims=True), eye)
            db = jnp.sum(g, axis=1, keepdims=True) - g_col
            dli = g_col
            db = db + (jnp.sum(dnum * f["qc"], axis=1, keepdims=True) + dden * f["qn"]) * w_inter
            dcn, dnn = dc_scr[h], dn_scr[h]
            dcnb = dcn.astype(BF16)
            dnnb = dnn.astype(BF16).astype(F32)
            dkw = _dot(v, dcnb, NT) + dnnb
            dk = dk + w_in * dkw
            dv = dv + _dot(f["kw"], dcnb, NN)
            ddecay = jnp.sum(dkw * f["kraw"], axis=1, keepdims=True) * w_in
            dw_old = (jnp.sum(jnp.sum(dcn * cst, axis=1, keepdims=True), axis=0, keepdims=True)
                      + jnp.sum(dnn * nst, axis=1, keepdims=True))
            db_end = dw_old * w_old + jnp.sum(ddecay, axis=0, keepdims=True)
            db = db - ddecay + jnp.where(rowid == CHUNK - 1, db_end, 0.0)
            dli = dli + ddecay
            dc_scr[h] = w_old * dcn + _dot(q, wdn, TN)
            dn_scr[h] = w_old * dnn + jnp.sum(wid * q.astype(F32), axis=0, keepdims=True)
            dlf = jnp.sum(jnp.where(triu, _col_to_row(db, eye), 0.0), axis=1, keepdims=True)
            dgt = dgt + jnp.where(lane == h, dli, 0.0) + jnp.where(lane == HEADS + h, dlf, 0.0)
            dpm_ref[:, h * DQK:(h + 1) * DQK] = (dq * QSCALE).astype(BF16)
            dpm_ref[:, QK_W + h * DQK:QK_W + (h + 1) * DQK] = dk.astype(BF16)
            dpm_ref[:, 2 * QK_W + h * DV:2 * QK_W + (h + 1) * DV] = dv.astype(BF16)
        dact = jnp.where(lane < HEADS, 1.0, 1.0 - _sigmoid(z)) * (1.0 - th * th)
        dgraw = jnp.where(real & (lane < 2 * HEADS), dgt * dact, 0.0)
        dpm_ref[:, GATE_COL:GATE_COL + 128] = dgraw.astype(BF16)
        db_ref[...] += jnp.sum(dgraw, axis=0, keepdims=True)

    rev = lambda i: (nc - 1 - i, 0)
    rev4 = lambda i: (nc - 1 - i, 0, 0, 0)
    return pl.pallas_call(
        body, name=name, grid=(nc,),
        in_specs=[pl.BlockSpec((CHUNK, MLSTM_W), rev), pl.BlockSpec((CHUNK, PM_W), rev),
                  pl.BlockSpec((CHUNK, MLSTM_W), rev),
                  pl.BlockSpec((None, HEADS, DQK, DV), rev4), pl.BlockSpec((None, HEADS, 1, DQK), rev4),
                  pl.BlockSpec((None, HEADS, 1, 1), rev4),
                  pl.BlockSpec((1, 128), lambda i: (0, 0)), pl.BlockSpec((1, MLSTM_W), lambda i: (0, 0))],
        out_specs=[pl.BlockSpec((CHUNK, PM_W), rev), pl.BlockSpec((1, MLSTM_W), lambda i: (0, 0)),
                   pl.BlockSpec((1, 128), lambda i: (0, 0))],
        out_shape=[jax.ShapeDtypeStruct((r, PM_W), BF16), jax.ShapeDtypeStruct((1, MLSTM_W), F32),
                   jax.ShapeDtypeStruct((1, 128), F32)],
        scratch_shapes=[pltpu.VMEM((HEADS, DQK, DV), F32), pltpu.VMEM((HEADS, 1, DQK), F32)],
        compiler_params=_params(),
    )(dcat, pm, ht, cs, ns, ms, bias, nw)


def _my_place():
    return lax.axis_index("x"), lax.axis_index("y"), lax.axis_index("c")


def _flip(v, bit):
    return 1 - v if bit else v


def _exchange_small(name, blk, reduce):
    r, c = blk.shape

    def body(x_ref, o_ref, *rest):
        slots = rest[0] if reduce else o_ref
        send_sems, recv_sems = rest[-2], rest[-1]
        x, y, cc = _my_place()
        me = 4 * x + 2 * y + cc
        slots[me] = x_ref[...]
        copies = []
        for k in range(1, N_DEV):
            peer = (_flip(x, k & 4), _flip(y, k & 2), _flip(cc, k & 1))
            cp = pltpu.make_async_remote_copy(
                src_ref=x_ref, dst_ref=slots.at[me], send_sem=send_sems.at[k - 1],
                recv_sem=recv_sems.at[k - 1], device_id=peer, device_id_type=MESH)
            cp.start()
            copies.append(cp)
        for cp in copies:
            cp.wait()
        if reduce:
            acc = slots[0]
            for d in range(1, N_DEV):
                acc = acc + slots[d]
            o_ref[...] = acc

    scratch = ([pltpu.VMEM((N_DEV, r, c), F32)] if reduce else []) + [
        pltpu.SemaphoreType.DMA((N_DEV - 1,)), pltpu.SemaphoreType.DMA((N_DEV - 1,))]
    return pl.pallas_call(
        body, name=name,
        out_shape=jax.ShapeDtypeStruct((r, c) if reduce else (N_DEV, r, c), F32),
        in_specs=[pl.BlockSpec(memory_space=pltpu.VMEM)], out_specs=pl.BlockSpec(memory_space=pltpu.VMEM),
        scratch_shapes=scratch, compiler_params=_params(),
    )(blk)


HBM_SPEC = pl.BlockSpec(memory_space=pltpu.HBM)
SEM_SPEC = pl.BlockSpec(memory_space=pltpu.SEMAPHORE)
ANY_SPEC = pl.BlockSpec(memory_space=pl.ANY)
DATAFLOW = pltpu.SideEffectType.DATAFLOW_SIDE_EFFECTING


def _split_copy(name, arrays, start=None, wait=None, after=None):
    n = len(arrays)
    n_wait = 2 if wait else 0
    n_after = 0 if after is None else 1
    n_new = 2 if start else 0

    def body(*refs):
        ins = refs[:n]
        if wait:
            for cp in wait[0](ins, refs[n], refs[n + 1]):
                cp.wait_send()
                cp.wait_recv()
        if start:
            at = n + n_wait + n_after
            for cp in start[0](ins, refs[at], refs[at + 1]):
                cp.start()
            token = refs[at + 2 + n]
            token[...] = jnp.zeros_like(token)

    operands = [pltpu.with_memory_space_constraint(a, pltpu.HBM) for a in arrays]
    in_specs = [HBM_SPEC] * n
    if wait:
        operands += list(wait[1])
        in_specs += [SEM_SPEC, SEM_SPEC]
    if after is not None:
        operands.append(after)
        in_specs.append(ANY_SPEC)
    out_shape, out_specs = [], []
    if start:
        out_shape += [pltpu.SemaphoreType.DMA((start[1],)), pltpu.SemaphoreType.DMA((start[1],))]
        out_specs += [SEM_SPEC, SEM_SPEC]
    out_shape += [pltpu.HBM(a.shape, a.dtype) for a in arrays]
    out_specs += [HBM_SPEC] * n
    if start:
        out_shape.append(jax.ShapeDtypeStruct((8, 128), F32))
        out_specs.append(pl.BlockSpec(memory_space=pltpu.VMEM))
    outs = pl.pallas_call(
        body, name=name, in_specs=in_specs, out_specs=out_specs, out_shape=out_shape,
        input_output_aliases={i: n_new + i for i in range(n)},
        compiler_params=pltpu.CompilerParams(has_side_effects=DATAFLOW),
    )(*operands)
    thru = list(outs[n_new:n_new + n])
    return thru, (tuple(outs[:2]) if start else None), (outs[n_new + n] if start else None)


def _remote(src, dst, send_sems, recv_sems, k, to):
    return pltpu.make_async_remote_copy(src_ref=src, dst_ref=dst, send_sem=send_sems.at[k],
                                        recv_sem=recv_sems.at[k], device_id=to, device_id_type=MESH)


def _gather_first(n):
    def copies(refs, send_sems, recv_sems):
        x, y, c = _my_place()
        targets = [(x, y, 1 - c), (1 - x, y, c), (x, 1 - y, c), (1 - x, 1 - y, c)]
        out = []
        for a in range(n):
            blk = refs[a].at[4 * x + 2 * y + c]
            out += [_remote(blk, blk, send_sems, recv_sems, 4 * a + k, to) for k, to in enumerate(targets)]
        return out
    return copies


def _gather_pass(n):
    def copies(refs, send_sems, recv_sems):
        x, y, c = _my_place()
        out = []
        for a in range(n):
            for j, (px, py) in enumerate([(1 - x, y), (x, 1 - y), (1 - x, 1 - y)]):
                blk = refs[a].at[4 * px + 2 * py + c]
                out.append(_remote(blk, blk, send_sems, recv_sems, 3 * a + j, (x, y, 1 - c)))
        return out
    return copies


def _scatter_sibling(n):
    def copies(refs, send_sems, recv_sems):
        x, y, c = _my_place()
        return [_remote(refs[a].at[2 * j + 1 - c], refs[n + a].at[j], send_sems, recv_sems, 4 * a + j, (x, y, 1 - c))
                for a in range(n) for j in range(4)]
    return copies


def _scatter_chips(n):
    def copies(refs, send_sems, recv_sems):
        x, y, c = _my_place()
        out = []
        for a in range(n):
            for k in range(1, 4):
                px, py = _flip(x, k & 2), _flip(y, k & 1)
                out.append(_remote(refs[a].at[2 * px + py], refs[n + a].at[2 * x + y], send_sems, recv_sems,
                                   3 * a + k - 1, (px, py, c)))
        return out
    return copies


def _pair_sum(name, core, g, t):
    _, r, c = g.shape
    tr = _tile(r, 512, 8)
    g4 = g.reshape(4, 2, r, c)

    def body(core_ref, g_ref, t_ref, o_ref):
        o_ref[...] = (g_ref[...].astype(F32) + t_ref[...].astype(F32)).astype(BF16)

    return pl.pallas_call(
        body, name=name,
        grid_spec=pltpu.PrefetchScalarGridSpec(
            num_scalar_prefetch=1, grid=(4, r // tr),
            in_specs=[pl.BlockSpec((None, None, tr, c), lambda j, i, core_ref: (j, core_ref[0], i, 0)),
                      pl.BlockSpec((None, tr, c), lambda j, i, core_ref: (j, i, 0))],
            out_specs=pl.BlockSpec((None, tr, c), lambda j, i, core_ref: (j, i, 0))),
        out_shape=jax.ShapeDtypeStruct((4, r, c), BF16), compiler_params=_params(),
    )(core, g4, t)


def _adam_math(w, g, m, v):
    m2 = ADAM_B1 * m + (1.0 - ADAM_B1) * g
    v2 = ADAM_B2 * v + (1.0 - ADAM_B2) * (g * g)
    m_hat = m2 / (1.0 - ADAM_B1 ** ADAM_STEP)
    v_hat = v2 / (1.0 - ADAM_B2 ** ADAM_STEP)
    delta = -ADAM_LR * (m_hat / (jnp.sqrt(v_hat) + ADAM_EPS) + ADAM_WD * w)
    return delta, m2, v2


def _adam_sharded(name, chip, w, m, v, grads, row_off=0, dep=None):
    _, r, c = w.shape
    tr = _tile(r, 256, 8)
    tc = c if tr < r else _tile(c, 256, 128)
    boff = row_off // tr

    def body(chip_ref, w_ref, m_ref, v_ref, p0_ref, q0_ref, p1_ref, q1_ref, *rest):
        g_ref, d_ref, nm_ref, nv_ref = rest[-4:]
        mine = chip_ref[0]

        def total(p_ref, q_ref):
            acc = None
            for j in range(4):
                part = jnp.where(mine == j, p_ref[j], q_ref[j]).astype(F32)
                acc = part if acc is None else acc + part
            return acc

        g = jnp.where(pl.program_id(0) == 0, total(p0_ref, q0_ref), total(p1_ref, q1_ref))
        delta, m2, v2 = _adam_math(w_ref[...], g, m_ref[...], v_ref[...])
        g_ref[...] = g
        d_ref[...] = delta
        nm_ref[...] = m2
        nv_ref[...] = v2

    wspec = pl.BlockSpec((None, tr, tc), lambda l, i, j, chip_ref: (l, i, j))
    qspec = pl.BlockSpec((4, tr, tc), lambda l, i, j, chip_ref: (0, boff + i, j))
    sds = jax.ShapeDtypeStruct(w.shape, F32)
    ins = [chip, w, m, v, grads[0][0], grads[0][1], grads[1][0], grads[1][1]]
    in_specs = [wspec, wspec, wspec, qspec, qspec, qspec, qspec]
    if dep is not None:
        ins.append(dep)
        in_specs.append(pl.BlockSpec((8, 128), lambda *_: (0, 0)))
    return pl.pallas_call(
        body, name=name,
        grid_spec=pltpu.PrefetchScalarGridSpec(
            num_scalar_prefetch=1, grid=(2, r // tr, c // tc), in_specs=in_specs, out_specs=[wspec] * 4),
        out_shape=[sds] * 4, compiler_params=_params(),
    )(*ins)


def _adam_small(name, w, m, v, g):
    def body(w_ref, m_ref, v_ref, g_ref, d_ref, nm_ref, nv_ref):
        delta, m2, v2 = _adam_math(w_ref[...], g_ref[...], m_ref[...], v_ref[...])
        d_ref[...] = delta
        nm_ref[...] = m2
        nv_ref[...] = v2

    sds = jax.ShapeDtypeStruct(w.shape, F32)
    vm = pl.BlockSpec(memory_space=pltpu.VMEM)
    return pl.pallas_call(body, name=name, in_specs=[vm] * 4, out_specs=[vm] * 3, out_shape=[sds] * 3,
                          compiler_params=_params())(w, m, v, g)


GATE_END = GATE_COL + 2 * HEADS


def _merge_dw_in(dwm_t, dwc_t):
    full = jnp.concatenate([dwm_t[:GATE_END], dwc_t.reshape(3 * CONV_W, D_MODEL)], axis=0)
    return full.reshape(N_DEV, IN_SH, D_MODEL)


def _pack128(parts):
    flat = jnp.concatenate([p.reshape(-1) for p in parts])
    n = flat.shape[0]
    rows = -(-n // 1024) * 8
    return jnp.pad(flat, (0, rows * 128 - n)).reshape(rows, 128)


def _unpack128(packed, shapes):
    flat = packed.reshape(-1)
    out, at = [], 0
    for s in shapes:
        n = int(np.prod(s))
        out.append(flat[at:at + n].reshape(s))
        at += n
    return out


def kernel(x, meta_tokens, norm_mix_w, w_in, b_gates, conv_w, mlstm_norm_w, w_out, norm_ffn_w, w_gate, w_up, w_down, norm_final_w, loss_target, m_meta_tokens, m_norm_mix_w, m_w_in, m_b_gates, m_conv_w, m_mlstm_norm_w, m_w_out, m_norm_ffn_w, m_w_gate, m_w_up, m_w_down, m_norm_final_w, v_meta_tokens, v_norm_mix_w, v_w_in, v_b_gates, v_conv_w, v_mlstm_norm_w, v_w_out, v_norm_ffn_w, v_w_gate, v_w_up, v_w_down, v_norm_final_w):
    seq = x.shape[1]
    rows = TOK0 + seq
    me = 4 * lax.axis_index("x") + 2 * lax.axis_index("y") + lax.axis_index("c")
    meta_sh = meta_tokens.shape[1]
    conv_sh = conv_w.shape[2]

    small = jnp.concatenate(
        [meta_tokens, jnp.pad(conv_w.reshape(DEPTH * 3, conv_sh), ((0, 2), (0, meta_sh - conv_sh)))], axis=0)
    slots = _exchange_small("gather_small", small, reduce=False)
    meta_full = jnp.transpose(slots[:, :N_META, :], (1, 0, 2)).reshape(N_META, D_MODEL)
    conv_full = jnp.transpose(slots[:, N_META:N_META + DEPTH * 3, :conv_sh], (1, 0, 2)).reshape(DEPTH, 3, CONV_W)
    conv_rows = [jnp.pad(conv_full[l], ((0, 5), (0, 0))) for l in range(DEPTH)]

    w_in_t, m_w_in_t, v_w_in_t = (jnp.transpose(a, (0, 2, 1)) for a in (w_in, m_w_in, v_w_in))
    w_gate_t, m_w_gate_t, v_w_gate_t = (jnp.transpose(a, (0, 2, 1)) for a in (w_gate, m_w_gate, v_w_gate))
    w_up_t, m_w_up_t, v_w_up_t = (jnp.transpose(a, (0, 2, 1)) for a in (w_up, m_w_up, v_w_up))
    shards = []
    for l in range(DEPTH):
        shards += [w_in_t[l].astype(BF16), w_out[l].astype(BF16),
                   jnp.concatenate([w_gate_t[l], w_up_t[l]], axis=0).astype(BF16), w_down[l].astype(BF16)]
    gather_names = [f"{nm}_{l}" for l in range(DEPTH) for nm in ("w_in", "w_out", "w_ffn_in", "w_down")]
    gather_state = {}

    def gather_start(i, after=None):
        buf = lax.dynamic_update_index_in_dim(lax.empty((N_DEV,) + shards[i].shape, BF16), shards[i], me, 0)
        arrs, sems, tok = _split_copy(f"gather_start_{gather_names[i]}", [buf], start=(_gather_first(1), 4),
                                      after=after)
        gather_state[i] = (arrs, sems)
        return tok

    def gathered(i, after):
        arrs, sems = gather_state[i]
        arrs, sems, tok = _split_copy(f"gather_pass_{gather_names[i]}", arrs, start=(_gather_pass(1), 3),
                                      wait=(_gather_first(1), sems), after=after)
        arrs, _, _ = _split_copy(f"gather_done_{gather_names[i]}", arrs, wait=(_gather_pass(1), sems), after=tok)
        return arrs[0]

    bias = [jnp.pad(b_gates[l].reshape(1, 2 * HEADS), ((0, 0), (0, 128 - 2 * HEADS))) for l in range(DEPTH)]
    nmix = [norm_mix_w[l].reshape(1, D_MODEL) for l in range(DEPTH)]
    nffn = [norm_ffn_w[l].reshape(1, D_MODEL) for l in range(DEPTH)]
    nmls = [mlstm_norm_w[l].reshape(1, MLSTM_W) for l in range(DEPTH)]
    weights = [dict() for _ in range(DEPTH)]
    saved = [dict() for _ in range(DEPTH)]

    def layer_fwd(l, h, after):
        w, s = weights[l], saved[l]
        w["win_t"] = gathered(4 * l, after).reshape(D_IN, D_MODEL)
        w["wc_t"] = w["win_t"][GATE_END:].reshape(3, CONV_W, D_MODEL)
        s["h0"] = h
        s["hn"] = _rms_fwd(f"norm_mix_{l}", h, nmix[l])
        s["pm"] = _mm_nt(f"proj_mlstm_{l}", s["hn"], w["win_t"], F32, tn=640, tk=D_MODEL, n=PM_W)
        s["pc"] = _mm_nt_bcols(f"proj_conv_{l}", s["hn"], w["wc_t"], F32)
        hm, s["ht"], s["cs"], s["ns"], s["ms"] = _mlstm_fwd(f"mlstm_fwd_{l}", s["pm"], bias[l], nmls[l])
        hc = _conv_fwd(f"conv_fwd_{l}", s["pc"], conv_rows[l])
        s["cat"] = jnp.concatenate([hm, hc], axis=1)
        w["wo"] = gathered(4 * l + 1, s["cat"]).reshape(D_MODEL, D_MODEL)
        s["h1"] = _mm_nn(f"out_proj_{l}", s["cat"], w["wo"], F32, res=s["h0"])
        s["hf"] = _rms_fwd(f"norm_ffn_{l}", s["h1"], nffn[l])
        w["wgu_t"] = gathered(4 * l + 2, s["hf"]).reshape(2 * N_DEV, FF_SH, D_MODEL)
        s["gu"] = _mm_nt_bcols(f"ffn_in_{l}", s["hf"], w["wgu_t"], F32).reshape(N_DEV, 2, rows, FF_SH)
        s["act"] = _swiglu_fwd(f"swiglu_{l}", s["gu"])
        w["wd"] = gathered(4 * l + 3, s["act"])
        return _mm_nn_ksum(f"ffn_out_{l}", s["act"], w["wd"], F32, res=s["h1"])

    tok = None
    for i in range(len(shards)):
        tok = gather_start(i, after=tok)
    h = jnp.concatenate([jnp.zeros((PAD_FRONT, D_MODEL), F32), meta_full, x[0]], axis=0)
    h = layer_fwd(0, h, tok)
    h = layer_fwd(1, h, h)

    dh, dh_b, d_final, loss_part = _final_loss("final_loss", h, norm_final_w.reshape(1, D_MODEL), loss_target[0])

    core = lax.axis_index("c").astype(jnp.int32).reshape(1)
    chip = (2 * lax.axis_index("x") + lax.axis_index("y")).astype(jnp.int32).reshape(1)
    scatter_state = {}

    def scatter_begin(nm, grad):
        land = lax.empty((4,) + grad.shape[1:], BF16)
        arrs, sems, tok = _split_copy(f"grad_sibling_start_{nm}", [grad, land], start=(_scatter_sibling(1), 4))
        scatter_state[nm] = (arrs, sems)
        return tok

    def scatter_advance(nm, after):
        arrs, sems = scatter_state[nm]
        arrs, _, _ = _split_copy(f"grad_sibling_done_{nm}", arrs, wait=(_scatter_sibling(1), sems), after=after)
        part = _pair_sum(f"grad_pair_sum_{nm}", core, arrs[0], arrs[1])
        arrs, sems, tok = _split_copy(f"grad_chips_start_{nm}", [part, lax.empty(part.shape, BF16)],
                                      start=(_scatter_chips(1), 3))
        scatter_state[nm] = (arrs, sems)
        return tok

    def scattered(nm, after):
        arrs, sems = scatter_state[nm]
        arrs, _, _ = _split_copy(f"grad_chips_done_{nm}", arrs, wait=(_scatter_chips(1), sems), after=after)
        return arrs[0], arrs[1]

    d_mix, d_ffn, d_mls, d_bias, d_conv = ([None] * DEPTH for _ in range(5))

    def layer_bwd(l, dh, dh_b, tok):
        w, s = weights[l], saved[l]
        da = _mm_nt_bcols(f"d_act_{l}", dh_b, w["wd"], F32, dep=tok)
        dw_down = _mm_tn_acols(f"dw_down_{l}", s["act"], dh_b, BF16)
        tok = scatter_begin(f"w_down_{l}", dw_down)
        dgu = _swiglu_bwd(f"swiglu_bwd_{l}", da, s["gu"]).reshape(2 * N_DEV, rows, FF_SH)
        dhf = _mm_nn_ksum(f"d_ffn_in_{l}", dgu, w["wgu_t"], F32, dep=tok)
        tok = scatter_advance(f"w_down_{l}", after=dhf)
        dw_gu = _mm_tn_acols(f"dw_ffn_in_{l}", dgu, s["hf"], BF16, dep=tok)
        tok = scatter_begin(f"w_ffn_in_{l}", dw_gu.reshape(N_DEV, 2 * FF_SH, D_MODEL))
        dh1, dh1_b, d_ffn[l] = _rms_bwd(f"norm_ffn_bwd_{l}", s["h1"], nffn[l], dhf, dh)
        dcat = _mm_nt(f"d_cat_{l}", dh1_b, w["wo"], F32, tk=D_MODEL, dep=tok)
        tok = scatter_advance(f"w_ffn_in_{l}", after=dcat)
        dw_out = _mm_tn(f"dw_out_{l}", s["cat"], dh1_b, BF16, tn=1024, dep=tok)
        tok = scatter_begin(f"w_out_{l}", dw_out.reshape(N_DEV, OUT_SH, D_MODEL))
        dpm, d_mls[l], d_bias[l] = _mlstm_bwd(f"mlstm_bwd_{l}", dcat, s["pm"], s["ht"], s["cs"], s["ns"],
                                               s["ms"], bias[l] + tok[:1], nmls[l])
        dpc, d_conv[l] = _conv_bwd(f"conv_bwd_{l}", dcat, s["pc"], conv_rows[l])
        tok = scatter_advance(f"w_out_{l}", after=dpc)
        dhn = _mm_nn_kt(f"d_norm_mlstm_{l}", dpm, w["win_t"], F32, dep=tok)
        dhn = _mm_nn_ksum(f"d_norm_conv_{l}", dpc, w["wc_t"], F32, res=dhn)
        dwm_t = _mm_tn(f"dw_mlstm_{l}", dpm, s["hn"], BF16, tm=640, tn=1024)
        dwc_t = _mm_tn_acols(f"dw_conv_{l}", dpc, s["hn"], BF16)
        tok = scatter_begin(f"w_in_{l}", _merge_dw_in(dwm_t, dwc_t))
        dh, dh_b, d_mix[l] = _rms_bwd(f"norm_mix_bwd_{l}", s["h0"], nmix[l] + tok[0, 0], dhn, dh1)
        return dh, dh_b, scatter_advance(f"w_in_{l}", after=dh)

    dh, dh_b, tok = layer_bwd(1, dh, dh_b, None)
    dh, dh_b, tok_tail = layer_bwd(0, dh, dh_b, tok)

    pq = {}
    after = tok_tail
    for l in reversed(range(DEPTH)):
        for nm in ("w_down", "w_ffn_in", "w_out", "w_in"):
            if (nm, l) != ("w_in", 0):
                pq[nm, l] = scattered(f"{nm}_{l}", after)
                after = pq[nm, l][0]
    untransposed = lambda outs: [jnp.transpose(o, (0, 2, 1)) for o in outs]
    g_out, d_out, nm_out, nv_out = _adam_sharded(
        "adam_w_out", chip, w_out, m_w_out, v_w_out, [pq["w_out", 0], pq["w_out", 1]])
    g_gate, d_gate, nm_gate, nv_gate = untransposed(_adam_sharded(
        "adam_w_gate", chip, w_gate_t, m_w_gate_t, v_w_gate_t, [pq["w_ffn_in", 0], pq["w_ffn_in", 1]]))
    g_up, d_up, nm_up, nv_up = untransposed(_adam_sharded(
        "adam_w_up", chip, w_up_t, m_w_up_t, v_w_up_t, [pq["w_ffn_in", 0], pq["w_ffn_in", 1]], row_off=FF_SH))
    g_down, d_down, nm_down, nv_down = _adam_sharded(
        "adam_w_down", chip, w_down, m_w_down, v_w_down, [pq["w_down", 0], pq["w_down", 1]])
    pq["w_in", 0] = scattered("w_in_0", nv_down)
    g_in, d_in, nm_in, nv_in = untransposed(_adam_sharded(
        "adam_w_in", chip, w_in_t, m_w_in_t, v_w_in_t, [pq["w_in", 0], pq["w_in", 1]]))

    bg = jnp.concatenate([d_bias[l][0, :2 * HEADS] for l in range(DEPTH)])
    red_in = jnp.concatenate([
        dh[PAD_FRONT:TOK0], d_mix[0], d_mix[1], d_ffn[0], d_ffn[1], d_final,
        jnp.concatenate([d_mls[0], d_mls[1]], axis=1),
        jnp.stack([d_conv[l][:3] for l in range(DEPTH)]).reshape(3, 2 * CONV_W),
        jnp.pad(bg, (0, D_MODEL - bg.shape[0])).reshape(1, D_MODEL),
        jnp.pad(loss_part[:, :1], ((0, 0), (0, D_MODEL - 1))),
        jnp.zeros((5, D_MODEL), F32) + tok_tail[0, 0]], axis=0)
    red = _exchange_small("reduce_small", red_in, reduce=True)
    loss = red[26, 0]
    g_meta = lax.dynamic_slice_in_dim(red[:N_META], me * meta_sh, meta_sh, axis=1)
    g_mix, g_ffn, g_final = red[16:18], red[18:20], red[20]
    g_mls = red[21].reshape(DEPTH, MLSTM_W)
    g_conv = lax.dynamic_slice_in_dim(red[22:25].reshape(DEPTH, 3, CONV_W), me * conv_sh, conv_sh, axis=2)
    g_bias = red[25, :DEPTH * 2 * HEADS].reshape(DEPTH, 2 * HEADS)

    small_w = [meta_tokens, norm_mix_w, b_gates, conv_w, mlstm_norm_w, norm_ffn_w, norm_final_w]
    small_m = [m_meta_tokens, m_norm_mix_w, m_b_gates, m_conv_w, m_mlstm_norm_w, m_norm_ffn_w, m_norm_final_w]
    small_v = [v_meta_tokens, v_norm_mix_w, v_b_gates, v_conv_w, v_mlstm_norm_w, v_norm_ffn_w, v_norm_final_w]
    small_g = [g_meta, g_mix, g_bias, g_conv, g_mls, g_ffn, g_final]
    shapes = [a.shape for a in small_w]
    packed = _adam_small("adam_small", _pack128(small_w), _pack128(small_m), _pack128(small_v), _pack128(small_g))
    (d_meta, d_nmix, d_bg, d_cw, d_nmls, d_nffn, d_nfin), (nm_meta, nm_nmix, nm_bg, nm_cw, nm_nmls, nm_nffn, nm_nfin), \
        (nv_meta, nv_nmix, nv_bg, nv_cw, nv_nmls, nv_nffn, nv_nfin) = (_unpack128(p, shapes) for p in packed)

    grad_x = dh[TOK0:].reshape(1, seq, D_MODEL)
    return (loss, grad_x,
            g_meta, g_mix, g_in, g_bias, g_conv, g_mls, g_out, g_ffn, g_gate, g_up, g_down, g_final,
            d_meta, d_nmix, d_in, d_bg, d_cw, d_nmls, d_out, d_nffn, d_gate, d_up, d_down, d_nfin,
            nm_meta, nm_nmix, nm_in, nm_bg, nm_cw, nm_nmls, nm_out, nm_nffn, nm_gate, nm_up, nm_down, nm_nfin,
            nv_meta, nv_nmix, nv_in, nv_bg, nv_cw, nv_nmls, nv_out, nv_nffn, nv_gate, nv_up, nv_down, nv_nfin)
```

```python
import functools

import numpy as np
import jax
import jax.numpy as jnp
from jax import lax
from jax.experimental import pallas as pl
from jax.experimental.pallas import tpu as pltpu

F32 = jnp.float32
BF16 = jnp.bfloat16
MESH = pl.DeviceIdType.MESH

D_MODEL = 2048
DEPTH = 2
N_META = 16
MLSTM_W = 1024
CONV_W = 1024
HEADS = 4
DV = 256
DQK = 128
QK_W = 512
CHUNK = 64
PAD_FRONT = 48
TOK0 = PAD_FRONT + N_META
D_FF = 5632
N_DEV = 8
FF_SH = D_FF // N_DEV
D_IN = 6152
IN_SH = D_IN // N_DEV
OUT_SH = D_MODEL // N_DEV
GATE_COL = 3072
PM_W = GATE_COL + 128
GATE_CAP = 15.0
EPS = 1e-6
QSCALE = DQK ** -0.5

ADAM_LR = 0.001
ADAM_B1 = 0.9
ADAM_B2 = 0.999
ADAM_EPS = 1e-08
ADAM_WD = 0.01
ADAM_STEP = 10

V7X_VMEM_LIMIT = 50 * 1024 * 1024


def _params(**kw):
    return pltpu.CompilerParams(vmem_limit_bytes=V7X_VMEM_LIMIT, **kw)


def _tile(n, target, mult):
    best = None
    for t in range(mult, min(n, target) + 1, mult):
        if n % t == 0:
            best = t
    return best if best is not None else n


def _sigmoid(x):
    return 1.0 / (1.0 + jnp.exp(-x))


NN = ((1,), (0,))
NT = ((1,), (1,))
TN = ((0,), (0,))


def _matmul(name, a, b, out_shape, out_dtype, grid, a_bs, b_bs, o_bs, dims, nk, acc_shape=None,
            res=None, res_bs=None, dep=None):
    has_res = res is not None
    n_in = 2 + has_res + (dep is not None)

    def body(*refs):
        a_ref, b_ref = refs[0], refs[1]
        r_ref = refs[2] if has_res else None
        o_ref = refs[n_in]
        x = lax.dot_general(a_ref[...], b_ref[...], (dims, ((), ())), preferred_element_type=F32)
        if nk == 1:
            if has_res:
                x = x + r_ref[...]
            o_ref[...] = x.astype(o_ref.dtype)
            return
        acc = refs[n_in + 1]
        k = pl.program_id(len(grid) - 1)

        @pl.when(k == 0)
        def _():
            acc[...] = (x + r_ref[...]) if has_res else x

        @pl.when(k > 0)
        def _():
            acc[...] += x

        @pl.when(k == nk - 1)
        def _():
            o_ref[...] = acc[...].astype(o_ref.dtype)

    ins = [a, b] + ([res] if has_res else [])
    specs = [a_bs, b_bs] + ([res_bs] if has_res else [])
    if dep is not None:
        ins.append(dep)
        specs.append(pl.BlockSpec((8, 128), lambda *_: (0, 0)))
    scratch = [pltpu.VMEM(acc_shape, F32)] if nk > 1 else []
    return pl.pallas_call(
        body, name=name, grid=grid, in_specs=specs, out_specs=o_bs,
        out_shape=jax.ShapeDtypeStruct(out_shape, out_dtype), scratch_shapes=scratch,
        compiler_params=_params(),
    )(*ins)


def _mm_nn(name, a, b, out_dtype, res=None, tm=1056, tn=512, dep=None):
    r, k = a.shape
    n = b.shape[1]
    tm, tn = _tile(r, tm, 8), _tile(n, tn, 128)
    return _matmul(name, a, b, (r, n), out_dtype, (r // tm, n // tn, 1),
                   pl.BlockSpec((tm, k), lambda i, j, s: (i, 0)),
                   pl.BlockSpec((k, tn), lambda i, j, s: (0, j)),
                   pl.BlockSpec((tm, tn), lambda i, j, s: (i, j)), NN, 1,
                   res=res, res_bs=pl.BlockSpec((tm, tn), lambda i, j, s: (i, j)), dep=dep)


def _mm_nn_kt(name, a, b, out_dtype, tm=1056, tn=1024, tk=640, dep=None):
    r, k = a.shape
    n = b.shape[1]
    tm, tn, tk = _tile(r, tm, 8), _tile(n, tn, 128), _tile(k, tk, 128)
    nk = k // tk
    return _matmul(name, a, b, (r, n), out_dtype, (r // tm, n // tn, nk),
                   pl.BlockSpec((tm, tk), lambda i, j, s: (i, s)),
                   pl.BlockSpec((tk, tn), lambda i, j, s: (s, j)),
                   pl.BlockSpec((tm, tn), lambda i, j, s: (i, j)), NN, nk, acc_shape=(tm, tn), dep=dep)


def _mm_nn_ksum(name, a3, b3, out_dtype, res=None, tm=1056, tn=1024, dep=None):
    e, r, kb = a3.shape
    n = b3.shape[2]
    tm, tn = _tile(r, tm, 8), _tile(n, tn, 128)
    return _matmul(name, a3, b3, (r, n), out_dtype, (r // tm, n // tn, e),
                   pl.BlockSpec((None, tm, kb), lambda i, j, s: (s, i, 0)),
                   pl.BlockSpec((None, kb, tn), lambda i, j, s: (s, 0, j)),
                   pl.BlockSpec((tm, tn), lambda i, j, s: (i, j)), NN, e, acc_shape=(tm, tn),
                   res=res, res_bs=pl.BlockSpec((tm, tn), lambda i, j, s: (i, j)), dep=dep)


def _mm_nt(name, a, b, out_dtype, res=None, tm=1056, tn=512, tk=640, n=None, dep=None):
    r, k = a.shape
    n = b.shape[0] if n is None else n
    tm, tn, tk = _tile(r, tm, 8), _tile(n, tn, 128), _tile(k, tk, 128)
    nk = k // tk
    return _matmul(name, a, b, (r, n), out_dtype, (r // tm, n // tn, nk),
                   pl.BlockSpec((tm, tk), lambda i, j, s: (i, s)),
                   pl.BlockSpec((tn, tk), lambda i, j, s: (j, s)),
                   pl.BlockSpec((tm, tn), lambda i, j, s: (i, j)), NT, nk, acc_shape=(tm, tn),
                   res=res, res_bs=pl.BlockSpec((tm, tn), lambda i, j, s: (i, j)), dep=dep)


def _mm_nt_bcols(name, a, b3, out_dtype, tm=1056, dep=None):
    r, k = a.shape
    e, n, _ = b3.shape
    tm = _tile(r, tm, 8)
    return _matmul(name, a, b3, (e, r, n), out_dtype, (r // tm, e, 1),
                   pl.BlockSpec((tm, k), lambda i, g, s: (i, 0)),
                   pl.BlockSpec((None, n, k), lambda i, g, s: (g, 0, 0)),
                   pl.BlockSpec((None, tm, n), lambda i, g, s: (g, i, 0)), NT, 1, dep=dep)


def _mm_tn(name, a, b, out_dtype, tm=1024, tn=640, dep=None):
    r, m = a.shape
    n = b.shape[1]
    tm, tn = _tile(m, tm, 128), _tile(n, tn, 128)
    return _matmul(name, a, b, (m, n), out_dtype, (m // tm, n // tn, 1),
                   pl.BlockSpec((r, tm), lambda i, j, s: (0, i)),
                   pl.BlockSpec((r, tn), lambda i, j, s: (0, j)),
                   pl.BlockSpec((tm, tn), lambda i, j, s: (i, j)), TN, 1, dep=dep)


def _mm_tn_acols(name, a3, b, out_dtype, tn=1024, dep=None):
    e, r, m = a3.shape
    n = b.shape[1]
    tn = _tile(n, tn, 128)
    return _matmul(name, a3, b, (e, m, n), out_dtype, (n // tn, e, 1),
                   pl.BlockSpec((None, r, m), lambda j, g, s: (g, 0, 0)),
                   pl.BlockSpec((r, tn), lambda j, g, s: (0, j)),
                   pl.BlockSpec((None, m, tn), lambda j, g, s: (g, 0, j)), TN, 1, dep=dep)


def _rms_fwd(name, h, w):
    r, d = h.shape
    tr = _tile(r, 264, 8)

    def body(h_ref, w_ref, o_ref):
        x = h_ref[...]
        rs = lax.rsqrt(jnp.mean(x * x, axis=1, keepdims=True) + EPS)
        o_ref[...] = (x * rs * w_ref[...]).astype(BF16)

    return pl.pallas_call(
        body, name=name, grid=(r // tr,),
        in_specs=[pl.BlockSpec((tr, d), lambda i: (i, 0)), pl.BlockSpec((1, d), lambda i: (0, 0))],
        out_specs=pl.BlockSpec((tr, d), lambda i: (i, 0)),
        out_shape=jax.ShapeDtypeStruct((r, d), BF16), compiler_params=_params(),
    )(h, w)


def _rms_bwd(name, x, w, dy, dres):
    r, d = x.shape
    tr = _tile(r, 264, 8)

    def body(x_ref, w_ref, dy_ref, dr_ref, dx_ref, dxb_ref, dw_ref):
        xv = x_ref[...]
        g = dy_ref[...]
        rs = lax.rsqrt(jnp.mean(xv * xv, axis=1, keepdims=True) + EPS)
        wg = g * w_ref[...]
        dx = rs * wg - xv * (rs * rs * rs) * jnp.mean(xv * wg, axis=1, keepdims=True) + dr_ref[...]
        dx_ref[...] = dx
        dxb_ref[...] = dx.astype(BF16)
        part = jnp.sum(g * xv * rs, axis=0, keepdims=True)

        @pl.when(pl.program_id(0) == 0)
        def _():
            dw_ref[...] = part

        @pl.when(pl.program_id(0) > 0)
        def _():
            dw_ref[...] += part

    row = pl.BlockSpec((tr, d), lambda i: (i, 0))
    vec = pl.BlockSpec((1, d), lambda i: (0, 0))
    return pl.pallas_call(
        body, name=name, grid=(r // tr,), in_specs=[row, vec, row, row], out_specs=[row, row, vec],
        out_shape=[jax.ShapeDtypeStruct((r, d), F32), jax.ShapeDtypeStruct((r, d), BF16),
                   jax.ShapeDtypeStruct((1, d), F32)],
        compiler_params=_params(),
    )(x, w, dy, dres)


def _final_loss(name, h, w, target):
    r, d = h.shape
    nb = r // CHUNK

    def body(h_ref, w_ref, t_ref, dh_ref, dhb_ref, dw_ref, ls_ref):
        i = pl.program_id(0)

        @pl.when(i == 0)
        def _():
            dh_ref[...] = jnp.zeros_like(dh_ref)
            dhb_ref[...] = jnp.zeros_like(dhb_ref)
            dw_ref[...] = jnp.zeros_like(dw_ref)
            ls_ref[...] = jnp.zeros_like(ls_ref)

        @pl.when(i > 0)
        def _():
            xv = h_ref[...]
            wv = w_ref[...]
            rs = lax.rsqrt(jnp.mean(xv * xv, axis=1, keepdims=True) + EPS)
            err = xv * rs * wv - t_ref[...]
            sq = jnp.sum(jnp.sum(err * err, axis=1, keepdims=True), axis=0, keepdims=True)
            ls_ref[...] += jnp.broadcast_to(sq * (0.5 / d), ls_ref.shape)
            g = err * (1.0 / d)
            wg = g * wv
            dx = rs * wg - xv * (rs * rs * rs) * jnp.mean(xv * wg, axis=1, keepdims=True)
            dh_ref[...] = dx
            dhb_ref[...] = dx.astype(BF16)
            dw_ref[...] += jnp.sum(g * xv * rs, axis=0, keepdims=True)

    row = pl.BlockSpec((CHUNK, d), lambda i: (i, 0))
    vec = pl.BlockSpec((1, d), lambda i: (0, 0))
    return pl.pallas_call(
        body, name=name, grid=(nb,),
        in_specs=[row, vec, pl.BlockSpec((CHUNK, d), lambda i: (jnp.maximum(i - 1, 0), 0))],
        out_specs=[row, row, vec, pl.BlockSpec((1, 128), lambda i: (0, 0))],
        out_shape=[jax.ShapeDtypeStruct((r, d), F32), jax.ShapeDtypeStruct((r, d), BF16),
                   jax.ShapeDtypeStruct((1, d), F32), jax.ShapeDtypeStruct((1, 128), F32)],
        compiler_params=_params(),
    )(h, w, target)


def _ffn_in(name, hf, wg_t, wu_t, dep=None, tm=1056, tn=512):
    r, d = hf.shape
    f = wg_t.shape[0]
    tm, tn = _tile(r, tm, 8), _tile(f, tn, 128)

    def body(h_ref, wg_ref, wu_ref, *rest):
        g_ref, u_ref, a_ref = rest[-3:]
        x = h_ref[...]
        g = lax.dot_general(x, wg_ref[...], (NT, ((), ())), preferred_element_type=F32)
        u = lax.dot_general(x, wu_ref[...], (NT, ((), ())), preferred_element_type=F32)
        g_ref[...] = g.astype(BF16)
        u_ref[...] = u.astype(BF16)
        a_ref[...] = (g * _sigmoid(g) * u).astype(BF16)

    wspec = pl.BlockSpec((tn, d), lambda i, j: (j, 0))
    ospec = pl.BlockSpec((tm, tn), lambda i, j: (i, j))
    ins, specs = [hf, wg_t, wu_t], [pl.BlockSpec((tm, d), lambda i, j: (i, 0)), wspec, wspec]
    if dep is not None:
        ins.append(dep)
        specs.append(pl.BlockSpec((8, 128), lambda *_: (0, 0)))
    return pl.pallas_call(
        body, name=name, grid=(r // tm, f // tn), in_specs=specs, out_specs=[ospec] * 3,
        out_shape=[jax.ShapeDtypeStruct((r, f), BF16)] * 3, compiler_params=_params(),
    )(*ins)


def _ffn_act_bwd(name, dh, wd, g, u, dep=None, tm=1056, tn=512):
    r, d = dh.shape
    f = wd.shape[0]
    tm, tn = _tile(r, tm, 8), _tile(f, tn, 128)

    def body(dh_ref, wd_ref, g_ref, u_ref, *rest):
        dg_ref, du_ref = rest[-2:]
        da = lax.dot_general(dh_ref[...], wd_ref[...], (NT, ((), ())), preferred_element_type=F32)
        gv = g_ref[...].astype(F32)
        s = _sigmoid(gv)
        dg_ref[...] = (da * u_ref[...].astype(F32) * (s + gv * s * (1.0 - s))).astype(BF16)
        du_ref[...] = (da * gv * s).astype(BF16)

    tile = pl.BlockSpec((tm, tn), lambda i, j: (i, j))
    ins = [dh, wd, g, u]
    specs = [pl.BlockSpec((tm, d), lambda i, j: (i, 0)), pl.BlockSpec((tn, d), lambda i, j: (j, 0)), tile, tile]
    if dep is not None:
        ins.append(dep)
        specs.append(pl.BlockSpec((8, 128), lambda *_: (0, 0)))
    return pl.pallas_call(
        body, name=name, grid=(r // tm, f // tn), in_specs=specs, out_specs=[tile] * 2,
        out_shape=[jax.ShapeDtypeStruct((r, f), BF16)] * 2, compiler_params=_params(),
    )(*ins)


def _shift_down(a, k):
    row = lax.broadcasted_iota(jnp.int32, a.shape, 0)
    return jnp.where(row >= k, pltpu.roll(a, k, 0), 0.0)


def _shift_up(a, k):
    n = a.shape[0]
    row = lax.broadcasted_iota(jnp.int32, a.shape, 0)
    return jnp.where(row < n - k, pltpu.roll(a, n - k, 0), 0.0)


def _conv_fwd(name, pc, cw):
    _, r, w = pc.shape

    def body(pc_ref, cw_ref, o_ref):
        a = pc_ref[2] * pc_ref[0]
        cwv = cw_ref[...]
        conv = _shift_down(a, 2) * cwv[0:1] + _shift_down(a, 1) * cwv[1:2] + a * cwv[2:3]
        o_ref[...] = (pc_ref[1] * conv).astype(BF16)

    return pl.pallas_call(
        body, name=name, grid=(w // 128,),
        in_specs=[pl.BlockSpec((3, r, 128), lambda j: (0, 0, j)), pl.BlockSpec((8, 128), lambda j: (0, j))],
        out_specs=pl.BlockSpec((r, 128), lambda j: (0, j)),
        out_shape=jax.ShapeDtypeStruct((r, w), BF16), compiler_params=_params(),
    )(pc, cw)


def _conv_bwd(name, dcat, pc, cw):
    _, r, w = pc.shape
    nblk = w // 128

    def body(dy_ref, pc_ref, cw_ref, dpc_ref, dcw_ref):
        u, gb, gc = pc_ref[0], pc_ref[1], pc_ref[2]
        cwv = cw_ref[...]
        dy = dy_ref[...]
        a = gc * u
        a1, a2 = _shift_down(a, 1), _shift_down(a, 2)
        conv = a2 * cwv[0:1] + a1 * cwv[1:2] + a * cwv[2:3]
        dconv = dy * gb
        da = dconv * cwv[2:3] + _shift_up(dconv, 1) * cwv[1:2] + _shift_up(dconv, 2) * cwv[0:1]
        dpc_ref[0] = (da * gc).astype(BF16)
        dpc_ref[1] = (dy * conv).astype(BF16)
        dpc_ref[2] = (da * u).astype(BF16)
        row = lax.broadcasted_iota(jnp.int32, (8, 128), 0)
        dw0 = jnp.sum(dconv * a2, axis=0, keepdims=True)
        dw1 = jnp.sum(dconv * a1, axis=0, keepdims=True)
        dw2 = jnp.sum(dconv * a, axis=0, keepdims=True)
        dcw_ref[...] = jnp.where(row == 0, dw0, jnp.where(row == 1, dw1, jnp.where(row == 2, dw2, 0.0)))

    return pl.pallas_call(
        body, name=name, grid=(nblk,),
        in_specs=[pl.BlockSpec((r, 128), lambda j: (0, nblk + j)),
                  pl.BlockSpec((3, r, 128), lambda j: (0, 0, j)), pl.BlockSpec((8, 128), lambda j: (0, j))],
        out_specs=[pl.BlockSpec((3, r, 128), lambda j: (0, 0, j)), pl.BlockSpec((8, 128), lambda j: (0, j))],
        out_shape=[jax.ShapeDtypeStruct((3, r, w), BF16), jax.ShapeDtypeStruct((8, w), F32)],
        compiler_params=_params(),
    )(dcat, pc, cw)


def _dot(a, b, dims):
    return lax.dot_general(a, b, (dims, ((), ())), preferred_element_type=F32)


def _col_to_row(xc, eye):
    return jnp.sum(jnp.where(eye, xc, 0.0), axis=0, keepdims=True)


def _row_to_col(xr, eye):
    return jnp.sum(jnp.where(eye, xr, 0.0), axis=1, keepdims=True)


def _gate_tiles(graw, bias, row0):
    th = jnp.tanh((graw + bias) / GATE_CAP)
    z = GATE_CAP * th
    row = lax.broadcasted_iota(jnp.int32, graw.shape, 0) + row0
    real = row >= PAD_FRONT
    li = jnp.where(real, z, -jnp.inf)
    lf = jnp.where(real, jnp.minimum(z, 0.0) - jnp.log(1.0 + jnp.exp(-jnp.abs(z))), 0.0)
    return th, z, li, lf, real


def _chunk_common(pm, h, li, lf, cst, nst, mst, tril, eye):
    kraw = pm[:, QK_W + h * DQK:QK_W + (h + 1) * DQK]
    q = (pm[:, h * DQK:(h + 1) * DQK] * QSCALE).astype(BF16)
    k = kraw.astype(BF16)
    v = pm[:, 2 * QK_W + h * DV:2 * QK_W + (h + 1) * DV].astype(BF16)
    li_c = li[:, h:h + 1]
    lf_c = lf[:, HEADS + h:HEADS + h + 1]
    li_r = _col_to_row(li_c, eye)
    lf_r = _col_to_row(lf_c, eye)
    b_c = jnp.sum(jnp.where(tril, lf_r, 0.0), axis=1, keepdims=True)
    b_r = _col_to_row(b_c, eye)
    dmat = jnp.where(tril, b_c - b_r + li_r, -jnp.inf)
    inter = b_c + mst
    mt = jnp.maximum(inter, jnp.max(dmat, axis=1, keepdims=True))
    w_inter = jnp.exp(inter - mt)
    p = jnp.exp(dmat - mt)
    s = _dot(q, k, NT) * p
    cb = cst.astype(BF16)
    nb = nst.astype(BF16).astype(F32)
    qc = _dot(q, cb, NN)
    qn = jnp.sum(q.astype(F32) * nb, axis=1, keepdims=True)
    den = w_inter * qn + jnp.sum(s, axis=1, keepdims=True)
    dn = jnp.maximum(jnp.abs(den), jnp.exp(-mt))
    b_end = b_c[CHUNK - 1:CHUNK, :]
    decay = b_end - b_c + li_c
    m_new = jnp.maximum(b_end + mst, jnp.max(decay, axis=0, keepdims=True))
    w_old = jnp.exp(b_end + mst - m_new)
    w_in = jnp.exp(decay - m_new)
    kw = (w_in * kraw).astype(BF16)
    return dict(q=q, k=k, v=v, kraw=kraw, mt=mt, w_inter=w_inter, p=p, s=s, cb=cb, nb=nb, qc=qc, qn=qn,
                den=den, dn=dn, m_new=m_new, w_old=w_old, w_in=w_in, kw=kw)


def _mlstm_fwd(name, pm, bias, nw):
    r = pm.shape[0]
    nc = r // CHUNK

    def body(pm_ref, b_ref, nw_ref, hm_ref, ht_ref, cs_ref, ns_ref, ms_ref, c_scr, n_scr, m_scr):
        ci = pl.program_id(0)

        @pl.when(ci == 0)
        def _():
            c_scr[...] = jnp.zeros_like(c_scr)
            n_scr[...] = jnp.zeros_like(n_scr)
            m_scr[...] = jnp.zeros_like(m_scr)

        pmv = pm_ref[...]
        rr = lax.broadcasted_iota(jnp.int32, (CHUNK, CHUNK), 0)
        cc = lax.broadcasted_iota(jnp.int32, (CHUNK, CHUNK), 1)
        tril, eye = cc <= rr, cc == rr
        _, _, li, lf, _ = _gate_tiles(pmv[:, GATE_COL:GATE_COL + 128], b_ref[...], ci * CHUNK)
        nwv = nw_ref[...]
        for h in range(HEADS):
            cst, nst, mst = c_scr[h], n_scr[h], m_scr[h]
            cs_ref[h] = cst
            ns_ref[h] = nst
            ms_ref[h] = mst
            f = _chunk_common(pmv, h, li, lf, cst, nst, mst, tril, eye)
            num = f["w_inter"] * f["qc"] + _dot(f["s"].astype(BF16), f["v"], NN)
            hh = num / f["dn"]
            c_scr[h] = f["w_old"] * cst + _dot(f["kw"], f["v"], TN)
            n_scr[h] = f["w_old"] * nst + jnp.sum(
                f["w_in"].astype(BF16).astype(F32) * f["k"].astype(F32), axis=0, keepdims=True)
            m_scr[h] = f["m_new"]
            sl = slice(h * DV, (h + 1) * DV)
            rs = lax.rsqrt(jnp.mean(hh * hh, axis=1, keepdims=True) + EPS)
            og = pmv[:, 2 * QK_W + MLSTM_W + h * DV:2 * QK_W + MLSTM_W + (h + 1) * DV]
            ht_ref[:, sl] = hh
            hm_ref[:, sl] = (_sigmoid(og) * (hh * rs * nwv[:, sl])).astype(BF16)

    return pl.pallas_call(
        body, name=name, grid=(nc,),
        in_specs=[pl.BlockSpec((CHUNK, PM_W), lambda i: (i, 0)), pl.BlockSpec((1, 128), lambda i: (0, 0)),
                  pl.BlockSpec((1, MLSTM_W), lambda i: (0, 0))],
        out_specs=[pl.BlockSpec((CHUNK, MLSTM_W), lambda i: (i, 0)),
                   pl.BlockSpec((CHUNK, MLSTM_W), lambda i: (i, 0)),
                   pl.BlockSpec((None, HEADS, DQK, DV), lambda i: (i, 0, 0, 0)),
                   pl.BlockSpec((None, HEADS, 1, DQK), lambda i: (i, 0, 0, 0)),
                   pl.BlockSpec((None, HEADS, 1, 1), lambda i: (i, 0, 0, 0))],
        out_shape=[jax.ShapeDtypeStruct((r, MLSTM_W), BF16), jax.ShapeDtypeStruct((r, MLSTM_W), F32),
                   jax.ShapeDtypeStruct((nc, HEADS, DQK, DV), F32),
                   jax.ShapeDtypeStruct((nc, HEADS, 1, DQK), F32),
                   jax.ShapeDtypeStruct((nc, HEADS, 1, 1), F32)],
        scratch_shapes=[pltpu.VMEM((HEADS, DQK, DV), F32), pltpu.VMEM((HEADS, 1, DQK), F32),
                        pltpu.VMEM((HEADS, 1, 1), F32)],
        compiler_params=_params(),
    )(pm, bias, nw)


def _mlstm_bwd(name, dcat, pm, ht, cs, ns, ms, bias, nw):
    r = pm.shape[0]
    nc = r // CHUNK

    def body(dy_ref, pm_ref, ht_ref, cs_ref, ns_ref, ms_ref, b_ref, nw_ref, dpm_ref, dnw_ref, db_ref,
             dc_scr, dn_scr):
        step = pl.program_id(0)
        ci = nc - 1 - step

        @pl.when(step == 0)
        def _():
            dc_scr[...] = jnp.zeros_like(dc_scr)
            dn_scr[...] = jnp.zeros_like(dn_scr)
            dnw_ref[...] = jnp.zeros_like(dnw_ref)
            db_ref[...] = jnp.zeros_like(db_ref)

        pmv = pm_ref[...]
        rr = lax.broadcasted_iota(jnp.int32, (CHUNK, CHUNK), 0)
        cc = lax.broadcasted_iota(jnp.int32, (CHUNK, CHUNK), 1)
        tril, eye, triu = cc <= rr, cc == rr, cc >= rr
        th, z, li, lf, real = _gate_tiles(pmv[:, GATE_COL:GATE_COL + 128], b_ref[...], ci * CHUNK)
        lane = lax.broadcasted_iota(jnp.int32, (CHUNK, 128), 1)
        rowid = lax.broadcasted_iota(jnp.int32, (CHUNK, 1), 0)
        nwv = nw_ref[...]
        dgt = jnp.zeros((CHUNK, 128), F32)
        for h in range(HEADS):
            cst, nst, mst = cs_ref[h], ns_ref[h], ms_ref[h]
            f = _chunk_common(pmv, h, li, lf, cst, nst, mst, tril, eye)
            q, k, v, s, p = f["q"], f["k"], f["v"], f["s"], f["p"]
            w_inter, w_in, w_old, dn = f["w_inter"], f["w_in"], f["w_old"], f["dn"]
            sl = slice(h * DV, (h + 1) * DV)
            osl = slice(2 * QK_W + MLSTM_W + h * DV, 2 * QK_W + MLSTM_W + (h + 1) * DV)
            hh = ht_ref[:, sl]
            y = dy_ref[:, sl]
            sg = _sigmoid(pmv[:, osl])
            rs = lax.rsqrt(jnp.mean(hh * hh, axis=1, keepdims=True) + EPS)
            nwh = nwv[:, sl]
            dpm_ref[:, osl] = (y * (hh * rs * nwh) * sg * (1.0 - sg)).astype(BF16)
            dhn = y * sg
            dnw_ref[:, sl] += jnp.sum(dhn * hh * rs, axis=0, keepdims=True)
            wd = dhn * nwh
            dhh = rs * wd - hh * (rs * rs * rs) * jnp.mean(hh * wd, axis=1, keepdims=True)
            dnum = dhh / dn
            dd = -jnp.sum(dhh * hh, axis=1, keepdims=True) / dn
            dden = jnp.where(jnp.abs(f["den"]) > jnp.exp(-f["mt"]), dd * jnp.sign(f["den"]), 0.0)
            dnum_b = dnum.astype(BF16)
            wdn = (w_inter * dnum).astype(BF16)
            wid = (w_inter * dden).astype(BF16).astype(F32)
            ds = _dot(dnum_b, v, NT) + dden
            dsp = (ds * p).astype(BF16)
            dq = _dot(dsp, k, NN) + _dot(wdn, f["cb"], NT) + wid * f["nb"]
            dk = _dot(dsp, q, TN)
            dv = _dot(s.astype(BF16), dnum_b, TN)
            g = ds * s
            g_col = _row_to_col(jnp.sum(g, axis=0, keepdims=True), eye)
            db = jnp.sum(g, axis=1, keepdims=True) - g_col
            dli = g_col
            db = db + (jnp.sum(dnum * f["qc"], axis=1, keepdims=True) + dden * f["qn"]) * w_inter
            dcn, dnn = dc_scr[h], dn_scr[h]
            dcnb = dcn.astype(BF16)
            dnnb = dnn.astype(BF16).astype(F32)
            dkw = _dot(v, dcnb, NT) + dnnb
            dk = dk + w_in * dkw
            dv = dv + _dot(f["kw"], dcnb, NN)
            ddecay = jnp.sum(dkw * f["kraw"], axis=1, keepdims=True) * w_in
            dw_old = (jnp.sum(jnp.sum(dcn * cst, axis=1, keepdims=True), axis=0, keepdims=True)
                      + jnp.sum(dnn * nst, axis=1, keepdims=True))
            db_end = dw_old * w_old + jnp.sum(ddecay, axis=0, keepdims=True)
            db = db - ddecay + jnp.where(rowid == CHUNK - 1, db_end, 0.0)
            dli = dli + ddecay
            dc_scr[h] = w_old * dcn + _dot(q, wdn, TN)
            dn_scr[h] = w_old * dnn + jnp.sum(wid * q.astype(F32), axis=0, keepdims=True)
            dlf = jnp.sum(jnp.where(triu, _col_to_row(db, eye), 0.0), axis=1, keepdims=True)
            dgt = dgt + jnp.where(lane == h, dli, 0.0) + jnp.where(lane == HEADS + h, dlf, 0.0)
            dpm_ref[:, h * DQK:(h + 1) * DQK] = (dq * QSCALE).astype(BF16)
            dpm_ref[:, QK_W + h * DQK:QK_W + (h + 1) * DQK] = dk.astype(BF16)
            dpm_ref[:, 2 * QK_W + h * DV:2 * QK_W + (h + 1) * DV] = dv.astype(BF16)
        dact = jnp.where(lane < HEADS, 1.0, 1.0 - _sigmoid(z)) * (1.0 - th * th)
        dgraw = jnp.where(real & (lane < 2 * HEADS), dgt * dact, 0.0)
        dpm_ref[:, GATE_COL:GATE_COL + 128] = dgraw.astype(BF16)
        db_ref[...] += jnp.sum(dgraw, axis=0, keepdims=True)

    rev = lambda i: (nc - 1 - i, 0)
    rev4 = lambda i: (nc - 1 - i, 0, 0, 0)
    return pl.pallas_call(
        body, name=name, grid=(nc,),
        in_specs=[pl.BlockSpec((CHUNK, MLSTM_W), rev), pl.BlockSpec((CHUNK, PM_W), rev),
                  pl.BlockSpec((CHUNK, MLSTM_W), rev),
                  pl.BlockSpec((None, HEADS, DQK, DV), rev4), pl.BlockSpec((None, HEADS, 1, DQK), rev4),
                  pl.BlockSpec((None, HEADS, 1, 1), rev4),
                  pl.BlockSpec((1, 128), lambda i: (0, 0)), pl.BlockSpec((1, MLSTM_W), lambda i: (0, 0))],
        out_specs=[pl.BlockSpec((CHUNK, PM_W), rev), pl.BlockSpec((1, MLSTM_W), lambda i: (0, 0)),
                   pl.BlockSpec((1, 128), lambda i: (0, 0))],
        out_shape=[jax.ShapeDtypeStruct((r, PM_W), BF16), jax.ShapeDtypeStruct((1, MLSTM_W), F32),
                   jax.ShapeDtypeStruct((1, 128), F32)],
        scratch_shapes=[pltpu.VMEM((HEADS, DQK, DV), F32), pltpu.VMEM((HEADS, 1, DQK), F32)],
        compiler_params=_params(),
    )(dcat, pm, ht, cs, ns, ms, bias, nw)


def _my_place():
    return lax.axis_index("x"), lax.axis_index("y"), lax.axis_index("c")


def _flip(v, bit):
    return 1 - v if bit else v


def _exchange_small(name, blk, reduce):
    r, c = blk.shape

    def body(x_ref, o_ref, *rest):
        slots = rest[0] if reduce else o_ref
        send_sems, recv_sems = rest[-2], rest[-1]
        x, y, cc = _my_place()
        me = 4 * x + 2 * y + cc
        slots[me] = x_ref[...]
        copies = []
        for k in range(1, N_DEV):
            peer = (_flip(x, k & 4), _flip(y, k & 2), _flip(cc, k & 1))
            cp = pltpu.make_async_remote_copy(
                src_ref=x_ref, dst_ref=slots.at[me], send_sem=send_sems.at[k - 1],
                recv_sem=recv_sems.at[k - 1], device_id=peer, device_id_type=MESH)
            cp.start()
            copies.append(cp)
        for cp in copies:
            cp.wait()
        if reduce:
            acc = slots[0]
            for d in range(1, N_DEV):
                acc = acc + slots[d]
            o_ref[...] = acc

    scratch = ([pltpu.VMEM((N_DEV, r, c), F32)] if reduce else []) + [
        pltpu.SemaphoreType.DMA((N_DEV - 1,)), pltpu.SemaphoreType.DMA((N_DEV - 1,))]
    return pl.pallas_call(
        body, name=name,
        out_shape=jax.ShapeDtypeStruct((r, c) if reduce else (N_DEV, r, c), F32),
        in_specs=[pl.BlockSpec(memory_space=pltpu.VMEM)], out_specs=pl.BlockSpec(memory_space=pltpu.VMEM),
        scratch_shapes=scratch, compiler_params=_params(),
    )(blk)


HBM_SPEC = pl.BlockSpec(memory_space=pltpu.HBM)
SEM_SPEC = pl.BlockSpec(memory_space=pltpu.SEMAPHORE)
ANY_SPEC = pl.BlockSpec(memory_space=pl.ANY)
DATAFLOW = pltpu.SideEffectType.DATAFLOW_SIDE_EFFECTING


def _split_copy(name, arrays, start=None, wait=None, after=None):
    n = len(arrays)
    n_wait = 2 if wait else 0
    n_after = 0 if after is None else 1
    n_new = 2 if start else 0

    def body(*refs):
        ins = refs[:n]
        if wait:
            for cp in wait[0](ins, refs[n], refs[n + 1]):
                cp.wait_send()
                cp.wait_recv()
        if start:
            at = n + n_wait + n_after
            for cp in start[0](ins, refs[at], refs[at + 1]):
                cp.start()
            token = refs[at + 2 + n]
            token[...] = jnp.zeros_like(token)

    operands = [pltpu.with_memory_space_constraint(a, pltpu.HBM) for a in arrays]
    in_specs = [HBM_SPEC] * n
    if wait:
        operands += list(wait[1])
        in_specs += [SEM_SPEC, SEM_SPEC]
    if after is not None:
        operands.append(after)
        in_specs.append(ANY_SPEC)
    out_shape, out_specs = [], []
    if start:
        out_shape += [pltpu.SemaphoreType.DMA((start[1],)), pltpu.SemaphoreType.DMA((start[1],))]
        out_specs += [SEM_SPEC, SEM_SPEC]
    out_shape += [pltpu.HBM(a.shape, a.dtype) for a in arrays]
    out_specs += [HBM_SPEC] * n
    if start:
        out_shape.append(jax.ShapeDtypeStruct((8, 128), F32))
        out_specs.append(pl.BlockSpec(memory_space=pltpu.VMEM))
    outs = pl.pallas_call(
        body, name=name, in_specs=in_specs, out_specs=out_specs, out_shape=out_shape,
        input_output_aliases={i: n_new + i for i in range(n)},
        compiler_params=pltpu.CompilerParams(has_side_effects=DATAFLOW),
    )(*operands)
    thru = list(outs[n_new:n_new + n])
    return thru, (tuple(outs[:2]) if start else None), (outs[n_new + n] if start else None)


def _remote(src, dst, send_sems, recv_sems, k, to):
    return pltpu.make_async_remote_copy(src_ref=src, dst_ref=dst, send_sem=send_sems.at[k],
                                        recv_sem=recv_sems.at[k], device_id=to, device_id_type=MESH)


def _gather_first(n):
    def copies(refs, send_sems, recv_sems):
        x, y, c = _my_place()
        targets = [(x, y, 1 - c), (1 - x, y, c), (x, 1 - y, c), (1 - x, 1 - y, c)]
        out = []
        for a in range(n):
            blk = refs[a].at[4 * x + 2 * y + c]
            out += [_remote(blk, blk, send_sems, recv_sems, 4 * a + k, to) for k, to in enumerate(targets)]
        return out
    return copies


def _gather_pass(n):
    def copies(refs, send_sems, recv_sems):
        x, y, c = _my_place()
        out = []
        for a in range(n):
            for j, (px, py) in enumerate([(1 - x, y), (x, 1 - y), (1 - x, 1 - y)]):
                blk = refs[a].at[4 * px + 2 * py + c]
                out.append(_remote(blk, blk, send_sems, recv_sems, 3 * a + j, (x, y, 1 - c)))
        return out
    return copies


def _scatter_sibling(n):
    def copies(refs, send_sems, recv_sems):
        x, y, c = _my_place()
        return [_remote(refs[a].at[2 * j + 1 - c], refs[n + a].at[j], send_sems, recv_sems, 4 * a + j, (x, y, 1 - c))
                for a in range(n) for j in range(4)]
    return copies


def _scatter_chips(n):
    def copies(refs, send_sems, recv_sems):
        x, y, c = _my_place()
        out = []
        for a in range(n):
            for k in range(1, 4):
                px, py = _flip(x, k & 2), _flip(y, k & 1)
                out.append(_remote(refs[a].at[2 * px + py], refs[n + a].at[2 * x + y], send_sems, recv_sems,
                                   3 * a + k - 1, (px, py, c)))
        return out
    return copies


def _pair_sum(name, core, g, t):
    _, r, c = g.shape
    tr = _tile(r, 512, 8)
    g4 = g.reshape(4, 2, r, c)

    def body(core_ref, g_ref, t_ref, o_ref):
        o_ref[...] = (g_ref[...].astype(F32) + t_ref[...].astype(F32)).astype(BF16)

    return pl.pallas_call(
        body, name=name,
        grid_spec=pltpu.PrefetchScalarGridSpec(
            num_scalar_prefetch=1, grid=(4, r // tr),
            in_specs=[pl.BlockSpec((None, None, tr, c), lambda j, i, core_ref: (j, core_ref[0], i, 0)),
                      pl.BlockSpec((None, tr, c), lambda j, i, core_ref: (j, i, 0))],
            out_specs=pl.BlockSpec((None, tr, c), lambda j, i, core_ref: (j, i, 0))),
        out_shape=jax.ShapeDtypeStruct((4, r, c), BF16), compiler_params=_params(),
    )(core, g4, t)


def _adam_math(w, g, m, v):
    m2 = ADAM_B1 * m + (1.0 - ADAM_B1) * g
    v2 = ADAM_B2 * v + (1.0 - ADAM_B2) * (g * g)
    m_hat = m2 / (1.0 - ADAM_B1 ** ADAM_STEP)
    v_hat = v2 / (1.0 - ADAM_B2 ** ADAM_STEP)
    delta = -ADAM_LR * (m_hat / (jnp.sqrt(v_hat) + ADAM_EPS) + ADAM_WD * w)
    return delta, m2, v2


def _adam_sharded(name, chip, w, m, v, grads, row_off=0):
    _, r, c = w.shape
    tr = _tile(r, 256, 8)
    tc = c if tr < r else _tile(c, 256, 128)
    boff = row_off // tr

    def body(chip_ref, w_ref, m_ref, v_ref, p0_ref, q0_ref, p1_ref, q1_ref, g_ref, d_ref, nm_ref, nv_ref):
        mine = chip_ref[0]

        def total(p_ref, q_ref):
            acc = None
            for j in range(4):
                part = jnp.where(mine == j, p_ref[...], q_ref[j]).astype(F32)
                acc = part if acc is None else acc + part
            return acc

        g = jnp.where(pl.program_id(0) == 0, total(p0_ref, q0_ref), total(p1_ref, q1_ref))
        delta, m2, v2 = _adam_math(w_ref[...], g, m_ref[...], v_ref[...])
        g_ref[...] = g
        d_ref[...] = delta
        nm_ref[...] = m2
        nv_ref[...] = v2

    def grad_specs(layer):
        at = lambda l, i, j: (jnp.where(l == layer, boff + i, boff), jnp.where(l == layer, j, 0))
        return [pl.BlockSpec((None, tr, tc), lambda l, i, j, chip_ref: (chip_ref[0],) + at(l, i, j)),
                pl.BlockSpec((4, tr, tc), lambda l, i, j, chip_ref: (0,) + at(l, i, j))]

    wspec = pl.BlockSpec((None, tr, tc), lambda l, i, j, chip_ref: (l, i, j))
    sds = jax.ShapeDtypeStruct(w.shape, F32)
    return pl.pallas_call(
        body, name=name,
        grid_spec=pltpu.PrefetchScalarGridSpec(
            num_scalar_prefetch=1, grid=(2, r // tr, c // tc),
            in_specs=[wspec, wspec, wspec] + grad_specs(0) + grad_specs(1), out_specs=[wspec] * 4),
        out_shape=[sds] * 4, compiler_params=_params(),
    )(chip, w, m, v, grads[0][0], grads[0][1], grads[1][0], grads[1][1])


def _adam_small(name, w, m, v, g):
    def body(w_ref, m_ref, v_ref, g_ref, d_ref, nm_ref, nv_ref):
        delta, m2, v2 = _adam_math(w_ref[...], g_ref[...], m_ref[...], v_ref[...])
        d_ref[...] = delta
        nm_ref[...] = m2
        nv_ref[...] = v2

    sds = jax.ShapeDtypeStruct(w.shape, F32)
    vm = pl.BlockSpec(memory_space=pltpu.VMEM)
    return pl.pallas_call(body, name=name, in_specs=[vm] * 4, out_specs=[vm] * 3, out_shape=[sds] * 3,
                          compiler_params=_params())(w, m, v, g)


GATE_END = GATE_COL + 2 * HEADS


def _merge_dw_in(dwm_t, dwc_t):
    full = jnp.concatenate([dwm_t[:GATE_END], dwc_t.reshape(3 * CONV_W, D_MODEL)], axis=0)
    return full.reshape(N_DEV, IN_SH, D_MODEL)


def _pack128(parts):
    flat = jnp.concatenate([p.reshape(-1) for p in parts])
    n = flat.shape[0]
    rows = -(-n // 1024) * 8
    return jnp.pad(flat, (0, rows * 128 - n)).reshape(rows, 128)


def _unpack128(packed, shapes):
    flat = packed.reshape(-1)
    out, at = [], 0
    for s in shapes:
        n = int(np.prod(s))
        out.append(flat[at:at + n].reshape(s))
        at += n
    return out


def kernel(x, meta_tokens, norm_mix_w, w_in, b_gates, conv_w, mlstm_norm_w, w_out, norm_ffn_w, w_gate, w_up, w_down, norm_final_w, loss_target, m_meta_tokens, m_norm_mix_w, m_w_in, m_b_gates, m_conv_w, m_mlstm_norm_w, m_w_out, m_norm_ffn_w, m_w_gate, m_w_up, m_w_down, m_norm_final_w, v_meta_tokens, v_norm_mix_w, v_w_in, v_b_gates, v_conv_w, v_mlstm_norm_w, v_w_out, v_norm_ffn_w, v_w_gate, v_w_up, v_w_down, v_norm_final_w):
    seq = x.shape[1]
    rows = TOK0 + seq
    me = 4 * lax.axis_index("x") + 2 * lax.axis_index("y") + lax.axis_index("c")
    meta_sh = meta_tokens.shape[1]
    conv_sh = conv_w.shape[2]

    small = jnp.concatenate(
        [meta_tokens, jnp.pad(conv_w.reshape(DEPTH * 3, conv_sh), ((0, 2), (0, meta_sh - conv_sh)))], axis=0)
    slots = _exchange_small("gather_small", small, reduce=False)
    meta_full = jnp.transpose(slots[:, :N_META, :], (1, 0, 2)).reshape(N_META, D_MODEL)
    conv_full = jnp.transpose(slots[:, N_META:N_META + DEPTH * 3, :conv_sh], (1, 0, 2)).reshape(DEPTH, 3, CONV_W)
    conv_rows = [jnp.pad(conv_full[l], ((0, 5), (0, 0))) for l in range(DEPTH)]

    w_in_t, m_w_in_t, v_w_in_t = (jnp.transpose(a, (0, 2, 1)) for a in (w_in, m_w_in, v_w_in))
    w_gate_t, m_w_gate_t, v_w_gate_t = (jnp.transpose(a, (0, 2, 1)) for a in (w_gate, m_w_gate, v_w_gate))
    w_up_t, m_w_up_t, v_w_up_t = (jnp.transpose(a, (0, 2, 1)) for a in (w_up, m_w_up, v_w_up))
    shards = []
    for l in range(DEPTH):
        shards += [w_in_t[l].astype(BF16), w_out[l].astype(BF16), w_gate_t[l].astype(BF16),
                   w_up_t[l].astype(BF16), w_down[l].astype(BF16)]
    per_layer = ("w_in", "w_out", "w_gate", "w_up", "w_down")
    gather_names = [f"{nm}_{l}" for l in range(DEPTH) for nm in per_layer]
    gather_state = {}

    def gather_start(i, after=None):
        buf = lax.dynamic_update_index_in_dim(lax.empty((N_DEV,) + shards[i].shape, BF16), shards[i], me, 0)
        arrs, sems, tok = _split_copy(f"gather_start_{gather_names[i]}", [buf], start=(_gather_first(1), 4),
                                      after=after)
        gather_state[i] = (arrs, sems)
        return tok

    def gather_pass(i, after):
        arrs, sems = gather_state[i]
        arrs, sems, tok = _split_copy(f"gather_pass_{gather_names[i]}", arrs, start=(_gather_pass(1), 3),
                                      wait=(_gather_first(1), sems), after=after)
        gather_state[i] = (arrs, sems)
        return tok

    def gathered(i, after):
        arrs, sems = gather_state[i]
        arrs, _, _ = _split_copy(f"gather_done_{gather_names[i]}", arrs, wait=(_gather_pass(1), sems), after=after)
        return arrs[0]

    bias = [jnp.pad(b_gates[l].reshape(1, 2 * HEADS), ((0, 0), (0, 128 - 2 * HEADS))) for l in range(DEPTH)]
    nmix = [norm_mix_w[l].reshape(1, D_MODEL) for l in range(DEPTH)]
    nffn = [norm_ffn_w[l].reshape(1, D_MODEL) for l in range(DEPTH)]
    nmls = [mlstm_norm_w[l].reshape(1, MLSTM_W) for l in range(DEPTH)]
    weights = [dict() for _ in range(DEPTH)]
    saved = [dict() for _ in range(DEPTH)]

    def layer_fwd(l, h, tok):
        w, s = weights[l], saved[l]
        first = len(per_layer) * l
        w["win_t"] = gathered(first, tok).reshape(D_IN, D_MODEL)
        w["wc_t"] = w["win_t"][GATE_END:].reshape(3, CONV_W, D_MODEL)
        s["h0"] = h
        s["hn"] = _rms_fwd(f"norm_mix_{l}", h, nmix[l])
        s["pm"] = _mm_nt(f"proj_mlstm_{l}", s["hn"], w["win_t"], F32, tn=640, tk=D_MODEL, n=PM_W)
        tok = gather_pass(first + 1, s["pm"])
        s["pc"] = _mm_nt_bcols(f"proj_conv_{l}", s["hn"], w["wc_t"], F32, dep=tok)
        tok = gather_pass(first + 2, s["pc"])
        hm, s["ht"], s["cs"], s["ns"], s["ms"] = _mlstm_fwd(f"mlstm_fwd_{l}", s["pm"], bias[l] + tok[:1], nmls[l])
        tok = gather_pass(first + 3, hm)
        hc = _conv_fwd(f"conv_fwd_{l}", s["pc"], conv_rows[l] + tok[0, 0])
        s["cat"] = jnp.concatenate([hm, hc], axis=1)
        w["wo"] = gathered(first + 1, s["cat"]).reshape(D_MODEL, D_MODEL)
        s["h1"] = _mm_nn(f"out_proj_{l}", s["cat"], w["wo"], F32, res=s["h0"])
        tok = gather_pass(first + 4, s["h1"])
        s["hf"] = _rms_fwd(f"norm_ffn_{l}", s["h1"], nffn[l])
        w["wg_t"] = gathered(first + 2, s["hf"]).reshape(D_FF, D_MODEL)
        w["wu_t"] = gathered(first + 3, s["hf"]).reshape(D_FF, D_MODEL)
        s["g"], s["u"], s["act"] = _ffn_in(f"ffn_in_{l}", s["hf"], w["wg_t"], w["wu_t"], dep=tok)
        tok = gather_pass(first + 5, s["act"]) if l + 1 < DEPTH else None
        w["wd"] = gathered(first + 4, s["act"]).reshape(D_FF, D_MODEL)
        return _mm_nn(f"ffn_out_{l}", s["act"], w["wd"], F32, res=s["h1"], tm=528, dep=tok)

    tok = None
    for i in range(len(shards)):
        tok = gather_start(i, after=tok)
    h = jnp.concatenate([jnp.zeros((PAD_FRONT, D_MODEL), F32), meta_full, x[0]], axis=0)
    h = layer_fwd(0, h, gather_pass(0, tok))
    h = layer_fwd(1, h, h)

    dh, dh_b, d_final, loss_part = _final_loss("final_loss", h, norm_final_w.reshape(1, D_MODEL), loss_target[0])

    core = lax.axis_index("c").astype(jnp.int32).reshape(1)
    chip = (2 * lax.axis_index("x") + lax.axis_index("y")).astype(jnp.int32).reshape(1)
    scatter_state = {}

    def scatter_begin(nm, grad):
        land = lax.empty((4,) + grad.shape[1:], BF16)
        arrs, sems, tok = _split_copy(f"grad_sibling_start_{nm}", [grad, land], start=(_scatter_sibling(1), 4))
        scatter_state[nm] = (arrs, sems)
        return tok

    def scatter_advance(nm, after):
        arrs, sems = scatter_state[nm]
        arrs, _, _ = _split_copy(f"grad_sibling_done_{nm}", arrs, wait=(_scatter_sibling(1), sems), after=after)
        part = _pair_sum(f"grad_pair_sum_{nm}", core, arrs[0], arrs[1])
        arrs, sems, tok = _split_copy(f"grad_chips_start_{nm}", [part, lax.empty(part.shape, BF16)],
                                      start=(_scatter_chips(1), 3))
        scatter_state[nm] = (arrs, sems)
        return tok

    def scattered(nm, after):
        arrs, sems = scatter_state[nm]
        arrs, _, _ = _split_copy(f"grad_chips_done_{nm}", arrs, wait=(_scatter_chips(1), sems), after=after)
        return arrs[0], arrs[1]

    d_mix, d_ffn, d_mls, d_bias, d_conv = ([None] * DEPTH for _ in range(5))

    def layer_bwd(l, dh, dh_b, tok):
        w, s = weights[l], saved[l]
        dg, du = _ffn_act_bwd(f"d_act_{l}", dh_b, w["wd"], s["g"], s["u"], dep=tok)
        dw_down = _mm_tn(f"dw_down_{l}", s["act"], dh_b, BF16, tm=512, tn=1024)
        tok = scatter_begin(f"w_down_{l}", dw_down.reshape(N_DEV, FF_SH, D_MODEL))
        dhf = _mm_nn(f"d_ffn_gate_{l}", dg, w["wg_t"], F32, tm=528, dep=tok)
        tok = scatter_advance(f"w_down_{l}", after=dhf)
        dhf = _mm_nn(f"d_ffn_up_{l}", du, w["wu_t"], F32, res=dhf, tm=528, dep=tok)
        dw_gate = _mm_tn(f"dw_gate_{l}", dg, s["hf"], BF16, tm=512, tn=1024)
        tok = scatter_begin(f"w_gate_{l}", dw_gate.reshape(N_DEV, FF_SH, D_MODEL))
        dw_up = _mm_tn(f"dw_up_{l}", du, s["hf"], BF16, tm=512, tn=1024, dep=tok)
        tok = scatter_begin(f"w_up_{l}", dw_up.reshape(N_DEV, FF_SH, D_MODEL))
        dh1, dh1_b, d_ffn[l] = _rms_bwd(f"norm_ffn_bwd_{l}", s["h1"], nffn[l] + tok[0, 0], dhf, dh)
        tok = scatter_advance(f"w_gate_{l}", after=dh1)
        dcat = _mm_nt(f"d_cat_{l}", dh1_b, w["wo"], F32, tk=D_MODEL, dep=tok)
        tok = scatter_advance(f"w_up_{l}", after=dcat)
        dw_out = _mm_tn(f"dw_out_{l}", s["cat"], dh1_b, BF16, tn=1024, dep=tok)
        tok = scatter_begin(f"w_out_{l}", dw_out.reshape(N_DEV, OUT_SH, D_MODEL))
        dpm, d_mls[l], d_bias[l] = _mlstm_bwd(f"mlstm_bwd_{l}", dcat, s["pm"], s["ht"], s["cs"], s["ns"],
                                               s["ms"], bias[l] + tok[:1], nmls[l])
        dpc, d_conv[l] = _conv_bwd(f"conv_bwd_{l}", dcat, s["pc"], conv_rows[l])
        tok = scatter_advance(f"w_out_{l}", after=dpc)
        dhn = _mm_nn_kt(f"d_norm_mlstm_{l}", dpm, w["win_t"], F32, dep=tok)
        dhn = _mm_nn_ksum(f"d_norm_conv_{l}", dpc, w["wc_t"], F32, res=dhn)
        dwm_t = _mm_tn(f"dw_mlstm_{l}", dpm, s["hn"], BF16, tm=640, tn=1024)
        dwc_t = _mm_tn_acols(f"dw_conv_{l}", dpc, s["hn"], BF16)
        tok = scatter_begin(f"w_in_{l}", _merge_dw_in(dwm_t, dwc_t))
        dh, dh_b, d_mix[l] = _rms_bwd(f"norm_mix_bwd_{l}", s["h0"], nmix[l] + tok[0, 0], dhn, dh1)
        return dh, dh_b, scatter_advance(f"w_in_{l}", after=dh)

    dh, dh_b, tok = layer_bwd(1, dh, dh_b, None)
    dh, dh_b, tok_tail = layer_bwd(0, dh, dh_b, tok)

    pq = {}
    after = tok_tail
    for l in reversed(range(DEPTH)):
        for nm in ("w_down", "w_gate", "w_up", "w_out", "w_in"):
            if (nm, l) != ("w_in", 0):
                pq[nm, l] = scattered(f"{nm}_{l}", after)
                after = pq[nm, l][0]
    untransposed = lambda outs: [jnp.transpose(o, (0, 2, 1)) for o in outs]
    g_out, d_out, nm_out, nv_out = _adam_sharded(
        "adam_w_out", chip, w_out, m_w_out, v_w_out, [pq["w_out", 0], pq["w_out", 1]])
    g_gate, d_gate, nm_gate, nv_gate = untransposed(_adam_sharded(
        "adam_w_gate", chip, w_gate_t, m_w_gate_t, v_w_gate_t, [pq["w_gate", 0], pq["w_gate", 1]]))
    g_up, d_up, nm_up, nv_up = untransposed(_adam_sharded(
        "adam_w_up", chip, w_up_t, m_w_up_t, v_w_up_t, [pq["w_up", 0], pq["w_up", 1]]))
    g_down, d_down, nm_down, nv_down = _adam_sharded(
        "adam_w_down", chip, w_down, m_w_down, v_w_down, [pq["w_down", 0], pq["w_down", 1]])
    pq["w_in", 0] = scattered("w_in_0", nv_down)
    g_in, d_in, nm_in, nv_in = untransposed(_adam_sharded(
        "adam_w_in", chip, w_in_t, m_w_in_t, v_w_in_t, [pq["w_in", 0], pq["w_in", 1]]))

    bg = jnp.concatenate([d_bias[l][0, :2 * HEADS] for l in range(DEPTH)])
    red_in = jnp.concatenate([
        dh[PAD_FRONT:TOK0], d_mix[0], d_mix[1], d_ffn[0], d_ffn[1], d_final,
        jnp.concatenate([d_mls[0], d_mls[1]], axis=1),
        jnp.stack([d_conv[l][:3] for l in range(DEPTH)]).reshape(3, 2 * CONV_W),
        jnp.pad(bg, (0, D_MODEL - bg.shape[0])).reshape(1, D_MODEL),
        jnp.pad(loss_part[:, :1], ((0, 0), (0, D_MODEL - 1))),
        jnp.zeros((5, D_MODEL), F32) + tok_tail[0, 0]], axis=0)
    red = _exchange_small("reduce_small", red_in, reduce=True)
    loss = red[26, 0]
    g_meta = lax.dynamic_slice_in_dim(red[:N_META], me * meta_sh, meta_sh, axis=1)
    g_mix, g_ffn, g_final = red[16:18], red[18:20], red[20]
    g_mls = red[21].reshape(DEPTH, MLSTM_W)
    g_conv = lax.dynamic_slice_in_dim(red[22:25].reshape(DEPTH, 3, CONV_W), me * conv_sh, conv_sh, axis=2)
    g_bias = red[25, :DEPTH * 2 * HEADS].reshape(DEPTH, 2 * HEADS)

    small_w = [meta_tokens, norm_mix_w, b_gates, conv_w, mlstm_norm_w, norm_ffn_w, norm_final_w]
    small_m = [m_meta_tokens, m_norm_mix_w, m_b_gates, m_conv_w, m_mlstm_norm_w, m_norm_ffn_w, m_norm_final_w]
    small_v = [v_meta_tokens, v_norm_mix_w, v_b_gates, v_conv_w, v_mlstm_norm_w, v_norm_ffn_w, v_norm_final_w]
    small_g = [g_meta, g_mix, g_bias, g_conv, g_mls, g_ffn, g_final]
    shapes = [a.shape for a in small_w]
    packed = _adam_small("adam_small", _pack128(small_w), _pack128(small_m), _pack128(small_v), _pack128(small_g))
    (d_meta, d_nmix, d_bg, d_cw, d_nmls, d_nffn, d_nfin), (nm_meta, nm_nmix, nm_bg, nm_cw, nm_nmls, nm_nffn, nm_nfin), \
        (nv_meta, nv_nmix, nv_bg, nv_cw, nv_nmls, nv_nffn, nv_nfin) = (_unpack128(p, shapes) for p in packed)

    grad_x = dh[TOK0:].reshape(1, seq, D_MODEL)
    return (loss, grad_x,
            g_meta, g_mix, g_in, g_bias, g_conv, g_mls, g_out, g_ffn, g_gate, g_up, g_down, g_final,
            d_meta, d_nmix, d_in, d_bg, d_cw, d_nmls, d_out, d_nffn, d_gate, d_up, d_down, d_nfin,
            nm_meta, nm_nmix, nm_in, nm_bg, nm_cw, nm_nmls, nm_out, nm_nffn, nm_gate, nm_up, nm_down, nm_nfin,
            nv_meta, nv_nmix, nv_in, nv_bg, nv_cw, nv_nmls, nv_out, nv_nffn, nv_gate, nv_up, nv_down, nv_nfin)
```

```python
import functools

import numpy as np
import jax
import jax.numpy as jnp
from jax import lax
from jax.experimental import pallas as pl
from jax.experimental.pallas import tpu as pltpu

F32 = jnp.float32
BF16 = jnp.bfloat16
MESH = pl.DeviceIdType.MESH

D_MODEL = 2048
DEPTH = 2
N_META = 16
MLSTM_W = 1024
CONV_W = 1024
HEADS = 4
DV = 256
DQK = 128
QK_W = 512
CHUNK = 64
PAD_FRONT = 48
TOK0 = PAD_FRONT + N_META
D_FF = 5632
N_DEV = 8
FF_SH = D_FF // N_DEV
D_IN = 6152
IN_SH = D_IN // N_DEV
OUT_SH = D_MODEL // N_DEV
GATE_COL = 3072
PM_W = GATE_COL + 128
GATE_CAP = 15.0
EPS = 1e-6
QSCALE = DQK ** -0.5

ADAM_LR = 0.001
ADAM_B1 = 0.9
ADAM_B2 = 0.999
ADAM_EPS = 1e-08
ADAM_WD = 0.01
ADAM_STEP = 10

V7X_VMEM_LIMIT = 50 * 1024 * 1024


def _params(**kw):
    return pltpu.CompilerParams(vmem_limit_bytes=V7X_VMEM_LIMIT, **kw)


def _tile(n, target, mult):
    best = None
    for t in range(mult, min(n, target) + 1, mult):
        if n % t == 0:
            best = t
    return best if best is not None else n


def _sigmoid(x):
    return 1.0 / (1.0 + jnp.exp(-x))


NN = ((1,), (0,))
NT = ((1,), (1,))
TN = ((0,), (0,))


def _matmul(name, a, b, out_shape, out_dtype, grid, a_bs, b_bs, o_bs, dims, nk, acc_shape=None,
            res=None, res_bs=None, dep=None):
    has_res = res is not None
    n_in = 2 + has_res + (dep is not None)

    def body(*refs):
        a_ref, b_ref = refs[0], refs[1]
        r_ref = refs[2] if has_res else None
        o_ref = refs[n_in]
        x = lax.dot_general(a_ref[...], b_ref[...], (dims, ((), ())), preferred_element_type=F32)
        if nk == 1:
            if has_res:
                x = x + r_ref[...]
            o_ref[...] = x.astype(o_ref.dtype)
            return
        acc = refs[n_in + 1]
        k = pl.program_id(len(grid) - 1)

        @pl.when(k == 0)
        def _():
            acc[...] = (x + r_ref[...]) if has_res else x

        @pl.when(k > 0)
        def _():
            acc[...] += x

        @pl.when(k == nk - 1)
        def _():
            o_ref[...] = acc[...].astype(o_ref.dtype)

    ins = [a, b] + ([res] if has_res else [])
    specs = [a_bs, b_bs] + ([res_bs] if has_res else [])
    if dep is not None:
        ins.append(dep)
        specs.append(pl.BlockSpec((8, 128), lambda *_: (0, 0)))
    scratch = [pltpu.VMEM(acc_shape, F32)] if nk > 1 else []
    return pl.pallas_call(
        body, name=name, grid=grid, in_specs=specs, out_specs=o_bs,
        out_shape=jax.ShapeDtypeStruct(out_shape, out_dtype), scratch_shapes=scratch,
        compiler_params=_params(),
    )(*ins)


def _mm_nn(name, a, b, out_dtype, res=None, tm=1056, tn=512, dep=None):
    r, k = a.shape
    n = b.shape[1]
    tm, tn = _tile(r, tm, 8), _tile(n, tn, 128)
    return _matmul(name, a, b, (r, n), out_dtype, (r // tm, n // tn, 1),
                   pl.BlockSpec((tm, k), lambda i, j, s: (i, 0)),
                   pl.BlockSpec((k, tn), lambda i, j, s: (0, j)),
                   pl.BlockSpec((tm, tn), lambda i, j, s: (i, j)), NN, 1,
                   res=res, res_bs=pl.BlockSpec((tm, tn), lambda i, j, s: (i, j)), dep=dep)


def _mm_nn_kt(name, a, b, out_dtype, tm=1056, tn=1024, tk=640, dep=None):
    r, k = a.shape
    n = b.shape[1]
    tm, tn, tk = _tile(r, tm, 8), _tile(n, tn, 128), _tile(k, tk, 128)
    nk = k // tk
    return _matmul(name, a, b, (r, n), out_dtype, (r // tm, n // tn, nk),
                   pl.BlockSpec((tm, tk), lambda i, j, s: (i, s)),
                   pl.BlockSpec((tk, tn), lambda i, j, s: (s, j)),
                   pl.BlockSpec((tm, tn), lambda i, j, s: (i, j)), NN, nk, acc_shape=(tm, tn), dep=dep)


def _mm_nn_ksum(name, a3, b3, out_dtype, res=None, tm=1056, tn=1024, dep=None):
    e, r, kb = a3.shape
    n = b3.shape[2]
    tm, tn = _tile(r, tm, 8), _tile(n, tn, 128)
    return _matmul(name, a3, b3, (r, n), out_dtype, (r // tm, n // tn, e),
                   pl.BlockSpec((None, tm, kb), lambda i, j, s: (s, i, 0)),
                   pl.BlockSpec((None, kb, tn), lambda i, j, s: (s, 0, j)),
                   pl.BlockSpec((tm, tn), lambda i, j, s: (i, j)), NN, e, acc_shape=(tm, tn),
                   res=res, res_bs=pl.BlockSpec((tm, tn), lambda i, j, s: (i, j)), dep=dep)


def _mm_nt(name, a, b, out_dtype, res=None, tm=1056, tn=512, tk=640, n=None, dep=None):
    r, k = a.shape
    n = b.shape[0] if n is None else n
    tm, tn, tk = _tile(r, tm, 8), _tile(n, tn, 128), _tile(k, tk, 128)
    nk = k // tk
    return _matmul(name, a, b, (r, n), out_dtype, (r // tm, n // tn, nk),
                   pl.BlockSpec((tm, tk), lambda i, j, s: (i, s)),
                   pl.BlockSpec((tn, tk), lambda i, j, s: (j, s)),
                   pl.BlockSpec((tm, tn), lambda i, j, s: (i, j)), NT, nk, acc_shape=(tm, tn),
                   res=res, res_bs=pl.BlockSpec((tm, tn), lambda i, j, s: (i, j)), dep=dep)


def _mm_nt_bcols(name, a, b3, out_dtype, tm=1056, dep=None):
    r, k = a.shape
    e, n, _ = b3.shape
    tm = _tile(r, tm, 8)
    return _matmul(name, a, b3, (e, r, n), out_dtype, (r // tm, e, 1),
                   pl.BlockSpec((tm, k), lambda i, g, s: (i, 0)),
                   pl.BlockSpec((None, n, k), lambda i, g, s: (g, 0, 0)),
                   pl.BlockSpec((None, tm, n), lambda i, g, s: (g, i, 0)), NT, 1, dep=dep)


def _mm_tn(name, a, b, out_dtype, tm=1024, tn=640, dep=None):
    r, m = a.shape
    n = b.shape[1]
    tm, tn = _tile(m, tm, 128), _tile(n, tn, 128)
    return _matmul(name, a, b, (m, n), out_dtype, (m // tm, n // tn, 1),
                   pl.BlockSpec((r, tm), lambda i, j, s: (0, i)),
                   pl.BlockSpec((r, tn), lambda i, j, s: (0, j)),
                   pl.BlockSpec((tm, tn), lambda i, j, s: (i, j)), TN, 1, dep=dep)


def _mm_tn_acols(name, a3, b, out_dtype, tn=1024, dep=None):
    e, r, m = a3.shape
    n = b.shape[1]
    tn = _tile(n, tn, 128)
    return _matmul(name, a3, b, (e, m, n), out_dtype, (n // tn, e, 1),
                   pl.BlockSpec((None, r, m), lambda j, g, s: (g, 0, 0)),
                   pl.BlockSpec((r, tn), lambda j, g, s: (0, j)),
                   pl.BlockSpec((None, m, tn), lambda j, g, s: (g, 0, j)), TN, 1, dep=dep)


def _rms_fwd(name, h, w):
    r, d = h.shape
    tr = _tile(r, 264, 8)

    def body(h_ref, w_ref, o_ref):
        x = h_ref[...]
        rs = lax.rsqrt(jnp.mean(x * x, axis=1, keepdims=True) + EPS)
        o_ref[...] = (x * rs * w_ref[...]).astype(BF16)

    return pl.pallas_call(
        body, name=name, grid=(r // tr,),
        in_specs=[pl.BlockSpec((tr, d), lambda i: (i, 0)), pl.BlockSpec((1, d), lambda i: (0, 0))],
        out_specs=pl.BlockSpec((tr, d), lambda i: (i, 0)),
        out_shape=jax.ShapeDtypeStruct((r, d), BF16), compiler_params=_params(),
    )(h, w)


def _rms_bwd(name, x, w, dy, dres):
    r, d = x.shape
    tr = _tile(r, 264, 8)

    def body(x_ref, w_ref, dy_ref, dr_ref, dx_ref, dxb_ref, dw_ref):
        xv = x_ref[...]
        g = dy_ref[...]
        rs = lax.rsqrt(jnp.mean(xv * xv, axis=1, keepdims=True) + EPS)
        wg = g * w_ref[...]
        dx = rs * wg - xv * (rs * rs * rs) * jnp.mean(xv * wg, axis=1, keepdims=True) + dr_ref[...]
        dx_ref[...] = dx
        dxb_ref[...] = dx.astype(BF16)
        part = jnp.sum(g * xv * rs, axis=0, keepdims=True)

        @pl.when(pl.program_id(0) == 0)
        def _():
            dw_ref[...] = part

        @pl.when(pl.program_id(0) > 0)
        def _():
            dw_ref[...] += part

    row = pl.BlockSpec((tr, d), lambda i: (i, 0))
    vec = pl.BlockSpec((1, d), lambda i: (0, 0))
    return pl.pallas_call(
        body, name=name, grid=(r // tr,), in_specs=[row, vec, row, row], out_specs=[row, row, vec],
        out_shape=[jax.ShapeDtypeStruct((r, d), F32), jax.ShapeDtypeStruct((r, d), BF16),
                   jax.ShapeDtypeStruct((1, d), F32)],
        compiler_params=_params(),
    )(x, w, dy, dres)


def _final_loss(name, h, w, target):
    r, d = h.shape
    nb = r // CHUNK

    def body(h_ref, w_ref, t_ref, dh_ref, dhb_ref, dw_ref, ls_ref):
        i = pl.program_id(0)

        @pl.when(i == 0)
        def _():
            dh_ref[...] = jnp.zeros_like(dh_ref)
            dhb_ref[...] = jnp.zeros_like(dhb_ref)
            dw_ref[...] = jnp.zeros_like(dw_ref)
            ls_ref[...] = jnp.zeros_like(ls_ref)

        @pl.when(i > 0)
        def _():
            xv = h_ref[...]
            wv = w_ref[...]
            rs = lax.rsqrt(jnp.mean(xv * xv, axis=1, keepdims=True) + EPS)
            err = xv * rs * wv - t_ref[...]
            sq = jnp.sum(jnp.sum(err * err, axis=1, keepdims=True), axis=0, keepdims=True)
            ls_ref[...] += jnp.broadcast_to(sq * (0.5 / d), ls_ref.shape)
            g = err * (1.0 / d)
            wg = g * wv
            dx = rs * wg - xv * (rs * rs * rs) * jnp.mean(xv * wg, axis=1, keepdims=True)
            dh_ref[...] = dx
            dhb_ref[...] = dx.astype(BF16)
            dw_ref[...] += jnp.sum(g * xv * rs, axis=0, keepdims=True)

    row = pl.BlockSpec((CHUNK, d), lambda i: (i, 0))
    vec = pl.BlockSpec((1, d), lambda i: (0, 0))
    return pl.pallas_call(
        body, name=name, grid=(nb,),
        in_specs=[row, vec, pl.BlockSpec((CHUNK, d), lambda i: (jnp.maximum(i - 1, 0), 0))],
        out_specs=[row, row, vec, pl.BlockSpec((1, 128), lambda i: (0, 0))],
        out_shape=[jax.ShapeDtypeStruct((r, d), F32), jax.ShapeDtypeStruct((r, d), BF16),
                   jax.ShapeDtypeStruct((1, d), F32), jax.ShapeDtypeStruct((1, 128), F32)],
        compiler_params=_params(),
    )(h, w, target)


def _ffn_in(name, hf, wg_t, wu_t, dep=None, tm=1056, tn=512):
    r, d = hf.shape
    f = wg_t.shape[0]
    tm, tn = _tile(r, tm, 8), _tile(f, tn, 128)

    def body(h_ref, wg_ref, wu_ref, *rest):
        g_ref, u_ref, a_ref = rest[-3:]
        x = h_ref[...]
        g = lax.dot_general(x, wg_ref[...], (NT, ((), ())), preferred_element_type=F32)
        u = lax.dot_general(x, wu_ref[...], (NT, ((), ())), preferred_element_type=F32)
        g_ref[...] = g.astype(BF16)
        u_ref[...] = u.astype(BF16)
        a_ref[...] = (g * _sigmoid(g) * u).astype(BF16)

    wspec = pl.BlockSpec((tn, d), lambda i, j: (j, 0))
    ospec = pl.BlockSpec((tm, tn), lambda i, j: (i, j))
    ins, specs = [hf, wg_t, wu_t], [pl.BlockSpec((tm, d), lambda i, j: (i, 0)), wspec, wspec]
    if dep is not None:
        ins.append(dep)
        specs.append(pl.BlockSpec((8, 128), lambda *_: (0, 0)))
    return pl.pallas_call(
        body, name=name, grid=(r // tm, f // tn), in_specs=specs, out_specs=[ospec] * 3,
        out_shape=[jax.ShapeDtypeStruct((r, f), BF16)] * 3, compiler_params=_params(),
    )(*ins)


def _ffn_act_bwd(name, dh, wd, g, u, dep=None, tm=1056, tn=512):
    r, d = dh.shape
    f = wd.shape[0]
    tm, tn = _tile(r, tm, 8), _tile(f, tn, 128)

    def body(dh_ref, wd_ref, g_ref, u_ref, *rest):
        dg_ref, du_ref = rest[-2:]
        da = lax.dot_general(dh_ref[...], wd_ref[...], (NT, ((), ())), preferred_element_type=F32)
        gv = g_ref[...].astype(F32)
        s = _sigmoid(gv)
        dg_ref[...] = (da * u_ref[...].astype(F32) * (s + gv * s * (1.0 - s))).astype(BF16)
        du_ref[...] = (da * gv * s).astype(BF16)

    tile = pl.BlockSpec((tm, tn), lambda i, j: (i, j))
    ins = [dh, wd, g, u]
    specs = [pl.BlockSpec((tm, d), lambda i, j: (i, 0)), pl.BlockSpec((tn, d), lambda i, j: (j, 0)), tile, tile]
    if dep is not None:
        ins.append(dep)
        specs.append(pl.BlockSpec((8, 128), lambda *_: (0, 0)))
    return pl.pallas_call(
        body, name=name, grid=(r // tm, f // tn), in_specs=specs, out_specs=[tile] * 2,
        out_shape=[jax.ShapeDtypeStruct((r, f), BF16)] * 2, compiler_params=_params(),
    )(*ins)


def _shift_down(a, k):
    row = lax.broadcasted_iota(jnp.int32, a.shape, 0)
    return jnp.where(row >= k, pltpu.roll(a, k, 0), 0.0)


def _shift_up(a, k):
    n = a.shape[0]
    row = lax.broadcasted_iota(jnp.int32, a.shape, 0)
    return jnp.where(row < n - k, pltpu.roll(a, n - k, 0), 0.0)


def _conv_fwd(name, pc, cw):
    _, r, w = pc.shape

    def body(pc_ref, cw_ref, o_ref):
        a = pc_ref[2] * pc_ref[0]
        cwv = cw_ref[...]
        conv = _shift_down(a, 2) * cwv[0:1] + _shift_down(a, 1) * cwv[1:2] + a * cwv[2:3]
        o_ref[...] = (pc_ref[1] * conv).astype(BF16)

    return pl.pallas_call(
        body, name=name, grid=(w // 128,),
        in_specs=[pl.BlockSpec((3, r, 128), lambda j: (0, 0, j)), pl.BlockSpec((8, 128), lambda j: (0, j))],
        out_specs=pl.BlockSpec((r, 128), lambda j: (0, j)),
        out_shape=jax.ShapeDtypeStruct((r, w), BF16), compiler_params=_params(),
    )(pc, cw)


def _conv_bwd(name, dcat, pc, cw):
    _, r, w = pc.shape
    nblk = w // 128

    def body(dy_ref, pc_ref, cw_ref, dpc_ref, dcw_ref):
        u, gb, gc = pc_ref[0], pc_ref[1], pc_ref[2]
        cwv = cw_ref[...]
        dy = dy_ref[...]
        a = gc * u
        a1, a2 = _shift_down(a, 1), _shift_down(a, 2)
        conv = a2 * cwv[0:1] + a1 * cwv[1:2] + a * cwv[2:3]
        dconv = dy * gb
        da = dconv * cwv[2:3] + _shift_up(dconv, 1) * cwv[1:2] + _shift_up(dconv, 2) * cwv[0:1]
        dpc_ref[0] = (da * gc).astype(BF16)
        dpc_ref[1] = (dy * conv).astype(BF16)
        dpc_ref[2] = (da * u).astype(BF16)
        row = lax.broadcasted_iota(jnp.int32, (8, 128), 0)
        dw0 = jnp.sum(dconv * a2, axis=0, keepdims=True)
        dw1 = jnp.sum(dconv * a1, axis=0, keepdims=True)
        dw2 = jnp.sum(dconv * a, axis=0, keepdims=True)
        dcw_ref[...] = jnp.where(row == 0, dw0, jnp.where(row == 1, dw1, jnp.where(row == 2, dw2, 0.0)))

    return pl.pallas_call(
        body, name=name, grid=(nblk,),
        in_specs=[pl.BlockSpec((r, 128), lambda j: (0, nblk + j)),
                  pl.BlockSpec((3, r, 128), lambda j: (0, 0, j)), pl.BlockSpec((8, 128), lambda j: (0, j))],
        out_specs=[pl.BlockSpec((3, r, 128), lambda j: (0, 0, j)), pl.BlockSpec((8, 128), lambda j: (0, j))],
        out_shape=[jax.ShapeDtypeStruct((3, r, w), BF16), jax.ShapeDtypeStruct((8, w), F32)],
        compiler_params=_params(),
    )(dcat, pc, cw)


def _dot(a, b, dims):
    return lax.dot_general(a, b, (dims, ((), ())), preferred_element_type=F32)


def _col_to_row(xc, eye):
    return jnp.sum(jnp.where(eye, xc, 0.0), axis=0, keepdims=True)


def _row_to_col(xr, eye):
    return jnp.sum(jnp.where(eye, xr, 0.0), axis=1, keepdims=True)


def _gate_tiles(graw, bias, row0):
    th = jnp.tanh((graw + bias) / GATE_CAP)
    z = GATE_CAP * th
    row = lax.broadcasted_iota(jnp.int32, graw.shape, 0) + row0
    real = row >= PAD_FRONT
    li = jnp.where(real, z, -jnp.inf)
    lf = jnp.where(real, jnp.minimum(z, 0.0) - jnp.log(1.0 + jnp.exp(-jnp.abs(z))), 0.0)
    return th, z, li, lf, real


def _chunk_common(pm, h, li, lf, cst, nst, mst, tril, eye):
    kraw = pm[:, QK_W + h * DQK:QK_W + (h + 1) * DQK]
    q = (pm[:, h * DQK:(h + 1) * DQK] * QSCALE).astype(BF16)
    k = kraw.astype(BF16)
    v = pm[:, 2 * QK_W + h * DV:2 * QK_W + (h + 1) * DV].astype(BF16)
    li_c = li[:, h:h + 1]
    lf_c = lf[:, HEADS + h:HEADS + h + 1]
    li_r = _col_to_row(li_c, eye)
    lf_r = _col_to_row(lf_c, eye)
    b_c = jnp.sum(jnp.where(tril, lf_r, 0.0), axis=1, keepdims=True)
    b_r = _col_to_row(b_c, eye)
    dmat = jnp.where(tril, b_c - b_r + li_r, -jnp.inf)
    inter = b_c + mst
    mt = jnp.maximum(inter, jnp.max(dmat, axis=1, keepdims=True))
    w_inter = jnp.exp(inter - mt)
    p = jnp.exp(dmat - mt)
    s = _dot(q, k, NT) * p
    cb = cst.astype(BF16)
    nb = nst.astype(BF16).astype(F32)
    qc = _dot(q, cb, NN)
    qn = jnp.sum(q.astype(F32) * nb, axis=1, keepdims=True)
    den = w_inter * qn + jnp.sum(s, axis=1, keepdims=True)
    dn = jnp.maximum(jnp.abs(den), jnp.exp(-mt))
    b_end = b_c[CHUNK - 1:CHUNK, :]
    decay = b_end - b_c + li_c
    m_new = jnp.maximum(b_end + mst, jnp.max(decay, axis=0, keepdims=True))
    w_old = jnp.exp(b_end + mst - m_new)
    w_in = jnp.exp(decay - m_new)
    kw = (w_in * kraw).astype(BF16)
    return dict(q=q, k=k, v=v, kraw=kraw, mt=mt, w_inter=w_inter, p=p, s=s, cb=cb, nb=nb, qc=qc, qn=qn,
                den=den, dn=dn, m_new=m_new, w_old=w_old, w_in=w_in, kw=kw)


def _mlstm_fwd(name, pm, bias, nw):
    r = pm.shape[0]
    nc = r // CHUNK

    def body(pm_ref, b_ref, nw_ref, hm_ref, ht_ref, cs_ref, ns_ref, ms_ref, c_scr, n_scr, m_scr):
        ci = pl.program_id(0)

        @pl.when(ci == 0)
        def _():
            c_scr[...] = jnp.zeros_like(c_scr)
            n_scr[...] = jnp.zeros_like(n_scr)
            m_scr[...] = jnp.zeros_like(m_scr)

        pmv = pm_ref[...]
        rr = lax.broadcasted_iota(jnp.int32, (CHUNK, CHUNK), 0)
        cc = lax.broadcasted_iota(jnp.int32, (CHUNK, CHUNK), 1)
        tril, eye = cc <= rr, cc == rr
        _, _, li, lf, _ = _gate_tiles(pmv[:, GATE_COL:GATE_COL + 128], b_ref[...], ci * CHUNK)
        nwv = nw_ref[...]
        for h in range(HEADS):
            cst, nst, mst = c_scr[h], n_scr[h], m_scr[h]
            cs_ref[h] = cst
            ns_ref[h] = nst
            ms_ref[h] = mst
            f = _chunk_common(pmv, h, li, lf, cst, nst, mst, tril, eye)
            num = f["w_inter"] * f["qc"] + _dot(f["s"].astype(BF16), f["v"], NN)
            hh = num / f["dn"]
            c_scr[h] = f["w_old"] * cst + _dot(f["kw"], f["v"], TN)
            n_scr[h] = f["w_old"] * nst + jnp.sum(
                f["w_in"].astype(BF16).astype(F32) * f["k"].astype(F32), axis=0, keepdims=True)
            m_scr[h] = f["m_new"]
            sl = slice(h * DV, (h + 1) * DV)
            rs = lax.rsqrt(jnp.mean(hh * hh, axis=1, keepdims=True) + EPS)
            og = pmv[:, 2 * QK_W + MLSTM_W + h * DV:2 * QK_W + MLSTM_W + (h + 1) * DV]
            ht_ref[:, sl] = hh
            hm_ref[:, sl] = (_sigmoid(og) * (hh * rs * nwv[:, sl])).astype(BF16)

    return pl.pallas_call(
        body, name=name, grid=(nc,),
        in_specs=[pl.BlockSpec((CHUNK, PM_W), lambda i: (i, 0)), pl.BlockSpec((1, 128), lambda i: (0, 0)),
                  pl.BlockSpec((1, MLSTM_W), lambda i: (0, 0))],
        out_specs=[pl.BlockSpec((CHUNK, MLSTM_W), lambda i: (i, 0)),
                   pl.BlockSpec((CHUNK, MLSTM_W), lambda i: (i, 0)),
                   pl.BlockSpec((None, HEADS, DQK, DV), lambda i: (i, 0, 0, 0)),
                   pl.BlockSpec((None, HEADS, 1, DQK), lambda i: (i, 0, 0, 0)),
                   pl.BlockSpec((None, HEADS, 1, 1), lambda i: (i, 0, 0, 0))],
        out_shape=[jax.ShapeDtypeStruct((r, MLSTM_W), BF16), jax.ShapeDtypeStruct((r, MLSTM_W), F32),
                   jax.ShapeDtypeStruct((nc, HEADS, DQK, DV), F32),
                   jax.ShapeDtypeStruct((nc, HEADS, 1, DQK), F32),
                   jax.ShapeDtypeStruct((nc, HEADS, 1, 1), F32)],
        scratch_shapes=[pltpu.VMEM((HEADS, DQK, DV), F32), pltpu.VMEM((HEADS, 1, DQK), F32),
                        pltpu.VMEM((HEADS, 1, 1), F32)],
        compiler_params=_params(),
    )(pm, bias, nw)


def _mlstm_bwd(name, dcat, pm, ht, cs, ns, ms, bias, nw):
    r = pm.shape[0]
    nc = r // CHUNK

    def body(dy_ref, pm_ref, ht_ref, cs_ref, ns_ref, ms_ref, b_ref, nw_ref, dpm_ref, dnw_ref, db_ref,
             dc_scr, dn_scr):
        step = pl.program_id(0)
        ci = nc - 1 - step

        @pl.when(step == 0)
        def _():
            dc_scr[...] = jnp.zeros_like(dc_scr)
            dn_scr[...] = jnp.zeros_like(dn_scr)
            dnw_ref[...] = jnp.zeros_like(dnw_ref)
            db_ref[...] = jnp.zeros_like(db_ref)

        pmv = pm_ref[...]
        rr = lax.broadcasted_iota(jnp.int32, (CHUNK, CHUNK), 0)
        cc = lax.broadcasted_iota(jnp.int32, (CHUNK, CHUNK), 1)
        tril, eye, triu = cc <= rr, cc == rr, cc >= rr
        th, z, li, lf, real = _gate_tiles(pmv[:, GATE_COL:GATE_COL + 128], b_ref[...], ci * CHUNK)
        lane = lax.broadcasted_iota(jnp.int32, (CHUNK, 128), 1)
        rowid = lax.broadcasted_iota(jnp.int32, (CHUNK, 1), 0)
        nwv = nw_ref[...]
        dgt = jnp.zeros((CHUNK, 128), F32)
        for h in range(HEADS):
            cst, nst, mst = cs_ref[h], ns_ref[h], ms_ref[h]
            f = _chunk_common(pmv, h, li, lf, cst, nst, mst, tril, eye)
            q, k, v, s, p = f["q"], f["k"], f["v"], f["s"], f["p"]
            w_inter, w_in, w_old, dn = f["w_inter"], f["w_in"], f["w_old"], f["dn"]
            sl = slice(h * DV, (h + 1) * DV)
            osl = slice(2 * QK_W + MLSTM_W + h * DV, 2 * QK_W + MLSTM_W + (h + 1) * DV)
            hh = ht_ref[:, sl]
            y = dy_ref[:, sl]
            sg = _sigmoid(pmv[:, osl])
            rs = lax.rsqrt(jnp.mean(hh * hh, axis=1, keepdims=True) + EPS)
            nwh = nwv[:, sl]
            dpm_ref[:, osl] = (y * (hh * rs * nwh) * sg * (1.0 - sg)).astype(BF16)
            dhn = y * sg
            dnw_ref[:, sl] += jnp.sum(dhn * hh * rs, axis=0, keepdims=True)
            wd = dhn * nwh
            dhh = rs * wd - hh * (rs * rs * rs) * jnp.mean(hh * wd, axis=1, keepdims=True)
            dnum = dhh / dn
            dd = -jnp.sum(dhh * hh, axis=1, keepdims=True) / dn
            dden = jnp.where(jnp.abs(f["den"]) > jnp.exp(-f["mt"]), dd * jnp.sign(f["den"]), 0.0)
            dnum_b = dnum.astype(BF16)
            wdn = (w_inter * dnum).astype(BF16)
            wid = (w_inter * dden).astype(BF16).astype(F32)
            ds = _dot(dnum_b, v, NT) + dden
            dsp = (ds * p).astype(BF16)
            dq = _dot(dsp, k, NN) + _dot(wdn, f["cb"], NT) + wid * f["nb"]
            dk = _dot(dsp, q, TN)
            dv = _dot(s.astype(BF16), dnum_b, TN)
            g = ds * s
            g_col = _row_to_col(jnp.sum(g, axis=0, keepdims=True), eye)
            db = jnp.sum(g, axis=1, keepdims=True) - g_col
            dli = g_col
            db = db + (jnp.sum(dnum * f["qc"], axis=1, keepdims=True) + dden * f["qn"]) * w_inter
            dcn, dnn = dc_scr[h], dn_scr[h]
            dcnb = dcn.astype(BF16)
            dnnb = dnn.astype(BF16).astype(F32)
            dkw = _dot(v, dcnb, NT) + dnnb
            dk = dk + w_in * dkw
            dv = dv + _dot(f["kw"], dcnb, NN)
            ddecay = jnp.sum(dkw * f["kraw"], axis=1, keepdims=True) * w_in
            dw_old = (jnp.sum(jnp.sum(dcn * cst, axis=1, keepdims=True), axis=0, keepdims=True)
                      + jnp.sum(dnn * nst, axis=1, keepdims=True))
            db_end = dw_old * w_old + jnp.sum(ddecay, axis=0, keepdims=True)
            db = db - ddecay + jnp.where(rowid == CHUNK - 1, db_end, 0.0)
            dli = dli + ddecay
            dc_scr[h] = w_old * dcn + _dot(q, wdn, TN)
            dn_scr[h] = w_old * dnn + jnp.sum(wid * q.astype(F32), axis=0, keepdims=True)
            dlf = jnp.sum(jnp.where(triu, _col_to_row(db, eye), 0.0), axis=1, keepdims=True)
            dgt = dgt + jnp.where(lane == h, dli, 0.0) + jnp.where(lane == HEADS + h, dlf, 0.0)
            dpm_ref[:, h * DQK:(h + 1) * DQK] = (dq * QSCALE).astype(BF16)
            dpm_ref[:, QK_W + h * DQK:QK_W + (h + 1) * DQK] = dk.astype(BF16)
            dpm_ref[:, 2 * QK_W + h * DV:2 * QK_W + (h + 1) * DV] = dv.astype(BF16)
        dact = jnp.where(lane < HEADS, 1.0, 1.0 - _sigmoid(z)) * (1.0 - th * th)
        dgraw = jnp.where(real & (lane < 2 * HEADS), dgt * dact, 0.0)
        dpm_ref[:, GATE_COL:GATE_COL + 128] = dgraw.astype(BF16)
        db_ref[...] += jnp.sum(dgraw, axis=0, keepdims=True)

    rev = lambda i: (nc - 1 - i, 0)
    rev4 = lambda i: (nc - 1 - i, 0, 0, 0)
    return pl.pallas_call(
        body, name=name, grid=(nc,),
        in_specs=[pl.BlockSpec((CHUNK, MLSTM_W), rev), pl.BlockSpec((CHUNK, PM_W), rev),
                  pl.BlockSpec((CHUNK, MLSTM_W), rev),
                  pl.BlockSpec((None, HEADS, DQK, DV), rev4), pl.BlockSpec((None, HEADS, 1, DQK), rev4),
                  pl.BlockSpec((None, HEADS, 1, 1), rev4),
                  pl.BlockSpec((1, 128), lambda i: (0, 0)), pl.BlockSpec((1, MLSTM_W), lambda i: (0, 0))],
        out_specs=[pl.BlockSpec((CHUNK, PM_W), rev), pl.BlockSpec((1, MLSTM_W), lambda i: (0, 0)),
                   pl.BlockSpec((1, 128), lambda i: (0, 0))],
        out_shape=[jax.ShapeDtypeStruct((r, PM_W), BF16), jax.ShapeDtypeStruct((1, MLSTM_W), F32),
                   jax.ShapeDtypeStruct((1, 128), F32)],
        scratch_shapes=[pltpu.VMEM((HEADS, DQK, DV), F32), pltpu.VMEM((HEADS, 1, DQK), F32)],
        compiler_params=_params(),
    )(dcat, pm, ht, cs, ns, ms, bias, nw)


def _my_place():
    return lax.axis_index("x"), lax.axis_index("y"), lax.axis_index("c")


def _flip(v, bit):
    return 1 - v if bit else v


def _exchange_small(name, blk, reduce):
    r, c = blk.shape

    def body(x_ref, o_ref, *rest):
        slots = rest[0] if reduce else o_ref
        send_sems, recv_sems = rest[-2], rest[-1]
        x, y, cc = _my_place()
        me = 4 * x + 2 * y + cc
        slots[me] = x_ref[...]
        copies = []
        for k in range(1, N_DEV):
            peer = (_flip(x, k & 4), _flip(y, k & 2), _flip(cc, k & 1))
            cp = pltpu.make_async_remote_copy(
                src_ref=x_ref, dst_ref=slots.at[me], send_sem=send_sems.at[k - 1],
                recv_sem=recv_sems.at[k - 1], device_id=peer, device_id_type=MESH)
            cp.start()
            copies.append(cp)
        for cp in copies:
            cp.wait()
        if reduce:
            acc = slots[0]
            for d in range(1, N_DEV):
                acc = acc + slots[d]
            o_ref[...] = acc

    scratch = ([pltpu.VMEM((N_DEV, r, c), F32)] if reduce else []) + [
        pltpu.SemaphoreType.DMA((N_DEV - 1,)), pltpu.SemaphoreType.DMA((N_DEV - 1,))]
    return pl.pallas_call(
        body, name=name,
        out_shape=jax.ShapeDtypeStruct((r, c) if reduce else (N_DEV, r, c), F32),
        in_specs=[pl.BlockSpec(memory_space=pltpu.VMEM)], out_specs=pl.BlockSpec(memory_space=pltpu.VMEM),
        scratch_shapes=scratch, compiler_params=_params(),
    )(blk)


HBM_SPEC = pl.BlockSpec(memory_space=pltpu.HBM)
SEM_SPEC = pl.BlockSpec(memory_space=pltpu.SEMAPHORE)
ANY_SPEC = pl.BlockSpec(memory_space=pl.ANY)
DATAFLOW = pltpu.SideEffectType.DATAFLOW_SIDE_EFFECTING


def _split_copy(name, arrays, start=None, wait=None, after=None):
    n = len(arrays)
    n_wait = 2 if wait else 0
    n_after = 0 if after is None else 1
    n_new = 2 if start else 0

    def body(*refs):
        ins = refs[:n]
        if wait:
            for cp in wait[0](ins, refs[n], refs[n + 1]):
                cp.wait_send()
                cp.wait_recv()
        if start:
            at = n + n_wait + n_after
            for cp in start[0](ins, refs[at], refs[at + 1]):
                cp.start()
            token = refs[at + 2 + n]
            token[...] = jnp.zeros_like(token)

    operands = [pltpu.with_memory_space_constraint(a, pltpu.HBM) for a in arrays]
    in_specs = [HBM_SPEC] * n
    if wait:
        operands += list(wait[1])
        in_specs += [SEM_SPEC, SEM_SPEC]
    if after is not None:
        operands.append(after)
        in_specs.append(ANY_SPEC)
    out_shape, out_specs = [], []
    if start:
        out_shape += [pltpu.SemaphoreType.DMA((start[1],)), pltpu.SemaphoreType.DMA((start[1],))]
        out_specs += [SEM_SPEC, SEM_SPEC]
    out_shape += [pltpu.HBM(a.shape, a.dtype) for a in arrays]
    out_specs += [HBM_SPEC] * n
    if start:
        out_shape.append(jax.ShapeDtypeStruct((8, 128), F32))
        out_specs.append(pl.BlockSpec(memory_space=pltpu.VMEM))
    outs = pl.pallas_call(
        body, name=name, in_specs=in_specs, out_specs=out_specs, out_shape=out_shape,
        input_output_aliases={i: n_new + i for i in range(n)},
        compiler_params=pltpu.CompilerParams(has_side_effects=DATAFLOW),
    )(*operands)
    thru = list(outs[n_new:n_new + n])
    return thru, (tuple(outs[:2]) if start else None), (outs[n_new + n] if start else None)


def _remote(src, dst, send_sems, recv_sems, k, to):
    return pltpu.make_async_remote_copy(src_ref=src, dst_ref=dst, send_sem=send_sems.at[k],
                                        recv_sem=recv_sems.at[k], device_id=to, device_id_type=MESH)


def _gather_first(n):
    def copies(refs, send_sems, recv_sems):
        x, y, c = _my_place()
        targets = [(x, y, 1 - c), (1 - x, y, c), (x, 1 - y, c), (1 - x, 1 - y, c)]
        out = []
        for a in range(n):
            blk = refs[a].at[4 * x + 2 * y + c]
            out += [_remote(blk, blk, send_sems, recv_sems, 4 * a + k, to) for k, to in enumerate(targets)]
        return out
    return copies


def _gather_pass(n):
    def copies(refs, send_sems, recv_sems):
        x, y, c = _my_place()
        out = []
        for a in range(n):
            for j, (px, py) in enumerate([(1 - x, y), (x, 1 - y), (1 - x, 1 - y)]):
                blk = refs[a].at[4 * px + 2 * py + c]
                out.append(_remote(blk, blk, send_sems, recv_sems, 3 * a + j, (x, y, 1 - c)))
        return out
    return copies


def _scatter_sibling(n):
    def copies(refs, send_sems, recv_sems):
        x, y, c = _my_place()
        return [_remote(refs[a].at[2 * j + 1 - c], refs[n + a].at[j], send_sems, recv_sems, 4 * a + j, (x, y, 1 - c))
                for a in range(n) for j in range(4)]
    return copies


def _scatter_chips(n):
    def copies(refs, send_sems, recv_sems):
        x, y, c = _my_place()
        out = []
        for a in range(n):
            for k in range(1, 4):
                px, py = _flip(x, k & 2), _flip(y, k & 1)
                out.append(_remote(refs[a].at[2 * px + py], refs[n + a].at[2 * x + y], send_sems, recv_sems,
                                   3 * a + k - 1, (px, py, c)))
        return out
    return copies


def _pair_sum(name, core, g, t):
    _, r, c = g.shape
    tr = _tile(r, 512, 8)
    g4 = g.reshape(4, 2, r, c)

    def body(core_ref, g_ref, t_ref, o_ref):
        o_ref[...] = (g_ref[...].astype(F32) + t_ref[...].astype(F32)).astype(BF16)

    return pl.pallas_call(
        body, name=name,
        grid_spec=pltpu.PrefetchScalarGridSpec(
            num_scalar_prefetch=1, grid=(4, r // tr),
            in_specs=[pl.BlockSpec((None, None, tr, c), lambda j, i, core_ref: (j, core_ref[0], i, 0)),
                      pl.BlockSpec((None, tr, c), lambda j, i, core_ref: (j, i, 0))],
            out_specs=pl.BlockSpec((None, tr, c), lambda j, i, core_ref: (j, i, 0))),
        out_shape=jax.ShapeDtypeStruct((4, r, c), BF16), compiler_params=_params(),
    )(core, g4, t)


def _adam_math(w, g, m, v):
    m2 = ADAM_B1 * m + (1.0 - ADAM_B1) * g
    v2 = ADAM_B2 * v + (1.0 - ADAM_B2) * (g * g)
    m_hat = m2 / (1.0 - ADAM_B1 ** ADAM_STEP)
    v_hat = v2 / (1.0 - ADAM_B2 ** ADAM_STEP)
    delta = -ADAM_LR * (m_hat / (jnp.sqrt(v_hat) + ADAM_EPS) + ADAM_WD * w)
    return delta, m2, v2


def _adam_sharded(name, chip, w, m, v, grads, row_off=0):
    _, r, c = w.shape
    tr = _tile(r, 256, 8)
    tc = c if tr < r else _tile(c, 256, 128)
    boff = row_off // tr

    def body(chip_ref, w_ref, m_ref, v_ref, p0_ref, q0_ref, p1_ref, q1_ref, g_ref, d_ref, nm_ref, nv_ref):
        mine = chip_ref[0]

        def total(p_ref, q_ref):
            acc = None
            for j in range(4):
                part = jnp.where(mine == j, p_ref[...], q_ref[j]).astype(F32)
                acc = part if acc is None else acc + part
            return acc

        g = jnp.where(pl.program_id(0) == 0, total(p0_ref, q0_ref), total(p1_ref, q1_ref))
        delta, m2, v2 = _adam_math(w_ref[...], g, m_ref[...], v_ref[...])
        g_ref[...] = g
        d_ref[...] = delta
        nm_ref[...] = m2
        nv_ref[...] = v2

    def grad_specs(layer):
        at = lambda l, i, j: (jnp.where(l == layer, boff + i, boff), jnp.where(l == layer, j, 0))
        return [pl.BlockSpec((None, tr, tc), lambda l, i, j, chip_ref: (chip_ref[0],) + at(l, i, j)),
                pl.BlockSpec((4, tr, tc), lambda l, i, j, chip_ref: (0,) + at(l, i, j))]

    wspec = pl.BlockSpec((None, tr, tc), lambda l, i, j, chip_ref: (l, i, j))
    sds = jax.ShapeDtypeStruct(w.shape, F32)
    return pl.pallas_call(
        body, name=name,
        grid_spec=pltpu.PrefetchScalarGridSpec(
            num_scalar_prefetch=1, grid=(2, r // tr, c // tc),
            in_specs=[wspec, wspec, wspec] + grad_specs(0) + grad_specs(1), out_specs=[wspec] * 4),
        out_shape=[sds] * 4, compiler_params=_params(),
    )(chip, w, m, v, grads[0][0], grads[0][1], grads[1][0], grads[1][1])


def _adam_small(name, w, m, v, g):
    def body(w_ref, m_ref, v_ref, g_ref, d_ref, nm_ref, nv_ref):
        delta, m2, v2 = _adam_math(w_ref[...], g_ref[...], m_ref[...], v_ref[...])
        d_ref[...] = delta
        nm_ref[...] = m2
        nv_ref[...] = v2

    sds = jax.ShapeDtypeStruct(w.shape, F32)
    vm = pl.BlockSpec(memory_space=pltpu.VMEM)
    return pl.pallas_call(body, name=name, in_specs=[vm] * 4, out_specs=[vm] * 3, out_shape=[sds] * 3,
                          compiler_params=_params())(w, m, v, g)


GATE_END = GATE_COL + 2 * HEADS


def _merge_dw_in(dwm_t, dwc_t):
    full = jnp.concatenate([dwm_t[:GATE_END], dwc_t.reshape(3 * CONV_W, D_MODEL)], axis=0)
    return full.reshape(N_DEV, IN_SH, D_MODEL)


def _pack128(parts):
    flat = jnp.concatenate([p.reshape(-1) for p in parts])
    n = flat.shape[0]
    rows = -(-n // 1024) * 8
    return jnp.pad(flat, (0, rows * 128 - n)).reshape(rows, 128)


def _unpack128(packed, shapes):
    flat = packed.reshape(-1)
    out, at = [], 0
    for s in shapes:
        n = int(np.prod(s))
        out.append(flat[at:at + n].reshape(s))
        at += n
    return out


def kernel(x, meta_tokens, norm_mix_w, w_in, b_gates, conv_w, mlstm_norm_w, w_out, norm_ffn_w, w_gate, w_up, w_down, norm_final_w, loss_target, m_meta_tokens, m_norm_mix_w, m_w_in, m_b_gates, m_conv_w, m_mlstm_norm_w, m_w_out, m_norm_ffn_w, m_w_gate, m_w_up, m_w_down, m_norm_final_w, v_meta_tokens, v_norm_mix_w, v_w_in, v_b_gates, v_conv_w, v_mlstm_norm_w, v_w_out, v_norm_ffn_w, v_w_gate, v_w_up, v_w_down, v_norm_final_w):
    seq = x.shape[1]
    rows = TOK0 + seq
    me = 4 * lax.axis_index("x") + 2 * lax.axis_index("y") + lax.axis_index("c")
    meta_sh = meta_tokens.shape[1]
    conv_sh = conv_w.shape[2]

    small = jnp.concatenate(
        [meta_tokens, jnp.pad(conv_w.reshape(DEPTH * 3, conv_sh), ((0, 2), (0, meta_sh - conv_sh)))], axis=0)
    slots = _exchange_small("gather_small", small, reduce=False)
    meta_full = jnp.transpose(slots[:, :N_META, :], (1, 0, 2)).reshape(N_META, D_MODEL)
    conv_full = jnp.transpose(slots[:, N_META:N_META + DEPTH * 3, :conv_sh], (1, 0, 2)).reshape(DEPTH, 3, CONV_W)
    conv_rows = [jnp.pad(conv_full[l], ((0, 5), (0, 0))) for l in range(DEPTH)]

    w_in_t, m_w_in_t, v_w_in_t = (jnp.transpose(a, (0, 2, 1)) for a in (w_in, m_w_in, v_w_in))
    w_gate_t, m_w_gate_t, v_w_gate_t = (jnp.transpose(a, (0, 2, 1)) for a in (w_gate, m_w_gate, v_w_gate))
    w_up_t, m_w_up_t, v_w_up_t = (jnp.transpose(a, (0, 2, 1)) for a in (w_up, m_w_up, v_w_up))
    shards = []
    for l in range(DEPTH):
        shards += [w_in_t[l].astype(BF16), w_out[l].astype(BF16), w_gate_t[l].astype(BF16),
                   w_up_t[l].astype(BF16), w_down[l].astype(BF16)]
    per_layer = ("w_in", "w_out", "w_gate", "w_up", "w_down")
    gather_names = [f"{nm}_{l}" for l in range(DEPTH) for nm in per_layer]
    gather_state = {}

    def gather_start(i, after=None):
        buf = lax.dynamic_update_index_in_dim(lax.empty((N_DEV,) + shards[i].shape, BF16), shards[i], me, 0)
        arrs, sems, tok = _split_copy(f"gather_start_{gather_names[i]}", [buf], start=(_gather_first(1), 4),
                                      after=after)
        gather_state[i] = (arrs, sems)
        return tok

    def gather_pass(i, after):
        arrs, sems = gather_state[i]
        arrs, sems, tok = _split_copy(f"gather_pass_{gather_names[i]}", arrs, start=(_gather_pass(1), 3),
                                      wait=(_gather_first(1), sems), after=after)
        gather_state[i] = (arrs, sems)
        return tok

    def gathered(i, after):
        arrs, sems = gather_state[i]
        arrs, _, _ = _split_copy(f"gather_done_{gather_names[i]}", arrs, wait=(_gather_pass(1), sems), after=after)
        return arrs[0]

    bias = [jnp.pad(b_gates[l].reshape(1, 2 * HEADS), ((0, 0), (0, 128 - 2 * HEADS))) for l in range(DEPTH)]
    nmix = [norm_mix_w[l].reshape(1, D_MODEL) for l in range(DEPTH)]
    nffn = [norm_ffn_w[l].reshape(1, D_MODEL) for l in range(DEPTH)]
    nmls = [mlstm_norm_w[l].reshape(1, MLSTM_W) for l in range(DEPTH)]
    weights = [dict() for _ in range(DEPTH)]
    saved = [dict() for _ in range(DEPTH)]

    def layer_fwd(l, h, tok):
        w, s = weights[l], saved[l]
        first = len(per_layer) * l
        w["win_t"] = gathered(first, tok).reshape(D_IN, D_MODEL)
        w["wc_t"] = w["win_t"][GATE_END:].reshape(3, CONV_W, D_MODEL)
        s["h0"] = h
        s["hn"] = _rms_fwd(f"norm_mix_{l}", h, nmix[l])
        s["pm"] = _mm_nt(f"proj_mlstm_{l}", s["hn"], w["win_t"], F32, tn=640, tk=D_MODEL, n=PM_W)
        tok = gather_pass(first + 1, s["pm"])
        s["pc"] = _mm_nt_bcols(f"proj_conv_{l}", s["hn"], w["wc_t"], F32, dep=tok)
        tok = gather_pass(first + 2, s["pc"])
        hm, s["ht"], s["cs"], s["ns"], s["ms"] = _mlstm_fwd(f"mlstm_fwd_{l}", s["pm"], bias[l] + tok[:1], nmls[l])
        tok = gather_pass(first + 3, hm)
        hc = _conv_fwd(f"conv_fwd_{l}", s["pc"], conv_rows[l] + tok[0, 0])
        s["cat"] = jnp.concatenate([hm, hc], axis=1)
        w["wo"] = gathered(first + 1, s["cat"]).reshape(D_MODEL, D_MODEL)
        s["h1"] = _mm_nn(f"out_proj_{l}", s["cat"], w["wo"], F32, res=s["h0"])
        tok = gather_pass(first + 4, s["h1"])
        s["hf"] = _rms_fwd(f"norm_ffn_{l}", s["h1"], nffn[l])
        w["wg_t"] = gathered(first + 2, s["hf"]).reshape(D_FF, D_MODEL)
        w["wu_t"] = gathered(first + 3, s["hf"]).reshape(D_FF, D_MODEL)
        s["g"], s["u"], s["act"] = _ffn_in(f"ffn_in_{l}", s["hf"], w["wg_t"], w["wu_t"], dep=tok)
        tok = gather_pass(first + 5, s["act"]) if l + 1 < DEPTH else None
        w["wd"] = gathered(first + 4, s["act"]).reshape(D_FF, D_MODEL)
        return _mm_nn(f"ffn_out_{l}", s["act"], w["wd"], F32, res=s["h1"], dep=tok)

    tok = None
    for i in range(len(shards)):
        tok = gather_start(i, after=tok)
    h = jnp.concatenate([jnp.zeros((PAD_FRONT, D_MODEL), F32), meta_full, x[0]], axis=0)
    h = layer_fwd(0, h, gather_pass(0, tok))
    h = layer_fwd(1, h, h)

    dh, dh_b, d_final, loss_part = _final_loss("final_loss", h, norm_final_w.reshape(1, D_MODEL), loss_target[0])

    core = lax.axis_index("c").astype(jnp.int32).reshape(1)
    chip = (2 * lax.axis_index("x") + lax.axis_index("y")).astype(jnp.int32).reshape(1)
    scatter_state = {}

    def scatter_begin(nm, grad):
        land = lax.empty((4,) + grad.shape[1:], BF16)
        arrs, sems, tok = _split_copy(f"grad_sibling_start_{nm}", [grad, land], start=(_scatter_sibling(1), 4))
        scatter_state[nm] = (arrs, sems)
        return tok

    def scatter_advance(nm, after):
        arrs, sems = scatter_state[nm]
        arrs, _, _ = _split_copy(f"grad_sibling_done_{nm}", arrs, wait=(_scatter_sibling(1), sems), after=after)
        part = _pair_sum(f"grad_pair_sum_{nm}", core, arrs[0], arrs[1])
        arrs, sems, tok = _split_copy(f"grad_chips_start_{nm}", [part, lax.empty(part.shape, BF16)],
                                      start=(_scatter_chips(1), 3))
        scatter_state[nm] = (arrs, sems)
        return tok

    def scattered(nm, after):
        arrs, sems = scatter_state[nm]
        arrs, _, _ = _split_copy(f"grad_chips_done_{nm}", arrs, wait=(_scatter_chips(1), sems), after=after)
        return arrs[0], arrs[1]

    d_mix, d_ffn, d_mls, d_bias, d_conv = ([None] * DEPTH for _ in range(5))

    def layer_bwd(l, dh, dh_b, tok):
        w, s = weights[l], saved[l]
        dg, du = _ffn_act_bwd(f"d_act_{l}", dh_b, w["wd"], s["g"], s["u"], dep=tok)
        dw_down = _mm_tn(f"dw_down_{l}", s["act"], dh_b, BF16, tm=1408, tn=1024)
        tok = scatter_begin(f"w_down_{l}", dw_down.reshape(N_DEV, FF_SH, D_MODEL))
        dhf = _mm_nn(f"d_ffn_gate_{l}", dg, w["wg_t"], F32, dep=tok)
        tok = scatter_advance(f"w_down_{l}", after=dhf)
        dhf = _mm_nn(f"d_ffn_up_{l}", du, w["wu_t"], F32, res=dhf, dep=tok)
        dw_gate = _mm_tn(f"dw_gate_{l}", dg, s["hf"], BF16, tm=1408, tn=1024)
        tok = scatter_begin(f"w_gate_{l}", dw_gate.reshape(N_DEV, FF_SH, D_MODEL))
        dw_up = _mm_tn(f"dw_up_{l}", du, s["hf"], BF16, tm=1408, tn=1024, dep=tok)
        tok = scatter_begin(f"w_up_{l}", dw_up.reshape(N_DEV, FF_SH, D_MODEL))
        dh1, dh1_b, d_ffn[l] = _rms_bwd(f"norm_ffn_bwd_{l}", s["h1"], nffn[l] + tok[0, 0], dhf, dh)
        tok = scatter_advance(f"w_gate_{l}", after=dh1)
        dcat = _mm_nt(f"d_cat_{l}", dh1_b, w["wo"], F32, tk=D_MODEL, dep=tok)
        tok = scatter_advance(f"w_up_{l}", after=dcat)
        dw_out = _mm_tn(f"dw_out_{l}", s["cat"], dh1_b, BF16, tn=1024, dep=tok)
        tok = scatter_begin(f"w_out_{l}", dw_out.reshape(N_DEV, OUT_SH, D_MODEL))
        dpm, d_mls[l], d_bias[l] = _mlstm_bwd(f"mlstm_bwd_{l}", dcat, s["pm"], s["ht"], s["cs"], s["ns"],
                                               s["ms"], bias[l] + tok[:1], nmls[l])
        dpc, d_conv[l] = _conv_bwd(f"conv_bwd_{l}", dcat, s["pc"], conv_rows[l])
        tok = scatter_advance(f"w_out_{l}", after=dpc)
        dhn = _mm_nn_kt(f"d_norm_mlstm_{l}", dpm, w["win_t"], F32, tk=PM_W, dep=tok)
        dhn = _mm_nn_ksum(f"d_norm_conv_{l}", dpc, w["wc_t"], F32, res=dhn)
        dwm_t = _mm_tn(f"dw_mlstm_{l}", dpm, s["hn"], BF16, tm=640, tn=1024)
        dwc_t = _mm_tn_acols(f"dw_conv_{l}", dpc, s["hn"], BF16)
        tok = scatter_begin(f"w_in_{l}", _merge_dw_in(dwm_t, dwc_t))
        dh, dh_b, d_mix[l] = _rms_bwd(f"norm_mix_bwd_{l}", s["h0"], nmix[l] + tok[0, 0], dhn, dh1)
        return dh, dh_b, scatter_advance(f"w_in_{l}", after=dh)

    dh, dh_b, tok = layer_bwd(1, dh, dh_b, None)
    dh, dh_b, tok_tail = layer_bwd(0, dh, dh_b, tok)

    pq = {}
    after = tok_tail
    for l in reversed(range(DEPTH)):
        for nm in ("w_down", "w_gate", "w_up", "w_out", "w_in"):
            if (nm, l) != ("w_in", 0):
                pq[nm, l] = scattered(f"{nm}_{l}", after)
                after = pq[nm, l][0]
    untransposed = lambda outs: [jnp.transpose(o, (0, 2, 1)) for o in outs]
    g_out, d_out, nm_out, nv_out = _adam_sharded(
        "adam_w_out", chip, w_out, m_w_out, v_w_out, [pq["w_out", 0], pq["w_out", 1]])
    g_gate, d_gate, nm_gate, nv_gate = untransposed(_adam_sharded(
        "adam_w_gate", chip, w_gate_t, m_w_gate_t, v_w_gate_t, [pq["w_gate", 0], pq["w_gate", 1]]))
    g_up, d_up, nm_up, nv_up = untransposed(_adam_sharded(
        "adam_w_up", chip, w_up_t, m_w_up_t, v_w_up_t, [pq["w_up", 0], pq["w_up", 1]]))
    g_down, d_down, nm_down, nv_down = _adam_sharded(
        "adam_w_down", chip, w_down, m_w_down, v_w_down, [pq["w_down", 0], pq["w_down", 1]])
    pq["w_in", 0] = scattered("w_in_0", nv_down)
    g_in, d_in, nm_in, nv_in = untransposed(_adam_sharded(
        "adam_w_in", chip, w_in_t, m_w_in_t, v_w_in_t, [pq["w_in", 0], pq["w_in", 1]]))

    bg = jnp.concatenate([d_bias[l][0, :2 * HEADS] for l in range(DEPTH)])
    red_in = jnp.concatenate([
        dh[PAD_FRONT:TOK0], d_mix[0], d_mix[1], d_ffn[0], d_ffn[1], d_final,
        jnp.concatenate([d_mls[0], d_mls[1]], axis=1),
        jnp.stack([d_conv[l][:3] for l in range(DEPTH)]).reshape(3, 2 * CONV_W),
        jnp.pad(bg, (0, D_MODEL - bg.shape[0])).reshape(1, D_MODEL),
        jnp.pad(loss_part[:, :1], ((0, 0), (0, D_MODEL - 1))),
        jnp.zeros((5, D_MODEL), F32) + tok_tail[0, 0]], axis=0)
    red = _exchange_small("reduce_small", red_in, reduce=True)
    loss = red[26, 0]
    g_meta = lax.dynamic_slice_in_dim(red[:N_META], me * meta_sh, meta_sh, axis=1)
    g_mix, g_ffn, g_final = red[16:18], red[18:20], red[20]
    g_mls = red[21].reshape(DEPTH, MLSTM_W)
    g_conv = lax.dynamic_slice_in_dim(red[22:25].reshape(DEPTH, 3, CONV_W), me * conv_sh, conv_sh, axis=2)
    g_bias = red[25, :DEPTH * 2 * HEADS].reshape(DEPTH, 2 * HEADS)

    small_w = [meta_tokens, norm_mix_w, b_gates, conv_w, mlstm_norm_w, norm_ffn_w, norm_final_w]
    small_m = [m_meta_tokens, m_norm_mix_w, m_b_gates, m_conv_w, m_mlstm_norm_w, m_norm_ffn_w, m_norm_final_w]
    small_v = [v_meta_tokens, v_norm_mix_w, v_b_gates, v_conv_w, v_mlstm_norm_w, v_norm_ffn_w, v_norm_final_w]
    small_g = [g_meta, g_mix, g_bias, g_conv, g_mls, g_ffn, g_final]
    shapes = [a.shape for a in small_w]
    packed = _adam_small("adam_small", _pack128(small_w), _pack128(small_m), _pack128(small_v), _pack128(small_g))
    (d_meta, d_nmix, d_bg, d_cw, d_nmls, d_nffn, d_nfin), (nm_meta, nm_nmix, nm_bg, nm_cw, nm_nmls, nm_nffn, nm_nfin), \
        (nv_meta, nv_nmix, nv_bg, nv_cw, nv_nmls, nv_nffn, nv_nfin) = (_unpack128(p, shapes) for p in packed)

    grad_x = dh[TOK0:].reshape(1, seq, D_MODEL)
    return (loss, grad_x,
            g_meta, g_mix, g_in, g_bias, g_conv, g_mls, g_out, g_ffn, g_gate, g_up, g_down, g_final,
            d_meta, d_nmix, d_in, d_bg, d_cw, d_nmls, d_out, d_nffn, d_gate, d_up, d_down, d_nfin,
            nm_meta, nm_nmix, nm_in, nm_bg, nm_cw, nm_nmls, nm_out, nm_nffn, nm_gate, nm_up, nm_down, nm_nfin,
            nv_meta, nv_nmix, nv_in, nv_bg, nv_cw, nv_nmls, nv_out, nv_nffn, nv_gate, nv_up, nv_down, nv_nfin)
```

```python
import functools

import numpy as np
import jax
import jax.numpy as jnp
from jax import lax
from jax.experimental import pallas as pl
from jax.experimental.pallas import tpu as pltpu

F32 = jnp.float32
BF16 = jnp.bfloat16
MESH = pl.DeviceIdType.MESH

D_MODEL = 2048
DEPTH = 2
N_META = 16
MLSTM_W = 1024
CONV_W = 1024
HEADS = 4
DV = 256
DQK = 128
QK_W = 512
CHUNK = 64
PAD_FRONT = 48
TOK0 = PAD_FRONT + N_META
D_FF = 5632
N_DEV = 8
FF_SH = D_FF // N_DEV
D_IN = 6152
IN_SH = D_IN // N_DEV
OUT_SH = D_MODEL // N_DEV
GATE_COL = 3072
PM_W = GATE_COL + 128
GATE_CAP = 15.0
EPS = 1e-6
QSCALE = DQK ** -0.5

ADAM_LR = 0.001
ADAM_B1 = 0.9
ADAM_B2 = 0.999
ADAM_EPS = 1e-08
ADAM_WD = 0.01
ADAM_STEP = 10

V7X_VMEM_LIMIT = 50 * 1024 * 1024


def _params(**kw):
    return pltpu.CompilerParams(vmem_limit_bytes=V7X_VMEM_LIMIT, **kw)


def _tile(n, target, mult):
    best = None
    for t in range(mult, min(n, target) + 1, mult):
        if n % t == 0:
            best = t
    return best if best is not None else n


def _sigmoid(x):
    return 1.0 / (1.0 + jnp.exp(-x))


NN = ((1,), (0,))
NT = ((1,), (1,))
TN = ((0,), (0,))


def _matmul(name, a, b, out_shape, out_dtype, grid, a_bs, b_bs, o_bs, dims, nk, acc_shape=None,
            res=None, res_bs=None, dep=None):
    has_res = res is not None
    n_in = 2 + has_res + (dep is not None)

    def body(*refs):
        a_ref, b_ref = refs[0], refs[1]
        r_ref = refs[2] if has_res else None
        o_ref = refs[n_in]
        x = lax.dot_general(a_ref[...], b_ref[...], (dims, ((), ())), preferred_element_type=F32)
        if nk == 1:
            if has_res:
                x = x + r_ref[...]
            o_ref[...] = x.astype(o_ref.dtype)
            return
        acc = refs[n_in + 1]
        k = pl.program_id(len(grid) - 1)

        @pl.when(k == 0)
        def _():
            acc[...] = (x + r_ref[...]) if has_res else x

        @pl.when(k > 0)
        def _():
            acc[...] += x

        @pl.when(k == nk - 1)
        def _():
            o_ref[...] = acc[...].astype(o_ref.dtype)

    ins = [a, b] + ([res] if has_res else [])
    specs = [a_bs, b_bs] + ([res_bs] if has_res else [])
    if dep is not None:
        ins.append(dep)
        specs.append(pl.BlockSpec((8, 128), lambda *_: (0, 0)))
    scratch = [pltpu.VMEM(acc_shape, F32)] if nk > 1 else []
    return pl.pallas_call(
        body, name=name, grid=grid, in_specs=specs, out_specs=o_bs,
        out_shape=jax.ShapeDtypeStruct(out_shape, out_dtype), scratch_shapes=scratch,
        compiler_params=_params(),
    )(*ins)


def _mm_nn(name, a, b, out_dtype, res=None, tm=1056, tn=512, dep=None):
    r, k = a.shape
    n = b.shape[1]
    tm, tn = _tile(r, tm, 8), _tile(n, tn, 128)
    return _matmul(name, a, b, (r, n), out_dtype, (r // tm, n // tn, 1),
                   pl.BlockSpec((tm, k), lambda i, j, s: (i, 0)),
                   pl.BlockSpec((k, tn), lambda i, j, s: (0, j)),
                   pl.BlockSpec((tm, tn), lambda i, j, s: (i, j)), NN, 1,
                   res=res, res_bs=pl.BlockSpec((tm, tn), lambda i, j, s: (i, j)), dep=dep)


def _mm_nn_kt(name, a, b, out_dtype, tm=1056, tn=1024, tk=640, dep=None):
    r, k = a.shape
    n = b.shape[1]
    tm, tn, tk = _tile(r, tm, 8), _tile(n, tn, 128), _tile(k, tk, 128)
    nk = k // tk
    return _matmul(name, a, b, (r, n), out_dtype, (r // tm, n // tn, nk),
                   pl.BlockSpec((tm, tk), lambda i, j, s: (i, s)),
                   pl.BlockSpec((tk, tn), lambda i, j, s: (s, j)),
                   pl.BlockSpec((tm, tn), lambda i, j, s: (i, j)), NN, nk, acc_shape=(tm, tn), dep=dep)


def _mm_nn_ksum(name, a3, b3, out_dtype, res=None, tm=1056, tn=1024, dep=None):
    e, r, kb = a3.shape
    n = b3.shape[2]
    tm, tn = _tile(r, tm, 8), _tile(n, tn, 128)
    return _matmul(name, a3, b3, (r, n), out_dtype, (r // tm, n // tn, e),
                   pl.BlockSpec((None, tm, kb), lambda i, j, s: (s, i, 0)),
                   pl.BlockSpec((None, kb, tn), lambda i, j, s: (s, 0, j)),
                   pl.BlockSpec((tm, tn), lambda i, j, s: (i, j)), NN, e, acc_shape=(tm, tn),
                   res=res, res_bs=pl.BlockSpec((tm, tn), lambda i, j, s: (i, j)), dep=dep)


def _mm_nt(name, a, b, out_dtype, res=None, tm=1056, tn=512, tk=640, n=None, dep=None):
    r, k = a.shape
    n = b.shape[0] if n is None else n
    tm, tn, tk = _tile(r, tm, 8), _tile(n, tn, 128), _tile(k, tk, 128)
    nk = k // tk
    return _matmul(name, a, b, (r, n), out_dtype, (r // tm, n // tn, nk),
                   pl.BlockSpec((tm, tk), lambda i, j, s: (i, s)),
                   pl.BlockSpec((tn, tk), lambda i, j, s: (j, s)),
                   pl.BlockSpec((tm, tn), lambda i, j, s: (i, j)), NT, nk, acc_shape=(tm, tn),
                   res=res, res_bs=pl.BlockSpec((tm, tn), lambda i, j, s: (i, j)), dep=dep)


def _mm_nt_bcols(name, a, b3, out_dtype, tm=1056, dep=None):
    r, k = a.shape
    e, n, _ = b3.shape
    tm = _tile(r, tm, 8)
    return _matmul(name, a, b3, (e, r, n), out_dtype, (r // tm, e, 1),
                   pl.BlockSpec((tm, k), lambda i, g, s: (i, 0)),
                   pl.BlockSpec((None, n, k), lambda i, g, s: (g, 0, 0)),
                   pl.BlockSpec((None, tm, n), lambda i, g, s: (g, i, 0)), NT, 1, dep=dep)


def _mm_tn(name, a, b, out_dtype, tm=1024, tn=640, dep=None):
    r, m = a.shape
    n = b.shape[1]
    tm, tn = _tile(m, tm, 128), _tile(n, tn, 128)
    return _matmul(name, a, b, (m, n), out_dtype, (m // tm, n // tn, 1),
                   pl.BlockSpec((r, tm), lambda i, j, s: (0, i)),
                   pl.BlockSpec((r, tn), lambda i, j, s: (0, j)),
                   pl.BlockSpec((tm, tn), lambda i, j, s: (i, j)), TN, 1, dep=dep)


def _mm_tn_acols(name, a3, b, out_dtype, tn=1024, dep=None):
    e, r, m = a3.shape
    n = b.shape[1]
    tn = _tile(n, tn, 128)
    return _matmul(name, a3, b, (e, m, n), out_dtype, (n // tn, e, 1),
                   pl.BlockSpec((None, r, m), lambda j, g, s: (g, 0, 0)),
                   pl.BlockSpec((r, tn), lambda j, g, s: (0, j)),
                   pl.BlockSpec((None, m, tn), lambda j, g, s: (g, 0, j)), TN, 1, dep=dep)


def _rms_fwd(name, h, w):
    r, d = h.shape
    tr = _tile(r, 264, 8)

    def body(h_ref, w_ref, o_ref):
        x = h_ref[...]
        rs = lax.rsqrt(jnp.mean(x * x, axis=1, keepdims=True) + EPS)
        o_ref[...] = (x * rs * w_ref[...]).astype(BF16)

    return pl.pallas_call(
        body, name=name, grid=(r // tr,),
        in_specs=[pl.BlockSpec((tr, d), lambda i: (i, 0)), pl.BlockSpec((1, d), lambda i: (0, 0))],
        out_specs=pl.BlockSpec((tr, d), lambda i: (i, 0)),
        out_shape=jax.ShapeDtypeStruct((r, d), BF16), compiler_params=_params(),
    )(h, w)


def _rms_bwd(name, x, w, dy, dres):
    r, d = x.shape
    tr = _tile(r, 264, 8)

    def body(x_ref, w_ref, dy_ref, dr_ref, dx_ref, dxb_ref, dw_ref):
        xv = x_ref[...]
        g = dy_ref[...]
        rs = lax.rsqrt(jnp.mean(xv * xv, axis=1, keepdims=True) + EPS)
        wg = g * w_ref[...]
        dx = rs * wg - xv * (rs * rs * rs) * jnp.mean(xv * wg, axis=1, keepdims=True) + dr_ref[...]
        dx_ref[...] = dx
        dxb_ref[...] = dx.astype(BF16)
        part = jnp.sum(g * xv * rs, axis=0, keepdims=True)

        @pl.when(pl.program_id(0) == 0)
        def _():
            dw_ref[...] = part

        @pl.when(pl.program_id(0) > 0)
        def _():
            dw_ref[...] += part

    row = pl.BlockSpec((tr, d), lambda i: (i, 0))
    vec = pl.BlockSpec((1, d), lambda i: (0, 0))
    return pl.pallas_call(
        body, name=name, grid=(r // tr,), in_specs=[row, vec, row, row], out_specs=[row, row, vec],
        out_shape=[jax.ShapeDtypeStruct((r, d), F32), jax.ShapeDtypeStruct((r, d), BF16),
                   jax.ShapeDtypeStruct((1, d), F32)],
        compiler_params=_params(),
    )(x, w, dy, dres)


def _final_loss(name, h, w, target):
    r, d = h.shape
    nb = r // CHUNK

    def body(h_ref, w_ref, t_ref, dh_ref, dhb_ref, dw_ref, ls_ref):
        i = pl.program_id(0)

        @pl.when(i == 0)
        def _():
            dh_ref[...] = jnp.zeros_like(dh_ref)
            dhb_ref[...] = jnp.zeros_like(dhb_ref)
            dw_ref[...] = jnp.zeros_like(dw_ref)
            ls_ref[...] = jnp.zeros_like(ls_ref)

        @pl.when(i > 0)
        def _():
            xv = h_ref[...]
            wv = w_ref[...]
            rs = lax.rsqrt(jnp.mean(xv * xv, axis=1, keepdims=True) + EPS)
            err = xv * rs * wv - t_ref[...]
            sq = jnp.sum(jnp.sum(err * err, axis=1, keepdims=True), axis=0, keepdims=True)
            ls_ref[...] += jnp.broadcast_to(sq * (0.5 / d), ls_ref.shape)
            g = err * (1.0 / d)
            wg = g * wv
            dx = rs * wg - xv * (rs * rs * rs) * jnp.mean(xv * wg, axis=1, keepdims=True)
            dh_ref[...] = dx
            dhb_ref[...] = dx.astype(BF16)
            dw_ref[...] += jnp.sum(g * xv * rs, axis=0, keepdims=True)

    row = pl.BlockSpec((CHUNK, d), lambda i: (i, 0))
    vec = pl.BlockSpec((1, d), lambda i: (0, 0))
    return pl.pallas_call(
        body, name=name, grid=(nb,),
        in_specs=[row, vec, pl.BlockSpec((CHUNK, d), lambda i: (jnp.maximum(i - 1, 0), 0))],
        out_specs=[row, row, vec, pl.BlockSpec((1, 128), lambda i: (0, 0))],
        out_shape=[jax.ShapeDtypeStruct((r, d), F32), jax.ShapeDtypeStruct((r, d), BF16),
                   jax.ShapeDtypeStruct((1, d), F32), jax.ShapeDtypeStruct((1, 128), F32)],
        compiler_params=_params(),
    )(h, w, target)


def _ffn_in(name, hf, wg_t, wu_t, dep=None, tm=1056, tn=512):
    r, d = hf.shape
    f = wg_t.shape[0]
    tm, tn = _tile(r, tm, 8), _tile(f, tn, 128)

    def body(h_ref, wg_ref, wu_ref, *rest):
        g_ref, u_ref, a_ref = rest[-3:]
        x = h_ref[...]
        g = lax.dot_general(x, wg_ref[...], (NT, ((), ())), preferred_element_type=F32)
        u = lax.dot_general(x, wu_ref[...], (NT, ((), ())), preferred_element_type=F32)
        g_ref[...] = g.astype(BF16)
        u_ref[...] = u.astype(BF16)
        a_ref[...] = (g * _sigmoid(g) * u).astype(BF16)

    wspec = pl.BlockSpec((tn, d), lambda i, j: (j, 0))
    ospec = pl.BlockSpec((tm, tn), lambda i, j: (i, j))
    ins, specs = [hf, wg_t, wu_t], [pl.BlockSpec((tm, d), lambda i, j: (i, 0)), wspec, wspec]
    if dep is not None:
        ins.append(dep)
        specs.append(pl.BlockSpec((8, 128), lambda *_: (0, 0)))
    return pl.pallas_call(
        body, name=name, grid=(r // tm, f // tn), in_specs=specs, out_specs=[ospec] * 3,
        out_shape=[jax.ShapeDtypeStruct((r, f), BF16)] * 3, compiler_params=_params(),
    )(*ins)


def _ffn_act_bwd(name, dh, wd, g, u, dep=None, tm=1056, tn=512):
    r, d = dh.shape
    f = wd.shape[0]
    tm, tn = _tile(r, tm, 8), _tile(f, tn, 128)

    def body(dh_ref, wd_ref, g_ref, u_ref, *rest):
        dg_ref, du_ref = rest[-2:]
        da = lax.dot_general(dh_ref[...], wd_ref[...], (NT, ((), ())), preferred_element_type=F32)
        gv = g_ref[...].astype(F32)
        s = _sigmoid(gv)
        dg_ref[...] = (da * u_ref[...].astype(F32) * (s + gv * s * (1.0 - s))).astype(BF16)
        du_ref[...] = (da * gv * s).astype(BF16)

    tile = pl.BlockSpec((tm, tn), lambda i, j: (i, j))
    ins = [dh, wd, g, u]
    specs = [pl.BlockSpec((tm, d), lambda i, j: (i, 0)), pl.BlockSpec((tn, d), lambda i, j: (j, 0)), tile, tile]
    if dep is not None:
        ins.append(dep)
        specs.append(pl.BlockSpec((8, 128), lambda *_: (0, 0)))
    return pl.pallas_call(
        body, name=name, grid=(r // tm, f // tn), in_specs=specs, out_specs=[tile] * 2,
        out_shape=[jax.ShapeDtypeStruct((r, f), BF16)] * 2, compiler_params=_params(),
    )(*ins)


def _shift_down(a, k):
    row = lax.broadcasted_iota(jnp.int32, a.shape, 0)
    return jnp.where(row >= k, pltpu.roll(a, k, 0), 0.0)


def _shift_up(a, k):
    n = a.shape[0]
    row = lax.broadcasted_iota(jnp.int32, a.shape, 0)
    return jnp.where(row < n - k, pltpu.roll(a, n - k, 0), 0.0)


def _conv_fwd(name, pc, cw):
    _, r, w = pc.shape

    def body(pc_ref, cw_ref, o_ref):
        a = pc_ref[2] * pc_ref[0]
        cwv = cw_ref[...]
        conv = _shift_down(a, 2) * cwv[0:1] + _shift_down(a, 1) * cwv[1:2] + a * cwv[2:3]
        o_ref[...] = (pc_ref[1] * conv).astype(BF16)

    return pl.pallas_call(
        body, name=name, grid=(w // 128,),
        in_specs=[pl.BlockSpec((3, r, 128), lambda j: (0, 0, j)), pl.BlockSpec((8, 128), lambda j: (0, j))],
        out_specs=pl.BlockSpec((r, 128), lambda j: (0, j)),
        out_shape=jax.ShapeDtypeStruct((r, w), BF16), compiler_params=_params(),
    )(pc, cw)


def _conv_bwd(name, dcat, pc, cw):
    _, r, w = pc.shape
    nblk = w // 128

    def body(dy_ref, pc_ref, cw_ref, dpc_ref, dcw_ref):
        u, gb, gc = pc_ref[0], pc_ref[1], pc_ref[2]
        cwv = cw_ref[...]
        dy = dy_ref[...]
        a = gc * u
        a1, a2 = _shift_down(a, 1), _shift_down(a, 2)
        conv = a2 * cwv[0:1] + a1 * cwv[1:2] + a * cwv[2:3]
        dconv = dy * gb
        da = dconv * cwv[2:3] + _shift_up(dconv, 1) * cwv[1:2] + _shift_up(dconv, 2) * cwv[0:1]
        dpc_ref[0] = (da * gc).astype(BF16)
        dpc_ref[1] = (dy * conv).astype(BF16)
        dpc_ref[2] = (da * u).astype(BF16)
        row = lax.broadcasted_iota(jnp.int32, (8, 128), 0)
        dw0 = jnp.sum(dconv * a2, axis=0, keepdims=True)
        dw1 = jnp.sum(dconv * a1, axis=0, keepdims=True)
        dw2 = jnp.sum(dconv * a, axis=0, keepdims=True)
        dcw_ref[...] = jnp.where(row == 0, dw0, jnp.where(row == 1, dw1, jnp.where(row == 2, dw2, 0.0)))

    return pl.pallas_call(
        body, name=name, grid=(nblk,),
        in_specs=[pl.BlockSpec((r, 128), lambda j: (0, nblk + j)),
                  pl.BlockSpec((3, r, 128), lambda j: (0, 0, j)), pl.BlockSpec((8, 128), lambda j: (0, j))],
        out_specs=[pl.BlockSpec((3, r, 128), lambda j: (0, 0, j)), pl.BlockSpec((8, 128), lambda j: (0, j))],
        out_shape=[jax.ShapeDtypeStruct((3, r, w), BF16), jax.ShapeDtypeStruct((8, w), F32)],
        compiler_params=_params(),
    )(dcat, pc, cw)


def _dot(a, b, dims):
    return lax.dot_general(a, b, (dims, ((), ())), preferred_element_type=F32)


def _col_to_row(xc, eye):
    return jnp.sum(jnp.where(eye, xc, 0.0), axis=0, keepdims=True)


def _row_to_col(xr, eye):
    return jnp.sum(jnp.where(eye, xr, 0.0), axis=1, keepdims=True)


def _gate_tiles(graw, bias, row0):
    th = jnp.tanh((graw + bias) / GATE_CAP)
    z = GATE_CAP * th
    row = lax.broadcasted_iota(jnp.int32, graw.shape, 0) + row0
    real = row >= PAD_FRONT
    li = jnp.where(real, z, -jnp.inf)
    lf = jnp.where(real, jnp.minimum(z, 0.0) - jnp.log(1.0 + jnp.exp(-jnp.abs(z))), 0.0)
    return th, z, li, lf, real


def _chunk_common(pm, h, li, lf, cst, nst, mst, tril, eye):
    kraw = pm[:, QK_W + h * DQK:QK_W + (h + 1) * DQK]
    q = (pm[:, h * DQK:(h + 1) * DQK] * QSCALE).astype(BF16)
    k = kraw.astype(BF16)
    v = pm[:, 2 * QK_W + h * DV:2 * QK_W + (h + 1) * DV].astype(BF16)
    li_c = li[:, h:h + 1]
    lf_c = lf[:, HEADS + h:HEADS + h + 1]
    li_r = _col_to_row(li_c, eye)
    lf_r = _col_to_row(lf_c, eye)
    b_c = jnp.sum(jnp.where(tril, lf_r, 0.0), axis=1, keepdims=True)
    b_r = _col_to_row(b_c, eye)
    dmat = jnp.where(tril, b_c - b_r + li_r, -jnp.inf)
    inter = b_c + mst
    mt = jnp.maximum(inter, jnp.max(dmat, axis=1, keepdims=True))
    w_inter = jnp.exp(inter - mt)
    p = jnp.exp(dmat - mt)
    s = _dot(q, k, NT) * p
    cb = cst.astype(BF16)
    nb = nst.astype(BF16).astype(F32)
    qc = _dot(q, cb, NN)
    qn = jnp.sum(q.astype(F32) * nb, axis=1, keepdims=True)
    den = w_inter * qn + jnp.sum(s, axis=1, keepdims=True)
    dn = jnp.maximum(jnp.abs(den), jnp.exp(-mt))
    b_end = b_c[CHUNK - 1:CHUNK, :]
    decay = b_end - b_c + li_c
    m_new = jnp.maximum(b_end + mst, jnp.max(decay, axis=0, keepdims=True))
    w_old = jnp.exp(b_end + mst - m_new)
    w_in = jnp.exp(decay - m_new)
    kw = (w_in * kraw).astype(BF16)
    return dict(q=q, k=k, v=v, kraw=kraw, mt=mt, w_inter=w_inter, p=p, s=s, cb=cb, nb=nb, qc=qc, qn=qn,
                den=den, dn=dn, m_new=m_new, w_old=w_old, w_in=w_in, kw=kw)


def _mlstm_fwd(name, pm, bias, nw):
    r = pm.shape[0]
    nc = r // CHUNK

    def body(pm_ref, b_ref, nw_ref, hm_ref, ht_ref, cs_ref, ns_ref, ms_ref, c_scr, n_scr, m_scr):
        ci = pl.program_id(0)

        @pl.when(ci == 0)
        def _():
            c_scr[...] = jnp.zeros_like(c_scr)
            n_scr[...] = jnp.zeros_like(n_scr)
            m_scr[...] = jnp.zeros_like(m_scr)

        pmv = pm_ref[...]
        rr = lax.broadcasted_iota(jnp.int32, (CHUNK, CHUNK), 0)
        cc = lax.broadcasted_iota(jnp.int32, (CHUNK, CHUNK), 1)
        tril, eye = cc <= rr, cc == rr
        _, _, li, lf, _ = _gate_tiles(pmv[:, GATE_COL:GATE_COL + 128], b_ref[...], ci * CHUNK)
        nwv = nw_ref[...]
        for h in range(HEADS):
            cst, nst, mst = c_scr[h], n_scr[h], m_scr[h]
            cs_ref[h] = cst
            ns_ref[h] = nst
            ms_ref[h] = mst
            f = _chunk_common(pmv, h, li, lf, cst, nst, mst, tril, eye)
            num = f["w_inter"] * f["qc"] + _dot(f["s"].astype(BF16), f["v"], NN)
            hh = num / f["dn"]
            c_scr[h] = f["w_old"] * cst + _dot(f["kw"], f["v"], TN)
            n_scr[h] = f["w_old"] * nst + jnp.sum(
                f["w_in"].astype(BF16).astype(F32) * f["k"].astype(F32), axis=0, keepdims=True)
            m_scr[h] = f["m_new"]
            sl = slice(h * DV, (h + 1) * DV)
            rs = lax.rsqrt(jnp.mean(hh * hh, axis=1, keepdims=True) + EPS)
            og = pmv[:, 2 * QK_W + MLSTM_W + h * DV:2 * QK_W + MLSTM_W + (h + 1) * DV]
            ht_ref[:, sl] = hh
            hm_ref[:, sl] = (_sigmoid(og) * (hh * rs * nwv[:, sl])).astype(BF16)

    return pl.pallas_call(
        body, name=name, grid=(nc,),
        in_specs=[pl.BlockSpec((CHUNK, PM_W), lambda i: (i, 0)), pl.BlockSpec((1, 128), lambda i: (0, 0)),
                  pl.BlockSpec((1, MLSTM_W), lambda i: (0, 0))],
        out_specs=[pl.BlockSpec((CHUNK, MLSTM_W), lambda i: (i, 0)),
                   pl.BlockSpec((CHUNK, MLSTM_W), lambda i: (i, 0)),
                   pl.BlockSpec((None, HEADS, DQK, DV), lambda i: (i, 0, 0, 0)),
                   pl.BlockSpec((None, HEADS, 1, DQK), lambda i: (i, 0, 0, 0)),
                   pl.BlockSpec((None, HEADS, 1, 1), lambda i: (i, 0, 0, 0))],
        out_shape=[jax.ShapeDtypeStruct((r, MLSTM_W), BF16), jax.ShapeDtypeStruct((r, MLSTM_W), F32),
                   jax.ShapeDtypeStruct((nc, HEADS, DQK, DV), F32),
                   jax.ShapeDtypeStruct((nc, HEADS, 1, DQK), F32),
                   jax.ShapeDtypeStruct((nc, HEADS, 1, 1), F32)],
        scratch_shapes=[pltpu.VMEM((HEADS, DQK, DV), F32), pltpu.VMEM((HEADS, 1, DQK), F32),
                        pltpu.VMEM((HEADS, 1, 1), F32)],
        compiler_params=_params(),
    )(pm, bias, nw)


def _mlstm_bwd(name, dcat, pm, ht, cs, ns, ms, bias, nw):
    r = pm.shape[0]
    nc = r // CHUNK

    def body(dy_ref, pm_ref, ht_ref, cs_ref, ns_ref, ms_ref, b_ref, nw_ref, dpm_ref, dnw_ref, db_ref,
             dc_scr, dn_scr):
        step = pl.program_id(0)
        ci = nc - 1 - step

        @pl.when(step == 0)
        def _():
            dc_scr[...] = jnp.zeros_like(dc_scr)
            dn_scr[...] = jnp.zeros_like(dn_scr)
            dnw_ref[...] = jnp.zeros_like(dnw_ref)
            db_ref[...] = jnp.zeros_like(db_ref)

        pmv = pm_ref[...]
        rr = lax.broadcasted_iota(jnp.int32, (CHUNK, CHUNK), 0)
        cc = lax.broadcasted_iota(jnp.int32, (CHUNK, CHUNK), 1)
        tril, eye, triu = cc <= rr, cc == rr, cc >= rr
        th, z, li, lf, real = _gate_tiles(pmv[:, GATE_COL:GATE_COL + 128], b_ref[...], ci * CHUNK)
        lane = lax.broadcasted_iota(jnp.int32, (CHUNK, 128), 1)
        rowid = lax.broadcasted_iota(jnp.int32, (CHUNK, 1), 0)
        nwv = nw_ref[...]
        dgt = jnp.zeros((CHUNK, 128), F32)
        for h in range(HEADS):
            cst, nst, mst = cs_ref[h], ns_ref[h], ms_ref[h]
            f = _chunk_common(pmv, h, li, lf, cst, nst, mst, tril, eye)
            q, k, v, s, p = f["q"], f["k"], f["v"], f["s"], f["p"]
            w_inter, w_in, w_old, dn = f["w_inter"], f["w_in"], f["w_old"], f["dn"]
            sl = slice(h * DV, (h + 1) * DV)
            osl = slice(2 * QK_W + MLSTM_W + h * DV, 2 * QK_W + MLSTM_W + (h + 1) * DV)
            hh = ht_ref[:, sl]
            y = dy_ref[:, sl]
            sg = _sigmoid(pmv[:, osl])
            rs = lax.rsqrt(jnp.mean(hh * hh, axis=1, keepdims=True) + EPS)
            nwh = nwv[:, sl]
            dpm_ref[:, osl] = (y * (hh * rs * nwh) * sg * (1.0 - sg)).astype(BF16)
            dhn = y * sg
            dnw_ref[:, sl] += jnp.sum(dhn * hh * rs, axis=0, keepdims=True)
            wd = dhn * nwh
            dhh = rs * wd - hh * (rs * rs * rs) * jnp.mean(hh * wd, axis=1, keepdims=True)
            dnum = dhh / dn
            dd = -jnp.sum(dhh * hh, axis=1, keepdims=True) / dn
            dden = jnp.where(jnp.abs(f["den"]) > jnp.exp(-f["mt"]), dd * jnp.sign(f["den"]), 0.0)
            dnum_b = dnum.astype(BF16)
            wdn = (w_inter * dnum).astype(BF16)
            wid = (w_inter * dden).astype(BF16).astype(F32)
            ds = _dot(dnum_b, v, NT) + dden
            dsp = (ds * p).astype(BF16)
            dq = _dot(dsp, k, NN) + _dot(wdn, f["cb"], NT) + wid * f["nb"]
            dk = _dot(dsp, q, TN)
            dv = _dot(s.astype(BF16), dnum_b, TN)
            g = ds * s
            g_col = _row_to_col(jnp.sum(g, axis=0, keepdims=True), eye)
            db = jnp.sum(g, axis=1, keepdims=True) - g_col
            dli = g_col
            db = db + (jnp.sum(dnum * f["qc"], axis=1, keepdims=True) + dden * f["qn"]) * w_inter
            dcn, dnn = dc_scr[h], dn_scr[h]
            dcnb = dcn.astype(BF16)
            dnnb = dnn.astype(BF16).astype(F32)
            dkw = _dot(v, dcnb, NT) + dnnb
            dk = dk + w_in * dkw
            dv = dv + _dot(f["kw"], dcnb, NN)
            ddecay = jnp.sum(dkw * f["kraw"], axis=1, keepdims=True) * w_in
            dw_old = (jnp.sum(jnp.sum(dcn * cst, axis=1, keepdims=True), axis=0, keepdims=True)
                      + jnp.sum(dnn * nst, axis=1, keepdims=True))
            db_end = dw_old * w_old + jnp.sum(ddecay, axis=0, keepdims=True)
            db = db - ddecay + jnp.where(rowid == CHUNK - 1, db_end, 0.0)
            dli = dli + ddecay
            dc_scr[h] = w_old * dcn + _dot(q, wdn, TN)
            dn_scr[h] = w_old * dnn + jnp.sum(wid * q.astype(F32), axis=0, keepdims=True)
            dlf = jnp.sum(jnp.where(triu, _col_to_row(db, eye), 0.0), axis=1, keepdims=True)
            dgt = dgt + jnp.where(lane == h, dli, 0.0) + jnp.where(lane == HEADS + h, dlf, 0.0)
            dpm_ref[:, h * DQK:(h + 1) * DQK] = (dq * QSCALE).astype(BF16)
            dpm_ref[:, QK_W + h * DQK:QK_W + (h + 1) * DQK] = dk.astype(BF16)
            dpm_ref[:, 2 * QK_W + h * DV:2 * QK_W + (h + 1) * DV] = dv.astype(BF16)
        dact = jnp.where(lane < HEADS, 1.0, 1.0 - _sigmoid(z)) * (1.0 - th * th)
        dgraw = jnp.where(real & (lane < 2 * HEADS), dgt * dact, 0.0)
        dpm_ref[:, GATE_COL:GATE_COL + 128] = dgraw.astype(BF16)
        db_ref[...] += jnp.sum(dgraw, axis=0, keepdims=True)

    rev = lambda i: (nc - 1 - i, 0)
    rev4 = lambda i: (nc - 1 - i, 0, 0, 0)
    return pl.pallas_call(
        body, name=name, grid=(nc,),
        in_specs=[pl.BlockSpec((CHUNK, MLSTM_W), rev), pl.BlockSpec((CHUNK, PM_W), rev),
                  pl.BlockSpec((CHUNK, MLSTM_W), rev),
                  pl.BlockSpec((None, HEADS, DQK, DV), rev4), pl.BlockSpec((None, HEADS, 1, DQK), rev4),
                  pl.BlockSpec((None, HEADS, 1, 1), rev4),
                  pl.BlockSpec((1, 128), lambda i: (0, 0)), pl.BlockSpec((1, MLSTM_W), lambda i: (0, 0))],
        out_specs=[pl.BlockSpec((CHUNK, PM_W), rev), pl.BlockSpec((1, MLSTM_W), lambda i: (0, 0)),
                   pl.BlockSpec((1, 128), lambda i: (0, 0))],
        out_shape=[jax.ShapeDtypeStruct((r, PM_W), BF16), jax.ShapeDtypeStruct((1, MLSTM_W), F32),
                   jax.ShapeDtypeStruct((1, 128), F32)],
        scratch_shapes=[pltpu.VMEM((HEADS, DQK, DV), F32), pltpu.VMEM((HEADS, 1, DQK), F32)],
        compiler_params=_params(),
    )(dcat, pm, ht, cs, ns, ms, bias, nw)


def _my_place():
    return lax.axis_index("x"), lax.axis_index("y"), lax.axis_index("c")


def _flip(v, bit):
    return 1 - v if bit else v


def _exchange_small(name, blk, reduce):
    r, c = blk.shape

    def body(x_ref, o_ref, *rest):
        slots = rest[0] if reduce else o_ref
        send_sems, recv_sems = rest[-2], rest[-1]
        x, y, cc = _my_place()
        me = 4 * x + 2 * y + cc
        slots[me] = x_ref[...]
        copies = []
        for k in range(1, N_DEV):
            peer = (_flip(x, k & 4), _flip(y, k & 2), _flip(cc, k & 1))
            cp = pltpu.make_async_remote_copy(
                src_ref=x_ref, dst_ref=slots.at[me], send_sem=send_sems.at[k - 1],
                recv_sem=recv_sems.at[k - 1], device_id=peer, device_id_type=MESH)
            cp.start()
            copies.append(cp)
        for cp in copies:
            cp.wait()
        if reduce:
            acc = slots[0]
            for d in range(1, N_DEV):
                acc = acc + slots[d]
            o_ref[...] = acc

    scratch = ([pltpu.VMEM((N_DEV, r, c), F32)] if reduce else []) + [
        pltpu.SemaphoreType.DMA((N_DEV - 1,)), pltpu.SemaphoreType.DMA((N_DEV - 1,))]
    return pl.pallas_call(
        body, name=name,
        out_shape=jax.ShapeDtypeStruct((r, c) if reduce else (N_DEV, r, c), F32),
        in_specs=[pl.BlockSpec(memory_space=pltpu.VMEM)], out_specs=pl.BlockSpec(memory_space=pltpu.VMEM),
        scratch_shapes=scratch, compiler_params=_params(),
    )(blk)


HBM_SPEC = pl.BlockSpec(memory_space=pltpu.HBM)
SEM_SPEC = pl.BlockSpec(memory_space=pltpu.SEMAPHORE)
ANY_SPEC = pl.BlockSpec(memory_space=pl.ANY)
DATAFLOW = pltpu.SideEffectType.DATAFLOW_SIDE_EFFECTING


def _split_copy(name, arrays, start=None, wait=None, after=None):
    n = len(arrays)
    n_wait = 2 if wait else 0
    n_after = 0 if after is None else 1
    n_new = 2 if start else 0

    def body(*refs):
        ins = refs[:n]
        if wait:
            for cp in wait[0](ins, refs[n], refs[n + 1]):
                cp.wait_send()
                cp.wait_recv()
        if start:
            at = n + n_wait + n_after
            for cp in start[0](ins, refs[at], refs[at + 1]):
                cp.start()
            token = refs[at + 2 + n]
            token[...] = jnp.zeros_like(token)

    operands = [pltpu.with_memory_space_constraint(a, pltpu.HBM) for a in arrays]
    in_specs = [HBM_SPEC] * n
    if wait:
        operands += list(wait[1])
        in_specs += [SEM_SPEC, SEM_SPEC]
    if after is not None:
        operands.append(after)
        in_specs.append(ANY_SPEC)
    out_shape, out_specs = [], []
    if start:
        out_shape += [pltpu.SemaphoreType.DMA((start[1],)), pltpu.SemaphoreType.DMA((start[1],))]
        out_specs += [SEM_SPEC, SEM_SPEC]
    out_shape += [pltpu.HBM(a.shape, a.dtype) for a in arrays]
    out_specs += [HBM_SPEC] * n
    if start:
        out_shape.append(jax.ShapeDtypeStruct((8, 128), F32))
        out_specs.append(pl.BlockSpec(memory_space=pltpu.VMEM))
    outs = pl.pallas_call(
        body, name=name, in_specs=in_specs, out_specs=out_specs, out_shape=out_shape,
        input_output_aliases={i: n_new + i for i in range(n)},
        compiler_params=pltpu.CompilerParams(has_side_effects=DATAFLOW),
    )(*operands)
    thru = list(outs[n_new:n_new + n])
    return thru, (tuple(outs[:2]) if start else None), (outs[n_new + n] if start else None)


def _remote(src, dst, send_sems, recv_sems, k, to):
    return pltpu.make_async_remote_copy(src_ref=src, dst_ref=dst, send_sem=send_sems.at[k],
                                        recv_sem=recv_sems.at[k], device_id=to, device_id_type=MESH)


def _gather_first(n):
    def copies(refs, send_sems, recv_sems):
        x, y, c = _my_place()
        targets = [(x, y, 1 - c), (1 - x, y, c), (x, 1 - y, c), (1 - x, 1 - y, c)]
        out = []
        for a in range(n):
            blk = refs[a].at[4 * x + 2 * y + c]
            out += [_remote(blk, blk, send_sems, recv_sems, 4 * a + k, to) for k, to in enumerate(targets)]
        return out
    return copies


def _gather_pass(n):
    def copies(refs, send_sems, recv_sems):
        x, y, c = _my_place()
        out = []
        for a in range(n):
            for j, (px, py) in enumerate([(1 - x, y), (x, 1 - y), (1 - x, 1 - y)]):
                blk = refs[a].at[4 * px + 2 * py + c]
                out.append(_remote(blk, blk, send_sems, recv_sems, 3 * a + j, (x, y, 1 - c)))
        return out
    return copies


def _scatter_sibling(n):
    def copies(refs, send_sems, recv_sems):
        x, y, c = _my_place()
        return [_remote(refs[a].at[2 * j + 1 - c], refs[n + a].at[j], send_sems, recv_sems, 4 * a + j, (x, y, 1 - c))
                for a in range(n) for j in range(4)]
    return copies


def _scatter_chips(n):
    def copies(refs, send_sems, recv_sems):
        x, y, c = _my_place()
        out = []
        for a in range(n):
            for k in range(1, 4):
                px, py = _flip(x, k & 2), _flip(y, k & 1)
                out.append(_remote(refs[a].at[2 * px + py], refs[n + a].at[2 * x + y], send_sems, recv_sems,
                                   3 * a + k - 1, (px, py, c)))
        return out
    return copies


def _pair_sum(name, core, g, t):
    _, r, c = g.shape
    tr = _tile(r, 512, 8)
    g4 = g.reshape(4, 2, r, c)

    def body(core_ref, g_ref, t_ref, o_ref):
        o_ref[...] = (g_ref[...].astype(F32) + t_ref[...].astype(F32)).astype(BF16)

    return pl.pallas_call(
        body, name=name,
        grid_spec=pltpu.PrefetchScalarGridSpec(
            num_scalar_prefetch=1, grid=(4, r // tr),
            in_specs=[pl.BlockSpec((None, None, tr, c), lambda j, i, core_ref: (j, core_ref[0], i, 0)),
                      pl.BlockSpec((None, tr, c), lambda j, i, core_ref: (j, i, 0))],
            out_specs=pl.BlockSpec((None, tr, c), lambda j, i, core_ref: (j, i, 0))),
        out_shape=jax.ShapeDtypeStruct((4, r, c), BF16), compiler_params=_params(),
    )(core, g4, t)


def _adam_math(w, g, m, v):
    m2 = ADAM_B1 * m + (1.0 - ADAM_B1) * g
    v2 = ADAM_B2 * v + (1.0 - ADAM_B2) * (g * g)
    m_hat = m2 / (1.0 - ADAM_B1 ** ADAM_STEP)
    v_hat = v2 / (1.0 - ADAM_B2 ** ADAM_STEP)
    delta = -ADAM_LR * (m_hat / (jnp.sqrt(v_hat) + ADAM_EPS) + ADAM_WD * w)
    return delta, m2, v2


def _adam_sharded(name, chip, w, m, v, grads, row_off=0):
    _, r, c = w.shape
    tr = _tile(r, 256, 8)
    tc = c if tr < r else _tile(c, 256, 128)
    boff = row_off // tr

    def body(chip_ref, w_ref, m_ref, v_ref, p0_ref, q0_ref, p1_ref, q1_ref, g_ref, d_ref, nm_ref, nv_ref):
        mine = chip_ref[0]

        def total(p_ref, q_ref):
            acc = None
            for j in range(4):
                part = jnp.where(mine == j, p_ref[...], q_ref[j]).astype(F32)
                acc = part if acc is None else acc + part
            return acc

        g = jnp.where(pl.program_id(0) == 0, total(p0_ref, q0_ref), total(p1_ref, q1_ref))
        delta, m2, v2 = _adam_math(w_ref[...], g, m_ref[...], v_ref[...])
        g_ref[...] = g
        d_ref[...] = delta
        nm_ref[...] = m2
        nv_ref[...] = v2

    def grad_specs(layer):
        at = lambda l, i, j: (jnp.where(l == layer, boff + i, boff), jnp.where(l == layer, j, 0))
        return [pl.BlockSpec((None, tr, tc), lambda l, i, j, chip_ref: (chip_ref[0],) + at(l, i, j)),
                pl.BlockSpec((4, tr, tc), lambda l, i, j, chip_ref: (0,) + at(l, i, j))]

    wspec = pl.BlockSpec((None, tr, tc), lambda l, i, j, chip_ref: (l, i, j))
    sds = jax.ShapeDtypeStruct(w.shape, F32)
    return pl.pallas_call(
        body, name=name,
        grid_spec=pltpu.PrefetchScalarGridSpec(
            num_scalar_prefetch=1, grid=(2, r // tr, c // tc),
            in_specs=[wspec, wspec, wspec] + grad_specs(0) + grad_specs(1), out_specs=[wspec] * 4),
        out_shape=[sds] * 4, compiler_params=_params(),
    )(chip, w, m, v, grads[0][0], grads[0][1], grads[1][0], grads[1][1])


def _adam_small(name, w, m, v, g):
    def body(w_ref, m_ref, v_ref, g_ref, d_ref, nm_ref, nv_ref):
        delta, m2, v2 = _adam_math(w_ref[...], g_ref[...], m_ref[...], v_ref[...])
        d_ref[...] = delta
        nm_ref[...] = m2
        nv_ref[...] = v2

    sds = jax.ShapeDtypeStruct(w.shape, F32)
    vm = pl.BlockSpec(memory_space=pltpu.VMEM)
    return pl.pallas_call(body, name=name, in_specs=[vm] * 4, out_specs=[vm] * 3, out_shape=[sds] * 3,
                          compiler_params=_params())(w, m, v, g)


GATE_END = GATE_COL + 2 * HEADS


def _merge_dw_in(dwm_t, dwc_t):
    full = jnp.concatenate([dwm_t[:GATE_END], dwc_t.reshape(3 * CONV_W, D_MODEL)], axis=0)
    return full.reshape(N_DEV, IN_SH, D_MODEL)


def _pack128(parts):
    flat = jnp.concatenate([p.reshape(-1) for p in parts])
    n = flat.shape[0]
    rows = -(-n // 1024) * 8
    return jnp.pad(flat, (0, rows * 128 - n)).reshape(rows, 128)


def _unpack128(packed, shapes):
    flat = packed.reshape(-1)
    out, at = [], 0
    for s in shapes:
        n = int(np.prod(s))
        out.append(flat[at:at + n].reshape(s))
        at += n
    return out


def kernel(x, meta_tokens, norm_mix_w, w_in, b_gates, conv_w, mlstm_norm_w, w_out, norm_ffn_w, w_gate, w_up, w_down, norm_final_w, loss_target, m_meta_tokens, m_norm_mix_w, m_w_in, m_b_gates, m_conv_w, m_mlstm_norm_w, m_w_out, m_norm_ffn_w, m_w_gate, m_w_up, m_w_down, m_norm_final_w, v_meta_tokens, v_norm_mix_w, v_w_in, v_b_gates, v_conv_w, v_mlstm_norm_w, v_w_out, v_norm_ffn_w, v_w_gate, v_w_up, v_w_down, v_norm_final_w):
    seq = x.shape[1]
    rows = TOK0 + seq
    me = 4 * lax.axis_index("x") + 2 * lax.axis_index("y") + lax.axis_index("c")
    meta_sh = meta_tokens.shape[1]
    conv_sh = conv_w.shape[2]

    small = jnp.concatenate(
        [meta_tokens, jnp.pad(conv_w.reshape(DEPTH * 3, conv_sh), ((0, 2), (0, meta_sh - conv_sh)))], axis=0)
    slots = _exchange_small("gather_small", small, reduce=False)
    meta_full = jnp.transpose(slots[:, :N_META, :], (1, 0, 2)).reshape(N_META, D_MODEL)
    conv_full = jnp.transpose(slots[:, N_META:N_META + DEPTH * 3, :conv_sh], (1, 0, 2)).reshape(DEPTH, 3, CONV_W)
    conv_rows = [jnp.pad(conv_full[l], ((0, 5), (0, 0))) for l in range(DEPTH)]

    w_in_t, m_w_in_t, v_w_in_t = (jnp.transpose(a, (0, 2, 1)) for a in (w_in, m_w_in, v_w_in))
    w_gate_t, m_w_gate_t, v_w_gate_t = (jnp.transpose(a, (0, 2, 1)) for a in (w_gate, m_w_gate, v_w_gate))
    w_up_t, m_w_up_t, v_w_up_t = (jnp.transpose(a, (0, 2, 1)) for a in (w_up, m_w_up, v_w_up))
    shards = []
    for l in range(DEPTH):
        shards += [w_in_t[l].astype(BF16), w_out[l].astype(BF16), w_gate_t[l].astype(BF16),
                   w_up_t[l].astype(BF16), w_down[l].astype(BF16)]
    per_layer = ("w_in", "w_out", "w_gate", "w_up", "w_down")
    gather_names = [f"{nm}_{l}" for l in range(DEPTH) for nm in per_layer]
    gather_state = {}

    def gather_start(i, after=None):
        buf = lax.dynamic_update_index_in_dim(lax.empty((N_DEV,) + shards[i].shape, BF16), shards[i], me, 0)
        arrs, sems, tok = _split_copy(f"gather_start_{gather_names[i]}", [buf], start=(_gather_first(1), 4),
                                      after=after)
        gather_state[i] = (arrs, sems)
        return tok

    def gather_pass(i, after):
        arrs, sems = gather_state[i]
        arrs, sems, tok = _split_copy(f"gather_pass_{gather_names[i]}", arrs, start=(_gather_pass(1), 3),
                                      wait=(_gather_first(1), sems), after=after)
        gather_state[i] = (arrs, sems)
        return tok

    def gathered(i, after):
        arrs, sems = gather_state[i]
        arrs, _, _ = _split_copy(f"gather_done_{gather_names[i]}", arrs, wait=(_gather_pass(1), sems), after=after)
        return arrs[0]

    bias = [jnp.pad(b_gates[l].reshape(1, 2 * HEADS), ((0, 0), (0, 128 - 2 * HEADS))) for l in range(DEPTH)]
    nmix = [norm_mix_w[l].reshape(1, D_MODEL) for l in range(DEPTH)]
    nffn = [norm_ffn_w[l].reshape(1, D_MODEL) for l in range(DEPTH)]
    nmls = [mlstm_norm_w[l].reshape(1, MLSTM_W) for l in range(DEPTH)]
    weights = [dict() for _ in range(DEPTH)]
    saved = [dict() for _ in range(DEPTH)]

    def layer_fwd(l, h, tok):
        w, s = weights[l], saved[l]
        first = len(per_layer) * l
        w["win_t"] = gathered(first, tok).reshape(D_IN, D_MODEL)
        w["wc_t"] = w["win_t"][GATE_END:].reshape(3, CONV_W, D_MODEL)
        s["h0"] = h
        s["hn"] = _rms_fwd(f"norm_mix_{l}", h, nmix[l])
        s["pm"] = _mm_nt(f"proj_mlstm_{l}", s["hn"], w["win_t"], F32, tn=640, tk=D_MODEL, n=PM_W)
        tok = gather_pass(first + 1, s["pm"])
        s["pc"] = _mm_nt_bcols(f"proj_conv_{l}", s["hn"], w["wc_t"], F32, dep=tok)
        tok = gather_pass(first + 2, s["pc"])
        hm, s["ht"], s["cs"], s["ns"], s["ms"] = _mlstm_fwd(f"mlstm_fwd_{l}", s["pm"], bias[l] + tok[:1], nmls[l])
        tok = gather_pass(first + 3, hm)
        hc = _conv_fwd(f"conv_fwd_{l}", s["pc"], conv_rows[l] + tok[0, 0])
        s["cat"] = jnp.concatenate([hm, hc], axis=1)
        w["wo"] = gathered(first + 1, s["cat"]).reshape(D_MODEL, D_MODEL)
        s["h1"] = _mm_nn(f"out_proj_{l}", s["cat"], w["wo"], F32, res=s["h0"])
        tok = gather_pass(first + 4, s["h1"])
        s["hf"] = _rms_fwd(f"norm_ffn_{l}", s["h1"], nffn[l])
        w["wg_t"] = gathered(first + 2, s["hf"]).reshape(D_FF, D_MODEL)
        w["wu_t"] = gathered(first + 3, s["hf"]).reshape(D_FF, D_MODEL)
        s["g"], s["u"], s["act"] = _ffn_in(f"ffn_in_{l}", s["hf"], w["wg_t"], w["wu_t"], dep=tok)
        tok = gather_pass(first + 5, s["act"]) if l + 1 < DEPTH else None
        w["wd"] = gathered(first + 4, s["act"]).reshape(D_FF, D_MODEL)
        return _mm_nn(f"ffn_out_{l}", s["act"], w["wd"], F32, res=s["h1"], dep=tok)

    tok = None
    for i in range(len(shards)):
        tok = gather_start(i, after=tok)
    h = jnp.concatenate([jnp.zeros((PAD_FRONT, D_MODEL), F32), meta_full, x[0]], axis=0)
    h = layer_fwd(0, h, gather_pass(0, tok))
    h = layer_fwd(1, h, h)

    dh, dh_b, d_final, loss_part = _final_loss("final_loss", h, norm_final_w.reshape(1, D_MODEL), loss_target[0])

    core = lax.axis_index("c").astype(jnp.int32).reshape(1)
    chip = (2 * lax.axis_index("x") + lax.axis_index("y")).astype(jnp.int32).reshape(1)
    scatter_state = {}

    def scatter_begin(nm, grad):
        land = lax.empty((4,) + grad.shape[1:], BF16)
        arrs, sems, tok = _split_copy(f"grad_sibling_start_{nm}", [grad, land], start=(_scatter_sibling(1), 4))
        scatter_state[nm] = (arrs, sems)
        return tok

    def scatter_advance(nm, after):
        arrs, sems = scatter_state[nm]
        arrs, _, _ = _split_copy(f"grad_sibling_done_{nm}", arrs, wait=(_scatter_sibling(1), sems), after=after)
        part = _pair_sum(f"grad_pair_sum_{nm}", core, arrs[0], arrs[1])
        arrs, sems, tok = _split_copy(f"grad_chips_start_{nm}", [part, lax.empty(part.shape, BF16)],
                                      start=(_scatter_chips(1), 3))
        scatter_state[nm] = (arrs, sems)
        return tok

    def scattered(nm, after):
        arrs, sems = scatter_state[nm]
        arrs, _, _ = _split_copy(f"grad_chips_done_{nm}", arrs, wait=(_scatter_chips(1), sems), after=after)
        return arrs[0], arrs[1]

    d_mix, d_ffn, d_mls, d_bias, d_conv = ([None] * DEPTH for _ in range(5))

    def layer_bwd(l, dh, dh_b, tok):
        w, s = weights[l], saved[l]
        dg, du = _ffn_act_bwd(f"d_act_{l}", dh_b, w["wd"], s["g"], s["u"], dep=tok)
        dw_down = _mm_tn(f"dw_down_{l}", s["act"], dh_b, BF16, tm=1408, tn=1024)
        tok = scatter_begin(f"w_down_{l}", dw_down.reshape(N_DEV, FF_SH, D_MODEL))
        dhf = _mm_nn(f"d_ffn_gate_{l}", dg, w["wg_t"], F32, dep=tok)
        tok = scatter_advance(f"w_down_{l}", after=dhf)
        dhf = _mm_nn(f"d_ffn_up_{l}", du, w["wu_t"], F32, res=dhf, dep=tok)
        dw_gate = _mm_tn(f"dw_gate_{l}", dg, s["hf"], BF16, tm=1408, tn=1024)
        tok = scatter_begin(f"w_gate_{l}", dw_gate.reshape(N_DEV, FF_SH, D_MODEL))
        dw_up = _mm_tn(f"dw_up_{l}", du, s["hf"], BF16, tm=1408, tn=1024, dep=tok)
        tok = scatter_begin(f"w_up_{l}", dw_up.reshape(N_DEV, FF_SH, D_MODEL))
        dh1, dh1_b, d_ffn[l] = _rms_bwd(f"norm_ffn_bwd_{l}", s["h1"], nffn[l] + tok[0, 0], dhf, dh)
        tok = scatter_advance(f"w_gate_{l}", after=dh1)
        dcat = _mm_nt(f"d_cat_{l}", dh1_b, w["wo"], F32, tk=D_MODEL, dep=tok)
        tok = scatter_advance(f"w_up_{l}", after=dcat)
        dw_out = _mm_tn(f"dw_out_{l}", s["cat"], dh1_b, BF16, tn=1024, dep=tok)
        tok = scatter_begin(f"w_out_{l}", dw_out.reshape(N_DEV, OUT_SH, D_MODEL))
        dpm, d_mls[l], d_bias[l] = _mlstm_bwd(f"mlstm_bwd_{l}", dcat, s["pm"], s["ht"], s["cs"], s["ns"],
                                               s["ms"], bias[l] + tok[:1], nmls[l])
        dpc, d_conv[l] = _conv_bwd(f"conv_bwd_{l}", dcat, s["pc"], conv_rows[l])
        tok = scatter_advance(f"w_out_{l}", after=dpc)
        dwm_t = _mm_tn(f"dw_mlstm_{l}", dpm, s["hn"], BF16, tm=640, tn=1024, dep=tok)
        dwc_t = _mm_tn_acols(f"dw_conv_{l}", dpc, s["hn"], BF16)
        tok = scatter_begin(f"w_in_{l}", _merge_dw_in(dwm_t, dwc_t))
        dhn = _mm_nn_kt(f"d_norm_mlstm_{l}", dpm, w["win_t"], F32, tk=PM_W, dep=tok)
        dhn = _mm_nn_ksum(f"d_norm_conv_{l}", dpc, w["wc_t"], F32, res=dhn)
        tok = scatter_advance(f"w_in_{l}", after=dhn)
        dh, dh_b, d_mix[l] = _rms_bwd(f"norm_mix_bwd_{l}", s["h0"], nmix[l] + tok[0, 0], dhn, dh1)
        return dh, dh_b, tok

    dh, dh_b, tok = layer_bwd(1, dh, dh_b, None)
    dh, dh_b, tok_tail = layer_bwd(0, dh, dh_b, tok)

    pq = {}
    after = dh
    for l in reversed(range(DEPTH)):
        for nm in ("w_down", "w_gate", "w_up", "w_out", "w_in"):
            if (nm, l) != ("w_in", 0):
                pq[nm, l] = scattered(f"{nm}_{l}", after)
                after = pq[nm, l][0]
    untransposed = lambda outs: [jnp.transpose(o, (0, 2, 1)) for o in outs]
    g_out, d_out, nm_out, nv_out = _adam_sharded(
        "adam_w_out", chip, w_out, m_w_out, v_w_out, [pq["w_out", 0], pq["w_out", 1]])
    g_gate, d_gate, nm_gate, nv_gate = untransposed(_adam_sharded(
        "adam_w_gate", chip, w_gate_t, m_w_gate_t, v_w_gate_t, [pq["w_gate", 0], pq["w_gate", 1]]))
    g_up, d_up, nm_up, nv_up = untransposed(_adam_sharded(
        "adam_w_up", chip, w_up_t, m_w_up_t, v_w_up_t, [pq["w_up", 0], pq["w_up", 1]]))
    g_down, d_down, nm_down, nv_down = _adam_sharded(
        "adam_w_down", chip, w_down, m_w_down, v_w_down, [pq["w_down", 0], pq["w_down", 1]])
    pq["w_in", 0] = scattered("w_in_0", nv_down)
    g_in, d_in, nm_in, nv_in = untransposed(_adam_sharded(
        "adam_w_in", chip, w_in_t, m_w_in_t, v_w_in_t, [pq["w_in", 0], pq["w_in", 1]]))

    bg = jnp.concatenate([d_bias[l][0, :2 * HEADS] for l in range(DEPTH)])
    red_in = jnp.concatenate([
        dh[PAD_FRONT:TOK0], d_mix[0], d_mix[1], d_ffn[0], d_ffn[1], d_final,
        jnp.concatenate([d_mls[0], d_mls[1]], axis=1),
        jnp.stack([d_conv[l][:3] for l in range(DEPTH)]).reshape(3, 2 * CONV_W),
        jnp.pad(bg, (0, D_MODEL - bg.shape[0])).reshape(1, D_MODEL),
        jnp.pad(loss_part[:, :1], ((0, 0), (0, D_MODEL - 1))),
        jnp.zeros((5, D_MODEL), F32) + tok_tail[0, 0]], axis=0)
    red = _exchange_small("reduce_small", red_in, reduce=True)
    loss = red[26, 0]
    g_meta = lax.dynamic_slice_in_dim(red[:N_META], me * meta_sh, meta_sh, axis=1)
    g_mix, g_ffn, g_final = red[16:18], red[18:20], red[20]
    g_mls = red[21].reshape(DEPTH, MLSTM_W)
    g_conv = lax.dynamic_slice_in_dim(red[22:25].reshape(DEPTH, 3, CONV_W), me * conv_sh, conv_sh, axis=2)
    g_bias = red[25, :DEPTH * 2 * HEADS].reshape(DEPTH, 2 * HEADS)

    small_w = [meta_tokens, norm_mix_w, b_gates, conv_w, mlstm_norm_w, norm_ffn_w, norm_final_w]
    small_m = [m_meta_tokens, m_norm_mix_w, m_b_gates, m_conv_w, m_mlstm_norm_w, m_norm_ffn_w, m_norm_final_w]
    small_v = [v_meta_tokens, v_norm_mix_w, v_b_gates, v_conv_w, v_mlstm_norm_w, v_norm_ffn_w, v_norm_final_w]
    small_g = [g_meta, g_mix, g_bias, g_conv, g_mls, g_ffn, g_final]
    shapes = [a.shape for a in small_w]
    packed = _adam_small("adam_small", _pack128(small_w), _pack128(small_m), _pack128(small_v), _pack128(small_g))
    (d_meta, d_nmix, d_bg, d_cw, d_nmls, d_nffn, d_nfin), (nm_meta, nm_nmix, nm_bg, nm_cw, nm_nmls, nm_nffn, nm_nfin), \
        (nv_meta, nv_nmix, nv_bg, nv_cw, nv_nmls, nv_nffn, nv_nfin) = (_unpack128(p, shapes) for p in packed)

    grad_x = dh[TOK0:].reshape(1, seq, D_MODEL)
    return (loss, grad_x,
            g_meta, g_mix, g_in, g_bias, g_conv, g_mls, g_out, g_ffn, g_gate, g_up, g_down, g_final,
            d_meta, d_nmix, d_in, d_bg, d_cw, d_nmls, d_out, d_nffn, d_gate, d_up, d_down, d_nfin,
            nm_meta, nm_nmix, nm_in, nm_bg, nm_cw, nm_nmls, nm_out, nm_nffn, nm_gate, nm_up, nm_down, nm_nfin,
            nv_meta, nv_nmix, nv_in, nv_bg, nv_cw, nv_nmls, nv_out, nv_nffn, nv_gate, nv_up, nv_down, nv_nfin)
```

```python
import functools

import numpy as np
import jax
import jax.numpy as jnp
from jax import lax
from jax.experimental import pallas as pl
from jax.experimental.pallas import tpu as pltpu

F32 = jnp.float32
BF16 = jnp.bfloat16
MESH = pl.DeviceIdType.MESH

D_MODEL = 2048
DEPTH = 2
N_META = 16
MLSTM_W = 1024
CONV_W = 1024
HEADS = 4
DV = 256
DQK = 128
QK_W = 512
CHUNK = 64
PAD_FRONT = 48
TOK0 = PAD_FRONT + N_META
D_FF = 5632
N_DEV = 8
FF_SH = D_FF // N_DEV
D_IN = 6152
IN_SH = D_IN // N_DEV
OUT_SH = D_MODEL // N_DEV
GATE_COL = 3072
PM_W = GATE_COL + 128
GATE_CAP = 15.0
EPS = 1e-6
QSCALE = DQK ** -0.5

ADAM_LR = 0.001
ADAM_B1 = 0.9
ADAM_B2 = 0.999
ADAM_EPS = 1e-08
ADAM_WD = 0.01
ADAM_STEP = 10

V7X_VMEM_LIMIT = 50 * 1024 * 1024


def _params(**kw):
    return pltpu.CompilerParams(vmem_limit_bytes=V7X_VMEM_LIMIT, **kw)


def _tile(n, target, mult):
    best = None
    for t in range(mult, min(n, target) + 1, mult):
        if n % t == 0:
            best = t
    return best if best is not None else n


def _sigmoid(x):
    return 1.0 / (1.0 + jnp.exp(-x))


NN = ((1,), (0,))
NT = ((1,), (1,))
TN = ((0,), (0,))


def _matmul(name, a, b, out_shape, out_dtype, grid, a_bs, b_bs, o_bs, dims, nk, acc_shape=None,
            res=None, res_bs=None, dep=None):
    has_res = res is not None
    n_in = 2 + has_res + (dep is not None)

    def body(*refs):
        a_ref, b_ref = refs[0], refs[1]
        r_ref = refs[2] if has_res else None
        o_ref = refs[n_in]
        x = lax.dot_general(a_ref[...], b_ref[...], (dims, ((), ())), preferred_element_type=F32)
        if nk == 1:
            if has_res:
                x = x + r_ref[...]
            o_ref[...] = x.astype(o_ref.dtype)
            return
        acc = refs[n_in + 1]
        k = pl.program_id(len(grid) - 1)

        @pl.when(k == 0)
        def _():
            acc[...] = (x + r_ref[...]) if has_res else x

        @pl.when(k > 0)
        def _():
            acc[...] += x

        @pl.when(k == nk - 1)
        def _():
            o_ref[...] = acc[...].astype(o_ref.dtype)

    ins = [a, b] + ([res] if has_res else [])
    specs = [a_bs, b_bs] + ([res_bs] if has_res else [])
    if dep is not None:
        ins.append(dep)
        specs.append(pl.BlockSpec((8, 128), lambda *_: (0, 0)))
    scratch = [pltpu.VMEM(acc_shape, F32)] if nk > 1 else []
    return pl.pallas_call(
        body, name=name, grid=grid, in_specs=specs, out_specs=o_bs,
        out_shape=jax.ShapeDtypeStruct(out_shape, out_dtype), scratch_shapes=scratch,
        compiler_params=_params(),
    )(*ins)


def _mm_nn(name, a, b, out_dtype, res=None, tm=1056, tn=512, dep=None):
    r, k = a.shape
    n = b.shape[1]
    tm, tn = _tile(r, tm, 8), _tile(n, tn, 128)
    return _matmul(name, a, b, (r, n), out_dtype, (r // tm, n // tn, 1),
                   pl.BlockSpec((tm, k), lambda i, j, s: (i, 0)),
                   pl.BlockSpec((k, tn), lambda i, j, s: (0, j)),
                   pl.BlockSpec((tm, tn), lambda i, j, s: (i, j)), NN, 1,
                   res=res, res_bs=pl.BlockSpec((tm, tn), lambda i, j, s: (i, j)), dep=dep)


def _mm_nn_kt(name, a, b, out_dtype, tm=1056, tn=1024, tk=640, dep=None):
    r, k = a.shape
    n = b.shape[1]
    tm, tn, tk = _tile(r, tm, 8), _tile(n, tn, 128), _tile(k, tk, 128)
    nk = k // tk
    return _matmul(name, a, b, (r, n), out_dtype, (r // tm, n // tn, nk),
                   pl.BlockSpec((tm, tk), lambda i, j, s: (i, s)),
                   pl.BlockSpec((tk, tn), lambda i, j, s: (s, j)),
                   pl.BlockSpec((tm, tn), lambda i, j, s: (i, j)), NN, nk, acc_shape=(tm, tn), dep=dep)


def _mm_nn_ksum(name, a3, b3, out_dtype, res=None, tm=1056, tn=1024, dep=None):
    e, r, kb = a3.shape
    n = b3.shape[2]
    tm, tn = _tile(r, tm, 8), _tile(n, tn, 128)
    return _matmul(name, a3, b3, (r, n), out_dtype, (r // tm, n // tn, e),
                   pl.BlockSpec((None, tm, kb), lambda i, j, s: (s, i, 0)),
                   pl.BlockSpec((None, kb, tn), lambda i, j, s: (s, 0, j)),
                   pl.BlockSpec((tm, tn), lambda i, j, s: (i, j)), NN, e, acc_shape=(tm, tn),
                   res=res, res_bs=pl.BlockSpec((tm, tn), lambda i, j, s: (i, j)), dep=dep)


def _mm_nt(name, a, b, out_dtype, res=None, tm=1056, tn=512, tk=640, n=None, dep=None):
    r, k = a.shape
    n = b.shape[0] if n is None else n
    tm, tn, tk = _tile(r, tm, 8), _tile(n, tn, 128), _tile(k, tk, 128)
    nk = k // tk
    return _matmul(name, a, b, (r, n), out_dtype, (r // tm, n // tn, nk),
                   pl.BlockSpec((tm, tk), lambda i, j, s: (i, s)),
                   pl.BlockSpec((tn, tk), lambda i, j, s: (j, s)),
                   pl.BlockSpec((tm, tn), lambda i, j, s: (i, j)), NT, nk, acc_shape=(tm, tn),
                   res=res, res_bs=pl.BlockSpec((tm, tn), lambda i, j, s: (i, j)), dep=dep)


def _mm_nt_bcols(name, a, b3, out_dtype, tm=1056, dep=None):
    r, k = a.shape
    e, n, _ = b3.shape
    tm = _tile(r, tm, 8)
    return _matmul(name, a, b3, (e, r, n), out_dtype, (r // tm, e, 1),
                   pl.BlockSpec((tm, k), lambda i, g, s: (i, 0)),
                   pl.BlockSpec((None, n, k), lambda i, g, s: (g, 0, 0)),
                   pl.BlockSpec((None, tm, n), lambda i, g, s: (g, i, 0)), NT, 1, dep=dep)


def _mm_tn(name, a, b, out_dtype, tm=1024, tn=640, dep=None):
    r, m = a.shape
    n = b.shape[1]
    tm, tn = _tile(m, tm, 128), _tile(n, tn, 128)
    return _matmul(name, a, b, (m, n), out_dtype, (m // tm, n // tn, 1),
                   pl.BlockSpec((r, tm), lambda i, j, s: (0, i)),
                   pl.BlockSpec((r, tn), lambda i, j, s: (0, j)),
                   pl.BlockSpec((tm, tn), lambda i, j, s: (i, j)), TN, 1, dep=dep)


def _mm_tn_acols(name, a3, b, out_dtype, tn=1024, dep=None):
    e, r, m = a3.shape
    n = b.shape[1]
    tn = _tile(n, tn, 128)
    return _matmul(name, a3, b, (e, m, n), out_dtype, (n // tn, e, 1),
                   pl.BlockSpec((None, r, m), lambda j, g, s: (g, 0, 0)),
                   pl.BlockSpec((r, tn), lambda j, g, s: (0, j)),
                   pl.BlockSpec((None, m, tn), lambda j, g, s: (g, 0, j)), TN, 1, dep=dep)


def _rms_fwd(name, h, w):
    r, d = h.shape
    tr = _tile(r, 264, 8)

    def body(h_ref, w_ref, o_ref):
        x = h_ref[...]
        rs = lax.rsqrt(jnp.mean(x * x, axis=1, keepdims=True) + EPS)
        o_ref[...] = (x * rs * w_ref[...]).astype(BF16)

    return pl.pallas_call(
        body, name=name, grid=(r // tr,),
        in_specs=[pl.BlockSpec((tr, d), lambda i: (i, 0)), pl.BlockSpec((1, d), lambda i: (0, 0))],
        out_specs=pl.BlockSpec((tr, d), lambda i: (i, 0)),
        out_shape=jax.ShapeDtypeStruct((r, d), BF16), compiler_params=_params(),
    )(h, w)


def _rms_bwd(name, x, w, dy, dres):
    r, d = x.shape
    tr = _tile(r, 264, 8)

    def body(x_ref, w_ref, dy_ref, dr_ref, dx_ref, dxb_ref, dw_ref):
        xv = x_ref[...]
        g = dy_ref[...]
        rs = lax.rsqrt(jnp.mean(xv * xv, axis=1, keepdims=True) + EPS)
        wg = g * w_ref[...]
        dx = rs * wg - xv * (rs * rs * rs) * jnp.mean(xv * wg, axis=1, keepdims=True) + dr_ref[...]
        dx_ref[...] = dx
        dxb_ref[...] = dx.astype(BF16)
        part = jnp.sum(g * xv * rs, axis=0, keepdims=True)

        @pl.when(pl.program_id(0) == 0)
        def _():
            dw_ref[...] = part

        @pl.when(pl.program_id(0) > 0)
        def _():
            dw_ref[...] += part

    row = pl.BlockSpec((tr, d), lambda i: (i, 0))
    vec = pl.BlockSpec((1, d), lambda i: (0, 0))
    return pl.pallas_call(
        body, name=name, grid=(r // tr,), in_specs=[row, vec, row, row], out_specs=[row, row, vec],
        out_shape=[jax.ShapeDtypeStruct((r, d), F32), jax.ShapeDtypeStruct((r, d), BF16),
                   jax.ShapeDtypeStruct((1, d), F32)],
        compiler_params=_params(),
    )(x, w, dy, dres)


def _final_loss(name, h, w, target):
    r, d = h.shape
    nb = r // CHUNK

    def body(h_ref, w_ref, t_ref, dh_ref, dhb_ref, dw_ref, ls_ref):
        i = pl.program_id(0)

        @pl.when(i == 0)
        def _():
            dh_ref[...] = jnp.zeros_like(dh_ref)
            dhb_ref[...] = jnp.zeros_like(dhb_ref)
            dw_ref[...] = jnp.zeros_like(dw_ref)
            ls_ref[...] = jnp.zeros_like(ls_ref)

        @pl.when(i > 0)
        def _():
            xv = h_ref[...]
            wv = w_ref[...]
            rs = lax.rsqrt(jnp.mean(xv * xv, axis=1, keepdims=True) + EPS)
            err = xv * rs * wv - t_ref[...]
            sq = jnp.sum(jnp.sum(err * err, axis=1, keepdims=True), axis=0, keepdims=True)
            ls_ref[...] += jnp.broadcast_to(sq * (0.5 / d), ls_ref.shape)
            g = err * (1.0 / d)
            wg = g * wv
            dx = rs * wg - xv * (rs * rs * rs) * jnp.mean(xv * wg, axis=1, keepdims=True)
            dh_ref[...] = dx
            dhb_ref[...] = dx.astype(BF16)
            dw_ref[...] += jnp.sum(g * xv * rs, axis=0, keepdims=True)

    row = pl.BlockSpec((CHUNK, d), lambda i: (i, 0))
    vec = pl.BlockSpec((1, d), lambda i: (0, 0))
    return pl.pallas_call(
        body, name=name, grid=(nb,),
        in_specs=[row, vec, pl.BlockSpec((CHUNK, d), lambda i: (jnp.maximum(i - 1, 0), 0))],
        out_specs=[row, row, vec, pl.BlockSpec((1, 128), lambda i: (0, 0))],
        out_shape=[jax.ShapeDtypeStruct((r, d), F32), jax.ShapeDtypeStruct((r, d), BF16),
                   jax.ShapeDtypeStruct((1, d), F32), jax.ShapeDtypeStruct((1, 128), F32)],
        compiler_params=_params(),
    )(h, w, target)


def _ffn_in(name, hf, wg_t, wu_t, dep=None, tm=1056, tn=512):
    r, d = hf.shape
    f = wg_t.shape[0]
    tm, tn = _tile(r, tm, 8), _tile(f, tn, 128)

    def body(h_ref, wg_ref, wu_ref, *rest):
        g_ref, u_ref, a_ref = rest[-3:]
        x = h_ref[...]
        g = lax.dot_general(x, wg_ref[...], (NT, ((), ())), preferred_element_type=F32)
        u = lax.dot_general(x, wu_ref[...], (NT, ((), ())), preferred_element_type=F32)
        g_ref[...] = g.astype(BF16)
        u_ref[...] = u.astype(BF16)
        a_ref[...] = (g * _sigmoid(g) * u).astype(BF16)

    wspec = pl.BlockSpec((tn, d), lambda i, j: (j, 0))
    ospec = pl.BlockSpec((tm, tn), lambda i, j: (i, j))
    ins, specs = [hf, wg_t, wu_t], [pl.BlockSpec((tm, d), lambda i, j: (i, 0)), wspec, wspec]
    if dep is not None:
        ins.append(dep)
        specs.append(pl.BlockSpec((8, 128), lambda *_: (0, 0)))
    return pl.pallas_call(
        body, name=name, grid=(r // tm, f // tn), in_specs=specs, out_specs=[ospec] * 3,
        out_shape=[jax.ShapeDtypeStruct((r, f), BF16)] * 3, compiler_params=_params(),
    )(*ins)


def _ffn_act_bwd(name, dh, wd, g, u, dep=None, tm=1056, tn=512):
    r, d = dh.shape
    f = wd.shape[0]
    tm, tn = _tile(r, tm, 8), _tile(f, tn, 128)

    def body(dh_ref, wd_ref, g_ref, u_ref, *rest):
        dg_ref, du_ref = rest[-2:]
        da = lax.dot_general(dh_ref[...], wd_ref[...], (NT, ((), ())), preferred_element_type=F32)
        gv = g_ref[...].astype(F32)
        s = _sigmoid(gv)
        dg_ref[...] = (da * u_ref[...].astype(F32) * (s + gv * s * (1.0 - s))).astype(BF16)
        du_ref[...] = (da * gv * s).astype(BF16)

    tile = pl.BlockSpec((tm, tn), lambda i, j: (i, j))
    ins = [dh, wd, g, u]
    specs = [pl.BlockSpec((tm, d), lambda i, j: (i, 0)), pl.BlockSpec((tn, d), lambda i, j: (j, 0)), tile, tile]
    if dep is not None:
        ins.append(dep)
        specs.append(pl.BlockSpec((8, 128), lambda *_: (0, 0)))
    return pl.pallas_call(
        body, name=name, grid=(r // tm, f // tn), in_specs=specs, out_specs=[tile] * 2,
        out_shape=[jax.ShapeDtypeStruct((r, f), BF16)] * 2, compiler_params=_params(),
    )(*ins)


def _shift_down(a, k):
    row = lax.broadcasted_iota(jnp.int32, a.shape, 0)
    return jnp.where(row >= k, pltpu.roll(a, k, 0), 0.0)


def _shift_up(a, k):
    n = a.shape[0]
    row = lax.broadcasted_iota(jnp.int32, a.shape, 0)
    return jnp.where(row < n - k, pltpu.roll(a, n - k, 0), 0.0)


def _conv_fwd(name, pc, cw):
    _, r, w = pc.shape

    def body(pc_ref, cw_ref, o_ref):
        a = pc_ref[2] * pc_ref[0]
        cwv = cw_ref[...]
        conv = _shift_down(a, 2) * cwv[0:1] + _shift_down(a, 1) * cwv[1:2] + a * cwv[2:3]
        o_ref[...] = (pc_ref[1] * conv).astype(BF16)

    return pl.pallas_call(
        body, name=name, grid=(w // 128,),
        in_specs=[pl.BlockSpec((3, r, 128), lambda j: (0, 0, j)), pl.BlockSpec((8, 128), lambda j: (0, j))],
        out_specs=pl.BlockSpec((r, 128), lambda j: (0, j)),
        out_shape=jax.ShapeDtypeStruct((r, w), BF16), compiler_params=_params(),
    )(pc, cw)


def _conv_bwd(name, dcat, pc, cw):
    _, r, w = pc.shape
    nblk = w // 128

    def body(dy_ref, pc_ref, cw_ref, dpc_ref, dcw_ref):
        u, gb, gc = pc_ref[0], pc_ref[1], pc_ref[2]
        cwv = cw_ref[...]
        dy = dy_ref[...]
        a = gc * u
        a1, a2 = _shift_down(a, 1), _shift_down(a, 2)
        conv = a2 * cwv[0:1] + a1 * cwv[1:2] + a * cwv[2:3]
        dconv = dy * gb
        da = dconv * cwv[2:3] + _shift_up(dconv, 1) * cwv[1:2] + _shift_up(dconv, 2) * cwv[0:1]
        dpc_ref[0] = (da * gc).astype(BF16)
        dpc_ref[1] = (dy * conv).astype(BF16)
        dpc_ref[2] = (da * u).astype(BF16)
        row = lax.broadcasted_iota(jnp.int32, (8, 128), 0)
        dw0 = jnp.sum(dconv * a2, axis=0, keepdims=True)
        dw1 = jnp.sum(dconv * a1, axis=0, keepdims=True)
        dw2 = jnp.sum(dconv * a, axis=0, keepdims=True)
        dcw_ref[...] = jnp.where(row == 0, dw0, jnp.where(row == 1, dw1, jnp.where(row == 2, dw2, 0.0)))

    return pl.pallas_call(
        body, name=name, grid=(nblk,),
        in_specs=[pl.BlockSpec((r, 128), lambda j: (0, nblk + j)),
                  pl.BlockSpec((3, r, 128), lambda j: (0, 0, j)), pl.BlockSpec((8, 128), lambda j: (0, j))],
        out_specs=[pl.BlockSpec((3, r, 128), lambda j: (0, 0, j)), pl.BlockSpec((8, 128), lambda j: (0, j))],
        out_shape=[jax.ShapeDtypeStruct((3, r, w), BF16), jax.ShapeDtypeStruct((8, w), F32)],
        compiler_params=_params(),
    )(dcat, pc, cw)


def _dot(a, b, dims):
    return lax.dot_general(a, b, (dims, ((), ())), preferred_element_type=F32)


def _col_to_row(xc, eye):
    return jnp.sum(jnp.where(eye, xc, 0.0), axis=0, keepdims=True)


def _row_to_col(xr, eye):
    return jnp.sum(jnp.where(eye, xr, 0.0), axis=1, keepdims=True)


def _gate_tiles(graw, bias, row0):
    th = jnp.tanh((graw + bias) / GATE_CAP)
    z = GATE_CAP * th
    row = lax.broadcasted_iota(jnp.int32, graw.shape, 0) + row0
    real = row >= PAD_FRONT
    li = jnp.where(real, z, -jnp.inf)
    lf = jnp.where(real, jnp.minimum(z, 0.0) - jnp.log(1.0 + jnp.exp(-jnp.abs(z))), 0.0)
    return th, z, li, lf, real


def _chunk_common(pm, h, li, lf, cst, nst, mst, tril, eye):
    kraw = pm[:, QK_W + h * DQK:QK_W + (h + 1) * DQK]
    q = (pm[:, h * DQK:(h + 1) * DQK] * QSCALE).astype(BF16)
    k = kraw.astype(BF16)
    v = pm[:, 2 * QK_W + h * DV:2 * QK_W + (h + 1) * DV].astype(BF16)
    li_c = li[:, h:h + 1]
    lf_c = lf[:, HEADS + h:HEADS + h + 1]
    li_r = _col_to_row(li_c, eye)
    lf_r = _col_to_row(lf_c, eye)
    b_c = jnp.sum(jnp.where(tril, lf_r, 0.0), axis=1, keepdims=True)
    b_r = _col_to_row(b_c, eye)
    dmat = jnp.where(tril, b_c - b_r + li_r, -jnp.inf)
    inter = b_c + mst
    mt = jnp.maximum(inter, jnp.max(dmat, axis=1, keepdims=True))
    w_inter = jnp.exp(inter - mt)
    p = jnp.exp(dmat - mt)
    s = _dot(q, k, NT) * p
    cb = cst.astype(BF16)
    nb = nst.astype(BF16).astype(F32)
    qc = _dot(q, cb, NN)
    qn = jnp.sum(q.astype(F32) * nb, axis=1, keepdims=True)
    den = w_inter * qn + jnp.sum(s, axis=1, keepdims=True)
    dn = jnp.maximum(jnp.abs(den), jnp.exp(-mt))
    b_end = b_c[CHUNK - 1:CHUNK, :]
    decay = b_end - b_c + li_c
    m_new = jnp.maximum(b_end + mst, jnp.max(decay, axis=0, keepdims=True))
    w_old = jnp.exp(b_end + mst - m_new)
    w_in = jnp.exp(decay - m_new)
    kw = (w_in * kraw).astype(BF16)
    return dict(q=q, k=k, v=v, kraw=kraw, mt=mt, w_inter=w_inter, p=p, s=s, cb=cb, nb=nb, qc=qc, qn=qn,
                den=den, dn=dn, m_new=m_new, w_old=w_old, w_in=w_in, kw=kw)


def _mlstm_fwd(name, pm, bias, nw):
    r = pm.shape[0]
    nc = r // CHUNK

    def body(pm_ref, b_ref, nw_ref, hm_ref, ht_ref, cs_ref, ns_ref, ms_ref, c_scr, n_scr, m_scr):
        ci = pl.program_id(0)

        @pl.when(ci == 0)
        def _():
            c_scr[...] = jnp.zeros_like(c_scr)
            n_scr[...] = jnp.zeros_like(n_scr)
            m_scr[...] = jnp.zeros_like(m_scr)

        pmv = pm_ref[...]
        rr = lax.broadcasted_iota(jnp.int32, (CHUNK, CHUNK), 0)
        cc = lax.broadcasted_iota(jnp.int32, (CHUNK, CHUNK), 1)
        tril, eye = cc <= rr, cc == rr
        _, _, li, lf, _ = _gate_tiles(pmv[:, GATE_COL:GATE_COL + 128], b_ref[...], ci * CHUNK)
        nwv = nw_ref[...]
        for h in range(HEADS):
            cst, nst, mst = c_scr[h], n_scr[h], m_scr[h]
            cs_ref[h] = cst
            ns_ref[h] = nst
            ms_ref[h] = mst
            f = _chunk_common(pmv, h, li, lf, cst, nst, mst, tril, eye)
            num = f["w_inter"] * f["qc"] + _dot(f["s"].astype(BF16), f["v"], NN)
            hh = num / f["dn"]
            c_scr[h] = f["w_old"] * cst + _dot(f["kw"], f["v"], TN)
            n_scr[h] = f["w_old"] * nst + jnp.sum(
                f["w_in"].astype(BF16).astype(F32) * f["k"].astype(F32), axis=0, keepdims=True)
            m_scr[h] = f["m_new"]
            sl = slice(h * DV, (h + 1) * DV)
            rs = lax.rsqrt(jnp.mean(hh * hh, axis=1, keepdims=True) + EPS)
            og = pmv[:, 2 * QK_W + MLSTM_W + h * DV:2 * QK_W + MLSTM_W + (h + 1) * DV]
            ht_ref[:, sl] = hh
            hm_ref[:, sl] = (_sigmoid(og) * (hh * rs * nwv[:, sl])).astype(BF16)

    return pl.pallas_call(
        body, name=name, grid=(nc,),
        in_specs=[pl.BlockSpec((CHUNK, PM_W), lambda i: (i, 0)), pl.BlockSpec((1, 128), lambda i: (0, 0)),
                  pl.BlockSpec((1, MLSTM_W), lambda i: (0, 0))],
        out_specs=[pl.BlockSpec((CHUNK, MLSTM_W), lambda i: (i, 0)),
                   pl.BlockSpec((CHUNK, MLSTM_W), lambda i: (i, 0)),
                   pl.BlockSpec((None, HEADS, DQK, DV), lambda i: (i, 0, 0, 0)),
                   pl.BlockSpec((None, HEADS, 1, DQK), lambda i: (i, 0, 0, 0)),
                   pl.BlockSpec((None, HEADS, 1, 1), lambda i: (i, 0, 0, 0))],
        out_shape=[jax.ShapeDtypeStruct((r, MLSTM_W), BF16), jax.ShapeDtypeStruct((r, MLSTM_W), F32),
                   jax.ShapeDtypeStruct((nc, HEADS, DQK, DV), F32),
                   jax.ShapeDtypeStruct((nc, HEADS, 1, DQK), F32),
                   jax.ShapeDtypeStruct((nc, HEADS, 1, 1), F32)],
        scratch_shapes=[pltpu.VMEM((HEADS, DQK, DV), F32), pltpu.VMEM((HEADS, 1, DQK), F32),
                        pltpu.VMEM((HEADS, 1, 1), F32)],
        compiler_params=_params(),
    )(pm, bias, nw)


def _mlstm_bwd(name, dcat, pm, ht, cs, ns, ms, bias, nw):
    r = pm.shape[0]
    nc = r // CHUNK

    def body(dy_ref, pm_ref, ht_ref, cs_ref, ns_ref, ms_ref, b_ref, nw_ref, dpm_ref, dnw_ref, db_ref,
             dc_scr, dn_scr):
        step = pl.program_id(0)
        ci = nc - 1 - step

        @pl.when(step == 0)
        def _():
            dc_scr[...] = jnp.zeros_like(dc_scr)
            dn_scr[...] = jnp.zeros_like(dn_scr)
            dnw_ref[...] = jnp.zeros_like(dnw_ref)
            db_ref[...] = jnp.zeros_like(db_ref)

        pmv = pm_ref[...]
        rr = lax.broadcasted_iota(jnp.int32, (CHUNK, CHUNK), 0)
        cc = lax.broadcasted_iota(jnp.int32, (CHUNK, CHUNK), 1)
        tril, eye, triu = cc <= rr, cc == rr, cc >= rr
        th, z, li, lf, real = _gate_tiles(pmv[:, GATE_COL:GATE_COL + 128], b_ref[...], ci * CHUNK)
        lane = lax.broadcasted_iota(jnp.int32, (CHUNK, 128), 1)
        rowid = lax.broadcasted_iota(jnp.int32, (CHUNK, 1), 0)
        nwv = nw_ref[...]
        dgt = jnp.zeros((CHUNK, 128), F32)
        for h in range(HEADS):
            cst, nst, mst = cs_ref[h], ns_ref[h], ms_ref[h]
            f = _chunk_common(pmv, h, li, lf, cst, nst, mst, tril, eye)
            q, k, v, s, p = f["q"], f["k"], f["v"], f["s"], f["p"]
            w_inter, w_in, w_old, dn = f["w_inter"], f["w_in"], f["w_old"], f["dn"]
            sl = slice(h * DV, (h + 1) * DV)
            osl = slice(2 * QK_W + MLSTM_W + h * DV, 2 * QK_W + MLSTM_W + (h + 1) * DV)
            hh = ht_ref[:, sl]
            y = dy_ref[:, sl]
            sg = _sigmoid(pmv[:, osl])
            rs = lax.rsqrt(jnp.mean(hh * hh, axis=1, keepdims=True) + EPS)
            nwh = nwv[:, sl]
            dpm_ref[:, osl] = (y * (hh * rs * nwh) * sg * (1.0 - sg)).astype(BF16)
            dhn = y * sg
            dnw_ref[:, sl] += jnp.sum(dhn * hh * rs, axis=0, keepdims=True)
            wd = dhn * nwh
            dhh = rs * wd - hh * (rs * rs * rs) * jnp.mean(hh * wd, axis=1, keepdims=True)
            dnum = dhh / dn
            dd = -jnp.sum(dhh * hh, axis=1, keepdims=True) / dn
            dden = jnp.where(jnp.abs(f["den"]) > jnp.exp(-f["mt"]), dd * jnp.sign(f["den"]), 0.0)
            dnum_b = dnum.astype(BF16)
            wdn = (w_inter * dnum).astype(BF16)
            wid = (w_inter * dden).astype(BF16).astype(F32)
            ds = _dot(dnum_b, v, NT) + dden
            dsp = (ds * p).astype(BF16)
            dq = _dot(dsp, k, NN) + _dot(wdn, f["cb"], NT) + wid * f["nb"]
            dk = _dot(dsp, q, TN)
            dv = _dot(s.astype(BF16), dnum_b, TN)
            g = ds * s
            g_col = _row_to_col(jnp.sum(g, axis=0, keepdims=True), eye)
            db = jnp.sum(g, axis=1, keepdims=True) - g_col
            dli = g_col
            db = db + (jnp.sum(dnum * f["qc"], axis=1, keepdims=True) + dden * f["qn"]) * w_inter
            dcn, dnn = dc_scr[h], dn_scr[h]
            dcnb = dcn.astype(BF16)
            dnnb = dnn.astype(BF16).astype(F32)
            dkw = _dot(v, dcnb, NT) + dnnb
            dk = dk + w_in * dkw
            dv = dv + _dot(f["kw"], dcnb, NN)
            ddecay = jnp.sum(dkw * f["kraw"], axis=1, keepdims=True) * w_in
            dw_old = (jnp.sum(jnp.sum(dcn * cst, axis=1, keepdims=True), axis=0, keepdims=True)
                      + jnp.sum(dnn * nst, axis=1, keepdims=True))
            db_end = dw_old * w_old + jnp.sum(ddecay, axis=0, keepdims=True)
            db = db - ddecay + jnp.where(rowid == CHUNK - 1, db_end, 0.0)
            dli = dli + ddecay
            dc_scr[h] = w_old * dcn + _dot(q, wdn, TN)
            dn_scr[h] = w_old * dnn + jnp.sum(wid * q.astype(F32), axis=0, keepdims=True)
            dlf = jnp.sum(jnp.where(triu, _col_to_row(db, eye), 0.0), axis=1, keepdims=True)
            dgt = dgt + jnp.where(lane == h, dli, 0.0) + jnp.where(lane == HEADS + h, dlf, 0.0)
            dpm_ref[:, h * DQK:(h + 1) * DQK] = (dq * QSCALE).astype(BF16)
            dpm_ref[:, QK_W + h * DQK:QK_W + (h + 1) * DQK] = dk.astype(BF16)
            dpm_ref[:, 2 * QK_W + h * DV:2 * QK_W + (h + 1) * DV] = dv.astype(BF16)
        dact = jnp.where(lane < HEADS, 1.0, 1.0 - _sigmoid(z)) * (1.0 - th * th)
        dgraw = jnp.where(real & (lane < 2 * HEADS), dgt * dact, 0.0)
        dpm_ref[:, GATE_COL:GATE_COL + 128] = dgraw.astype(BF16)
        db_ref[...] += jnp.sum(dgraw, axis=0, keepdims=True)

    rev = lambda i: (nc - 1 - i, 0)
    rev4 = lambda i: (nc - 1 - i, 0, 0, 0)
    return pl.pallas_call(
        body, name=name, grid=(nc,),
        in_specs=[pl.BlockSpec((CHUNK, MLSTM_W), rev), pl.BlockSpec((CHUNK, PM_W), rev),
                  pl.BlockSpec((CHUNK, MLSTM_W), rev),
                  pl.BlockSpec((None, HEADS, DQK, DV), rev4), pl.BlockSpec((None, HEADS, 1, DQK), rev4),
                  pl.BlockSpec((None, HEADS, 1, 1), rev4),
                  pl.BlockSpec((1, 128), lambda i: (0, 0)), pl.BlockSpec((1, MLSTM_W), lambda i: (0, 0))],
        out_specs=[pl.BlockSpec((CHUNK, PM_W), rev), pl.BlockSpec((1, MLSTM_W), lambda i: (0, 0)),
                   pl.BlockSpec((1, 128), lambda i: (0, 0))],
        out_shape=[jax.ShapeDtypeStruct((r, PM_W), BF16), jax.ShapeDtypeStruct((1, MLSTM_W), F32),
                   jax.ShapeDtypeStruct((1, 128), F32)],
        scratch_shapes=[pltpu.VMEM((HEADS, DQK, DV), F32), pltpu.VMEM((HEADS, 1, DQK), F32)],
        compiler_params=_params(),
    )(dcat, pm, ht, cs, ns, ms, bias, nw)


def _my_place():
    return lax.axis_index("x"), lax.axis_index("y"), lax.axis_index("c")


def _flip(v, bit):
    return 1 - v if bit else v


def _exchange_small(name, blk, reduce):
    r, c = blk.shape

    def body(x_ref, o_ref, *rest):
        slots = rest[0] if reduce else o_ref
        send_sems, recv_sems = rest[-2], rest[-1]
        x, y, cc = _my_place()
        me = 4 * x + 2 * y + cc
        slots[me] = x_ref[...]
        copies = []
        for k in range(1, N_DEV):
            peer = (_flip(x, k & 4), _flip(y, k & 2), _flip(cc, k & 1))
            cp = pltpu.make_async_remote_copy(
                src_ref=x_ref, dst_ref=slots.at[me], send_sem=send_sems.at[k - 1],
                recv_sem=recv_sems.at[k - 1], device_id=peer, device_id_type=MESH)
            cp.start()
            copies.append(cp)
        for cp in copies:
            cp.wait()
        if reduce:
            acc = slots[0]
            for d in range(1, N_DEV):
                acc = acc + slots[d]
            o_ref[...] = acc

    scratch = ([pltpu.VMEM((N_DEV, r, c), F32)] if reduce else []) + [
        pltpu.SemaphoreType.DMA((N_DEV - 1,)), pltpu.SemaphoreType.DMA((N_DEV - 1,))]
    return pl.pallas_call(
        body, name=name,
        out_shape=jax.ShapeDtypeStruct((r, c) if reduce else (N_DEV, r, c), F32),
        in_specs=[pl.BlockSpec(memory_space=pltpu.VMEM)], out_specs=pl.BlockSpec(memory_space=pltpu.VMEM),
        scratch_shapes=scratch, compiler_params=_params(),
    )(blk)


HBM_SPEC = pl.BlockSpec(memory_space=pltpu.HBM)
SEM_SPEC = pl.BlockSpec(memory_space=pltpu.SEMAPHORE)
ANY_SPEC = pl.BlockSpec(memory_space=pl.ANY)
DATAFLOW = pltpu.SideEffectType.DATAFLOW_SIDE_EFFECTING


def _split_copy(name, arrays, start=None, wait=None, after=None):
    n = len(arrays)
    n_wait = 2 if wait else 0
    n_after = 0 if after is None else 1
    n_new = 2 if start else 0

    def body(*refs):
        ins = refs[:n]
        if wait:
            for cp in wait[0](ins, refs[n], refs[n + 1]):
                cp.wait_send()
                cp.wait_recv()
        if start:
            at = n + n_wait + n_after
            for cp in start[0](ins, refs[at], refs[at + 1]):
                cp.start()
            token = refs[at + 2 + n]
            token[...] = jnp.zeros_like(token)

    operands = [pltpu.with_memory_space_constraint(a, pltpu.HBM) for a in arrays]
    in_specs = [HBM_SPEC] * n
    if wait:
        operands += list(wait[1])
        in_specs += [SEM_SPEC, SEM_SPEC]
    if after is not None:
        operands.append(after)
        in_specs.append(ANY_SPEC)
    out_shape, out_specs = [], []
    if start:
        out_shape += [pltpu.SemaphoreType.DMA((start[1],)), pltpu.SemaphoreType.DMA((start[1],))]
        out_specs += [SEM_SPEC, SEM_SPEC]
    out_shape += [pltpu.HBM(a.shape, a.dtype) for a in arrays]
    out_specs += [HBM_SPEC] * n
    if start:
        out_shape.append(jax.ShapeDtypeStruct((8, 128), F32))
        out_specs.append(pl.BlockSpec(memory_space=pltpu.VMEM))
    outs = pl.pallas_call(
        body, name=name, in_specs=in_specs, out_specs=out_specs, out_shape=out_shape,
        input_output_aliases={i: n_new + i for i in range(n)},
        compiler_params=pltpu.CompilerParams(has_side_effects=DATAFLOW),
    )(*operands)
    thru = list(outs[n_new:n_new + n])
    return thru, (tuple(outs[:2]) if start else None), (outs[n_new + n] if start else None)


def _remote(src, dst, send_sems, recv_sems, k, to):
    return pltpu.make_async_remote_copy(src_ref=src, dst_ref=dst, send_sem=send_sems.at[k],
                                        recv_sem=recv_sems.at[k], device_id=to, device_id_type=MESH)


def _slot(px, py, pc):
    return 4 * px + 2 * py + pc


def _gather_first(refs, send_sems, recv_sems):
    x, y, c = _my_place()
    blk = refs[0].at[_slot(x, y, c)]
    targets = [(x, y, 1 - c), (1 - x, y, c), (x, 1 - y, c)]
    return [_remote(blk, blk, send_sems, recv_sems, k, to) for k, to in enumerate(targets)]


def _gather_relay(refs, send_sems, recv_sems):
    x, y, c = _my_place()
    rows = refs[0].shape[1]
    half = rows // 32 * 16
    from_x, from_y = _slot(1 - x, y, c), _slot(x, 1 - y, c)
    upper = refs[0].at[from_x, pl.ds(0, half)]
    lower = refs[0].at[from_y, pl.ds(half, rows - half)]
    return [_remote(upper, upper, send_sems, recv_sems, 0, (x, 1 - y, c)),
            _remote(lower, lower, send_sems, recv_sems, 1, (1 - x, y, c)),
            _remote(refs[0].at[from_x], refs[0].at[from_x], send_sems, recv_sems, 2, (x, y, 1 - c)),
            _remote(refs[0].at[from_y], refs[0].at[from_y], send_sems, recv_sems, 3, (x, y, 1 - c))]


def _gather_last(refs, send_sems, recv_sems):
    x, y, c = _my_place()
    blk = refs[0].at[_slot(1 - x, 1 - y, c)]
    return [_remote(blk, blk, send_sems, recv_sems, 0, (x, y, 1 - c))]


def _scatter_sibling(n):
    def copies(refs, send_sems, recv_sems):
        x, y, c = _my_place()
        return [_remote(refs[a].at[2 * j + 1 - c], refs[n + a].at[j], send_sems, recv_sems, 4 * a + j, (x, y, 1 - c))
                for a in range(n) for j in range(4)]
    return copies


def _scatter_chips(n):
    def copies(refs, send_sems, recv_sems):
        x, y, c = _my_place()
        out = []
        for a in range(n):
            for k in range(1, 4):
                px, py = _flip(x, k & 2), _flip(y, k & 1)
                out.append(_remote(refs[a].at[2 * px + py], refs[n + a].at[2 * x + y], send_sems, recv_sems,
                                   3 * a + k - 1, (px, py, c)))
        return out
    return copies


def _pair_sum(name, core, g, t):
    _, r, c = g.shape
    tr = _tile(r, 512, 8)
    g4 = g.reshape(4, 2, r, c)

    def body(core_ref, g_ref, t_ref, o_ref):
        o_ref[...] = (g_ref[...].astype(F32) + t_ref[...].astype(F32)).astype(BF16)

    return pl.pallas_call(
        body, name=name,
        grid_spec=pltpu.PrefetchScalarGridSpec(
            num_scalar_prefetch=1, grid=(4, r // tr),
            in_specs=[pl.BlockSpec((None, None, tr, c), lambda j, i, core_ref: (j, core_ref[0], i, 0)),
                      pl.BlockSpec((None, tr, c), lambda j, i, core_ref: (j, i, 0))],
            out_specs=pl.BlockSpec((None, tr, c), lambda j, i, core_ref: (j, i, 0))),
        out_shape=jax.ShapeDtypeStruct((4, r, c), BF16), compiler_params=_params(),
    )(core, g4, t)


def _adam_math(w, g, m, v):
    m2 = ADAM_B1 * m + (1.0 - ADAM_B1) * g
    v2 = ADAM_B2 * v + (1.0 - ADAM_B2) * (g * g)
    m_hat = m2 / (1.0 - ADAM_B1 ** ADAM_STEP)
    v_hat = v2 / (1.0 - ADAM_B2 ** ADAM_STEP)
    delta = -ADAM_LR * (m_hat / (jnp.sqrt(v_hat) + ADAM_EPS) + ADAM_WD * w)
    return delta, m2, v2


def _adam_sharded(name, chip, w, m, v, grads, row_off=0):
    _, r, c = w.shape
    tr = _tile(r, 256, 8)
    tc = c if tr < r else _tile(c, 256, 128)
    boff = row_off // tr

    def body(chip_ref, w_ref, m_ref, v_ref, p0_ref, q0_ref, p1_ref, q1_ref, g_ref, d_ref, nm_ref, nv_ref):
        mine = chip_ref[0]

        def total(p_ref, q_ref):
            acc = None
            for j in range(4):
                part = jnp.where(mine == j, p_ref[...], q_ref[j]).astype(F32)
                acc = part if acc is None else acc + part
            return acc

        g = jnp.where(pl.program_id(0) == 0, total(p0_ref, q0_ref), total(p1_ref, q1_ref))
        delta, m2, v2 = _adam_math(w_ref[...], g, m_ref[...], v_ref[...])
        g_ref[...] = g
        d_ref[...] = delta
        nm_ref[...] = m2
        nv_ref[...] = v2

    def grad_specs(layer):
        at = lambda l, i, j: (jnp.where(l == layer, boff + i, boff), jnp.where(l == layer, j, 0))
        return [pl.BlockSpec((None, tr, tc), lambda l, i, j, chip_ref: (chip_ref[0],) + at(l, i, j)),
                pl.BlockSpec((4, tr, tc), lambda l, i, j, chip_ref: (0,) + at(l, i, j))]

    wspec = pl.BlockSpec((None, tr, tc), lambda l, i, j, chip_ref: (l, i, j))
    sds = jax.ShapeDtypeStruct(w.shape, F32)
    return pl.pallas_call(
        body, name=name,
        grid_spec=pltpu.PrefetchScalarGridSpec(
            num_scalar_prefetch=1, grid=(2, r // tr, c // tc),
            in_specs=[wspec, wspec, wspec] + grad_specs(0) + grad_specs(1), out_specs=[wspec] * 4),
        out_shape=[sds] * 4, compiler_params=_params(),
    )(chip, w, m, v, grads[0][0], grads[0][1], grads[1][0], grads[1][1])


def _adam_small(name, w, m, v, g):
    def body(w_ref, m_ref, v_ref, g_ref, d_ref, nm_ref, nv_ref):
        delta, m2, v2 = _adam_math(w_ref[...], g_ref[...], m_ref[...], v_ref[...])
        d_ref[...] = delta
        nm_ref[...] = m2
        nv_ref[...] = v2

    sds = jax.ShapeDtypeStruct(w.shape, F32)
    vm = pl.BlockSpec(memory_space=pltpu.VMEM)
    return pl.pallas_call(body, name=name, in_specs=[vm] * 4, out_specs=[vm] * 3, out_shape=[sds] * 3,
                          compiler_params=_params())(w, m, v, g)


GATE_END = GATE_COL + 2 * HEADS


def _merge_dw_in(dwm_t, dwc_t):
    full = jnp.concatenate([dwm_t[:GATE_END], dwc_t.reshape(3 * CONV_W, D_MODEL)], axis=0)
    return full.reshape(N_DEV, IN_SH, D_MODEL)


def _pack128(parts):
    flat = jnp.concatenate([p.reshape(-1) for p in parts])
    n = flat.shape[0]
    rows = -(-n // 1024) * 8
    return jnp.pad(flat, (0, rows * 128 - n)).reshape(rows, 128)


def _unpack128(packed, shapes):
    flat = packed.reshape(-1)
    out, at = [], 0
    for s in shapes:
        n = int(np.prod(s))
        out.append(flat[at:at + n].reshape(s))
        at += n
    return out


def kernel(x, meta_tokens, norm_mix_w, w_in, b_gates, conv_w, mlstm_norm_w, w_out, norm_ffn_w, w_gate, w_up, w_down, norm_final_w, loss_target, m_meta_tokens, m_norm_mix_w, m_w_in, m_b_gates, m_conv_w, m_mlstm_norm_w, m_w_out, m_norm_ffn_w, m_w_gate, m_w_up, m_w_down, m_norm_final_w, v_meta_tokens, v_norm_mix_w, v_w_in, v_b_gates, v_conv_w, v_mlstm_norm_w, v_w_out, v_norm_ffn_w, v_w_gate, v_w_up, v_w_down, v_norm_final_w):
    seq = x.shape[1]
    rows = TOK0 + seq
    me = 4 * lax.axis_index("x") + 2 * lax.axis_index("y") + lax.axis_index("c")
    meta_sh = meta_tokens.shape[1]
    conv_sh = conv_w.shape[2]

    small = jnp.concatenate(
        [meta_tokens, jnp.pad(conv_w.reshape(DEPTH * 3, conv_sh), ((0, 2), (0, meta_sh - conv_sh)))], axis=0)
    slots = _exchange_small("gather_small", small, reduce=False)
    meta_full = jnp.transpose(slots[:, :N_META, :], (1, 0, 2)).reshape(N_META, D_MODEL)
    conv_full = jnp.transpose(slots[:, N_META:N_META + DEPTH * 3, :conv_sh], (1, 0, 2)).reshape(DEPTH, 3, CONV_W)
    conv_rows = [jnp.pad(conv_full[l], ((0, 5), (0, 0))) for l in range(DEPTH)]

    w_in_t, m_w_in_t, v_w_in_t = (jnp.transpose(a, (0, 2, 1)) for a in (w_in, m_w_in, v_w_in))
    w_gate_t, m_w_gate_t, v_w_gate_t = (jnp.transpose(a, (0, 2, 1)) for a in (w_gate, m_w_gate, v_w_gate))
    w_up_t, m_w_up_t, v_w_up_t = (jnp.transpose(a, (0, 2, 1)) for a in (w_up, m_w_up, v_w_up))
    shards = []
    for l in range(DEPTH):
        shards += [w_in_t[l].astype(BF16), w_out[l].astype(BF16), w_gate_t[l].astype(BF16),
                   w_up_t[l].astype(BF16), w_down[l].astype(BF16)]
    per_layer = ("w_in", "w_out", "w_gate", "w_up", "w_down")
    gather_names = [f"{nm}_{l}" for l in range(DEPTH) for nm in per_layer]
    gather_state = {}

    def gather_start(i, after):
        if i >= len(shards):
            return after
        buf = lax.dynamic_update_index_in_dim(lax.empty((N_DEV,) + shards[i].shape, BF16), shards[i], me, 0)
        arrs, sems, tok = _split_copy(f"gather_start_{gather_names[i]}", [buf], start=(_gather_first, 3), after=after)
        gather_state[i] = (arrs, sems)
        return tok

    def gather_relay(i, after):
        if i >= len(shards):
            return after
        arrs, sems = gather_state[i]
        arrs, sems, tok = _split_copy(f"gather_relay_{gather_names[i]}", arrs, start=(_gather_relay, 4),
                                      wait=(_gather_first, sems), after=after)
        gather_state[i] = (arrs, sems)
        return tok

    def gather_last(i, after):
        arrs, sems = gather_state[i]
        arrs, sems, tok = _split_copy(f"gather_last_{gather_names[i]}", arrs, start=(_gather_last, 1),
                                      wait=(_gather_relay, sems), after=after)
        gather_state[i] = (arrs, sems)
        return tok

    def gathered(i, after):
        arrs, sems = gather_state[i]
        arrs, _, _ = _split_copy(f"gather_done_{gather_names[i]}", arrs, wait=(_gather_last, sems), after=after)
        return arrs[0]

    bias = [jnp.pad(b_gates[l].reshape(1, 2 * HEADS), ((0, 0), (0, 128 - 2 * HEADS))) for l in range(DEPTH)]
    nmix = [norm_mix_w[l].reshape(1, D_MODEL) for l in range(DEPTH)]
    nffn = [norm_ffn_w[l].reshape(1, D_MODEL) for l in range(DEPTH)]
    nmls = [mlstm_norm_w[l].reshape(1, MLSTM_W) for l in range(DEPTH)]
    weights = [dict() for _ in range(DEPTH)]
    saved = [dict() for _ in range(DEPTH)]

    def layer_fwd(l, h, after):
        w, s = weights[l], saved[l]
        k0 = len(per_layer) * l
        tok = gather_last(k0, after)
        w["win_t"] = gathered(k0, tok).reshape(D_IN, D_MODEL)
        w["wc_t"] = w["win_t"][GATE_END:].reshape(3, CONV_W, D_MODEL)
        tok = gather_start(k0 + 3, gather_relay(k0 + 1, tok))
        s["h0"] = h
        s["hn"] = _rms_fwd(f"norm_mix_{l}", h, nmix[l] + tok[0, 0])
        s["pm"] = _mm_nt(f"proj_mlstm_{l}", s["hn"], w["win_t"], F32, tn=640, tk=D_MODEL, n=PM_W)
        tok = gather_start(k0 + 4, gather_relay(k0 + 2, s["pm"]))
        tok = gather_last(k0 + 1, tok)
        s["pc"] = _mm_nt_bcols(f"proj_conv_{l}", s["hn"], w["wc_t"], F32, dep=tok)
        hm, s["ht"], s["cs"], s["ns"], s["ms"] = _mlstm_fwd(f"mlstm_fwd_{l}", s["pm"], bias[l] + tok[:1], nmls[l])
        tok = gather_start(k0 + 5, gather_relay(k0 + 3, hm))
        tok = gather_last(k0 + 2, tok)
        hc = _conv_fwd(f"conv_fwd_{l}", s["pc"], conv_rows[l] + tok[0, 0])
        s["cat"] = jnp.concatenate([hm, hc], axis=1)
        w["wo"] = gathered(k0 + 1, s["cat"]).reshape(D_MODEL, D_MODEL)
        s["h1"] = _mm_nn(f"out_proj_{l}", s["cat"], w["wo"], F32, res=s["h0"])
        tok = gather_start(k0 + 6, gather_relay(k0 + 4, s["h1"]))
        s["hf"] = _rms_fwd(f"norm_ffn_{l}", s["h1"], nffn[l] + tok[0, 0])
        tok = gather_last(k0 + 3, s["hf"])
        w["wg_t"] = gathered(k0 + 2, tok).reshape(D_FF, D_MODEL)
        w["wu_t"] = gathered(k0 + 3, tok).reshape(D_FF, D_MODEL)
        s["g"], s["u"], s["act"] = _ffn_in(f"ffn_in_{l}", s["hf"], w["wg_t"], w["wu_t"])
        tok = gather_last(k0 + 4, s["act"])
        w["wd"] = gathered(k0 + 4, tok).reshape(D_FF, D_MODEL)
        tok = gather_start(k0 + 7, gather_relay(k0 + 5, tok))
        return _mm_nn(f"ffn_out_{l}", s["act"], w["wd"], F32, res=s["h1"], dep=tok)

    tok = None
    for i in range(3):
        tok = gather_start(i, tok)
    tok = gather_relay(0, tok)
    h = jnp.concatenate([jnp.zeros((PAD_FRONT, D_MODEL), F32), meta_full, x[0]], axis=0)
    h = layer_fwd(0, h, tok)
    h = layer_fwd(1, h, h)

    dh, dh_b, d_final, loss_part = _final_loss("final_loss", h, norm_final_w.reshape(1, D_MODEL), loss_target[0])

    core = lax.axis_index("c").astype(jnp.int32).reshape(1)
    chip = (2 * lax.axis_index("x") + lax.axis_index("y")).astype(jnp.int32).reshape(1)
    scatter_state = {}

    def scatter_begin(nm, grad):
        land = lax.empty((4,) + grad.shape[1:], BF16)
        arrs, sems, tok = _split_copy(f"grad_sibling_start_{nm}", [grad, land], start=(_scatter_sibling(1), 4))
        scatter_state[nm] = (arrs, sems)
        return tok

    def scatter_advance(nm, after):
        arrs, sems = scatter_state[nm]
        arrs, _, _ = _split_copy(f"grad_sibling_done_{nm}", arrs, wait=(_scatter_sibling(1), sems), after=after)
        part = _pair_sum(f"grad_pair_sum_{nm}", core, arrs[0], arrs[1])
        arrs, sems, tok = _split_copy(f"grad_chips_start_{nm}", [part, lax.empty(part.shape, BF16)],
                                      start=(_scatter_chips(1), 3))
        scatter_state[nm] = (arrs, sems)
        return tok

    def scattered(nm, after):
        arrs, sems = scatter_state[nm]
        arrs, _, _ = _split_copy(f"grad_chips_done_{nm}", arrs, wait=(_scatter_chips(1), sems), after=after)
        return arrs[0], arrs[1]

    d_mix, d_ffn, d_mls, d_bias, d_conv = ([None] * DEPTH for _ in range(5))

    def layer_bwd(l, dh, dh_b, tok):
        w, s = weights[l], saved[l]
        dg, du = _ffn_act_bwd(f"d_act_{l}", dh_b, w["wd"], s["g"], s["u"], dep=tok)
        dw_down = _mm_tn(f"dw_down_{l}", s["act"], dh_b, BF16, tm=1408, tn=1024)
        tok = scatter_begin(f"w_down_{l}", dw_down.reshape(N_DEV, FF_SH, D_MODEL))
        dhf = _mm_nn(f"d_ffn_gate_{l}", dg, w["wg_t"], F32, dep=tok)
        tok = scatter_advance(f"w_down_{l}", after=dhf)
        dhf = _mm_nn(f"d_ffn_up_{l}", du, w["wu_t"], F32, res=dhf, dep=tok)
        dw_gate = _mm_tn(f"dw_gate_{l}", dg, s["hf"], BF16, tm=1408, tn=1024)
        tok = scatter_begin(f"w_gate_{l}", dw_gate.reshape(N_DEV, FF_SH, D_MODEL))
        dw_up = _mm_tn(f"dw_up_{l}", du, s["hf"], BF16, tm=1408, tn=1024, dep=tok)
        tok = scatter_begin(f"w_up_{l}", dw_up.reshape(N_DEV, FF_SH, D_MODEL))
        dh1, dh1_b, d_ffn[l] = _rms_bwd(f"norm_ffn_bwd_{l}", s["h1"], nffn[l] + tok[0, 0], dhf, dh)
        tok = scatter_advance(f"w_gate_{l}", after=dh1)
        dcat = _mm_nt(f"d_cat_{l}", dh1_b, w["wo"], F32, tk=D_MODEL, dep=tok)
        tok = scatter_advance(f"w_up_{l}", after=dcat)
        dw_out = _mm_tn(f"dw_out_{l}", s["cat"], dh1_b, BF16, tn=1024, dep=tok)
        tok = scatter_begin(f"w_out_{l}", dw_out.reshape(N_DEV, OUT_SH, D_MODEL))
        dpm, d_mls[l], d_bias[l] = _mlstm_bwd(f"mlstm_bwd_{l}", dcat, s["pm"], s["ht"], s["cs"], s["ns"],
                                               s["ms"], bias[l] + tok[:1], nmls[l])
        dpc, d_conv[l] = _conv_bwd(f"conv_bwd_{l}", dcat, s["pc"], conv_rows[l])
        tok = scatter_advance(f"w_out_{l}", after=dpc)
        dwm_t = _mm_tn(f"dw_mlstm_{l}", dpm, s["hn"], BF16, tm=640, tn=1024, dep=tok)
        dwc_t = _mm_tn_acols(f"dw_conv_{l}", dpc, s["hn"], BF16)
        tok = scatter_begin(f"w_in_{l}", _merge_dw_in(dwm_t, dwc_t))
        dhn = _mm_nn_kt(f"d_norm_mlstm_{l}", dpm, w["win_t"], F32, tk=PM_W, dep=tok)
        dhn = _mm_nn_ksum(f"d_norm_conv_{l}", dpc, w["wc_t"], F32, res=dhn)
        tok = scatter_advance(f"w_in_{l}", after=dhn)
        dh, dh_b, d_mix[l] = _rms_bwd(f"norm_mix_bwd_{l}", s["h0"], nmix[l] + tok[0, 0], dhn, dh1)
        return dh, dh_b, tok

    dh, dh_b, tok = layer_bwd(1, dh, dh_b, None)
    dh, dh_b, tok_tail = layer_bwd(0, dh, dh_b, tok)

    pq = {}
    after = dh
    for l in reversed(range(DEPTH)):
        for nm in ("w_down", "w_gate", "w_up", "w_out", "w_in"):
            if (nm, l) != ("w_in", 0):
                pq[nm, l] = scattered(f"{nm}_{l}", after)
                after = pq[nm, l][0]
    untransposed = lambda outs: [jnp.transpose(o, (0, 2, 1)) for o in outs]
    g_out, d_out, nm_out, nv_out = _adam_sharded(
        "adam_w_out", chip, w_out, m_w_out, v_w_out, [pq["w_out", 0], pq["w_out", 1]])
    g_gate, d_gate, nm_gate, nv_gate = untransposed(_adam_sharded(
        "adam_w_gate", chip, w_gate_t, m_w_gate_t, v_w_gate_t, [pq["w_gate", 0], pq["w_gate", 1]]))
    g_up, d_up, nm_up, nv_up = untransposed(_adam_sharded(
        "adam_w_up", chip, w_up_t, m_w_up_t, v_w_up_t, [pq["w_up", 0], pq["w_up", 1]]))
    g_down, d_down, nm_down, nv_down = _adam_sharded(
        "adam_w_down", chip, w_down, m_w_down, v_w_down, [pq["w_down", 0], pq["w_down", 1]])
    pq["w_in", 0] = scattered("w_in_0", nv_down)
    g_in, d_in, nm_in, nv_in = untransposed(_adam_sharded(
        "adam_w_in", chip, w_in_t, m_w_in_t, v_w_in_t, [pq["w_in", 0], pq["w_in", 1]]))

    bg = jnp.concatenate([d_bias[l][0, :2 * HEADS] for l in range(DEPTH)])
    red_in = jnp.concatenate([
        dh[PAD_FRONT:TOK0], d_mix[0], d_mix[1], d_ffn[0], d_ffn[1], d_final,
        jnp.concatenate([d_mls[0], d_mls[1]], axis=1),
        jnp.stack([d_conv[l][:3] for l in range(DEPTH)]).reshape(3, 2 * CONV_W),
        jnp.pad(bg, (0, D_MODEL - bg.shape[0])).reshape(1, D_MODEL),
        jnp.pad(loss_part[:, :1], ((0, 0), (0, D_MODEL - 1))),
        jnp.zeros((5, D_MODEL), F32) + tok_tail[0, 0]], axis=0)
    red = _exchange_small("reduce_small", red_in, reduce=True)
    loss = red[26, 0]
    g_meta = lax.dynamic_slice_in_dim(red[:N_META], me * meta_sh, meta_sh, axis=1)
    g_mix, g_ffn, g_final = red[16:18], red[18:20], red[20]
    g_mls = red[21].reshape(DEPTH, MLSTM_W)
    g_conv = lax.dynamic_slice_in_dim(red[22:25].reshape(DEPTH, 3, CONV_W), me * conv_sh, conv_sh, axis=2)
    g_bias = red[25, :DEPTH * 2 * HEADS].reshape(DEPTH, 2 * HEADS)

    small_w = [meta_tokens, norm_mix_w, b_gates, conv_w, mlstm_norm_w, norm_ffn_w, norm_final_w]
    small_m = [m_meta_tokens, m_norm_mix_w, m_b_gates, m_conv_w, m_mlstm_norm_w, m_norm_ffn_w, m_norm_final_w]
    small_v = [v_meta_tokens, v_norm_mix_w, v_b_gates, v_conv_w, v_mlstm_norm_w, v_norm_ffn_w, v_norm_final_w]
    small_g = [g_meta, g_mix, g_bias, g_conv, g_mls, g_ffn, g_final]
    shapes = [a.shape for a in small_w]
    packed = _adam_small("adam_small", _pack128(small_w), _pack128(small_m), _pack128(small_v), _pack128(small_g))
    (d_meta, d_nmix, d_bg, d_cw, d_nmls, d_nffn, d_nfin), (nm_meta, nm_nmix, nm_bg, nm_cw, nm_nmls, nm_nffn, nm_nfin), \
        (nv_meta, nv_nmix, nv_bg, nv_cw, nv_nmls, nv_nffn, nv_nfin) = (_unpack128(p, shapes) for p in packed)

    grad_x = dh[TOK0:].reshape(1, seq, D_MODEL)
    return (loss, grad_x,
            g_meta, g_mix, g_in, g_bias, g_conv, g_mls, g_out, g_ffn, g_gate, g_up, g_down, g_final,
            d_meta, d_nmix, d_in, d_bg, d_cw, d_nmls, d_out, d_nffn, d_gate, d_up, d_down, d_nfin,
            nm_meta, nm_nmix, nm_in, nm_bg, nm_cw, nm_nmls, nm_out, nm_nffn, nm_gate, nm_up, nm_down, nm_nfin,
            nv_meta, nv_nmix, nv_in, nv_bg, nv_cw, nv_nmls, nv_out, nv_nffn, nv_gate, nv_up, nv_down, nv_nfin)
```

```python
import functools

import numpy as np
import jax
import jax.numpy as jnp
from jax import lax
from jax.experimental import pallas as pl
from jax.experimental.pallas import tpu as pltpu

F32 = jnp.float32
BF16 = jnp.bfloat16
MESH = pl.DeviceIdType.MESH

D_MODEL = 2048
DEPTH = 2
N_META = 16
MLSTM_W = 1024
CONV_W = 1024
HEADS = 4
DV = 256
DQK = 128
QK_W = 512
CHUNK = 64
PAD_FRONT = 48
TOK0 = PAD_FRONT + N_META
D_FF = 5632
N_DEV = 8
FF_SH = D_FF // N_DEV
D_IN = 6152
IN_SH = D_IN // N_DEV
OUT_SH = D_MODEL // N_DEV
GATE_COL = 3072
PM_W = GATE_COL + 128
GATE_CAP = 15.0
EPS = 1e-6
QSCALE = DQK ** -0.5

ADAM_LR = 0.001
ADAM_B1 = 0.9
ADAM_B2 = 0.999
ADAM_EPS = 1e-08
ADAM_WD = 0.01
ADAM_STEP = 10

V7X_VMEM_LIMIT = 50 * 1024 * 1024


def _params(**kw):
    return pltpu.CompilerParams(vmem_limit_bytes=V7X_VMEM_LIMIT, **kw)


def _tile(n, target, mult):
    best = None
    for t in range(mult, min(n, target) + 1, mult):
        if n % t == 0:
            best = t
    return best if best is not None else n


def _sigmoid(x):
    return 1.0 / (1.0 + jnp.exp(-x))


NN = ((1,), (0,))
NT = ((1,), (1,))
TN = ((0,), (0,))


def _matmul(name, a, b, out_shape, out_dtype, grid, a_bs, b_bs, o_bs, dims, nk, acc_shape=None,
            res=None, res_bs=None, dep=None):
    has_res = res is not None
    n_in = 2 + has_res + (dep is not None)

    def body(*refs):
        a_ref, b_ref = refs[0], refs[1]
        r_ref = refs[2] if has_res else None
        o_ref = refs[n_in]
        x = lax.dot_general(a_ref[...], b_ref[...], (dims, ((), ())), preferred_element_type=F32)
        if nk == 1:
            if has_res:
                x = x + r_ref[...]
            o_ref[...] = x.astype(o_ref.dtype)
            return
        acc = refs[n_in + 1]
        k = pl.program_id(len(grid) - 1)

        @pl.when(k == 0)
        def _():
            acc[...] = (x + r_ref[...]) if has_res else x

        @pl.when(k > 0)
        def _():
            acc[...] += x

        @pl.when(k == nk - 1)
        def _():
            o_ref[...] = acc[...].astype(o_ref.dtype)

    ins = [a, b] + ([res] if has_res else [])
    specs = [a_bs, b_bs] + ([res_bs] if has_res else [])
    if dep is not None:
        ins.append(dep)
        specs.append(pl.BlockSpec((8, 128), lambda *_: (0, 0)))
    scratch = [pltpu.VMEM(acc_shape, F32)] if nk > 1 else []
    return pl.pallas_call(
        body, name=name, grid=grid, in_specs=specs, out_specs=o_bs,
        out_shape=jax.ShapeDtypeStruct(out_shape, out_dtype), scratch_shapes=scratch,
        compiler_params=_params(),
    )(*ins)


def _mm_nn(name, a, b, out_dtype, res=None, tm=1056, tn=512, dep=None):
    r, k = a.shape
    n = b.shape[1]
    tm, tn = _tile(r, tm, 8), _tile(n, tn, 128)
    return _matmul(name, a, b, (r, n), out_dtype, (r // tm, n // tn, 1),
                   pl.BlockSpec((tm, k), lambda i, j, s: (i, 0)),
                   pl.BlockSpec((k, tn), lambda i, j, s: (0, j)),
                   pl.BlockSpec((tm, tn), lambda i, j, s: (i, j)), NN, 1,
                   res=res, res_bs=pl.BlockSpec((tm, tn), lambda i, j, s: (i, j)), dep=dep)


def _mm_nn_kt(name, a, b, out_dtype, tm=1056, tn=1024, tk=640, dep=None):
    r, k = a.shape
    n = b.shape[1]
    tm, tn, tk = _tile(r, tm, 8), _tile(n, tn, 128), _tile(k, tk, 128)
    nk = k // tk
    return _matmul(name, a, b, (r, n), out_dtype, (r // tm, n // tn, nk),
                   pl.BlockSpec((tm, tk), lambda i, j, s: (i, s)),
                   pl.BlockSpec((tk, tn), lambda i, j, s: (s, j)),
                   pl.BlockSpec((tm, tn), lambda i, j, s: (i, j)), NN, nk, acc_shape=(tm, tn), dep=dep)


def _mm_nn_ksum(name, a3, b3, out_dtype, res=None, tm=1056, tn=1024, dep=None):
    e, r, kb = a3.shape
    n = b3.shape[2]
    tm, tn = _tile(r, tm, 8), _tile(n, tn, 128)
    return _matmul(name, a3, b3, (r, n), out_dtype, (r // tm, n // tn, e),
                   pl.BlockSpec((None, tm, kb), lambda i, j, s: (s, i, 0)),
                   pl.BlockSpec((None, kb, tn), lambda i, j, s: (s, 0, j)),
                   pl.BlockSpec((tm, tn), lambda i, j, s: (i, j)), NN, e, acc_shape=(tm, tn),
                   res=res, res_bs=pl.BlockSpec((tm, tn), lambda i, j, s: (i, j)), dep=dep)


def _mm_nt(name, a, b, out_dtype, res=None, tm=1056, tn=512, tk=640, n=None, dep=None):
    r, k = a.shape
    n = b.shape[0] if n is None else n
    tm, tn, tk = _tile(r, tm, 8), _tile(n, tn, 128), _tile(k, tk, 128)
    nk = k // tk
    return _matmul(name, a, b, (r, n), out_dtype, (r // tm, n // tn, nk),
                   pl.BlockSpec((tm, tk), lambda i, j, s: (i, s)),
                   pl.BlockSpec((tn, tk), lambda i, j, s: (j, s)),
                   pl.BlockSpec((tm, tn), lambda i, j, s: (i, j)), NT, nk, acc_shape=(tm, tn),
                   res=res, res_bs=pl.BlockSpec((tm, tn), lambda i, j, s: (i, j)), dep=dep)


def _mm_nt_bcols(name, a, b3, out_dtype, tm=1056, dep=None):
    r, k = a.shape
    e, n, _ = b3.shape
    tm = _tile(r, tm, 8)
    return _matmul(name, a, b3, (e, r, n), out_dtype, (r // tm, e, 1),
                   pl.BlockSpec((tm, k), lambda i, g, s: (i, 0)),
                   pl.BlockSpec((None, n, k), lambda i, g, s: (g, 0, 0)),
                   pl.BlockSpec((None, tm, n), lambda i, g, s: (g, i, 0)), NT, 1, dep=dep)


def _mm_tn(name, a, b, out_dtype, tm=1024, tn=640, dep=None):
    r, m = a.shape
    n = b.shape[1]
    tm, tn = _tile(m, tm, 128), _tile(n, tn, 128)
    return _matmul(name, a, b, (m, n), out_dtype, (m // tm, n // tn, 1),
                   pl.BlockSpec((r, tm), lambda i, j, s: (0, i)),
                   pl.BlockSpec((r, tn), lambda i, j, s: (0, j)),
                   pl.BlockSpec((tm, tn), lambda i, j, s: (i, j)), TN, 1, dep=dep)


def _mm_tn_acols(name, a3, b, out_dtype, tn=1024, dep=None):
    e, r, m = a3.shape
    n = b.shape[1]
    tn = _tile(n, tn, 128)
    return _matmul(name, a3, b, (e, m, n), out_dtype, (n // tn, e, 1),
                   pl.BlockSpec((None, r, m), lambda j, g, s: (g, 0, 0)),
                   pl.BlockSpec((r, tn), lambda j, g, s: (0, j)),
                   pl.BlockSpec((None, m, tn), lambda j, g, s: (g, 0, j)), TN, 1, dep=dep)


def _rms_fwd(name, h, w):
    r, d = h.shape
    tr = _tile(r, 264, 8)

    def body(h_ref, w_ref, o_ref):
        x = h_ref[...]
        rs = lax.rsqrt(jnp.mean(x * x, axis=1, keepdims=True) + EPS)
        o_ref[...] = (x * rs * w_ref[...]).astype(BF16)

    return pl.pallas_call(
        body, name=name, grid=(r // tr,),
        in_specs=[pl.BlockSpec((tr, d), lambda i: (i, 0)), pl.BlockSpec((1, d), lambda i: (0, 0))],
        out_specs=pl.BlockSpec((tr, d), lambda i: (i, 0)),
        out_shape=jax.ShapeDtypeStruct((r, d), BF16), compiler_params=_params(),
    )(h, w)


def _rms_bwd(name, x, w, dy, dres):
    r, d = x.shape
    tr = _tile(r, 264, 8)

    def body(x_ref, w_ref, dy_ref, dr_ref, dx_ref, dxb_ref, dw_ref):
        xv = x_ref[...]
        g = dy_ref[...]
        rs = lax.rsqrt(jnp.mean(xv * xv, axis=1, keepdims=True) + EPS)
        wg = g * w_ref[...]
        dx = rs * wg - xv * (rs * rs * rs) * jnp.mean(xv * wg, axis=1, keepdims=True) + dr_ref[...]
        dx_ref[...] = dx
        dxb_ref[...] = dx.astype(BF16)
        part = jnp.sum(g * xv * rs, axis=0, keepdims=True)

        @pl.when(pl.program_id(0) == 0)
        def _():
            dw_ref[...] = part

        @pl.when(pl.program_id(0) > 0)
        def _():
            dw_ref[...] += part

    row = pl.BlockSpec((tr, d), lambda i: (i, 0))
    vec = pl.BlockSpec((1, d), lambda i: (0, 0))
    return pl.pallas_call(
        body, name=name, grid=(r // tr,), in_specs=[row, vec, row, row], out_specs=[row, row, vec],
        out_shape=[jax.ShapeDtypeStruct((r, d), F32), jax.ShapeDtypeStruct((r, d), BF16),
                   jax.ShapeDtypeStruct((1, d), F32)],
        compiler_params=_params(),
    )(x, w, dy, dres)


def _final_loss(name, h, w, target):
    r, d = h.shape
    nb = r // CHUNK

    def body(h_ref, w_ref, t_ref, dh_ref, dhb_ref, dw_ref, ls_ref):
        i = pl.program_id(0)

        @pl.when(i == 0)
        def _():
            dh_ref[...] = jnp.zeros_like(dh_ref)
            dhb_ref[...] = jnp.zeros_like(dhb_ref)
            dw_ref[...] = jnp.zeros_like(dw_ref)
            ls_ref[...] = jnp.zeros_like(ls_ref)

        @pl.when(i > 0)
        def _():
            xv = h_ref[...]
            wv = w_ref[...]
            rs = lax.rsqrt(jnp.mean(xv * xv, axis=1, keepdims=True) + EPS)
            err = xv * rs * wv - t_ref[...]
            sq = jnp.sum(jnp.sum(err * err, axis=1, keepdims=True), axis=0, keepdims=True)
            ls_ref[...] += jnp.broadcast_to(sq * (0.5 / d), ls_ref.shape)
            g = err * (1.0 / d)
            wg = g * wv
            dx = rs * wg - xv * (rs * rs * rs) * jnp.mean(xv * wg, axis=1, keepdims=True)
            dh_ref[...] = dx
            dhb_ref[...] = dx.astype(BF16)
            dw_ref[...] += jnp.sum(g * xv * rs, axis=0, keepdims=True)

    row = pl.BlockSpec((CHUNK, d), lambda i: (i, 0))
    vec = pl.BlockSpec((1, d), lambda i: (0, 0))
    return pl.pallas_call(
        body, name=name, grid=(nb,),
        in_specs=[row, vec, pl.BlockSpec((CHUNK, d), lambda i: (jnp.maximum(i - 1, 0), 0))],
        out_specs=[row, row, vec, pl.BlockSpec((1, 128), lambda i: (0, 0))],
        out_shape=[jax.ShapeDtypeStruct((r, d), F32), jax.ShapeDtypeStruct((r, d), BF16),
                   jax.ShapeDtypeStruct((1, d), F32), jax.ShapeDtypeStruct((1, 128), F32)],
        compiler_params=_params(),
    )(h, w, target)


def _ffn_in(name, hf, wg_t, wu_t, dep=None, tm=1056, tn=512):
    r, d = hf.shape
    f = wg_t.shape[0]
    tm, tn = _tile(r, tm, 8), _tile(f, tn, 128)

    def body(h_ref, wg_ref, wu_ref, *rest):
        g_ref, u_ref, a_ref = rest[-3:]
        x = h_ref[...]
        g = lax.dot_general(x, wg_ref[...], (NT, ((), ())), preferred_element_type=F32)
        u = lax.dot_general(x, wu_ref[...], (NT, ((), ())), preferred_element_type=F32)
        g_ref[...] = g.astype(BF16)
        u_ref[...] = u.astype(BF16)
        a_ref[...] = (g * _sigmoid(g) * u).astype(BF16)

    wspec = pl.BlockSpec((tn, d), lambda i, j: (j, 0))
    ospec = pl.BlockSpec((tm, tn), lambda i, j: (i, j))
    ins, specs = [hf, wg_t, wu_t], [pl.BlockSpec((tm, d), lambda i, j: (i, 0)), wspec, wspec]
    if dep is not None:
        ins.append(dep)
        specs.append(pl.BlockSpec((8, 128), lambda *_: (0, 0)))
    return pl.pallas_call(
        body, name=name, grid=(r // tm, f // tn), in_specs=specs, out_specs=[ospec] * 3,
        out_shape=[jax.ShapeDtypeStruct((r, f), BF16)] * 3, compiler_params=_params(),
    )(*ins)


def _ffn_act_bwd(name, dh, wd, g, u, dep=None, tm=1056, tn=512):
    r, d = dh.shape
    f = wd.shape[0]
    tm, tn = _tile(r, tm, 8), _tile(f, tn, 128)

    def body(dh_ref, wd_ref, g_ref, u_ref, *rest):
        dg_ref, du_ref = rest[-2:]
        da = lax.dot_general(dh_ref[...], wd_ref[...], (NT, ((), ())), preferred_element_type=F32)
        gv = g_ref[...].astype(F32)
        s = _sigmoid(gv)
        dg_ref[...] = (da * u_ref[...].astype(F32) * (s + gv * s * (1.0 - s))).astype(BF16)
        du_ref[...] = (da * gv * s).astype(BF16)

    tile = pl.BlockSpec((tm, tn), lambda i, j: (i, j))
    ins = [dh, wd, g, u]
    specs = [pl.BlockSpec((tm, d), lambda i, j: (i, 0)), pl.BlockSpec((tn, d), lambda i, j: (j, 0)), tile, tile]
    if dep is not None:
        ins.append(dep)
        specs.append(pl.BlockSpec((8, 128), lambda *_: (0, 0)))
    return pl.pallas_call(
        body, name=name, grid=(r // tm, f // tn), in_specs=specs, out_specs=[tile] * 2,
        out_shape=[jax.ShapeDtypeStruct((r, f), BF16)] * 2, compiler_params=_params(),
    )(*ins)


def _shift_down(a, k):
    row = lax.broadcasted_iota(jnp.int32, a.shape, 0)
    return jnp.where(row >= k, pltpu.roll(a, k, 0), 0.0)


def _shift_up(a, k):
    n = a.shape[0]
    row = lax.broadcasted_iota(jnp.int32, a.shape, 0)
    return jnp.where(row < n - k, pltpu.roll(a, n - k, 0), 0.0)


def _conv_fwd(name, pc, cw):
    _, r, w = pc.shape

    def body(pc_ref, cw_ref, o_ref):
        a = pc_ref[2] * pc_ref[0]
        cwv = cw_ref[...]
        conv = _shift_down(a, 2) * cwv[0:1] + _shift_down(a, 1) * cwv[1:2] + a * cwv[2:3]
        o_ref[...] = (pc_ref[1] * conv).astype(BF16)

    return pl.pallas_call(
        body, name=name, grid=(w // 128,),
        in_specs=[pl.BlockSpec((3, r, 128), lambda j: (0, 0, j)), pl.BlockSpec((8, 128), lambda j: (0, j))],
        out_specs=pl.BlockSpec((r, 128), lambda j: (0, j)),
        out_shape=jax.ShapeDtypeStruct((r, w), BF16), compiler_params=_params(),
    )(pc, cw)


def _conv_bwd(name, dcat, pc, cw):
    _, r, w = pc.shape
    nblk = w // 128

    def body(dy_ref, pc_ref, cw_ref, dpc_ref, dcw_ref):
        u, gb, gc = pc_ref[0], pc_ref[1], pc_ref[2]
        cwv = cw_ref[...]
        dy = dy_ref[...]
        a = gc * u
        a1, a2 = _shift_down(a, 1), _shift_down(a, 2)
        conv = a2 * cwv[0:1] + a1 * cwv[1:2] + a * cwv[2:3]
        dconv = dy * gb
        da = dconv * cwv[2:3] + _shift_up(dconv, 1) * cwv[1:2] + _shift_up(dconv, 2) * cwv[0:1]
        dpc_ref[0] = (da * gc).astype(BF16)
        dpc_ref[1] = (dy * conv).astype(BF16)
        dpc_ref[2] = (da * u).astype(BF16)
        row = lax.broadcasted_iota(jnp.int32, (8, 128), 0)
        dw0 = jnp.sum(dconv * a2, axis=0, keepdims=True)
        dw1 = jnp.sum(dconv * a1, axis=0, keepdims=True)
        dw2 = jnp.sum(dconv * a, axis=0, keepdims=True)
        dcw_ref[...] = jnp.where(row == 0, dw0, jnp.where(row == 1, dw1, jnp.where(row == 2, dw2, 0.0)))

    return pl.pallas_call(
        body, name=name, grid=(nblk,),
        in_specs=[pl.BlockSpec((r, 128), lambda j: (0, nblk + j)),
                  pl.BlockSpec((3, r, 128), lambda j: (0, 0, j)), pl.BlockSpec((8, 128), lambda j: (0, j))],
        out_specs=[pl.BlockSpec((3, r, 128), lambda j: (0, 0, j)), pl.BlockSpec((8, 128), lambda j: (0, j))],
        out_shape=[jax.ShapeDtypeStruct((3, r, w), BF16), jax.ShapeDtypeStruct((8, w), F32)],
        compiler_params=_params(),
    )(dcat, pc, cw)


def _dot(a, b, dims):
    return lax.dot_general(a, b, (dims, ((), ())), preferred_element_type=F32)


def _col_to_row(xc, eye):
    return jnp.sum(jnp.where(eye, xc, 0.0), axis=0, keepdims=True)


def _row_to_col(xr, eye):
    return jnp.sum(jnp.where(eye, xr, 0.0), axis=1, keepdims=True)


def _gate_tiles(graw, bias, row0):
    th = jnp.tanh((graw + bias) / GATE_CAP)
    z = GATE_CAP * th
    row = lax.broadcasted_iota(jnp.int32, graw.shape, 0) + row0
    real = row >= PAD_FRONT
    li = jnp.where(real, z, -jnp.inf)
    lf = jnp.where(real, jnp.minimum(z, 0.0) - jnp.log(1.0 + jnp.exp(-jnp.abs(z))), 0.0)
    return th, z, li, lf, real


def _chunk_common(pm, h, li, lf, cst, nst, mst, tril, eye):
    kraw = pm[:, QK_W + h * DQK:QK_W + (h + 1) * DQK]
    q = (pm[:, h * DQK:(h + 1) * DQK] * QSCALE).astype(BF16)
    k = kraw.astype(BF16)
    v = pm[:, 2 * QK_W + h * DV:2 * QK_W + (h + 1) * DV].astype(BF16)
    li_c = li[:, h:h + 1]
    lf_c = lf[:, HEADS + h:HEADS + h + 1]
    li_r = _col_to_row(li_c, eye)
    lf_r = _col_to_row(lf_c, eye)
    b_c = jnp.sum(jnp.where(tril, lf_r, 0.0), axis=1, keepdims=True)
    b_r = _col_to_row(b_c, eye)
    dmat = jnp.where(tril, b_c - b_r + li_r, -jnp.inf)
    inter = b_c + mst
    mt = jnp.maximum(inter, jnp.max(dmat, axis=1, keepdims=True))
    w_inter = jnp.exp(inter - mt)
    p = jnp.exp(dmat - mt)
    s = _dot(q, k, NT) * p
    cb = cst.astype(BF16)
    nb = nst.astype(BF16).astype(F32)
    qc = _dot(q, cb, NN)
    qn = jnp.sum(q.astype(F32) * nb, axis=1, keepdims=True)
    den = w_inter * qn + jnp.sum(s, axis=1, keepdims=True)
    dn = jnp.maximum(jnp.abs(den), jnp.exp(-mt))
    b_end = b_c[CHUNK - 1:CHUNK, :]
    decay = b_end - b_c + li_c
    m_new = jnp.maximum(b_end + mst, jnp.max(decay, axis=0, keepdims=True))
    w_old = jnp.exp(b_end + mst - m_new)
    w_in = jnp.exp(decay - m_new)
    kw = (w_in * kraw).astype(BF16)
    return dict(q=q, k=k, v=v, kraw=kraw, mt=mt, w_inter=w_inter, p=p, s=s, cb=cb, nb=nb, qc=qc, qn=qn,
                den=den, dn=dn, m_new=m_new, w_old=w_old, w_in=w_in, kw=kw)


def _mlstm_fwd(name, pm, bias, nw):
    r = pm.shape[0]
    nc = r // CHUNK

    def body(pm_ref, b_ref, nw_ref, hm_ref, ht_ref, cs_ref, ns_ref, ms_ref, c_scr, n_scr, m_scr):
        ci = pl.program_id(0)

        @pl.when(ci == 0)
        def _():
            c_scr[...] = jnp.zeros_like(c_scr)
            n_scr[...] = jnp.zeros_like(n_scr)
            m_scr[...] = jnp.zeros_like(m_scr)

        pmv = pm_ref[...]
        rr = lax.broadcasted_iota(jnp.int32, (CHUNK, CHUNK), 0)
        cc = lax.broadcasted_iota(jnp.int32, (CHUNK, CHUNK), 1)
        tril, eye = cc <= rr, cc == rr
        _, _, li, lf, _ = _gate_tiles(pmv[:, GATE_COL:GATE_COL + 128], b_ref[...], ci * CHUNK)
        nwv = nw_ref[...]
        for h in range(HEADS):
            cst, nst, mst = c_scr[h], n_scr[h], m_scr[h]
            cs_ref[h] = cst
            ns_ref[h] = nst
            ms_ref[h] = mst
            f = _chunk_common(pmv, h, li, lf, cst, nst, mst, tril, eye)
            num = f["w_inter"] * f["qc"] + _dot(f["s"].astype(BF16), f["v"], NN)
            hh = num / f["dn"]
            c_scr[h] = f["w_old"] * cst + _dot(f["kw"], f["v"], TN)
            n_scr[h] = f["w_old"] * nst + jnp.sum(
                f["w_in"].astype(BF16).astype(F32) * f["k"].astype(F32), axis=0, keepdims=True)
            m_scr[h] = f["m_new"]
            sl = slice(h * DV, (h + 1) * DV)
            rs = lax.rsqrt(jnp.mean(hh * hh, axis=1, keepdims=True) + EPS)
            og = pmv[:, 2 * QK_W + MLSTM_W + h * DV:2 * QK_W + MLSTM_W + (h + 1) * DV]
            ht_ref[:, sl] = hh
            hm_ref[:, sl] = (_sigmoid(og) * (hh * rs * nwv[:, sl])).astype(BF16)

    return pl.pallas_call(
        body, name=name, grid=(nc,),
        in_specs=[pl.BlockSpec((CHUNK, PM_W), lambda i: (i, 0)), pl.BlockSpec((1, 128), lambda i: (0, 0)),
                  pl.BlockSpec((1, MLSTM_W), lambda i: (0, 0))],
        out_specs=[pl.BlockSpec((CHUNK, MLSTM_W), lambda i: (i, 0)),
                   pl.BlockSpec((CHUNK, MLSTM_W), lambda i: (i, 0)),
                   pl.BlockSpec((None, HEADS, DQK, DV), lambda i: (i, 0, 0, 0)),
                   pl.BlockSpec((None, HEADS, 1, DQK), lambda i: (i, 0, 0, 0)),
                   pl.BlockSpec((None, HEADS, 1, 1), lambda i: (i, 0, 0, 0))],
        out_shape=[jax.ShapeDtypeStruct((r, MLSTM_W), BF16), jax.ShapeDtypeStruct((r, MLSTM_W), F32),
                   jax.ShapeDtypeStruct((nc, HEADS, DQK, DV), F32),
                   jax.ShapeDtypeStruct((nc, HEADS, 1, DQK), F32),
                   jax.ShapeDtypeStruct((nc, HEADS, 1, 1), F32)],
        scratch_shapes=[pltpu.VMEM((HEADS, DQK, DV), F32), pltpu.VMEM((HEADS, 1, DQK), F32),
                        pltpu.VMEM((HEADS, 1, 1), F32)],
        compiler_params=_params(),
    )(pm, bias, nw)


def _mlstm_bwd(name, dcat, pm, ht, cs, ns, ms, bias, nw):
    r = pm.shape[0]
    nc = r // CHUNK

    def body(dy_ref, pm_ref, ht_ref, cs_ref, ns_ref, ms_ref, b_ref, nw_ref, dpm_ref, dnw_ref, db_ref,
             dc_scr, dn_scr):
        step = pl.program_id(0)
        ci = nc - 1 - step

        @pl.when(step == 0)
        def _():
            dc_scr[...] = jnp.zeros_like(dc_scr)
            dn_scr[...] = jnp.zeros_like(dn_scr)
            dnw_ref[...] = jnp.zeros_like(dnw_ref)
            db_ref[...] = jnp.zeros_like(db_ref)

        pmv = pm_ref[...]
        rr = lax.broadcasted_iota(jnp.int32, (CHUNK, CHUNK), 0)
        cc = lax.broadcasted_iota(jnp.int32, (CHUNK, CHUNK), 1)
        tril, eye, triu = cc <= rr, cc == rr, cc >= rr
        th, z, li, lf, real = _gate_tiles(pmv[:, GATE_COL:GATE_COL + 128], b_ref[...], ci * CHUNK)
        lane = lax.broadcasted_iota(jnp.int32, (CHUNK, 128), 1)
        rowid = lax.broadcasted_iota(jnp.int32, (CHUNK, 1), 0)
        nwv = nw_ref[...]
        dgt = jnp.zeros((CHUNK, 128), F32)
        for h in range(HEADS):
            cst, nst, mst = cs_ref[h], ns_ref[h], ms_ref[h]
            f = _chunk_common(pmv, h, li, lf, cst, nst, mst, tril, eye)
            q, k, v, s, p = f["q"], f["k"], f["v"], f["s"], f["p"]
            w_inter, w_in, w_old, dn = f["w_inter"], f["w_in"], f["w_old"], f["dn"]
            sl = slice(h * DV, (h + 1) * DV)
            osl = slice(2 * QK_W + MLSTM_W + h * DV, 2 * QK_W + MLSTM_W + (h + 1) * DV)
            hh = ht_ref[:, sl]
            y = dy_ref[:, sl]
            sg = _sigmoid(pmv[:, osl])
            rs = lax.rsqrt(jnp.mean(hh * hh, axis=1, keepdims=True) + EPS)
            nwh = nwv[:, sl]
            dpm_ref[:, osl] = (y * (hh * rs * nwh) * sg * (1.0 - sg)).astype(BF16)
            dhn = y * sg
            dnw_ref[:, sl] += jnp.sum(dhn * hh * rs, axis=0, keepdims=True)
            wd = dhn * nwh
            dhh = rs * wd - hh * (rs * rs * rs) * jnp.mean(hh * wd, axis=1, keepdims=True)
            dnum = dhh / dn
            dd = -jnp.sum(dhh * hh, axis=1, keepdims=True) / dn
            dden = jnp.where(jnp.abs(f["den"]) > jnp.exp(-f["mt"]), dd * jnp.sign(f["den"]), 0.0)
            dnum_b = dnum.astype(BF16)
            wdn = (w_inter * dnum).astype(BF16)
            wid = (w_inter * dden).astype(BF16).astype(F32)
            ds = _dot(dnum_b, v, NT) + dden
            dsp = (ds * p).astype(BF16)
            dq = _dot(dsp, k, NN) + _dot(wdn, f["cb"], NT) + wid * f["nb"]
            dk = _dot(dsp, q, TN)
            dv = _dot(s.astype(BF16), dnum_b, TN)
            g = ds * s
            g_col = _row_to_col(jnp.sum(g, axis=0, keepdims=True), eye)
            db = jnp.sum(g, axis=1, keepdims=True) - g_col
            dli = g_col
            db = db + (jnp.sum(dnum * f["qc"], axis=1, keepdims=True) + dden * f["qn"]) * w_inter
            dcn, dnn = dc_scr[h], dn_scr[h]
            dcnb = dcn.astype(BF16)
            dnnb = dnn.astype(BF16).astype(F32)
            dkw = _dot(v, dcnb, NT) + dnnb
            dk = dk + w_in * dkw
            dv = dv + _dot(f["kw"], dcnb, NN)
            ddecay = jnp.sum(dkw * f["kraw"], axis=1, keepdims=True) * w_in
            dw_old = (jnp.sum(jnp.sum(dcn * cst, axis=1, keepdims=True), axis=0, keepdims=True)
                      + jnp.sum(dnn * nst, axis=1, keepdims=True))
            db_end = dw_old * w_old + jnp.sum(ddecay, axis=0, keepdims=True)
            db = db - ddecay + jnp.where(rowid == CHUNK - 1, db_end, 0.0)
            dli = dli + ddecay
            dc_scr[h] = w_old * dcn + _dot(q, wdn, TN)
            dn_scr[h] = w_old * dnn + jnp.sum(wid * q.astype(F32), axis=0, keepdims=True)
            dlf = jnp.sum(jnp.where(triu, _col_to_row(db, eye), 0.0), axis=1, keepdims=True)
            dgt = dgt + jnp.where(lane == h, dli, 0.0) + jnp.where(lane == HEADS + h, dlf, 0.0)
            dpm_ref[:, h * DQK:(h + 1) * DQK] = (dq * QSCALE).astype(BF16)
            dpm_ref[:, QK_W + h * DQK:QK_W + (h + 1) * DQK] = dk.astype(BF16)
            dpm_ref[:, 2 * QK_W + h * DV:2 * QK_W + (h + 1) * DV] = dv.astype(BF16)
        dact = jnp.where(lane < HEADS, 1.0, 1.0 - _sigmoid(z)) * (1.0 - th * th)
        dgraw = jnp.where(real & (lane < 2 * HEADS), dgt * dact, 0.0)
        dpm_ref[:, GATE_COL:GATE_COL + 128] = dgraw.astype(BF16)
        db_ref[...] += jnp.sum(dgraw, axis=0, keepdims=True)

    rev = lambda i: (nc - 1 - i, 0)
    rev4 = lambda i: (nc - 1 - i, 0, 0, 0)
    return pl.pallas_call(
        body, name=name, grid=(nc,),
        in_specs=[pl.BlockSpec((CHUNK, MLSTM_W), rev), pl.BlockSpec((CHUNK, PM_W), rev),
                  pl.BlockSpec((CHUNK, MLSTM_W), rev),
                  pl.BlockSpec((None, HEADS, DQK, DV), rev4), pl.BlockSpec((None, HEADS, 1, DQK), rev4),
                  pl.BlockSpec((None, HEADS, 1, 1), rev4),
                  pl.BlockSpec((1, 128), lambda i: (0, 0)), pl.BlockSpec((1, MLSTM_W), lambda i: (0, 0))],
        out_specs=[pl.BlockSpec((CHUNK, PM_W), rev), pl.BlockSpec((1, MLSTM_W), lambda i: (0, 0)),
                   pl.BlockSpec((1, 128), lambda i: (0, 0))],
        out_shape=[jax.ShapeDtypeStruct((r, PM_W), BF16), jax.ShapeDtypeStruct((1, MLSTM_W), F32),
                   jax.ShapeDtypeStruct((1, 128), F32)],
        scratch_shapes=[pltpu.VMEM((HEADS, DQK, DV), F32), pltpu.VMEM((HEADS, 1, DQK), F32)],
        compiler_params=_params(),
    )(dcat, pm, ht, cs, ns, ms, bias, nw)


def _my_place():
    return lax.axis_index("x"), lax.axis_index("y"), lax.axis_index("c")


def _flip(v, bit):
    return 1 - v if bit else v


def _exchange_small(name, blk, reduce):
    r, c = blk.shape

    def body(x_ref, o_ref, *rest):
        slots = rest[0] if reduce else o_ref
        send_sems, recv_sems = rest[-2], rest[-1]
        x, y, cc = _my_place()
        me = 4 * x + 2 * y + cc
        slots[me] = x_ref[...]
        copies = []
        for k in range(1, N_DEV):
            peer = (_flip(x, k & 4), _flip(y, k & 2), _flip(cc, k & 1))
            cp = pltpu.make_async_remote_copy(
                src_ref=x_ref, dst_ref=slots.at[me], send_sem=send_sems.at[k - 1],
                recv_sem=recv_sems.at[k - 1], device_id=peer, device_id_type=MESH)
            cp.start()
            copies.append(cp)
        for cp in copies:
            cp.wait()
        if reduce:
            acc = slots[0]
            for d in range(1, N_DEV):
                acc = acc + slots[d]
            o_ref[...] = acc

    scratch = ([pltpu.VMEM((N_DEV, r, c), F32)] if reduce else []) + [
        pltpu.SemaphoreType.DMA((N_DEV - 1,)), pltpu.SemaphoreType.DMA((N_DEV - 1,))]
    return pl.pallas_call(
        body, name=name,
        out_shape=jax.ShapeDtypeStruct((r, c) if reduce else (N_DEV, r, c), F32),
        in_specs=[pl.BlockSpec(memory_space=pltpu.VMEM)], out_specs=pl.BlockSpec(memory_space=pltpu.VMEM),
        scratch_shapes=scratch, compiler_params=_params(),
    )(blk)


HBM_SPEC = pl.BlockSpec(memory_space=pltpu.HBM)
SEM_SPEC = pl.BlockSpec(memory_space=pltpu.SEMAPHORE)
ANY_SPEC = pl.BlockSpec(memory_space=pl.ANY)
DATAFLOW = pltpu.SideEffectType.DATAFLOW_SIDE_EFFECTING


def _split_copy(name, arrays, start=None, wait=None, after=None):
    n = len(arrays)
    n_wait = 2 if wait else 0
    n_after = 0 if after is None else 1
    n_new = 2 if start else 0

    def body(*refs):
        ins = refs[:n]
        if wait:
            for cp in wait[0](ins, refs[n], refs[n + 1]):
                cp.wait_send()
                cp.wait_recv()
        if start:
            at = n + n_wait + n_after
            for cp in start[0](ins, refs[at], refs[at + 1]):
                cp.start()
            token = refs[at + 2 + n]
            token[...] = jnp.zeros_like(token)

    operands = [pltpu.with_memory_space_constraint(a, pltpu.HBM) for a in arrays]
    in_specs = [HBM_SPEC] * n
    if wait:
        operands += list(wait[1])
        in_specs += [SEM_SPEC, SEM_SPEC]
    if after is not None:
        operands.append(after)
        in_specs.append(ANY_SPEC)
    out_shape, out_specs = [], []
    if start:
        out_shape += [pltpu.SemaphoreType.DMA((start[1],)), pltpu.SemaphoreType.DMA((start[1],))]
        out_specs += [SEM_SPEC, SEM_SPEC]
    out_shape += [pltpu.HBM(a.shape, a.dtype) for a in arrays]
    out_specs += [HBM_SPEC] * n
    if start:
        out_shape.append(jax.ShapeDtypeStruct((8, 128), F32))
        out_specs.append(pl.BlockSpec(memory_space=pltpu.VMEM))
    outs = pl.pallas_call(
        body, name=name, in_specs=in_specs, out_specs=out_specs, out_shape=out_shape,
        input_output_aliases={i: n_new + i for i in range(n)},
        compiler_params=pltpu.CompilerParams(has_side_effects=DATAFLOW),
    )(*operands)
    thru = list(outs[n_new:n_new + n])
    return thru, (tuple(outs[:2]) if start else None), (outs[n_new + n] if start else None)


def _remote(src, dst, send_sems, recv_sems, k, to):
    return pltpu.make_async_remote_copy(src_ref=src, dst_ref=dst, send_sem=send_sems.at[k],
                                        recv_sem=recv_sems.at[k], device_id=to, device_id_type=MESH)


def _slot(px, py, pc):
    return 4 * px + 2 * py + pc


def _gather_first(refs, send_sems, recv_sems):
    x, y, c = _my_place()
    blk = refs[0].at[_slot(x, y, c)]
    targets = [(x, y, 1 - c), (1 - x, y, c), (x, 1 - y, c)]
    return [_remote(blk, blk, send_sems, recv_sems, k, to) for k, to in enumerate(targets)]


def _gather_relay(refs, send_sems, recv_sems):
    x, y, c = _my_place()
    rows = refs[0].shape[1]
    half = rows // 32 * 16
    from_x, from_y = _slot(1 - x, y, c), _slot(x, 1 - y, c)
    upper = refs[0].at[from_x, pl.ds(0, half)]
    lower = refs[0].at[from_y, pl.ds(half, rows - half)]
    return [_remote(upper, upper, send_sems, recv_sems, 0, (x, 1 - y, c)),
            _remote(lower, lower, send_sems, recv_sems, 1, (1 - x, y, c)),
            _remote(refs[0].at[from_x], refs[0].at[from_x], send_sems, recv_sems, 2, (x, y, 1 - c)),
            _remote(refs[0].at[from_y], refs[0].at[from_y], send_sems, recv_sems, 3, (x, y, 1 - c))]


def _gather_last(refs, send_sems, recv_sems):
    x, y, c = _my_place()
    blk = refs[0].at[_slot(1 - x, 1 - y, c)]
    return [_remote(blk, blk, send_sems, recv_sems, 0, (x, y, 1 - c))]


def _scatter_sibling(n):
    def copies(refs, send_sems, recv_sems):
        x, y, c = _my_place()
        return [_remote(refs[a].at[2 * j + 1 - c], refs[n + a].at[j], send_sems, recv_sems, 4 * a + j, (x, y, 1 - c))
                for a in range(n) for j in range(4)]
    return copies


def _scatter_chips(n):
    def copies(refs, send_sems, recv_sems):
        x, y, c = _my_place()
        out = []
        for a in range(n):
            for k in range(1, 4):
                px, py = _flip(x, k & 2), _flip(y, k & 1)
                out.append(_remote(refs[a].at[2 * px + py], refs[n + a].at[2 * x + y], send_sems, recv_sems,
                                   3 * a + k - 1, (px, py, c)))
        return out
    return copies


def _pair_sum(name, core, g, t):
    _, r, c = g.shape
    tr = _tile(r, 512, 8)
    g4 = g.reshape(4, 2, r, c)

    def body(core_ref, g_ref, t_ref, o_ref):
        o_ref[...] = (g_ref[...].astype(F32) + t_ref[...].astype(F32)).astype(BF16)

    return pl.pallas_call(
        body, name=name,
        grid_spec=pltpu.PrefetchScalarGridSpec(
            num_scalar_prefetch=1, grid=(4, r // tr),
            in_specs=[pl.BlockSpec((None, None, tr, c), lambda j, i, core_ref: (j, core_ref[0], i, 0)),
                      pl.BlockSpec((None, tr, c), lambda j, i, core_ref: (j, i, 0))],
            out_specs=pl.BlockSpec((None, tr, c), lambda j, i, core_ref: (j, i, 0))),
        out_shape=jax.ShapeDtypeStruct((4, r, c), BF16), compiler_params=_params(),
    )(core, g4, t)


def _adam_math(w, g, m, v):
    m2 = ADAM_B1 * m + (1.0 - ADAM_B1) * g
    v2 = ADAM_B2 * v + (1.0 - ADAM_B2) * (g * g)
    m_hat = m2 / (1.0 - ADAM_B1 ** ADAM_STEP)
    v_hat = v2 / (1.0 - ADAM_B2 ** ADAM_STEP)
    delta = -ADAM_LR * (m_hat / (jnp.sqrt(v_hat) + ADAM_EPS) + ADAM_WD * w)
    return delta, m2, v2


def _adam_sharded(name, chip, w, m, v, grads, row_off=0):
    _, r, c = w.shape
    tr = _tile(r, 256, 8)
    tc = c if tr < r else _tile(c, 256, 128)
    boff = row_off // tr

    def body(chip_ref, w_ref, m_ref, v_ref, p0_ref, q0_ref, p1_ref, q1_ref, g_ref, d_ref, nm_ref, nv_ref):
        mine = chip_ref[0]

        def total(p_ref, q_ref):
            acc = None
            for j in range(4):
                part = jnp.where(mine == j, p_ref[...], q_ref[j]).astype(F32)
                acc = part if acc is None else acc + part
            return acc

        g = jnp.where(pl.program_id(0) == 0, total(p0_ref, q0_ref), total(p1_ref, q1_ref))
        delta, m2, v2 = _adam_math(w_ref[...], g, m_ref[...], v_ref[...])
        g_ref[...] = g
        d_ref[...] = delta
        nm_ref[...] = m2
        nv_ref[...] = v2

    def grad_specs(layer):
        at = lambda l, i, j: (jnp.where(l == layer, boff + i, boff), jnp.where(l == layer, j, 0))
        return [pl.BlockSpec((None, tr, tc), lambda l, i, j, chip_ref: (chip_ref[0],) + at(l, i, j)),
                pl.BlockSpec((4, tr, tc), lambda l, i, j, chip_ref: (0,) + at(l, i, j))]

    wspec = pl.BlockSpec((None, tr, tc), lambda l, i, j, chip_ref: (l, i, j))
    sds = jax.ShapeDtypeStruct(w.shape, F32)
    return pl.pallas_call(
        body, name=name,
        grid_spec=pltpu.PrefetchScalarGridSpec(
            num_scalar_prefetch=1, grid=(2, r // tr, c // tc),
            in_specs=[wspec, wspec, wspec] + grad_specs(0) + grad_specs(1), out_specs=[wspec] * 4),
        out_shape=[sds] * 4, compiler_params=_params(),
    )(chip, w, m, v, grads[0][0], grads[0][1], grads[1][0], grads[1][1])


def _adam_small(name, w, m, v, g):
    def body(w_ref, m_ref, v_ref, g_ref, d_ref, nm_ref, nv_ref):
        delta, m2, v2 = _adam_math(w_ref[...], g_ref[...], m_ref[...], v_ref[...])
        d_ref[...] = delta
        nm_ref[...] = m2
        nv_ref[...] = v2

    sds = jax.ShapeDtypeStruct(w.shape, F32)
    vm = pl.BlockSpec(memory_space=pltpu.VMEM)
    return pl.pallas_call(body, name=name, in_specs=[vm] * 4, out_specs=[vm] * 3, out_shape=[sds] * 3,
                          compiler_params=_params())(w, m, v, g)


GATE_END = GATE_COL + 2 * HEADS


def _merge_dw_in(dwm_t, dwc_t):
    full = jnp.concatenate([dwm_t[:GATE_END], dwc_t.reshape(3 * CONV_W, D_MODEL)], axis=0)
    return full.reshape(N_DEV, IN_SH, D_MODEL)


def _pack128(parts):
    flat = jnp.concatenate([p.reshape(-1) for p in parts])
    n = flat.shape[0]
    rows = -(-n // 1024) * 8
    return jnp.pad(flat, (0, rows * 128 - n)).reshape(rows, 128)


def _unpack128(packed, shapes):
    flat = packed.reshape(-1)
    out, at = [], 0
    for s in shapes:
        n = int(np.prod(s))
        out.append(flat[at:at + n].reshape(s))
        at += n
    return out


def kernel(x, meta_tokens, norm_mix_w, w_in, b_gates, conv_w, mlstm_norm_w, w_out, norm_ffn_w, w_gate, w_up, w_down, norm_final_w, loss_target, m_meta_tokens, m_norm_mix_w, m_w_in, m_b_gates, m_conv_w, m_mlstm_norm_w, m_w_out, m_norm_ffn_w, m_w_gate, m_w_up, m_w_down, m_norm_final_w, v_meta_tokens, v_norm_mix_w, v_w_in, v_b_gates, v_conv_w, v_mlstm_norm_w, v_w_out, v_norm_ffn_w, v_w_gate, v_w_up, v_w_down, v_norm_final_w):
    seq = x.shape[1]
    rows = TOK0 + seq
    me = 4 * lax.axis_index("x") + 2 * lax.axis_index("y") + lax.axis_index("c")
    meta_sh = meta_tokens.shape[1]
    conv_sh = conv_w.shape[2]

    w_gate_t, m_w_gate_t, v_w_gate_t = (jnp.transpose(a, (0, 2, 1)) for a in (w_gate, m_w_gate, v_w_gate))
    w_up_t, m_w_up_t, v_w_up_t = (jnp.transpose(a, (0, 2, 1)) for a in (w_up, m_w_up, v_w_up))
    shards = []
    for l in range(DEPTH):
        shards += [jnp.transpose(w_in[l]).astype(BF16), w_out[l].astype(BF16), w_gate_t[l].astype(BF16),
                   w_up_t[l].astype(BF16), w_down[l].astype(BF16)]
    per_layer = ("w_in", "w_out", "w_gate", "w_up", "w_down")
    gather_names = [f"{nm}_{l}" for l in range(DEPTH) for nm in per_layer]
    gather_state = {}

    def gather_start(i, after):
        if i >= len(shards):
            return after
        buf = lax.dynamic_update_index_in_dim(lax.empty((N_DEV,) + shards[i].shape, BF16), shards[i], me, 0)
        arrs, sems, tok = _split_copy(f"gather_start_{gather_names[i]}", [buf], start=(_gather_first, 3), after=after)
        gather_state[i] = (arrs, sems)
        return tok

    def gather_relay(i, after):
        if i >= len(shards):
            return after
        arrs, sems = gather_state[i]
        arrs, sems, tok = _split_copy(f"gather_relay_{gather_names[i]}", arrs, start=(_gather_relay, 4),
                                      wait=(_gather_first, sems), after=after)
        gather_state[i] = (arrs, sems)
        return tok

    def gather_last(i, after):
        arrs, sems = gather_state[i]
        arrs, sems, tok = _split_copy(f"gather_last_{gather_names[i]}", arrs, start=(_gather_last, 1),
                                      wait=(_gather_relay, sems), after=after)
        gather_state[i] = (arrs, sems)
        return tok

    def gathered(i, after):
        arrs, sems = gather_state[i]
        arrs, _, _ = _split_copy(f"gather_done_{gather_names[i]}", arrs, wait=(_gather_last, sems), after=after)
        return arrs[0]

    bias = [jnp.pad(b_gates[l].reshape(1, 2 * HEADS), ((0, 0), (0, 128 - 2 * HEADS))) for l in range(DEPTH)]
    nmix = [norm_mix_w[l].reshape(1, D_MODEL) for l in range(DEPTH)]
    nffn = [norm_ffn_w[l].reshape(1, D_MODEL) for l in range(DEPTH)]
    nmls = [mlstm_norm_w[l].reshape(1, MLSTM_W) for l in range(DEPTH)]
    weights = [dict() for _ in range(DEPTH)]
    saved = [dict() for _ in range(DEPTH)]

    def layer_fwd(l, h, after):
        w, s = weights[l], saved[l]
        k0 = len(per_layer) * l
        tok = gather_last(k0, after)
        w["win_t"] = gathered(k0, tok).reshape(D_IN, D_MODEL)
        w["wc_t"] = w["win_t"][GATE_END:].reshape(3, CONV_W, D_MODEL)
        tok = gather_start(k0 + 3, gather_relay(k0 + 1, tok))
        s["h0"] = h
        s["hn"] = _rms_fwd(f"norm_mix_{l}", h, nmix[l] + tok[0, 0])
        s["pm"] = _mm_nt(f"proj_mlstm_{l}", s["hn"], w["win_t"], F32, tn=640, tk=D_MODEL, n=PM_W)
        tok = gather_start(k0 + 4, gather_relay(k0 + 2, s["pm"]))
        tok = gather_last(k0 + 1, tok)
        s["pc"] = _mm_nt_bcols(f"proj_conv_{l}", s["hn"], w["wc_t"], F32, dep=tok)
        hm, s["ht"], s["cs"], s["ns"], s["ms"] = _mlstm_fwd(f"mlstm_fwd_{l}", s["pm"], bias[l] + tok[:1], nmls[l])
        tok = gather_start(k0 + 5, gather_relay(k0 + 3, hm))
        tok = gather_last(k0 + 2, tok)
        hc = _conv_fwd(f"conv_fwd_{l}", s["pc"], conv_rows[l] + tok[0, 0])
        s["cat"] = jnp.concatenate([hm, hc], axis=1)
        w["wo"] = gathered(k0 + 1, s["cat"]).reshape(D_MODEL, D_MODEL)
        s["h1"] = _mm_nn(f"out_proj_{l}", s["cat"], w["wo"], F32, res=s["h0"])
        tok = gather_start(k0 + 6, gather_relay(k0 + 4, s["h1"]))
        s["hf"] = _rms_fwd(f"norm_ffn_{l}", s["h1"], nffn[l] + tok[0, 0])
        tok = gather_last(k0 + 3, s["hf"])
        w["wg_t"] = gathered(k0 + 2, tok).reshape(D_FF, D_MODEL)
        w["wu_t"] = gathered(k0 + 3, tok).reshape(D_FF, D_MODEL)
        s["g"], s["u"], s["act"] = _ffn_in(f"ffn_in_{l}", s["hf"], w["wg_t"], w["wu_t"])
        tok = gather_last(k0 + 4, s["act"])
        w["wd"] = gathered(k0 + 4, tok).reshape(D_FF, D_MODEL)
        tok = gather_start(k0 + 7, gather_relay(k0 + 5, tok))
        return _mm_nn(f"ffn_out_{l}", s["act"], w["wd"], F32, res=s["h1"], dep=tok)

    tok = gather_start(0, None)
    zero = tok[0, 0]
    small = jnp.concatenate(
        [meta_tokens + zero, jnp.pad(conv_w.reshape(DEPTH * 3, conv_sh), ((0, 2), (0, meta_sh - conv_sh)))], axis=0)
    slots = _exchange_small("gather_small", small, reduce=False)
    meta_full = jnp.transpose(slots[:, :N_META, :], (1, 0, 2)).reshape(N_META, D_MODEL)
    conv_full = jnp.transpose(slots[:, N_META:N_META + DEPTH * 3, :conv_sh], (1, 0, 2)).reshape(DEPTH, 3, CONV_W)
    conv_rows = [jnp.pad(conv_full[l], ((0, 5), (0, 0))) for l in range(DEPTH)]
    w_in_t, m_w_in_t, v_w_in_t = (jnp.transpose(a + zero, (0, 2, 1)) for a in (w_in, m_w_in, v_w_in))
    tok, w_in_t, m_w_in_t, v_w_in_t, meta_full = lax.optimization_barrier(
        (tok, w_in_t, m_w_in_t, v_w_in_t, meta_full))
    tok = gather_relay(0, tok)
    tok = gather_start(2, gather_start(1, tok))
    h = jnp.concatenate([jnp.zeros((PAD_FRONT, D_MODEL), F32), meta_full, x[0]], axis=0)
    h = layer_fwd(0, h, tok)
    h = layer_fwd(1, h, h)

    dh, dh_b, d_final, loss_part = _final_loss("final_loss", h, norm_final_w.reshape(1, D_MODEL), loss_target[0])

    core = lax.axis_index("c").astype(jnp.int32).reshape(1)
    chip = (2 * lax.axis_index("x") + lax.axis_index("y")).astype(jnp.int32).reshape(1)
    scatter_state = {}

    def scatter_begin(nm, grad):
        land = lax.empty((4,) + grad.shape[1:], BF16)
        arrs, sems, tok = _split_copy(f"grad_sibling_start_{nm}", [grad, land], start=(_scatter_sibling(1), 4))
        scatter_state[nm] = (arrs, sems)
        return tok

    def scatter_advance(nm, after):
        arrs, sems = scatter_state[nm]
        arrs, _, _ = _split_copy(f"grad_sibling_done_{nm}", arrs, wait=(_scatter_sibling(1), sems), after=after)
        part = _pair_sum(f"grad_pair_sum_{nm}", core, arrs[0], arrs[1])
        arrs, sems, tok = _split_copy(f"grad_chips_start_{nm}", [part, lax.empty(part.shape, BF16)],
                                      start=(_scatter_chips(1), 3))
        scatter_state[nm] = (arrs, sems)
        return tok

    def scattered(nm, after):
        arrs, sems = scatter_state[nm]
        arrs, _, _ = _split_copy(f"grad_chips_done_{nm}", arrs, wait=(_scatter_chips(1), sems), after=after)
        return arrs[0], arrs[1]

    d_mix, d_ffn, d_mls, d_bias, d_conv = ([None] * DEPTH for _ in range(5))

    def layer_bwd(l, dh, dh_b, tok):
        w, s = weights[l], saved[l]
        dg, du = _ffn_act_bwd(f"d_act_{l}", dh_b, w["wd"], s["g"], s["u"], dep=tok)
        dw_down = _mm_tn(f"dw_down_{l}", s["act"], dh_b, BF16, tm=1408, tn=1024)
        tok = scatter_begin(f"w_down_{l}", dw_down.reshape(N_DEV, FF_SH, D_MODEL))
        dhf = _mm_nn(f"d_ffn_gate_{l}", dg, w["wg_t"], F32, dep=tok)
        tok = scatter_advance(f"w_down_{l}", after=dhf)
        dhf = _mm_nn(f"d_ffn_up_{l}", du, w["wu_t"], F32, res=dhf, dep=tok)
        dw_gate = _mm_tn(f"dw_gate_{l}", dg, s["hf"], BF16, tm=1408, tn=1024)
        tok = scatter_begin(f"w_gate_{l}", dw_gate.reshape(N_DEV, FF_SH, D_MODEL))
        dw_up = _mm_tn(f"dw_up_{l}", du, s["hf"], BF16, tm=1408, tn=1024, dep=tok)
        tok = scatter_begin(f"w_up_{l}", dw_up.reshape(N_DEV, FF_SH, D_MODEL))
        dh1, dh1_b, d_ffn[l] = _rms_bwd(f"norm_ffn_bwd_{l}", s["h1"], nffn[l] + tok[0, 0], dhf, dh)
        tok = scatter_advance(f"w_gate_{l}", after=dh1)
        dcat = _mm_nt(f"d_cat_{l}", dh1_b, w["wo"], F32, tk=D_MODEL, dep=tok)
        tok = scatter_advance(f"w_up_{l}", after=dcat)
        dw_out = _mm_tn(f"dw_out_{l}", s["cat"], dh1_b, BF16, tn=1024, dep=tok)
        tok = scatter_begin(f"w_out_{l}", dw_out.reshape(N_DEV, OUT_SH, D_MODEL))
        dpm, d_mls[l], d_bias[l] = _mlstm_bwd(f"mlstm_bwd_{l}", dcat, s["pm"], s["ht"], s["cs"], s["ns"],
                                               s["ms"], bias[l] + tok[:1], nmls[l])
        dpc, d_conv[l] = _conv_bwd(f"conv_bwd_{l}", dcat, s["pc"], conv_rows[l])
        tok = scatter_advance(f"w_out_{l}", after=dpc)
        dwm_t = _mm_tn(f"dw_mlstm_{l}", dpm, s["hn"], BF16, tm=640, tn=1024, dep=tok)
        dwc_t = _mm_tn_acols(f"dw_conv_{l}", dpc, s["hn"], BF16)
        tok = scatter_begin(f"w_in_{l}", _merge_dw_in(dwm_t, dwc_t))
        dhn = _mm_nn_kt(f"d_norm_mlstm_{l}", dpm, w["win_t"], F32, tk=PM_W, dep=tok)
        dhn = _mm_nn_ksum(f"d_norm_conv_{l}", dpc, w["wc_t"], F32, res=dhn)
        tok = scatter_advance(f"w_in_{l}", after=dhn)
        dh, dh_b, d_mix[l] = _rms_bwd(f"norm_mix_bwd_{l}", s["h0"], nmix[l] + tok[0, 0], dhn, dh1)
        return dh, dh_b, tok

    dh, dh_b, tok = layer_bwd(1, dh, dh_b, None)
    dh, dh_b, tok_tail = layer_bwd(0, dh, dh_b, tok)

    pq = {}
    after = dh
    for l in reversed(range(DEPTH)):
        for nm in ("w_down", "w_gate", "w_up", "w_out", "w_in"):
            if (nm, l) != ("w_in", 0):
                pq[nm, l] = scattered(f"{nm}_{l}", after)
                after = pq[nm, l][0]
    untransposed = lambda outs: [jnp.transpose(o, (0, 2, 1)) for o in outs]
    g_out, d_out, nm_out, nv_out = _adam_sharded(
        "adam_w_out", chip, w_out, m_w_out, v_w_out, [pq["w_out", 0], pq["w_out", 1]])
    g_gate, d_gate, nm_gate, nv_gate = untransposed(_adam_sharded(
        "adam_w_gate", chip, w_gate_t, m_w_gate_t, v_w_gate_t, [pq["w_gate", 0], pq["w_gate", 1]]))
    g_up, d_up, nm_up, nv_up = untransposed(_adam_sharded(
        "adam_w_up", chip, w_up_t, m_w_up_t, v_w_up_t, [pq["w_up", 0], pq["w_up", 1]]))
    g_down, d_down, nm_down, nv_down = _adam_sharded(
        "adam_w_down", chip, w_down, m_w_down, v_w_down, [pq["w_down", 0], pq["w_down", 1]])
    pq["w_in", 0] = scattered("w_in_0", nv_down)
    g_in, d_in, nm_in, nv_in = untransposed(_adam_sharded(
        "adam_w_in", chip, w_in_t, m_w_in_t, v_w_in_t, [pq["w_in", 0], pq["w_in", 1]]))

    bg = jnp.concatenate([d_bias[l][0, :2 * HEADS] for l in range(DEPTH)])
    red_in = jnp.concatenate([
        dh[PAD_FRONT:TOK0], d_mix[0], d_mix[1], d_ffn[0], d_ffn[1], d_final,
        jnp.concatenate([d_mls[0], d_mls[1]], axis=1),
        jnp.stack([d_conv[l][:3] for l in range(DEPTH)]).reshape(3, 2 * CONV_W),
        jnp.pad(bg, (0, D_MODEL - bg.shape[0])).reshape(1, D_MODEL),
        jnp.pad(loss_part[:, :1], ((0, 0), (0, D_MODEL - 1))),
        jnp.zeros((5, D_MODEL), F32) + tok_tail[0, 0]], axis=0)
    red = _exchange_small("reduce_small", red_in, reduce=True)
    loss = red[26, 0]
    g_meta = lax.dynamic_slice_in_dim(red[:N_META], me * meta_sh, meta_sh, axis=1)
    g_mix, g_ffn, g_final = red[16:18], red[18:20], red[20]
    g_mls = red[21].reshape(DEPTH, MLSTM_W)
    g_conv = lax.dynamic_slice_in_dim(red[22:25].reshape(DEPTH, 3, CONV_W), me * conv_sh, conv_sh, axis=2)
    g_bias = red[25, :DEPTH * 2 * HEADS].reshape(DEPTH, 2 * HEADS)

    small_w = [meta_tokens, norm_mix_w, b_gates, conv_w, mlstm_norm_w, norm_ffn_w, norm_final_w]
    small_m = [m_meta_tokens, m_norm_mix_w, m_b_gates, m_conv_w, m_mlstm_norm_w, m_norm_ffn_w, m_norm_final_w]
    small_v = [v_meta_tokens, v_norm_mix_w, v_b_gates, v_conv_w, v_mlstm_norm_w, v_norm_ffn_w, v_norm_final_w]
    small_g = [g_meta, g_mix, g_bias, g_conv, g_mls, g_ffn, g_final]
    shapes = [a.shape for a in small_w]
    packed = _adam_small("adam_small", _pack128(small_w), _pack128(small_m), _pack128(small_v), _pack128(small_g))
    (d_meta, d_nmix, d_bg, d_cw, d_nmls, d_nffn, d_nfin), (nm_meta, nm_nmix, nm_bg, nm_cw, nm_nmls, nm_nffn, nm_nfin), \
        (nv_meta, nv_nmix, nv_bg, nv_cw, nv_nmls, nv_nffn, nv_nfin) = (_unpack128(p, shapes) for p in packed)

    grad_x = dh[TOK0:].reshape(1, seq, D_MODEL)
    return (loss, grad_x,
            g_meta, g_mix, g_in, g_bias, g_conv, g_mls, g_out, g_ffn, g_gate, g_up, g_down, g_final,
            d_meta, d_nmix, d_in, d_bg, d_cw, d_nmls, d_out, d_nffn, d_gate, d_up, d_down, d_nfin,
            nm_meta, nm_nmix, nm_in, nm_bg, nm_cw, nm_nmls, nm_out, nm_nffn, nm_gate, nm_up, nm_down, nm_nfin,
            nv_meta, nv_nmix, nv_in, nv_bg, nv_cw, nv_nmls, nv_out, nv_nffn, nv_gate, nv_up, nv_down, nv_nfin)
```

```python
import functools

import numpy as np
import jax
import jax.numpy as jnp
from jax import lax
from jax.experimental import pallas as pl
from jax.experimental.pallas import tpu as pltpu

F32 = jnp.float32
BF16 = jnp.bfloat16
MESH = pl.DeviceIdType.MESH

D_MODEL = 2048
DEPTH = 2
N_META = 16
MLSTM_W = 1024
CONV_W = 1024
HEADS = 4
DV = 256
DQK = 128
QK_W = 512
CHUNK = 64
PAD_FRONT = 48
TOK0 = PAD_FRONT + N_META
D_FF = 5632
N_DEV = 8
FF_SH = D_FF // N_DEV
D_IN = 6152
IN_SH = D_IN // N_DEV
OUT_SH = D_MODEL // N_DEV
GATE_COL = 3072
PM_W = GATE_COL + 128
GATE_CAP = 15.0
EPS = 1e-6
QSCALE = DQK ** -0.5

ADAM_LR = 0.001
ADAM_B1 = 0.9
ADAM_B2 = 0.999
ADAM_EPS = 1e-08
ADAM_WD = 0.01
ADAM_STEP = 10

V7X_VMEM_LIMIT = 50 * 1024 * 1024
V7X_MXU_COLS = 256


def _params(**kw):
    return pltpu.CompilerParams(vmem_limit_bytes=V7X_VMEM_LIMIT, **kw)


def _tile(n, target, mult):
    best = None
    for t in range(mult, min(n, target) + 1, mult):
        if n % t == 0:
            best = t
    return best if best is not None else n


def _sigmoid(x):
    return 1.0 / (1.0 + jnp.exp(-x))


NN = ((1,), (0,))
NT = ((1,), (1,))
TN = ((0,), (0,))


def _matmul(name, a, b, out_shape, out_dtype, grid, a_bs, b_bs, o_bs, dims, nk, acc_shape=None,
            res=None, res_bs=None, dep=None):
    has_res = res is not None
    n_in = 2 + has_res + (dep is not None)

    def body(*refs):
        a_ref, b_ref = refs[0], refs[1]
        r_ref = refs[2] if has_res else None
        o_ref = refs[n_in]
        x = lax.dot_general(a_ref[...], b_ref[...], (dims, ((), ())), preferred_element_type=F32)
        if nk == 1:
            if has_res:
                x = x + r_ref[...]
            o_ref[...] = x.astype(o_ref.dtype)
            return
        acc = refs[n_in + 1]
        k = pl.program_id(len(grid) - 1)

        @pl.when(k == 0)
        def _():
            acc[...] = (x + r_ref[...]) if has_res else x

        @pl.when(k > 0)
        def _():
            acc[...] += x

        @pl.when(k == nk - 1)
        def _():
            o_ref[...] = acc[...].astype(o_ref.dtype)

    ins = [a, b] + ([res] if has_res else [])
    specs = [a_bs, b_bs] + ([res_bs] if has_res else [])
    if dep is not None:
        ins.append(dep)
        specs.append(pl.BlockSpec((8, 128), lambda *_: (0, 0)))
    scratch = [pltpu.VMEM(acc_shape, F32)] if nk > 1 else []
    return pl.pallas_call(
        body, name=name, grid=grid, in_specs=specs, out_specs=o_bs,
        out_shape=jax.ShapeDtypeStruct(out_shape, out_dtype), scratch_shapes=scratch,
        compiler_params=_params(),
    )(*ins)


def _mm_nn(name, a, b, out_dtype, res=None, tm=1056, tn=512, dep=None):
    r, k = a.shape
    n = b.shape[1]
    tm, tn = _tile(r, tm, 8), _tile(n, tn, 128)
    return _matmul(name, a, b, (r, n), out_dtype, (r // tm, n // tn, 1),
                   pl.BlockSpec((tm, k), lambda i, j, s: (i, 0)),
                   pl.BlockSpec((k, tn), lambda i, j, s: (0, j)),
                   pl.BlockSpec((tm, tn), lambda i, j, s: (i, j)), NN, 1,
                   res=res, res_bs=pl.BlockSpec((tm, tn), lambda i, j, s: (i, j)), dep=dep)


def _mm_nn_kt(name, a, b, out_dtype, tm=1056, tn=1024, tk=640, dep=None):
    r, k = a.shape
    n = b.shape[1]
    tm, tn, tk = _tile(r, tm, 8), _tile(n, tn, 128), _tile(k, tk, 128)
    nk = k // tk
    return _matmul(name, a, b, (r, n), out_dtype, (r // tm, n // tn, nk),
                   pl.BlockSpec((tm, tk), lambda i, j, s: (i, s)),
                   pl.BlockSpec((tk, tn), lambda i, j, s: (s, j)),
                   pl.BlockSpec((tm, tn), lambda i, j, s: (i, j)), NN, nk, acc_shape=(tm, tn), dep=dep)


def _mm_nn_ksum(name, a3, b3, out_dtype, res=None, tm=1056, tn=1024, dep=None):
    e, r, kb = a3.shape
    n = b3.shape[2]
    tm, tn = _tile(r, tm, 8), _tile(n, tn, 128)
    return _matmul(name, a3, b3, (r, n), out_dtype, (r // tm, n // tn, e),
                   pl.BlockSpec((None, tm, kb), lambda i, j, s: (s, i, 0)),
                   pl.BlockSpec((None, kb, tn), lambda i, j, s: (s, 0, j)),
                   pl.BlockSpec((tm, tn), lambda i, j, s: (i, j)), NN, e, acc_shape=(tm, tn),
                   res=res, res_bs=pl.BlockSpec((tm, tn), lambda i, j, s: (i, j)), dep=dep)


def _mm_nt(name, a, b, out_dtype, res=None, tm=1056, tn=512, tk=640, n=None, dep=None):
    r, k = a.shape
    n = b.shape[0] if n is None else n
    tm, tn, tk = _tile(r, tm, 8), _tile(n, tn, 128), _tile(k, tk, 128)
    nk = k // tk
    return _matmul(name, a, b, (r, n), out_dtype, (r // tm, n // tn, nk),
                   pl.BlockSpec((tm, tk), lambda i, j, s: (i, s)),
                   pl.BlockSpec((tn, tk), lambda i, j, s: (j, s)),
                   pl.BlockSpec((tm, tn), lambda i, j, s: (i, j)), NT, nk, acc_shape=(tm, tn),
                   res=res, res_bs=pl.BlockSpec((tm, tn), lambda i, j, s: (i, j)), dep=dep)


def _mm_nt_bcols(name, a, b3, out_dtype, tm=1056, dep=None):
    r, k = a.shape
    e, n, _ = b3.shape
    tm = _tile(r, tm, 8)
    return _matmul(name, a, b3, (e, r, n), out_dtype, (r // tm, e, 1),
                   pl.BlockSpec((tm, k), lambda i, g, s: (i, 0)),
                   pl.BlockSpec((None, n, k), lambda i, g, s: (g, 0, 0)),
                   pl.BlockSpec((None, tm, n), lambda i, g, s: (g, i, 0)), NT, 1, dep=dep)


def _mm_tn(name, a, b, out_dtype, tm=1024, tn=640, dep=None):
    r, m = a.shape
    n = b.shape[1]
    tm, tn = _tile(m, tm, 128), _tile(n, tn, 128)
    return _matmul(name, a, b, (m, n), out_dtype, (m // tm, n // tn, 1),
                   pl.BlockSpec((r, tm), lambda i, j, s: (0, i)),
                   pl.BlockSpec((r, tn), lambda i, j, s: (0, j)),
                   pl.BlockSpec((tm, tn), lambda i, j, s: (i, j)), TN, 1, dep=dep)


def _mm_tn_acols(name, a3, b, out_dtype, tn=1024, dep=None):
    e, r, m = a3.shape
    n = b.shape[1]
    tn = _tile(n, tn, 128)
    return _matmul(name, a3, b, (e, m, n), out_dtype, (n // tn, e, 1),
                   pl.BlockSpec((None, r, m), lambda j, g, s: (g, 0, 0)),
                   pl.BlockSpec((r, tn), lambda j, g, s: (0, j)),
                   pl.BlockSpec((None, m, tn), lambda j, g, s: (g, 0, j)), TN, 1, dep=dep)


def _rms_fwd(name, h, w):
    r, d = h.shape
    tr = _tile(r, 264, 8)

    def body(h_ref, w_ref, o_ref):
        x = h_ref[...]
        rs = lax.rsqrt(jnp.mean(x * x, axis=1, keepdims=True) + EPS)
        o_ref[...] = (x * rs * w_ref[...]).astype(BF16)

    return pl.pallas_call(
        body, name=name, grid=(r // tr,),
        in_specs=[pl.BlockSpec((tr, d), lambda i: (i, 0)), pl.BlockSpec((1, d), lambda i: (0, 0))],
        out_specs=pl.BlockSpec((tr, d), lambda i: (i, 0)),
        out_shape=jax.ShapeDtypeStruct((r, d), BF16), compiler_params=_params(),
    )(h, w)


def _rms_bwd(name, x, w, dy, dres):
    r, d = x.shape
    tr = _tile(r, 264, 8)

    def body(x_ref, w_ref, dy_ref, dr_ref, dx_ref, dxb_ref, dw_ref):
        xv = x_ref[...]
        g = dy_ref[...]
        rs = lax.rsqrt(jnp.mean(xv * xv, axis=1, keepdims=True) + EPS)
        wg = g * w_ref[...]
        dx = rs * wg - xv * (rs * rs * rs) * jnp.mean(xv * wg, axis=1, keepdims=True) + dr_ref[...]
        dx_ref[...] = dx
        dxb_ref[...] = dx.astype(BF16)
        part = jnp.sum(g * xv * rs, axis=0, keepdims=True)

        @pl.when(pl.program_id(0) == 0)
        def _():
            dw_ref[...] = part

        @pl.when(pl.program_id(0) > 0)
        def _():
            dw_ref[...] += part

    row = pl.BlockSpec((tr, d), lambda i: (i, 0))
    vec = pl.BlockSpec((1, d), lambda i: (0, 0))
    return pl.pallas_call(
        body, name=name, grid=(r // tr,), in_specs=[row, vec, row, row], out_specs=[row, row, vec],
        out_shape=[jax.ShapeDtypeStruct((r, d), F32), jax.ShapeDtypeStruct((r, d), BF16),
                   jax.ShapeDtypeStruct((1, d), F32)],
        compiler_params=_params(),
    )(x, w, dy, dres)


def _final_loss(name, h, w, target):
    r, d = h.shape
    nb = r // CHUNK

    def body(h_ref, w_ref, t_ref, dh_ref, dhb_ref, dw_ref, ls_ref):
        i = pl.program_id(0)

        @pl.when(i == 0)
        def _():
            dh_ref[...] = jnp.zeros_like(dh_ref)
            dhb_ref[...] = jnp.zeros_like(dhb_ref)
            dw_ref[...] = jnp.zeros_like(dw_ref)
            ls_ref[...] = jnp.zeros_like(ls_ref)

        @pl.when(i > 0)
        def _():
            xv = h_ref[...]
            wv = w_ref[...]
            rs = lax.rsqrt(jnp.mean(xv * xv, axis=1, keepdims=True) + EPS)
            err = xv * rs * wv - t_ref[...]
            sq = jnp.sum(jnp.sum(err * err, axis=1, keepdims=True), axis=0, keepdims=True)
            ls_ref[...] += jnp.broadcast_to(sq * (0.5 / d), ls_ref.shape)
            g = err * (1.0 / d)
            wg = g * wv
            dx = rs * wg - xv * (rs * rs * rs) * jnp.mean(xv * wg, axis=1, keepdims=True)
            dh_ref[...] = dx
            dhb_ref[...] = dx.astype(BF16)
            dw_ref[...] += jnp.sum(g * xv * rs, axis=0, keepdims=True)

    row = pl.BlockSpec((CHUNK, d), lambda i: (i, 0))
    vec = pl.BlockSpec((1, d), lambda i: (0, 0))
    return pl.pallas_call(
        body, name=name, grid=(nb,),
        in_specs=[row, vec, pl.BlockSpec((CHUNK, d), lambda i: (jnp.maximum(i - 1, 0), 0))],
        out_specs=[row, row, vec, pl.BlockSpec((1, 128), lambda i: (0, 0))],
        out_shape=[jax.ShapeDtypeStruct((r, d), F32), jax.ShapeDtypeStruct((r, d), BF16),
                   jax.ShapeDtypeStruct((1, d), F32), jax.ShapeDtypeStruct((1, 128), F32)],
        compiler_params=_params(),
    )(h, w, target)


def _ffn_in(name, hf, wg_t, wu_t, dep=None, tm=1056, tn=512):
    r, d = hf.shape
    f = wg_t.shape[0]
    tm, tn = _tile(r, tm, 8), _tile(f, tn, 128)

    def body(h_ref, wg_ref, wu_ref, *rest):
        g_ref, u_ref, a_ref = rest[-3:]
        x = h_ref[...]
        g = lax.dot_general(x, wg_ref[...], (NT, ((), ())), preferred_element_type=F32)
        u = lax.dot_general(x, wu_ref[...], (NT, ((), ())), preferred_element_type=F32)
        g_ref[...] = g.astype(BF16)
        u_ref[...] = u.astype(BF16)
        a_ref[...] = (g * _sigmoid(g) * u).astype(BF16)

    wspec = pl.BlockSpec((tn, d), lambda i, j: (j, 0))
    ospec = pl.BlockSpec((tm, tn), lambda i, j: (i, j))
    ins, specs = [hf, wg_t, wu_t], [pl.BlockSpec((tm, d), lambda i, j: (i, 0)), wspec, wspec]
    if dep is not None:
        ins.append(dep)
        specs.append(pl.BlockSpec((8, 128), lambda *_: (0, 0)))
    return pl.pallas_call(
        body, name=name, grid=(r // tm, f // tn), in_specs=specs, out_specs=[ospec] * 3,
        out_shape=[jax.ShapeDtypeStruct((r, f), BF16)] * 3, compiler_params=_params(),
    )(*ins)


def _ffn_act_bwd(name, dh, wd, g, u, dep=None, tm=1056, tn=512):
    r, d = dh.shape
    f = wd.shape[0]
    tm, tn = _tile(r, tm, 8), _tile(f, tn, 128)

    def body(dh_ref, wd_ref, g_ref, u_ref, *rest):
        dg_ref, du_ref = rest[-2:]
        dhv = dh_ref[...]
        for c0 in range(0, tn, V7X_MXU_COLS):
            cols = slice(c0, c0 + V7X_MXU_COLS)
            da = lax.dot_general(dhv, wd_ref[cols, :], (NT, ((), ())), preferred_element_type=F32)
            gv = g_ref[:, cols].astype(F32)
            s = _sigmoid(gv)
            t = da * s
            du_ref[:, cols] = (t * gv).astype(BF16)
            dg_ref[:, cols] = (t * u_ref[:, cols].astype(F32) * (1.0 + gv - gv * s)).astype(BF16)

    tile = pl.BlockSpec((tm, tn), lambda i, j: (i, j))
    ins = [dh, wd, g, u]
    specs = [pl.BlockSpec((tm, d), lambda i, j: (i, 0)), pl.BlockSpec((tn, d), lambda i, j: (j, 0)), tile, tile]
    if dep is not None:
        ins.append(dep)
        specs.append(pl.BlockSpec((8, 128), lambda *_: (0, 0)))
    return pl.pallas_call(
        body, name=name, grid=(r // tm, f // tn), in_specs=specs, out_specs=[tile] * 2,
        out_shape=[jax.ShapeDtypeStruct((r, f), BF16)] * 2, compiler_params=_params(),
    )(*ins)


def _shift_down(a, k):
    row = lax.broadcasted_iota(jnp.int32, a.shape, 0)
    return jnp.where(row >= k, pltpu.roll(a, k, 0), 0.0)


def _shift_up(a, k):
    n = a.shape[0]
    row = lax.broadcasted_iota(jnp.int32, a.shape, 0)
    return jnp.where(row < n - k, pltpu.roll(a, n - k, 0), 0.0)


def _conv_fwd(name, pc, cw):
    _, r, w = pc.shape

    def body(pc_ref, cw_ref, o_ref):
        a = pc_ref[2] * pc_ref[0]
        cwv = cw_ref[...]
        conv = _shift_down(a, 2) * cwv[0:1] + _shift_down(a, 1) * cwv[1:2] + a * cwv[2:3]
        o_ref[...] = (pc_ref[1] * conv).astype(BF16)

    return pl.pallas_call(
        body, name=name, grid=(w // 128,),
        in_specs=[pl.BlockSpec((3, r, 128), lambda j: (0, 0, j)), pl.BlockSpec((8, 128), lambda j: (0, j))],
        out_specs=pl.BlockSpec((r, 128), lambda j: (0, j)),
        out_shape=jax.ShapeDtypeStruct((r, w), BF16), compiler_params=_params(),
    )(pc, cw)


def _conv_bwd(name, dcat, pc, cw):
    _, r, w = pc.shape
    nblk = w // 128

    def body(dy_ref, pc_ref, cw_ref, dpc_ref, dcw_ref):
        u, gb, gc = pc_ref[0], pc_ref[1], pc_ref[2]
        cwv = cw_ref[...]
        dy = dy_ref[...]
        a = gc * u
        a1, a2 = _shift_down(a, 1), _shift_down(a, 2)
        conv = a2 * cwv[0:1] + a1 * cwv[1:2] + a * cwv[2:3]
        dconv = dy * gb
        da = dconv * cwv[2:3] + _shift_up(dconv, 1) * cwv[1:2] + _shift_up(dconv, 2) * cwv[0:1]
        dpc_ref[0] = (da * gc).astype(BF16)
        dpc_ref[1] = (dy * conv).astype(BF16)
        dpc_ref[2] = (da * u).astype(BF16)
        row = lax.broadcasted_iota(jnp.int32, (8, 128), 0)
        dw0 = jnp.sum(dconv * a2, axis=0, keepdims=True)
        dw1 = jnp.sum(dconv * a1, axis=0, keepdims=True)
        dw2 = jnp.sum(dconv * a, axis=0, keepdims=True)
        dcw_ref[...] = jnp.where(row == 0, dw0, jnp.where(row == 1, dw1, jnp.where(row == 2, dw2, 0.0)))

    return pl.pallas_call(
        body, name=name, grid=(nblk,),
        in_specs=[pl.BlockSpec((r, 128), lambda j: (0, nblk + j)),
                  pl.BlockSpec((3, r, 128), lambda j: (0, 0, j)), pl.BlockSpec((8, 128), lambda j: (0, j))],
        out_specs=[pl.BlockSpec((3, r, 128), lambda j: (0, 0, j)), pl.BlockSpec((8, 128), lambda j: (0, j))],
        out_shape=[jax.ShapeDtypeStruct((3, r, w), BF16), jax.ShapeDtypeStruct((8, w), F32)],
        compiler_params=_params(),
    )(dcat, pc, cw)


def _dot(a, b, dims):
    return lax.dot_general(a, b, (dims, ((), ())), preferred_element_type=F32)


def _col_to_row(xc, eye):
    return jnp.sum(jnp.where(eye, xc, 0.0), axis=0, keepdims=True)


def _row_to_col(xr, eye):
    return jnp.sum(jnp.where(eye, xr, 0.0), axis=1, keepdims=True)


def _gate_tiles(graw, bias, row0):
    th = jnp.tanh((graw + bias) / GATE_CAP)
    z = GATE_CAP * th
    row = lax.broadcasted_iota(jnp.int32, graw.shape, 0) + row0
    real = row >= PAD_FRONT
    li = jnp.where(real, z, -jnp.inf)
    lf = jnp.where(real, jnp.minimum(z, 0.0) - jnp.log(1.0 + jnp.exp(-jnp.abs(z))), 0.0)
    return th, z, li, lf, real


def _interleave(gens):
    results = [None] * len(gens)
    live = list(enumerate(gens))
    while live:
        still = []
        for i, gen in live:
            try:
                next(gen)
                still.append((i, gen))
            except StopIteration as stop:
                results[i] = stop.value
        live = still
    return results


def _chunk_common(pm, h, li, lf, cst, nst, mst, tril, eye):
    kraw = pm[:, QK_W + h * DQK:QK_W + (h + 1) * DQK]
    q = (pm[:, h * DQK:(h + 1) * DQK] * QSCALE).astype(BF16)
    yield
    k = kraw.astype(BF16)
    v = pm[:, 2 * QK_W + h * DV:2 * QK_W + (h + 1) * DV].astype(BF16)
    yield
    li_c = li[:, h:h + 1]
    lf_c = lf[:, HEADS + h:HEADS + h + 1]
    li_r = _col_to_row(li_c, eye)
    yield
    lf_r = _col_to_row(lf_c, eye)
    yield
    b_c = jnp.sum(jnp.where(tril, lf_r, 0.0), axis=1, keepdims=True)
    yield
    b_r = _col_to_row(b_c, eye)
    yield
    dmat = jnp.where(tril, b_c - b_r + li_r, -jnp.inf)
    inter = b_c + mst
    yield
    mt = jnp.maximum(inter, jnp.max(dmat, axis=1, keepdims=True))
    yield
    w_inter = jnp.exp(inter - mt)
    p = jnp.exp(dmat - mt)
    yield
    s = _dot(q, k, NT) * p
    yield
    cb = cst.astype(BF16)
    nb = nst.astype(BF16).astype(F32)
    qc = _dot(q, cb, NN)
    yield
    qn = jnp.sum(q.astype(F32) * nb, axis=1, keepdims=True)
    yield
    den = w_inter * qn + jnp.sum(s, axis=1, keepdims=True)
    yield
    dn = jnp.maximum(jnp.abs(den), jnp.exp(-mt))
    b_end = b_c[CHUNK - 1:CHUNK, :]
    decay = b_end - b_c + li_c
    yield
    m_new = jnp.maximum(b_end + mst, jnp.max(decay, axis=0, keepdims=True))
    yield
    w_old = jnp.exp(b_end + mst - m_new)
    w_in = jnp.exp(decay - m_new)
    kw = (w_in * kraw).astype(BF16)
    yield
    return dict(q=q, k=k, v=v, kraw=kraw, mt=mt, w_inter=w_inter, p=p, s=s, cb=cb, nb=nb, qc=qc, qn=qn,
                den=den, dn=dn, m_new=m_new, w_old=w_old, w_in=w_in, kw=kw)


def _chunks_per_step(nc):
    return 1


def _mlstm_fwd(name, pm, bias, nw):
    r = pm.shape[0]
    nc = r // CHUNK
    grp = _chunks_per_step(nc)

    def body(pm_ref, b_ref, nw_ref, hm_ref, ht_ref, cs_ref, ns_ref, ms_ref, c_scr, n_scr, m_scr):
        step = pl.program_id(0)

        @pl.when(step == 0)
        def _():
            c_scr[...] = jnp.zeros_like(c_scr)
            n_scr[...] = jnp.zeros_like(n_scr)
            m_scr[...] = jnp.zeros_like(m_scr)

        rr = lax.broadcasted_iota(jnp.int32, (CHUNK, CHUNK), 0)
        cc = lax.broadcasted_iota(jnp.int32, (CHUNK, CHUNK), 1)
        tril, eye = cc <= rr, cc == rr
        bv, nwv = b_ref[...], nw_ref[...]
        states = [(c_scr[h], n_scr[h], m_scr[h]) for h in range(HEADS)]
        for g in range(grp):
            rows = slice(g * CHUNK, (g + 1) * CHUNK)
            pmv = pm_ref[rows, :]
            _, _, li, lf, _ = _gate_tiles(pmv[:, GATE_COL:GATE_COL + 128], bv, (step * grp + g) * CHUNK)
            def head(h, cst, nst, mst, g=g, rows=rows, pmv=pmv, li=li, lf=lf):
                f = yield from _chunk_common(pmv, h, li, lf, cst, nst, mst, tril, eye)
                num = f["w_inter"] * f["qc"] + _dot(f["s"].astype(BF16), f["v"], NN)
                yield
                hh = num / f["dn"]
                yield
                c_new = f["w_old"] * cst + _dot(f["kw"], f["v"], TN)
                yield
                n_new = f["w_old"] * nst + jnp.sum(
                    f["w_in"].astype(BF16).astype(F32) * f["k"].astype(F32), axis=0, keepdims=True)
                yield
                sl = slice(h * DV, (h + 1) * DV)
                rs = lax.rsqrt(jnp.mean(hh * hh, axis=1, keepdims=True) + EPS)
                yield
                og = pmv[:, 2 * QK_W + MLSTM_W + h * DV:2 * QK_W + MLSTM_W + (h + 1) * DV]
                cs_ref[g, h] = cst
                ns_ref[g, h] = nst
                ms_ref[g, h] = mst
                ht_ref[rows, sl] = hh
                yield
                hm_ref[rows, sl] = (_sigmoid(og) * (hh * rs * nwv[:, sl])).astype(BF16)
                return c_new, n_new, f["m_new"]

            states = _interleave([head(h, *states[h]) for h in range(HEADS)])
        for h, (cst, nst, mst) in enumerate(states):
            c_scr[h] = cst
            n_scr[h] = nst
            m_scr[h] = mst

    return pl.pallas_call(
        body, name=name, grid=(nc // grp,),
        in_specs=[pl.BlockSpec((grp * CHUNK, PM_W), lambda i: (i, 0)), pl.BlockSpec((1, 128), lambda i: (0, 0)),
                  pl.BlockSpec((1, MLSTM_W), lambda i: (0, 0))],
        out_specs=[pl.BlockSpec((grp * CHUNK, MLSTM_W), lambda i: (i, 0)),
                   pl.BlockSpec((grp * CHUNK, MLSTM_W), lambda i: (i, 0)),
                   pl.BlockSpec((grp, HEADS, DQK, DV), lambda i: (i, 0, 0, 0)),
                   pl.BlockSpec((grp, HEADS, 1, DQK), lambda i: (i, 0, 0, 0)),
                   pl.BlockSpec((grp, HEADS, 1, 1), lambda i: (i, 0, 0, 0))],
        out_shape=[jax.ShapeDtypeStruct((r, MLSTM_W), BF16), jax.ShapeDtypeStruct((r, MLSTM_W), F32),
                   jax.ShapeDtypeStruct((nc, HEADS, DQK, DV), F32),
                   jax.ShapeDtypeStruct((nc, HEADS, 1, DQK), F32),
                   jax.ShapeDtypeStruct((nc, HEADS, 1, 1), F32)],
        scratch_shapes=[pltpu.VMEM((HEADS, DQK, DV), F32), pltpu.VMEM((HEADS, 1, DQK), F32),
                        pltpu.VMEM((HEADS, 1, 1), F32)],
        compiler_params=_params(),
    )(pm, bias, nw)


def _mlstm_bwd(name, dcat, pm, ht, cs, ns, ms, bias, nw):
    r = pm.shape[0]
    nc = r // CHUNK
    grp = _chunks_per_step(nc)
    nsteps = nc // grp

    def body(dy_ref, pm_ref, ht_ref, cs_ref, ns_ref, ms_ref, b_ref, nw_ref, dpm_ref, dnw_ref, db_ref,
             dc_scr, dn_scr):
        step = pl.program_id(0)

        @pl.when(step == 0)
        def _():
            dc_scr[...] = jnp.zeros_like(dc_scr)
            dn_scr[...] = jnp.zeros_like(dn_scr)
            dnw_ref[...] = jnp.zeros_like(dnw_ref)
            db_ref[...] = jnp.zeros_like(db_ref)

        rr = lax.broadcasted_iota(jnp.int32, (CHUNK, CHUNK), 0)
        cc = lax.broadcasted_iota(jnp.int32, (CHUNK, CHUNK), 1)
        tril, eye, triu = cc <= rr, cc == rr, cc >= rr
        lane = lax.broadcasted_iota(jnp.int32, (CHUNK, 128), 1)
        rowid = lax.broadcasted_iota(jnp.int32, (CHUNK, 1), 0)
        bv, nwv = b_ref[...], nw_ref[...]
        carried = [(dc_scr[h], dn_scr[h]) for h in range(HEADS)]
        dnw_acc = [jnp.zeros((1, DV), F32) for _ in range(HEADS)]
        db_acc = jnp.zeros((1, 128), F32)
        for g in reversed(range(grp)):
            rows = slice(g * CHUNK, (g + 1) * CHUNK)
            ci = (nsteps - 1 - step) * grp + g
            pmv = pm_ref[rows, :]
            th, z, li, lf, real = _gate_tiles(pmv[:, GATE_COL:GATE_COL + 128], bv, ci * CHUNK)
            heads = _interleave([
                _mlstm_bwd_head(h, pmv, ht_ref[rows, h * DV:(h + 1) * DV], dy_ref[rows, h * DV:(h + 1) * DV], nwv,
                                li, lf, cs_ref[g, h], ns_ref[g, h], ms_ref[g, h], carried[h][0], carried[h][1],
                                tril, eye, triu, lane, rowid, dpm_ref, rows)
                for h in range(HEADS)])
            carried = [(dc_new, dn_new) for _, dc_new, dn_new, _ in heads]
            dgt = heads[0][0] + heads[1][0] + heads[2][0] + heads[3][0]
            dnw_acc = [dnw_acc[h] + heads[h][3] for h in range(HEADS)]
            dact = jnp.where(lane < HEADS, 1.0, 1.0 - _sigmoid(z)) * (1.0 - th * th)
            dgraw = jnp.where(real & (lane < 2 * HEADS), dgt * dact, 0.0)
            dpm_ref[rows, GATE_COL:GATE_COL + 128] = dgraw.astype(BF16)
            db_acc = db_acc + jnp.sum(dgraw, axis=0, keepdims=True)
        for h, (dcn, dnn) in enumerate(carried):
            dc_scr[h] = dcn
            dn_scr[h] = dnn
            dnw_ref[:, h * DV:(h + 1) * DV] += dnw_acc[h]
        db_ref[...] += db_acc

    rev = lambda i: (nsteps - 1 - i, 0)
    rev4 = lambda i: (nsteps - 1 - i, 0, 0, 0)
    return pl.pallas_call(
        body, name=name, grid=(nsteps,),
        in_specs=[pl.BlockSpec((grp * CHUNK, MLSTM_W), rev), pl.BlockSpec((grp * CHUNK, PM_W), rev),
                  pl.BlockSpec((grp * CHUNK, MLSTM_W), rev),
                  pl.BlockSpec((grp, HEADS, DQK, DV), rev4), pl.BlockSpec((grp, HEADS, 1, DQK), rev4),
                  pl.BlockSpec((grp, HEADS, 1, 1), rev4),
                  pl.BlockSpec((1, 128), lambda i: (0, 0)), pl.BlockSpec((1, MLSTM_W), lambda i: (0, 0))],
        out_specs=[pl.BlockSpec((grp * CHUNK, PM_W), rev), pl.BlockSpec((1, MLSTM_W), lambda i: (0, 0)),
                   pl.BlockSpec((1, 128), lambda i: (0, 0))],
        out_shape=[jax.ShapeDtypeStruct((r, PM_W), BF16), jax.ShapeDtypeStruct((1, MLSTM_W), F32),
                   jax.ShapeDtypeStruct((1, 128), F32)],
        scratch_shapes=[pltpu.VMEM((HEADS, DQK, DV), F32), pltpu.VMEM((HEADS, 1, DQK), F32)],
        compiler_params=_params(),
    )(dcat, pm, ht, cs, ns, ms, bias, nw)


def _mlstm_bwd_head(h, pmv, hh, y, nwv, li, lf, cst, nst, mst, dcn, dnn, tril, eye, triu, lane, rowid,
                    dpm_ref, rows):
    f = yield from _chunk_common(pmv, h, li, lf, cst, nst, mst, tril, eye)
    q, k, v, s, p = f["q"], f["k"], f["v"], f["s"], f["p"]
    w_inter, w_in, w_old, dn = f["w_inter"], f["w_in"], f["w_old"], f["dn"]
    osl = slice(2 * QK_W + MLSTM_W + h * DV, 2 * QK_W + MLSTM_W + (h + 1) * DV)
    sg = _sigmoid(pmv[:, osl])
    yield
    rs = lax.rsqrt(jnp.mean(hh * hh, axis=1, keepdims=True) + EPS)
    yield
    nwh = nwv[:, h * DV:(h + 1) * DV]
    dpm_ref[rows, osl] = (y * (hh * rs * nwh) * sg * (1.0 - sg)).astype(BF16)
    yield
    dhn = y * sg
    dnw_h = jnp.sum(dhn * hh * rs, axis=0, keepdims=True)
    yield
    wd = dhn * nwh
    dhh = rs * wd - hh * (rs * rs * rs) * jnp.mean(hh * wd, axis=1, keepdims=True)
    yield
    dnum = dhh / dn
    dd = -jnp.sum(dhh * hh, axis=1, keepdims=True) / dn
    yield
    dden = jnp.where(jnp.abs(f["den"]) > jnp.exp(-f["mt"]), dd * jnp.sign(f["den"]), 0.0)
    dnum_b = dnum.astype(BF16)
    wdn = (w_inter * dnum).astype(BF16)
    wid = (w_inter * dden).astype(BF16).astype(F32)
    yield
    ds = _dot(dnum_b, v, NT) + dden
    yield
    dsp = (ds * p).astype(BF16)
    yield
    dq = _dot(dsp, k, NN) + _dot(wdn, f["cb"], NT) + wid * f["nb"]
    yield
    dk = _dot(dsp, q, TN)
    yield
    dv = _dot(s.astype(BF16), dnum_b, TN)
    yield
    g = ds * s
    g_col = _row_to_col(jnp.sum(g, axis=0, keepdims=True), eye)
    yield
    db = jnp.sum(g, axis=1, keepdims=True) - g_col
    dli = g_col
    yield
    db = db + (jnp.sum(dnum * f["qc"], axis=1, keepdims=True) + dden * f["qn"]) * w_inter
    yield
    dcnb = dcn.astype(BF16)
    dnnb = dnn.astype(BF16).astype(F32)
    dkw = _dot(v, dcnb, NT) + dnnb
    yield
    dk = dk + w_in * dkw
    dv = dv + _dot(f["kw"], dcnb, NN)
    yield
    ddecay = jnp.sum(dkw * f["kraw"], axis=1, keepdims=True) * w_in
    yield
    dw_old = (jnp.sum(jnp.sum(dcn * cst, axis=1, keepdims=True), axis=0, keepdims=True)
              + jnp.sum(dnn * nst, axis=1, keepdims=True))
    yield
    db_end = dw_old * w_old + jnp.sum(ddecay, axis=0, keepdims=True)
    db = db - ddecay + jnp.where(rowid == CHUNK - 1, db_end, 0.0)
    dli = dli + ddecay
    yield
    dc_new = w_old * dcn + _dot(q, wdn, TN)
    yield
    dn_new = w_old * dnn + jnp.sum(wid * q.astype(F32), axis=0, keepdims=True)
    yield
    dlf = jnp.sum(jnp.where(triu, _col_to_row(db, eye), 0.0), axis=1, keepdims=True)
    yield
    gate_part = jnp.where(lane == h, dli, 0.0) + jnp.where(lane == HEADS + h, dlf, 0.0)
    dpm_ref[rows, h * DQK:(h + 1) * DQK] = (dq * QSCALE).astype(BF16)
    yield
    dpm_ref[rows, QK_W + h * DQK:QK_W + (h + 1) * DQK] = dk.astype(BF16)
    yield
    dpm_ref[rows, 2 * QK_W + h * DV:2 * QK_W + (h + 1) * DV] = dv.astype(BF16)
    return gate_part, dc_new, dn_new, dnw_h


def _my_place():
    return lax.axis_index("x"), lax.axis_index("y"), lax.axis_index("c")


def _flip(v, bit):
    return 1 - v if bit else v


def _exchange_small(name, blk, reduce):
    r, c = blk.shape

    def body(x_ref, o_ref, *rest):
        slots = rest[0] if reduce else o_ref
        send_sems, recv_sems = rest[-2], rest[-1]
        x, y, cc = _my_place()
        me = 4 * x + 2 * y + cc
        slots[me] = x_ref[...]
        copies = []
        for k in range(1, N_DEV):
            peer = (_flip(x, k & 4), _flip(y, k & 2), _flip(cc, k & 1))
            cp = pltpu.make_async_remote_copy(
                src_ref=x_ref, dst_ref=slots.at[me], send_sem=send_sems.at[k - 1],
                recv_sem=recv_sems.at[k - 1], device_id=peer, device_id_type=MESH)
            cp.start()
            copies.append(cp)
        for cp in copies:
            cp.wait()
        if reduce:
            acc = slots[0]
            for d in range(1, N_DEV):
                acc = acc + slots[d]
            o_ref[...] = acc

    scratch = ([pltpu.VMEM((N_DEV, r, c), F32)] if reduce else []) + [
        pltpu.SemaphoreType.DMA((N_DEV - 1,)), pltpu.SemaphoreType.DMA((N_DEV - 1,))]
    return pl.pallas_call(
        body, name=name,
        out_shape=jax.ShapeDtypeStruct((r, c) if reduce else (N_DEV, r, c), F32),
        in_specs=[pl.BlockSpec(memory_space=pltpu.VMEM)], out_specs=pl.BlockSpec(memory_space=pltpu.VMEM),
        scratch_shapes=scratch, compiler_params=_params(),
    )(blk)


HBM_SPEC = pl.BlockSpec(memory_space=pltpu.HBM)
SEM_SPEC = pl.BlockSpec(memory_space=pltpu.SEMAPHORE)
ANY_SPEC = pl.BlockSpec(memory_space=pl.ANY)
DATAFLOW = pltpu.SideEffectType.DATAFLOW_SIDE_EFFECTING


def _split_copy(name, arrays, start=None, wait=None, after=None):
    n = len(arrays)
    n_wait = 2 if wait else 0
    n_after = 0 if after is None else 1
    n_new = 2 if start else 0

    def body(*refs):
        ins = refs[:n]
        if wait:
            for cp in wait[0](ins, refs[n], refs[n + 1]):
                cp.wait_send()
                cp.wait_recv()
        if start:
            at = n + n_wait + n_after
            for cp in start[0](ins, refs[at], refs[at + 1]):
                cp.start()
            token = refs[at + 2 + n]
            token[...] = jnp.zeros_like(token)

    operands = [pltpu.with_memory_space_constraint(a, pltpu.HBM) for a in arrays]
    in_specs = [HBM_SPEC] * n
    if wait:
        operands += list(wait[1])
        in_specs += [SEM_SPEC, SEM_SPEC]
    if after is not None:
        operands.append(after)
        in_specs.append(ANY_SPEC)
    out_shape, out_specs = [], []
    if start:
        out_shape += [pltpu.SemaphoreType.DMA((start[1],)), pltpu.SemaphoreType.DMA((start[1],))]
        out_specs += [SEM_SPEC, SEM_SPEC]
    out_shape += [pltpu.HBM(a.shape, a.dtype) for a in arrays]
    out_specs += [HBM_SPEC] * n
    if start:
        out_shape.append(jax.ShapeDtypeStruct((8, 128), F32))
        out_specs.append(pl.BlockSpec(memory_space=pltpu.VMEM))
    outs = pl.pallas_call(
        body, name=name, in_specs=in_specs, out_specs=out_specs, out_shape=out_shape,
        input_output_aliases={i: n_new + i for i in range(n)},
        compiler_params=pltpu.CompilerParams(has_side_effects=DATAFLOW),
    )(*operands)
    thru = list(outs[n_new:n_new + n])
    return thru, (tuple(outs[:2]) if start else None), (outs[n_new + n] if start else None)


def _remote(src, dst, send_sems, recv_sems, k, to):
    return pltpu.make_async_remote_copy(src_ref=src, dst_ref=dst, send_sem=send_sems.at[k],
                                        recv_sem=recv_sems.at[k], device_id=to, device_id_type=MESH)


def _slot(px, py, pc):
    return 4 * px + 2 * py + pc


def _gather_first(refs, send_sems, recv_sems):
    x, y, c = _my_place()
    blk = refs[0].at[_slot(x, y, c)]
    targets = [(x, y, 1 - c), (1 - x, y, c), (x, 1 - y, c)]
    return [_remote(blk, blk, send_sems, recv_sems, k, to) for k, to in enumerate(targets)]


def _gather_relay(refs, send_sems, recv_sems):
    x, y, c = _my_place()
    rows = refs[0].shape[1]
    half = rows // 32 * 16
    from_x, from_y = _slot(1 - x, y, c), _slot(x, 1 - y, c)
    upper = refs[0].at[from_x, pl.ds(0, half)]
    lower = refs[0].at[from_y, pl.ds(half, rows - half)]
    return [_remote(upper, upper, send_sems, recv_sems, 0, (x, 1 - y, c)),
            _remote(lower, lower, send_sems, recv_sems, 1, (1 - x, y, c)),
            _remote(refs[0].at[from_x], refs[0].at[from_x], send_sems, recv_sems, 2, (x, y, 1 - c)),
            _remote(refs[0].at[from_y], refs[0].at[from_y], send_sems, recv_sems, 3, (x, y, 1 - c))]


def _gather_last(refs, send_sems, recv_sems):
    x, y, c = _my_place()
    blk = refs[0].at[_slot(1 - x, 1 - y, c)]
    return [_remote(blk, blk, send_sems, recv_sems, 0, (x, y, 1 - c))]


def _scatter_sibling(n):
    def copies(refs, send_sems, recv_sems):
        x, y, c = _my_place()
        return [_remote(refs[a].at[2 * j + 1 - c], refs[n + a].at[j], send_sems, recv_sems, 4 * a + j, (x, y, 1 - c))
                for a in range(n) for j in range(4)]
    return copies


def _scatter_chips(n):
    def copies(refs, send_sems, recv_sems):
        x, y, c = _my_place()
        out = []
        for a in range(n):
            for k in range(1, 4):
                px, py = _flip(x, k & 2), _flip(y, k & 1)
                out.append(_remote(refs[a].at[2 * px + py], refs[n + a].at[2 * x + y], send_sems, recv_sems,
                                   3 * a + k - 1, (px, py, c)))
        return out
    return copies


def _pair_sum(name, core, g, t):
    _, r, c = g.shape
    tr = _tile(r, 512, 8)
    g4 = g.reshape(4, 2, r, c)

    def body(core_ref, g_ref, t_ref, o_ref):
        o_ref[...] = (g_ref[...].astype(F32) + t_ref[...].astype(F32)).astype(BF16)

    return pl.pallas_call(
        body, name=name,
        grid_spec=pltpu.PrefetchScalarGridSpec(
            num_scalar_prefetch=1, grid=(4, r // tr),
            in_specs=[pl.BlockSpec((None, None, tr, c), lambda j, i, core_ref: (j, core_ref[0], i, 0)),
                      pl.BlockSpec((None, tr, c), lambda j, i, core_ref: (j, i, 0))],
            out_specs=pl.BlockSpec((None, tr, c), lambda j, i, core_ref: (j, i, 0))),
        out_shape=jax.ShapeDtypeStruct((4, r, c), BF16), compiler_params=_params(),
    )(core, g4, t)


def _adam_math(w, g, m, v):
    m2 = ADAM_B1 * m + (1.0 - ADAM_B1) * g
    v2 = ADAM_B2 * v + (1.0 - ADAM_B2) * (g * g)
    m_hat = m2 / (1.0 - ADAM_B1 ** ADAM_STEP)
    v_hat = v2 / (1.0 - ADAM_B2 ** ADAM_STEP)
    delta = -ADAM_LR * (m_hat / (jnp.sqrt(v_hat) + ADAM_EPS) + ADAM_WD * w)
    return delta, m2, v2


def _adam_sharded(name, chip, w, m, v, grads, row_off=0):
    _, r, c = w.shape
    tr = _tile(r, 256, 8)
    tc = c if tr < r else _tile(c, 256, 128)
    boff = row_off // tr

    def body(chip_ref, w_ref, m_ref, v_ref, p0_ref, q0_ref, p1_ref, q1_ref, g_ref, d_ref, nm_ref, nv_ref):
        mine = chip_ref[0]

        def total(p_ref, q_ref):
            acc = None
            for j in range(4):
                part = jnp.where(mine == j, p_ref[...], q_ref[j]).astype(F32)
                acc = part if acc is None else acc + part
            return acc

        g = jnp.where(pl.program_id(0) == 0, total(p0_ref, q0_ref), total(p1_ref, q1_ref))
        delta, m2, v2 = _adam_math(w_ref[...], g, m_ref[...], v_ref[...])
        g_ref[...] = g
        d_ref[...] = delta
        nm_ref[...] = m2
        nv_ref[...] = v2

    def grad_specs(layer):
        at = lambda l, i, j: (jnp.where(l == layer, boff + i, boff), jnp.where(l == layer, j, 0))
        return [pl.BlockSpec((None, tr, tc), lambda l, i, j, chip_ref: (chip_ref[0],) + at(l, i, j)),
                pl.BlockSpec((4, tr, tc), lambda l, i, j, chip_ref: (0,) + at(l, i, j))]

    wspec = pl.BlockSpec((None, tr, tc), lambda l, i, j, chip_ref: (l, i, j))
    sds = jax.ShapeDtypeStruct(w.shape, F32)
    return pl.pallas_call(
        body, name=name,
        grid_spec=pltpu.PrefetchScalarGridSpec(
            num_scalar_prefetch=1, grid=(2, r // tr, c // tc),
            in_specs=[wspec, wspec, wspec] + grad_specs(0) + grad_specs(1), out_specs=[wspec] * 4),
        out_shape=[sds] * 4, compiler_params=_params(),
    )(chip, w, m, v, grads[0][0], grads[0][1], grads[1][0], grads[1][1])


def _adam_small(name, w, m, v, g):
    def body(w_ref, m_ref, v_ref, g_ref, d_ref, nm_ref, nv_ref):
        delta, m2, v2 = _adam_math(w_ref[...], g_ref[...], m_ref[...], v_ref[...])
        d_ref[...] = delta
        nm_ref[...] = m2
        nv_ref[...] = v2

    sds = jax.ShapeDtypeStruct(w.shape, F32)
    vm = pl.BlockSpec(memory_space=pltpu.VMEM)
    return pl.pallas_call(body, name=name, in_specs=[vm] * 4, out_specs=[vm] * 3, out_shape=[sds] * 3,
                          compiler_params=_params())(w, m, v, g)


GATE_END = GATE_COL + 2 * HEADS


def _merge_dw_in(dwm_t, dwc_t):
    full = jnp.concatenate([dwm_t[:GATE_END], dwc_t.reshape(3 * CONV_W, D_MODEL)], axis=0)
    return full.reshape(N_DEV, IN_SH, D_MODEL)


def _pack128(parts):
    flat = jnp.concatenate([p.reshape(-1) for p in parts])
    n = flat.shape[0]
    rows = -(-n // 1024) * 8
    return jnp.pad(flat, (0, rows * 128 - n)).reshape(rows, 128)


def _unpack128(packed, shapes):
    flat = packed.reshape(-1)
    out, at = [], 0
    for s in shapes:
        n = int(np.prod(s))
        out.append(flat[at:at + n].reshape(s))
        at += n
    return out


def kernel(x, meta_tokens, norm_mix_w, w_in, b_gates, conv_w, mlstm_norm_w, w_out, norm_ffn_w, w_gate, w_up, w_down, norm_final_w, loss_target, m_meta_tokens, m_norm_mix_w, m_w_in, m_b_gates, m_conv_w, m_mlstm_norm_w, m_w_out, m_norm_ffn_w, m_w_gate, m_w_up, m_w_down, m_norm_final_w, v_meta_tokens, v_norm_mix_w, v_w_in, v_b_gates, v_conv_w, v_mlstm_norm_w, v_w_out, v_norm_ffn_w, v_w_gate, v_w_up, v_w_down, v_norm_final_w):
    seq = x.shape[1]
    rows = TOK0 + seq
    me = 4 * lax.axis_index("x") + 2 * lax.axis_index("y") + lax.axis_index("c")
    meta_sh = meta_tokens.shape[1]
    conv_sh = conv_w.shape[2]

    w_gate_t, m_w_gate_t, v_w_gate_t = (jnp.transpose(a, (0, 2, 1)) for a in (w_gate, m_w_gate, v_w_gate))
    w_up_t, m_w_up_t, v_w_up_t = (jnp.transpose(a, (0, 2, 1)) for a in (w_up, m_w_up, v_w_up))
    shards = []
    for l in range(DEPTH):
        shards += [jnp.transpose(w_in[l]).astype(BF16), w_out[l].astype(BF16), w_gate_t[l].astype(BF16),
                   w_up_t[l].astype(BF16), w_down[l].astype(BF16)]
    per_layer = ("w_in", "w_out", "w_gate", "w_up", "w_down")
    gather_names = [f"{nm}_{l}" for l in range(DEPTH) for nm in per_layer]
    gather_state = {}

    def gather_start(i, after):
        if i >= len(shards):
            return after
        buf = lax.dynamic_update_index_in_dim(lax.empty((N_DEV,) + shards[i].shape, BF16), shards[i], me, 0)
        arrs, sems, tok = _split_copy(f"gather_start_{gather_names[i]}", [buf], start=(_gather_first, 3), after=after)
        gather_state[i] = (arrs, sems)
        return tok

    def gather_relay(i, after):
        if i >= len(shards):
            return after
        arrs, sems = gather_state[i]
        arrs, sems, tok = _split_copy(f"gather_relay_{gather_names[i]}", arrs, start=(_gather_relay, 4),
                                      wait=(_gather_first, sems), after=after)
        gather_state[i] = (arrs, sems)
        return tok

    def gather_last(i, after):
        arrs, sems = gather_state[i]
        arrs, sems, tok = _split_copy(f"gather_last_{gather_names[i]}", arrs, start=(_gather_last, 1),
                                      wait=(_gather_relay, sems), after=after)
        gather_state[i] = (arrs, sems)
        return tok

    def gathered(i, after):
        arrs, sems = gather_state[i]
        arrs, _, _ = _split_copy(f"gather_done_{gather_names[i]}", arrs, wait=(_gather_last, sems), after=after)
        return arrs[0]

    bias = [jnp.pad(b_gates[l].reshape(1, 2 * HEADS), ((0, 0), (0, 128 - 2 * HEADS))) for l in range(DEPTH)]
    nmix = [norm_mix_w[l].reshape(1, D_MODEL) for l in range(DEPTH)]
    nffn = [norm_ffn_w[l].reshape(1, D_MODEL) for l in range(DEPTH)]
    nmls = [mlstm_norm_w[l].reshape(1, MLSTM_W) for l in range(DEPTH)]
    weights = [dict() for _ in range(DEPTH)]
    saved = [dict() for _ in range(DEPTH)]

    def layer_fwd(l, h, after):
        w, s = weights[l], saved[l]
        k0 = len(per_layer) * l
        tok = gather_last(k0, after)
        w["win_t"] = gathered(k0, tok).reshape(D_IN, D_MODEL)
        w["wc_t"] = w["win_t"][GATE_END:].reshape(3, CONV_W, D_MODEL)
        tok = gather_start(k0 + 3, gather_relay(k0 + 1, tok))
        s["h0"] = h
        s["hn"] = _rms_fwd(f"norm_mix_{l}", h, nmix[l] + tok[0, 0])
        s["pm"] = _mm_nt(f"proj_mlstm_{l}", s["hn"], w["win_t"], F32, tn=640, tk=D_MODEL, n=PM_W)
        tok = gather_start(k0 + 4, gather_relay(k0 + 2, s["pm"]))
        tok = gather_last(k0 + 1, tok)
        s["pc"] = _mm_nt_bcols(f"proj_conv_{l}", s["hn"], w["wc_t"], F32, dep=tok)
        hm, s["ht"], s["cs"], s["ns"], s["ms"] = _mlstm_fwd(f"mlstm_fwd_{l}", s["pm"], bias[l] + tok[:1], nmls[l])
        tok = gather_start(k0 + 5, gather_relay(k0 + 3, hm))
        tok = gather_last(k0 + 2, tok)
        hc = _conv_fwd(f"conv_fwd_{l}", s["pc"], conv_rows[l] + tok[0, 0])
        s["cat"] = jnp.concatenate([hm, hc], axis=1)
        w["wo"] = gathered(k0 + 1, s["cat"]).reshape(D_MODEL, D_MODEL)
        s["h1"] = _mm_nn(f"out_proj_{l}", s["cat"], w["wo"], F32, res=s["h0"])
        tok = gather_start(k0 + 6, gather_relay(k0 + 4, s["h1"]))
        s["hf"] = _rms_fwd(f"norm_ffn_{l}", s["h1"], nffn[l] + tok[0, 0])
        tok = gather_last(k0 + 3, s["hf"])
        w["wg_t"] = gathered(k0 + 2, tok).reshape(D_FF, D_MODEL)
        w["wu_t"] = gathered(k0 + 3, tok).reshape(D_FF, D_MODEL)
        s["g"], s["u"], s["act"] = _ffn_in(f"ffn_in_{l}", s["hf"], w["wg_t"], w["wu_t"])
        tok = gather_last(k0 + 4, s["act"])
        w["wd"] = gathered(k0 + 4, tok).reshape(D_FF, D_MODEL)
        tok = gather_start(k0 + 7, gather_relay(k0 + 5, tok))
        return _mm_nn(f"ffn_out_{l}", s["act"], w["wd"], F32, res=s["h1"], dep=tok)

    tok = gather_start(0, None)
    zero = tok[0, 0]
    small = jnp.concatenate(
        [meta_tokens + zero, jnp.pad(conv_w.reshape(DEPTH * 3, conv_sh), ((0, 2), (0, meta_sh - conv_sh)))], axis=0)
    slots = _exchange_small("gather_small", small, reduce=False)
    meta_full = jnp.transpose(slots[:, :N_META, :], (1, 0, 2)).reshape(N_META, D_MODEL)
    conv_full = jnp.transpose(slots[:, N_META:N_META + DEPTH * 3, :conv_sh], (1, 0, 2)).reshape(DEPTH, 3, CONV_W)
    conv_rows = [jnp.pad(conv_full[l], ((0, 5), (0, 0))) for l in range(DEPTH)]
    w_in_t, m_w_in_t, v_w_in_t = (jnp.transpose(a + zero, (0, 2, 1)) for a in (w_in, m_w_in, v_w_in))
    tok, w_in_t, m_w_in_t, v_w_in_t, meta_full = lax.optimization_barrier(
        (tok, w_in_t, m_w_in_t, v_w_in_t, meta_full))
    tok = gather_relay(0, tok)
    tok = gather_start(2, gather_start(1, tok))
    h = jnp.concatenate([jnp.zeros((PAD_FRONT, D_MODEL), F32), meta_full, x[0]], axis=0)
    h = layer_fwd(0, h, tok)
    h = layer_fwd(1, h, h)

    dh, dh_b, d_final, loss_part = _final_loss("final_loss", h, norm_final_w.reshape(1, D_MODEL), loss_target[0])

    core = lax.axis_index("c").astype(jnp.int32).reshape(1)
    chip = (2 * lax.axis_index("x") + lax.axis_index("y")).astype(jnp.int32).reshape(1)
    scatter_state = {}

    def scatter_begin(nm, grad):
        land = lax.empty((4,) + grad.shape[1:], BF16)
        arrs, sems, tok = _split_copy(f"grad_sibling_start_{nm}", [grad, land], start=(_scatter_sibling(1), 4))
        scatter_state[nm] = (arrs, sems)
        return tok

    def scatter_advance(nm, after):
        arrs, sems = scatter_state[nm]
        arrs, _, _ = _split_copy(f"grad_sibling_done_{nm}", arrs, wait=(_scatter_sibling(1), sems), after=after)
        part = _pair_sum(f"grad_pair_sum_{nm}", core, arrs[0], arrs[1])
        arrs, sems, tok = _split_copy(f"grad_chips_start_{nm}", [part, lax.empty(part.shape, BF16)],
                                      start=(_scatter_chips(1), 3))
        scatter_state[nm] = (arrs, sems)
        return tok

    def scattered(nm, after):
        arrs, sems = scatter_state[nm]
        arrs, _, _ = _split_copy(f"grad_chips_done_{nm}", arrs, wait=(_scatter_chips(1), sems), after=after)
        return arrs[0], arrs[1]

    d_mix, d_ffn, d_mls, d_bias, d_conv = ([None] * DEPTH for _ in range(5))

    def layer_bwd(l, dh, dh_b, tok):
        w, s = weights[l], saved[l]
        dg, du = _ffn_act_bwd(f"d_act_{l}", dh_b, w["wd"], s["g"], s["u"], dep=tok)
        dw_down = _mm_tn(f"dw_down_{l}", s["act"], dh_b, BF16, tm=1408, tn=1024)
        tok = scatter_begin(f"w_down_{l}", dw_down.reshape(N_DEV, FF_SH, D_MODEL))
        dhf = _mm_nn(f"d_ffn_gate_{l}", dg, w["wg_t"], F32, dep=tok)
        tok = scatter_advance(f"w_down_{l}", after=dhf)
        dhf = _mm_nn(f"d_ffn_up_{l}", du, w["wu_t"], F32, res=dhf, dep=tok)
        dw_gate = _mm_tn(f"dw_gate_{l}", dg, s["hf"], BF16, tm=1408, tn=1024)
        tok = scatter_begin(f"w_gate_{l}", dw_gate.reshape(N_DEV, FF_SH, D_MODEL))
        dw_up = _mm_tn(f"dw_up_{l}", du, s["hf"], BF16, tm=1408, tn=1024, dep=tok)
        tok = scatter_begin(f"w_up_{l}", dw_up.reshape(N_DEV, FF_SH, D_MODEL))
        dh1, dh1_b, d_ffn[l] = _rms_bwd(f"norm_ffn_bwd_{l}", s["h1"], nffn[l] + tok[0, 0], dhf, dh)
        tok = scatter_advance(f"w_gate_{l}", after=dh1)
        dcat = _mm_nt(f"d_cat_{l}", dh1_b, w["wo"], F32, tk=D_MODEL, dep=tok)
        tok = scatter_advance(f"w_up_{l}", after=dcat)
        dw_out = _mm_tn(f"dw_out_{l}", s["cat"], dh1_b, BF16, tn=1024, dep=tok)
        tok = scatter_begin(f"w_out_{l}", dw_out.reshape(N_DEV, OUT_SH, D_MODEL))
        dpm, d_mls[l], d_bias[l] = _mlstm_bwd(f"mlstm_bwd_{l}", dcat, s["pm"], s["ht"], s["cs"], s["ns"],
                                               s["ms"], bias[l] + tok[:1], nmls[l])
        dpc, d_conv[l] = _conv_bwd(f"conv_bwd_{l}", dcat, s["pc"], conv_rows[l])
        tok = scatter_advance(f"w_out_{l}", after=dpc)
        dwm_t = _mm_tn(f"dw_mlstm_{l}", dpm, s["hn"], BF16, tm=640, tn=1024, dep=tok)
        dwc_t = _mm_tn_acols(f"dw_conv_{l}", dpc, s["hn"], BF16)
        tok = scatter_begin(f"w_in_{l}", _merge_dw_in(dwm_t, dwc_t))
        dhn = _mm_nn_kt(f"d_norm_mlstm_{l}", dpm, w["win_t"], F32, tk=PM_W, dep=tok)
        dhn = _mm_nn_ksum(f"d_norm_conv_{l}", dpc, w["wc_t"], F32, res=dhn)
        tok = scatter_advance(f"w_in_{l}", after=dhn)
        dh, dh_b, d_mix[l] = _rms_bwd(f"norm_mix_bwd_{l}", s["h0"], nmix[l] + tok[0, 0], dhn, dh1)
        return dh, dh_b, tok

    dh, dh_b, tok = layer_bwd(1, dh, dh_b, None)
    dh, dh_b, tok_tail = layer_bwd(0, dh, dh_b, tok)

    pq = {}
    after = dh
    for l in reversed(range(DEPTH)):
        for nm in ("w_down", "w_gate", "w_up", "w_out", "w_in"):
            if (nm, l) != ("w_in", 0):
                pq[nm, l] = scattered(f"{nm}_{l}", after)
                after = pq[nm, l][0]
    untransposed = lambda outs: [jnp.transpose(o, (0, 2, 1)) for o in outs]
    g_out, d_out, nm_out, nv_out = _adam_sharded(
        "adam_w_out", chip, w_out, m_w_out, v_w_out, [pq["w_out", 0], pq["w_out", 1]])
    g_gate, d_gate, nm_gate, nv_gate = untransposed(_adam_sharded(
        "adam_w_gate", chip, w_gate_t, m_w_gate_t, v_w_gate_t, [pq["w_gate", 0], pq["w_gate", 1]]))
    g_up, d_up, nm_up, nv_up = untransposed(_adam_sharded(
        "adam_w_up", chip, w_up_t, m_w_up_t, v_w_up_t, [pq["w_up", 0], pq["w_up", 1]]))
    g_down, d_down, nm_down, nv_down = _adam_sharded(
        "adam_w_down", chip, w_down, m_w_down, v_w_down, [pq["w_down", 0], pq["w_down", 1]])
    pq["w_in", 0] = scattered("w_in_0", nv_down)
    g_in, d_in, nm_in, nv_in = untransposed(_adam_sharded(
        "adam_w_in", chip, w_in_t, m_w_in_t, v_w_in_t, [pq["w_in", 0], pq["w_in", 1]]))

    bg = jnp.concatenate([d_bias[l][0, :2 * HEADS] for l in range(DEPTH)])
    red_in = jnp.concatenate([
        dh[PAD_FRONT:TOK0], d_mix[0], d_mix[1], d_ffn[0], d_ffn[1], d_final,
        jnp.concatenate([d_mls[0], d_mls[1]], axis=1),
        jnp.stack([d_conv[l][:3] for l in range(DEPTH)]).reshape(3, 2 * CONV_W),
        jnp.pad(bg, (0, D_MODEL - bg.shape[0])).reshape(1, D_MODEL),
        jnp.pad(loss_part[:, :1], ((0, 0), (0, D_MODEL - 1))),
        jnp.zeros((5, D_MODEL), F32) + tok_tail[0, 0]], axis=0)
    red = _exchange_small("reduce_small", red_in, reduce=True)
    loss = red[26, 0]
    g_meta = lax.dynamic_slice_in_dim(red[:N_META], me * meta_sh, meta_sh, axis=1)
    g_mix, g_ffn, g_final = red[16:18], red[18:20], red[20]
    g_mls = red[21].reshape(DEPTH, MLSTM_W)
    g_conv = lax.dynamic_slice_in_dim(red[22:25].reshape(DEPTH, 3, CONV_W), me * conv_sh, conv_sh, axis=2)
    g_bias = red[25, :DEPTH * 2 * HEADS].reshape(DEPTH, 2 * HEADS)

    small_w = [meta_tokens, norm_mix_w, b_gates, conv_w, mlstm_norm_w, norm_ffn_w, norm_final_w]
    small_m = [m_meta_tokens, m_norm_mix_w, m_b_gates, m_conv_w, m_mlstm_norm_w, m_norm_ffn_w, m_norm_final_w]
    small_v = [v_meta_tokens, v_norm_mix_w, v_b_gates, v_conv_w, v_mlstm_norm_w, v_norm_ffn_w, v_norm_final_w]
    small_g = [g_meta, g_mix, g_bias, g_conv, g_mls, g_ffn, g_final]
    shapes = [a.shape for a in small_w]
    packed = _adam_small("adam_small", _pack128(small_w), _pack128(small_m), _pack128(small_v), _pack128(small_g))
    (d_meta, d_nmix, d_bg, d_cw, d_nmls, d_nffn, d_nfin), (nm_meta, nm_nmix, nm_bg, nm_cw, nm_nmls, nm_nffn, nm_nfin), \
        (nv_meta, nv_nmix, nv_bg, nv_cw, nv_nmls, nv_nffn, nv_nfin) = (_unpack128(p, shapes) for p in packed)

    grad_x = dh[TOK0:].reshape(1, seq, D_MODEL)
    return (loss, grad_x,
            g_meta, g_mix, g_in, g_bias, g_conv, g_mls, g_out, g_ffn, g_gate, g_up, g_down, g_final,
            d_meta, d_nmix, d_in, d_bg, d_cw, d_nmls, d_out, d_nffn, d_gate, d_up, d_down, d_nfin,
            nm_meta, nm_nmix, nm_in, nm_bg, nm_cw, nm_nmls, nm_out, nm_nffn, nm_gate, nm_up, nm_down, nm_nfin,
            nv_meta, nv_nmix, nv_in, nv_bg, nv_cw, nv_nmls, nv_out, nv_nffn, nv_gate, nv_up, nv_down, nv_nfin)
```

```python
import functools

import numpy as np
import jax
import jax.numpy as jnp
from jax import lax
from jax.experimental import pallas as pl
from jax.experimental.pallas import tpu as pltpu

F32 = jnp.float32
BF16 = jnp.bfloat16
MESH = pl.DeviceIdType.MESH

D_MODEL = 2048
DEPTH = 2
N_META = 16
MLSTM_W = 1024
CONV_W = 1024
HEADS = 4
DV = 256
DQK = 128
QK_W = 512
CHUNK = 64
PAD_FRONT = 48
TOK0 = PAD_FRONT + N_META
D_FF = 5632
N_DEV = 8
FF_SH = D_FF // N_DEV
D_IN = 6152
IN_SH = D_IN // N_DEV
OUT_SH = D_MODEL // N_DEV
GATE_COL = 3072
PM_W = GATE_COL + 128
GATE_CAP = 15.0
EPS = 1e-6
QSCALE = DQK ** -0.5

ADAM_LR = 0.001
ADAM_B1 = 0.9
ADAM_B2 = 0.999
ADAM_EPS = 1e-08
ADAM_WD = 0.01
ADAM_STEP = 10

V7X_VMEM_LIMIT = 50 * 1024 * 1024
V7X_MXU_COLS = 256


def _params(**kw):
    return pltpu.CompilerParams(vmem_limit_bytes=V7X_VMEM_LIMIT, **kw)


def _tile(n, target, mult):
    best = None
    for t in range(mult, min(n, target) + 1, mult):
        if n % t == 0:
            best = t
    return best if best is not None else n


def _sigmoid(x):
    return 1.0 / (1.0 + jnp.exp(-x))


NN = ((1,), (0,))
NT = ((1,), (1,))
TN = ((0,), (0,))


def _matmul(name, a, b, out_shape, out_dtype, grid, a_bs, b_bs, o_bs, dims, nk, acc_shape=None,
            res=None, res_bs=None, dep=None):
    has_res = res is not None
    n_in = 2 + has_res + (dep is not None)

    def body(*refs):
        a_ref, b_ref = refs[0], refs[1]
        r_ref = refs[2] if has_res else None
        o_ref = refs[n_in]
        x = lax.dot_general(a_ref[...], b_ref[...], (dims, ((), ())), preferred_element_type=F32)
        if nk == 1:
            if has_res:
                x = x + r_ref[...]
            o_ref[...] = x.astype(o_ref.dtype)
            return
        acc = refs[n_in + 1]
        k = pl.program_id(len(grid) - 1)

        @pl.when(k == 0)
        def _():
            acc[...] = (x + r_ref[...]) if has_res else x

        @pl.when(k > 0)
        def _():
            acc[...] += x

        @pl.when(k == nk - 1)
        def _():
            o_ref[...] = acc[...].astype(o_ref.dtype)

    ins = [a, b] + ([res] if has_res else [])
    specs = [a_bs, b_bs] + ([res_bs] if has_res else [])
    if dep is not None:
        ins.append(dep)
        specs.append(pl.BlockSpec((8, 128), lambda *_: (0, 0)))
    scratch = [pltpu.VMEM(acc_shape, F32)] if nk > 1 else []
    return pl.pallas_call(
        body, name=name, grid=grid, in_specs=specs, out_specs=o_bs,
        out_shape=jax.ShapeDtypeStruct(out_shape, out_dtype), scratch_shapes=scratch,
        compiler_params=_params(),
    )(*ins)


def _mm_nn(name, a, b, out_dtype, res=None, tm=1056, tn=512, dep=None):
    r, k = a.shape
    n = b.shape[1]
    tm, tn = _tile(r, tm, 8), _tile(n, tn, 128)
    return _matmul(name, a, b, (r, n), out_dtype, (r // tm, n // tn, 1),
                   pl.BlockSpec((tm, k), lambda i, j, s: (i, 0)),
                   pl.BlockSpec((k, tn), lambda i, j, s: (0, j)),
                   pl.BlockSpec((tm, tn), lambda i, j, s: (i, j)), NN, 1,
                   res=res, res_bs=pl.BlockSpec((tm, tn), lambda i, j, s: (i, j)), dep=dep)


def _mm_nn_kt(name, a, b, out_dtype, tm=1056, tn=1024, tk=640, dep=None):
    r, k = a.shape
    n = b.shape[1]
    tm, tn, tk = _tile(r, tm, 8), _tile(n, tn, 128), _tile(k, tk, 128)
    nk = k // tk
    return _matmul(name, a, b, (r, n), out_dtype, (r // tm, n // tn, nk),
                   pl.BlockSpec((tm, tk), lambda i, j, s: (i, s)),
                   pl.BlockSpec((tk, tn), lambda i, j, s: (s, j)),
                   pl.BlockSpec((tm, tn), lambda i, j, s: (i, j)), NN, nk, acc_shape=(tm, tn), dep=dep)


def _mm_nn_ksum(name, a3, b3, out_dtype, res=None, tm=1056, tn=1024, dep=None):
    e, r, kb = a3.shape
    n = b3.shape[2]
    tm, tn = _tile(r, tm, 8), _tile(n, tn, 128)
    return _matmul(name, a3, b3, (r, n), out_dtype, (r // tm, n // tn, e),
                   pl.BlockSpec((None, tm, kb), lambda i, j, s: (s, i, 0)),
                   pl.BlockSpec((None, kb, tn), lambda i, j, s: (s, 0, j)),
                   pl.BlockSpec((tm, tn), lambda i, j, s: (i, j)), NN, e, acc_shape=(tm, tn),
                   res=res, res_bs=pl.BlockSpec((tm, tn), lambda i, j, s: (i, j)), dep=dep)


def _mm_nt(name, a, b, out_dtype, res=None, tm=1056, tn=512, tk=640, n=None, dep=None):
    r, k = a.shape
    n = b.shape[0] if n is None else n
    tm, tn, tk = _tile(r, tm, 8), _tile(n, tn, 128), _tile(k, tk, 128)
    nk = k // tk
    return _matmul(name, a, b, (r, n), out_dtype, (r // tm, n // tn, nk),
                   pl.BlockSpec((tm, tk), lambda i, j, s: (i, s)),
                   pl.BlockSpec((tn, tk), lambda i, j, s: (j, s)),
                   pl.BlockSpec((tm, tn), lambda i, j, s: (i, j)), NT, nk, acc_shape=(tm, tn),
                   res=res, res_bs=pl.BlockSpec((tm, tn), lambda i, j, s: (i, j)), dep=dep)


def _mm_nt_bcols(name, a, b3, out_dtype, tm=1056, dep=None):
    r, k = a.shape
    e, n, _ = b3.shape
    tm = _tile(r, tm, 8)
    return _matmul(name, a, b3, (e, r, n), out_dtype, (r // tm, e, 1),
                   pl.BlockSpec((tm, k), lambda i, g, s: (i, 0)),
                   pl.BlockSpec((None, n, k), lambda i, g, s: (g, 0, 0)),
                   pl.BlockSpec((None, tm, n), lambda i, g, s: (g, i, 0)), NT, 1, dep=dep)


def _mm_tn(name, a, b, out_dtype, tm=1024, tn=640, dep=None):
    r, m = a.shape
    n = b.shape[1]
    tm, tn = _tile(m, tm, 128), _tile(n, tn, 128)
    return _matmul(name, a, b, (m, n), out_dtype, (m // tm, n // tn, 1),
                   pl.BlockSpec((r, tm), lambda i, j, s: (0, i)),
                   pl.BlockSpec((r, tn), lambda i, j, s: (0, j)),
                   pl.BlockSpec((tm, tn), lambda i, j, s: (i, j)), TN, 1, dep=dep)


def _mm_tn_acols(name, a3, b, out_dtype, tn=1024, dep=None):
    e, r, m = a3.shape
    n = b.shape[1]
    tn = _tile(n, tn, 128)
    return _matmul(name, a3, b, (e, m, n), out_dtype, (n // tn, e, 1),
                   pl.BlockSpec((None, r, m), lambda j, g, s: (g, 0, 0)),
                   pl.BlockSpec((r, tn), lambda j, g, s: (0, j)),
                   pl.BlockSpec((None, m, tn), lambda j, g, s: (g, 0, j)), TN, 1, dep=dep)


def _rms_fwd(name, h, w):
    r, d = h.shape
    tr = _tile(r, 264, 8)

    def body(h_ref, w_ref, o_ref):
        x = h_ref[...]
        rs = lax.rsqrt(jnp.mean(x * x, axis=1, keepdims=True) + EPS)
        o_ref[...] = (x * rs * w_ref[...]).astype(BF16)

    return pl.pallas_call(
        body, name=name, grid=(r // tr,),
        in_specs=[pl.BlockSpec((tr, d), lambda i: (i, 0)), pl.BlockSpec((1, d), lambda i: (0, 0))],
        out_specs=pl.BlockSpec((tr, d), lambda i: (i, 0)),
        out_shape=jax.ShapeDtypeStruct((r, d), BF16), compiler_params=_params(),
    )(h, w)


def _rms_bwd(name, x, w, dy, dres):
    r, d = x.shape
    tr = _tile(r, 264, 8)

    def body(x_ref, w_ref, dy_ref, dr_ref, dx_ref, dxb_ref, dw_ref):
        xv = x_ref[...]
        g = dy_ref[...]
        rs = lax.rsqrt(jnp.mean(xv * xv, axis=1, keepdims=True) + EPS)
        wg = g * w_ref[...]
        dx = rs * wg - xv * (rs * rs * rs) * jnp.mean(xv * wg, axis=1, keepdims=True) + dr_ref[...]
        dx_ref[...] = dx
        dxb_ref[...] = dx.astype(BF16)
        part = jnp.sum(g * xv * rs, axis=0, keepdims=True)

        @pl.when(pl.program_id(0) == 0)
        def _():
            dw_ref[...] = part

        @pl.when(pl.program_id(0) > 0)
        def _():
            dw_ref[...] += part

    row = pl.BlockSpec((tr, d), lambda i: (i, 0))
    vec = pl.BlockSpec((1, d), lambda i: (0, 0))
    return pl.pallas_call(
        body, name=name, grid=(r // tr,), in_specs=[row, vec, row, row], out_specs=[row, row, vec],
        out_shape=[jax.ShapeDtypeStruct((r, d), F32), jax.ShapeDtypeStruct((r, d), BF16),
                   jax.ShapeDtypeStruct((1, d), F32)],
        compiler_params=_params(),
    )(x, w, dy, dres)


def _final_loss(name, h, w, target):
    r, d = h.shape
    nb = r // CHUNK

    def body(h_ref, w_ref, t_ref, dh_ref, dhb_ref, dw_ref, ls_ref):
        i = pl.program_id(0)

        @pl.when(i == 0)
        def _():
            dh_ref[...] = jnp.zeros_like(dh_ref)
            dhb_ref[...] = jnp.zeros_like(dhb_ref)
            dw_ref[...] = jnp.zeros_like(dw_ref)
            ls_ref[...] = jnp.zeros_like(ls_ref)

        @pl.when(i > 0)
        def _():
            xv = h_ref[...]
            wv = w_ref[...]
            rs = lax.rsqrt(jnp.mean(xv * xv, axis=1, keepdims=True) + EPS)
            err = xv * rs * wv - t_ref[...]
            sq = jnp.sum(jnp.sum(err * err, axis=1, keepdims=True), axis=0, keepdims=True)
            ls_ref[...] += jnp.broadcast_to(sq * (0.5 / d), ls_ref.shape)
            g = err * (1.0 / d)
            wg = g * wv
            dx = rs * wg - xv * (rs * rs * rs) * jnp.mean(xv * wg, axis=1, keepdims=True)
            dh_ref[...] = dx
            dhb_ref[...] = dx.astype(BF16)
            dw_ref[...] += jnp.sum(g * xv * rs, axis=0, keepdims=True)

    row = pl.BlockSpec((CHUNK, d), lambda i: (i, 0))
    vec = pl.BlockSpec((1, d), lambda i: (0, 0))
    return pl.pallas_call(
        body, name=name, grid=(nb,),
        in_specs=[row, vec, pl.BlockSpec((CHUNK, d), lambda i: (jnp.maximum(i - 1, 0), 0))],
        out_specs=[row, row, vec, pl.BlockSpec((1, 128), lambda i: (0, 0))],
        out_shape=[jax.ShapeDtypeStruct((r, d), F32), jax.ShapeDtypeStruct((r, d), BF16),
                   jax.ShapeDtypeStruct((1, d), F32), jax.ShapeDtypeStruct((1, 128), F32)],
        compiler_params=_params(),
    )(h, w, target)


def _ffn_in(name, hf, wg_t, wu_t, dep=None, tm=1056, tn=512):
    r, d = hf.shape
    f = wg_t.shape[0]
    tm, tn = _tile(r, tm, 8), _tile(f, tn, 128)

    def body(h_ref, wg_ref, wu_ref, *rest):
        g_ref, u_ref, a_ref = rest[-3:]
        x = h_ref[...]
        g = lax.dot_general(x, wg_ref[...], (NT, ((), ())), preferred_element_type=F32)
        u = lax.dot_general(x, wu_ref[...], (NT, ((), ())), preferred_element_type=F32)
        g_ref[...] = g.astype(BF16)
        u_ref[...] = u.astype(BF16)
        a_ref[...] = (g * _sigmoid(g) * u).astype(BF16)

    wspec = pl.BlockSpec((tn, d), lambda i, j: (j, 0))
    ospec = pl.BlockSpec((tm, tn), lambda i, j: (i, j))
    ins, specs = [hf, wg_t, wu_t], [pl.BlockSpec((tm, d), lambda i, j: (i, 0)), wspec, wspec]
    if dep is not None:
        ins.append(dep)
        specs.append(pl.BlockSpec((8, 128), lambda *_: (0, 0)))
    return pl.pallas_call(
        body, name=name, grid=(r // tm, f // tn), in_specs=specs, out_specs=[ospec] * 3,
        out_shape=[jax.ShapeDtypeStruct((r, f), BF16)] * 3, compiler_params=_params(),
    )(*ins)


def _ffn_act_bwd(name, dh, wd, g, u, dep=None, tm=1056, tn=512):
    r, d = dh.shape
    f = wd.shape[0]
    tm, tn = _tile(r, tm, 8), _tile(f, tn, 128)

    def body(dh_ref, wd_ref, g_ref, u_ref, *rest):
        dg_ref, du_ref = rest[-2:]
        tr = _tile(tm, 264, 8)
        for r0 in range(0, tm, tr):
            for c0 in range(0, tn, V7X_MXU_COLS):
                rows, cols = slice(r0, r0 + tr), slice(c0, c0 + V7X_MXU_COLS)
                da = lax.dot_general(dh_ref[rows, :], wd_ref[cols, :], (NT, ((), ())), preferred_element_type=F32)
                gv = g_ref[rows, cols].astype(F32)
                s = _sigmoid(gv)
                t = da * s
                du_ref[rows, cols] = (t * gv).astype(BF16)
                dg_ref[rows, cols] = (t * u_ref[rows, cols].astype(F32) * (1.0 + gv - gv * s)).astype(BF16)

    tile = pl.BlockSpec((tm, tn), lambda i, j: (i, j))
    ins = [dh, wd, g, u]
    specs = [pl.BlockSpec((tm, d), lambda i, j: (i, 0)), pl.BlockSpec((tn, d), lambda i, j: (j, 0)), tile, tile]
    if dep is not None:
        ins.append(dep)
        specs.append(pl.BlockSpec((8, 128), lambda *_: (0, 0)))
    return pl.pallas_call(
        body, name=name, grid=(r // tm, f // tn), in_specs=specs, out_specs=[tile] * 2,
        out_shape=[jax.ShapeDtypeStruct((r, f), BF16)] * 2, compiler_params=_params(),
    )(*ins)


def _shift_down(a, k):
    row = lax.broadcasted_iota(jnp.int32, a.shape, 0)
    return jnp.where(row >= k, pltpu.roll(a, k, 0), 0.0)


def _shift_up(a, k):
    n = a.shape[0]
    row = lax.broadcasted_iota(jnp.int32, a.shape, 0)
    return jnp.where(row < n - k, pltpu.roll(a, n - k, 0), 0.0)


def _conv_fwd(name, pc, cw):
    _, r, w = pc.shape

    def body(pc_ref, cw_ref, o_ref):
        a = pc_ref[2] * pc_ref[0]
        cwv = cw_ref[...]
        conv = _shift_down(a, 2) * cwv[0:1] + _shift_down(a, 1) * cwv[1:2] + a * cwv[2:3]
        o_ref[...] = (pc_ref[1] * conv).astype(BF16)

    return pl.pallas_call(
        body, name=name, grid=(w // 128,),
        in_specs=[pl.BlockSpec((3, r, 128), lambda j: (0, 0, j)), pl.BlockSpec((8, 128), lambda j: (0, j))],
        out_specs=pl.BlockSpec((r, 128), lambda j: (0, j)),
        out_shape=jax.ShapeDtypeStruct((r, w), BF16), compiler_params=_params(),
    )(pc, cw)


def _conv_bwd(name, dcat, pc, cw):
    _, r, w = pc.shape
    nblk = w // 128

    def body(dy_ref, pc_ref, cw_ref, dpc_ref, dcw_ref):
        u, gb, gc = pc_ref[0], pc_ref[1], pc_ref[2]
        cwv = cw_ref[...]
        dy = dy_ref[...]
        a = gc * u
        a1, a2 = _shift_down(a, 1), _shift_down(a, 2)
        conv = a2 * cwv[0:1] + a1 * cwv[1:2] + a * cwv[2:3]
        dconv = dy * gb
        da = dconv * cwv[2:3] + _shift_up(dconv, 1) * cwv[1:2] + _shift_up(dconv, 2) * cwv[0:1]
        dpc_ref[0] = (da * gc).astype(BF16)
        dpc_ref[1] = (dy * conv).astype(BF16)
        dpc_ref[2] = (da * u).astype(BF16)
        row = lax.broadcasted_iota(jnp.int32, (8, 128), 0)
        dw0 = jnp.sum(dconv * a2, axis=0, keepdims=True)
        dw1 = jnp.sum(dconv * a1, axis=0, keepdims=True)
        dw2 = jnp.sum(dconv * a, axis=0, keepdims=True)
        dcw_ref[...] = jnp.where(row == 0, dw0, jnp.where(row == 1, dw1, jnp.where(row == 2, dw2, 0.0)))

    return pl.pallas_call(
        body, name=name, grid=(nblk,),
        in_specs=[pl.BlockSpec((r, 128), lambda j: (0, nblk + j)),
                  pl.BlockSpec((3, r, 128), lambda j: (0, 0, j)), pl.BlockSpec((8, 128), lambda j: (0, j))],
        out_specs=[pl.BlockSpec((3, r, 128), lambda j: (0, 0, j)), pl.BlockSpec((8, 128), lambda j: (0, j))],
        out_shape=[jax.ShapeDtypeStruct((3, r, w), BF16), jax.ShapeDtypeStruct((8, w), F32)],
        compiler_params=_params(),
    )(dcat, pc, cw)


def _dot(a, b, dims):
    return lax.dot_general(a, b, (dims, ((), ())), preferred_element_type=F32)


def _col_to_row(xc, eye):
    return jnp.sum(jnp.where(eye, xc, 0.0), axis=0, keepdims=True)


def _row_to_col(xr, eye):
    return jnp.sum(jnp.where(eye, xr, 0.0), axis=1, keepdims=True)


def _gate_tiles(graw, bias, row0):
    th = jnp.tanh((graw + bias) / GATE_CAP)
    z = GATE_CAP * th
    row = lax.broadcasted_iota(jnp.int32, graw.shape, 0) + row0
    real = row >= PAD_FRONT
    li = jnp.where(real, z, -jnp.inf)
    lf = jnp.where(real, jnp.minimum(z, 0.0) - jnp.log(1.0 + jnp.exp(-jnp.abs(z))), 0.0)
    return th, z, li, lf, real


def _interleave(gens):
    results = [None] * len(gens)
    live = list(enumerate(gens))
    while live:
        still = []
        for i, gen in live:
            try:
                next(gen)
                still.append((i, gen))
            except StopIteration as stop:
                results[i] = stop.value
        live = still
    return results


def _chunk_common(pm, h, li, lf, cst, nst, mst, tril, eye):
    kraw = pm[:, QK_W + h * DQK:QK_W + (h + 1) * DQK]
    q = (pm[:, h * DQK:(h + 1) * DQK] * QSCALE).astype(BF16)
    yield
    k = kraw.astype(BF16)
    v = pm[:, 2 * QK_W + h * DV:2 * QK_W + (h + 1) * DV].astype(BF16)
    yield
    li_c = li[:, h:h + 1]
    lf_c = lf[:, HEADS + h:HEADS + h + 1]
    li_r = _col_to_row(li_c, eye)
    yield
    lf_r = _col_to_row(lf_c, eye)
    yield
    b_c = jnp.sum(jnp.where(tril, lf_r, 0.0), axis=1, keepdims=True)
    yield
    b_r = _col_to_row(b_c, eye)
    yield
    dmat = jnp.where(tril, b_c - b_r + li_r, -jnp.inf)
    inter = b_c + mst
    yield
    mt = jnp.maximum(inter, jnp.max(dmat, axis=1, keepdims=True))
    yield
    w_inter = jnp.exp(inter - mt)
    p = jnp.exp(dmat - mt)
    yield
    s = _dot(q, k, NT) * p
    yield
    cb = cst.astype(BF16)
    nb = nst.astype(BF16).astype(F32)
    qc = _dot(q, cb, NN)
    yield
    qn = jnp.sum(q.astype(F32) * nb, axis=1, keepdims=True)
    yield
    den = w_inter * qn + jnp.sum(s, axis=1, keepdims=True)
    yield
    dn = jnp.maximum(jnp.abs(den), jnp.exp(-mt))
    b_end = b_c[CHUNK - 1:CHUNK, :]
    decay = b_end - b_c + li_c
    yield
    m_new = jnp.maximum(b_end + mst, jnp.max(decay, axis=0, keepdims=True))
    yield
    w_old = jnp.exp(b_end + mst - m_new)
    w_in = jnp.exp(decay - m_new)
    kw = (w_in * kraw).astype(BF16)
    yield
    return dict(q=q, k=k, v=v, kraw=kraw, mt=mt, w_inter=w_inter, p=p, s=s, cb=cb, nb=nb, qc=qc, qn=qn,
                den=den, dn=dn, m_new=m_new, w_old=w_old, w_in=w_in, kw=kw)


def _chunks_per_step(nc):
    return 1


def _mlstm_fwd(name, pm, bias, nw):
    r = pm.shape[0]
    nc = r // CHUNK
    grp = _chunks_per_step(nc)

    def body(pm_ref, b_ref, nw_ref, hm_ref, ht_ref, cs_ref, ns_ref, ms_ref, c_scr, n_scr, m_scr):
        step = pl.program_id(0)

        @pl.when(step == 0)
        def _():
            c_scr[...] = jnp.zeros_like(c_scr)
            n_scr[...] = jnp.zeros_like(n_scr)
            m_scr[...] = jnp.zeros_like(m_scr)

        rr = lax.broadcasted_iota(jnp.int32, (CHUNK, CHUNK), 0)
        cc = lax.broadcasted_iota(jnp.int32, (CHUNK, CHUNK), 1)
        tril, eye = cc <= rr, cc == rr
        bv, nwv = b_ref[...], nw_ref[...]
        states = [(c_scr[h], n_scr[h], m_scr[h]) for h in range(HEADS)]
        for g in range(grp):
            rows = slice(g * CHUNK, (g + 1) * CHUNK)
            pmv = pm_ref[rows, :]
            _, _, li, lf, _ = _gate_tiles(pmv[:, GATE_COL:GATE_COL + 128], bv, (step * grp + g) * CHUNK)
            def head(h, cst, nst, mst, g=g, rows=rows, pmv=pmv, li=li, lf=lf):
                f = yield from _chunk_common(pmv, h, li, lf, cst, nst, mst, tril, eye)
                num = f["w_inter"] * f["qc"] + _dot(f["s"].astype(BF16), f["v"], NN)
                yield
                hh = num / f["dn"]
                yield
                c_new = f["w_old"] * cst + _dot(f["kw"], f["v"], TN)
                yield
                n_new = f["w_old"] * nst + jnp.sum(
                    f["w_in"].astype(BF16).astype(F32) * f["k"].astype(F32), axis=0, keepdims=True)
                yield
                sl = slice(h * DV, (h + 1) * DV)
                rs = lax.rsqrt(jnp.mean(hh * hh, axis=1, keepdims=True) + EPS)
                yield
                og = pmv[:, 2 * QK_W + MLSTM_W + h * DV:2 * QK_W + MLSTM_W + (h + 1) * DV]
                cs_ref[g, h] = cst
                ns_ref[g, h] = nst
                ms_ref[g, h] = mst
                ht_ref[rows, sl] = hh
                yield
                hm_ref[rows, sl] = (_sigmoid(og) * (hh * rs * nwv[:, sl])).astype(BF16)
                return c_new, n_new, f["m_new"]

            states = _interleave([head(h, *states[h]) for h in range(HEADS)])
        for h, (cst, nst, mst) in enumerate(states):
            c_scr[h] = cst
            n_scr[h] = nst
            m_scr[h] = mst

    return pl.pallas_call(
        body, name=name, grid=(nc // grp,),
        in_specs=[pl.BlockSpec((grp * CHUNK, PM_W), lambda i: (i, 0)), pl.BlockSpec((1, 128), lambda i: (0, 0)),
                  pl.BlockSpec((1, MLSTM_W), lambda i: (0, 0))],
        out_specs=[pl.BlockSpec((grp * CHUNK, MLSTM_W), lambda i: (i, 0)),
                   pl.BlockSpec((grp * CHUNK, MLSTM_W), lambda i: (i, 0)),
                   pl.BlockSpec((grp, HEADS, DQK, DV), lambda i: (i, 0, 0, 0)),
                   pl.BlockSpec((grp, HEADS, 1, DQK), lambda i: (i, 0, 0, 0)),
                   pl.BlockSpec((grp, HEADS, 1, 1), lambda i: (i, 0, 0, 0))],
        out_shape=[jax.ShapeDtypeStruct((r, MLSTM_W), BF16), jax.ShapeDtypeStruct((r, MLSTM_W), F32),
                   jax.ShapeDtypeStruct((nc, HEADS, DQK, DV), F32),
                   jax.ShapeDtypeStruct((nc, HEADS, 1, DQK), F32),
                   jax.ShapeDtypeStruct((nc, HEADS, 1, 1), F32)],
        scratch_shapes=[pltpu.VMEM((HEADS, DQK, DV), F32), pltpu.VMEM((HEADS, 1, DQK), F32),
                        pltpu.VMEM((HEADS, 1, 1), F32)],
        compiler_params=_params(),
    )(pm, bias, nw)


def _mlstm_bwd(name, dcat, pm, ht, cs, ns, ms, bias, nw):
    r = pm.shape[0]
    nc = r // CHUNK
    grp = _chunks_per_step(nc)
    nsteps = nc // grp

    def body(dy_ref, pm_ref, ht_ref, cs_ref, ns_ref, ms_ref, b_ref, nw_ref, dpm_ref, dnw_ref, db_ref,
             dc_scr, dn_scr):
        step = pl.program_id(0)

        @pl.when(step == 0)
        def _():
            dc_scr[...] = jnp.zeros_like(dc_scr)
            dn_scr[...] = jnp.zeros_like(dn_scr)
            dnw_ref[...] = jnp.zeros_like(dnw_ref)
            db_ref[...] = jnp.zeros_like(db_ref)

        rr = lax.broadcasted_iota(jnp.int32, (CHUNK, CHUNK), 0)
        cc = lax.broadcasted_iota(jnp.int32, (CHUNK, CHUNK), 1)
        tril, eye, triu = cc <= rr, cc == rr, cc >= rr
        lane = lax.broadcasted_iota(jnp.int32, (CHUNK, 128), 1)
        rowid = lax.broadcasted_iota(jnp.int32, (CHUNK, 1), 0)
        bv, nwv = b_ref[...], nw_ref[...]
        carried = [(dc_scr[h], dn_scr[h]) for h in range(HEADS)]
        dnw_acc = [jnp.zeros((1, DV), F32) for _ in range(HEADS)]
        db_acc = jnp.zeros((1, 128), F32)
        for g in reversed(range(grp)):
            rows = slice(g * CHUNK, (g + 1) * CHUNK)
            ci = (nsteps - 1 - step) * grp + g
            pmv = pm_ref[rows, :]
            th, z, li, lf, real = _gate_tiles(pmv[:, GATE_COL:GATE_COL + 128], bv, ci * CHUNK)
            heads = _interleave([
                _mlstm_bwd_head(h, pmv, ht_ref[rows, h * DV:(h + 1) * DV], dy_ref[rows, h * DV:(h + 1) * DV], nwv,
                                li, lf, cs_ref[g, h], ns_ref[g, h], ms_ref[g, h], carried[h][0], carried[h][1],
                                tril, eye, triu, lane, rowid, dpm_ref, rows)
                for h in range(HEADS)])
            carried = [(dc_new, dn_new) for _, dc_new, dn_new, _ in heads]
            dgt = heads[0][0] + heads[1][0] + heads[2][0] + heads[3][0]
            dnw_acc = [dnw_acc[h] + heads[h][3] for h in range(HEADS)]
            dact = jnp.where(lane < HEADS, 1.0, 1.0 - _sigmoid(z)) * (1.0 - th * th)
            dgraw = jnp.where(real & (lane < 2 * HEADS), dgt * dact, 0.0)
            dpm_ref[rows, GATE_COL:GATE_COL + 128] = dgraw.astype(BF16)
            db_acc = db_acc + jnp.sum(dgraw, axis=0, keepdims=True)
        for h, (dcn, dnn) in enumerate(carried):
            dc_scr[h] = dcn
            dn_scr[h] = dnn
            dnw_ref[:, h * DV:(h + 1) * DV] += dnw_acc[h]
        db_ref[...] += db_acc

    rev = lambda i: (nsteps - 1 - i, 0)
    rev4 = lambda i: (nsteps - 1 - i, 0, 0, 0)
    return pl.pallas_call(
        body, name=name, grid=(nsteps,),
        in_specs=[pl.BlockSpec((grp * CHUNK, MLSTM_W), rev), pl.BlockSpec((grp * CHUNK, PM_W), rev),
                  pl.BlockSpec((grp * CHUNK, MLSTM_W), rev),
                  pl.BlockSpec((grp, HEADS, DQK, DV), rev4), pl.BlockSpec((grp, HEADS, 1, DQK), rev4),
                  pl.BlockSpec((grp, HEADS, 1, 1), rev4),
                  pl.BlockSpec((1, 128), lambda i: (0, 0)), pl.BlockSpec((1, MLSTM_W), lambda i: (0, 0))],
        out_specs=[pl.BlockSpec((grp * CHUNK, PM_W), rev), pl.BlockSpec((1, MLSTM_W), lambda i: (0, 0)),
                   pl.BlockSpec((1, 128), lambda i: (0, 0))],
        out_shape=[jax.ShapeDtypeStruct((r, PM_W), BF16), jax.ShapeDtypeStruct((1, MLSTM_W), F32),
                   jax.ShapeDtypeStruct((1, 128), F32)],
        scratch_shapes=[pltpu.VMEM((HEADS, DQK, DV), F32), pltpu.VMEM((HEADS, 1, DQK), F32)],
        compiler_params=_params(),
    )(dcat, pm, ht, cs, ns, ms, bias, nw)


def _mlstm_bwd_head(h, pmv, hh, y, nwv, li, lf, cst, nst, mst, dcn, dnn, tril, eye, triu, lane, rowid,
                    dpm_ref, rows):
    f = yield from _chunk_common(pmv, h, li, lf, cst, nst, mst, tril, eye)
    q, k, v, s, p = f["q"], f["k"], f["v"], f["s"], f["p"]
    w_inter, w_in, w_old, dn = f["w_inter"], f["w_in"], f["w_old"], f["dn"]
    osl = slice(2 * QK_W + MLSTM_W + h * DV, 2 * QK_W + MLSTM_W + (h + 1) * DV)
    sg = _sigmoid(pmv[:, osl])
    yield
    rs = lax.rsqrt(jnp.mean(hh * hh, axis=1, keepdims=True) + EPS)
    yield
    nwh = nwv[:, h * DV:(h + 1) * DV]
    dpm_ref[rows, osl] = (y * (hh * rs * nwh) * sg * (1.0 - sg)).astype(BF16)
    yield
    dhn = y * sg
    dnw_h = jnp.sum(dhn * hh * rs, axis=0, keepdims=True)
    yield
    wd = dhn * nwh
    dhh = rs * wd - hh * (rs * rs * rs) * jnp.mean(hh * wd, axis=1, keepdims=True)
    yield
    dnum = dhh / dn
    dd = -jnp.sum(dhh * hh, axis=1, keepdims=True) / dn
    yield
    dden = jnp.where(jnp.abs(f["den"]) > jnp.exp(-f["mt"]), dd * jnp.sign(f["den"]), 0.0)
    dnum_b = dnum.astype(BF16)
    wdn = (w_inter * dnum).astype(BF16)
    wid = (w_inter * dden).astype(BF16).astype(F32)
    yield
    ds = _dot(dnum_b, v, NT) + dden
    yield
    dsp = (ds * p).astype(BF16)
    yield
    dq = _dot(dsp, k, NN) + _dot(wdn, f["cb"], NT) + wid * f["nb"]
    yield
    dk = _dot(dsp, q, TN)
    yield
    dv = _dot(s.astype(BF16), dnum_b, TN)
    yield
    g = ds * s
    g_col = _row_to_col(jnp.sum(g, axis=0, keepdims=True), eye)
    yield
    db = jnp.sum(g, axis=1, keepdims=True) - g_col
    dli = g_col
    yield
    db = db + (jnp.sum(dnum * f["qc"], axis=1, keepdims=True) + dden * f["qn"]) * w_inter
    yield
    dcnb = dcn.astype(BF16)
    dnnb = dnn.astype(BF16).astype(F32)
    dkw = _dot(v, dcnb, NT) + dnnb
    yield
    dk = dk + w_in * dkw
    dv = dv + _dot(f["kw"], dcnb, NN)
    yield
    ddecay = jnp.sum(dkw * f["kraw"], axis=1, keepdims=True) * w_in
    yield
    dw_old = (jnp.sum(jnp.sum(dcn * cst, axis=1, keepdims=True), axis=0, keepdims=True)
              + jnp.sum(dnn * nst, axis=1, keepdims=True))
    yield
    db_end = dw_old * w_old + jnp.sum(ddecay, axis=0, keepdims=True)
    db = db - ddecay + jnp.where(rowid == CHUNK - 1, db_end, 0.0)
    dli = dli + ddecay
    yield
    dc_new = w_old * dcn + _dot(q, wdn, TN)
    yield
    dn_new = w_old * dnn + jnp.sum(wid * q.astype(F32), axis=0, keepdims=True)
    yield
    dlf = jnp.sum(jnp.where(triu, _col_to_row(db, eye), 0.0), axis=1, keepdims=True)
    yield
    gate_part = jnp.where(lane == h, dli, 0.0) + jnp.where(lane == HEADS + h, dlf, 0.0)
    dpm_ref[rows, h * DQK:(h + 1) * DQK] = (dq * QSCALE).astype(BF16)
    yield
    dpm_ref[rows, QK_W + h * DQK:QK_W + (h + 1) * DQK] = dk.astype(BF16)
    yield
    dpm_ref[rows, 2 * QK_W + h * DV:2 * QK_W + (h + 1) * DV] = dv.astype(BF16)
    return gate_part, dc_new, dn_new, dnw_h


def _my_place():
    return lax.axis_index("x"), lax.axis_index("y"), lax.axis_index("c")


def _flip(v, bit):
    return 1 - v if bit else v


def _exchange_small(name, blk, reduce):
    r, c = blk.shape

    def body(x_ref, o_ref, *rest):
        slots = rest[0] if reduce else o_ref
        send_sems, recv_sems = rest[-2], rest[-1]
        x, y, cc = _my_place()
        me = 4 * x + 2 * y + cc
        slots[me] = x_ref[...]
        copies = []
        for k in range(1, N_DEV):
            peer = (_flip(x, k & 4), _flip(y, k & 2), _flip(cc, k & 1))
            cp = pltpu.make_async_remote_copy(
                src_ref=x_ref, dst_ref=slots.at[me], send_sem=send_sems.at[k - 1],
                recv_sem=recv_sems.at[k - 1], device_id=peer, device_id_type=MESH)
            cp.start()
            copies.append(cp)
        for cp in copies:
            cp.wait()
        if reduce:
            acc = slots[0]
            for d in range(1, N_DEV):
                acc = acc + slots[d]
            o_ref[...] = acc

    scratch = ([pltpu.VMEM((N_DEV, r, c), F32)] if reduce else []) + [
        pltpu.SemaphoreType.DMA((N_DEV - 1,)), pltpu.SemaphoreType.DMA((N_DEV - 1,))]
    return pl.pallas_call(
        body, name=name,
        out_shape=jax.ShapeDtypeStruct((r, c) if reduce else (N_DEV, r, c), F32),
        in_specs=[pl.BlockSpec(memory_space=pltpu.VMEM)], out_specs=pl.BlockSpec(memory_space=pltpu.VMEM),
        scratch_shapes=scratch, compiler_params=_params(),
    )(blk)


HBM_SPEC = pl.BlockSpec(memory_space=pltpu.HBM)
SEM_SPEC = pl.BlockSpec(memory_space=pltpu.SEMAPHORE)
ANY_SPEC = pl.BlockSpec(memory_space=pl.ANY)
DATAFLOW = pltpu.SideEffectType.DATAFLOW_SIDE_EFFECTING


def _split_copy(name, arrays, start=None, wait=None, after=None):
    results, token = _split_copies(name, [(arrays, start, wait)], after)
    return results[0][0], results[0][1], token


def _split_copies(name, jobs, after=None):
    operands, in_specs, out_shape, out_specs, aliases = [], [], [], [], {}
    in_at, out_at = [], []
    for arrays, start, wait in jobs:
        in_at.append(len(operands))
        operands += [pltpu.with_memory_space_constraint(a, pltpu.HBM) for a in arrays]
        in_specs += [HBM_SPEC] * len(arrays)
        if wait:
            operands += list(wait[1])
            in_specs += [SEM_SPEC, SEM_SPEC]
    if after is not None:
        operands.append(after)
        in_specs.append(ANY_SPEC)
    for j, (arrays, start, wait) in enumerate(jobs):
        out_at.append(len(out_shape))
        if start:
            out_shape += [pltpu.SemaphoreType.DMA((start[1],)), pltpu.SemaphoreType.DMA((start[1],))]
            out_specs += [SEM_SPEC, SEM_SPEC]
        for i, a in enumerate(arrays):
            aliases[in_at[j] + i] = len(out_shape)
            out_shape.append(pltpu.HBM(a.shape, a.dtype))
            out_specs.append(HBM_SPEC)
    any_start = any(start for _, start, _ in jobs)
    if any_start:
        out_shape.append(jax.ShapeDtypeStruct((8, 128), F32))
        out_specs.append(pl.BlockSpec(memory_space=pltpu.VMEM))
    n_in = len(operands)

    def body(*refs):
        for j, (arrays, start, wait) in enumerate(jobs):
            if wait:
                ins = refs[in_at[j]:in_at[j] + len(arrays)]
                at = in_at[j] + len(arrays)
                for cp in wait[0](ins, refs[at], refs[at + 1]):
                    cp.wait_send()
                    cp.wait_recv()
        for j, (arrays, start, wait) in enumerate(jobs):
            if start:
                ins = refs[in_at[j]:in_at[j] + len(arrays)]
                at = n_in + out_at[j]
                for cp in start[0](ins, refs[at], refs[at + 1]):
                    cp.start()
        if any_start:
            token = refs[n_in + len(out_shape) - 1]
            token[...] = jnp.zeros_like(token)

    outs = pl.pallas_call(
        body, name=name, in_specs=in_specs, out_specs=out_specs, out_shape=out_shape,
        input_output_aliases=aliases, compiler_params=pltpu.CompilerParams(has_side_effects=DATAFLOW),
    )(*operands)
    results = []
    for j, (arrays, start, wait) in enumerate(jobs):
        at = out_at[j]
        sems = (outs[at], outs[at + 1]) if start else None
        at += 2 if start else 0
        results.append((list(outs[at:at + len(arrays)]), sems))
    return results, (outs[-1] if any_start else None)


def _remote(src, dst, send_sems, recv_sems, k, to):
    return pltpu.make_async_remote_copy(src_ref=src, dst_ref=dst, send_sem=send_sems.at[k],
                                        recv_sem=recv_sems.at[k], device_id=to, device_id_type=MESH)


def _slot(px, py, pc):
    return 4 * px + 2 * py + pc


def _gather_first(refs, send_sems, recv_sems):
    x, y, c = _my_place()
    blk = refs[0].at[_slot(x, y, c)]
    targets = [(x, y, 1 - c), (1 - x, y, c), (x, 1 - y, c)]
    return [_remote(blk, blk, send_sems, recv_sems, k, to) for k, to in enumerate(targets)]


def _gather_relay(refs, send_sems, recv_sems):
    x, y, c = _my_place()
    rows = refs[0].shape[1]
    half = rows // 32 * 16
    from_x, from_y = _slot(1 - x, y, c), _slot(x, 1 - y, c)
    upper = refs[0].at[from_x, pl.ds(0, half)]
    lower = refs[0].at[from_y, pl.ds(half, rows - half)]
    return [_remote(upper, upper, send_sems, recv_sems, 0, (x, 1 - y, c)),
            _remote(lower, lower, send_sems, recv_sems, 1, (1 - x, y, c)),
            _remote(refs[0].at[from_x], refs[0].at[from_x], send_sems, recv_sems, 2, (x, y, 1 - c)),
            _remote(refs[0].at[from_y], refs[0].at[from_y], send_sems, recv_sems, 3, (x, y, 1 - c))]


def _gather_last(refs, send_sems, recv_sems):
    x, y, c = _my_place()
    blk = refs[0].at[_slot(1 - x, 1 - y, c)]
    return [_remote(blk, blk, send_sems, recv_sems, 0, (x, y, 1 - c))]


def _scatter_sibling(n):
    def copies(refs, send_sems, recv_sems):
        x, y, c = _my_place()
        return [_remote(refs[a].at[2 * j + 1 - c], refs[n + a].at[j], send_sems, recv_sems, 4 * a + j, (x, y, 1 - c))
                for a in range(n) for j in range(4)]
    return copies


def _scatter_chips(n):
    def copies(refs, send_sems, recv_sems):
        x, y, c = _my_place()
        out = []
        for a in range(n):
            for k in range(1, 4):
                px, py = _flip(x, k & 2), _flip(y, k & 1)
                out.append(_remote(refs[a].at[2 * px + py], refs[n + a].at[2 * x + y], send_sems, recv_sems,
                                   3 * a + k - 1, (px, py, c)))
        return out
    return copies


def _pair_sum(name, core, g, t):
    _, r, c = g.shape
    tr = _tile(r, 512, 8)
    g4 = g.reshape(4, 2, r, c)

    def body(core_ref, g_ref, t_ref, o_ref):
        o_ref[...] = (g_ref[...].astype(F32) + t_ref[...].astype(F32)).astype(BF16)

    return pl.pallas_call(
        body, name=name,
        grid_spec=pltpu.PrefetchScalarGridSpec(
            num_scalar_prefetch=1, grid=(4, r // tr),
            in_specs=[pl.BlockSpec((None, None, tr, c), lambda j, i, core_ref: (j, core_ref[0], i, 0)),
                      pl.BlockSpec((None, tr, c), lambda j, i, core_ref: (j, i, 0))],
            out_specs=pl.BlockSpec((None, tr, c), lambda j, i, core_ref: (j, i, 0))),
        out_shape=jax.ShapeDtypeStruct((4, r, c), BF16), compiler_params=_params(),
    )(core, g4, t)


def _adam_math(w, g, m, v):
    m2 = ADAM_B1 * m + (1.0 - ADAM_B1) * g
    v2 = ADAM_B2 * v + (1.0 - ADAM_B2) * (g * g)
    m_hat = m2 / (1.0 - ADAM_B1 ** ADAM_STEP)
    v_hat = v2 / (1.0 - ADAM_B2 ** ADAM_STEP)
    delta = -ADAM_LR * (m_hat / (jnp.sqrt(v_hat) + ADAM_EPS) + ADAM_WD * w)
    return delta, m2, v2


def _adam_sharded(name, chip, w, m, v, grads, row_off=0):
    _, r, c = w.shape
    tr = _tile(r, 256, 8)
    tc = c if tr < r else _tile(c, 256, 128)
    boff = row_off // tr

    def body(chip_ref, w_ref, m_ref, v_ref, p0_ref, q0_ref, p1_ref, q1_ref, g_ref, d_ref, nm_ref, nv_ref):
        mine = chip_ref[0]

        def total(p_ref, q_ref):
            acc = None
            for j in range(4):
                part = jnp.where(mine == j, p_ref[...], q_ref[j]).astype(F32)
                acc = part if acc is None else acc + part
            return acc

        g = jnp.where(pl.program_id(0) == 0, total(p0_ref, q0_ref), total(p1_ref, q1_ref))
        delta, m2, v2 = _adam_math(w_ref[...], g, m_ref[...], v_ref[...])
        g_ref[...] = g
        d_ref[...] = delta
        nm_ref[...] = m2
        nv_ref[...] = v2

    def grad_specs(layer):
        at = lambda l, i, j: (jnp.where(l == layer, boff + i, boff), jnp.where(l == layer, j, 0))
        return [pl.BlockSpec((None, tr, tc), lambda l, i, j, chip_ref: (chip_ref[0],) + at(l, i, j)),
                pl.BlockSpec((4, tr, tc), lambda l, i, j, chip_ref: (0,) + at(l, i, j))]

    wspec = pl.BlockSpec((None, tr, tc), lambda l, i, j, chip_ref: (l, i, j))
    sds = jax.ShapeDtypeStruct(w.shape, F32)
    return pl.pallas_call(
        body, name=name,
        grid_spec=pltpu.PrefetchScalarGridSpec(
            num_scalar_prefetch=1, grid=(2, r // tr, c // tc),
            in_specs=[wspec, wspec, wspec] + grad_specs(0) + grad_specs(1), out_specs=[wspec] * 4),
        out_shape=[sds] * 4, compiler_params=_params(),
    )(chip, w, m, v, grads[0][0], grads[0][1], grads[1][0], grads[1][1])


def _adam_small(name, w, m, v, g):
    def body(w_ref, m_ref, v_ref, g_ref, d_ref, nm_ref, nv_ref):
        delta, m2, v2 = _adam_math(w_ref[...], g_ref[...], m_ref[...], v_ref[...])
        d_ref[...] = delta
        nm_ref[...] = m2
        nv_ref[...] = v2

    sds = jax.ShapeDtypeStruct(w.shape, F32)
    vm = pl.BlockSpec(memory_space=pltpu.VMEM)
    return pl.pallas_call(body, name=name, in_specs=[vm] * 4, out_specs=[vm] * 3, out_shape=[sds] * 3,
                          compiler_params=_params())(w, m, v, g)


GATE_END = GATE_COL + 2 * HEADS


def _merge_dw_in(dwm_t, dwc_t):
    full = jnp.concatenate([dwm_t[:GATE_END], dwc_t.reshape(3 * CONV_W, D_MODEL)], axis=0)
    return full.reshape(N_DEV, IN_SH, D_MODEL)


def _pack128(parts):
    flat = jnp.concatenate([p.reshape(-1) for p in parts])
    n = flat.shape[0]
    rows = -(-n // 1024) * 8
    return jnp.pad(flat, (0, rows * 128 - n)).reshape(rows, 128)


def _unpack128(packed, shapes):
    flat = packed.reshape(-1)
    out, at = [], 0
    for s in shapes:
        n = int(np.prod(s))
        out.append(flat[at:at + n].reshape(s))
        at += n
    return out


def kernel(x, meta_tokens, norm_mix_w, w_in, b_gates, conv_w, mlstm_norm_w, w_out, norm_ffn_w, w_gate, w_up, w_down, norm_final_w, loss_target, m_meta_tokens, m_norm_mix_w, m_w_in, m_b_gates, m_conv_w, m_mlstm_norm_w, m_w_out, m_norm_ffn_w, m_w_gate, m_w_up, m_w_down, m_norm_final_w, v_meta_tokens, v_norm_mix_w, v_w_in, v_b_gates, v_conv_w, v_mlstm_norm_w, v_w_out, v_norm_ffn_w, v_w_gate, v_w_up, v_w_down, v_norm_final_w):
    seq = x.shape[1]
    rows = TOK0 + seq
    me = 4 * lax.axis_index("x") + 2 * lax.axis_index("y") + lax.axis_index("c")
    meta_sh = meta_tokens.shape[1]
    conv_sh = conv_w.shape[2]

    w_gate_t, m_w_gate_t, v_w_gate_t = (jnp.transpose(a, (0, 2, 1)) for a in (w_gate, m_w_gate, v_w_gate))
    w_up_t, m_w_up_t, v_w_up_t = (jnp.transpose(a, (0, 2, 1)) for a in (w_up, m_w_up, v_w_up))
    shards = []
    for l in range(DEPTH):
        shards += [jnp.transpose(w_in[l]).astype(BF16), w_out[l].astype(BF16), w_gate_t[l].astype(BF16),
                   w_up_t[l].astype(BF16), w_down[l].astype(BF16)]
    per_layer = ("w_in", "w_out", "w_gate", "w_up", "w_down")
    gather_names = [f"{nm}_{l}" for l in range(DEPTH) for nm in per_layer]
    gather_state = {}

    def gather_step(tag, after, start=None, relay=None, last=None, done=()):
        jobs, idx = [], []
        if start is not None and start < len(shards):
            buf = lax.dynamic_update_index_in_dim(lax.empty((N_DEV,) + shards[start].shape, BF16), shards[start], me, 0)
            jobs.append(([buf], (_gather_first, 3), None))
            idx.append(start)
        if relay is not None and relay < len(shards):
            jobs.append((gather_state[relay][0], (_gather_relay, 4), (_gather_first, gather_state[relay][1])))
            idx.append(relay)
        if last is not None:
            jobs.append((gather_state[last][0], (_gather_last, 1), (_gather_relay, gather_state[last][1])))
            idx.append(last)
        for i in done:
            jobs.append((gather_state[i][0], None, (_gather_last, gather_state[i][1])))
            idx.append(i)
        if not jobs:
            return after, []
        results, tok = _split_copies(f"gather_{tag}", jobs, after)
        for i, res in zip(idx, results):
            gather_state[i] = res
        return (after if tok is None else tok), [gather_state[i][0][0] for i in done]

    bias = [jnp.pad(b_gates[l].reshape(1, 2 * HEADS), ((0, 0), (0, 128 - 2 * HEADS))) for l in range(DEPTH)]
    nmix = [norm_mix_w[l].reshape(1, D_MODEL) for l in range(DEPTH)]
    nffn = [norm_ffn_w[l].reshape(1, D_MODEL) for l in range(DEPTH)]
    nmls = [mlstm_norm_w[l].reshape(1, MLSTM_W) for l in range(DEPTH)]
    weights = [dict() for _ in range(DEPTH)]
    saved = [dict() for _ in range(DEPTH)]

    def layer_fwd(l, h, after):
        w, s = weights[l], saved[l]
        k0 = len(per_layer) * l
        if l == 0:
            tok, _ = gather_step("l0_a", after, last=k0)
            _, (g_in,) = gather_step("l0_b", tok, done=[k0])
            tok, _ = gather_step("l0_c", g_in, relay=k0 + 1, start=k0 + 3)
        else:
            tok, _ = gather_step(f"l{l}_a", after, last=k0, relay=k0 + 1, start=k0 + 3)
            _, (g_in,) = gather_step(f"l{l}_b", tok, done=[k0])
        w["win_t"] = g_in.reshape(D_IN, D_MODEL)
        w["wc_t"] = w["win_t"][GATE_END:].reshape(3, CONV_W, D_MODEL)
        s["h0"] = h
        s["hn"] = _rms_fwd(f"norm_mix_{l}", h, nmix[l] + tok[0, 0])
        s["pm"] = _mm_nt(f"proj_mlstm_{l}", s["hn"], w["win_t"], F32, tn=640, tk=D_MODEL, n=PM_W)
        tok, _ = gather_step(f"l{l}_d", s["pm"], relay=k0 + 2, start=k0 + 4, last=k0 + 1)
        s["pc"] = _mm_nt_bcols(f"proj_conv_{l}", s["hn"], w["wc_t"], F32, dep=tok)
        hm, s["ht"], s["cs"], s["ns"], s["ms"] = _mlstm_fwd(f"mlstm_fwd_{l}", s["pm"], bias[l] + tok[:1], nmls[l])
        tok, _ = gather_step(f"l{l}_e", hm, relay=k0 + 3, start=k0 + 5, last=k0 + 2)
        hc = _conv_fwd(f"conv_fwd_{l}", s["pc"], conv_rows[l] + tok[0, 0])
        s["cat"] = jnp.concatenate([hm, hc], axis=1)
        _, (g_out,) = gather_step(f"l{l}_f", s["cat"], done=[k0 + 1])
        w["wo"] = g_out.reshape(D_MODEL, D_MODEL)
        s["h1"] = _mm_nn(f"out_proj_{l}", s["cat"], w["wo"], F32, res=s["h0"])
        tok, _ = gather_step(f"l{l}_g", s["h1"], relay=k0 + 4, start=k0 + 6)
        s["hf"] = _rms_fwd(f"norm_ffn_{l}", s["h1"], nffn[l] + tok[0, 0])
        tok, _ = gather_step(f"l{l}_h", s["hf"], last=k0 + 3)
        _, (g_gate, g_up) = gather_step(f"l{l}_i", tok, done=[k0 + 2, k0 + 3])
        w["wg_t"] = g_gate.reshape(D_FF, D_MODEL)
        w["wu_t"] = g_up.reshape(D_FF, D_MODEL)
        s["g"], s["u"], s["act"] = _ffn_in(f"ffn_in_{l}", s["hf"], w["wg_t"], w["wu_t"])
        tok, _ = gather_step(f"l{l}_j", s["act"], last=k0 + 4, relay=k0 + 5, start=k0 + 7)
        _, (g_down,) = gather_step(f"l{l}_k", tok, done=[k0 + 4])
        w["wd"] = g_down.reshape(D_FF, D_MODEL)
        return _mm_nn(f"ffn_out_{l}", s["act"], w["wd"], F32, res=s["h1"], dep=tok)

    tok, _ = gather_step("first", None, start=0)
    zero = tok[0, 0]
    small = jnp.concatenate(
        [meta_tokens + zero, jnp.pad(conv_w.reshape(DEPTH * 3, conv_sh), ((0, 2), (0, meta_sh - conv_sh)))], axis=0)
    slots = _exchange_small("gather_small", small, reduce=False)
    meta_full = jnp.transpose(slots[:, :N_META, :], (1, 0, 2)).reshape(N_META, D_MODEL)
    conv_full = jnp.transpose(slots[:, N_META:N_META + DEPTH * 3, :conv_sh], (1, 0, 2)).reshape(DEPTH, 3, CONV_W)
    conv_rows = [jnp.pad(conv_full[l], ((0, 5), (0, 0))) for l in range(DEPTH)]
    w_in_t, m_w_in_t, v_w_in_t = (jnp.transpose(a + zero, (0, 2, 1)) for a in (w_in, m_w_in, v_w_in))
    tok, w_in_t, m_w_in_t, v_w_in_t, meta_full = lax.optimization_barrier(
        (tok, w_in_t, m_w_in_t, v_w_in_t, meta_full))
    tok, _ = gather_step("pre_a", tok, relay=0)
    tok, _ = gather_step("pre_b", tok, start=1)
    tok, _ = gather_step("pre_c", tok, start=2)
    h = jnp.concatenate([jnp.zeros((PAD_FRONT, D_MODEL), F32), meta_full, x[0]], axis=0)
    h = layer_fwd(0, h, tok)
    h = layer_fwd(1, h, h)

    dh, dh_b, d_final, loss_part = _final_loss("final_loss", h, norm_final_w.reshape(1, D_MODEL), loss_target[0])

    core = lax.axis_index("c").astype(jnp.int32).reshape(1)
    chip = (2 * lax.axis_index("x") + lax.axis_index("y")).astype(jnp.int32).reshape(1)
    scatter_state = {}

    def scatter_begin(nm, grad):
        land = lax.empty((4,) + grad.shape[1:], BF16)
        arrs, sems, tok = _split_copy(f"grad_sibling_start_{nm}", [grad, land], start=(_scatter_sibling(1), 4))
        scatter_state[nm] = (arrs, sems)
        return tok

    def scatter_advance(nm, after):
        arrs, sems = scatter_state[nm]
        arrs, _, _ = _split_copy(f"grad_sibling_done_{nm}", arrs, wait=(_scatter_sibling(1), sems), after=after)
        part = _pair_sum(f"grad_pair_sum_{nm}", core, arrs[0], arrs[1])
        arrs, sems, tok = _split_copy(f"grad_chips_start_{nm}", [part, lax.empty(part.shape, BF16)],
                                      start=(_scatter_chips(1), 3))
        scatter_state[nm] = (arrs, sems)
        return tok

    def scattered(nm, after):
        arrs, sems = scatter_state[nm]
        arrs, _, _ = _split_copy(f"grad_chips_done_{nm}", arrs, wait=(_scatter_chips(1), sems), after=after)
        return arrs[0], arrs[1]

    d_mix, d_ffn, d_mls, d_bias, d_conv = ([None] * DEPTH for _ in range(5))

    def layer_bwd(l, dh, dh_b, tok):
        w, s = weights[l], saved[l]
        dg, du = _ffn_act_bwd(f"d_act_{l}", dh_b, w["wd"], s["g"], s["u"], dep=tok)
        dw_down = _mm_tn(f"dw_down_{l}", s["act"], dh_b, BF16, tm=1408, tn=1024)
        tok = scatter_begin(f"w_down_{l}", dw_down.reshape(N_DEV, FF_SH, D_MODEL))
        dhf = _mm_nn(f"d_ffn_gate_{l}", dg, w["wg_t"], F32, dep=tok)
        tok = scatter_advance(f"w_down_{l}", after=dhf)
        dhf = _mm_nn(f"d_ffn_up_{l}", du, w["wu_t"], F32, res=dhf, dep=tok)
        dw_gate = _mm_tn(f"dw_gate_{l}", dg, s["hf"], BF16, tm=1408, tn=1024)
        tok = scatter_begin(f"w_gate_{l}", dw_gate.reshape(N_DEV, FF_SH, D_MODEL))
        dw_up = _mm_tn(f"dw_up_{l}", du, s["hf"], BF16, tm=1408, tn=1024, dep=tok)
        tok = scatter_begin(f"w_up_{l}", dw_up.reshape(N_DEV, FF_SH, D_MODEL))
        dh1, dh1_b, d_ffn[l] = _rms_bwd(f"norm_ffn_bwd_{l}", s["h1"], nffn[l] + tok[0, 0], dhf, dh)
        tok = scatter_advance(f"w_gate_{l}", after=dh1)
        dcat = _mm_nt(f"d_cat_{l}", dh1_b, w["wo"], F32, tk=D_MODEL, dep=tok)
        tok = scatter_advance(f"w_up_{l}", after=dcat)
        dw_out = _mm_tn(f"dw_out_{l}", s["cat"], dh1_b, BF16, tn=1024, dep=tok)
        tok = scatter_begin(f"w_out_{l}", dw_out.reshape(N_DEV, OUT_SH, D_MODEL))
        dpm, d_mls[l], d_bias[l] = _mlstm_bwd(f"mlstm_bwd_{l}", dcat, s["pm"], s["ht"], s["cs"], s["ns"],
                                               s["ms"], bias[l] + tok[:1], nmls[l])
        dpc, d_conv[l] = _conv_bwd(f"conv_bwd_{l}", dcat, s["pc"], conv_rows[l])
        tok = scatter_advance(f"w_out_{l}", after=dpc)
        dwm_t = _mm_tn(f"dw_mlstm_{l}", dpm, s["hn"], BF16, tm=640, tn=1024, dep=tok)
        dwc_t = _mm_tn_acols(f"dw_conv_{l}", dpc, s["hn"], BF16)
        tok = scatter_begin(f"w_in_{l}", _merge_dw_in(dwm_t, dwc_t))
        dhn = _mm_nn_kt(f"d_norm_mlstm_{l}", dpm, w["win_t"], F32, tk=PM_W, dep=tok)
        dhn = _mm_nn_ksum(f"d_norm_conv_{l}", dpc, w["wc_t"], F32, res=dhn)
        tok = scatter_advance(f"w_in_{l}", after=dhn)
        dh, dh_b, d_mix[l] = _rms_bwd(f"norm_mix_bwd_{l}", s["h0"], nmix[l] + tok[0, 0], dhn, dh1)
        return dh, dh_b, tok

    dh, dh_b, tok = layer_bwd(1, dh, dh_b, None)
    dh, dh_b, tok_tail = layer_bwd(0, dh, dh_b, tok)

    pq = {}
    after = dh
    for l in reversed(range(DEPTH)):
        for nm in ("w_down", "w_gate", "w_up", "w_out", "w_in"):
            if (nm, l) != ("w_in", 0):
                pq[nm, l] = scattered(f"{nm}_{l}", after)
                after = pq[nm, l][0]
    untransposed = lambda outs: [jnp.transpose(o, (0, 2, 1)) for o in outs]
    g_out, d_out, nm_out, nv_out = _adam_sharded(
        "adam_w_out", chip, w_out, m_w_out, v_w_out, [pq["w_out", 0], pq["w_out", 1]])
    g_gate, d_gate, nm_gate, nv_gate = untransposed(_adam_sharded(
        "adam_w_gate", chip, w_gate_t, m_w_gate_t, v_w_gate_t, [pq["w_gate", 0], pq["w_gate", 1]]))
    g_up, d_up, nm_up, nv_up = untransposed(_adam_sharded(
        "adam_w_up", chip, w_up_t, m_w_up_t, v_w_up_t, [pq["w_up", 0], pq["w_up", 1]]))
    g_down, d_down, nm_down, nv_down = _adam_sharded(
        "adam_w_down", chip, w_down, m_w_down, v_w_down, [pq["w_down", 0], pq["w_down", 1]])

    bg = jnp.concatenate([d_bias[l][0, :2 * HEADS] for l in range(DEPTH)])
    red_in = jnp.concatenate([
        dh[PAD_FRONT:TOK0], d_mix[0], d_mix[1], d_ffn[0], d_ffn[1], d_final,
        jnp.concatenate([d_mls[0], d_mls[1]], axis=1),
        jnp.stack([d_conv[l][:3] for l in range(DEPTH)]).reshape(3, 2 * CONV_W),
        jnp.pad(bg, (0, D_MODEL - bg.shape[0])).reshape(1, D_MODEL),
        jnp.pad(loss_part[:, :1], ((0, 0), (0, D_MODEL - 1))),
        jnp.zeros((5, D_MODEL), F32) + tok_tail[0, 0]], axis=0)
    red = _exchange_small("reduce_small", red_in, reduce=True)
    loss = red[26, 0]
    g_meta = lax.dynamic_slice_in_dim(red[:N_META], me * meta_sh, meta_sh, axis=1)
    g_mix, g_ffn, g_final = red[16:18], red[18:20], red[20]
    g_mls = red[21].reshape(DEPTH, MLSTM_W)
    g_conv = lax.dynamic_slice_in_dim(red[22:25].reshape(DEPTH, 3, CONV_W), me * conv_sh, conv_sh, axis=2)
    g_bias = red[25, :DEPTH * 2 * HEADS].reshape(DEPTH, 2 * HEADS)

    small_w = [meta_tokens, norm_mix_w, b_gates, conv_w, mlstm_norm_w, norm_ffn_w, norm_final_w]
    small_m = [m_meta_tokens, m_norm_mix_w, m_b_gates, m_conv_w, m_mlstm_norm_w, m_norm_ffn_w, m_norm_final_w]
    small_v = [v_meta_tokens, v_norm_mix_w, v_b_gates, v_conv_w, v_mlstm_norm_w, v_norm_ffn_w, v_norm_final_w]
    small_g = [g_meta, g_mix, g_bias, g_conv, g_mls, g_ffn, g_final]
    shapes = [a.shape for a in small_w]
    packed = _adam_small("adam_small", _pack128(small_w), _pack128(small_m), _pack128(small_v), _pack128(small_g))
    (d_meta, d_nmix, d_bg, d_cw, d_nmls, d_nffn, d_nfin), (nm_meta, nm_nmix, nm_bg, nm_cw, nm_nmls, nm_nffn, nm_nfin), \
        (nv_meta, nv_nmix, nv_bg, nv_cw, nv_nmls, nv_nffn, nv_nfin) = (_unpack128(p, shapes) for p in packed)

    pq["w_in", 0] = scattered("w_in_0", jnp.concatenate([packed[0][:8], nv_down[0, :8, :128]], axis=0))
    g_in, d_in, nm_in, nv_in = untransposed(_adam_sharded(
        "adam_w_in", chip, w_in_t, m_w_in_t, v_w_in_t, [pq["w_in", 0], pq["w_in", 1]]))

    grad_x = dh[TOK0:].reshape(1, seq, D_MODEL)
    return (loss, grad_x,
            g_meta, g_mix, g_in, g_bias, g_conv, g_mls, g_out, g_ffn, g_gate, g_up, g_down, g_final,
            d_meta, d_nmix, d_in, d_bg, d_cw, d_nmls, d_out, d_nffn, d_gate, d_up, d_down, d_nfin,
            nm_meta, nm_nmix, nm_in, nm_bg, nm_cw, nm_nmls, nm_out, nm_nffn, nm_gate, nm_up, nm_down, nm_nfin,
            nv_meta, nv_nmix, nv_in, nv_bg, nv_cw, nv_nmls, nv_out, nv_nffn, nv_gate, nv_up, nv_down, nv_nfin)
```

```python
import functools

import numpy as np
import jax
import jax.numpy as jnp
from jax import lax
from jax.experimental import pallas as pl
from jax.experimental.pallas import tpu as pltpu

F32 = jnp.float32
BF16 = jnp.bfloat16
MESH = pl.DeviceIdType.MESH

D_MODEL = 2048
DEPTH = 2
N_META = 16
MLSTM_W = 1024
CONV_W = 1024
HEADS = 4
DV = 256
DQK = 128
QK_W = 512
CHUNK = 64
PAD_FRONT = 48
TOK0 = PAD_FRONT + N_META
D_FF = 5632
N_DEV = 8
FF_SH = D_FF // N_DEV
D_IN = 6152
IN_SH = D_IN // N_DEV
OUT_SH = D_MODEL // N_DEV
GATE_COL = 3072
PM_W = GATE_COL + 128
GATE_CAP = 15.0
EPS = 1e-6
QSCALE = DQK ** -0.5

ADAM_LR = 0.001
ADAM_B1 = 0.9
ADAM_B2 = 0.999
ADAM_EPS = 1e-08
ADAM_WD = 0.01
ADAM_STEP = 10

V7X_VMEM_LIMIT = 50 * 1024 * 1024
V7X_MXU_COLS = 256


def _params(**kw):
    return pltpu.CompilerParams(vmem_limit_bytes=V7X_VMEM_LIMIT, **kw)


def _tile(n, target, mult):
    best = None
    for t in range(mult, min(n, target) + 1, mult):
        if n % t == 0:
            best = t
    return best if best is not None else n


def _sigmoid(x):
    return 1.0 / (1.0 + jnp.exp(-x))


NN = ((1,), (0,))
NT = ((1,), (1,))
TN = ((0,), (0,))


def _matmul(name, a, b, out_shape, out_dtype, grid, a_bs, b_bs, o_bs, dims, nk, acc_shape=None,
            res=None, res_bs=None, dep=None):
    has_res = res is not None
    n_in = 2 + has_res + (dep is not None)

    def body(*refs):
        a_ref, b_ref = refs[0], refs[1]
        r_ref = refs[2] if has_res else None
        o_ref = refs[n_in]
        x = lax.dot_general(a_ref[...], b_ref[...], (dims, ((), ())), preferred_element_type=F32)
        if nk == 1:
            if has_res:
                x = x + r_ref[...]
            o_ref[...] = x.astype(o_ref.dtype)
            return
        acc = refs[n_in + 1]
        k = pl.program_id(len(grid) - 1)

        @pl.when(k == 0)
        def _():
            acc[...] = (x + r_ref[...]) if has_res else x

        @pl.when(k > 0)
        def _():
            acc[...] += x

        @pl.when(k == nk - 1)
        def _():
            o_ref[...] = acc[...].astype(o_ref.dtype)

    ins = [a, b] + ([res] if has_res else [])
    specs = [a_bs, b_bs] + ([res_bs] if has_res else [])
    if dep is not None:
        ins.append(dep)
        specs.append(pl.BlockSpec((8, 128), lambda *_: (0, 0)))
    scratch = [pltpu.VMEM(acc_shape, F32)] if nk > 1 else []
    return pl.pallas_call(
        body, name=name, grid=grid, in_specs=specs, out_specs=o_bs,
        out_shape=jax.ShapeDtypeStruct(out_shape, out_dtype), scratch_shapes=scratch,
        compiler_params=_params(),
    )(*ins)


def _mm_nn(name, a, b, out_dtype, res=None, tm=1056, tn=512, dep=None):
    r, k = a.shape
    n = b.shape[1]
    tm, tn = _tile(r, tm, 8), _tile(n, tn, 128)
    return _matmul(name, a, b, (r, n), out_dtype, (r // tm, n // tn, 1),
                   pl.BlockSpec((tm, k), lambda i, j, s: (i, 0)),
                   pl.BlockSpec((k, tn), lambda i, j, s: (0, j)),
                   pl.BlockSpec((tm, tn), lambda i, j, s: (i, j)), NN, 1,
                   res=res, res_bs=pl.BlockSpec((tm, tn), lambda i, j, s: (i, j)), dep=dep)


def _mm_nn_kt(name, a, b, out_dtype, tm=1056, tn=1024, tk=640, dep=None):
    r, k = a.shape
    n = b.shape[1]
    tm, tn, tk = _tile(r, tm, 8), _tile(n, tn, 128), _tile(k, tk, 128)
    nk = k // tk
    return _matmul(name, a, b, (r, n), out_dtype, (r // tm, n // tn, nk),
                   pl.BlockSpec((tm, tk), lambda i, j, s: (i, s)),
                   pl.BlockSpec((tk, tn), lambda i, j, s: (s, j)),
                   pl.BlockSpec((tm, tn), lambda i, j, s: (i, j)), NN, nk, acc_shape=(tm, tn), dep=dep)


def _mm_nn_ksum(name, a3, b3, out_dtype, res=None, tm=1056, tn=1024, dep=None):
    e, r, kb = a3.shape
    n = b3.shape[2]
    tm, tn = _tile(r, tm, 8), _tile(n, tn, 128)
    return _matmul(name, a3, b3, (r, n), out_dtype, (r // tm, n // tn, e),
                   pl.BlockSpec((None, tm, kb), lambda i, j, s: (s, i, 0)),
                   pl.BlockSpec((None, kb, tn), lambda i, j, s: (s, 0, j)),
                   pl.BlockSpec((tm, tn), lambda i, j, s: (i, j)), NN, e, acc_shape=(tm, tn),
                   res=res, res_bs=pl.BlockSpec((tm, tn), lambda i, j, s: (i, j)), dep=dep)


def _mm_nt(name, a, b, out_dtype, res=None, tm=1056, tn=512, tk=640, n=None, dep=None):
    r, k = a.shape
    n = b.shape[0] if n is None else n
    tm, tn, tk = _tile(r, tm, 8), _tile(n, tn, 128), _tile(k, tk, 128)
    nk = k // tk
    return _matmul(name, a, b, (r, n), out_dtype, (r // tm, n // tn, nk),
                   pl.BlockSpec((tm, tk), lambda i, j, s: (i, s)),
                   pl.BlockSpec((tn, tk), lambda i, j, s: (j, s)),
                   pl.BlockSpec((tm, tn), lambda i, j, s: (i, j)), NT, nk, acc_shape=(tm, tn),
                   res=res, res_bs=pl.BlockSpec((tm, tn), lambda i, j, s: (i, j)), dep=dep)


def _mm_nt_bcols(name, a, b3, out_dtype, tm=1056, dep=None):
    r, k = a.shape
    e, n, _ = b3.shape
    tm = _tile(r, tm, 8)
    return _matmul(name, a, b3, (e, r, n), out_dtype, (r // tm, e, 1),
                   pl.BlockSpec((tm, k), lambda i, g, s: (i, 0)),
                   pl.BlockSpec((None, n, k), lambda i, g, s: (g, 0, 0)),
                   pl.BlockSpec((None, tm, n), lambda i, g, s: (g, i, 0)), NT, 1, dep=dep)


def _mm_tn(name, a, b, out_dtype, tm=1024, tn=640, dep=None):
    r, m = a.shape
    n = b.shape[1]
    tm, tn = _tile(m, tm, 128), _tile(n, tn, 128)
    return _matmul(name, a, b, (m, n), out_dtype, (m // tm, n // tn, 1),
                   pl.BlockSpec((r, tm), lambda i, j, s: (0, i)),
                   pl.BlockSpec((r, tn), lambda i, j, s: (0, j)),
                   pl.BlockSpec((tm, tn), lambda i, j, s: (i, j)), TN, 1, dep=dep)


def _mm_tn_acols(name, a3, b, out_dtype, tn=1024, dep=None):
    e, r, m = a3.shape
    n = b.shape[1]
    tn = _tile(n, tn, 128)
    return _matmul(name, a3, b, (e, m, n), out_dtype, (n // tn, e, 1),
                   pl.BlockSpec((None, r, m), lambda j, g, s: (g, 0, 0)),
                   pl.BlockSpec((r, tn), lambda j, g, s: (0, j)),
                   pl.BlockSpec((None, m, tn), lambda j, g, s: (g, 0, j)), TN, 1, dep=dep)


def _rms_fwd(name, h, w):
    r, d = h.shape
    tr = _tile(r, 264, 8)

    def body(h_ref, w_ref, o_ref):
        x = h_ref[...]
        rs = lax.rsqrt(jnp.mean(x * x, axis=1, keepdims=True) + EPS)
        o_ref[...] = (x * rs * w_ref[...]).astype(BF16)

    return pl.pallas_call(
        body, name=name, grid=(r // tr,),
        in_specs=[pl.BlockSpec((tr, d), lambda i: (i, 0)), pl.BlockSpec((1, d), lambda i: (0, 0))],
        out_specs=pl.BlockSpec((tr, d), lambda i: (i, 0)),
        out_shape=jax.ShapeDtypeStruct((r, d), BF16), compiler_params=_params(),
    )(h, w)


def _rms_bwd(name, x, w, dy, dres):
    r, d = x.shape
    tr = _tile(r, 264, 8)

    def body(x_ref, w_ref, dy_ref, dr_ref, dx_ref, dxb_ref, dw_ref):
        xv = x_ref[...]
        g = dy_ref[...]
        rs = lax.rsqrt(jnp.mean(xv * xv, axis=1, keepdims=True) + EPS)
        wg = g * w_ref[...]
        dx = rs * wg - xv * (rs * rs * rs) * jnp.mean(xv * wg, axis=1, keepdims=True) + dr_ref[...]
        dx_ref[...] = dx
        dxb_ref[...] = dx.astype(BF16)
        part = jnp.sum(g * xv * rs, axis=0, keepdims=True)

        @pl.when(pl.program_id(0) == 0)
        def _():
            dw_ref[...] = part

        @pl.when(pl.program_id(0) > 0)
        def _():
            dw_ref[...] += part

    row = pl.BlockSpec((tr, d), lambda i: (i, 0))
    vec = pl.BlockSpec((1, d), lambda i: (0, 0))
    return pl.pallas_call(
        body, name=name, grid=(r // tr,), in_specs=[row, vec, row, row], out_specs=[row, row, vec],
        out_shape=[jax.ShapeDtypeStruct((r, d), F32), jax.ShapeDtypeStruct((r, d), BF16),
                   jax.ShapeDtypeStruct((1, d), F32)],
        compiler_params=_params(),
    )(x, w, dy, dres)


def _final_loss(name, h, w, target):
    r, d = h.shape
    nb = r // CHUNK

    def body(h_ref, w_ref, t_ref, dh_ref, dhb_ref, dw_ref, ls_ref):
        i = pl.program_id(0)

        @pl.when(i == 0)
        def _():
            dh_ref[...] = jnp.zeros_like(dh_ref)
            dhb_ref[...] = jnp.zeros_like(dhb_ref)
            dw_ref[...] = jnp.zeros_like(dw_ref)
            ls_ref[...] = jnp.zeros_like(ls_ref)

        @pl.when(i > 0)
        def _():
            xv = h_ref[...]
            wv = w_ref[...]
            rs = lax.rsqrt(jnp.mean(xv * xv, axis=1, keepdims=True) + EPS)
            err = xv * rs * wv - t_ref[...]
            sq = jnp.sum(jnp.sum(err * err, axis=1, keepdims=True), axis=0, keepdims=True)
            ls_ref[...] += jnp.broadcast_to(sq * (0.5 / d), ls_ref.shape)
            g = err * (1.0 / d)
            wg = g * wv
            dx = rs * wg - xv * (rs * rs * rs) * jnp.mean(xv * wg, axis=1, keepdims=True)
            dh_ref[...] = dx
            dhb_ref[...] = dx.astype(BF16)
            dw_ref[...] += jnp.sum(g * xv * rs, axis=0, keepdims=True)

    row = pl.BlockSpec((CHUNK, d), lambda i: (i, 0))
    vec = pl.BlockSpec((1, d), lambda i: (0, 0))
    return pl.pallas_call(
        body, name=name, grid=(nb,),
        in_specs=[row, vec, pl.BlockSpec((CHUNK, d), lambda i: (jnp.maximum(i - 1, 0), 0))],
        out_specs=[row, row, vec, pl.BlockSpec((1, 128), lambda i: (0, 0))],
        out_shape=[jax.ShapeDtypeStruct((r, d), F32), jax.ShapeDtypeStruct((r, d), BF16),
                   jax.ShapeDtypeStruct((1, d), F32), jax.ShapeDtypeStruct((1, 128), F32)],
        compiler_params=_params(),
    )(h, w, target)


def _ffn_in(name, hf, wg_t, wu_t, dep=None, tm=1056, tn=512):
    r, d = hf.shape
    f = wg_t.shape[0]
    tm, tn = _tile(r, tm, 8), _tile(f, tn, 128)

    def body(h_ref, wg_ref, wu_ref, *rest):
        g_ref, u_ref, a_ref = rest[-3:]
        x = h_ref[...]
        g = lax.dot_general(x, wg_ref[...], (NT, ((), ())), preferred_element_type=F32)
        u = lax.dot_general(x, wu_ref[...], (NT, ((), ())), preferred_element_type=F32)
        g_ref[...] = g.astype(BF16)
        u_ref[...] = u.astype(BF16)
        a_ref[...] = (g * _sigmoid(g) * u).astype(BF16)

    wspec = pl.BlockSpec((tn, d), lambda i, j: (j, 0))
    ospec = pl.BlockSpec((tm, tn), lambda i, j: (i, j))
    ins, specs = [hf, wg_t, wu_t], [pl.BlockSpec((tm, d), lambda i, j: (i, 0)), wspec, wspec]
    if dep is not None:
        ins.append(dep)
        specs.append(pl.BlockSpec((8, 128), lambda *_: (0, 0)))
    return pl.pallas_call(
        body, name=name, grid=(r // tm, f // tn), in_specs=specs, out_specs=[ospec] * 3,
        out_shape=[jax.ShapeDtypeStruct((r, f), BF16)] * 3, compiler_params=_params(),
    )(*ins)


def _ffn_act_bwd(name, dh, wd, g, u, dep=None, tm=1056, tn=512):
    r, d = dh.shape
    f = wd.shape[0]
    tm, tn = _tile(r, tm, 8), _tile(f, tn, 128)

    def body(dh_ref, wd_ref, g_ref, u_ref, *rest):
        dg_ref, du_ref = rest[-2:]
        tr = _tile(tm, 264, 8)
        for r0 in range(0, tm, tr):
            for c0 in range(0, tn, V7X_MXU_COLS):
                rows, cols = slice(r0, r0 + tr), slice(c0, c0 + V7X_MXU_COLS)
                da = lax.dot_general(dh_ref[rows, :], wd_ref[cols, :], (NT, ((), ())), preferred_element_type=F32)
                gv = g_ref[rows, cols].astype(F32)
                s = _sigmoid(gv)
                t = da * s
                du_ref[rows, cols] = (t * gv).astype(BF16)
                dg_ref[rows, cols] = (t * u_ref[rows, cols].astype(F32) * (1.0 + gv - gv * s)).astype(BF16)

    tile = pl.BlockSpec((tm, tn), lambda i, j: (i, j))
    ins = [dh, wd, g, u]
    specs = [pl.BlockSpec((tm, d), lambda i, j: (i, 0)), pl.BlockSpec((tn, d), lambda i, j: (j, 0)), tile, tile]
    if dep is not None:
        ins.append(dep)
        specs.append(pl.BlockSpec((8, 128), lambda *_: (0, 0)))
    return pl.pallas_call(
        body, name=name, grid=(r // tm, f // tn), in_specs=specs, out_specs=[tile] * 2,
        out_shape=[jax.ShapeDtypeStruct((r, f), BF16)] * 2, compiler_params=_params(),
    )(*ins)


def _shift_down(a, k):
    row = lax.broadcasted_iota(jnp.int32, a.shape, 0)
    return jnp.where(row >= k, pltpu.roll(a, k, 0), 0.0)


def _shift_up(a, k):
    n = a.shape[0]
    row = lax.broadcasted_iota(jnp.int32, a.shape, 0)
    return jnp.where(row < n - k, pltpu.roll(a, n - k, 0), 0.0)


def _conv_fwd(name, pc, cw):
    _, r, w = pc.shape

    def body(pc_ref, cw_ref, o_ref):
        a = pc_ref[2] * pc_ref[0]
        cwv = cw_ref[...]
        conv = _shift_down(a, 2) * cwv[0:1] + _shift_down(a, 1) * cwv[1:2] + a * cwv[2:3]
        o_ref[...] = (pc_ref[1] * conv).astype(BF16)

    return pl.pallas_call(
        body, name=name, grid=(w // 128,),
        in_specs=[pl.BlockSpec((3, r, 128), lambda j: (0, 0, j)), pl.BlockSpec((8, 128), lambda j: (0, j))],
        out_specs=pl.BlockSpec((r, 128), lambda j: (0, j)),
        out_shape=jax.ShapeDtypeStruct((r, w), BF16), compiler_params=_params(),
    )(pc, cw)


def _conv_bwd(name, dcat, pc, cw):
    _, r, w = pc.shape
    nblk = w // 128

    def body(dy_ref, pc_ref, cw_ref, dpc_ref, dcw_ref):
        u, gb, gc = pc_ref[0], pc_ref[1], pc_ref[2]
        cwv = cw_ref[...]
        dy = dy_ref[...]
        a = gc * u
        a1, a2 = _shift_down(a, 1), _shift_down(a, 2)
        conv = a2 * cwv[0:1] + a1 * cwv[1:2] + a * cwv[2:3]
        dconv = dy * gb
        da = dconv * cwv[2:3] + _shift_up(dconv, 1) * cwv[1:2] + _shift_up(dconv, 2) * cwv[0:1]
        dpc_ref[0] = (da * gc).astype(BF16)
        dpc_ref[1] = (dy * conv).astype(BF16)
        dpc_ref[2] = (da * u).astype(BF16)
        row = lax.broadcasted_iota(jnp.int32, (8, 128), 0)
        dw0 = jnp.sum(dconv * a2, axis=0, keepdims=True)
        dw1 = jnp.sum(dconv * a1, axis=0, keepdims=True)
        dw2 = jnp.sum(dconv * a, axis=0, keepdims=True)
        dcw_ref[...] = jnp.where(row == 0, dw0, jnp.where(row == 1, dw1, jnp.where(row == 2, dw2, 0.0)))

    return pl.pallas_call(
        body, name=name, grid=(nblk,),
        in_specs=[pl.BlockSpec((r, 128), lambda j: (0, nblk + j)),
                  pl.BlockSpec((3, r, 128), lambda j: (0, 0, j)), pl.BlockSpec((8, 128), lambda j: (0, j))],
        out_specs=[pl.BlockSpec((3, r, 128), lambda j: (0, 0, j)), pl.BlockSpec((8, 128), lambda j: (0, j))],
        out_shape=[jax.ShapeDtypeStruct((3, r, w), BF16), jax.ShapeDtypeStruct((8, w), F32)],
        compiler_params=_params(),
    )(dcat, pc, cw)


def _dot(a, b, dims):
    return lax.dot_general(a, b, (dims, ((), ())), preferred_element_type=F32)


def _col_to_row(xc, eye):
    return jnp.sum(jnp.where(eye, xc, 0.0), axis=0, keepdims=True)


def _row_to_col(xr, eye):
    return jnp.sum(jnp.where(eye, xr, 0.0), axis=1, keepdims=True)


def _gate_tiles(graw, bias, row0):
    th = jnp.tanh((graw + bias) / GATE_CAP)
    z = GATE_CAP * th
    row = lax.broadcasted_iota(jnp.int32, graw.shape, 0) + row0
    real = row >= PAD_FRONT
    li = jnp.where(real, z, -jnp.inf)
    lf = jnp.where(real, jnp.minimum(z, 0.0) - jnp.log(1.0 + jnp.exp(-jnp.abs(z))), 0.0)
    return th, z, li, lf, real


def _interleave(gens):
    results = [None] * len(gens)
    live = list(enumerate(gens))
    while live:
        still = []
        for i, gen in live:
            try:
                next(gen)
                still.append((i, gen))
            except StopIteration as stop:
                results[i] = stop.value
        live = still
    return results


def _chunk_common(pm, h, li, lf, cst, nst, mst, tril, eye):
    kraw = pm[:, QK_W + h * DQK:QK_W + (h + 1) * DQK]
    q = (pm[:, h * DQK:(h + 1) * DQK] * QSCALE).astype(BF16)
    yield
    k = kraw.astype(BF16)
    v = pm[:, 2 * QK_W + h * DV:2 * QK_W + (h + 1) * DV].astype(BF16)
    yield
    li_c = li[:, h:h + 1]
    lf_c = lf[:, HEADS + h:HEADS + h + 1]
    li_r = _col_to_row(li_c, eye)
    yield
    lf_r = _col_to_row(lf_c, eye)
    yield
    b_c = jnp.sum(jnp.where(tril, lf_r, 0.0), axis=1, keepdims=True)
    yield
    b_r = _col_to_row(b_c, eye)
    yield
    dmat = jnp.where(tril, b_c - b_r + li_r, -jnp.inf)
    inter = b_c + mst
    yield
    mt = jnp.maximum(inter, jnp.max(dmat, axis=1, keepdims=True))
    yield
    w_inter = jnp.exp(inter - mt)
    p = jnp.exp(dmat - mt)
    yield
    s = _dot(q, k, NT) * p
    yield
    cb = cst.astype(BF16)
    nb = nst.astype(BF16).astype(F32)
    qc = _dot(q, cb, NN)
    yield
    qn = jnp.sum(q.astype(F32) * nb, axis=1, keepdims=True)
    yield
    den = w_inter * qn + jnp.sum(s, axis=1, keepdims=True)
    yield
    dn = jnp.maximum(jnp.abs(den), jnp.exp(-mt))
    b_end = b_c[CHUNK - 1:CHUNK, :]
    decay = b_end - b_c + li_c
    yield
    m_new = jnp.maximum(b_end + mst, jnp.max(decay, axis=0, keepdims=True))
    yield
    w_old = jnp.exp(b_end + mst - m_new)
    w_in = jnp.exp(decay - m_new)
    kw = (w_in * kraw).astype(BF16)
    yield
    return dict(q=q, k=k, v=v, kraw=kraw, mt=mt, w_inter=w_inter, p=p, s=s, cb=cb, nb=nb, qc=qc, qn=qn,
                den=den, dn=dn, m_new=m_new, w_old=w_old, w_in=w_in, kw=kw)


def _chunks_per_step(nc):
    return 1


def _mlstm_fwd(name, pm, bias, nw):
    r = pm.shape[0]
    nc = r // CHUNK
    grp = _chunks_per_step(nc)

    def body(pm_ref, b_ref, nw_ref, hm_ref, ht_ref, cs_ref, ns_ref, ms_ref, c_scr, n_scr, m_scr):
        step = pl.program_id(0)

        @pl.when(step == 0)
        def _():
            c_scr[...] = jnp.zeros_like(c_scr)
            n_scr[...] = jnp.zeros_like(n_scr)
            m_scr[...] = jnp.zeros_like(m_scr)

        rr = lax.broadcasted_iota(jnp.int32, (CHUNK, CHUNK), 0)
        cc = lax.broadcasted_iota(jnp.int32, (CHUNK, CHUNK), 1)
        tril, eye = cc <= rr, cc == rr
        bv, nwv = b_ref[...], nw_ref[...]
        states = [(c_scr[h], n_scr[h], m_scr[h]) for h in range(HEADS)]
        for g in range(grp):
            rows = slice(g * CHUNK, (g + 1) * CHUNK)
            pmv = pm_ref[rows, :]
            _, _, li, lf, _ = _gate_tiles(pmv[:, GATE_COL:GATE_COL + 128], bv, (step * grp + g) * CHUNK)
            def head(h, cst, nst, mst, g=g, rows=rows, pmv=pmv, li=li, lf=lf):
                f = yield from _chunk_common(pmv, h, li, lf, cst, nst, mst, tril, eye)
                num = f["w_inter"] * f["qc"] + _dot(f["s"].astype(BF16), f["v"], NN)
                yield
                hh = num / f["dn"]
                yield
                c_new = f["w_old"] * cst + _dot(f["kw"], f["v"], TN)
                yield
                n_new = f["w_old"] * nst + jnp.sum(
                    f["w_in"].astype(BF16).astype(F32) * f["k"].astype(F32), axis=0, keepdims=True)
                yield
                sl = slice(h * DV, (h + 1) * DV)
                rs = lax.rsqrt(jnp.mean(hh * hh, axis=1, keepdims=True) + EPS)
                yield
                og = pmv[:, 2 * QK_W + MLSTM_W + h * DV:2 * QK_W + MLSTM_W + (h + 1) * DV]
                cs_ref[g, h] = cst
                ns_ref[g, h] = nst
                ms_ref[g, h] = mst
                ht_ref[rows, sl] = hh
                yield
                hm_ref[rows, sl] = (_sigmoid(og) * (hh * rs * nwv[:, sl])).astype(BF16)
                return c_new, n_new, f["m_new"]

            states = _interleave([head(h, *states[h]) for h in range(HEADS)])
        for h, (cst, nst, mst) in enumerate(states):
            c_scr[h] = cst
            n_scr[h] = nst
            m_scr[h] = mst

    return pl.pallas_call(
        body, name=name, grid=(nc // grp,),
        in_specs=[pl.BlockSpec((grp * CHUNK, PM_W), lambda i: (i, 0)), pl.BlockSpec((1, 128), lambda i: (0, 0)),
                  pl.BlockSpec((1, MLSTM_W), lambda i: (0, 0))],
        out_specs=[pl.BlockSpec((grp * CHUNK, MLSTM_W), lambda i: (i, 0)),
                   pl.BlockSpec((grp * CHUNK, MLSTM_W), lambda i: (i, 0)),
                   pl.BlockSpec((grp, HEADS, DQK, DV), lambda i: (i, 0, 0, 0)),
                   pl.BlockSpec((grp, HEADS, 1, DQK), lambda i: (i, 0, 0, 0)),
                   pl.BlockSpec((grp, HEADS, 1, 1), lambda i: (i, 0, 0, 0))],
        out_shape=[jax.ShapeDtypeStruct((r, MLSTM_W), BF16), jax.ShapeDtypeStruct((r, MLSTM_W), F32),
                   jax.ShapeDtypeStruct((nc, HEADS, DQK, DV), F32),
                   jax.ShapeDtypeStruct((nc, HEADS, 1, DQK), F32),
                   jax.ShapeDtypeStruct((nc, HEADS, 1, 1), F32)],
        scratch_shapes=[pltpu.VMEM((HEADS, DQK, DV), F32), pltpu.VMEM((HEADS, 1, DQK), F32),
                        pltpu.VMEM((HEADS, 1, 1), F32)],
        compiler_params=_params(),
    )(pm, bias, nw)


def _mlstm_bwd(name, dcat, pm, ht, cs, ns, ms, bias, nw):
    r = pm.shape[0]
    nc = r // CHUNK
    grp = _chunks_per_step(nc)
    nsteps = nc // grp

    def body(dy_ref, pm_ref, ht_ref, cs_ref, ns_ref, ms_ref, b_ref, nw_ref, dpm_ref, dnw_ref, db_ref,
             dc_scr, dn_scr):
        step = pl.program_id(0)

        @pl.when(step == 0)
        def _():
            dc_scr[...] = jnp.zeros_like(dc_scr)
            dn_scr[...] = jnp.zeros_like(dn_scr)
            dnw_ref[...] = jnp.zeros_like(dnw_ref)
            db_ref[...] = jnp.zeros_like(db_ref)

        rr = lax.broadcasted_iota(jnp.int32, (CHUNK, CHUNK), 0)
        cc = lax.broadcasted_iota(jnp.int32, (CHUNK, CHUNK), 1)
        tril, eye, triu = cc <= rr, cc == rr, cc >= rr
        lane = lax.broadcasted_iota(jnp.int32, (CHUNK, 128), 1)
        rowid = lax.broadcasted_iota(jnp.int32, (CHUNK, 1), 0)
        bv, nwv = b_ref[...], nw_ref[...]
        carried = [(dc_scr[h], dn_scr[h]) for h in range(HEADS)]
        dnw_acc = [jnp.zeros((1, DV), F32) for _ in range(HEADS)]
        db_acc = jnp.zeros((1, 128), F32)
        for g in reversed(range(grp)):
            rows = slice(g * CHUNK, (g + 1) * CHUNK)
            ci = (nsteps - 1 - step) * grp + g
            pmv = pm_ref[rows, :]
            th, z, li, lf, real = _gate_tiles(pmv[:, GATE_COL:GATE_COL + 128], bv, ci * CHUNK)
            heads = _interleave([
                _mlstm_bwd_head(h, pmv, ht_ref[rows, h * DV:(h + 1) * DV], dy_ref[rows, h * DV:(h + 1) * DV], nwv,
                                li, lf, cs_ref[g, h], ns_ref[g, h], ms_ref[g, h], carried[h][0], carried[h][1],
                                tril, eye, triu, lane, rowid, dpm_ref, rows)
                for h in range(HEADS)])
            carried = [(dc_new, dn_new) for _, dc_new, dn_new, _ in heads]
            dgt = heads[0][0] + heads[1][0] + heads[2][0] + heads[3][0]
            dnw_acc = [dnw_acc[h] + heads[h][3] for h in range(HEADS)]
            dact = jnp.where(lane < HEADS, 1.0, 1.0 - _sigmoid(z)) * (1.0 - th * th)
            dgraw = jnp.where(real & (lane < 2 * HEADS), dgt * dact, 0.0)
            dpm_ref[rows, GATE_COL:GATE_COL + 128] = dgraw.astype(BF16)
            db_acc = db_acc + jnp.sum(dgraw, axis=0, keepdims=True)
        for h, (dcn, dnn) in enumerate(carried):
            dc_scr[h] = dcn
            dn_scr[h] = dnn
            dnw_ref[:, h * DV:(h + 1) * DV] += dnw_acc[h]
        db_ref[...] += db_acc

    rev = lambda i: (nsteps - 1 - i, 0)
    rev4 = lambda i: (nsteps - 1 - i, 0, 0, 0)
    return pl.pallas_call(
        body, name=name, grid=(nsteps,),
        in_specs=[pl.BlockSpec((grp * CHUNK, MLSTM_W), rev), pl.BlockSpec((grp * CHUNK, PM_W), rev),
                  pl.BlockSpec((grp * CHUNK, MLSTM_W), rev),
                  pl.BlockSpec((grp, HEADS, DQK, DV), rev4), pl.BlockSpec((grp, HEADS, 1, DQK), rev4),
                  pl.BlockSpec((grp, HEADS, 1, 1), rev4),
                  pl.BlockSpec((1, 128), lambda i: (0, 0)), pl.BlockSpec((1, MLSTM_W), lambda i: (0, 0))],
        out_specs=[pl.BlockSpec((grp * CHUNK, PM_W), rev), pl.BlockSpec((1, MLSTM_W), lambda i: (0, 0)),
                   pl.BlockSpec((1, 128), lambda i: (0, 0))],
        out_shape=[jax.ShapeDtypeStruct((r, PM_W), BF16), jax.ShapeDtypeStruct((1, MLSTM_W), F32),
                   jax.ShapeDtypeStruct((1, 128), F32)],
        scratch_shapes=[pltpu.VMEM((HEADS, DQK, DV), F32), pltpu.VMEM((HEADS, 1, DQK), F32)],
        compiler_params=_params(),
    )(dcat, pm, ht, cs, ns, ms, bias, nw)


def _mlstm_bwd_head(h, pmv, hh, y, nwv, li, lf, cst, nst, mst, dcn, dnn, tril, eye, triu, lane, rowid,
                    dpm_ref, rows):
    f = yield from _chunk_common(pmv, h, li, lf, cst, nst, mst, tril, eye)
    q, k, v, s, p = f["q"], f["k"], f["v"], f["s"], f["p"]
    w_inter, w_in, w_old, dn = f["w_inter"], f["w_in"], f["w_old"], f["dn"]
    osl = slice(2 * QK_W + MLSTM_W + h * DV, 2 * QK_W + MLSTM_W + (h + 1) * DV)
    sg = _sigmoid(pmv[:, osl])
    yield
    rs = lax.rsqrt(jnp.mean(hh * hh, axis=1, keepdims=True) + EPS)
    yield
    nwh = nwv[:, h * DV:(h + 1) * DV]
    dpm_ref[rows, osl] = (y * (hh * rs * nwh) * sg * (1.0 - sg)).astype(BF16)
    yield
    dhn = y * sg
    dnw_h = jnp.sum(dhn * hh * rs, axis=0, keepdims=True)
    yield
    wd = dhn * nwh
    dhh = rs * wd - hh * (rs * rs * rs) * jnp.mean(hh * wd, axis=1, keepdims=True)
    yield
    dnum = dhh / dn
    dd = -jnp.sum(dhh * hh, axis=1, keepdims=True) / dn
    yield
    dden = jnp.where(jnp.abs(f["den"]) > jnp.exp(-f["mt"]), dd * jnp.sign(f["den"]), 0.0)
    dnum_b = dnum.astype(BF16)
    wdn = (w_inter * dnum).astype(BF16)
    wid = (w_inter * dden).astype(BF16).astype(F32)
    yield
    ds = _dot(dnum_b, v, NT) + dden
    yield
    dsp = (ds * p).astype(BF16)
    yield
    dq = _dot(dsp, k, NN) + _dot(wdn, f["cb"], NT) + wid * f["nb"]
    yield
    dk = _dot(dsp, q, TN)
    yield
    dv = _dot(s.astype(BF16), dnum_b, TN)
    yield
    g = ds * s
    g_col = _row_to_col(jnp.sum(g, axis=0, keepdims=True), eye)
    yield
    db = jnp.sum(g, axis=1, keepdims=True) - g_col
    dli = g_col
    yield
    db = db + (jnp.sum(dnum * f["qc"], axis=1, keepdims=True) + dden * f["qn"]) * w_inter
    yield
    dcnb = dcn.astype(BF16)
    dnnb = dnn.astype(BF16).astype(F32)
    dkw = _dot(v, dcnb, NT) + dnnb
    yield
    dk = dk + w_in * dkw
    dv = dv + _dot(f["kw"], dcnb, NN)
    yield
    ddecay = jnp.sum(dkw * f["kraw"], axis=1, keepdims=True) * w_in
    yield
    dw_old = (jnp.sum(jnp.sum(dcn * cst, axis=1, keepdims=True), axis=0, keepdims=True)
              + jnp.sum(dnn * nst, axis=1, keepdims=True))
    yield
    db_end = dw_old * w_old + jnp.sum(ddecay, axis=0, keepdims=True)
    db = db - ddecay + jnp.where(rowid == CHUNK - 1, db_end, 0.0)
    dli = dli + ddecay
    yield
    dc_new = w_old * dcn + _dot(q, wdn, TN)
    yield
    dn_new = w_old * dnn + jnp.sum(wid * q.astype(F32), axis=0, keepdims=True)
    yield
    dlf = jnp.sum(jnp.where(triu, _col_to_row(db, eye), 0.0), axis=1, keepdims=True)
    yield
    gate_part = jnp.where(lane == h, dli, 0.0) + jnp.where(lane == HEADS + h, dlf, 0.0)
    dpm_ref[rows, h * DQK:(h + 1) * DQK] = (dq * QSCALE).astype(BF16)
    yield
    dpm_ref[rows, QK_W + h * DQK:QK_W + (h + 1) * DQK] = dk.astype(BF16)
    yield
    dpm_ref[rows, 2 * QK_W + h * DV:2 * QK_W + (h + 1) * DV] = dv.astype(BF16)
    return gate_part, dc_new, dn_new, dnw_h


def _my_place():
    return lax.axis_index("x"), lax.axis_index("y"), lax.axis_index("c")


def _flip(v, bit):
    return 1 - v if bit else v


def _exchange_small(name, blk, reduce):
    r, c = blk.shape

    def body(x_ref, o_ref, *rest):
        slots = rest[0] if reduce else o_ref
        send_sems, recv_sems = rest[-2], rest[-1]
        x, y, cc = _my_place()
        me = 4 * x + 2 * y + cc
        slots[me] = x_ref[...]
        copies = []
        for k in range(1, N_DEV):
            peer = (_flip(x, k & 4), _flip(y, k & 2), _flip(cc, k & 1))
            cp = pltpu.make_async_remote_copy(
                src_ref=x_ref, dst_ref=slots.at[me], send_sem=send_sems.at[k - 1],
                recv_sem=recv_sems.at[k - 1], device_id=peer, device_id_type=MESH)
            cp.start()
            copies.append(cp)
        for cp in copies:
            cp.wait()
        if reduce:
            acc = slots[0]
            for d in range(1, N_DEV):
                acc = acc + slots[d]
            o_ref[...] = acc

    scratch = ([pltpu.VMEM((N_DEV, r, c), F32)] if reduce else []) + [
        pltpu.SemaphoreType.DMA((N_DEV - 1,)), pltpu.SemaphoreType.DMA((N_DEV - 1,))]
    return pl.pallas_call(
        body, name=name,
        out_shape=jax.ShapeDtypeStruct((r, c) if reduce else (N_DEV, r, c), F32),
        in_specs=[pl.BlockSpec(memory_space=pltpu.VMEM)], out_specs=pl.BlockSpec(memory_space=pltpu.VMEM),
        scratch_shapes=scratch, compiler_params=_params(),
    )(blk)


HBM_SPEC = pl.BlockSpec(memory_space=pltpu.HBM)
SEM_SPEC = pl.BlockSpec(memory_space=pltpu.SEMAPHORE)
ANY_SPEC = pl.BlockSpec(memory_space=pl.ANY)
DATAFLOW = pltpu.SideEffectType.DATAFLOW_SIDE_EFFECTING


def _split_copy(name, arrays, start=None, wait=None, after=None):
    results, token = _split_copies(name, [(arrays, start, wait)], after)
    return results[0][0], results[0][1], token


def _split_copies(name, jobs, after=None):
    operands, in_specs, out_shape, out_specs, aliases = [], [], [], [], {}
    in_at, out_at = [], []
    for arrays, start, wait in jobs:
        in_at.append(len(operands))
        operands += [pltpu.with_memory_space_constraint(a, pltpu.HBM) for a in arrays]
        in_specs += [HBM_SPEC] * len(arrays)
        if wait:
            operands += list(wait[1])
            in_specs += [SEM_SPEC, SEM_SPEC]
    if after is not None:
        operands.append(after)
        in_specs.append(ANY_SPEC)
    for j, (arrays, start, wait) in enumerate(jobs):
        out_at.append(len(out_shape))
        if start:
            out_shape += [pltpu.SemaphoreType.DMA((start[1],)), pltpu.SemaphoreType.DMA((start[1],))]
            out_specs += [SEM_SPEC, SEM_SPEC]
        for i, a in enumerate(arrays):
            aliases[in_at[j] + i] = len(out_shape)
            out_shape.append(pltpu.HBM(a.shape, a.dtype))
            out_specs.append(HBM_SPEC)
    any_start = any(start for _, start, _ in jobs)
    if any_start:
        out_shape.append(jax.ShapeDtypeStruct((8, 128), F32))
        out_specs.append(pl.BlockSpec(memory_space=pltpu.VMEM))
    n_in = len(operands)

    def body(*refs):
        for j, (arrays, start, wait) in enumerate(jobs):
            if wait:
                ins = refs[in_at[j]:in_at[j] + len(arrays)]
                at = in_at[j] + len(arrays)
                for cp in wait[0](ins, refs[at], refs[at + 1]):
                    cp.wait_send()
                    cp.wait_recv()
        for j, (arrays, start, wait) in enumerate(jobs):
            if start:
                ins = refs[in_at[j]:in_at[j] + len(arrays)]
                at = n_in + out_at[j]
                for cp in start[0](ins, refs[at], refs[at + 1]):
                    cp.start()
        if any_start:
            token = refs[n_in + len(out_shape) - 1]
            token[...] = jnp.zeros_like(token)

    outs = pl.pallas_call(
        body, name=name, in_specs=in_specs, out_specs=out_specs, out_shape=out_shape,
        input_output_aliases=aliases, compiler_params=pltpu.CompilerParams(has_side_effects=DATAFLOW),
    )(*operands)
    results = []
    for j, (arrays, start, wait) in enumerate(jobs):
        at = out_at[j]
        sems = (outs[at], outs[at + 1]) if start else None
        at += 2 if start else 0
        results.append((list(outs[at:at + len(arrays)]), sems))
    return results, (outs[-1] if any_start else None)


def _remote(src, dst, send_sems, recv_sems, k, to):
    return pltpu.make_async_remote_copy(src_ref=src, dst_ref=dst, send_sem=send_sems.at[k],
                                        recv_sem=recv_sems.at[k], device_id=to, device_id_type=MESH)


def _slot(px, py, pc):
    return 4 * px + 2 * py + pc


def _gather_first(refs, send_sems, recv_sems):
    x, y, c = _my_place()
    blk = refs[0].at[_slot(x, y, c)]
    targets = [(x, y, 1 - c), (1 - x, y, c), (x, 1 - y, c)]
    return [_remote(blk, blk, send_sems, recv_sems, k, to) for k, to in enumerate(targets)]


def _gather_relay(refs, send_sems, recv_sems):
    x, y, c = _my_place()
    rows = refs[0].shape[1]
    half = rows // 32 * 16
    from_x, from_y = _slot(1 - x, y, c), _slot(x, 1 - y, c)
    upper = refs[0].at[from_x, pl.ds(0, half)]
    lower = refs[0].at[from_y, pl.ds(half, rows - half)]
    return [_remote(upper, upper, send_sems, recv_sems, 0, (x, 1 - y, c)),
            _remote(lower, lower, send_sems, recv_sems, 1, (1 - x, y, c)),
            _remote(refs[0].at[from_x], refs[0].at[from_x], send_sems, recv_sems, 2, (x, y, 1 - c)),
            _remote(refs[0].at[from_y], refs[0].at[from_y], send_sems, recv_sems, 3, (x, y, 1 - c))]


def _gather_last(refs, send_sems, recv_sems):
    x, y, c = _my_place()
    blk = refs[0].at[_slot(1 - x, 1 - y, c)]
    return [_remote(blk, blk, send_sems, recv_sems, 0, (x, y, 1 - c))]


def _scatter_sibling(n):
    def copies(refs, send_sems, recv_sems):
        x, y, c = _my_place()
        return [_remote(refs[a].at[2 * j + 1 - c], refs[n + a].at[j], send_sems, recv_sems, 4 * a + j, (x, y, 1 - c))
                for a in range(n) for j in range(4)]
    return copies


def _scatter_chips(n):
    def copies(refs, send_sems, recv_sems):
        x, y, c = _my_place()
        out = []
        for a in range(n):
            for k in range(1, 4):
                px, py = _flip(x, k & 2), _flip(y, k & 1)
                out.append(_remote(refs[a].at[2 * px + py], refs[n + a].at[2 * x + y], send_sems, recv_sems,
                                   3 * a + k - 1, (px, py, c)))
        return out
    return copies


def _pair_sum(name, core, g, t):
    _, r, c = g.shape
    tr = _tile(r, 512, 8)
    g4 = g.reshape(4, 2, r, c)

    def body(core_ref, g_ref, t_ref, o_ref):
        o_ref[...] = (g_ref[...].astype(F32) + t_ref[...].astype(F32)).astype(BF16)

    return pl.pallas_call(
        body, name=name,
        grid_spec=pltpu.PrefetchScalarGridSpec(
            num_scalar_prefetch=1, grid=(4, r // tr),
            in_specs=[pl.BlockSpec((None, None, tr, c), lambda j, i, core_ref: (j, core_ref[0], i, 0)),
                      pl.BlockSpec((None, tr, c), lambda j, i, core_ref: (j, i, 0))],
            out_specs=pl.BlockSpec((None, tr, c), lambda j, i, core_ref: (j, i, 0))),
        out_shape=jax.ShapeDtypeStruct((4, r, c), BF16), compiler_params=_params(),
    )(core, g4, t)


def _adam_math(w, g, m, v):
    m2 = ADAM_B1 * m + (1.0 - ADAM_B1) * g
    v2 = ADAM_B2 * v + (1.0 - ADAM_B2) * (g * g)
    m_hat = m2 / (1.0 - ADAM_B1 ** ADAM_STEP)
    v_hat = v2 / (1.0 - ADAM_B2 ** ADAM_STEP)
    delta = -ADAM_LR * (m_hat / (jnp.sqrt(v_hat) + ADAM_EPS) + ADAM_WD * w)
    return delta, m2, v2


def _adam_sharded(name, chip, w, m, v, grads, row_off=0):
    _, r, c = w.shape
    tr = _tile(r, 256, 8)
    tc = c if tr < r else _tile(c, 256, 128)
    boff = row_off // tr

    def body(chip_ref, w_ref, m_ref, v_ref, p0_ref, q0_ref, p1_ref, q1_ref, g_ref, d_ref, nm_ref, nv_ref):
        mine = chip_ref[0]

        def total(p_ref, q_ref):
            acc = None
            for j in range(4):
                part = jnp.where(mine == j, p_ref[...], q_ref[j]).astype(F32)
                acc = part if acc is None else acc + part
            return acc

        g = jnp.where(pl.program_id(0) == 0, total(p0_ref, q0_ref), total(p1_ref, q1_ref))
        delta, m2, v2 = _adam_math(w_ref[...], g, m_ref[...], v_ref[...])
        g_ref[...] = g
        d_ref[...] = delta
        nm_ref[...] = m2
        nv_ref[...] = v2

    def grad_specs(layer):
        at = lambda l, i, j: (jnp.where(l == layer, boff + i, boff), jnp.where(l == layer, j, 0))
        return [pl.BlockSpec((None, tr, tc), lambda l, i, j, chip_ref: (chip_ref[0],) + at(l, i, j)),
                pl.BlockSpec((4, tr, tc), lambda l, i, j, chip_ref: (0,) + at(l, i, j))]

    wspec = pl.BlockSpec((None, tr, tc), lambda l, i, j, chip_ref: (l, i, j))
    sds = jax.ShapeDtypeStruct(w.shape, F32)
    return pl.pallas_call(
        body, name=name,
        grid_spec=pltpu.PrefetchScalarGridSpec(
            num_scalar_prefetch=1, grid=(2, r // tr, c // tc),
            in_specs=[wspec, wspec, wspec] + grad_specs(0) + grad_specs(1), out_specs=[wspec] * 4),
        out_shape=[sds] * 4, compiler_params=_params(),
    )(chip, w, m, v, grads[0][0], grads[0][1], grads[1][0], grads[1][1])


def _adam_small(name, w, m, v, g):
    def body(w_ref, m_ref, v_ref, g_ref, d_ref, nm_ref, nv_ref):
        delta, m2, v2 = _adam_math(w_ref[...], g_ref[...], m_ref[...], v_ref[...])
        d_ref[...] = delta
        nm_ref[...] = m2
        nv_ref[...] = v2

    sds = jax.ShapeDtypeStruct(w.shape, F32)
    vm = pl.BlockSpec(memory_space=pltpu.VMEM)
    return pl.pallas_call(body, name=name, in_specs=[vm] * 4, out_specs=[vm] * 3, out_shape=[sds] * 3,
                          compiler_params=_params())(w, m, v, g)


GATE_END = GATE_COL + 2 * HEADS


def _merge_dw_in(dwm_t, dwc_t):
    full = jnp.concatenate([dwm_t[:GATE_END], dwc_t.reshape(3 * CONV_W, D_MODEL)], axis=0)
    return full.reshape(N_DEV, IN_SH, D_MODEL)


def _pack128(parts):
    flat = jnp.concatenate([p.reshape(-1) for p in parts])
    n = flat.shape[0]
    rows = -(-n // 1024) * 8
    return jnp.pad(flat, (0, rows * 128 - n)).reshape(rows, 128)


def _unpack128(packed, shapes):
    flat = packed.reshape(-1)
    out, at = [], 0
    for s in shapes:
        n = int(np.prod(s))
        out.append(flat[at:at + n].reshape(s))
        at += n
    return out


def kernel(x, meta_tokens, norm_mix_w, w_in, b_gates, conv_w, mlstm_norm_w, w_out, norm_ffn_w, w_gate, w_up, w_down, norm_final_w, loss_target, m_meta_tokens, m_norm_mix_w, m_w_in, m_b_gates, m_conv_w, m_mlstm_norm_w, m_w_out, m_norm_ffn_w, m_w_gate, m_w_up, m_w_down, m_norm_final_w, v_meta_tokens, v_norm_mix_w, v_w_in, v_b_gates, v_conv_w, v_mlstm_norm_w, v_w_out, v_norm_ffn_w, v_w_gate, v_w_up, v_w_down, v_norm_final_w):
    seq = x.shape[1]
    rows = TOK0 + seq
    me = 4 * lax.axis_index("x") + 2 * lax.axis_index("y") + lax.axis_index("c")
    meta_sh = meta_tokens.shape[1]
    conv_sh = conv_w.shape[2]

    w_gate_t, m_w_gate_t, v_w_gate_t = (jnp.transpose(a, (0, 2, 1)) for a in (w_gate, m_w_gate, v_w_gate))
    w_up_t, m_w_up_t, v_w_up_t = (jnp.transpose(a, (0, 2, 1)) for a in (w_up, m_w_up, v_w_up))
    shards = []
    for l in range(DEPTH):
        shards += [jnp.transpose(w_in[l]).astype(BF16), w_out[l].astype(BF16), w_gate_t[l].astype(BF16),
                   w_up_t[l].astype(BF16), w_down[l].astype(BF16)]
    per_layer = ("w_in", "w_out", "w_gate", "w_up", "w_down")
    gather_names = [f"{nm}_{l}" for l in range(DEPTH) for nm in per_layer]
    gather_state = {}

    def gather_step(tag, after, start=None, relay=None, last=None, done=()):
        jobs, idx = [], []
        if start is not None and start < len(shards):
            buf = lax.dynamic_update_index_in_dim(lax.empty((N_DEV,) + shards[start].shape, BF16), shards[start], me, 0)
            jobs.append(([buf], (_gather_first, 3), None))
            idx.append(start)
        if relay is not None and relay < len(shards):
            jobs.append((gather_state[relay][0], (_gather_relay, 4), (_gather_first, gather_state[relay][1])))
            idx.append(relay)
        if last is not None:
            jobs.append((gather_state[last][0], (_gather_last, 1), (_gather_relay, gather_state[last][1])))
            idx.append(last)
        for i in done:
            jobs.append((gather_state[i][0], None, (_gather_last, gather_state[i][1])))
            idx.append(i)
        if not jobs:
            return after, []
        results, tok = _split_copies(f"gather_{tag}", jobs, after)
        for i, res in zip(idx, results):
            gather_state[i] = res
        return (after if tok is None else tok), [gather_state[i][0][0] for i in done]

    bias = [jnp.pad(b_gates[l].reshape(1, 2 * HEADS), ((0, 0), (0, 128 - 2 * HEADS))) for l in range(DEPTH)]
    nmix = [norm_mix_w[l].reshape(1, D_MODEL) for l in range(DEPTH)]
    nffn = [norm_ffn_w[l].reshape(1, D_MODEL) for l in range(DEPTH)]
    nmls = [mlstm_norm_w[l].reshape(1, MLSTM_W) for l in range(DEPTH)]
    weights = [dict() for _ in range(DEPTH)]
    saved = [dict() for _ in range(DEPTH)]

    def layer_fwd(l, h, after):
        w, s = weights[l], saved[l]
        k0 = len(per_layer) * l
        tok, _ = gather_step(f"l{l}_a", after, last=k0)
        _, (g_in,) = gather_step(f"l{l}_b", tok, done=[k0])
        tok, _ = gather_step(f"l{l}_c", g_in, relay=k0 + 1, start=k0 + 3)
        w["win_t"] = g_in.reshape(D_IN, D_MODEL)
        w["wc_t"] = w["win_t"][GATE_END:].reshape(3, CONV_W, D_MODEL)
        s["h0"] = h
        s["hn"] = _rms_fwd(f"norm_mix_{l}", h, nmix[l] + tok[0, 0])
        s["pm"] = _mm_nt(f"proj_mlstm_{l}", s["hn"], w["win_t"], F32, tn=640, tk=D_MODEL, n=PM_W)
        tok, _ = gather_step(f"l{l}_d", s["pm"], relay=k0 + 2, start=k0 + 4)
        tok, _ = gather_step(f"l{l}_d2", tok, last=k0 + 1)
        s["pc"] = _mm_nt_bcols(f"proj_conv_{l}", s["hn"], w["wc_t"], F32, dep=tok)
        hm, s["ht"], s["cs"], s["ns"], s["ms"] = _mlstm_fwd(f"mlstm_fwd_{l}", s["pm"], bias[l] + tok[:1], nmls[l])
        tok, _ = gather_step(f"l{l}_e", hm, relay=k0 + 3, start=k0 + 5)
        tok, _ = gather_step(f"l{l}_e2", tok, last=k0 + 2)
        hc = _conv_fwd(f"conv_fwd_{l}", s["pc"], conv_rows[l] + tok[0, 0])
        s["cat"] = jnp.concatenate([hm, hc], axis=1)
        _, (g_out,) = gather_step(f"l{l}_f", s["cat"], done=[k0 + 1])
        w["wo"] = g_out.reshape(D_MODEL, D_MODEL)
        s["h1"] = _mm_nn(f"out_proj_{l}", s["cat"], w["wo"], F32, res=s["h0"])
        tok, _ = gather_step(f"l{l}_g", s["h1"], relay=k0 + 4, start=k0 + 6)
        s["hf"] = _rms_fwd(f"norm_ffn_{l}", s["h1"], nffn[l] + tok[0, 0])
        tok, _ = gather_step(f"l{l}_h", s["hf"], last=k0 + 3)
        _, (g_gate, g_up) = gather_step(f"l{l}_i", tok, done=[k0 + 2, k0 + 3])
        w["wg_t"] = g_gate.reshape(D_FF, D_MODEL)
        w["wu_t"] = g_up.reshape(D_FF, D_MODEL)
        s["g"], s["u"], s["act"] = _ffn_in(f"ffn_in_{l}", s["hf"], w["wg_t"], w["wu_t"])
        tok, _ = gather_step(f"l{l}_j", s["act"], last=k0 + 4)
        _, (g_down,) = gather_step(f"l{l}_k", tok, done=[k0 + 4])
        w["wd"] = g_down.reshape(D_FF, D_MODEL)
        tok, _ = gather_step(f"l{l}_k2", tok, relay=k0 + 5, start=k0 + 7)
        return _mm_nn(f"ffn_out_{l}", s["act"], w["wd"], F32, res=s["h1"], dep=tok)

    tok, _ = gather_step("first", None, start=0)
    zero = tok[0, 0]
    small = jnp.concatenate(
        [meta_tokens + zero, jnp.pad(conv_w.reshape(DEPTH * 3, conv_sh), ((0, 2), (0, meta_sh - conv_sh)))], axis=0)
    slots = _exchange_small("gather_small", small, reduce=False)
    meta_full = jnp.transpose(slots[:, :N_META, :], (1, 0, 2)).reshape(N_META, D_MODEL)
    conv_full = jnp.transpose(slots[:, N_META:N_META + DEPTH * 3, :conv_sh], (1, 0, 2)).reshape(DEPTH, 3, CONV_W)
    conv_rows = [jnp.pad(conv_full[l], ((0, 5), (0, 0))) for l in range(DEPTH)]
    w_in_t, m_w_in_t, v_w_in_t = (jnp.transpose(a + zero, (0, 2, 1)) for a in (w_in, m_w_in, v_w_in))
    tok, w_in_t, m_w_in_t, v_w_in_t, meta_full = lax.optimization_barrier(
        (tok, w_in_t, m_w_in_t, v_w_in_t, meta_full))
    tok, _ = gather_step("pre_a", tok, relay=0)
    tok, _ = gather_step("pre_b", tok, start=1)
    tok, _ = gather_step("pre_c", tok, start=2)
    h = jnp.concatenate([jnp.zeros((PAD_FRONT, D_MODEL), F32), meta_full, x[0]], axis=0)
    h = layer_fwd(0, h, tok)
    h = layer_fwd(1, h, h)

    dh, dh_b, d_final, loss_part = _final_loss("final_loss", h, norm_final_w.reshape(1, D_MODEL), loss_target[0])

    core = lax.axis_index("c").astype(jnp.int32).reshape(1)
    chip = (2 * lax.axis_index("x") + lax.axis_index("y")).astype(jnp.int32).reshape(1)
    scatter_state = {}

    def scatter_begin(nm, grad):
        land = lax.empty((4,) + grad.shape[1:], BF16)
        arrs, sems, tok = _split_copy(f"grad_sibling_start_{nm}", [grad, land], start=(_scatter_sibling(1), 4))
        scatter_state[nm] = (arrs, sems)
        return tok

    def scatter_advance(nm, after):
        arrs, sems = scatter_state[nm]
        arrs, _, _ = _split_copy(f"grad_sibling_done_{nm}", arrs, wait=(_scatter_sibling(1), sems), after=after)
        part = _pair_sum(f"grad_pair_sum_{nm}", core, arrs[0], arrs[1])
        arrs, sems, tok = _split_copy(f"grad_chips_start_{nm}", [part, lax.empty(part.shape, BF16)],
                                      start=(_scatter_chips(1), 3))
        scatter_state[nm] = (arrs, sems)
        return tok

    def scattered(nm, after):
        arrs, sems = scatter_state[nm]
        arrs, _, _ = _split_copy(f"grad_chips_done_{nm}", arrs, wait=(_scatter_chips(1), sems), after=after)
        return arrs[0], arrs[1]

    d_mix, d_ffn, d_mls, d_bias, d_conv = ([None] * DEPTH for _ in range(5))

    def layer_bwd(l, dh, dh_b, tok):
        w, s = weights[l], saved[l]
        dg, du = _ffn_act_bwd(f"d_act_{l}", dh_b, w["wd"], s["g"], s["u"], dep=tok)
        dw_down = _mm_tn(f"dw_down_{l}", s["act"], dh_b, BF16, tm=1408, tn=1024)
        tok = scatter_begin(f"w_down_{l}", dw_down.reshape(N_DEV, FF_SH, D_MODEL))
        dhf = _mm_nn(f"d_ffn_gate_{l}", dg, w["wg_t"], F32, dep=tok)
        tok = scatter_advance(f"w_down_{l}", after=dhf)
        dhf = _mm_nn(f"d_ffn_up_{l}", du, w["wu_t"], F32, res=dhf, dep=tok)
        dw_gate = _mm_tn(f"dw_gate_{l}", dg, s["hf"], BF16, tm=1408, tn=1024)
        tok = scatter_begin(f"w_gate_{l}", dw_gate.reshape(N_DEV, FF_SH, D_MODEL))
        dw_up = _mm_tn(f"dw_up_{l}", du, s["hf"], BF16, tm=1408, tn=1024, dep=tok)
        tok = scatter_begin(f"w_up_{l}", dw_up.reshape(N_DEV, FF_SH, D_MODEL))
        dh1, dh1_b, d_ffn[l] = _rms_bwd(f"norm_ffn_bwd_{l}", s["h1"], nffn[l] + tok[0, 0], dhf, dh)
        tok = scatter_advance(f"w_gate_{l}", after=dh1)
        dcat = _mm_nt(f"d_cat_{l}", dh1_b, w["wo"], F32, tk=D_MODEL, dep=tok)
        tok = scatter_advance(f"w_up_{l}", after=dcat)
        dw_out = _mm_tn(f"dw_out_{l}", s["cat"], dh1_b, BF16, tn=1024, dep=tok)
        tok = scatter_begin(f"w_out_{l}", dw_out.reshape(N_DEV, OUT_SH, D_MODEL))
        dpm, d_mls[l], d_bias[l] = _mlstm_bwd(f"mlstm_bwd_{l}", dcat, s["pm"], s["ht"], s["cs"], s["ns"],
                                               s["ms"], bias[l] + tok[:1], nmls[l])
        dpc, d_conv[l] = _conv_bwd(f"conv_bwd_{l}", dcat, s["pc"], conv_rows[l])
        tok = scatter_advance(f"w_out_{l}", after=dpc)
        dwm_t = _mm_tn(f"dw_mlstm_{l}", dpm, s["hn"], BF16, tm=640, tn=1024, dep=tok)
        dwc_t = _mm_tn_acols(f"dw_conv_{l}", dpc, s["hn"], BF16)
        tok = scatter_begin(f"w_in_{l}", _merge_dw_in(dwm_t, dwc_t))
        dhn = _mm_nn_kt(f"d_norm_mlstm_{l}", dpm, w["win_t"], F32, tk=PM_W, dep=tok)
        dhn = _mm_nn_ksum(f"d_norm_conv_{l}", dpc, w["wc_t"], F32, res=dhn)
        tok = scatter_advance(f"w_in_{l}", after=dhn)
        dh, dh_b, d_mix[l] = _rms_bwd(f"norm_mix_bwd_{l}", s["h0"], nmix[l] + tok[0, 0], dhn, dh1)
        return dh, dh_b, tok

    dh, dh_b, tok = layer_bwd(1, dh, dh_b, None)
    dh, dh_b, tok_tail = layer_bwd(0, dh, dh_b, tok)

    pq = {}
    after = dh
    for l in reversed(range(DEPTH)):
        for nm in ("w_down", "w_gate", "w_up", "w_out", "w_in"):
            if (nm, l) != ("w_in", 0):
                pq[nm, l] = scattered(f"{nm}_{l}", after)
                after = pq[nm, l][0]
    untransposed = lambda outs: [jnp.transpose(o, (0, 2, 1)) for o in outs]
    g_out, d_out, nm_out, nv_out = _adam_sharded(
        "adam_w_out", chip, w_out, m_w_out, v_w_out, [pq["w_out", 0], pq["w_out", 1]])
    g_gate, d_gate, nm_gate, nv_gate = untransposed(_adam_sharded(
        "adam_w_gate", chip, w_gate_t, m_w_gate_t, v_w_gate_t, [pq["w_gate", 0], pq["w_gate", 1]]))
    g_up, d_up, nm_up, nv_up = untransposed(_adam_sharded(
        "adam_w_up", chip, w_up_t, m_w_up_t, v_w_up_t, [pq["w_up", 0], pq["w_up", 1]]))
    g_down, d_down, nm_down, nv_down = _adam_sharded(
        "adam_w_down", chip, w_down, m_w_down, v_w_down, [pq["w_down", 0], pq["w_down", 1]])
    pq["w_in", 0] = scattered("w_in_0", nv_down)
    g_in, d_in, nm_in, nv_in = untransposed(_adam_sharded(
        "adam_w_in", chip, w_in_t, m_w_in_t, v_w_in_t, [pq["w_in", 0], pq["w_in", 1]]))

    bg = jnp.concatenate([d_bias[l][0, :2 * HEADS] for l in range(DEPTH)])
    red_in = jnp.concatenate([
        dh[PAD_FRONT:TOK0], d_mix[0], d_mix[1], d_ffn[0], d_ffn[1], d_final,
        jnp.concatenate([d_mls[0], d_mls[1]], axis=1),
        jnp.stack([d_conv[l][:3] for l in range(DEPTH)]).reshape(3, 2 * CONV_W),
        jnp.pad(bg, (0, D_MODEL - bg.shape[0])).reshape(1, D_MODEL),
        jnp.pad(loss_part[:, :1], ((0, 0), (0, D_MODEL - 1))),
        jnp.zeros((5, D_MODEL), F32) + tok_tail[0, 0]], axis=0)
    red = _exchange_small("reduce_small", red_in, reduce=True)
    loss = red[26, 0]
    g_meta = lax.dynamic_slice_in_dim(red[:N_META], me * meta_sh, meta_sh, axis=1)
    g_mix, g_ffn, g_final = red[16:18], red[18:20], red[20]
    g_mls = red[21].reshape(DEPTH, MLSTM_W)
    g_conv = lax.dynamic_slice_in_dim(red[22:25].reshape(DEPTH, 3, CONV_W), me * conv_sh, conv_sh, axis=2)
    g_bias = red[25, :DEPTH * 2 * HEADS].reshape(DEPTH, 2 * HEADS)

    small_w = [meta_tokens, norm_mix_w, b_gates, conv_w, mlstm_norm_w, norm_ffn_w, norm_final_w]
    small_m = [m_meta_tokens, m_norm_mix_w, m_b_gates, m_conv_w, m_mlstm_norm_w, m_norm_ffn_w, m_norm_final_w]
    small_v = [v_meta_tokens, v_norm_mix_w, v_b_gates, v_conv_w, v_mlstm_norm_w, v_norm_ffn_w, v_norm_final_w]
    small_g = [g_meta, g_mix, g_bias, g_conv, g_mls, g_ffn, g_final]
    shapes = [a.shape for a in small_w]
    packed = _adam_small("adam_small", _pack128(small_w), _pack128(small_m), _pack128(small_v), _pack128(small_g))
    (d_meta, d_nmix, d_bg, d_cw, d_nmls, d_nffn, d_nfin), (nm_meta, nm_nmix, nm_bg, nm_cw, nm_nmls, nm_nffn, nm_nfin), \
        (nv_meta, nv_nmix, nv_bg, nv_cw, nv_nmls, nv_nffn, nv_nfin) = (_unpack128(p, shapes) for p in packed)

    grad_x = dh[TOK0:].reshape(1, seq, D_MODEL)
    return (loss, grad_x,
            g_meta, g_mix, g_in, g_bias, g_conv, g_mls, g_out, g_ffn, g_gate, g_up, g_down, g_final,
            d_meta, d_nmix, d_in, d_bg, d_cw, d_nmls, d_out, d_nffn, d_gate, d_up, d_down, d_nfin,
            nm_meta, nm_nmix, nm_in, nm_bg, nm_cw, nm_nmls, nm_out, nm_nffn, nm_gate, nm_up, nm_down, nm_nfin,
            nv_meta, nv_nmix, nv_in, nv_bg, nv_cw, nv_nmls, nv_out, nv_nffn, nv_gate, nv_up, nv_down, nv_nfin)
```

```python
import functools

import numpy as np
import jax
import jax.numpy as jnp
from jax import lax
from jax.experimental import pallas as pl
from jax.experimental.pallas import tpu as pltpu

F32 = jnp.float32
BF16 = jnp.bfloat16
MESH = pl.DeviceIdType.MESH

D_MODEL = 2048
DEPTH = 2
N_META = 16
MLSTM_W = 1024
CONV_W = 1024
HEADS = 4
DV = 256
DQK = 128
QK_W = 512
CHUNK = 64
PAD_FRONT = 48
TOK0 = PAD_FRONT + N_META
D_FF = 5632
N_DEV = 8
FF_SH = D_FF // N_DEV
D_IN = 6152
IN_SH = D_IN // N_DEV
OUT_SH = D_MODEL // N_DEV
GATE_COL = 3072
PM_W = GATE_COL + 128
GATE_CAP = 15.0
EPS = 1e-6
QSCALE = DQK ** -0.5

ADAM_LR = 0.001
ADAM_B1 = 0.9
ADAM_B2 = 0.999
ADAM_EPS = 1e-08
ADAM_WD = 0.01
ADAM_STEP = 10

V7X_VMEM_LIMIT = 50 * 1024 * 1024
V7X_MXU_COLS = 256


def _params(**kw):
    return pltpu.CompilerParams(vmem_limit_bytes=V7X_VMEM_LIMIT, **kw)


def _tile(n, target, mult):
    best = None
    for t in range(mult, min(n, target) + 1, mult):
        if n % t == 0:
            best = t
    return best if best is not None else n


def _sigmoid(x):
    return 1.0 / (1.0 + jnp.exp(-x))


NN = ((1,), (0,))
NT = ((1,), (1,))
TN = ((0,), (0,))


def _matmul(name, a, b, out_shape, out_dtype, grid, a_bs, b_bs, o_bs, dims, nk, acc_shape=None,
            res=None, res_bs=None, dep=None):
    has_res = res is not None
    n_in = 2 + has_res + (dep is not None)

    def body(*refs):
        a_ref, b_ref = refs[0], refs[1]
        r_ref = refs[2] if has_res else None
        o_ref = refs[n_in]
        x = lax.dot_general(a_ref[...], b_ref[...], (dims, ((), ())), preferred_element_type=F32)
        if nk == 1:
            if has_res:
                x = x + r_ref[...]
            o_ref[...] = x.astype(o_ref.dtype)
            return
        acc = refs[n_in + 1]
        k = pl.program_id(len(grid) - 1)

        @pl.when(k == 0)
        def _():
            acc[...] = (x + r_ref[...]) if has_res else x

        @pl.when(k > 0)
        def _():
            acc[...] += x

        @pl.when(k == nk - 1)
        def _():
            o_ref[...] = acc[...].astype(o_ref.dtype)

    ins = [a, b] + ([res] if has_res else [])
    specs = [a_bs, b_bs] + ([res_bs] if has_res else [])
    if dep is not None:
        ins.append(dep)
        specs.append(pl.BlockSpec((8, 128), lambda *_: (0, 0)))
    scratch = [pltpu.VMEM(acc_shape, F32)] if nk > 1 else []
    return pl.pallas_call(
        body, name=name, grid=grid, in_specs=specs, out_specs=o_bs,
        out_shape=jax.ShapeDtypeStruct(out_shape, out_dtype), scratch_shapes=scratch,
        compiler_params=_params(),
    )(*ins)


def _mm_nn(name, a, b, out_dtype, res=None, tm=1056, tn=512, dep=None):
    r, k = a.shape
    n = b.shape[1]
    tm, tn = _tile(r, tm, 8), _tile(n, tn, 128)
    return _matmul(name, a, b, (r, n), out_dtype, (r // tm, n // tn, 1),
                   pl.BlockSpec((tm, k), lambda i, j, s: (i, 0)),
                   pl.BlockSpec((k, tn), lambda i, j, s: (0, j)),
                   pl.BlockSpec((tm, tn), lambda i, j, s: (i, j)), NN, 1,
                   res=res, res_bs=pl.BlockSpec((tm, tn), lambda i, j, s: (i, j)), dep=dep)


def _mm_nn_kt(name, a, b, out_dtype, tm=1056, tn=1024, tk=640, dep=None):
    r, k = a.shape
    n = b.shape[1]
    tm, tn, tk = _tile(r, tm, 8), _tile(n, tn, 128), _tile(k, tk, 128)
    nk = k // tk
    return _matmul(name, a, b, (r, n), out_dtype, (r // tm, n // tn, nk),
                   pl.BlockSpec((tm, tk), lambda i, j, s: (i, s)),
                   pl.BlockSpec((tk, tn), lambda i, j, s: (s, j)),
                   pl.BlockSpec((tm, tn), lambda i, j, s: (i, j)), NN, nk, acc_shape=(tm, tn), dep=dep)


def _mm_nn_ksum(name, a3, b3, out_dtype, res=None, tm=1056, tn=1024, dep=None):
    e, r, kb = a3.shape
    n = b3.shape[2]
    tm, tn = _tile(r, tm, 8), _tile(n, tn, 128)
    return _matmul(name, a3, b3, (r, n), out_dtype, (r // tm, n // tn, e),
                   pl.BlockSpec((None, tm, kb), lambda i, j, s: (s, i, 0)),
                   pl.BlockSpec((None, kb, tn), lambda i, j, s: (s, 0, j)),
                   pl.BlockSpec((tm, tn), lambda i, j, s: (i, j)), NN, e, acc_shape=(tm, tn),
                   res=res, res_bs=pl.BlockSpec((tm, tn), lambda i, j, s: (i, j)), dep=dep)


def _mm_nt(name, a, b, out_dtype, res=None, tm=1056, tn=512, tk=640, n=None, dep=None):
    r, k = a.shape
    n = b.shape[0] if n is None else n
    tm, tn, tk = _tile(r, tm, 8), _tile(n, tn, 128), _tile(k, tk, 128)
    nk = k // tk
    return _matmul(name, a, b, (r, n), out_dtype, (r // tm, n // tn, nk),
                   pl.BlockSpec((tm, tk), lambda i, j, s: (i, s)),
                   pl.BlockSpec((tn, tk), lambda i, j, s: (j, s)),
                   pl.BlockSpec((tm, tn), lambda i, j, s: (i, j)), NT, nk, acc_shape=(tm, tn),
                   res=res, res_bs=pl.BlockSpec((tm, tn), lambda i, j, s: (i, j)), dep=dep)


def _mm_nt_bcols(name, a, b3, out_dtype, tm=1056, dep=None):
    r, k = a.shape
    e, n, _ = b3.shape
    tm = _tile(r, tm, 8)
    return _matmul(name, a, b3, (e, r, n), out_dtype, (r // tm, e, 1),
                   pl.BlockSpec((tm, k), lambda i, g, s: (i, 0)),
                   pl.BlockSpec((None, n, k), lambda i, g, s: (g, 0, 0)),
                   pl.BlockSpec((None, tm, n), lambda i, g, s: (g, i, 0)), NT, 1, dep=dep)


def _mm_tn(name, a, b, out_dtype, tm=1024, tn=640, dep=None):
    r, m = a.shape
    n = b.shape[1]
    tm, tn = _tile(m, tm, 128), _tile(n, tn, 128)
    return _matmul(name, a, b, (m, n), out_dtype, (m // tm, n // tn, 1),
                   pl.BlockSpec((r, tm), lambda i, j, s: (0, i)),
                   pl.BlockSpec((r, tn), lambda i, j, s: (0, j)),
                   pl.BlockSpec((tm, tn), lambda i, j, s: (i, j)), TN, 1, dep=dep)


def _mm_tn_acols(name, a3, b, out_dtype, tn=1024, dep=None):
    e, r, m = a3.shape
    n = b.shape[1]
    tn = _tile(n, tn, 128)
    return _matmul(name, a3, b, (e, m, n), out_dtype, (n // tn, e, 1),
                   pl.BlockSpec((None, r, m), lambda j, g, s: (g, 0, 0)),
                   pl.BlockSpec((r, tn), lambda j, g, s: (0, j)),
                   pl.BlockSpec((None, m, tn), lambda j, g, s: (g, 0, j)), TN, 1, dep=dep)


def _rms_fwd(name, h, w):
    r, d = h.shape
    tr = _tile(r, 264, 8)

    def body(h_ref, w_ref, o_ref):
        x = h_ref[...]
        rs = lax.rsqrt(jnp.mean(x * x, axis=1, keepdims=True) + EPS)
        o_ref[...] = (x * rs * w_ref[...]).astype(BF16)

    return pl.pallas_call(
        body, name=name, grid=(r // tr,),
        in_specs=[pl.BlockSpec((tr, d), lambda i: (i, 0)), pl.BlockSpec((1, d), lambda i: (0, 0))],
        out_specs=pl.BlockSpec((tr, d), lambda i: (i, 0)),
        out_shape=jax.ShapeDtypeStruct((r, d), BF16), compiler_params=_params(),
    )(h, w)


def _rms_bwd(name, x, w, dy, dres):
    r, d = x.shape
    tr = _tile(r, 264, 8)

    def body(x_ref, w_ref, dy_ref, dr_ref, dx_ref, dxb_ref, dw_ref):
        xv = x_ref[...]
        g = dy_ref[...]
        rs = lax.rsqrt(jnp.mean(xv * xv, axis=1, keepdims=True) + EPS)
        wg = g * w_ref[...]
        dx = rs * wg - xv * (rs * rs * rs) * jnp.mean(xv * wg, axis=1, keepdims=True) + dr_ref[...]
        dx_ref[...] = dx
        dxb_ref[...] = dx.astype(BF16)
        part = jnp.sum(g * xv * rs, axis=0, keepdims=True)

        @pl.when(pl.program_id(0) == 0)
        def _():
            dw_ref[...] = part

        @pl.when(pl.program_id(0) > 0)
        def _():
            dw_ref[...] += part

    row = pl.BlockSpec((tr, d), lambda i: (i, 0))
    vec = pl.BlockSpec((1, d), lambda i: (0, 0))
    return pl.pallas_call(
        body, name=name, grid=(r // tr,), in_specs=[row, vec, row, row], out_specs=[row, row, vec],
        out_shape=[jax.ShapeDtypeStruct((r, d), F32), jax.ShapeDtypeStruct((r, d), BF16),
                   jax.ShapeDtypeStruct((1, d), F32)],
        compiler_params=_params(),
    )(x, w, dy, dres)


def _final_loss(name, h, w, target):
    r, d = h.shape
    nb = r // CHUNK

    def body(h_ref, w_ref, t_ref, dh_ref, dhb_ref, dw_ref, ls_ref):
        i = pl.program_id(0)

        @pl.when(i == 0)
        def _():
            dh_ref[...] = jnp.zeros_like(dh_ref)
            dhb_ref[...] = jnp.zeros_like(dhb_ref)
            dw_ref[...] = jnp.zeros_like(dw_ref)
            ls_ref[...] = jnp.zeros_like(ls_ref)

        @pl.when(i > 0)
        def _():
            xv = h_ref[...]
            wv = w_ref[...]
            rs = lax.rsqrt(jnp.mean(xv * xv, axis=1, keepdims=True) + EPS)
            err = xv * rs * wv - t_ref[...]
            sq = jnp.sum(jnp.sum(err * err, axis=1, keepdims=True), axis=0, keepdims=True)
            ls_ref[...] += jnp.broadcast_to(sq * (0.5 / d), ls_ref.shape)
            g = err * (1.0 / d)
            wg = g * wv
            dx = rs * wg - xv * (rs * rs * rs) * jnp.mean(xv * wg, axis=1, keepdims=True)
            dh_ref[...] = dx
            dhb_ref[...] = dx.astype(BF16)
            dw_ref[...] += jnp.sum(g * xv * rs, axis=0, keepdims=True)

    row = pl.BlockSpec((CHUNK, d), lambda i: (i, 0))
    vec = pl.BlockSpec((1, d), lambda i: (0, 0))
    return pl.pallas_call(
        body, name=name, grid=(nb,),
        in_specs=[row, vec, pl.BlockSpec((CHUNK, d), lambda i: (jnp.maximum(i - 1, 0), 0))],
        out_specs=[row, row, vec, pl.BlockSpec((1, 128), lambda i: (0, 0))],
        out_shape=[jax.ShapeDtypeStruct((r, d), F32), jax.ShapeDtypeStruct((r, d), BF16),
                   jax.ShapeDtypeStruct((1, d), F32), jax.ShapeDtypeStruct((1, 128), F32)],
        compiler_params=_params(),
    )(h, w, target)


def _ffn_in(name, hf, wg_t, wu_t, dep=None, tm=1056, tn=512):
    r, d = hf.shape
    f = wg_t.shape[0]
    tm, tn = _tile(r, tm, 8), _tile(f, tn, 128)

    def body(h_ref, wg_ref, wu_ref, *rest):
        g_ref, u_ref, a_ref = rest[-3:]
        x = h_ref[...]
        g = lax.dot_general(x, wg_ref[...], (NT, ((), ())), preferred_element_type=F32)
        u = lax.dot_general(x, wu_ref[...], (NT, ((), ())), preferred_element_type=F32)
        g_ref[...] = g.astype(BF16)
        u_ref[...] = u.astype(BF16)
        a_ref[...] = (g * _sigmoid(g) * u).astype(BF16)

    wspec = pl.BlockSpec((tn, d), lambda i, j: (j, 0))
    ospec = pl.BlockSpec((tm, tn), lambda i, j: (i, j))
    ins, specs = [hf, wg_t, wu_t], [pl.BlockSpec((tm, d), lambda i, j: (i, 0)), wspec, wspec]
    if dep is not None:
        ins.append(dep)
        specs.append(pl.BlockSpec((8, 128), lambda *_: (0, 0)))
    return pl.pallas_call(
        body, name=name, grid=(r // tm, f // tn), in_specs=specs, out_specs=[ospec] * 3,
        out_shape=[jax.ShapeDtypeStruct((r, f), BF16)] * 3, compiler_params=_params(),
    )(*ins)


def _ffn_act_bwd(name, dh, wd, g, u, dep=None, tm=1056, tn=512):
    r, d = dh.shape
    f = wd.shape[0]
    tm, tn = _tile(r, tm, 8), _tile(f, tn, 128)

    def body(dh_ref, wd_ref, g_ref, u_ref, *rest):
        dg_ref, du_ref = rest[-2:]
        tr = _tile(tm, 264, 8)
        for r0 in range(0, tm, tr):
            for c0 in range(0, tn, V7X_MXU_COLS):
                rows, cols = slice(r0, r0 + tr), slice(c0, c0 + V7X_MXU_COLS)
                da = lax.dot_general(dh_ref[rows, :], wd_ref[cols, :], (NT, ((), ())), preferred_element_type=F32)
                gv = g_ref[rows, cols].astype(F32)
                s = _sigmoid(gv)
                t = da * s
                du_ref[rows, cols] = (t * gv).astype(BF16)
                dg_ref[rows, cols] = (t * u_ref[rows, cols].astype(F32) * (1.0 + gv - gv * s)).astype(BF16)

    tile = pl.BlockSpec((tm, tn), lambda i, j: (i, j))
    ins = [dh, wd, g, u]
    specs = [pl.BlockSpec((tm, d), lambda i, j: (i, 0)), pl.BlockSpec((tn, d), lambda i, j: (j, 0)), tile, tile]
    if dep is not None:
        ins.append(dep)
        specs.append(pl.BlockSpec((8, 128), lambda *_: (0, 0)))
    return pl.pallas_call(
        body, name=name, grid=(r // tm, f // tn), in_specs=specs, out_specs=[tile] * 2,
        out_shape=[jax.ShapeDtypeStruct((r, f), BF16)] * 2, compiler_params=_params(),
    )(*ins)


def _shift_down(a, k):
    row = lax.broadcasted_iota(jnp.int32, a.shape, 0)
    return jnp.where(row >= k, pltpu.roll(a, k, 0), 0.0)


def _shift_up(a, k):
    n = a.shape[0]
    row = lax.broadcasted_iota(jnp.int32, a.shape, 0)
    return jnp.where(row < n - k, pltpu.roll(a, n - k, 0), 0.0)


def _conv_fwd(name, pc, cw):
    _, r, w = pc.shape

    def body(pc_ref, cw_ref, o_ref):
        a = pc_ref[2] * pc_ref[0]
        cwv = cw_ref[...]
        conv = _shift_down(a, 2) * cwv[0:1] + _shift_down(a, 1) * cwv[1:2] + a * cwv[2:3]
        o_ref[...] = (pc_ref[1] * conv).astype(BF16)

    return pl.pallas_call(
        body, name=name, grid=(w // 128,),
        in_specs=[pl.BlockSpec((3, r, 128), lambda j: (0, 0, j)), pl.BlockSpec((8, 128), lambda j: (0, j))],
        out_specs=pl.BlockSpec((r, 128), lambda j: (0, j)),
        out_shape=jax.ShapeDtypeStruct((r, w), BF16), compiler_params=_params(),
    )(pc, cw)


def _conv_bwd(name, dcat, pc, cw):
    _, r, w = pc.shape
    nblk = w // 128

    def body(dy_ref, pc_ref, cw_ref, dpc_ref, dcw_ref):
        u, gb, gc = pc_ref[0], pc_ref[1], pc_ref[2]
        cwv = cw_ref[...]
        dy = dy_ref[...]
        a = gc * u
        a1, a2 = _shift_down(a, 1), _shift_down(a, 2)
        conv = a2 * cwv[0:1] + a1 * cwv[1:2] + a * cwv[2:3]
        dconv = dy * gb
        da = dconv * cwv[2:3] + _shift_up(dconv, 1) * cwv[1:2] + _shift_up(dconv, 2) * cwv[0:1]
        dpc_ref[0] = (da * gc).astype(BF16)
        dpc_ref[1] = (dy * conv).astype(BF16)
        dpc_ref[2] = (da * u).astype(BF16)
        row = lax.broadcasted_iota(jnp.int32, (8, 128), 0)
        dw0 = jnp.sum(dconv * a2, axis=0, keepdims=True)
        dw1 = jnp.sum(dconv * a1, axis=0, keepdims=True)
        dw2 = jnp.sum(dconv * a, axis=0, keepdims=True)
        dcw_ref[...] = jnp.where(row == 0, dw0, jnp.where(row == 1, dw1, jnp.where(row == 2, dw2, 0.0)))

    return pl.pallas_call(
        body, name=name, grid=(nblk,),
        in_specs=[pl.BlockSpec((r, 128), lambda j: (0, nblk + j)),
                  pl.BlockSpec((3, r, 128), lambda j: (0, 0, j)), pl.BlockSpec((8, 128), lambda j: (0, j))],
        out_specs=[pl.BlockSpec((3, r, 128), lambda j: (0, 0, j)), pl.BlockSpec((8, 128), lambda j: (0, j))],
        out_shape=[jax.ShapeDtypeStruct((3, r, w), BF16), jax.ShapeDtypeStruct((8, w), F32)],
        compiler_params=_params(),
    )(dcat, pc, cw)


def _dot(a, b, dims):
    return lax.dot_general(a, b, (dims, ((), ())), preferred_element_type=F32)


def _col_to_row(xc, eye):
    return jnp.sum(jnp.where(eye, xc, 0.0), axis=0, keepdims=True)


def _row_to_col(xr, eye):
    return jnp.sum(jnp.where(eye, xr, 0.0), axis=1, keepdims=True)


def _gate_tiles(graw, bias, row0):
    th = jnp.tanh((graw + bias) / GATE_CAP)
    z = GATE_CAP * th
    row = lax.broadcasted_iota(jnp.int32, graw.shape, 0) + row0
    real = row >= PAD_FRONT
    li = jnp.where(real, z, -jnp.inf)
    lf = jnp.where(real, jnp.minimum(z, 0.0) - jnp.log(1.0 + jnp.exp(-jnp.abs(z))), 0.0)
    return th, z, li, lf, real


def _interleave(gens):
    results = [None] * len(gens)
    live = list(enumerate(gens))
    while live:
        still = []
        for i, gen in live:
            try:
                next(gen)
                still.append((i, gen))
            except StopIteration as stop:
                results[i] = stop.value
        live = still
    return results


def _chunk_common(pm, h, li, lf, cst, nst, mst, tril, eye):
    kraw = pm[:, QK_W + h * DQK:QK_W + (h + 1) * DQK]
    q = (pm[:, h * DQK:(h + 1) * DQK] * QSCALE).astype(BF16)
    yield
    k = kraw.astype(BF16)
    v = pm[:, 2 * QK_W + h * DV:2 * QK_W + (h + 1) * DV].astype(BF16)
    yield
    li_c = li[:, h:h + 1]
    lf_c = lf[:, HEADS + h:HEADS + h + 1]
    li_r = _col_to_row(li_c, eye)
    yield
    lf_r = _col_to_row(lf_c, eye)
    yield
    b_c = jnp.sum(jnp.where(tril, lf_r, 0.0), axis=1, keepdims=True)
    yield
    b_r = _col_to_row(b_c, eye)
    yield
    dmat = jnp.where(tril, b_c - b_r + li_r, -jnp.inf)
    inter = b_c + mst
    yield
    mt = jnp.maximum(inter, jnp.max(dmat, axis=1, keepdims=True))
    yield
    w_inter = jnp.exp(inter - mt)
    p = jnp.exp(dmat - mt)
    yield
    s = _dot(q, k, NT) * p
    yield
    cb = cst.astype(BF16)
    nb = nst.astype(BF16).astype(F32)
    qc = _dot(q, cb, NN)
    yield
    qn = jnp.sum(q.astype(F32) * nb, axis=1, keepdims=True)
    yield
    den = w_inter * qn + jnp.sum(s, axis=1, keepdims=True)
    yield
    dn = jnp.maximum(jnp.abs(den), jnp.exp(-mt))
    b_end = b_c[CHUNK - 1:CHUNK, :]
    decay = b_end - b_c + li_c
    yield
    m_new = jnp.maximum(b_end + mst, jnp.max(decay, axis=0, keepdims=True))
    yield
    w_old = jnp.exp(b_end + mst - m_new)
    w_in = jnp.exp(decay - m_new)
    kw = (w_in * kraw).astype(BF16)
    yield
    return dict(q=q, k=k, v=v, kraw=kraw, mt=mt, w_inter=w_inter, p=p, s=s, cb=cb, nb=nb, qc=qc, qn=qn,
                den=den, dn=dn, m_new=m_new, w_old=w_old, w_in=w_in, kw=kw)


def _chunks_per_step(nc):
    return 1


def _mlstm_fwd(name, pm, bias, nw):
    r = pm.shape[0]
    nc = r // CHUNK
    grp = _chunks_per_step(nc)

    def body(pm_ref, b_ref, nw_ref, hm_ref, ht_ref, cs_ref, ns_ref, ms_ref, c_scr, n_scr, m_scr):
        step = pl.program_id(0)

        @pl.when(step == 0)
        def _():
            c_scr[...] = jnp.zeros_like(c_scr)
            n_scr[...] = jnp.zeros_like(n_scr)
            m_scr[...] = jnp.zeros_like(m_scr)

        rr = lax.broadcasted_iota(jnp.int32, (CHUNK, CHUNK), 0)
        cc = lax.broadcasted_iota(jnp.int32, (CHUNK, CHUNK), 1)
        tril, eye = cc <= rr, cc == rr
        bv, nwv = b_ref[...], nw_ref[...]
        states = [(c_scr[h], n_scr[h], m_scr[h]) for h in range(HEADS)]
        for g in range(grp):
            rows = slice(g * CHUNK, (g + 1) * CHUNK)
            pmv = pm_ref[rows, :]
            _, _, li, lf, _ = _gate_tiles(pmv[:, GATE_COL:GATE_COL + 128], bv, (step * grp + g) * CHUNK)
            def head(h, cst, nst, mst, g=g, rows=rows, pmv=pmv, li=li, lf=lf):
                f = yield from _chunk_common(pmv, h, li, lf, cst, nst, mst, tril, eye)
                num = f["w_inter"] * f["qc"] + _dot(f["s"].astype(BF16), f["v"], NN)
                yield
                hh = num / f["dn"]
                yield
                c_new = f["w_old"] * cst + _dot(f["kw"], f["v"], TN)
                yield
                n_new = f["w_old"] * nst + jnp.sum(
                    f["w_in"].astype(BF16).astype(F32) * f["k"].astype(F32), axis=0, keepdims=True)
                yield
                sl = slice(h * DV, (h + 1) * DV)
                rs = lax.rsqrt(jnp.mean(hh * hh, axis=1, keepdims=True) + EPS)
                yield
                og = pmv[:, 2 * QK_W + MLSTM_W + h * DV:2 * QK_W + MLSTM_W + (h + 1) * DV]
                cs_ref[g, h] = cst
                ns_ref[g, h] = nst
                ms_ref[g, h] = mst
                ht_ref[rows, sl] = hh
                yield
                hm_ref[rows, sl] = (_sigmoid(og) * (hh * rs * nwv[:, sl])).astype(BF16)
                return c_new, n_new, f["m_new"]

            states = _interleave([head(h, *states[h]) for h in range(HEADS)])
        for h, (cst, nst, mst) in enumerate(states):
            c_scr[h] = cst
            n_scr[h] = nst
            m_scr[h] = mst

    return pl.pallas_call(
        body, name=name, grid=(nc // grp,),
        in_specs=[pl.BlockSpec((grp * CHUNK, PM_W), lambda i: (i, 0)), pl.BlockSpec((1, 128), lambda i: (0, 0)),
                  pl.BlockSpec((1, MLSTM_W), lambda i: (0, 0))],
        out_specs=[pl.BlockSpec((grp * CHUNK, MLSTM_W), lambda i: (i, 0)),
                   pl.BlockSpec((grp * CHUNK, MLSTM_W), lambda i: (i, 0)),
                   pl.BlockSpec((grp, HEADS, DQK, DV), lambda i: (i, 0, 0, 0)),
                   pl.BlockSpec((grp, HEADS, 1, DQK), lambda i: (i, 0, 0, 0)),
                   pl.BlockSpec((grp, HEADS, 1, 1), lambda i: (i, 0, 0, 0))],
        out_shape=[jax.ShapeDtypeStruct((r, MLSTM_W), BF16), jax.ShapeDtypeStruct((r, MLSTM_W), F32),
                   jax.ShapeDtypeStruct((nc, HEADS, DQK, DV), F32),
                   jax.ShapeDtypeStruct((nc, HEADS, 1, DQK), F32),
                   jax.ShapeDtypeStruct((nc, HEADS, 1, 1), F32)],
        scratch_shapes=[pltpu.VMEM((HEADS, DQK, DV), F32), pltpu.VMEM((HEADS, 1, DQK), F32),
                        pltpu.VMEM((HEADS, 1, 1), F32)],
        compiler_params=_params(),
    )(pm, bias, nw)


def _mlstm_bwd(name, dcat, pm, ht, cs, ns, ms, bias, nw):
    r = pm.shape[0]
    nc = r // CHUNK
    grp = _chunks_per_step(nc)
    nsteps = nc // grp

    def body(dy_ref, pm_ref, ht_ref, cs_ref, ns_ref, ms_ref, b_ref, nw_ref, dpm_ref, dnw_ref, db_ref,
             dc_scr, dn_scr):
        step = pl.program_id(0)

        @pl.when(step == 0)
        def _():
            dc_scr[...] = jnp.zeros_like(dc_scr)
            dn_scr[...] = jnp.zeros_like(dn_scr)
            dnw_ref[...] = jnp.zeros_like(dnw_ref)
            db_ref[...] = jnp.zeros_like(db_ref)

        rr = lax.broadcasted_iota(jnp.int32, (CHUNK, CHUNK), 0)
        cc = lax.broadcasted_iota(jnp.int32, (CHUNK, CHUNK), 1)
        tril, eye, triu = cc <= rr, cc == rr, cc >= rr
        lane = lax.broadcasted_iota(jnp.int32, (CHUNK, 128), 1)
        rowid = lax.broadcasted_iota(jnp.int32, (CHUNK, 1), 0)
        bv, nwv = b_ref[...], nw_ref[...]
        carried = [(dc_scr[h], dn_scr[h]) for h in range(HEADS)]
        dnw_acc = [jnp.zeros((1, DV), F32) for _ in range(HEADS)]
        db_acc = jnp.zeros((1, 128), F32)
        for g in reversed(range(grp)):
            rows = slice(g * CHUNK, (g + 1) * CHUNK)
            ci = (nsteps - 1 - step) * grp + g
            pmv = pm_ref[rows, :]
            th, z, li, lf, real = _gate_tiles(pmv[:, GATE_COL:GATE_COL + 128], bv, ci * CHUNK)
            heads = _interleave([
                _mlstm_bwd_head(h, pmv, ht_ref[rows, h * DV:(h + 1) * DV], dy_ref[rows, h * DV:(h + 1) * DV], nwv,
                                li, lf, cs_ref[g, h], ns_ref[g, h], ms_ref[g, h], carried[h][0], carried[h][1],
                                tril, eye, triu, lane, rowid, dpm_ref, rows)
                for h in range(HEADS)])
            carried = [(dc_new, dn_new) for _, dc_new, dn_new, _ in heads]
            dgt = heads[0][0] + heads[1][0] + heads[2][0] + heads[3][0]
            dnw_acc = [dnw_acc[h] + heads[h][3] for h in range(HEADS)]
            dact = jnp.where(lane < HEADS, 1.0, 1.0 - _sigmoid(z)) * (1.0 - th * th)
            dgraw = jnp.where(real & (lane < 2 * HEADS), dgt * dact, 0.0)
            dpm_ref[rows, GATE_COL:GATE_COL + 128] = dgraw.astype(BF16)
            db_acc = db_acc + jnp.sum(dgraw, axis=0, keepdims=True)
        for h, (dcn, dnn) in enumerate(carried):
            dc_scr[h] = dcn
            dn_scr[h] = dnn
            dnw_ref[:, h * DV:(h + 1) * DV] += dnw_acc[h]
        db_ref[...] += db_acc

    rev = lambda i: (nsteps - 1 - i, 0)
    rev4 = lambda i: (nsteps - 1 - i, 0, 0, 0)
    return pl.pallas_call(
        body, name=name, grid=(nsteps,),
        in_specs=[pl.BlockSpec((grp * CHUNK, MLSTM_W), rev), pl.BlockSpec((grp * CHUNK, PM_W), rev),
                  pl.BlockSpec((grp * CHUNK, MLSTM_W), rev),
                  pl.BlockSpec((grp, HEADS, DQK, DV), rev4), pl.BlockSpec((grp, HEADS, 1, DQK), rev4),
                  pl.BlockSpec((grp, HEADS, 1, 1), rev4),
                  pl.BlockSpec((1, 128), lambda i: (0, 0)), pl.BlockSpec((1, MLSTM_W), lambda i: (0, 0))],
        out_specs=[pl.BlockSpec((grp * CHUNK, PM_W), rev), pl.BlockSpec((1, MLSTM_W), lambda i: (0, 0)),
                   pl.BlockSpec((1, 128), lambda i: (0, 0))],
        out_shape=[jax.ShapeDtypeStruct((r, PM_W), BF16), jax.ShapeDtypeStruct((1, MLSTM_W), F32),
                   jax.ShapeDtypeStruct((1, 128), F32)],
        scratch_shapes=[pltpu.VMEM((HEADS, DQK, DV), F32), pltpu.VMEM((HEADS, 1, DQK), F32)],
        compiler_params=_params(),
    )(dcat, pm, ht, cs, ns, ms, bias, nw)


def _mlstm_bwd_head(h, pmv, hh, y, nwv, li, lf, cst, nst, mst, dcn, dnn, tril, eye, triu, lane, rowid,
                    dpm_ref, rows):
    f = yield from _chunk_common(pmv, h, li, lf, cst, nst, mst, tril, eye)
    q, k, v, s, p = f["q"], f["k"], f["v"], f["s"], f["p"]
    w_inter, w_in, w_old, dn = f["w_inter"], f["w_in"], f["w_old"], f["dn"]
    osl = slice(2 * QK_W + MLSTM_W + h * DV, 2 * QK_W + MLSTM_W + (h + 1) * DV)
    sg = _sigmoid(pmv[:, osl])
    yield
    rs = lax.rsqrt(jnp.mean(hh * hh, axis=1, keepdims=True) + EPS)
    yield
    nwh = nwv[:, h * DV:(h + 1) * DV]
    dpm_ref[rows, osl] = (y * (hh * rs * nwh) * sg * (1.0 - sg)).astype(BF16)
    yield
    dhn = y * sg
    dnw_h = jnp.sum(dhn * hh * rs, axis=0, keepdims=True)
    yield
    wd = dhn * nwh
    dhh = rs * wd - hh * (rs * rs * rs) * jnp.mean(hh * wd, axis=1, keepdims=True)
    yield
    dnum = dhh / dn
    dd = -jnp.sum(dhh * hh, axis=1, keepdims=True) / dn
    yield
    dden = jnp.where(jnp.abs(f["den"]) > jnp.exp(-f["mt"]), dd * jnp.sign(f["den"]), 0.0)
    dnum_b = dnum.astype(BF16)
    wdn = (w_inter * dnum).astype(BF16)
    wid = (w_inter * dden).astype(BF16).astype(F32)
    yield
    ds = _dot(dnum_b, v, NT) + dden
    yield
    dsp = (ds * p).astype(BF16)
    yield
    dq = _dot(dsp, k, NN) + _dot(wdn, f["cb"], NT) + wid * f["nb"]
    yield
    dk = _dot(dsp, q, TN)
    yield
    dv = _dot(s.astype(BF16), dnum_b, TN)
    yield
    g = ds * s
    g_col = _row_to_col(jnp.sum(g, axis=0, keepdims=True), eye)
    yield
    db = jnp.sum(g, axis=1, keepdims=True) - g_col
    dli = g_col
    yield
    db = db + (jnp.sum(dnum * f["qc"], axis=1, keepdims=True) + dden * f["qn"]) * w_inter
    yield
    dcnb = dcn.astype(BF16)
    dnnb = dnn.astype(BF16).astype(F32)
    dkw = _dot(v, dcnb, NT) + dnnb
    yield
    dk = dk + w_in * dkw
    dv = dv + _dot(f["kw"], dcnb, NN)
    yield
    ddecay = jnp.sum(dkw * f["kraw"], axis=1, keepdims=True) * w_in
    yield
    dw_old = (jnp.sum(jnp.sum(dcn * cst, axis=1, keepdims=True), axis=0, keepdims=True)
              + jnp.sum(dnn * nst, axis=1, keepdims=True))
    yield
    db_end = dw_old * w_old + jnp.sum(ddecay, axis=0, keepdims=True)
    db = db - ddecay + jnp.where(rowid == CHUNK - 1, db_end, 0.0)
    dli = dli + ddecay
    yield
    dc_new = w_old * dcn + _dot(q, wdn, TN)
    yield
    dn_new = w_old * dnn + jnp.sum(wid * q.astype(F32), axis=0, keepdims=True)
    yield
    dlf = jnp.sum(jnp.where(triu, _col_to_row(db, eye), 0.0), axis=1, keepdims=True)
    yield
    gate_part = jnp.where(lane == h, dli, 0.0) + jnp.where(lane == HEADS + h, dlf, 0.0)
    dpm_ref[rows, h * DQK:(h + 1) * DQK] = (dq * QSCALE).astype(BF16)
    yield
    dpm_ref[rows, QK_W + h * DQK:QK_W + (h + 1) * DQK] = dk.astype(BF16)
    yield
    dpm_ref[rows, 2 * QK_W + h * DV:2 * QK_W + (h + 1) * DV] = dv.astype(BF16)
    return gate_part, dc_new, dn_new, dnw_h


def _my_place():
    return lax.axis_index("x"), lax.axis_index("y"), lax.axis_index("c")


def _flip(v, bit):
    return 1 - v if bit else v


def _exchange_small(name, blk, reduce):
    r, c = blk.shape

    def body(x_ref, o_ref, *rest):
        slots = rest[0] if reduce else o_ref
        send_sems, recv_sems = rest[-2], rest[-1]
        x, y, cc = _my_place()
        me = 4 * x + 2 * y + cc
        slots[me] = x_ref[...]
        copies = []
        for k in range(1, N_DEV):
            peer = (_flip(x, k & 4), _flip(y, k & 2), _flip(cc, k & 1))
            cp = pltpu.make_async_remote_copy(
                src_ref=x_ref, dst_ref=slots.at[me], send_sem=send_sems.at[k - 1],
                recv_sem=recv_sems.at[k - 1], device_id=peer, device_id_type=MESH)
            cp.start()
            copies.append(cp)
        for cp in copies:
            cp.wait()
        if reduce:
            acc = slots[0]
            for d in range(1, N_DEV):
                acc = acc + slots[d]
            o_ref[...] = acc

    scratch = ([pltpu.VMEM((N_DEV, r, c), F32)] if reduce else []) + [
        pltpu.SemaphoreType.DMA((N_DEV - 1,)), pltpu.SemaphoreType.DMA((N_DEV - 1,))]
    return pl.pallas_call(
        body, name=name,
        out_shape=jax.ShapeDtypeStruct((r, c) if reduce else (N_DEV, r, c), F32),
        in_specs=[pl.BlockSpec(memory_space=pltpu.VMEM)], out_specs=pl.BlockSpec(memory_space=pltpu.VMEM),
        scratch_shapes=scratch, compiler_params=_params(),
    )(blk)


HBM_SPEC = pl.BlockSpec(memory_space=pltpu.HBM)
SEM_SPEC = pl.BlockSpec(memory_space=pltpu.SEMAPHORE)
ANY_SPEC = pl.BlockSpec(memory_space=pl.ANY)
DATAFLOW = pltpu.SideEffectType.DATAFLOW_SIDE_EFFECTING


def _split_copy(name, arrays, start=None, wait=None, after=None):
    results, token = _split_copies(name, [(arrays, start, wait)], after)
    return results[0][0], results[0][1], token


def _split_copies(name, jobs, after=None):
    operands, in_specs, out_shape, out_specs, aliases = [], [], [], [], {}
    in_at, out_at = [], []
    for arrays, start, wait in jobs:
        in_at.append(len(operands))
        operands += [pltpu.with_memory_space_constraint(a, pltpu.HBM) for a in arrays]
        in_specs += [HBM_SPEC] * len(arrays)
        if wait:
            operands += list(wait[1])
            in_specs += [SEM_SPEC, SEM_SPEC]
    if after is not None:
        operands.append(after)
        in_specs.append(ANY_SPEC)
    for j, (arrays, start, wait) in enumerate(jobs):
        out_at.append(len(out_shape))
        if start:
            out_shape += [pltpu.SemaphoreType.DMA((start[1],)), pltpu.SemaphoreType.DMA((start[1],))]
            out_specs += [SEM_SPEC, SEM_SPEC]
        for i, a in enumerate(arrays):
            aliases[in_at[j] + i] = len(out_shape)
            out_shape.append(pltpu.HBM(a.shape, a.dtype))
            out_specs.append(HBM_SPEC)
    any_start = any(start for _, start, _ in jobs)
    if any_start:
        out_shape.append(jax.ShapeDtypeStruct((8, 128), F32))
        out_specs.append(pl.BlockSpec(memory_space=pltpu.VMEM))
    n_in = len(operands)

    def body(*refs):
        for j, (arrays, start, wait) in enumerate(jobs):
            if wait:
                ins = refs[in_at[j]:in_at[j] + len(arrays)]
                at = in_at[j] + len(arrays)
                for cp in wait[0](ins, refs[at], refs[at + 1]):
                    cp.wait_send()
                    cp.wait_recv()
        for j, (arrays, start, wait) in enumerate(jobs):
            if start:
                ins = refs[in_at[j]:in_at[j] + len(arrays)]
                at = n_in + out_at[j]
                for cp in start[0](ins, refs[at], refs[at + 1]):
                    cp.start()
        if any_start:
            token = refs[n_in + len(out_shape) - 1]
            token[...] = jnp.zeros_like(token)

    outs = pl.pallas_call(
        body, name=name, in_specs=in_specs, out_specs=out_specs, out_shape=out_shape,
        input_output_aliases=aliases, compiler_params=pltpu.CompilerParams(has_side_effects=DATAFLOW),
    )(*operands)
    results = []
    for j, (arrays, start, wait) in enumerate(jobs):
        at = out_at[j]
        sems = (outs[at], outs[at + 1]) if start else None
        at += 2 if start else 0
        results.append((list(outs[at:at + len(arrays)]), sems))
    return results, (outs[-1] if any_start else None)


def _remote(src, dst, send_sems, recv_sems, k, to):
    return pltpu.make_async_remote_copy(src_ref=src, dst_ref=dst, send_sem=send_sems.at[k],
                                        recv_sem=recv_sems.at[k], device_id=to, device_id_type=MESH)


def _slot(px, py, pc):
    return 4 * px + 2 * py + pc


def _gather_first(refs, send_sems, recv_sems):
    x, y, c = _my_place()
    blk = refs[0].at[_slot(x, y, c)]
    targets = [(x, y, 1 - c), (1 - x, y, c), (x, 1 - y, c)]
    return [_remote(blk, blk, send_sems, recv_sems, k, to) for k, to in enumerate(targets)]


def _gather_relay(refs, send_sems, recv_sems):
    x, y, c = _my_place()
    rows = refs[0].shape[1]
    half = rows // 32 * 16
    from_x, from_y = _slot(1 - x, y, c), _slot(x, 1 - y, c)
    upper = refs[0].at[from_x, pl.ds(0, half)]
    lower = refs[0].at[from_y, pl.ds(half, rows - half)]
    return [_remote(upper, upper, send_sems, recv_sems, 0, (x, 1 - y, c)),
            _remote(lower, lower, send_sems, recv_sems, 1, (1 - x, y, c)),
            _remote(refs[0].at[from_x], refs[0].at[from_x], send_sems, recv_sems, 2, (x, y, 1 - c)),
            _remote(refs[0].at[from_y], refs[0].at[from_y], send_sems, recv_sems, 3, (x, y, 1 - c))]


def _gather_last(refs, send_sems, recv_sems):
    x, y, c = _my_place()
    blk = refs[0].at[_slot(1 - x, 1 - y, c)]
    return [_remote(blk, blk, send_sems, recv_sems, 0, (x, y, 1 - c))]


def _scatter_sibling(n):
    def copies(refs, send_sems, recv_sems):
        x, y, c = _my_place()
        return [_remote(refs[a].at[2 * j + 1 - c], refs[n + a].at[j], send_sems, recv_sems, 4 * a + j, (x, y, 1 - c))
                for a in range(n) for j in range(4)]
    return copies


def _scatter_chips(n):
    def copies(refs, send_sems, recv_sems):
        x, y, c = _my_place()
        out = []
        for a in range(n):
            for k in range(1, 4):
                px, py = _flip(x, k & 2), _flip(y, k & 1)
                out.append(_remote(refs[a].at[2 * px + py], refs[n + a].at[2 * x + y], send_sems, recv_sems,
                                   3 * a + k - 1, (px, py, c)))
        return out
    return copies


def _pair_sum(name, core, g, t):
    _, r, c = g.shape
    tr = _tile(r, 512, 8)
    g4 = g.reshape(4, 2, r, c)

    def body(core_ref, g_ref, t_ref, o_ref):
        o_ref[...] = (g_ref[...].astype(F32) + t_ref[...].astype(F32)).astype(BF16)

    return pl.pallas_call(
        body, name=name,
        grid_spec=pltpu.PrefetchScalarGridSpec(
            num_scalar_prefetch=1, grid=(4, r // tr),
            in_specs=[pl.BlockSpec((None, None, tr, c), lambda j, i, core_ref: (j, core_ref[0], i, 0)),
                      pl.BlockSpec((None, tr, c), lambda j, i, core_ref: (j, i, 0))],
            out_specs=pl.BlockSpec((None, tr, c), lambda j, i, core_ref: (j, i, 0))),
        out_shape=jax.ShapeDtypeStruct((4, r, c), BF16), compiler_params=_params(),
    )(core, g4, t)


def _adam_math(w, g, m, v):
    m2 = ADAM_B1 * m + (1.0 - ADAM_B1) * g
    v2 = ADAM_B2 * v + (1.0 - ADAM_B2) * (g * g)
    m_hat = m2 / (1.0 - ADAM_B1 ** ADAM_STEP)
    v_hat = v2 / (1.0 - ADAM_B2 ** ADAM_STEP)
    delta = -ADAM_LR * (m_hat / (jnp.sqrt(v_hat) + ADAM_EPS) + ADAM_WD * w)
    return delta, m2, v2


def _adam_sharded(name, chip, w, m, v, grads, row_off=0):
    _, r, c = w.shape
    tr = _tile(r, 256, 8)
    tc = c if tr < r else _tile(c, 256, 128)
    boff = row_off // tr

    def body(chip_ref, w_ref, m_ref, v_ref, p0_ref, q0_ref, p1_ref, q1_ref, g_ref, d_ref, nm_ref, nv_ref):
        mine = chip_ref[0]

        def total(p_ref, q_ref):
            acc = None
            for j in range(4):
                part = jnp.where(mine == j, p_ref[...], q_ref[j]).astype(F32)
                acc = part if acc is None else acc + part
            return acc

        g = jnp.where(pl.program_id(0) == 0, total(p0_ref, q0_ref), total(p1_ref, q1_ref))
        delta, m2, v2 = _adam_math(w_ref[...], g, m_ref[...], v_ref[...])
        g_ref[...] = g
        d_ref[...] = delta
        nm_ref[...] = m2
        nv_ref[...] = v2

    def grad_specs(layer):
        at = lambda l, i, j: (jnp.where(l == layer, boff + i, boff), jnp.where(l == layer, j, 0))
        return [pl.BlockSpec((None, tr, tc), lambda l, i, j, chip_ref: (chip_ref[0],) + at(l, i, j)),
                pl.BlockSpec((4, tr, tc), lambda l, i, j, chip_ref: (0,) + at(l, i, j))]

    wspec = pl.BlockSpec((None, tr, tc), lambda l, i, j, chip_ref: (l, i, j))
    sds = jax.ShapeDtypeStruct(w.shape, F32)
    return pl.pallas_call(
        body, name=name,
        grid_spec=pltpu.PrefetchScalarGridSpec(
            num_scalar_prefetch=1, grid=(2, r // tr, c // tc),
            in_specs=[wspec, wspec, wspec] + grad_specs(0) + grad_specs(1), out_specs=[wspec] * 4),
        out_shape=[sds] * 4, compiler_params=_params(),
    )(chip, w, m, v, grads[0][0], grads[0][1], grads[1][0], grads[1][1])


def _adam_small(name, w, m, v, g):
    def body(w_ref, m_ref, v_ref, g_ref, d_ref, nm_ref, nv_ref):
        delta, m2, v2 = _adam_math(w_ref[...], g_ref[...], m_ref[...], v_ref[...])
        d_ref[...] = delta
        nm_ref[...] = m2
        nv_ref[...] = v2

    sds = jax.ShapeDtypeStruct(w.shape, F32)
    vm = pl.BlockSpec(memory_space=pltpu.VMEM)
    return pl.pallas_call(body, name=name, in_specs=[vm] * 4, out_specs=[vm] * 3, out_shape=[sds] * 3,
                          compiler_params=_params())(w, m, v, g)


GATE_END = GATE_COL + 2 * HEADS


def _merge_dw_in(dwm_t, dwc_t):
    full = jnp.concatenate([dwm_t[:GATE_END], dwc_t.reshape(3 * CONV_W, D_MODEL)], axis=0)
    return full.reshape(N_DEV, IN_SH, D_MODEL)


def _pack128(parts):
    flat = jnp.concatenate([p.reshape(-1) for p in parts])
    n = flat.shape[0]
    rows = -(-n // 1024) * 8
    return jnp.pad(flat, (0, rows * 128 - n)).reshape(rows, 128)


def _unpack128(packed, shapes):
    flat = packed.reshape(-1)
    out, at = [], 0
    for s in shapes:
        n = int(np.prod(s))
        out.append(flat[at:at + n].reshape(s))
        at += n
    return out


def kernel(x, meta_tokens, norm_mix_w, w_in, b_gates, conv_w, mlstm_norm_w, w_out, norm_ffn_w, w_gate, w_up, w_down, norm_final_w, loss_target, m_meta_tokens, m_norm_mix_w, m_w_in, m_b_gates, m_conv_w, m_mlstm_norm_w, m_w_out, m_norm_ffn_w, m_w_gate, m_w_up, m_w_down, m_norm_final_w, v_meta_tokens, v_norm_mix_w, v_w_in, v_b_gates, v_conv_w, v_mlstm_norm_w, v_w_out, v_norm_ffn_w, v_w_gate, v_w_up, v_w_down, v_norm_final_w):
    seq = x.shape[1]
    rows = TOK0 + seq
    me = 4 * lax.axis_index("x") + 2 * lax.axis_index("y") + lax.axis_index("c")
    meta_sh = meta_tokens.shape[1]
    conv_sh = conv_w.shape[2]

    w_gate_t, m_w_gate_t, v_w_gate_t = (jnp.transpose(a, (0, 2, 1)) for a in (w_gate, m_w_gate, v_w_gate))
    w_up_t, m_w_up_t, v_w_up_t = (jnp.transpose(a, (0, 2, 1)) for a in (w_up, m_w_up, v_w_up))
    shards = []
    for l in range(DEPTH):
        shards += [jnp.transpose(w_in[l]).astype(BF16), w_out[l].astype(BF16), w_gate_t[l].astype(BF16),
                   w_up_t[l].astype(BF16), w_down[l].astype(BF16)]
    per_layer = ("w_in", "w_out", "w_gate", "w_up", "w_down")
    gather_names = [f"{nm}_{l}" for l in range(DEPTH) for nm in per_layer]
    gather_state = {}

    def gather_step(tag, after, start=None, relay=None, last=None, done=()):
        jobs, idx = [], []
        if relay is not None and relay < len(shards):
            jobs.append((gather_state[relay][0], (_gather_relay, 4), (_gather_first, gather_state[relay][1])))
            idx.append(relay)
        if start is not None and start < len(shards):
            buf = lax.dynamic_update_index_in_dim(lax.empty((N_DEV,) + shards[start].shape, BF16), shards[start], me, 0)
            jobs.append(([buf], (_gather_first, 3), None))
            idx.append(start)
        if last is not None:
            jobs.append((gather_state[last][0], (_gather_last, 1), (_gather_relay, gather_state[last][1])))
            idx.append(last)
        for i in done:
            jobs.append((gather_state[i][0], None, (_gather_last, gather_state[i][1])))
            idx.append(i)
        if not jobs:
            return after, []
        results, tok = _split_copies(f"gather_{tag}", jobs, after)
        for i, res in zip(idx, results):
            gather_state[i] = res
        return (after if tok is None else tok), [gather_state[i][0][0] for i in done]

    bias = [jnp.pad(b_gates[l].reshape(1, 2 * HEADS), ((0, 0), (0, 128 - 2 * HEADS))) for l in range(DEPTH)]
    nmix = [norm_mix_w[l].reshape(1, D_MODEL) for l in range(DEPTH)]
    nffn = [norm_ffn_w[l].reshape(1, D_MODEL) for l in range(DEPTH)]
    nmls = [mlstm_norm_w[l].reshape(1, MLSTM_W) for l in range(DEPTH)]
    weights = [dict() for _ in range(DEPTH)]
    saved = [dict() for _ in range(DEPTH)]

    def layer_fwd(l, h, after):
        w, s = weights[l], saved[l]
        k0 = len(per_layer) * l
        tok, _ = gather_step(f"l{l}_a", after, last=k0)
        _, (g_in,) = gather_step(f"l{l}_b", tok, done=[k0])
        tok, _ = gather_step(f"l{l}_c", g_in, relay=k0 + 1, start=k0 + 3)
        w["win_t"] = g_in.reshape(D_IN, D_MODEL)
        w["wc_t"] = w["win_t"][GATE_END:].reshape(3, CONV_W, D_MODEL)
        s["h0"] = h
        s["hn"] = _rms_fwd(f"norm_mix_{l}", h, nmix[l] + tok[0, 0])
        s["pm"] = _mm_nt(f"proj_mlstm_{l}", s["hn"], w["win_t"], F32, tn=640, tk=D_MODEL, n=PM_W)
        tok, _ = gather_step(f"l{l}_d", s["pm"], relay=k0 + 2, start=k0 + 4)
        tok, _ = gather_step(f"l{l}_d2", tok, last=k0 + 1)
        s["pc"] = _mm_nt_bcols(f"proj_conv_{l}", s["hn"], w["wc_t"], F32, dep=tok)
        hm, s["ht"], s["cs"], s["ns"], s["ms"] = _mlstm_fwd(f"mlstm_fwd_{l}", s["pm"], bias[l] + tok[:1], nmls[l])
        tok, _ = gather_step(f"l{l}_e", hm, relay=k0 + 3, start=k0 + 5)
        tok, _ = gather_step(f"l{l}_e2", tok, last=k0 + 2)
        hc = _conv_fwd(f"conv_fwd_{l}", s["pc"], conv_rows[l] + tok[0, 0])
        s["cat"] = jnp.concatenate([hm, hc], axis=1)
        _, (g_out,) = gather_step(f"l{l}_f", s["cat"], done=[k0 + 1])
        w["wo"] = g_out.reshape(D_MODEL, D_MODEL)
        s["h1"] = _mm_nn(f"out_proj_{l}", s["cat"], w["wo"], F32, res=s["h0"])
        tok, _ = gather_step(f"l{l}_g", s["h1"], relay=k0 + 4, start=k0 + 6)
        s["hf"] = _rms_fwd(f"norm_ffn_{l}", s["h1"], nffn[l] + tok[0, 0])
        tok, _ = gather_step(f"l{l}_h", s["hf"], last=k0 + 3)
        _, (g_gate, g_up) = gather_step(f"l{l}_i", tok, done=[k0 + 2, k0 + 3])
        w["wg_t"] = g_gate.reshape(D_FF, D_MODEL)
        w["wu_t"] = g_up.reshape(D_FF, D_MODEL)
        s["g"], s["u"], s["act"] = _ffn_in(f"ffn_in_{l}", s["hf"], w["wg_t"], w["wu_t"])
        tok, _ = gather_step(f"l{l}_j", s["act"], last=k0 + 4)
        _, (g_down,) = gather_step(f"l{l}_k", tok, done=[k0 + 4])
        w["wd"] = g_down.reshape(D_FF, D_MODEL)
        tok, _ = gather_step(f"l{l}_k2", tok, relay=k0 + 5, start=k0 + 7)
        return _mm_nn(f"ffn_out_{l}", s["act"], w["wd"], F32, res=s["h1"], dep=tok)

    tok, _ = gather_step("first", None, start=0)
    zero = tok[0, 0]
    small = jnp.concatenate(
        [meta_tokens + zero, jnp.pad(conv_w.reshape(DEPTH * 3, conv_sh), ((0, 2), (0, meta_sh - conv_sh)))], axis=0)
    slots = _exchange_small("gather_small", small, reduce=False)
    meta_full = jnp.transpose(slots[:, :N_META, :], (1, 0, 2)).reshape(N_META, D_MODEL)
    conv_full = jnp.transpose(slots[:, N_META:N_META + DEPTH * 3, :conv_sh], (1, 0, 2)).reshape(DEPTH, 3, CONV_W)
    conv_rows = [jnp.pad(conv_full[l], ((0, 5), (0, 0))) for l in range(DEPTH)]
    w_in_t, m_w_in_t, v_w_in_t = (jnp.transpose(a + zero, (0, 2, 1)) for a in (w_in, m_w_in, v_w_in))
    tok, w_in_t, m_w_in_t, v_w_in_t, meta_full = lax.optimization_barrier(
        (tok, w_in_t, m_w_in_t, v_w_in_t, meta_full))
    tok, _ = gather_step("pre_a", tok, relay=0)
    tok, _ = gather_step("pre_b", tok, start=1)
    tok, _ = gather_step("pre_c", tok, start=2)
    h = jnp.concatenate([jnp.zeros((PAD_FRONT, D_MODEL), F32), meta_full, x[0]], axis=0)
    h = layer_fwd(0, h, tok)
    h = layer_fwd(1, h, h)

    dh, dh_b, d_final, loss_part = _final_loss("final_loss", h, norm_final_w.reshape(1, D_MODEL), loss_target[0])

    core = lax.axis_index("c").astype(jnp.int32).reshape(1)
    chip = (2 * lax.axis_index("x") + lax.axis_index("y")).astype(jnp.int32).reshape(1)
    scatter_state = {}

    def scatter_begin(nm, grad):
        land = lax.empty((4,) + grad.shape[1:], BF16)
        arrs, sems, tok = _split_copy(f"grad_sibling_start_{nm}", [grad, land], start=(_scatter_sibling(1), 4))
        scatter_state[nm] = (arrs, sems)
        return tok

    def scatter_advance(nm, after):
        arrs, sems = scatter_state[nm]
        arrs, _, _ = _split_copy(f"grad_sibling_done_{nm}", arrs, wait=(_scatter_sibling(1), sems), after=after)
        part = _pair_sum(f"grad_pair_sum_{nm}", core, arrs[0], arrs[1])
        arrs, sems, tok = _split_copy(f"grad_chips_start_{nm}", [part, lax.empty(part.shape, BF16)],
                                      start=(_scatter_chips(1), 3))
        scatter_state[nm] = (arrs, sems)
        return tok

    def scattered(nm, after):
        arrs, sems = scatter_state[nm]
        arrs, _, _ = _split_copy(f"grad_chips_done_{nm}", arrs, wait=(_scatter_chips(1), sems), after=after)
        return arrs[0], arrs[1]

    d_mix, d_ffn, d_mls, d_bias, d_conv = ([None] * DEPTH for _ in range(5))

    def layer_bwd(l, dh, dh_b, tok):
        w, s = weights[l], saved[l]
        dg, du = _ffn_act_bwd(f"d_act_{l}", dh_b, w["wd"], s["g"], s["u"], dep=tok)
        dw_down = _mm_tn(f"dw_down_{l}", s["act"], dh_b, BF16, tm=1408, tn=1024)
        tok = scatter_begin(f"w_down_{l}", dw_down.reshape(N_DEV, FF_SH, D_MODEL))
        dhf = _mm_nn(f"d_ffn_gate_{l}", dg, w["wg_t"], F32, dep=tok)
        tok = scatter_advance(f"w_down_{l}", after=dhf)
        dhf = _mm_nn(f"d_ffn_up_{l}", du, w["wu_t"], F32, res=dhf, dep=tok)
        dw_gate = _mm_tn(f"dw_gate_{l}", dg, s["hf"], BF16, tm=1408, tn=1024)
        tok = scatter_begin(f"w_gate_{l}", dw_gate.reshape(N_DEV, FF_SH, D_MODEL))
        dw_up = _mm_tn(f"dw_up_{l}", du, s["hf"], BF16, tm=1408, tn=1024, dep=tok)
        tok = scatter_begin(f"w_up_{l}", dw_up.reshape(N_DEV, FF_SH, D_MODEL))
        dh1, dh1_b, d_ffn[l] = _rms_bwd(f"norm_ffn_bwd_{l}", s["h1"], nffn[l] + tok[0, 0], dhf, dh)
        tok = scatter_advance(f"w_gate_{l}", after=dh1)
        dcat = _mm_nt(f"d_cat_{l}", dh1_b, w["wo"], F32, tk=D_MODEL, dep=tok)
        tok = scatter_advance(f"w_up_{l}", after=dcat)
        dw_out = _mm_tn(f"dw_out_{l}", s["cat"], dh1_b, BF16, tn=1024, dep=tok)
        tok = scatter_begin(f"w_out_{l}", dw_out.reshape(N_DEV, OUT_SH, D_MODEL))
        dpm, d_mls[l], d_bias[l] = _mlstm_bwd(f"mlstm_bwd_{l}", dcat, s["pm"], s["ht"], s["cs"], s["ns"],
                                               s["ms"], bias[l] + tok[:1], nmls[l])
        dpc, d_conv[l] = _conv_bwd(f"conv_bwd_{l}", dcat, s["pc"], conv_rows[l])
        tok = scatter_advance(f"w_out_{l}", after=dpc)
        dwm_t = _mm_tn(f"dw_mlstm_{l}", dpm, s["hn"], BF16, tm=640, tn=1024, dep=tok)
        dwc_t = _mm_tn_acols(f"dw_conv_{l}", dpc, s["hn"], BF16)
        tok = scatter_begin(f"w_in_{l}", _merge_dw_in(dwm_t, dwc_t))
        dhn = _mm_nn_kt(f"d_norm_mlstm_{l}", dpm, w["win_t"], F32, tk=PM_W, dep=tok)
        dhn = _mm_nn_ksum(f"d_norm_conv_{l}", dpc, w["wc_t"], F32, res=dhn)
        tok = scatter_advance(f"w_in_{l}", after=dhn)
        dh, dh_b, d_mix[l] = _rms_bwd(f"norm_mix_bwd_{l}", s["h0"], nmix[l] + tok[0, 0], dhn, dh1)
        return dh, dh_b, tok

    dh, dh_b, tok = layer_bwd(1, dh, dh_b, None)
    dh, dh_b, tok_tail = layer_bwd(0, dh, dh_b, tok)

    pq = {}
    after = dh
    for l in reversed(range(DEPTH)):
        for nm in ("w_down", "w_gate", "w_up", "w_out", "w_in"):
            if (nm, l) != ("w_in", 0):
                pq[nm, l] = scattered(f"{nm}_{l}", after)
                after = pq[nm, l][0]
    untransposed = lambda outs: [jnp.transpose(o, (0, 2, 1)) for o in outs]
    g_out, d_out, nm_out, nv_out = _adam_sharded(
        "adam_w_out", chip, w_out, m_w_out, v_w_out, [pq["w_out", 0], pq["w_out", 1]])
    g_gate, d_gate, nm_gate, nv_gate = untransposed(_adam_sharded(
        "adam_w_gate", chip, w_gate_t, m_w_gate_t, v_w_gate_t, [pq["w_gate", 0], pq["w_gate", 1]]))
    g_up, d_up, nm_up, nv_up = untransposed(_adam_sharded(
        "adam_w_up", chip, w_up_t, m_w_up_t, v_w_up_t, [pq["w_up", 0], pq["w_up", 1]]))
    g_down, d_down, nm_down, nv_down = _adam_sharded(
        "adam_w_down", chip, w_down, m_w_down, v_w_down, [pq["w_down", 0], pq["w_down", 1]])
    pq["w_in", 0] = scattered("w_in_0", nv_down)
    g_in, d_in, nm_in, nv_in = untransposed(_adam_sharded(
        "adam_w_in", chip, w_in_t, m_w_in_t, v_w_in_t, [pq["w_in", 0], pq["w_in", 1]]))

    bg = jnp.concatenate([d_bias[l][0, :2 * HEADS] for l in range(DEPTH)])
    red_in = jnp.concatenate([
        dh[PAD_FRONT:TOK0], d_mix[0], d_mix[1], d_ffn[0], d_ffn[1], d_final,
        jnp.concatenate([d_mls[0], d_mls[1]], axis=1),
        jnp.stack([d_conv[l][:3] for l in range(DEPTH)]).reshape(3, 2 * CONV_W),
        jnp.pad(bg, (0, D_MODEL - bg.shape[0])).reshape(1, D_MODEL),
        jnp.pad(loss_part[:, :1], ((0, 0), (0, D_MODEL - 1))),
        jnp.zeros((5, D_MODEL), F32) + tok_tail[0, 0]], axis=0)
    red = _exchange_small("reduce_small", red_in, reduce=True)
    loss = red[26, 0]
    g_meta = lax.dynamic_slice_in_dim(red[:N_META], me * meta_sh, meta_sh, axis=1)
    g_mix, g_ffn, g_final = red[16:18], red[18:20], red[20]
    g_mls = red[21].reshape(DEPTH, MLSTM_W)
    g_conv = lax.dynamic_slice_in_dim(red[22:25].reshape(DEPTH, 3, CONV_W), me * conv_sh, conv_sh, axis=2)
    g_bias = red[25, :DEPTH * 2 * HEADS].reshape(DEPTH, 2 * HEADS)

    small_w = [meta_tokens, norm_mix_w, b_gates, conv_w, mlstm_norm_w, norm_ffn_w, norm_final_w]
    small_m = [m_meta_tokens, m_norm_mix_w, m_b_gates, m_conv_w, m_mlstm_norm_w, m_norm_ffn_w, m_norm_final_w]
    small_v = [v_meta_tokens, v_norm_mix_w, v_b_gates, v_conv_w, v_mlstm_norm_w, v_norm_ffn_w, v_norm_final_w]
    small_g = [g_meta, g_mix, g_bias, g_conv, g_mls, g_ffn, g_final]
    shapes = [a.shape for a in small_w]
    packed = _adam_small("adam_small", _pack128(small_w), _pack128(small_m), _pack128(small_v), _pack128(small_g))
    (d_meta, d_nmix, d_bg, d_cw, d_nmls, d_nffn, d_nfin), (nm_meta, nm_nmix, nm_bg, nm_cw, nm_nmls, nm_nffn, nm_nfin), \
        (nv_meta, nv_nmix, nv_bg, nv_cw, nv_nmls, nv_nffn, nv_nfin) = (_unpack128(p, shapes) for p in packed)

    grad_x = dh[TOK0:].reshape(1, seq, D_MODEL)
    return (loss, grad_x,
            g_meta, g_mix, g_in, g_bias, g_conv, g_mls, g_out, g_ffn, g_gate, g_up, g_down, g_final,
            d_meta, d_nmix, d_in, d_bg, d_cw, d_nmls, d_out, d_nffn, d_gate, d_up, d_down, d_nfin,
            nm_meta, nm_nmix, nm_in, nm_bg, nm_cw, nm_nmls, nm_out, nm_nffn, nm_gate, nm_up, nm_down, nm_nfin,
            nv_meta, nv_nmix, nv_in, nv_bg, nv_cw, nv_nmls, nv_out, nv_nffn, nv_gate, nv_up, nv_down, nv_nfin)
```

```python
import functools

import numpy as np
import jax
import jax.numpy as jnp
from jax import lax
from jax.experimental import pallas as pl
from jax.experimental.pallas import tpu as pltpu

F32 = jnp.float32
BF16 = jnp.bfloat16
MESH = pl.DeviceIdType.MESH

D_MODEL = 2048
DEPTH = 2
N_META = 16
MLSTM_W = 1024
CONV_W = 1024
HEADS = 4
DV = 256
DQK = 128
QK_W = 512
CHUNK = 64
PAD_FRONT = 48
TOK0 = PAD_FRONT + N_META
D_FF = 5632
N_DEV = 8
FF_SH = D_FF // N_DEV
D_IN = 6152
IN_SH = D_IN // N_DEV
OUT_SH = D_MODEL // N_DEV
GATE_COL = 3072
PM_W = GATE_COL + 128
GATE_CAP = 15.0
EPS = 1e-6
QSCALE = DQK ** -0.5

ADAM_LR = 0.001
ADAM_B1 = 0.9
ADAM_B2 = 0.999
ADAM_EPS = 1e-08
ADAM_WD = 0.01
ADAM_STEP = 10

V7X_VMEM_LIMIT = 50 * 1024 * 1024
V7X_MXU_COLS = 256


def _params(**kw):
    return pltpu.CompilerParams(vmem_limit_bytes=V7X_VMEM_LIMIT, **kw)


def _tile(n, target, mult):
    best = None
    for t in range(mult, min(n, target) + 1, mult):
        if n % t == 0:
            best = t
    return best if best is not None else n


def _sigmoid(x):
    return 1.0 / (1.0 + jnp.exp(-x))


NN = ((1,), (0,))
NT = ((1,), (1,))
TN = ((0,), (0,))


def _matmul(name, a, b, out_shape, out_dtype, grid, a_bs, b_bs, o_bs, dims, nk, acc_shape=None,
            res=None, res_bs=None, dep=None):
    has_res = res is not None
    n_in = 2 + has_res + (dep is not None)

    def body(*refs):
        a_ref, b_ref = refs[0], refs[1]
        r_ref = refs[2] if has_res else None
        o_ref = refs[n_in]
        x = lax.dot_general(a_ref[...], b_ref[...], (dims, ((), ())), preferred_element_type=F32)
        if nk == 1:
            if has_res:
                x = x + r_ref[...]
            o_ref[...] = x.astype(o_ref.dtype)
            return
        acc = refs[n_in + 1]
        k = pl.program_id(len(grid) - 1)

        @pl.when(k == 0)
        def _():
            acc[...] = (x + r_ref[...]) if has_res else x

        @pl.when(k > 0)
        def _():
            acc[...] += x

        @pl.when(k == nk - 1)
        def _():
            o_ref[...] = acc[...].astype(o_ref.dtype)

    ins = [a, b] + ([res] if has_res else [])
    specs = [a_bs, b_bs] + ([res_bs] if has_res else [])
    if dep is not None:
        ins.append(dep)
        specs.append(pl.BlockSpec((8, 128), lambda *_: (0, 0)))
    scratch = [pltpu.VMEM(acc_shape, F32)] if nk > 1 else []
    return pl.pallas_call(
        body, name=name, grid=grid, in_specs=specs, out_specs=o_bs,
        out_shape=jax.ShapeDtypeStruct(out_shape, out_dtype), scratch_shapes=scratch,
        compiler_params=_params(),
    )(*ins)


def _mm_nn(name, a, b, out_dtype, res=None, tm=1056, tn=512, dep=None):
    r, k = a.shape
    n = b.shape[1]
    tm, tn = _tile(r, tm, 8), _tile(n, tn, 128)
    return _matmul(name, a, b, (r, n), out_dtype, (r // tm, n // tn, 1),
                   pl.BlockSpec((tm, k), lambda i, j, s: (i, 0)),
                   pl.BlockSpec((k, tn), lambda i, j, s: (0, j)),
                   pl.BlockSpec((tm, tn), lambda i, j, s: (i, j)), NN, 1,
                   res=res, res_bs=pl.BlockSpec((tm, tn), lambda i, j, s: (i, j)), dep=dep)


def _mm_nn_kt(name, a, b, out_dtype, tm=1056, tn=1024, tk=640, dep=None):
    r, k = a.shape
    n = b.shape[1]
    tm, tn, tk = _tile(r, tm, 8), _tile(n, tn, 128), _tile(k, tk, 128)
    nk = k // tk
    return _matmul(name, a, b, (r, n), out_dtype, (r // tm, n // tn, nk),
                   pl.BlockSpec((tm, tk), lambda i, j, s: (i, s)),
                   pl.BlockSpec((tk, tn), lambda i, j, s: (s, j)),
                   pl.BlockSpec((tm, tn), lambda i, j, s: (i, j)), NN, nk, acc_shape=(tm, tn), dep=dep)


def _mm_nn_ksum(name, a3, b3, out_dtype, res=None, tm=1056, tn=1024, dep=None):
    e, r, kb = a3.shape
    n = b3.shape[2]
    tm, tn = _tile(r, tm, 8), _tile(n, tn, 128)
    return _matmul(name, a3, b3, (r, n), out_dtype, (r // tm, n // tn, e),
                   pl.BlockSpec((None, tm, kb), lambda i, j, s: (s, i, 0)),
                   pl.BlockSpec((None, kb, tn), lambda i, j, s: (s, 0, j)),
                   pl.BlockSpec((tm, tn), lambda i, j, s: (i, j)), NN, e, acc_shape=(tm, tn),
                   res=res, res_bs=pl.BlockSpec((tm, tn), lambda i, j, s: (i, j)), dep=dep)


def _mm_nt(name, a, b, out_dtype, res=None, tm=1056, tn=512, tk=640, n=None, dep=None):
    r, k = a.shape
    n = b.shape[0] if n is None else n
    tm, tn, tk = _tile(r, tm, 8), _tile(n, tn, 128), _tile(k, tk, 128)
    nk = k // tk
    return _matmul(name, a, b, (r, n), out_dtype, (r // tm, n // tn, nk),
                   pl.BlockSpec((tm, tk), lambda i, j, s: (i, s)),
                   pl.BlockSpec((tn, tk), lambda i, j, s: (j, s)),
                   pl.BlockSpec((tm, tn), lambda i, j, s: (i, j)), NT, nk, acc_shape=(tm, tn),
                   res=res, res_bs=pl.BlockSpec((tm, tn), lambda i, j, s: (i, j)), dep=dep)


def _mm_nt_bcols(name, a, b3, out_dtype, tm=1056, dep=None):
    r, k = a.shape
    e, n, _ = b3.shape
    tm = _tile(r, tm, 8)
    return _matmul(name, a, b3, (e, r, n), out_dtype, (r // tm, e, 1),
                   pl.BlockSpec((tm, k), lambda i, g, s: (i, 0)),
                   pl.BlockSpec((None, n, k), lambda i, g, s: (g, 0, 0)),
                   pl.BlockSpec((None, tm, n), lambda i, g, s: (g, i, 0)), NT, 1, dep=dep)


def _mm_tn(name, a, b, out_dtype, tm=1024, tn=640, dep=None):
    r, m = a.shape
    n = b.shape[1]
    tm, tn = _tile(m, tm, 128), _tile(n, tn, 128)
    return _matmul(name, a, b, (m, n), out_dtype, (m // tm, n // tn, 1),
                   pl.BlockSpec((r, tm), lambda i, j, s: (0, i)),
                   pl.BlockSpec((r, tn), lambda i, j, s: (0, j)),
                   pl.BlockSpec((tm, tn), lambda i, j, s: (i, j)), TN, 1, dep=dep)


def _mm_tn_acols(name, a3, b, out_dtype, tn=1024, dep=None):
    e, r, m = a3.shape
    n = b.shape[1]
    tn = _tile(n, tn, 128)
    return _matmul(name, a3, b, (e, m, n), out_dtype, (n // tn, e, 1),
                   pl.BlockSpec((None, r, m), lambda j, g, s: (g, 0, 0)),
                   pl.BlockSpec((r, tn), lambda j, g, s: (0, j)),
                   pl.BlockSpec((None, m, tn), lambda j, g, s: (g, 0, j)), TN, 1, dep=dep)


def _rms_fwd(name, h, w):
    r, d = h.shape
    tr = _tile(r, 264, 8)

    def body(h_ref, w_ref, o_ref):
        x = h_ref[...]
        rs = lax.rsqrt(jnp.mean(x * x, axis=1, keepdims=True) + EPS)
        o_ref[...] = (x * rs * w_ref[...]).astype(BF16)

    return pl.pallas_call(
        body, name=name, grid=(r // tr,),
        in_specs=[pl.BlockSpec((tr, d), lambda i: (i, 0)), pl.BlockSpec((1, d), lambda i: (0, 0))],
        out_specs=pl.BlockSpec((tr, d), lambda i: (i, 0)),
        out_shape=jax.ShapeDtypeStruct((r, d), BF16), compiler_params=_params(),
    )(h, w)


def _rms_bwd(name, x, w, dy, dres):
    r, d = x.shape
    tr = _tile(r, 264, 8)

    def body(x_ref, w_ref, dy_ref, dr_ref, dx_ref, dxb_ref, dw_ref):
        xv = x_ref[...]
        g = dy_ref[...]
        rs = lax.rsqrt(jnp.mean(xv * xv, axis=1, keepdims=True) + EPS)
        wg = g * w_ref[...]
        dx = rs * wg - xv * (rs * rs * rs) * jnp.mean(xv * wg, axis=1, keepdims=True) + dr_ref[...]
        dx_ref[...] = dx
        dxb_ref[...] = dx.astype(BF16)
        part = jnp.sum(g * xv * rs, axis=0, keepdims=True)

        @pl.when(pl.program_id(0) == 0)
        def _():
            dw_ref[...] = part

        @pl.when(pl.program_id(0) > 0)
        def _():
            dw_ref[...] += part

    row = pl.BlockSpec((tr, d), lambda i: (i, 0))
    vec = pl.BlockSpec((1, d), lambda i: (0, 0))
    return pl.pallas_call(
        body, name=name, grid=(r // tr,), in_specs=[row, vec, row, row], out_specs=[row, row, vec],
        out_shape=[jax.ShapeDtypeStruct((r, d), F32), jax.ShapeDtypeStruct((r, d), BF16),
                   jax.ShapeDtypeStruct((1, d), F32)],
        compiler_params=_params(),
    )(x, w, dy, dres)


def _final_loss(name, h, w, target):
    r, d = h.shape
    nb = r // CHUNK

    def body(h_ref, w_ref, t_ref, dh_ref, dhb_ref, dw_ref, ls_ref):
        i = pl.program_id(0)

        @pl.when(i == 0)
        def _():
            dh_ref[...] = jnp.zeros_like(dh_ref)
            dhb_ref[...] = jnp.zeros_like(dhb_ref)
            dw_ref[...] = jnp.zeros_like(dw_ref)
            ls_ref[...] = jnp.zeros_like(ls_ref)

        @pl.when(i > 0)
        def _():
            xv = h_ref[...]
            wv = w_ref[...]
            rs = lax.rsqrt(jnp.mean(xv * xv, axis=1, keepdims=True) + EPS)
            err = xv * rs * wv - t_ref[...]
            sq = jnp.sum(jnp.sum(err * err, axis=1, keepdims=True), axis=0, keepdims=True)
            ls_ref[...] += jnp.broadcast_to(sq * (0.5 / d), ls_ref.shape)
            g = err * (1.0 / d)
            wg = g * wv
            dx = rs * wg - xv * (rs * rs * rs) * jnp.mean(xv * wg, axis=1, keepdims=True)
            dh_ref[...] = dx
            dhb_ref[...] = dx.astype(BF16)
            dw_ref[...] += jnp.sum(g * xv * rs, axis=0, keepdims=True)

    row = pl.BlockSpec((CHUNK, d), lambda i: (i, 0))
    vec = pl.BlockSpec((1, d), lambda i: (0, 0))
    return pl.pallas_call(
        body, name=name, grid=(nb,),
        in_specs=[row, vec, pl.BlockSpec((CHUNK, d), lambda i: (jnp.maximum(i - 1, 0), 0))],
        out_specs=[row, row, vec, pl.BlockSpec((1, 128), lambda i: (0, 0))],
        out_shape=[jax.ShapeDtypeStruct((r, d), F32), jax.ShapeDtypeStruct((r, d), BF16),
                   jax.ShapeDtypeStruct((1, d), F32), jax.ShapeDtypeStruct((1, 128), F32)],
        compiler_params=_params(),
    )(h, w, target)


def _ffn_in(name, hf, wg_t, wu_t, dep=None, tm=1056, tn=512):
    r, d = hf.shape
    f = wg_t.shape[0]
    tm, tn = _tile(r, tm, 8), _tile(f, tn, 128)

    def body(h_ref, wg_ref, wu_ref, *rest):
        g_ref, u_ref, a_ref = rest[-3:]
        x = h_ref[...]
        g = lax.dot_general(x, wg_ref[...], (NT, ((), ())), preferred_element_type=F32)
        u = lax.dot_general(x, wu_ref[...], (NT, ((), ())), preferred_element_type=F32)
        g_ref[...] = g.astype(BF16)
        u_ref[...] = u.astype(BF16)
        a_ref[...] = (g * _sigmoid(g) * u).astype(BF16)

    wspec = pl.BlockSpec((tn, d), lambda i, j: (j, 0))
    ospec = pl.BlockSpec((tm, tn), lambda i, j: (i, j))
    ins, specs = [hf, wg_t, wu_t], [pl.BlockSpec((tm, d), lambda i, j: (i, 0)), wspec, wspec]
    if dep is not None:
        ins.append(dep)
        specs.append(pl.BlockSpec((8, 128), lambda *_: (0, 0)))
    return pl.pallas_call(
        body, name=name, grid=(r // tm, f // tn), in_specs=specs, out_specs=[ospec] * 3,
        out_shape=[jax.ShapeDtypeStruct((r, f), BF16)] * 3, compiler_params=_params(),
    )(*ins)


def _ffn_act_bwd(name, dh, wd, g, u, dep=None, tm=1056, tn=512):
    r, d = dh.shape
    f = wd.shape[0]
    tm, tn = _tile(r, tm, 8), _tile(f, tn, 128)

    def body(dh_ref, wd_ref, g_ref, u_ref, *rest):
        dg_ref, du_ref = rest[-2:]
        tr = _tile(tm, 264, 8)
        for r0 in range(0, tm, tr):
            for c0 in range(0, tn, V7X_MXU_COLS):
                rows, cols = slice(r0, r0 + tr), slice(c0, c0 + V7X_MXU_COLS)
                da = lax.dot_general(dh_ref[rows, :], wd_ref[cols, :], (NT, ((), ())), preferred_element_type=F32)
                gv = g_ref[rows, cols].astype(F32)
                s = _sigmoid(gv)
                t = da * s
                du_ref[rows, cols] = (t * gv).astype(BF16)
                dg_ref[rows, cols] = (t * u_ref[rows, cols].astype(F32) * (1.0 + gv - gv * s)).astype(BF16)

    tile = pl.BlockSpec((tm, tn), lambda i, j: (i, j))
    ins = [dh, wd, g, u]
    specs = [pl.BlockSpec((tm, d), lambda i, j: (i, 0)), pl.BlockSpec((tn, d), lambda i, j: (j, 0)), tile, tile]
    if dep is not None:
        ins.append(dep)
        specs.append(pl.BlockSpec((8, 128), lambda *_: (0, 0)))
    return pl.pallas_call(
        body, name=name, grid=(r // tm, f // tn), in_specs=specs, out_specs=[tile] * 2,
        out_shape=[jax.ShapeDtypeStruct((r, f), BF16)] * 2, compiler_params=_params(),
    )(*ins)


def _shift_down(a, k):
    row = lax.broadcasted_iota(jnp.int32, a.shape, 0)
    return jnp.where(row >= k, pltpu.roll(a, k, 0), 0.0)


def _shift_up(a, k):
    n = a.shape[0]
    row = lax.broadcasted_iota(jnp.int32, a.shape, 0)
    return jnp.where(row < n - k, pltpu.roll(a, n - k, 0), 0.0)


def _conv_fwd(name, pc, cw):
    _, r, w = pc.shape

    def body(pc_ref, cw_ref, o_ref):
        a = pc_ref[2] * pc_ref[0]
        cwv = cw_ref[...]
        conv = _shift_down(a, 2) * cwv[0:1] + _shift_down(a, 1) * cwv[1:2] + a * cwv[2:3]
        o_ref[...] = (pc_ref[1] * conv).astype(BF16)

    return pl.pallas_call(
        body, name=name, grid=(w // 128,),
        in_specs=[pl.BlockSpec((3, r, 128), lambda j: (0, 0, j)), pl.BlockSpec((8, 128), lambda j: (0, j))],
        out_specs=pl.BlockSpec((r, 128), lambda j: (0, j)),
        out_shape=jax.ShapeDtypeStruct((r, w), BF16), compiler_params=_params(),
    )(pc, cw)


def _conv_bwd(name, dcat, pc, cw):
    _, r, w = pc.shape
    nblk = w // 128

    def body(dy_ref, pc_ref, cw_ref, dpc_ref, dcw_ref):
        u, gb, gc = pc_ref[0], pc_ref[1], pc_ref[2]
        cwv = cw_ref[...]
        dy = dy_ref[...]
        a = gc * u
        a1, a2 = _shift_down(a, 1), _shift_down(a, 2)
        conv = a2 * cwv[0:1] + a1 * cwv[1:2] + a * cwv[2:3]
        dconv = dy * gb
        da = dconv * cwv[2:3] + _shift_up(dconv, 1) * cwv[1:2] + _shift_up(dconv, 2) * cwv[0:1]
        dpc_ref[0] = (da * gc).astype(BF16)
        dpc_ref[1] = (dy * conv).astype(BF16)
        dpc_ref[2] = (da * u).astype(BF16)
        row = lax.broadcasted_iota(jnp.int32, (8, 128), 0)
        dw0 = jnp.sum(dconv * a2, axis=0, keepdims=True)
        dw1 = jnp.sum(dconv * a1, axis=0, keepdims=True)
        dw2 = jnp.sum(dconv * a, axis=0, keepdims=True)
        dcw_ref[...] = jnp.where(row == 0, dw0, jnp.where(row == 1, dw1, jnp.where(row == 2, dw2, 0.0)))

    return pl.pallas_call(
        body, name=name, grid=(nblk,),
        in_specs=[pl.BlockSpec((r, 128), lambda j: (0, nblk + j)),
                  pl.BlockSpec((3, r, 128), lambda j: (0, 0, j)), pl.BlockSpec((8, 128), lambda j: (0, j))],
        out_specs=[pl.BlockSpec((3, r, 128), lambda j: (0, 0, j)), pl.BlockSpec((8, 128), lambda j: (0, j))],
        out_shape=[jax.ShapeDtypeStruct((3, r, w), BF16), jax.ShapeDtypeStruct((8, w), F32)],
        compiler_params=_params(),
    )(dcat, pc, cw)


def _dot(a, b, dims):
    return lax.dot_general(a, b, (dims, ((), ())), preferred_element_type=F32)


def _col_to_row(xc, eye):
    return jnp.sum(jnp.where(eye, xc, 0.0), axis=0, keepdims=True)


def _row_to_col(xr, eye):
    return jnp.sum(jnp.where(eye, xr, 0.0), axis=1, keepdims=True)


def _gate_tiles(graw, bias, row0):
    th = jnp.tanh((graw + bias) / GATE_CAP)
    z = GATE_CAP * th
    row = lax.broadcasted_iota(jnp.int32, graw.shape, 0) + row0
    real = row >= PAD_FRONT
    li = jnp.where(real, z, -jnp.inf)
    lf = jnp.where(real, jnp.minimum(z, 0.0) - jnp.log(1.0 + jnp.exp(-jnp.abs(z))), 0.0)
    return th, z, li, lf, real


def _interleave(gens):
    results = [None] * len(gens)
    live = list(enumerate(gens))
    while live:
        still = []
        for i, gen in live:
            try:
                next(gen)
                still.append((i, gen))
            except StopIteration as stop:
                results[i] = stop.value
        live = still
    return results


def _chunk_common(pm, h, li, lf, cst, nst, mst, tril, eye):
    kraw = pm[:, QK_W + h * DQK:QK_W + (h + 1) * DQK]
    q = (pm[:, h * DQK:(h + 1) * DQK] * QSCALE).astype(BF16)
    yield
    k = kraw.astype(BF16)
    v = pm[:, 2 * QK_W + h * DV:2 * QK_W + (h + 1) * DV].astype(BF16)
    yield
    li_c = li[:, h:h + 1]
    lf_c = lf[:, HEADS + h:HEADS + h + 1]
    li_r = _col_to_row(li_c, eye)
    yield
    lf_r = _col_to_row(lf_c, eye)
    yield
    b_c = jnp.sum(jnp.where(tril, lf_r, 0.0), axis=1, keepdims=True)
    yield
    b_r = _col_to_row(b_c, eye)
    yield
    dmat = jnp.where(tril, b_c - b_r + li_r, -jnp.inf)
    inter = b_c + mst
    yield
    mt = jnp.maximum(inter, jnp.max(dmat, axis=1, keepdims=True))
    yield
    w_inter = jnp.exp(inter - mt)
    p = jnp.exp(dmat - mt)
    yield
    s = _dot(q, k, NT) * p
    yield
    cb = cst.astype(BF16)
    nb = nst.astype(BF16).astype(F32)
    qc = _dot(q, cb, NN)
    yield
    qn = jnp.sum(q.astype(F32) * nb, axis=1, keepdims=True)
    yield
    den = w_inter * qn + jnp.sum(s, axis=1, keepdims=True)
    yield
    dn = jnp.maximum(jnp.abs(den), jnp.exp(-mt))
    b_end = b_c[CHUNK - 1:CHUNK, :]
    decay = b_end - b_c + li_c
    yield
    m_new = jnp.maximum(b_end + mst, jnp.max(decay, axis=0, keepdims=True))
    yield
    w_old = jnp.exp(b_end + mst - m_new)
    w_in = jnp.exp(decay - m_new)
    kw = (w_in * kraw).astype(BF16)
    yield
    return dict(q=q, k=k, v=v, kraw=kraw, mt=mt, w_inter=w_inter, p=p, s=s, cb=cb, nb=nb, qc=qc, qn=qn,
                den=den, dn=dn, m_new=m_new, w_old=w_old, w_in=w_in, kw=kw)


def _chunks_per_step(nc):
    return 1


def _mlstm_fwd(name, pm, bias, nw):
    r = pm.shape[0]
    nc = r // CHUNK
    grp = _chunks_per_step(nc)

    def body(pm_ref, b_ref, nw_ref, hm_ref, ht_ref, cs_ref, ns_ref, ms_ref, c_scr, n_scr, m_scr):
        step = pl.program_id(0)

        @pl.when(step == 0)
        def _():
            c_scr[...] = jnp.zeros_like(c_scr)
            n_scr[...] = jnp.zeros_like(n_scr)
            m_scr[...] = jnp.zeros_like(m_scr)

        rr = lax.broadcasted_iota(jnp.int32, (CHUNK, CHUNK), 0)
        cc = lax.broadcasted_iota(jnp.int32, (CHUNK, CHUNK), 1)
        tril, eye = cc <= rr, cc == rr
        bv, nwv = b_ref[...], nw_ref[...]
        states = [(c_scr[h], n_scr[h], m_scr[h]) for h in range(HEADS)]
        for g in range(grp):
            rows = slice(g * CHUNK, (g + 1) * CHUNK)
            pmv = pm_ref[rows, :]
            _, _, li, lf, _ = _gate_tiles(pmv[:, GATE_COL:GATE_COL + 128], bv, (step * grp + g) * CHUNK)
            def head(h, cst, nst, mst, g=g, rows=rows, pmv=pmv, li=li, lf=lf):
                f = yield from _chunk_common(pmv, h, li, lf, cst, nst, mst, tril, eye)
                num = f["w_inter"] * f["qc"] + _dot(f["s"].astype(BF16), f["v"], NN)
                yield
                hh = num / f["dn"]
                yield
                c_new = f["w_old"] * cst + _dot(f["kw"], f["v"], TN)
                yield
                n_new = f["w_old"] * nst + jnp.sum(
                    f["w_in"].astype(BF16).astype(F32) * f["k"].astype(F32), axis=0, keepdims=True)
                yield
                sl = slice(h * DV, (h + 1) * DV)
                rs = lax.rsqrt(jnp.mean(hh * hh, axis=1, keepdims=True) + EPS)
                yield
                og = pmv[:, 2 * QK_W + MLSTM_W + h * DV:2 * QK_W + MLSTM_W + (h + 1) * DV]
                cs_ref[g, h] = cst
                ns_ref[g, h] = nst
                ms_ref[g, h] = mst
                ht_ref[rows, sl] = hh
                yield
                hm_ref[rows, sl] = (_sigmoid(og) * (hh * rs * nwv[:, sl])).astype(BF16)
                return c_new, n_new, f["m_new"]

            states = _interleave([head(h, *states[h]) for h in range(HEADS)])
        for h, (cst, nst, mst) in enumerate(states):
            c_scr[h] = cst
            n_scr[h] = nst
            m_scr[h] = mst

    return pl.pallas_call(
        body, name=name, grid=(nc // grp,),
        in_specs=[pl.BlockSpec((grp * CHUNK, PM_W), lambda i: (i, 0)), pl.BlockSpec((1, 128), lambda i: (0, 0)),
                  pl.BlockSpec((1, MLSTM_W), lambda i: (0, 0))],
        out_specs=[pl.BlockSpec((grp * CHUNK, MLSTM_W), lambda i: (i, 0)),
                   pl.BlockSpec((grp * CHUNK, MLSTM_W), lambda i: (i, 0)),
                   pl.BlockSpec((grp, HEADS, DQK, DV), lambda i: (i, 0, 0, 0)),
                   pl.BlockSpec((grp, HEADS, 1, DQK), lambda i: (i, 0, 0, 0)),
                   pl.BlockSpec((grp, HEADS, 1, 1), lambda i: (i, 0, 0, 0))],
        out_shape=[jax.ShapeDtypeStruct((r, MLSTM_W), BF16), jax.ShapeDtypeStruct((r, MLSTM_W), F32),
                   jax.ShapeDtypeStruct((nc, HEADS, DQK, DV), F32),
                   jax.ShapeDtypeStruct((nc, HEADS, 1, DQK), F32),
                   jax.ShapeDtypeStruct((nc, HEADS, 1, 1), F32)],
        scratch_shapes=[pltpu.VMEM((HEADS, DQK, DV), F32), pltpu.VMEM((HEADS, 1, DQK), F32),
                        pltpu.VMEM((HEADS, 1, 1), F32)],
        compiler_params=_params(),
    )(pm, bias, nw)


def _mlstm_bwd(name, dcat, pm, ht, cs, ns, ms, bias, nw):
    r = pm.shape[0]
    nc = r // CHUNK
    grp = _chunks_per_step(nc)
    nsteps = nc // grp

    def body(dy_ref, pm_ref, ht_ref, cs_ref, ns_ref, ms_ref, b_ref, nw_ref, dpm_ref, dnw_ref, db_ref,
             dc_scr, dn_scr):
        step = pl.program_id(0)

        @pl.when(step == 0)
        def _():
            dc_scr[...] = jnp.zeros_like(dc_scr)
            dn_scr[...] = jnp.zeros_like(dn_scr)
            dnw_ref[...] = jnp.zeros_like(dnw_ref)
            db_ref[...] = jnp.zeros_like(db_ref)

        rr = lax.broadcasted_iota(jnp.int32, (CHUNK, CHUNK), 0)
        cc = lax.broadcasted_iota(jnp.int32, (CHUNK, CHUNK), 1)
        tril, eye, triu = cc <= rr, cc == rr, cc >= rr
        lane = lax.broadcasted_iota(jnp.int32, (CHUNK, 128), 1)
        rowid = lax.broadcasted_iota(jnp.int32, (CHUNK, 1), 0)
        bv, nwv = b_ref[...], nw_ref[...]
        carried = [(dc_scr[h], dn_scr[h]) for h in range(HEADS)]
        dnw_acc = [jnp.zeros((1, DV), F32) for _ in range(HEADS)]
        db_acc = jnp.zeros((1, 128), F32)
        for g in reversed(range(grp)):
            rows = slice(g * CHUNK, (g + 1) * CHUNK)
            ci = (nsteps - 1 - step) * grp + g
            pmv = pm_ref[rows, :]
            th, z, li, lf, real = _gate_tiles(pmv[:, GATE_COL:GATE_COL + 128], bv, ci * CHUNK)
            heads = _interleave([
                _mlstm_bwd_head(h, pmv, ht_ref[rows, h * DV:(h + 1) * DV], dy_ref[rows, h * DV:(h + 1) * DV], nwv,
                                li, lf, cs_ref[g, h], ns_ref[g, h], ms_ref[g, h], carried[h][0], carried[h][1],
                                tril, eye, triu, lane, rowid, dpm_ref, rows)
                for h in range(HEADS)])
            carried = [(dc_new, dn_new) for _, dc_new, dn_new, _ in heads]
            dgt = heads[0][0] + heads[1][0] + heads[2][0] + heads[3][0]
            dnw_acc = [dnw_acc[h] + heads[h][3] for h in range(HEADS)]
            dact = jnp.where(lane < HEADS, 1.0, 1.0 - _sigmoid(z)) * (1.0 - th * th)
            dgraw = jnp.where(real & (lane < 2 * HEADS), dgt * dact, 0.0)
            dpm_ref[rows, GATE_COL:GATE_COL + 128] = dgraw.astype(BF16)
            db_acc = db_acc + jnp.sum(dgraw, axis=0, keepdims=True)
        for h, (dcn, dnn) in enumerate(carried):
            dc_scr[h] = dcn
            dn_scr[h] = dnn
            dnw_ref[:, h * DV:(h + 1) * DV] += dnw_acc[h]
        db_ref[...] += db_acc

    rev = lambda i: (nsteps - 1 - i, 0)
    rev4 = lambda i: (nsteps - 1 - i, 0, 0, 0)
    return pl.pallas_call(
        body, name=name, grid=(nsteps,),
        in_specs=[pl.BlockSpec((grp * CHUNK, MLSTM_W), rev), pl.BlockSpec((grp * CHUNK, PM_W), rev),
                  pl.BlockSpec((grp * CHUNK, MLSTM_W), rev),
                  pl.BlockSpec((grp, HEADS, DQK, DV), rev4), pl.BlockSpec((grp, HEADS, 1, DQK), rev4),
                  pl.BlockSpec((grp, HEADS, 1, 1), rev4),
                  pl.BlockSpec((1, 128), lambda i: (0, 0)), pl.BlockSpec((1, MLSTM_W), lambda i: (0, 0))],
        out_specs=[pl.BlockSpec((grp * CHUNK, PM_W), rev), pl.BlockSpec((1, MLSTM_W), lambda i: (0, 0)),
                   pl.BlockSpec((1, 128), lambda i: (0, 0))],
        out_shape=[jax.ShapeDtypeStruct((r, PM_W), BF16), jax.ShapeDtypeStruct((1, MLSTM_W), F32),
                   jax.ShapeDtypeStruct((1, 128), F32)],
        scratch_shapes=[pltpu.VMEM((HEADS, DQK, DV), F32), pltpu.VMEM((HEADS, 1, DQK), F32)],
        compiler_params=_params(),
    )(dcat, pm, ht, cs, ns, ms, bias, nw)


def _mlstm_bwd_head(h, pmv, hh, y, nwv, li, lf, cst, nst, mst, dcn, dnn, tril, eye, triu, lane, rowid,
                    dpm_ref, rows):
    f = yield from _chunk_common(pmv, h, li, lf, cst, nst, mst, tril, eye)
    q, k, v, s, p = f["q"], f["k"], f["v"], f["s"], f["p"]
    w_inter, w_in, w_old, dn = f["w_inter"], f["w_in"], f["w_old"], f["dn"]
    osl = slice(2 * QK_W + MLSTM_W + h * DV, 2 * QK_W + MLSTM_W + (h + 1) * DV)
    sg = _sigmoid(pmv[:, osl])
    yield
    rs = lax.rsqrt(jnp.mean(hh * hh, axis=1, keepdims=True) + EPS)
    yield
    nwh = nwv[:, h * DV:(h + 1) * DV]
    dpm_ref[rows, osl] = (y * (hh * rs * nwh) * sg * (1.0 - sg)).astype(BF16)
    yield
    dhn = y * sg
    dnw_h = jnp.sum(dhn * hh * rs, axis=0, keepdims=True)
    yield
    wd = dhn * nwh
    dhh = rs * wd - hh * (rs * rs * rs) * jnp.mean(hh * wd, axis=1, keepdims=True)
    yield
    dnum = dhh / dn
    dd = -jnp.sum(dhh * hh, axis=1, keepdims=True) / dn
    yield
    dden = jnp.where(jnp.abs(f["den"]) > jnp.exp(-f["mt"]), dd * jnp.sign(f["den"]), 0.0)
    dnum_b = dnum.astype(BF16)
    wdn = (w_inter * dnum).astype(BF16)
    wid = (w_inter * dden).astype(BF16).astype(F32)
    yield
    ds = _dot(dnum_b, v, NT) + dden
    yield
    dsp = (ds * p).astype(BF16)
    yield
    dq = _dot(dsp, k, NN) + _dot(wdn, f["cb"], NT) + wid * f["nb"]
    yield
    dk = _dot(dsp, q, TN)
    yield
    dv = _dot(s.astype(BF16), dnum_b, TN)
    yield
    g = ds * s
    g_col = _row_to_col(jnp.sum(g, axis=0, keepdims=True), eye)
    yield
    db = jnp.sum(g, axis=1, keepdims=True) - g_col
    dli = g_col
    yield
    db = db + (jnp.sum(dnum * f["qc"], axis=1, keepdims=True) + dden * f["qn"]) * w_inter
    yield
    dcnb = dcn.astype(BF16)
    dnnb = dnn.astype(BF16).astype(F32)
    dkw = _dot(v, dcnb, NT) + dnnb
    yield
    dk = dk + w_in * dkw
    dv = dv + _dot(f["kw"], dcnb, NN)
    yield
    ddecay = jnp.sum(dkw * f["kraw"], axis=1, keepdims=True) * w_in
    yield
    dw_old = (jnp.sum(jnp.sum(dcn * cst, axis=1, keepdims=True), axis=0, keepdims=True)
              + jnp.sum(dnn * nst, axis=1, keepdims=True))
    yield
    db_end = dw_old * w_old + jnp.sum(ddecay, axis=0, keepdims=True)
    db = db - ddecay + jnp.where(rowid == CHUNK - 1, db_end, 0.0)
    dli = dli + ddecay
    yield
    dc_new = w_old * dcn + _dot(q, wdn, TN)
    yield
    dn_new = w_old * dnn + jnp.sum(wid * q.astype(F32), axis=0, keepdims=True)
    yield
    dlf = jnp.sum(jnp.where(triu, _col_to_row(db, eye), 0.0), axis=1, keepdims=True)
    yield
    gate_part = jnp.where(lane == h, dli, 0.0) + jnp.where(lane == HEADS + h, dlf, 0.0)
    dpm_ref[rows, h * DQK:(h + 1) * DQK] = (dq * QSCALE).astype(BF16)
    yield
    dpm_ref[rows, QK_W + h * DQK:QK_W + (h + 1) * DQK] = dk.astype(BF16)
    yield
    dpm_ref[rows, 2 * QK_W + h * DV:2 * QK_W + (h + 1) * DV] = dv.astype(BF16)
    return gate_part, dc_new, dn_new, dnw_h


def _my_place():
    return lax.axis_index("x"), lax.axis_index("y"), lax.axis_index("c")


def _flip(v, bit):
    return 1 - v if bit else v


def _exchange_small(name, blk, reduce):
    r, c = blk.shape

    def body(x_ref, o_ref, *rest):
        slots = rest[0] if reduce else o_ref
        send_sems, recv_sems = rest[-2], rest[-1]
        x, y, cc = _my_place()
        me = 4 * x + 2 * y + cc
        slots[me] = x_ref[...]
        copies = []
        for k in range(1, N_DEV):
            peer = (_flip(x, k & 4), _flip(y, k & 2), _flip(cc, k & 1))
            cp = pltpu.make_async_remote_copy(
                src_ref=x_ref, dst_ref=slots.at[me], send_sem=send_sems.at[k - 1],
                recv_sem=recv_sems.at[k - 1], device_id=peer, device_id_type=MESH)
            cp.start()
            copies.append(cp)
        for cp in copies:
            cp.wait()
        if reduce:
            acc = slots[0]
            for d in range(1, N_DEV):
                acc = acc + slots[d]
            o_ref[...] = acc

    scratch = ([pltpu.VMEM((N_DEV, r, c), F32)] if reduce else []) + [
        pltpu.SemaphoreType.DMA((N_DEV - 1,)), pltpu.SemaphoreType.DMA((N_DEV - 1,))]
    return pl.pallas_call(
        body, name=name,
        out_shape=jax.ShapeDtypeStruct((r, c) if reduce else (N_DEV, r, c), F32),
        in_specs=[pl.BlockSpec(memory_space=pltpu.VMEM)], out_specs=pl.BlockSpec(memory_space=pltpu.VMEM),
        scratch_shapes=scratch, compiler_params=_params(),
    )(blk)


HBM_SPEC = pl.BlockSpec(memory_space=pltpu.HBM)
SEM_SPEC = pl.BlockSpec(memory_space=pltpu.SEMAPHORE)
ANY_SPEC = pl.BlockSpec(memory_space=pl.ANY)
DATAFLOW = pltpu.SideEffectType.DATAFLOW_SIDE_EFFECTING


def _split_copy(name, arrays, start=None, wait=None, after=None):
    results, token = _split_copies(name, [(arrays, start, wait)], after)
    return results[0][0], results[0][1], token


def _split_copies(name, jobs, after=None):
    operands, in_specs, out_shape, out_specs, aliases = [], [], [], [], {}
    in_at, out_at = [], []
    for arrays, start, wait in jobs:
        in_at.append(len(operands))
        operands += [pltpu.with_memory_space_constraint(a, pltpu.HBM) for a in arrays]
        in_specs += [HBM_SPEC] * len(arrays)
        if wait:
            operands += list(wait[1])
            in_specs += [SEM_SPEC, SEM_SPEC]
    if after is not None:
        operands.append(after)
        in_specs.append(ANY_SPEC)
    for j, (arrays, start, wait) in enumerate(jobs):
        out_at.append(len(out_shape))
        if start:
            out_shape += [pltpu.SemaphoreType.DMA((start[1],)), pltpu.SemaphoreType.DMA((start[1],))]
            out_specs += [SEM_SPEC, SEM_SPEC]
        for i, a in enumerate(arrays):
            aliases[in_at[j] + i] = len(out_shape)
            out_shape.append(pltpu.HBM(a.shape, a.dtype))
            out_specs.append(HBM_SPEC)
    any_start = any(start for _, start, _ in jobs)
    if any_start:
        out_shape.append(jax.ShapeDtypeStruct((8, 128), F32))
        out_specs.append(pl.BlockSpec(memory_space=pltpu.VMEM))
    n_in = len(operands)

    def body(*refs):
        for j, (arrays, start, wait) in enumerate(jobs):
            if wait:
                ins = refs[in_at[j]:in_at[j] + len(arrays)]
                at = in_at[j] + len(arrays)
                for cp in wait[0](ins, refs[at], refs[at + 1]):
                    cp.wait_send()
                    cp.wait_recv()
        for j, (arrays, start, wait) in enumerate(jobs):
            if start:
                ins = refs[in_at[j]:in_at[j] + len(arrays)]
                at = n_in + out_at[j]
                for cp in start[0](ins, refs[at], refs[at + 1]):
                    cp.start()
        if any_start:
            token = refs[n_in + len(out_shape) - 1]
            token[...] = jnp.zeros_like(token)

    outs = pl.pallas_call(
        body, name=name, in_specs=in_specs, out_specs=out_specs, out_shape=out_shape,
        input_output_aliases=aliases, compiler_params=pltpu.CompilerParams(has_side_effects=DATAFLOW),
    )(*operands)
    results = []
    for j, (arrays, start, wait) in enumerate(jobs):
        at = out_at[j]
        sems = (outs[at], outs[at + 1]) if start else None
        at += 2 if start else 0
        results.append((list(outs[at:at + len(arrays)]), sems))
    return results, (outs[-1] if any_start else None)


def _remote(src, dst, send_sems, recv_sems, k, to):
    return pltpu.make_async_remote_copy(src_ref=src, dst_ref=dst, send_sem=send_sems.at[k],
                                        recv_sem=recv_sems.at[k], device_id=to, device_id_type=MESH)


def _slot(px, py, pc):
    return 4 * px + 2 * py + pc


def _gather_first(refs, send_sems, recv_sems):
    x, y, c = _my_place()
    blk = refs[0].at[_slot(x, y, c)]
    targets = [(x, y, 1 - c), (1 - x, y, c), (x, 1 - y, c)]
    return [_remote(blk, blk, send_sems, recv_sems, k, to) for k, to in enumerate(targets)]


def _gather_relay(refs, send_sems, recv_sems):
    x, y, c = _my_place()
    rows = refs[0].shape[1]
    half = rows // 32 * 16
    from_x, from_y = _slot(1 - x, y, c), _slot(x, 1 - y, c)
    upper = refs[0].at[from_x, pl.ds(0, half)]
    lower = refs[0].at[from_y, pl.ds(half, rows - half)]
    return [_remote(upper, upper, send_sems, recv_sems, 0, (x, 1 - y, c)),
            _remote(lower, lower, send_sems, recv_sems, 1, (1 - x, y, c)),
            _remote(refs[0].at[from_x], refs[0].at[from_x], send_sems, recv_sems, 2, (x, y, 1 - c)),
            _remote(refs[0].at[from_y], refs[0].at[from_y], send_sems, recv_sems, 3, (x, y, 1 - c))]


def _gather_last(refs, send_sems, recv_sems):
    x, y, c = _my_place()
    blk = refs[0].at[_slot(1 - x, 1 - y, c)]
    return [_remote(blk, blk, send_sems, recv_sems, 0, (x, y, 1 - c))]


def _scatter_sibling(n):
    def copies(refs, send_sems, recv_sems):
        x, y, c = _my_place()
        return [_remote(refs[a].at[2 * j + 1 - c], refs[n + a].at[j], send_sems, recv_sems, 4 * a + j, (x, y, 1 - c))
                for a in range(n) for j in range(4)]
    return copies


def _scatter_chips(n):
    def copies(refs, send_sems, recv_sems):
        x, y, c = _my_place()
        out = []
        for a in range(n):
            for k in range(1, 4):
                px, py = _flip(x, k & 2), _flip(y, k & 1)
                out.append(_remote(refs[a].at[2 * px + py], refs[n + a].at[2 * x + y], send_sems, recv_sems,
                                   3 * a + k - 1, (px, py, c)))
        return out
    return copies


def _pair_sum(name, core, g, t):
    _, r, c = g.shape
    tr = _tile(r, 512, 8)
    g4 = g.reshape(4, 2, r, c)

    def body(core_ref, g_ref, t_ref, o_ref):
        o_ref[...] = (g_ref[...].astype(F32) + t_ref[...].astype(F32)).astype(BF16)

    return pl.pallas_call(
        body, name=name,
        grid_spec=pltpu.PrefetchScalarGridSpec(
            num_scalar_prefetch=1, grid=(4, r // tr),
            in_specs=[pl.BlockSpec((None, None, tr, c), lambda j, i, core_ref: (j, core_ref[0], i, 0)),
                      pl.BlockSpec((None, tr, c), lambda j, i, core_ref: (j, i, 0))],
            out_specs=pl.BlockSpec((None, tr, c), lambda j, i, core_ref: (j, i, 0))),
        out_shape=jax.ShapeDtypeStruct((4, r, c), BF16), compiler_params=_params(),
    )(core, g4, t)


BF16_TILE_ROWS = 16
IN_STRIDE = IN_SH // BF16_TILE_ROWS * BF16_TILE_ROWS
IN_WIN = IN_STRIDE + BF16_TILE_ROWS
IN_DENSE = IN_STRIDE * (N_DEV - 1) + IN_WIN
assert IN_SH * (N_DEV - 1) - IN_STRIDE * (N_DEV - 1) + IN_SH <= IN_WIN and D_IN <= IN_DENSE


def _window(ref, d):
    return ref.at[pl.ds(pl.multiple_of(IN_STRIDE * d, BF16_TILE_ROWS), IN_WIN)]


def _scatter_sibling_windows(refs, send_sems, recv_sems):
    x, y, c = _my_place()
    return [_remote(_window(refs[0], 2 * j + 1 - c), refs[1].at[j], send_sems, recv_sems, j, (x, y, 1 - c))
            for j in range(4)]


def _pair_sum_windows(name, core, g, t):
    c = g.shape[1]

    def body(core_ref, g_ref, t_ref, o_ref, buf, sem):
        copy = pltpu.make_async_copy(_window(g_ref, 2 * pl.program_id(0) + core_ref[0]), buf, sem)
        copy.start()
        copy.wait()
        o_ref[...] = (buf[...].astype(F32) + t_ref[...].astype(F32)).astype(BF16)

    slot = pl.BlockSpec((None, IN_WIN, c), lambda j, core_ref: (j, 0, 0))
    return pl.pallas_call(
        body, name=name,
        grid_spec=pltpu.PrefetchScalarGridSpec(
            num_scalar_prefetch=1, grid=(4,), in_specs=[ANY_SPEC, slot], out_specs=slot,
            scratch_shapes=[pltpu.VMEM((IN_WIN, c), BF16), pltpu.SemaphoreType.DMA]),
        out_shape=jax.ShapeDtypeStruct((4, IN_WIN, c), BF16), compiler_params=_params(),
    )(core, g, t)


def _adam_math(w, g, m, v):
    m2 = ADAM_B1 * m + (1.0 - ADAM_B1) * g
    v2 = ADAM_B2 * v + (1.0 - ADAM_B2) * (g * g)
    m_hat = m2 / (1.0 - ADAM_B1 ** ADAM_STEP)
    v_hat = v2 / (1.0 - ADAM_B2 ** ADAM_STEP)
    delta = -ADAM_LR * (m_hat / (jnp.sqrt(v_hat) + ADAM_EPS) + ADAM_WD * w)
    return delta, m2, v2


def _adam_sharded(name, chip, w, m, v, grads, row_off=0):
    _, r, c = w.shape
    tr = _tile(r, 256, 8)
    tc = c if tr < r else _tile(c, 256, 128)
    boff = row_off // tr

    def body(chip_ref, w_ref, m_ref, v_ref, p0_ref, q0_ref, p1_ref, q1_ref, g_ref, d_ref, nm_ref, nv_ref):
        mine = chip_ref[0]

        def total(p_ref, q_ref):
            acc = None
            for j in range(4):
                part = jnp.where(mine == j, p_ref[...], q_ref[j]).astype(F32)
                acc = part if acc is None else acc + part
            return acc

        g = jnp.where(pl.program_id(0) == 0, total(p0_ref, q0_ref), total(p1_ref, q1_ref))
        delta, m2, v2 = _adam_math(w_ref[...], g, m_ref[...], v_ref[...])
        g_ref[...] = g
        d_ref[...] = delta
        nm_ref[...] = m2
        nv_ref[...] = v2

    def grad_specs(layer):
        at = lambda l, i, j: (jnp.where(l == layer, boff + i, boff), jnp.where(l == layer, j, 0))
        return [pl.BlockSpec((None, tr, tc), lambda l, i, j, chip_ref: (chip_ref[0],) + at(l, i, j)),
                pl.BlockSpec((4, tr, tc), lambda l, i, j, chip_ref: (0,) + at(l, i, j))]

    wspec = pl.BlockSpec((None, tr, tc), lambda l, i, j, chip_ref: (l, i, j))
    sds = jax.ShapeDtypeStruct(w.shape, F32)
    return pl.pallas_call(
        body, name=name,
        grid_spec=pltpu.PrefetchScalarGridSpec(
            num_scalar_prefetch=1, grid=(2, r // tr, c // tc),
            in_specs=[wspec, wspec, wspec] + grad_specs(0) + grad_specs(1), out_specs=[wspec] * 4),
        out_shape=[sds] * 4, compiler_params=_params(),
    )(chip, w, m, v, grads[0][0], grads[0][1], grads[1][0], grads[1][1])


def _adam_small(name, w, m, v, g):
    def body(w_ref, m_ref, v_ref, g_ref, d_ref, nm_ref, nv_ref):
        delta, m2, v2 = _adam_math(w_ref[...], g_ref[...], m_ref[...], v_ref[...])
        d_ref[...] = delta
        nm_ref[...] = m2
        nv_ref[...] = v2

    sds = jax.ShapeDtypeStruct(w.shape, F32)
    vm = pl.BlockSpec(memory_space=pltpu.VMEM)
    return pl.pallas_call(body, name=name, in_specs=[vm] * 4, out_specs=[vm] * 3, out_shape=[sds] * 3,
                          compiler_params=_params())(w, m, v, g)


GATE_END = GATE_COL + 2 * HEADS


def _merge_dw_in(dwm_t, dwc_t):
    return jnp.concatenate([dwm_t[:GATE_END], dwc_t.reshape(3 * CONV_W, D_MODEL),
                            jnp.zeros((IN_DENSE - D_IN, D_MODEL), BF16)], axis=0)


def _pack128(parts):
    flat = jnp.concatenate([p.reshape(-1) for p in parts])
    n = flat.shape[0]
    rows = -(-n // 1024) * 8
    return jnp.pad(flat, (0, rows * 128 - n)).reshape(rows, 128)


def _unpack128(packed, shapes):
    flat = packed.reshape(-1)
    out, at = [], 0
    for s in shapes:
        n = int(np.prod(s))
        out.append(flat[at:at + n].reshape(s))
        at += n
    return out


def kernel(x, meta_tokens, norm_mix_w, w_in, b_gates, conv_w, mlstm_norm_w, w_out, norm_ffn_w, w_gate, w_up, w_down, norm_final_w, loss_target, m_meta_tokens, m_norm_mix_w, m_w_in, m_b_gates, m_conv_w, m_mlstm_norm_w, m_w_out, m_norm_ffn_w, m_w_gate, m_w_up, m_w_down, m_norm_final_w, v_meta_tokens, v_norm_mix_w, v_w_in, v_b_gates, v_conv_w, v_mlstm_norm_w, v_w_out, v_norm_ffn_w, v_w_gate, v_w_up, v_w_down, v_norm_final_w):
    seq = x.shape[1]
    rows = TOK0 + seq
    me = 4 * lax.axis_index("x") + 2 * lax.axis_index("y") + lax.axis_index("c")
    meta_sh = meta_tokens.shape[1]
    conv_sh = conv_w.shape[2]

    w_gate_t, m_w_gate_t, v_w_gate_t = (jnp.transpose(a, (0, 2, 1)) for a in (w_gate, m_w_gate, v_w_gate))
    w_up_t, m_w_up_t, v_w_up_t = (jnp.transpose(a, (0, 2, 1)) for a in (w_up, m_w_up, v_w_up))
    shards = []
    for l in range(DEPTH):
        shards += [jnp.transpose(w_in[l]).astype(BF16), w_out[l].astype(BF16), w_gate_t[l].astype(BF16),
                   w_up_t[l].astype(BF16), w_down[l].astype(BF16)]
    per_layer = ("w_in", "w_out", "w_gate", "w_up", "w_down")
    gather_names = [f"{nm}_{l}" for l in range(DEPTH) for nm in per_layer]
    gather_state = {}

    def gather_step(tag, after, start=None, relay=None, last=None, done=()):
        jobs, idx = [], []
        if relay is not None and relay < len(shards):
            jobs.append((gather_state[relay][0], (_gather_relay, 4), (_gather_first, gather_state[relay][1])))
            idx.append(relay)
        if start is not None and start < len(shards):
            buf = lax.dynamic_update_index_in_dim(lax.empty((N_DEV,) + shards[start].shape, BF16), shards[start], me, 0)
            jobs.append(([buf], (_gather_first, 3), None))
            idx.append(start)
        if last is not None:
            jobs.append((gather_state[last][0], (_gather_last, 1), (_gather_relay, gather_state[last][1])))
            idx.append(last)
        for i in done:
            jobs.append((gather_state[i][0], None, (_gather_last, gather_state[i][1])))
            idx.append(i)
        if not jobs:
            return after, []
        results, tok = _split_copies(f"gather_{tag}", jobs, after)
        for i, res in zip(idx, results):
            gather_state[i] = res
        return (after if tok is None else tok), [gather_state[i][0][0] for i in done]

    bias = [jnp.pad(b_gates[l].reshape(1, 2 * HEADS), ((0, 0), (0, 128 - 2 * HEADS))) for l in range(DEPTH)]
    nmix = [norm_mix_w[l].reshape(1, D_MODEL) for l in range(DEPTH)]
    nffn = [norm_ffn_w[l].reshape(1, D_MODEL) for l in range(DEPTH)]
    nmls = [mlstm_norm_w[l].reshape(1, MLSTM_W) for l in range(DEPTH)]
    weights = [dict() for _ in range(DEPTH)]
    saved = [dict() for _ in range(DEPTH)]

    def layer_fwd(l, h, after):
        w, s = weights[l], saved[l]
        k0 = len(per_layer) * l
        tok, _ = gather_step(f"l{l}_a", after, last=k0)
        _, (g_in,) = gather_step(f"l{l}_b", tok, done=[k0])
        tok, _ = gather_step(f"l{l}_c", g_in, relay=k0 + 1, start=k0 + 3)
        w["win_t"] = g_in.reshape(D_IN, D_MODEL)
        w["wc_t"] = w["win_t"][GATE_END:].reshape(3, CONV_W, D_MODEL)
        s["h0"] = h
        s["hn"] = _rms_fwd(f"norm_mix_{l}", h, nmix[l] + tok[0, 0])
        s["pm"] = _mm_nt(f"proj_mlstm_{l}", s["hn"], w["win_t"], F32, tn=640, tk=D_MODEL, n=PM_W)
        tok, _ = gather_step(f"l{l}_d", s["pm"], relay=k0 + 2, start=k0 + 4)
        tok, _ = gather_step(f"l{l}_d2", tok, last=k0 + 1)
        s["pc"] = _mm_nt_bcols(f"proj_conv_{l}", s["hn"], w["wc_t"], F32, dep=tok)
        hm, s["ht"], s["cs"], s["ns"], s["ms"] = _mlstm_fwd(f"mlstm_fwd_{l}", s["pm"], bias[l] + tok[:1], nmls[l])
        tok, _ = gather_step(f"l{l}_e", hm, relay=k0 + 3, start=k0 + 5)
        tok, _ = gather_step(f"l{l}_e2", tok, last=k0 + 2)
        hc = _conv_fwd(f"conv_fwd_{l}", s["pc"], conv_rows[l] + tok[0, 0])
        s["cat"] = jnp.concatenate([hm, hc], axis=1)
        _, (g_out,) = gather_step(f"l{l}_f", s["cat"], done=[k0 + 1])
        w["wo"] = g_out.reshape(D_MODEL, D_MODEL)
        s["h1"] = _mm_nn(f"out_proj_{l}", s["cat"], w["wo"], F32, res=s["h0"])
        tok, _ = gather_step(f"l{l}_g", s["h1"], relay=k0 + 4, start=k0 + 6)
        s["hf"] = _rms_fwd(f"norm_ffn_{l}", s["h1"], nffn[l] + tok[0, 0])
        tok, _ = gather_step(f"l{l}_h", s["hf"], last=k0 + 3)
        _, (g_gate, g_up) = gather_step(f"l{l}_i", tok, done=[k0 + 2, k0 + 3])
        w["wg_t"] = g_gate.reshape(D_FF, D_MODEL)
        w["wu_t"] = g_up.reshape(D_FF, D_MODEL)
        s["g"], s["u"], s["act"] = _ffn_in(f"ffn_in_{l}", s["hf"], w["wg_t"], w["wu_t"])
        tok, _ = gather_step(f"l{l}_j", s["act"], last=k0 + 4)
        _, (g_down,) = gather_step(f"l{l}_k", tok, done=[k0 + 4])
        w["wd"] = g_down.reshape(D_FF, D_MODEL)
        tok, _ = gather_step(f"l{l}_k2", tok, relay=k0 + 5, start=k0 + 7)
        return _mm_nn(f"ffn_out_{l}", s["act"], w["wd"], F32, res=s["h1"], dep=tok)

    tok, _ = gather_step("first", None, start=0)
    zero = tok[0, 0]
    small = jnp.concatenate(
        [meta_tokens + zero, jnp.pad(conv_w.reshape(DEPTH * 3, conv_sh), ((0, 2), (0, meta_sh - conv_sh)))], axis=0)
    slots = _exchange_small("gather_small", small, reduce=False)
    meta_full = jnp.transpose(slots[:, :N_META, :], (1, 0, 2)).reshape(N_META, D_MODEL)
    conv_full = jnp.transpose(slots[:, N_META:N_META + DEPTH * 3, :conv_sh], (1, 0, 2)).reshape(DEPTH, 3, CONV_W)
    conv_rows = [jnp.pad(conv_full[l], ((0, 5), (0, 0))) for l in range(DEPTH)]
    w_in_t, m_w_in_t, v_w_in_t = (
        lax.dynamic_update_slice(jnp.zeros((DEPTH, IN_WIN, D_MODEL), F32), jnp.transpose(a + zero, (0, 2, 1)), (0, me, 0))
        for a in (w_in, m_w_in, v_w_in))
    tok, w_in_t, m_w_in_t, v_w_in_t, meta_full = lax.optimization_barrier(
        (tok, w_in_t, m_w_in_t, v_w_in_t, meta_full))
    tok, _ = gather_step("pre_a", tok, relay=0)
    tok, _ = gather_step("pre_b", tok, start=1)
    tok, _ = gather_step("pre_c", tok, start=2)
    h = jnp.concatenate([jnp.zeros((PAD_FRONT, D_MODEL), F32), meta_full, x[0]], axis=0)
    h = layer_fwd(0, h, tok)
    h = layer_fwd(1, h, h)

    dh, dh_b, d_final, loss_part = _final_loss("final_loss", h, norm_final_w.reshape(1, D_MODEL), loss_target[0])

    core = lax.axis_index("c").astype(jnp.int32).reshape(1)
    chip = (2 * lax.axis_index("x") + lax.axis_index("y")).astype(jnp.int32).reshape(1)
    scatter_state = {}

    def scatter_begin(nm, grad):
        dense = grad.ndim == 2
        land = lax.empty((4, IN_WIN, grad.shape[1]) if dense else (4,) + grad.shape[1:], BF16)
        copies = _scatter_sibling_windows if dense else _scatter_sibling(1)
        arrs, sems, tok = _split_copy(f"grad_sibling_start_{nm}", [grad, land], start=(copies, 4))
        scatter_state[nm] = (arrs, sems, copies)
        return tok

    def scatter_advance(nm, after):
        arrs, sems, copies = scatter_state[nm]
        arrs, _, _ = _split_copy(f"grad_sibling_done_{nm}", arrs, wait=(copies, sems), after=after)
        pair_sum = _pair_sum_windows if arrs[0].ndim == 2 else _pair_sum
        part = pair_sum(f"grad_pair_sum_{nm}", core, arrs[0], arrs[1])
        arrs, sems, tok = _split_copy(f"grad_chips_start_{nm}", [part, lax.empty(part.shape, BF16)],
                                      start=(_scatter_chips(1), 3))
        scatter_state[nm] = (arrs, sems)
        return tok

    def scattered(nm, after):
        arrs, sems = scatter_state[nm]
        arrs, _, _ = _split_copy(f"grad_chips_done_{nm}", arrs, wait=(_scatter_chips(1), sems), after=after)
        return arrs[0], arrs[1]

    d_mix, d_ffn, d_mls, d_bias, d_conv = ([None] * DEPTH for _ in range(5))

    def layer_bwd(l, dh, dh_b, tok):
        w, s = weights[l], saved[l]
        dg, du = _ffn_act_bwd(f"d_act_{l}", dh_b, w["wd"], s["g"], s["u"], dep=tok)
        dw_down = _mm_tn(f"dw_down_{l}", s["act"], dh_b, BF16, tm=1408, tn=1024)
        tok = scatter_begin(f"w_down_{l}", dw_down.reshape(N_DEV, FF_SH, D_MODEL))
        dhf = _mm_nn(f"d_ffn_gate_{l}", dg, w["wg_t"], F32, dep=tok)
        tok = scatter_advance(f"w_down_{l}", after=dhf)
        dhf = _mm_nn(f"d_ffn_up_{l}", du, w["wu_t"], F32, res=dhf, dep=tok)
        dw_gate = _mm_tn(f"dw_gate_{l}", dg, s["hf"], BF16, tm=1408, tn=1024)
        tok = scatter_begin(f"w_gate_{l}", dw_gate.reshape(N_DEV, FF_SH, D_MODEL))
        dw_up = _mm_tn(f"dw_up_{l}", du, s["hf"], BF16, tm=1408, tn=1024, dep=tok)
        tok = scatter_begin(f"w_up_{l}", dw_up.reshape(N_DEV, FF_SH, D_MODEL))
        dh1, dh1_b, d_ffn[l] = _rms_bwd(f"norm_ffn_bwd_{l}", s["h1"], nffn[l] + tok[0, 0], dhf, dh)
        tok = scatter_advance(f"w_gate_{l}", after=dh1)
        dcat = _mm_nt(f"d_cat_{l}", dh1_b, w["wo"], F32, tk=D_MODEL, dep=tok)
        tok = scatter_advance(f"w_up_{l}", after=dcat)
        dw_out = _mm_tn(f"dw_out_{l}", s["cat"], dh1_b, BF16, tn=1024, dep=tok)
        tok = scatter_begin(f"w_out_{l}", dw_out.reshape(N_DEV, OUT_SH, D_MODEL))
        dpm, d_mls[l], d_bias[l] = _mlstm_bwd(f"mlstm_bwd_{l}", dcat, s["pm"], s["ht"], s["cs"], s["ns"],
                                               s["ms"], bias[l] + tok[:1], nmls[l])
        dpc, d_conv[l] = _conv_bwd(f"conv_bwd_{l}", dcat, s["pc"], conv_rows[l])
        tok = scatter_advance(f"w_out_{l}", after=dpc)
        dwm_t = _mm_tn(f"dw_mlstm_{l}", dpm, s["hn"], BF16, tm=640, tn=1024, dep=tok)
        dwc_t = _mm_tn_acols(f"dw_conv_{l}", dpc, s["hn"], BF16)
        tok = scatter_begin(f"w_in_{l}", _merge_dw_in(dwm_t, dwc_t))
        dhn = _mm_nn_kt(f"d_norm_mlstm_{l}", dpm, w["win_t"], F32, tk=PM_W, dep=tok)
        dhn = _mm_nn_ksum(f"d_norm_conv_{l}", dpc, w["wc_t"], F32, res=dhn)
        tok = scatter_advance(f"w_in_{l}", after=dhn)
        dh, dh_b, d_mix[l] = _rms_bwd(f"norm_mix_bwd_{l}", s["h0"], nmix[l] + tok[0, 0], dhn, dh1)
        return dh, dh_b, tok

    dh, dh_b, tok = layer_bwd(1, dh, dh_b, None)
    dh, dh_b, tok_tail = layer_bwd(0, dh, dh_b, tok)

    pq = {}
    after = dh
    for l in reversed(range(DEPTH)):
        for nm in ("w_down", "w_gate", "w_up", "w_out", "w_in"):
            if (nm, l) != ("w_in", 0):
                pq[nm, l] = scattered(f"{nm}_{l}", after)
                after = pq[nm, l][0]
    untransposed = lambda outs: [jnp.transpose(o, (0, 2, 1)) for o in outs]
    g_out, d_out, nm_out, nv_out = _adam_sharded(
        "adam_w_out", chip, w_out, m_w_out, v_w_out, [pq["w_out", 0], pq["w_out", 1]])
    g_gate, d_gate, nm_gate, nv_gate = untransposed(_adam_sharded(
        "adam_w_gate", chip, w_gate_t, m_w_gate_t, v_w_gate_t, [pq["w_gate", 0], pq["w_gate", 1]]))
    g_up, d_up, nm_up, nv_up = untransposed(_adam_sharded(
        "adam_w_up", chip, w_up_t, m_w_up_t, v_w_up_t, [pq["w_up", 0], pq["w_up", 1]]))
    g_down, d_down, nm_down, nv_down = _adam_sharded(
        "adam_w_down", chip, w_down, m_w_down, v_w_down, [pq["w_down", 0], pq["w_down", 1]])
    pq["w_in", 0] = scattered("w_in_0", nv_down)
    g_in, d_in, nm_in, nv_in = untransposed(
        [lax.dynamic_slice(o, (0, me, 0), (DEPTH, IN_SH, D_MODEL)) for o in _adam_sharded(
            "adam_w_in", chip, w_in_t, m_w_in_t, v_w_in_t, [pq["w_in", 0], pq["w_in", 1]])])

    bg = jnp.concatenate([d_bias[l][0, :2 * HEADS] for l in range(DEPTH)])
    red_in = jnp.concatenate([
        dh[PAD_FRONT:TOK0], d_mix[0], d_mix[1], d_ffn[0], d_ffn[1], d_final,
        jnp.concatenate([d_mls[0], d_mls[1]], axis=1),
        jnp.stack([d_conv[l][:3] for l in range(DEPTH)]).reshape(3, 2 * CONV_W),
        jnp.pad(bg, (0, D_MODEL - bg.shape[0])).reshape(1, D_MODEL),
        jnp.pad(loss_part[:, :1], ((0, 0), (0, D_MODEL - 1))),
        jnp.zeros((5, D_MODEL), F32) + tok_tail[0, 0]], axis=0)
    red = _exchange_small("reduce_small", red_in, reduce=True)
    loss = red[26, 0]
    g_meta = lax.dynamic_slice_in_dim(red[:N_META], me * meta_sh, meta_sh, axis=1)
    g_mix, g_ffn, g_final = red[16:18], red[18:20], red[20]
    g_mls = red[21].reshape(DEPTH, MLSTM_W)
    g_conv = lax.dynamic_slice_in_dim(red[22:25].reshape(DEPTH, 3, CONV_W), me * conv_sh, conv_sh, axis=2)
    g_bias = red[25, :DEPTH * 2 * HEADS].reshape(DEPTH, 2 * HEADS)

    small_w = [meta_tokens, norm_mix_w, b_gates, conv_w, mlstm_norm_w, norm_ffn_w, norm_final_w]
    small_m = [m_meta_tokens, m_norm_mix_w, m_b_gates, m_conv_w, m_mlstm_norm_w, m_norm_ffn_w, m_norm_final_w]
    small_v = [v_meta_tokens, v_norm_mix_w, v_b_gates, v_conv_w, v_mlstm_norm_w, v_norm_ffn_w, v_norm_final_w]
    small_g = [g_meta, g_mix, g_bias, g_conv, g_mls, g_ffn, g_final]
    shapes = [a.shape for a in small_w]
    packed = _adam_small("adam_small", _pack128(small_w), _pack128(small_m), _pack128(small_v), _pack128(small_g))
    (d_meta, d_nmix, d_bg, d_cw, d_nmls, d_nffn, d_nfin), (nm_meta, nm_nmix, nm_bg, nm_cw, nm_nmls, nm_nffn, nm_nfin), \
        (nv_meta, nv_nmix, nv_bg, nv_cw, nv_nmls, nv_nffn, nv_nfin) = (_unpack128(p, shapes) for p in packed)

    grad_x = dh[TOK0:].reshape(1, seq, D_MODEL)
    return (loss, grad_x,
            g_meta, g_mix, g_in, g_bias, g_conv, g_mls, g_out, g_ffn, g_gate, g_up, g_down, g_final,
            d_meta, d_nmix, d_in, d_bg, d_cw, d_nmls, d_out, d_nffn, d_gate, d_up, d_down, d_nfin,
            nm_meta, nm_nmix, nm_in, nm_bg, nm_cw, nm_nmls, nm_out, nm_nffn, nm_gate, nm_up, nm_down, nm_nfin,
            nv_meta, nv_nmix, nv_in, nv_bg, nv_cw, nv_nmls, nv_out, nv_nffn, nv_gate, nv_up, nv_down, nv_nfin)
```

```python
import functools

import numpy as np
import jax
import jax.numpy as jnp
from jax import lax
from jax.experimental import pallas as pl
from jax.experimental.pallas import tpu as pltpu

F32 = jnp.float32
BF16 = jnp.bfloat16
MESH = pl.DeviceIdType.MESH

D_MODEL = 2048
DEPTH = 2
N_META = 16
MLSTM_W = 1024
CONV_W = 1024
HEADS = 4
DV = 256
DQK = 128
QK_W = 512
CHUNK = 64
PAD_FRONT = 48
TOK0 = PAD_FRONT + N_META
D_FF = 5632
N_DEV = 8
FF_SH = D_FF // N_DEV
D_IN = 6152
IN_SH = D_IN // N_DEV
OUT_SH = D_MODEL // N_DEV
GATE_COL = 3072
PM_W = GATE_COL + 128
GATE_CAP = 15.0
EPS = 1e-6
QSCALE = DQK ** -0.5

ADAM_LR = 0.001
ADAM_B1 = 0.9
ADAM_B2 = 0.999
ADAM_EPS = 1e-08
ADAM_WD = 0.01
ADAM_STEP = 10

V7X_VMEM_LIMIT = 50 * 1024 * 1024
V7X_MXU_COLS = 256


def _params(**kw):
    return pltpu.CompilerParams(vmem_limit_bytes=V7X_VMEM_LIMIT, **kw)


def _tile(n, target, mult):
    best = None
    for t in range(mult, min(n, target) + 1, mult):
        if n % t == 0:
            best = t
    return best if best is not None else n


def _sigmoid(x):
    return 1.0 / (1.0 + jnp.exp(-x))


NN = ((1,), (0,))
NT = ((1,), (1,))
TN = ((0,), (0,))


def _matmul(name, a, b, out_shape, out_dtype, grid, a_bs, b_bs, o_bs, dims, nk, acc_shape=None,
            res=None, res_bs=None, dep=None):
    has_res = res is not None
    n_in = 2 + has_res + (dep is not None)

    def body(*refs):
        a_ref, b_ref = refs[0], refs[1]
        r_ref = refs[2] if has_res else None
        o_ref = refs[n_in]
        x = lax.dot_general(a_ref[...], b_ref[...], (dims, ((), ())), preferred_element_type=F32)
        if nk == 1:
            if has_res:
                x = x + r_ref[...]
            o_ref[...] = x.astype(o_ref.dtype)
            return
        acc = refs[n_in + 1]
        k = pl.program_id(len(grid) - 1)

        @pl.when(k == 0)
        def _():
            acc[...] = (x + r_ref[...]) if has_res else x

        @pl.when(k > 0)
        def _():
            acc[...] += x

        @pl.when(k == nk - 1)
        def _():
            o_ref[...] = acc[...].astype(o_ref.dtype)

    ins = [a, b] + ([res] if has_res else [])
    specs = [a_bs, b_bs] + ([res_bs] if has_res else [])
    if dep is not None:
        ins.append(dep)
        specs.append(pl.BlockSpec((8, 128), lambda *_: (0, 0)))
    scratch = [pltpu.VMEM(acc_shape, F32)] if nk > 1 else []
    return pl.pallas_call(
        body, name=name, grid=grid, in_specs=specs, out_specs=o_bs,
        out_shape=jax.ShapeDtypeStruct(out_shape, out_dtype), scratch_shapes=scratch,
        compiler_params=_params(),
    )(*ins)


def _mm_nn(name, a, b, out_dtype, res=None, tm=1056, tn=512, dep=None):
    r, k = a.shape
    n = b.shape[1]
    tm, tn = _tile(r, tm, 8), _tile(n, tn, 128)
    return _matmul(name, a, b, (r, n), out_dtype, (r // tm, n // tn, 1),
                   pl.BlockSpec((tm, k), lambda i, j, s: (i, 0)),
                   pl.BlockSpec((k, tn), lambda i, j, s: (0, j)),
                   pl.BlockSpec((tm, tn), lambda i, j, s: (i, j)), NN, 1,
                   res=res, res_bs=pl.BlockSpec((tm, tn), lambda i, j, s: (i, j)), dep=dep)


def _mm_nn_kt(name, a, b, out_dtype, tm=1056, tn=1024, tk=640, dep=None):
    r, k = a.shape
    n = b.shape[1]
    tm, tn, tk = _tile(r, tm, 8), _tile(n, tn, 128), _tile(k, tk, 128)
    nk = k // tk
    return _matmul(name, a, b, (r, n), out_dtype, (r // tm, n // tn, nk),
                   pl.BlockSpec((tm, tk), lambda i, j, s: (i, s)),
                   pl.BlockSpec((tk, tn), lambda i, j, s: (s, j)),
                   pl.BlockSpec((tm, tn), lambda i, j, s: (i, j)), NN, nk, acc_shape=(tm, tn), dep=dep)


def _mm_nn_ksum(name, a3, b3, out_dtype, res=None, tm=1056, tn=1024, dep=None):
    e, r, kb = a3.shape
    n = b3.shape[2]
    tm, tn = _tile(r, tm, 8), _tile(n, tn, 128)
    return _matmul(name, a3, b3, (r, n), out_dtype, (r // tm, n // tn, e),
                   pl.BlockSpec((None, tm, kb), lambda i, j, s: (s, i, 0)),
                   pl.BlockSpec((None, kb, tn), lambda i, j, s: (s, 0, j)),
                   pl.BlockSpec((tm, tn), lambda i, j, s: (i, j)), NN, e, acc_shape=(tm, tn),
                   res=res, res_bs=pl.BlockSpec((tm, tn), lambda i, j, s: (i, j)), dep=dep)


def _mm_nt(name, a, b, out_dtype, res=None, tm=1056, tn=512, tk=640, n=None, dep=None):
    r, k = a.shape
    n = b.shape[0] if n is None else n
    tm, tn, tk = _tile(r, tm, 8), _tile(n, tn, 128), _tile(k, tk, 128)
    nk = k // tk
    return _matmul(name, a, b, (r, n), out_dtype, (r // tm, n // tn, nk),
                   pl.BlockSpec((tm, tk), lambda i, j, s: (i, s)),
                   pl.BlockSpec((tn, tk), lambda i, j, s: (j, s)),
                   pl.BlockSpec((tm, tn), lambda i, j, s: (i, j)), NT, nk, acc_shape=(tm, tn),
                   res=res, res_bs=pl.BlockSpec((tm, tn), lambda i, j, s: (i, j)), dep=dep)


def _mm_nt_bcols(name, a, b3, out_dtype, tm=1056, dep=None):
    r, k = a.shape
    e, n, _ = b3.shape
    tm = _tile(r, tm, 8)
    return _matmul(name, a, b3, (e, r, n), out_dtype, (r // tm, e, 1),
                   pl.BlockSpec((tm, k), lambda i, g, s: (i, 0)),
                   pl.BlockSpec((None, n, k), lambda i, g, s: (g, 0, 0)),
                   pl.BlockSpec((None, tm, n), lambda i, g, s: (g, i, 0)), NT, 1, dep=dep)


def _mm_tn(name, a, b, out_dtype, tm=1024, tn=640, dep=None):
    r, m = a.shape
    n = b.shape[1]
    tm, tn = _tile(m, tm, 128), _tile(n, tn, 128)
    return _matmul(name, a, b, (m, n), out_dtype, (m // tm, n // tn, 1),
                   pl.BlockSpec((r, tm), lambda i, j, s: (0, i)),
                   pl.BlockSpec((r, tn), lambda i, j, s: (0, j)),
                   pl.BlockSpec((tm, tn), lambda i, j, s: (i, j)), TN, 1, dep=dep)


def _mm_tn_acols(name, a3, b, out_dtype, tn=1024, dep=None):
    e, r, m = a3.shape
    n = b.shape[1]
    tn = _tile(n, tn, 128)
    return _matmul(name, a3, b, (e, m, n), out_dtype, (n // tn, e, 1),
                   pl.BlockSpec((None, r, m), lambda j, g, s: (g, 0, 0)),
                   pl.BlockSpec((r, tn), lambda j, g, s: (0, j)),
                   pl.BlockSpec((None, m, tn), lambda j, g, s: (g, 0, j)), TN, 1, dep=dep)


def _rms_fwd(name, h, w):
    r, d = h.shape
    tr = _tile(r, 264, 8)

    def body(h_ref, w_ref, o_ref):
        x = h_ref[...]
        rs = lax.rsqrt(jnp.mean(x * x, axis=1, keepdims=True) + EPS)
        o_ref[...] = (x * rs * w_ref[...]).astype(BF16)

    return pl.pallas_call(
        body, name=name, grid=(r // tr,),
        in_specs=[pl.BlockSpec((tr, d), lambda i: (i, 0)), pl.BlockSpec((1, d), lambda i: (0, 0))],
        out_specs=pl.BlockSpec((tr, d), lambda i: (i, 0)),
        out_shape=jax.ShapeDtypeStruct((r, d), BF16), compiler_params=_params(),
    )(h, w)


def _rms_bwd(name, x, w, dy, dres):
    r, d = x.shape
    tr = _tile(r, 264, 8)

    def body(x_ref, w_ref, dy_ref, dr_ref, dx_ref, dxb_ref, dw_ref):
        xv = x_ref[...]
        g = dy_ref[...]
        rs = lax.rsqrt(jnp.mean(xv * xv, axis=1, keepdims=True) + EPS)
        wg = g * w_ref[...]
        dx = rs * wg - xv * (rs * rs * rs) * jnp.mean(xv * wg, axis=1, keepdims=True) + dr_ref[...]
        dx_ref[...] = dx
        dxb_ref[...] = dx.astype(BF16)
        part = jnp.sum(g * xv * rs, axis=0, keepdims=True)

        @pl.when(pl.program_id(0) == 0)
        def _():
            dw_ref[...] = part

        @pl.when(pl.program_id(0) > 0)
        def _():
            dw_ref[...] += part

    row = pl.BlockSpec((tr, d), lambda i: (i, 0))
    vec = pl.BlockSpec((1, d), lambda i: (0, 0))
    return pl.pallas_call(
        body, name=name, grid=(r // tr,), in_specs=[row, vec, row, row], out_specs=[row, row, vec],
        out_shape=[jax.ShapeDtypeStruct((r, d), F32), jax.ShapeDtypeStruct((r, d), BF16),
                   jax.ShapeDtypeStruct((1, d), F32)],
        compiler_params=_params(),
    )(x, w, dy, dres)


def _final_loss(name, h, w, target):
    r, d = h.shape
    nb = r // CHUNK

    def body(h_ref, w_ref, t_ref, dh_ref, dhb_ref, dw_ref, ls_ref):
        i = pl.program_id(0)

        @pl.when(i == 0)
        def _():
            dh_ref[...] = jnp.zeros_like(dh_ref)
            dhb_ref[...] = jnp.zeros_like(dhb_ref)
            dw_ref[...] = jnp.zeros_like(dw_ref)
            ls_ref[...] = jnp.zeros_like(ls_ref)

        @pl.when(i > 0)
        def _():
            xv = h_ref[...]
            wv = w_ref[...]
            rs = lax.rsqrt(jnp.mean(xv * xv, axis=1, keepdims=True) + EPS)
            err = xv * rs * wv - t_ref[...]
            sq = jnp.sum(jnp.sum(err * err, axis=1, keepdims=True), axis=0, keepdims=True)
            ls_ref[...] += jnp.broadcast_to(sq * (0.5 / d), ls_ref.shape)
            g = err * (1.0 / d)
            wg = g * wv
            dx = rs * wg - xv * (rs * rs * rs) * jnp.mean(xv * wg, axis=1, keepdims=True)
            dh_ref[...] = dx
            dhb_ref[...] = dx.astype(BF16)
            dw_ref[...] += jnp.sum(g * xv * rs, axis=0, keepdims=True)

    row = pl.BlockSpec((CHUNK, d), lambda i: (i, 0))
    vec = pl.BlockSpec((1, d), lambda i: (0, 0))
    return pl.pallas_call(
        body, name=name, grid=(nb,),
        in_specs=[row, vec, pl.BlockSpec((CHUNK, d), lambda i: (jnp.maximum(i - 1, 0), 0))],
        out_specs=[row, row, vec, pl.BlockSpec((1, 128), lambda i: (0, 0))],
        out_shape=[jax.ShapeDtypeStruct((r, d), F32), jax.ShapeDtypeStruct((r, d), BF16),
                   jax.ShapeDtypeStruct((1, d), F32), jax.ShapeDtypeStruct((1, 128), F32)],
        compiler_params=_params(),
    )(h, w, target)


def _ffn_in(name, hf, wg_t, wu_t, dep=None, tm=1056, tn=512):
    r, d = hf.shape
    f = wg_t.shape[0]
    tm, tn = _tile(r, tm, 8), _tile(f, tn, 128)

    def body(h_ref, wg_ref, wu_ref, *rest):
        g_ref, u_ref, a_ref = rest[-3:]
        x = h_ref[...]
        g = lax.dot_general(x, wg_ref[...], (NT, ((), ())), preferred_element_type=F32)
        u = lax.dot_general(x, wu_ref[...], (NT, ((), ())), preferred_element_type=F32)
        g_ref[...] = g.astype(BF16)
        u_ref[...] = u.astype(BF16)
        a_ref[...] = (g * _sigmoid(g) * u).astype(BF16)

    wspec = pl.BlockSpec((tn, d), lambda i, j: (j, 0))
    ospec = pl.BlockSpec((tm, tn), lambda i, j: (i, j))
    ins, specs = [hf, wg_t, wu_t], [pl.BlockSpec((tm, d), lambda i, j: (i, 0)), wspec, wspec]
    if dep is not None:
        ins.append(dep)
        specs.append(pl.BlockSpec((8, 128), lambda *_: (0, 0)))
    return pl.pallas_call(
        body, name=name, grid=(r // tm, f // tn), in_specs=specs, out_specs=[ospec] * 3,
        out_shape=[jax.ShapeDtypeStruct((r, f), BF16)] * 3, compiler_params=_params(),
    )(*ins)


def _ffn_act_bwd(name, dh, wd, g, u, dep=None, tm=1056, tn=512):
    r, d = dh.shape
    f = wd.shape[0]
    tm, tn = _tile(r, tm, 8), _tile(f, tn, 128)

    def body(dh_ref, wd_ref, g_ref, u_ref, *rest):
        dg_ref, du_ref = rest[-2:]
        tr = _tile(tm, 264, 8)
        for r0 in range(0, tm, tr):
            for c0 in range(0, tn, V7X_MXU_COLS):
                rows, cols = slice(r0, r0 + tr), slice(c0, c0 + V7X_MXU_COLS)
                da = lax.dot_general(dh_ref[rows, :], wd_ref[cols, :], (NT, ((), ())), preferred_element_type=F32)
                gv = g_ref[rows, cols].astype(F32)
                s = _sigmoid(gv)
                t = da * s
                du_ref[rows, cols] = (t * gv).astype(BF16)
                dg_ref[rows, cols] = (t * u_ref[rows, cols].astype(F32) * (1.0 + gv - gv * s)).astype(BF16)

    tile = pl.BlockSpec((tm, tn), lambda i, j: (i, j))
    ins = [dh, wd, g, u]
    specs = [pl.BlockSpec((tm, d), lambda i, j: (i, 0)), pl.BlockSpec((tn, d), lambda i, j: (j, 0)), tile, tile]
    if dep is not None:
        ins.append(dep)
        specs.append(pl.BlockSpec((8, 128), lambda *_: (0, 0)))
    return pl.pallas_call(
        body, name=name, grid=(r // tm, f // tn), in_specs=specs, out_specs=[tile] * 2,
        out_shape=[jax.ShapeDtypeStruct((r, f), BF16)] * 2, compiler_params=_params(),
    )(*ins)


def _shift_down(a, k):
    row = lax.broadcasted_iota(jnp.int32, a.shape, 0)
    return jnp.where(row >= k, pltpu.roll(a, k, 0), 0.0)


def _shift_up(a, k):
    n = a.shape[0]
    row = lax.broadcasted_iota(jnp.int32, a.shape, 0)
    return jnp.where(row < n - k, pltpu.roll(a, n - k, 0), 0.0)


def _conv_fwd(name, pc, cw):
    _, r, w = pc.shape

    def body(pc_ref, cw_ref, o_ref):
        a = pc_ref[2] * pc_ref[0]
        cwv = cw_ref[...]
        conv = _shift_down(a, 2) * cwv[0:1] + _shift_down(a, 1) * cwv[1:2] + a * cwv[2:3]
        o_ref[...] = (pc_ref[1] * conv).astype(BF16)

    return pl.pallas_call(
        body, name=name, grid=(w // 128,),
        in_specs=[pl.BlockSpec((3, r, 128), lambda j: (0, 0, j)), pl.BlockSpec((8, 128), lambda j: (0, j))],
        out_specs=pl.BlockSpec((r, 128), lambda j: (0, j)),
        out_shape=jax.ShapeDtypeStruct((r, w), BF16), compiler_params=_params(),
    )(pc, cw)


def _conv_bwd(name, dcat, pc, cw):
    _, r, w = pc.shape
    nblk = w // 128

    def body(dy_ref, pc_ref, cw_ref, dpc_ref, dcw_ref):
        u, gb, gc = pc_ref[0], pc_ref[1], pc_ref[2]
        cwv = cw_ref[...]
        dy = dy_ref[...]
        a = gc * u
        a1, a2 = _shift_down(a, 1), _shift_down(a, 2)
        conv = a2 * cwv[0:1] + a1 * cwv[1:2] + a * cwv[2:3]
        dconv = dy * gb
        da = dconv * cwv[2:3] + _shift_up(dconv, 1) * cwv[1:2] + _shift_up(dconv, 2) * cwv[0:1]
        dpc_ref[0] = (da * gc).astype(BF16)
        dpc_ref[1] = (dy * conv).astype(BF16)
        dpc_ref[2] = (da * u).astype(BF16)
        row = lax.broadcasted_iota(jnp.int32, (8, 128), 0)
        dw0 = jnp.sum(dconv * a2, axis=0, keepdims=True)
        dw1 = jnp.sum(dconv * a1, axis=0, keepdims=True)
        dw2 = jnp.sum(dconv * a, axis=0, keepdims=True)
        dcw_ref[...] = jnp.where(row == 0, dw0, jnp.where(row == 1, dw1, jnp.where(row == 2, dw2, 0.0)))

    return pl.pallas_call(
        body, name=name, grid=(nblk,),
        in_specs=[pl.BlockSpec((r, 128), lambda j: (0, nblk + j)),
                  pl.BlockSpec((3, r, 128), lambda j: (0, 0, j)), pl.BlockSpec((8, 128), lambda j: (0, j))],
        out_specs=[pl.BlockSpec((3, r, 128), lambda j: (0, 0, j)), pl.BlockSpec((8, 128), lambda j: (0, j))],
        out_shape=[jax.ShapeDtypeStruct((3, r, w), BF16), jax.ShapeDtypeStruct((8, w), F32)],
        compiler_params=_params(),
    )(dcat, pc, cw)


def _dot(a, b, dims):
    return lax.dot_general(a, b, (dims, ((), ())), preferred_element_type=F32)


def _col_to_row(xc, eye):
    return jnp.sum(jnp.where(eye, xc, 0.0), axis=0, keepdims=True)


def _row_to_col(xr, eye):
    return jnp.sum(jnp.where(eye, xr, 0.0), axis=1, keepdims=True)


def _gate_tiles(graw, bias, row0):
    th = jnp.tanh((graw + bias) / GATE_CAP)
    z = GATE_CAP * th
    row = lax.broadcasted_iota(jnp.int32, graw.shape, 0) + row0
    real = row >= PAD_FRONT
    li = jnp.where(real, z, -jnp.inf)
    lf = jnp.where(real, jnp.minimum(z, 0.0) - jnp.log(1.0 + jnp.exp(-jnp.abs(z))), 0.0)
    return th, z, li, lf, real


def _interleave(gens):
    results = [None] * len(gens)
    live = list(enumerate(gens))
    while live:
        still = []
        for i, gen in live:
            try:
                next(gen)
                still.append((i, gen))
            except StopIteration as stop:
                results[i] = stop.value
        live = still
    return results


def _chunk_common(pm, h, li, lf, cst, nst, mst, tril, eye):
    kraw = pm[:, QK_W + h * DQK:QK_W + (h + 1) * DQK]
    q = (pm[:, h * DQK:(h + 1) * DQK] * QSCALE).astype(BF16)
    yield
    k = kraw.astype(BF16)
    v = pm[:, 2 * QK_W + h * DV:2 * QK_W + (h + 1) * DV].astype(BF16)
    yield
    li_c = li[:, h:h + 1]
    lf_c = lf[:, HEADS + h:HEADS + h + 1]
    li_r = _col_to_row(li_c, eye)
    yield
    lf_r = _col_to_row(lf_c, eye)
    yield
    b_c = jnp.sum(jnp.where(tril, lf_r, 0.0), axis=1, keepdims=True)
    yield
    b_r = _col_to_row(b_c, eye)
    yield
    dmat = jnp.where(tril, b_c - b_r + li_r, -jnp.inf)
    inter = b_c + mst
    yield
    mt = jnp.maximum(inter, jnp.max(dmat, axis=1, keepdims=True))
    yield
    w_inter = jnp.exp(inter - mt)
    p = jnp.exp(dmat - mt)
    yield
    s = _dot(q, k, NT) * p
    yield
    cb = cst.astype(BF16)
    nb = nst.astype(BF16).astype(F32)
    qc = _dot(q, cb, NN)
    yield
    qn = jnp.sum(q.astype(F32) * nb, axis=1, keepdims=True)
    yield
    den = w_inter * qn + jnp.sum(s, axis=1, keepdims=True)
    yield
    dn = jnp.maximum(jnp.abs(den), jnp.exp(-mt))
    b_end = b_c[CHUNK - 1:CHUNK, :]
    decay = b_end - b_c + li_c
    yield
    m_new = jnp.maximum(b_end + mst, jnp.max(decay, axis=0, keepdims=True))
    yield
    w_old = jnp.exp(b_end + mst - m_new)
    w_in = jnp.exp(decay - m_new)
    kw = (w_in * kraw).astype(BF16)
    yield
    return dict(q=q, k=k, v=v, kraw=kraw, mt=mt, w_inter=w_inter, p=p, s=s, cb=cb, nb=nb, qc=qc, qn=qn,
                den=den, dn=dn, m_new=m_new, w_old=w_old, w_in=w_in, kw=kw)


def _chunks_per_step(nc):
    return 1


def _mlstm_fwd(name, pm, bias, nw):
    r = pm.shape[0]
    nc = r // CHUNK
    grp = _chunks_per_step(nc)

    def body(pm_ref, b_ref, nw_ref, hm_ref, ht_ref, cs_ref, ns_ref, ms_ref, c_scr, n_scr, m_scr):
        step = pl.program_id(0)

        @pl.when(step == 0)
        def _():
            c_scr[...] = jnp.zeros_like(c_scr)
            n_scr[...] = jnp.zeros_like(n_scr)
            m_scr[...] = jnp.zeros_like(m_scr)

        rr = lax.broadcasted_iota(jnp.int32, (CHUNK, CHUNK), 0)
        cc = lax.broadcasted_iota(jnp.int32, (CHUNK, CHUNK), 1)
        tril, eye = cc <= rr, cc == rr
        bv, nwv = b_ref[...], nw_ref[...]
        states = [(c_scr[h], n_scr[h], m_scr[h]) for h in range(HEADS)]
        for g in range(grp):
            rows = slice(g * CHUNK, (g + 1) * CHUNK)
            pmv = pm_ref[rows, :]
            _, _, li, lf, _ = _gate_tiles(pmv[:, GATE_COL:GATE_COL + 128], bv, (step * grp + g) * CHUNK)
            def head(h, cst, nst, mst, g=g, rows=rows, pmv=pmv, li=li, lf=lf):
                f = yield from _chunk_common(pmv, h, li, lf, cst, nst, mst, tril, eye)
                num = f["w_inter"] * f["qc"] + _dot(f["s"].astype(BF16), f["v"], NN)
                yield
                hh = num / f["dn"]
                yield
                c_new = f["w_old"] * cst + _dot(f["kw"], f["v"], TN)
                yield
                n_new = f["w_old"] * nst + jnp.sum(
                    f["w_in"].astype(BF16).astype(F32) * f["k"].astype(F32), axis=0, keepdims=True)
                yield
                sl = slice(h * DV, (h + 1) * DV)
                rs = lax.rsqrt(jnp.mean(hh * hh, axis=1, keepdims=True) + EPS)
                yield
                og = pmv[:, 2 * QK_W + MLSTM_W + h * DV:2 * QK_W + MLSTM_W + (h + 1) * DV]
                cs_ref[g, h] = cst
                ns_ref[g, h] = nst
                ms_ref[g, h] = mst
                ht_ref[rows, sl] = hh
                yield
                hm_ref[rows, sl] = (_sigmoid(og) * (hh * rs * nwv[:, sl])).astype(BF16)
                return c_new, n_new, f["m_new"]

            states = _interleave([head(h, *states[h]) for h in range(HEADS)])
        for h, (cst, nst, mst) in enumerate(states):
            c_scr[h] = cst
            n_scr[h] = nst
            m_scr[h] = mst

    return pl.pallas_call(
        body, name=name, grid=(nc // grp,),
        in_specs=[pl.BlockSpec((grp * CHUNK, PM_W), lambda i: (i, 0)), pl.BlockSpec((1, 128), lambda i: (0, 0)),
                  pl.BlockSpec((1, MLSTM_W), lambda i: (0, 0))],
        out_specs=[pl.BlockSpec((grp * CHUNK, MLSTM_W), lambda i: (i, 0)),
                   pl.BlockSpec((grp * CHUNK, MLSTM_W), lambda i: (i, 0)),
                   pl.BlockSpec((grp, HEADS, DQK, DV), lambda i: (i, 0, 0, 0)),
                   pl.BlockSpec((grp, HEADS, 1, DQK), lambda i: (i, 0, 0, 0)),
                   pl.BlockSpec((grp, HEADS, 1, 1), lambda i: (i, 0, 0, 0))],
        out_shape=[jax.ShapeDtypeStruct((r, MLSTM_W), BF16), jax.ShapeDtypeStruct((r, MLSTM_W), F32),
                   jax.ShapeDtypeStruct((nc, HEADS, DQK, DV), F32),
                   jax.ShapeDtypeStruct((nc, HEADS, 1, DQK), F32),
                   jax.ShapeDtypeStruct((nc, HEADS, 1, 1), F32)],
        scratch_shapes=[pltpu.VMEM((HEADS, DQK, DV), F32), pltpu.VMEM((HEADS, 1, DQK), F32),
                        pltpu.VMEM((HEADS, 1, 1), F32)],
        compiler_params=_params(),
    )(pm, bias, nw)


def _mlstm_bwd(name, dcat, pm, ht, cs, ns, ms, bias, nw):
    r = pm.shape[0]
    nc = r // CHUNK
    grp = _chunks_per_step(nc)
    nsteps = nc // grp

    def body(dy_ref, pm_ref, ht_ref, cs_ref, ns_ref, ms_ref, b_ref, nw_ref, dpm_ref, dnw_ref, db_ref,
             dc_scr, dn_scr):
        step = pl.program_id(0)

        @pl.when(step == 0)
        def _():
            dc_scr[...] = jnp.zeros_like(dc_scr)
            dn_scr[...] = jnp.zeros_like(dn_scr)
            dnw_ref[...] = jnp.zeros_like(dnw_ref)
            db_ref[...] = jnp.zeros_like(db_ref)

        rr = lax.broadcasted_iota(jnp.int32, (CHUNK, CHUNK), 0)
        cc = lax.broadcasted_iota(jnp.int32, (CHUNK, CHUNK), 1)
        tril, eye, triu = cc <= rr, cc == rr, cc >= rr
        lane = lax.broadcasted_iota(jnp.int32, (CHUNK, 128), 1)
        rowid = lax.broadcasted_iota(jnp.int32, (CHUNK, 1), 0)
        bv, nwv = b_ref[...], nw_ref[...]
        carried = [(dc_scr[h], dn_scr[h]) for h in range(HEADS)]
        dnw_acc = [jnp.zeros((1, DV), F32) for _ in range(HEADS)]
        db_acc = jnp.zeros((1, 128), F32)
        for g in reversed(range(grp)):
            rows = slice(g * CHUNK, (g + 1) * CHUNK)
            ci = (nsteps - 1 - step) * grp + g
            pmv = pm_ref[rows, :]
            th, z, li, lf, real = _gate_tiles(pmv[:, GATE_COL:GATE_COL + 128], bv, ci * CHUNK)
            heads = _interleave([
                _mlstm_bwd_head(h, pmv, ht_ref[rows, h * DV:(h + 1) * DV], dy_ref[rows, h * DV:(h + 1) * DV], nwv,
                                li, lf, cs_ref[g, h], ns_ref[g, h], ms_ref[g, h], carried[h][0], carried[h][1],
                                tril, eye, triu, lane, rowid, dpm_ref, rows)
                for h in range(HEADS)])
            carried = [(dc_new, dn_new) for _, dc_new, dn_new, _ in heads]
            dgt = heads[0][0] + heads[1][0] + heads[2][0] + heads[3][0]
            dnw_acc = [dnw_acc[h] + heads[h][3] for h in range(HEADS)]
            dact = jnp.where(lane < HEADS, 1.0, 1.0 - _sigmoid(z)) * (1.0 - th * th)
            dgraw = jnp.where(real & (lane < 2 * HEADS), dgt * dact, 0.0)
            dpm_ref[rows, GATE_COL:GATE_COL + 128] = dgraw.astype(BF16)
            db_acc = db_acc + jnp.sum(dgraw, axis=0, keepdims=True)
        for h, (dcn, dnn) in enumerate(carried):
            dc_scr[h] = dcn
            dn_scr[h] = dnn
            dnw_ref[:, h * DV:(h + 1) * DV] += dnw_acc[h]
        db_ref[...] += db_acc

    rev = lambda i: (nsteps - 1 - i, 0)
    rev4 = lambda i: (nsteps - 1 - i, 0, 0, 0)
    return pl.pallas_call(
        body, name=name, grid=(nsteps,),
        in_specs=[pl.BlockSpec((grp * CHUNK, MLSTM_W), rev), pl.BlockSpec((grp * CHUNK, PM_W), rev),
                  pl.BlockSpec((grp * CHUNK, MLSTM_W), rev),
                  pl.BlockSpec((grp, HEADS, DQK, DV), rev4), pl.BlockSpec((grp, HEADS, 1, DQK), rev4),
                  pl.BlockSpec((grp, HEADS, 1, 1), rev4),
                  pl.BlockSpec((1, 128), lambda i: (0, 0)), pl.BlockSpec((1, MLSTM_W), lambda i: (0, 0))],
        out_specs=[pl.BlockSpec((grp * CHUNK, PM_W), rev), pl.BlockSpec((1, MLSTM_W), lambda i: (0, 0)),
                   pl.BlockSpec((1, 128), lambda i: (0, 0))],
        out_shape=[jax.ShapeDtypeStruct((r, PM_W), BF16), jax.ShapeDtypeStruct((1, MLSTM_W), F32),
                   jax.ShapeDtypeStruct((1, 128), F32)],
        scratch_shapes=[pltpu.VMEM((HEADS, DQK, DV), F32), pltpu.VMEM((HEADS, 1, DQK), F32)],
        compiler_params=_params(),
    )(dcat, pm, ht, cs, ns, ms, bias, nw)


def _mlstm_bwd_head(h, pmv, hh, y, nwv, li, lf, cst, nst, mst, dcn, dnn, tril, eye, triu, lane, rowid,
                    dpm_ref, rows):
    f = yield from _chunk_common(pmv, h, li, lf, cst, nst, mst, tril, eye)
    q, k, v, s, p = f["q"], f["k"], f["v"], f["s"], f["p"]
    w_inter, w_in, w_old, dn = f["w_inter"], f["w_in"], f["w_old"], f["dn"]
    osl = slice(2 * QK_W + MLSTM_W + h * DV, 2 * QK_W + MLSTM_W + (h + 1) * DV)
    sg = _sigmoid(pmv[:, osl])
    yield
    rs = lax.rsqrt(jnp.mean(hh * hh, axis=1, keepdims=True) + EPS)
    yield
    nwh = nwv[:, h * DV:(h + 1) * DV]
    dpm_ref[rows, osl] = (y * (hh * rs * nwh) * sg * (1.0 - sg)).astype(BF16)
    yield
    dhn = y * sg
    dnw_h = jnp.sum(dhn * hh * rs, axis=0, keepdims=True)
    yield
    wd = dhn * nwh
    dhh = rs * wd - hh * (rs * rs * rs) * jnp.mean(hh * wd, axis=1, keepdims=True)
    yield
    dnum = dhh / dn
    dd = -jnp.sum(dhh * hh, axis=1, keepdims=True) / dn
    yield
    dden = jnp.where(jnp.abs(f["den"]) > jnp.exp(-f["mt"]), dd * jnp.sign(f["den"]), 0.0)
    dnum_b = dnum.astype(BF16)
    wdn = (w_inter * dnum).astype(BF16)
    wid = (w_inter * dden).astype(BF16).astype(F32)
    yield
    ds = _dot(dnum_b, v, NT) + dden
    yield
    dsp = (ds * p).astype(BF16)
    yield
    dq = _dot(dsp, k, NN) + _dot(wdn, f["cb"], NT) + wid * f["nb"]
    yield
    dk = _dot(dsp, q, TN)
    yield
    dv = _dot(s.astype(BF16), dnum_b, TN)
    yield
    g = ds * s
    g_col = _row_to_col(jnp.sum(g, axis=0, keepdims=True), eye)
    yield
    db = jnp.sum(g, axis=1, keepdims=True) - g_col
    dli = g_col
    yield
    db = db + (jnp.sum(dnum * f["qc"], axis=1, keepdims=True) + dden * f["qn"]) * w_inter
    yield
    dcnb = dcn.astype(BF16)
    dnnb = dnn.astype(BF16).astype(F32)
    dkw = _dot(v, dcnb, NT) + dnnb
    yield
    dk = dk + w_in * dkw
    dv = dv + _dot(f["kw"], dcnb, NN)
    yield
    ddecay = jnp.sum(dkw * f["kraw"], axis=1, keepdims=True) * w_in
    yield
    dw_old = (jnp.sum(jnp.sum(dcn * cst, axis=1, keepdims=True), axis=0, keepdims=True)
              + jnp.sum(dnn * nst, axis=1, keepdims=True))
    yield
    db_end = dw_old * w_old + jnp.sum(ddecay, axis=0, keepdims=True)
    db = db - ddecay + jnp.where(rowid == CHUNK - 1, db_end, 0.0)
    dli = dli + ddecay
    yield
    dc_new = w_old * dcn + _dot(q, wdn, TN)
    yield
    dn_new = w_old * dnn + jnp.sum(wid * q.astype(F32), axis=0, keepdims=True)
    yield
    dlf = jnp.sum(jnp.where(triu, _col_to_row(db, eye), 0.0), axis=1, keepdims=True)
    yield
    gate_part = jnp.where(lane == h, dli, 0.0) + jnp.where(lane == HEADS + h, dlf, 0.0)
    dpm_ref[rows, h * DQK:(h + 1) * DQK] = (dq * QSCALE).astype(BF16)
    yield
    dpm_ref[rows, QK_W + h * DQK:QK_W + (h + 1) * DQK] = dk.astype(BF16)
    yield
    dpm_ref[rows, 2 * QK_W + h * DV:2 * QK_W + (h + 1) * DV] = dv.astype(BF16)
    return gate_part, dc_new, dn_new, dnw_h


def _my_place():
    return lax.axis_index("x"), lax.axis_index("y"), lax.axis_index("c")


def _flip(v, bit):
    return 1 - v if bit else v


def _exchange_small(name, blk, reduce):
    r, c = blk.shape

    def body(x_ref, o_ref, *rest):
        slots = rest[0] if reduce else o_ref
        send_sems, recv_sems = rest[-2], rest[-1]
        x, y, cc = _my_place()
        me = 4 * x + 2 * y + cc
        slots[me] = x_ref[...]
        copies = []
        for k in range(1, N_DEV):
            peer = (_flip(x, k & 4), _flip(y, k & 2), _flip(cc, k & 1))
            cp = pltpu.make_async_remote_copy(
                src_ref=x_ref, dst_ref=slots.at[me], send_sem=send_sems.at[k - 1],
                recv_sem=recv_sems.at[k - 1], device_id=peer, device_id_type=MESH)
            cp.start()
            copies.append(cp)
        for cp in copies:
            cp.wait()
        if reduce:
            acc = slots[0]
            for d in range(1, N_DEV):
                acc = acc + slots[d]
            o_ref[...] = acc

    scratch = ([pltpu.VMEM((N_DEV, r, c), F32)] if reduce else []) + [
        pltpu.SemaphoreType.DMA((N_DEV - 1,)), pltpu.SemaphoreType.DMA((N_DEV - 1,))]
    return pl.pallas_call(
        body, name=name,
        out_shape=jax.ShapeDtypeStruct((r, c) if reduce else (N_DEV, r, c), F32),
        in_specs=[pl.BlockSpec(memory_space=pltpu.VMEM)], out_specs=pl.BlockSpec(memory_space=pltpu.VMEM),
        scratch_shapes=scratch, compiler_params=_params(),
    )(blk)


HBM_SPEC = pl.BlockSpec(memory_space=pltpu.HBM)
SEM_SPEC = pl.BlockSpec(memory_space=pltpu.SEMAPHORE)
ANY_SPEC = pl.BlockSpec(memory_space=pl.ANY)
DATAFLOW = pltpu.SideEffectType.DATAFLOW_SIDE_EFFECTING


def _split_copy(name, arrays, start=None, wait=None, after=None):
    results, token = _split_copies(name, [(arrays, start, wait)], after)
    return results[0][0], results[0][1], token


def _split_copies(name, jobs, after=None):
    operands, in_specs, out_shape, out_specs, aliases = [], [], [], [], {}
    in_at, out_at = [], []
    for arrays, start, wait in jobs:
        in_at.append(len(operands))
        operands += [pltpu.with_memory_space_constraint(a, pltpu.HBM) for a in arrays]
        in_specs += [HBM_SPEC] * len(arrays)
        if wait:
            operands += list(wait[1])
            in_specs += [SEM_SPEC, SEM_SPEC]
    if after is not None:
        operands.append(after)
        in_specs.append(ANY_SPEC)
    for j, (arrays, start, wait) in enumerate(jobs):
        out_at.append(len(out_shape))
        if start:
            out_shape += [pltpu.SemaphoreType.DMA((start[1],)), pltpu.SemaphoreType.DMA((start[1],))]
            out_specs += [SEM_SPEC, SEM_SPEC]
        for i, a in enumerate(arrays):
            aliases[in_at[j] + i] = len(out_shape)
            out_shape.append(pltpu.HBM(a.shape, a.dtype))
            out_specs.append(HBM_SPEC)
    any_start = any(start for _, start, _ in jobs)
    if any_start:
        out_shape.append(jax.ShapeDtypeStruct((8, 128), F32))
        out_specs.append(pl.BlockSpec(memory_space=pltpu.VMEM))
    n_in = len(operands)

    def body(*refs):
        for j, (arrays, start, wait) in enumerate(jobs):
            if wait:
                ins = refs[in_at[j]:in_at[j] + len(arrays)]
                at = in_at[j] + len(arrays)
                for cp in wait[0](ins, refs[at], refs[at + 1]):
                    cp.wait_send()
                    cp.wait_recv()
        for j, (arrays, start, wait) in enumerate(jobs):
            if start:
                ins = refs[in_at[j]:in_at[j] + len(arrays)]
                at = n_in + out_at[j]
                for cp in start[0](ins, refs[at], refs[at + 1]):
                    cp.start()
        if any_start:
            token = refs[n_in + len(out_shape) - 1]
            token[...] = jnp.zeros_like(token)

    outs = pl.pallas_call(
        body, name=name, in_specs=in_specs, out_specs=out_specs, out_shape=out_shape,
        input_output_aliases=aliases, compiler_params=pltpu.CompilerParams(has_side_effects=DATAFLOW),
    )(*operands)
    results = []
    for j, (arrays, start, wait) in enumerate(jobs):
        at = out_at[j]
        sems = (outs[at], outs[at + 1]) if start else None
        at += 2 if start else 0
        results.append((list(outs[at:at + len(arrays)]), sems))
    return results, (outs[-1] if any_start else None)


def _remote(src, dst, send_sems, recv_sems, k, to):
    return pltpu.make_async_remote_copy(src_ref=src, dst_ref=dst, send_sem=send_sems.at[k],
                                        recv_sem=recv_sems.at[k], device_id=to, device_id_type=MESH)


def _slot(px, py, pc):
    return 4 * px + 2 * py + pc


def _gather_first(refs, send_sems, recv_sems):
    x, y, c = _my_place()
    blk = refs[0].at[_slot(x, y, c)]
    targets = [(x, y, 1 - c), (1 - x, y, c), (x, 1 - y, c)]
    return [_remote(blk, blk, send_sems, recv_sems, k, to) for k, to in enumerate(targets)]


def _gather_relay(refs, send_sems, recv_sems):
    x, y, c = _my_place()
    rows = refs[0].shape[1]
    half = rows // 32 * 16
    from_x, from_y = _slot(1 - x, y, c), _slot(x, 1 - y, c)
    upper = refs[0].at[from_x, pl.ds(0, half)]
    lower = refs[0].at[from_y, pl.ds(half, rows - half)]
    return [_remote(upper, upper, send_sems, recv_sems, 0, (x, 1 - y, c)),
            _remote(lower, lower, send_sems, recv_sems, 1, (1 - x, y, c)),
            _remote(refs[0].at[from_x], refs[0].at[from_x], send_sems, recv_sems, 2, (x, y, 1 - c)),
            _remote(refs[0].at[from_y], refs[0].at[from_y], send_sems, recv_sems, 3, (x, y, 1 - c))]


def _gather_last(refs, send_sems, recv_sems):
    x, y, c = _my_place()
    blk = refs[0].at[_slot(1 - x, 1 - y, c)]
    return [_remote(blk, blk, send_sems, recv_sems, 0, (x, y, 1 - c))]


def _scatter_sibling(n):
    def copies(refs, send_sems, recv_sems):
        x, y, c = _my_place()
        return [_remote(refs[a].at[2 * j + 1 - c], refs[n + a].at[j], send_sems, recv_sems, 4 * a + j, (x, y, 1 - c))
                for a in range(n) for j in range(4)]
    return copies


def _scatter_chips(n):
    def copies(refs, send_sems, recv_sems):
        x, y, c = _my_place()
        out = []
        for a in range(n):
            for k in range(1, 4):
                px, py = _flip(x, k & 2), _flip(y, k & 1)
                out.append(_remote(refs[a].at[2 * px + py], refs[n + a].at[2 * x + y], send_sems, recv_sems,
                                   3 * a + k - 1, (px, py, c)))
        return out
    return copies


def _pair_sum(name, core, g, t):
    _, r, c = g.shape
    tr = _tile(r, 512, 8)
    g4 = g.reshape(4, 2, r, c)

    def body(core_ref, g_ref, t_ref, o_ref):
        o_ref[...] = (g_ref[...].astype(F32) + t_ref[...].astype(F32)).astype(BF16)

    return pl.pallas_call(
        body, name=name,
        grid_spec=pltpu.PrefetchScalarGridSpec(
            num_scalar_prefetch=1, grid=(4, r // tr),
            in_specs=[pl.BlockSpec((None, None, tr, c), lambda j, i, core_ref: (j, core_ref[0], i, 0)),
                      pl.BlockSpec((None, tr, c), lambda j, i, core_ref: (j, i, 0))],
            out_specs=pl.BlockSpec((None, tr, c), lambda j, i, core_ref: (j, i, 0))),
        out_shape=jax.ShapeDtypeStruct((4, r, c), BF16), compiler_params=_params(),
    )(core, g4, t)


BF16_TILE_ROWS = 16
IN_STRIDE = IN_SH // BF16_TILE_ROWS * BF16_TILE_ROWS
IN_WIN = IN_STRIDE + BF16_TILE_ROWS
IN_DENSE = IN_STRIDE * (N_DEV - 1) + IN_WIN
assert IN_SH * (N_DEV - 1) - IN_STRIDE * (N_DEV - 1) + IN_SH <= IN_WIN and D_IN <= IN_DENSE


def _window(ref, d):
    return ref.at[pl.ds(pl.multiple_of(IN_STRIDE * d, BF16_TILE_ROWS), IN_WIN)]


def _scatter_sibling_windows(refs, send_sems, recv_sems):
    x, y, c = _my_place()
    return [_remote(_window(refs[0], 2 * j + 1 - c), refs[1].at[j], send_sems, recv_sems, j, (x, y, 1 - c))
            for j in range(4)]


def _pair_sum_windows(name, core, g, t):
    c = g.shape[1]

    def body(core_ref, g_ref, t_ref, o_ref, buf, sem):
        copy = pltpu.make_async_copy(_window(g_ref, 2 * pl.program_id(0) + core_ref[0]), buf, sem)
        copy.start()
        copy.wait()
        o_ref[...] = (buf[...].astype(F32) + t_ref[...].astype(F32)).astype(BF16)

    slot = pl.BlockSpec((None, IN_WIN, c), lambda j, core_ref: (j, 0, 0))
    return pl.pallas_call(
        body, name=name,
        grid_spec=pltpu.PrefetchScalarGridSpec(
            num_scalar_prefetch=1, grid=(4,), in_specs=[ANY_SPEC, slot], out_specs=slot,
            scratch_shapes=[pltpu.VMEM((IN_WIN, c), BF16), pltpu.SemaphoreType.DMA]),
        out_shape=jax.ShapeDtypeStruct((4, IN_WIN, c), BF16), compiler_params=_params(),
    )(core, g, t)


def _adam_math(w, g, m, v):
    m2 = ADAM_B1 * m + (1.0 - ADAM_B1) * g
    v2 = ADAM_B2 * v + (1.0 - ADAM_B2) * (g * g)
    m_hat = m2 / (1.0 - ADAM_B1 ** ADAM_STEP)
    v_hat = v2 / (1.0 - ADAM_B2 ** ADAM_STEP)
    delta = -ADAM_LR * (m_hat / (jnp.sqrt(v_hat) + ADAM_EPS) + ADAM_WD * w)
    return delta, m2, v2


def _adam_sharded(name, chip, w, m, v, grads, row_off=0):
    _, r, c = w.shape
    windows = chip.shape[0] == 2
    tr = r if windows else _tile(r, 256, 8)
    tc = c if tr < r else _tile(c, 256, 128)
    boff = row_off // tr

    def body(chip_ref, w_ref, m_ref, v_ref, p0_ref, q0_ref, p1_ref, q1_ref, g_ref, d_ref, nm_ref, nv_ref):
        mine = chip_ref[0]

        def total(p_ref, q_ref):
            acc = None
            for j in range(4):
                part = jnp.where(mine == j, p_ref[...], q_ref[j]).astype(F32)
                acc = part if acc is None else acc + part
            return acc

        g = jnp.where(pl.program_id(0) == 0, total(p0_ref, q0_ref), total(p1_ref, q1_ref))
        if windows:
            first = chip_ref[1]
            g = pltpu.roll(g, jnp.where(first == 0, 0, r - first), 0)
        delta, m2, v2 = _adam_math(w_ref[...], g, m_ref[...], v_ref[...])
        g_ref[...] = g
        d_ref[...] = delta
        nm_ref[...] = m2
        nv_ref[...] = v2

    def grad_specs(layer):
        at = lambda l, i, j: (jnp.where(l == layer, boff + i, boff), jnp.where(l == layer, j, 0))
        return [pl.BlockSpec((None, tr, tc), lambda l, i, j, chip_ref: (chip_ref[0],) + at(l, i, j)),
                pl.BlockSpec((4, tr, tc), lambda l, i, j, chip_ref: (0,) + at(l, i, j))]

    wspec = pl.BlockSpec((None, tr, tc), lambda l, i, j, chip_ref: (l, i, j))
    sds = jax.ShapeDtypeStruct(w.shape, F32)
    return pl.pallas_call(
        body, name=name,
        grid_spec=pltpu.PrefetchScalarGridSpec(
            num_scalar_prefetch=1, grid=(2, r // tr, c // tc),
            in_specs=[wspec, wspec, wspec] + grad_specs(0) + grad_specs(1), out_specs=[wspec] * 4),
        out_shape=[sds] * 4, compiler_params=_params(),
    )(chip, w, m, v, grads[0][0], grads[0][1], grads[1][0], grads[1][1])


def _adam_small(name, w, m, v, g):
    def body(w_ref, m_ref, v_ref, g_ref, d_ref, nm_ref, nv_ref):
        delta, m2, v2 = _adam_math(w_ref[...], g_ref[...], m_ref[...], v_ref[...])
        d_ref[...] = delta
        nm_ref[...] = m2
        nv_ref[...] = v2

    sds = jax.ShapeDtypeStruct(w.shape, F32)
    vm = pl.BlockSpec(memory_space=pltpu.VMEM)
    return pl.pallas_call(body, name=name, in_specs=[vm] * 4, out_specs=[vm] * 3, out_shape=[sds] * 3,
                          compiler_params=_params())(w, m, v, g)


GATE_END = GATE_COL + 2 * HEADS


def _merge_dw_in(dwm_t, dwc_t):
    return jnp.concatenate([dwm_t[:GATE_END], dwc_t.reshape(3 * CONV_W, D_MODEL),
                            jnp.zeros((IN_DENSE - D_IN, D_MODEL), BF16)], axis=0)


def _pack128(parts):
    flat = jnp.concatenate([p.reshape(-1) for p in parts])
    n = flat.shape[0]
    rows = -(-n // 1024) * 8
    return jnp.pad(flat, (0, rows * 128 - n)).reshape(rows, 128)


def _unpack128(packed, shapes):
    flat = packed.reshape(-1)
    out, at = [], 0
    for s in shapes:
        n = int(np.prod(s))
        out.append(flat[at:at + n].reshape(s))
        at += n
    return out


def kernel(x, meta_tokens, norm_mix_w, w_in, b_gates, conv_w, mlstm_norm_w, w_out, norm_ffn_w, w_gate, w_up, w_down, norm_final_w, loss_target, m_meta_tokens, m_norm_mix_w, m_w_in, m_b_gates, m_conv_w, m_mlstm_norm_w, m_w_out, m_norm_ffn_w, m_w_gate, m_w_up, m_w_down, m_norm_final_w, v_meta_tokens, v_norm_mix_w, v_w_in, v_b_gates, v_conv_w, v_mlstm_norm_w, v_w_out, v_norm_ffn_w, v_w_gate, v_w_up, v_w_down, v_norm_final_w):
    seq = x.shape[1]
    rows = TOK0 + seq
    me = 4 * lax.axis_index("x") + 2 * lax.axis_index("y") + lax.axis_index("c")
    meta_sh = meta_tokens.shape[1]
    conv_sh = conv_w.shape[2]

    w_gate_t, m_w_gate_t, v_w_gate_t = (jnp.transpose(a, (0, 2, 1)) for a in (w_gate, m_w_gate, v_w_gate))
    w_up_t, m_w_up_t, v_w_up_t = (jnp.transpose(a, (0, 2, 1)) for a in (w_up, m_w_up, v_w_up))
    shards = []
    for l in range(DEPTH):
        shards += [jnp.transpose(w_in[l]).astype(BF16), w_out[l].astype(BF16), w_gate_t[l].astype(BF16),
                   w_up_t[l].astype(BF16), w_down[l].astype(BF16)]
    per_layer = ("w_in", "w_out", "w_gate", "w_up", "w_down")
    gather_names = [f"{nm}_{l}" for l in range(DEPTH) for nm in per_layer]
    gather_state = {}

    def gather_step(tag, after, start=None, relay=None, last=None, done=()):
        jobs, idx = [], []
        if relay is not None and relay < len(shards):
            jobs.append((gather_state[relay][0], (_gather_relay, 4), (_gather_first, gather_state[relay][1])))
            idx.append(relay)
        if start is not None and start < len(shards):
            buf = lax.dynamic_update_index_in_dim(lax.empty((N_DEV,) + shards[start].shape, BF16), shards[start], me, 0)
            jobs.append(([buf], (_gather_first, 3), None))
            idx.append(start)
        if last is not None:
            jobs.append((gather_state[last][0], (_gather_last, 1), (_gather_relay, gather_state[last][1])))
            idx.append(last)
        for i in done:
            jobs.append((gather_state[i][0], None, (_gather_last, gather_state[i][1])))
            idx.append(i)
        if not jobs:
            return after, []
        results, tok = _split_copies(f"gather_{tag}", jobs, after)
        for i, res in zip(idx, results):
            gather_state[i] = res
        return (after if tok is None else tok), [gather_state[i][0][0] for i in done]

    bias = [jnp.pad(b_gates[l].reshape(1, 2 * HEADS), ((0, 0), (0, 128 - 2 * HEADS))) for l in range(DEPTH)]
    nmix = [norm_mix_w[l].reshape(1, D_MODEL) for l in range(DEPTH)]
    nffn = [norm_ffn_w[l].reshape(1, D_MODEL) for l in range(DEPTH)]
    nmls = [mlstm_norm_w[l].reshape(1, MLSTM_W) for l in range(DEPTH)]
    weights = [dict() for _ in range(DEPTH)]
    saved = [dict() for _ in range(DEPTH)]

    def layer_fwd(l, h, after):
        w, s = weights[l], saved[l]
        k0 = len(per_layer) * l
        tok, _ = gather_step(f"l{l}_a", after, last=k0)
        _, (g_in,) = gather_step(f"l{l}_b", tok, done=[k0])
        tok, _ = gather_step(f"l{l}_c", g_in, relay=k0 + 1, start=k0 + 3)
        w["win_t"] = g_in.reshape(D_IN, D_MODEL)
        w["wc_t"] = w["win_t"][GATE_END:].reshape(3, CONV_W, D_MODEL)
        s["h0"] = h
        s["hn"] = _rms_fwd(f"norm_mix_{l}", h, nmix[l] + tok[0, 0])
        s["pm"] = _mm_nt(f"proj_mlstm_{l}", s["hn"], w["win_t"], F32, tn=640, tk=D_MODEL, n=PM_W)
        tok, _ = gather_step(f"l{l}_d", s["pm"], relay=k0 + 2, start=k0 + 4)
        tok, _ = gather_step(f"l{l}_d2", tok, last=k0 + 1)
        s["pc"] = _mm_nt_bcols(f"proj_conv_{l}", s["hn"], w["wc_t"], F32, dep=tok)
        hm, s["ht"], s["cs"], s["ns"], s["ms"] = _mlstm_fwd(f"mlstm_fwd_{l}", s["pm"], bias[l] + tok[:1], nmls[l])
        tok, _ = gather_step(f"l{l}_e", hm, relay=k0 + 3, start=k0 + 5)
        tok, _ = gather_step(f"l{l}_e2", tok, last=k0 + 2)
        hc = _conv_fwd(f"conv_fwd_{l}", s["pc"], conv_rows[l] + tok[0, 0])
        s["cat"] = jnp.concatenate([hm, hc], axis=1)
        _, (g_out,) = gather_step(f"l{l}_f", s["cat"], done=[k0 + 1])
        w["wo"] = g_out.reshape(D_MODEL, D_MODEL)
        s["h1"] = _mm_nn(f"out_proj_{l}", s["cat"], w["wo"], F32, res=s["h0"])
        tok, _ = gather_step(f"l{l}_g", s["h1"], relay=k0 + 4, start=k0 + 6)
        s["hf"] = _rms_fwd(f"norm_ffn_{l}", s["h1"], nffn[l] + tok[0, 0])
        tok, _ = gather_step(f"l{l}_h", s["hf"], last=k0 + 3)
        _, (g_gate, g_up) = gather_step(f"l{l}_i", tok, done=[k0 + 2, k0 + 3])
        w["wg_t"] = g_gate.reshape(D_FF, D_MODEL)
        w["wu_t"] = g_up.reshape(D_FF, D_MODEL)
        s["g"], s["u"], s["act"] = _ffn_in(f"ffn_in_{l}", s["hf"], w["wg_t"], w["wu_t"])
        tok, _ = gather_step(f"l{l}_j", s["act"], last=k0 + 4)
        _, (g_down,) = gather_step(f"l{l}_k", tok, done=[k0 + 4])
        w["wd"] = g_down.reshape(D_FF, D_MODEL)
        tok, _ = gather_step(f"l{l}_k2", tok, relay=k0 + 5, start=k0 + 7)
        return _mm_nn(f"ffn_out_{l}", s["act"], w["wd"], F32, res=s["h1"], dep=tok)

    tok, _ = gather_step("first", None, start=0)
    zero = tok[0, 0]
    small = jnp.concatenate(
        [meta_tokens + zero, jnp.pad(conv_w.reshape(DEPTH * 3, conv_sh), ((0, 2), (0, meta_sh - conv_sh)))], axis=0)
    slots = _exchange_small("gather_small", small, reduce=False)
    meta_full = jnp.transpose(slots[:, :N_META, :], (1, 0, 2)).reshape(N_META, D_MODEL)
    conv_full = jnp.transpose(slots[:, N_META:N_META + DEPTH * 3, :conv_sh], (1, 0, 2)).reshape(DEPTH, 3, CONV_W)
    conv_rows = [jnp.pad(conv_full[l], ((0, 5), (0, 0))) for l in range(DEPTH)]
    w_in_t, m_w_in_t, v_w_in_t = (
        jnp.pad(jnp.transpose(a + zero, (0, 2, 1)), ((0, 0), (0, IN_WIN - IN_SH), (0, 0)))
        for a in (w_in, m_w_in, v_w_in))
    tok, w_in_t, m_w_in_t, v_w_in_t, meta_full = lax.optimization_barrier(
        (tok, w_in_t, m_w_in_t, v_w_in_t, meta_full))
    tok, _ = gather_step("pre_a", tok, relay=0)
    tok, _ = gather_step("pre_b", tok, start=1)
    tok, _ = gather_step("pre_c", tok, start=2)
    h = jnp.concatenate([jnp.zeros((PAD_FRONT, D_MODEL), F32), meta_full, x[0]], axis=0)
    h = layer_fwd(0, h, tok)
    h = layer_fwd(1, h, h)

    dh, dh_b, d_final, loss_part = _final_loss("final_loss", h, norm_final_w.reshape(1, D_MODEL), loss_target[0])

    core = lax.axis_index("c").astype(jnp.int32).reshape(1)
    chip = (2 * lax.axis_index("x") + lax.axis_index("y")).astype(jnp.int32).reshape(1)
    scatter_state = {}

    def scatter_begin(nm, grad):
        dense = grad.ndim == 2
        land = lax.empty((4, IN_WIN, grad.shape[1]) if dense else (4,) + grad.shape[1:], BF16)
        copies = _scatter_sibling_windows if dense else _scatter_sibling(1)
        arrs, sems, tok = _split_copy(f"grad_sibling_start_{nm}", [grad, land], start=(copies, 4))
        scatter_state[nm] = (arrs, sems, copies)
        return tok

    def scatter_advance(nm, after):
        arrs, sems, copies = scatter_state[nm]
        arrs, _, _ = _split_copy(f"grad_sibling_done_{nm}", arrs, wait=(copies, sems), after=after)
        pair_sum = _pair_sum_windows if arrs[0].ndim == 2 else _pair_sum
        part = pair_sum(f"grad_pair_sum_{nm}", core, arrs[0], arrs[1])
        arrs, sems, tok = _split_copy(f"grad_chips_start_{nm}", [part, lax.empty(part.shape, BF16)],
                                      start=(_scatter_chips(1), 3))
        scatter_state[nm] = (arrs, sems)
        return tok

    def scattered(nm, after):
        arrs, sems = scatter_state[nm]
        arrs, _, _ = _split_copy(f"grad_chips_done_{nm}", arrs, wait=(_scatter_chips(1), sems), after=after)
        return arrs[0], arrs[1]

    d_mix, d_ffn, d_mls, d_bias, d_conv = ([None] * DEPTH for _ in range(5))

    def layer_bwd(l, dh, dh_b, tok):
        w, s = weights[l], saved[l]
        dg, du = _ffn_act_bwd(f"d_act_{l}", dh_b, w["wd"], s["g"], s["u"], dep=tok)
        dw_down = _mm_tn(f"dw_down_{l}", s["act"], dh_b, BF16, tm=1408, tn=1024)
        tok = scatter_begin(f"w_down_{l}", dw_down.reshape(N_DEV, FF_SH, D_MODEL))
        dhf = _mm_nn(f"d_ffn_gate_{l}", dg, w["wg_t"], F32, dep=tok)
        tok = scatter_advance(f"w_down_{l}", after=dhf)
        dhf = _mm_nn(f"d_ffn_up_{l}", du, w["wu_t"], F32, res=dhf, dep=tok)
        dw_gate = _mm_tn(f"dw_gate_{l}", dg, s["hf"], BF16, tm=1408, tn=1024)
        tok = scatter_begin(f"w_gate_{l}", dw_gate.reshape(N_DEV, FF_SH, D_MODEL))
        dw_up = _mm_tn(f"dw_up_{l}", du, s["hf"], BF16, tm=1408, tn=1024, dep=tok)
        tok = scatter_begin(f"w_up_{l}", dw_up.reshape(N_DEV, FF_SH, D_MODEL))
        dh1, dh1_b, d_ffn[l] = _rms_bwd(f"norm_ffn_bwd_{l}", s["h1"], nffn[l] + tok[0, 0], dhf, dh)
        tok = scatter_advance(f"w_gate_{l}", after=dh1)
        dcat = _mm_nt(f"d_cat_{l}", dh1_b, w["wo"], F32, tk=D_MODEL, dep=tok)
        tok = scatter_advance(f"w_up_{l}", after=dcat)
        dw_out = _mm_tn(f"dw_out_{l}", s["cat"], dh1_b, BF16, tn=1024, dep=tok)
        tok = scatter_begin(f"w_out_{l}", dw_out.reshape(N_DEV, OUT_SH, D_MODEL))
        dpm, d_mls[l], d_bias[l] = _mlstm_bwd(f"mlstm_bwd_{l}", dcat, s["pm"], s["ht"], s["cs"], s["ns"],
                                               s["ms"], bias[l] + tok[:1], nmls[l])
        dpc, d_conv[l] = _conv_bwd(f"conv_bwd_{l}", dcat, s["pc"], conv_rows[l])
        tok = scatter_advance(f"w_out_{l}", after=dpc)
        dwm_t = _mm_tn(f"dw_mlstm_{l}", dpm, s["hn"], BF16, tm=640, tn=1024, dep=tok)
        dwc_t = _mm_tn_acols(f"dw_conv_{l}", dpc, s["hn"], BF16)
        tok = scatter_begin(f"w_in_{l}", _merge_dw_in(dwm_t, dwc_t))
        dhn = _mm_nn_kt(f"d_norm_mlstm_{l}", dpm, w["win_t"], F32, tk=PM_W, dep=tok)
        dhn = _mm_nn_ksum(f"d_norm_conv_{l}", dpc, w["wc_t"], F32, res=dhn)
        tok = scatter_advance(f"w_in_{l}", after=dhn)
        dh, dh_b, d_mix[l] = _rms_bwd(f"norm_mix_bwd_{l}", s["h0"], nmix[l] + tok[0, 0], dhn, dh1)
        return dh, dh_b, tok

    dh, dh_b, tok = layer_bwd(1, dh, dh_b, None)
    dh, dh_b, tok_tail = layer_bwd(0, dh, dh_b, tok)

    pq = {}
    after = dh
    for l in reversed(range(DEPTH)):
        for nm in ("w_down", "w_gate", "w_up", "w_out", "w_in"):
            if (nm, l) != ("w_in", 0):
                pq[nm, l] = scattered(f"{nm}_{l}", after)
                after = pq[nm, l][0]
    untransposed = lambda outs: [jnp.transpose(o, (0, 2, 1)) for o in outs]
    g_out, d_out, nm_out, nv_out = _adam_sharded(
        "adam_w_out", chip, w_out, m_w_out, v_w_out, [pq["w_out", 0], pq["w_out", 1]])
    g_gate, d_gate, nm_gate, nv_gate = untransposed(_adam_sharded(
        "adam_w_gate", chip, w_gate_t, m_w_gate_t, v_w_gate_t, [pq["w_gate", 0], pq["w_gate", 1]]))
    g_up, d_up, nm_up, nv_up = untransposed(_adam_sharded(
        "adam_w_up", chip, w_up_t, m_w_up_t, v_w_up_t, [pq["w_up", 0], pq["w_up", 1]]))
    g_down, d_down, nm_down, nv_down = _adam_sharded(
        "adam_w_down", chip, w_down, m_w_down, v_w_down, [pq["w_down", 0], pq["w_down", 1]])
    pq["w_in", 0] = scattered("w_in_0", nv_down)
    chip_and_row = jnp.concatenate([chip, me.astype(jnp.int32).reshape(1)])
    g_in, d_in, nm_in, nv_in = untransposed([o[:, :IN_SH] for o in _adam_sharded(
        "adam_w_in", chip_and_row, w_in_t, m_w_in_t, v_w_in_t, [pq["w_in", 0], pq["w_in", 1]])])

    bg = jnp.concatenate([d_bias[l][0, :2 * HEADS] for l in range(DEPTH)])
    red_in = jnp.concatenate([
        dh[PAD_FRONT:TOK0], d_mix[0], d_mix[1], d_ffn[0], d_ffn[1], d_final,
        jnp.concatenate([d_mls[0], d_mls[1]], axis=1),
        jnp.stack([d_conv[l][:3] for l in range(DEPTH)]).reshape(3, 2 * CONV_W),
        jnp.pad(bg, (0, D_MODEL - bg.shape[0])).reshape(1, D_MODEL),
        jnp.pad(loss_part[:, :1], ((0, 0), (0, D_MODEL - 1))),
        jnp.zeros((5, D_MODEL), F32) + tok_tail[0, 0]], axis=0)
    red = _exchange_small("reduce_small", red_in, reduce=True)
    loss = red[26, 0]
    g_meta = lax.dynamic_slice_in_dim(red[:N_META], me * meta_sh, meta_sh, axis=1)
    g_mix, g_ffn, g_final = red[16:18], red[18:20], red[20]
    g_mls = red[21].reshape(DEPTH, MLSTM_W)
    g_conv = lax.dynamic_slice_in_dim(red[22:25].reshape(DEPTH, 3, CONV_W), me * conv_sh, conv_sh, axis=2)
    g_bias = red[25, :DEPTH * 2 * HEADS].reshape(DEPTH, 2 * HEADS)

    small_w = [meta_tokens, norm_mix_w, b_gates, conv_w, mlstm_norm_w, norm_ffn_w, norm_final_w]
    small_m = [m_meta_tokens, m_norm_mix_w, m_b_gates, m_conv_w, m_mlstm_norm_w, m_norm_ffn_w, m_norm_final_w]
    small_v = [v_meta_tokens, v_norm_mix_w, v_b_gates, v_conv_w, v_mlstm_norm_w, v_norm_ffn_w, v_norm_final_w]
    small_g = [g_meta, g_mix, g_bias, g_conv, g_mls, g_ffn, g_final]
    shapes = [a.shape for a in small_w]
    packed = _adam_small("adam_small", _pack128(small_w), _pack128(small_m), _pack128(small_v), _pack128(small_g))
    (d_meta, d_nmix, d_bg, d_cw, d_nmls, d_nffn, d_nfin), (nm_meta, nm_nmix, nm_bg, nm_cw, nm_nmls, nm_nffn, nm_nfin), \
        (nv_meta, nv_nmix, nv_bg, nv_cw, nv_nmls, nv_nffn, nv_nfin) = (_unpack128(p, shapes) for p in packed)

    grad_x = dh[TOK0:].reshape(1, seq, D_MODEL)
    return (loss, grad_x,
            g_meta, g_mix, g_in, g_bias, g_conv, g_mls, g_out, g_ffn, g_gate, g_up, g_down, g_final,
            d_meta, d_nmix, d_in, d_bg, d_cw, d_nmls, d_out, d_nffn, d_gate, d_up, d_down, d_nfin,
            nm_meta, nm_nmix, nm_in, nm_bg, nm_cw, nm_nmls, nm_out, nm_nffn, nm_gate, nm_up, nm_down, nm_nfin,
            nv_meta, nv_nmix, nv_in, nv_bg, nv_cw, nv_nmls, nv_out, nv_nffn, nv_gate, nv_up, nv_down, nv_nfin)
```

```python
import functools

import numpy as np
import jax
import jax.numpy as jnp
from jax import lax
from jax.experimental import pallas as pl
from jax.experimental.pallas import tpu as pltpu

F32 = jnp.float32
BF16 = jnp.bfloat16
MESH = pl.DeviceIdType.MESH

D_MODEL = 2048
DEPTH = 2
N_META = 16
MLSTM_W = 1024
CONV_W = 1024
HEADS = 4
DV = 256
DQK = 128
QK_W = 512
CHUNK = 64
PAD_FRONT = 48
TOK0 = PAD_FRONT + N_META
D_FF = 5632
N_DEV = 8
FF_SH = D_FF // N_DEV
D_IN = 6152
IN_SH = D_IN // N_DEV
OUT_SH = D_MODEL // N_DEV
GATE_COL = 3072
PM_W = GATE_COL + 128
GATE_CAP = 15.0
EPS = 1e-6
QSCALE = DQK ** -0.5

ADAM_LR = 0.001
ADAM_B1 = 0.9
ADAM_B2 = 0.999
ADAM_EPS = 1e-08
ADAM_WD = 0.01
ADAM_STEP = 10

V7X_VMEM_LIMIT = 50 * 1024 * 1024
V7X_MXU_COLS = 256


def _params(**kw):
    return pltpu.CompilerParams(vmem_limit_bytes=V7X_VMEM_LIMIT, **kw)


def _tile(n, target, mult):
    best = None
    for t in range(mult, min(n, target) + 1, mult):
        if n % t == 0:
            best = t
    return best if best is not None else n


def _sigmoid(x):
    return 1.0 / (1.0 + jnp.exp(-x))


NN = ((1,), (0,))
NT = ((1,), (1,))
TN = ((0,), (0,))


def _matmul(name, a, b, out_shape, out_dtype, grid, a_bs, b_bs, o_bs, dims, nk, acc_shape=None,
            res=None, res_bs=None, dep=None):
    has_res = res is not None
    n_in = 2 + has_res + (dep is not None)

    def body(*refs):
        a_ref, b_ref = refs[0], refs[1]
        r_ref = refs[2] if has_res else None
        o_ref = refs[n_in]
        x = lax.dot_general(a_ref[...], b_ref[...], (dims, ((), ())), preferred_element_type=F32)
        if nk == 1:
            if has_res:
                x = x + r_ref[...]
            o_ref[...] = x.astype(o_ref.dtype)
            return
        acc = refs[n_in + 1]
        k = pl.program_id(len(grid) - 1)

        @pl.when(k == 0)
        def _():
            acc[...] = (x + r_ref[...]) if has_res else x

        @pl.when(k > 0)
        def _():
            acc[...] += x

        @pl.when(k == nk - 1)
        def _():
            o_ref[...] = acc[...].astype(o_ref.dtype)

    ins = [a, b] + ([res] if has_res else [])
    specs = [a_bs, b_bs] + ([res_bs] if has_res else [])
    if dep is not None:
        ins.append(dep)
        specs.append(pl.BlockSpec((8, 128), lambda *_: (0, 0)))
    scratch = [pltpu.VMEM(acc_shape, F32)] if nk > 1 else []
    return pl.pallas_call(
        body, name=name, grid=grid, in_specs=specs, out_specs=o_bs,
        out_shape=jax.ShapeDtypeStruct(out_shape, out_dtype), scratch_shapes=scratch,
        compiler_params=_params(),
    )(*ins)


def _mm_nn(name, a, b, out_dtype, res=None, tm=1056, tn=512, dep=None):
    r, k = a.shape
    n = b.shape[1]
    tm, tn = _tile(r, tm, 8), _tile(n, tn, 128)
    return _matmul(name, a, b, (r, n), out_dtype, (r // tm, n // tn, 1),
                   pl.BlockSpec((tm, k), lambda i, j, s: (i, 0)),
                   pl.BlockSpec((k, tn), lambda i, j, s: (0, j)),
                   pl.BlockSpec((tm, tn), lambda i, j, s: (i, j)), NN, 1,
                   res=res, res_bs=pl.BlockSpec((tm, tn), lambda i, j, s: (i, j)), dep=dep)


def _mm_nn_kt(name, a, b, out_dtype, tm=1056, tn=1024, tk=640, dep=None):
    r, k = a.shape
    n = b.shape[1]
    tm, tn, tk = _tile(r, tm, 8), _tile(n, tn, 128), _tile(k, tk, 128)
    nk = k // tk
    return _matmul(name, a, b, (r, n), out_dtype, (r // tm, n // tn, nk),
                   pl.BlockSpec((tm, tk), lambda i, j, s: (i, s)),
                   pl.BlockSpec((tk, tn), lambda i, j, s: (s, j)),
                   pl.BlockSpec((tm, tn), lambda i, j, s: (i, j)), NN, nk, acc_shape=(tm, tn), dep=dep)


def _mm_nn_ksum(name, a3, b3, out_dtype, res=None, tm=1056, tn=1024, dep=None):
    e, r, kb = a3.shape
    n = b3.shape[2]
    tm, tn = _tile(r, tm, 8), _tile(n, tn, 128)
    return _matmul(name, a3, b3, (r, n), out_dtype, (r // tm, n // tn, e),
                   pl.BlockSpec((None, tm, kb), lambda i, j, s: (s, i, 0)),
                   pl.BlockSpec((None, kb, tn), lambda i, j, s: (s, 0, j)),
                   pl.BlockSpec((tm, tn), lambda i, j, s: (i, j)), NN, e, acc_shape=(tm, tn),
                   res=res, res_bs=pl.BlockSpec((tm, tn), lambda i, j, s: (i, j)), dep=dep)


def _mm_nt(name, a, b, out_dtype, res=None, tm=1056, tn=512, tk=640, n=None, dep=None):
    r, k = a.shape
    n = b.shape[0] if n is None else n
    tm, tn, tk = _tile(r, tm, 8), _tile(n, tn, 128), _tile(k, tk, 128)
    nk = k // tk
    return _matmul(name, a, b, (r, n), out_dtype, (r // tm, n // tn, nk),
                   pl.BlockSpec((tm, tk), lambda i, j, s: (i, s)),
                   pl.BlockSpec((tn, tk), lambda i, j, s: (j, s)),
                   pl.BlockSpec((tm, tn), lambda i, j, s: (i, j)), NT, nk, acc_shape=(tm, tn),
                   res=res, res_bs=pl.BlockSpec((tm, tn), lambda i, j, s: (i, j)), dep=dep)


def _mm_nt_bcols(name, a, b3, out_dtype, tm=1056, dep=None):
    r, k = a.shape
    e, n, _ = b3.shape
    tm = _tile(r, tm, 8)
    return _matmul(name, a, b3, (e, r, n), out_dtype, (r // tm, e, 1),
                   pl.BlockSpec((tm, k), lambda i, g, s: (i, 0)),
                   pl.BlockSpec((None, n, k), lambda i, g, s: (g, 0, 0)),
                   pl.BlockSpec((None, tm, n), lambda i, g, s: (g, i, 0)), NT, 1, dep=dep)


def _mm_tn(name, a, b, out_dtype, tm=1024, tn=640, dep=None):
    r, m = a.shape
    n = b.shape[1]
    tm, tn = _tile(m, tm, 128), _tile(n, tn, 128)
    return _matmul(name, a, b, (m, n), out_dtype, (m // tm, n // tn, 1),
                   pl.BlockSpec((r, tm), lambda i, j, s: (0, i)),
                   pl.BlockSpec((r, tn), lambda i, j, s: (0, j)),
                   pl.BlockSpec((tm, tn), lambda i, j, s: (i, j)), TN, 1, dep=dep)


def _mm_tn_acols(name, a3, b, out_dtype, tn=1024, dep=None):
    e, r, m = a3.shape
    n = b.shape[1]
    tn = _tile(n, tn, 128)
    return _matmul(name, a3, b, (e, m, n), out_dtype, (n // tn, e, 1),
                   pl.BlockSpec((None, r, m), lambda j, g, s: (g, 0, 0)),
                   pl.BlockSpec((r, tn), lambda j, g, s: (0, j)),
                   pl.BlockSpec((None, m, tn), lambda j, g, s: (g, 0, j)), TN, 1, dep=dep)


def _rms_fwd(name, h, w):
    r, d = h.shape
    tr = _tile(r, 264, 8)

    def body(h_ref, w_ref, o_ref):
        x = h_ref[...]
        rs = lax.rsqrt(jnp.mean(x * x, axis=1, keepdims=True) + EPS)
        o_ref[...] = (x * rs * w_ref[...]).astype(BF16)

    return pl.pallas_call(
        body, name=name, grid=(r // tr,),
        in_specs=[pl.BlockSpec((tr, d), lambda i: (i, 0)), pl.BlockSpec((1, d), lambda i: (0, 0))],
        out_specs=pl.BlockSpec((tr, d), lambda i: (i, 0)),
        out_shape=jax.ShapeDtypeStruct((r, d), BF16), compiler_params=_params(),
    )(h, w)


def _proj_res_norm(name, a, b, res, w, tm=528):
    r, k = a.shape
    d = b.shape[1]
    tm = _tile(r, tm, 8)

    def body(a_ref, b_ref, r_ref, w_ref, y_ref, n_ref):
        y = lax.dot_general(a_ref[...], b_ref[...], (NN, ((), ())), preferred_element_type=F32) + r_ref[...]
        y_ref[...] = y
        rs = lax.rsqrt(jnp.mean(y * y, axis=1, keepdims=True) + EPS)
        n_ref[...] = (y * rs * w_ref[...]).astype(BF16)

    row = pl.BlockSpec((tm, d), lambda i: (i, 0))
    return pl.pallas_call(
        body, name=name, grid=(r // tm,),
        in_specs=[pl.BlockSpec((tm, k), lambda i: (i, 0)), pl.BlockSpec((k, d), lambda i: (0, 0)), row,
                  pl.BlockSpec((1, d), lambda i: (0, 0))],
        out_specs=[row, row],
        out_shape=[jax.ShapeDtypeStruct((r, d), F32), jax.ShapeDtypeStruct((r, d), BF16)],
        compiler_params=_params(),
    )(a, b, res, w)


def _rms_bwd(name, x, w, dy, dres):
    r, d = x.shape
    tr = _tile(r, 264, 8)

    def body(x_ref, w_ref, dy_ref, dr_ref, dx_ref, dxb_ref, dw_ref):
        xv = x_ref[...]
        g = dy_ref[...]
        rs = lax.rsqrt(jnp.mean(xv * xv, axis=1, keepdims=True) + EPS)
        wg = g * w_ref[...]
        dx = rs * wg - xv * (rs * rs * rs) * jnp.mean(xv * wg, axis=1, keepdims=True) + dr_ref[...]
        dx_ref[...] = dx
        dxb_ref[...] = dx.astype(BF16)
        part = jnp.sum(g * xv * rs, axis=0, keepdims=True)

        @pl.when(pl.program_id(0) == 0)
        def _():
            dw_ref[...] = part

        @pl.when(pl.program_id(0) > 0)
        def _():
            dw_ref[...] += part

    row = pl.BlockSpec((tr, d), lambda i: (i, 0))
    vec = pl.BlockSpec((1, d), lambda i: (0, 0))
    return pl.pallas_call(
        body, name=name, grid=(r // tr,), in_specs=[row, vec, row, row], out_specs=[row, row, vec],
        out_shape=[jax.ShapeDtypeStruct((r, d), F32), jax.ShapeDtypeStruct((r, d), BF16),
                   jax.ShapeDtypeStruct((1, d), F32)],
        compiler_params=_params(),
    )(x, w, dy, dres)


def _final_loss(name, h, w, target):
    r, d = h.shape
    nb = r // CHUNK

    def body(h_ref, w_ref, t_ref, dh_ref, dhb_ref, dw_ref, ls_ref):
        i = pl.program_id(0)

        @pl.when(i == 0)
        def _():
            dh_ref[...] = jnp.zeros_like(dh_ref)
            dhb_ref[...] = jnp.zeros_like(dhb_ref)
            dw_ref[...] = jnp.zeros_like(dw_ref)
            ls_ref[...] = jnp.zeros_like(ls_ref)

        @pl.when(i > 0)
        def _():
            xv = h_ref[...]
            wv = w_ref[...]
            rs = lax.rsqrt(jnp.mean(xv * xv, axis=1, keepdims=True) + EPS)
            err = xv * rs * wv - t_ref[...]
            sq = jnp.sum(jnp.sum(err * err, axis=1, keepdims=True), axis=0, keepdims=True)
            ls_ref[...] += jnp.broadcast_to(sq * (0.5 / d), ls_ref.shape)
            g = err * (1.0 / d)
            wg = g * wv
            dx = rs * wg - xv * (rs * rs * rs) * jnp.mean(xv * wg, axis=1, keepdims=True)
            dh_ref[...] = dx
            dhb_ref[...] = dx.astype(BF16)
            dw_ref[...] += jnp.sum(g * xv * rs, axis=0, keepdims=True)

    row = pl.BlockSpec((CHUNK, d), lambda i: (i, 0))
    vec = pl.BlockSpec((1, d), lambda i: (0, 0))
    return pl.pallas_call(
        body, name=name, grid=(nb,),
        in_specs=[row, vec, pl.BlockSpec((CHUNK, d), lambda i: (jnp.maximum(i - 1, 0), 0))],
        out_specs=[row, row, vec, pl.BlockSpec((1, 128), lambda i: (0, 0))],
        out_shape=[jax.ShapeDtypeStruct((r, d), F32), jax.ShapeDtypeStruct((r, d), BF16),
                   jax.ShapeDtypeStruct((1, d), F32), jax.ShapeDtypeStruct((1, 128), F32)],
        compiler_params=_params(),
    )(h, w, target)


def _ffn_in(name, hf, wg_t, wu_t, dep=None, tm=1056, tn=512):
    r, d = hf.shape
    f = wg_t.shape[0]
    tm, tn = _tile(r, tm, 8), _tile(f, tn, 128)

    def body(h_ref, wg_ref, wu_ref, *rest):
        g_ref, u_ref, a_ref = rest[-3:]
        tr = _tile(tm, 264, 8)
        for r0 in range(0, tm, tr):
            for c0 in range(0, tn, V7X_MXU_COLS):
                rows, cols = slice(r0, r0 + tr), slice(c0, c0 + V7X_MXU_COLS)
                x = h_ref[rows, :]
                g = lax.dot_general(x, wg_ref[cols, :], (NT, ((), ())), preferred_element_type=F32)
                u = lax.dot_general(x, wu_ref[cols, :], (NT, ((), ())), preferred_element_type=F32)
                g_ref[rows, cols] = g.astype(BF16)
                u_ref[rows, cols] = u.astype(BF16)
                a_ref[rows, cols] = (g * _sigmoid(g) * u).astype(BF16)

    wspec = pl.BlockSpec((tn, d), lambda i, j: (j, 0))
    ospec = pl.BlockSpec((tm, tn), lambda i, j: (i, j))
    ins, specs = [hf, wg_t, wu_t], [pl.BlockSpec((tm, d), lambda i, j: (i, 0)), wspec, wspec]
    if dep is not None:
        ins.append(dep)
        specs.append(pl.BlockSpec((8, 128), lambda *_: (0, 0)))
    return pl.pallas_call(
        body, name=name, grid=(r // tm, f // tn), in_specs=specs, out_specs=[ospec] * 3,
        out_shape=[jax.ShapeDtypeStruct((r, f), BF16)] * 3, compiler_params=_params(),
    )(*ins)


def _ffn_act_bwd(name, dh, wd, g, u, dep=None, tm=1056, tn=512):
    r, d = dh.shape
    f = wd.shape[0]
    tm, tn = _tile(r, tm, 8), _tile(f, tn, 128)

    def body(dh_ref, wd_ref, g_ref, u_ref, *rest):
        dg_ref, du_ref = rest[-2:]
        tr = _tile(tm, 264, 8)
        for r0 in range(0, tm, tr):
            for c0 in range(0, tn, V7X_MXU_COLS):
                rows, cols = slice(r0, r0 + tr), slice(c0, c0 + V7X_MXU_COLS)
                da = lax.dot_general(dh_ref[rows, :], wd_ref[cols, :], (NT, ((), ())), preferred_element_type=F32)
                gv = g_ref[rows, cols].astype(F32)
                s = _sigmoid(gv)
                t = da * s
                du_ref[rows, cols] = (t * gv).astype(BF16)
                dg_ref[rows, cols] = (t * u_ref[rows, cols].astype(F32) * (1.0 + gv - gv * s)).astype(BF16)

    tile = pl.BlockSpec((tm, tn), lambda i, j: (i, j))
    ins = [dh, wd, g, u]
    specs = [pl.BlockSpec((tm, d), lambda i, j: (i, 0)), pl.BlockSpec((tn, d), lambda i, j: (j, 0)), tile, tile]
    if dep is not None:
        ins.append(dep)
        specs.append(pl.BlockSpec((8, 128), lambda *_: (0, 0)))
    return pl.pallas_call(
        body, name=name, grid=(r // tm, f // tn), in_specs=specs, out_specs=[tile] * 2,
        out_shape=[jax.ShapeDtypeStruct((r, f), BF16)] * 2, compiler_params=_params(),
    )(*ins)


def _shift_down(a, k):
    row = lax.broadcasted_iota(jnp.int32, a.shape, 0)
    return jnp.where(row >= k, pltpu.roll(a, k, 0), 0.0)


def _shift_up(a, k):
    n = a.shape[0]
    row = lax.broadcasted_iota(jnp.int32, a.shape, 0)
    return jnp.where(row < n - k, pltpu.roll(a, n - k, 0), 0.0)


def _conv_fwd(name, pc, cw, cat):
    _, r, w = pc.shape
    nblk = w // 128

    def body(pc_ref, cw_ref, cat_ref, o_ref):
        a = pc_ref[2] * pc_ref[0]
        cwv = cw_ref[...]
        conv = _shift_down(a, 2) * cwv[0:1] + _shift_down(a, 1) * cwv[1:2] + a * cwv[2:3]
        o_ref[...] = (pc_ref[1] * conv).astype(BF16)

    return pl.pallas_call(
        body, name=name, grid=(nblk,),
        in_specs=[pl.BlockSpec((3, r, 128), lambda j: (0, 0, j)), pl.BlockSpec((8, 128), lambda j: (0, j)),
                  pl.BlockSpec(memory_space=pl.ANY)],
        out_specs=pl.BlockSpec((r, 128), lambda j: (0, nblk + j)),
        out_shape=jax.ShapeDtypeStruct(cat.shape, BF16), input_output_aliases={2: 0},
        compiler_params=_params(),
    )(pc, cw, cat)


def _conv_bwd(name, dcat, pc, cw):
    _, r, w = pc.shape
    nblk = w // 128

    def body(dy_ref, pc_ref, cw_ref, dpc_ref, dcw_ref):
        u, gb, gc = pc_ref[0], pc_ref[1], pc_ref[2]
        cwv = cw_ref[...]
        dy = dy_ref[...]
        a = gc * u
        a1, a2 = _shift_down(a, 1), _shift_down(a, 2)
        conv = a2 * cwv[0:1] + a1 * cwv[1:2] + a * cwv[2:3]
        dconv = dy * gb
        da = dconv * cwv[2:3] + _shift_up(dconv, 1) * cwv[1:2] + _shift_up(dconv, 2) * cwv[0:1]
        dpc_ref[0] = (da * gc).astype(BF16)
        dpc_ref[1] = (dy * conv).astype(BF16)
        dpc_ref[2] = (da * u).astype(BF16)
        row = lax.broadcasted_iota(jnp.int32, (8, 128), 0)
        dw0 = jnp.sum(dconv * a2, axis=0, keepdims=True)
        dw1 = jnp.sum(dconv * a1, axis=0, keepdims=True)
        dw2 = jnp.sum(dconv * a, axis=0, keepdims=True)
        dcw_ref[...] = jnp.where(row == 0, dw0, jnp.where(row == 1, dw1, jnp.where(row == 2, dw2, 0.0)))

    return pl.pallas_call(
        body, name=name, grid=(nblk,),
        in_specs=[pl.BlockSpec((r, 128), lambda j: (0, nblk + j)),
                  pl.BlockSpec((3, r, 128), lambda j: (0, 0, j)), pl.BlockSpec((8, 128), lambda j: (0, j))],
        out_specs=[pl.BlockSpec((3, r, 128), lambda j: (0, 0, j)), pl.BlockSpec((8, 128), lambda j: (0, j))],
        out_shape=[jax.ShapeDtypeStruct((3, r, w), BF16), jax.ShapeDtypeStruct((8, w), F32)],
        compiler_params=_params(),
    )(dcat, pc, cw)


def _dot(a, b, dims):
    return lax.dot_general(a, b, (dims, ((), ())), preferred_element_type=F32)


def _col_to_row(xc, eye):
    return jnp.sum(jnp.where(eye, xc, 0.0), axis=0, keepdims=True)


def _row_to_col(xr, eye):
    return jnp.sum(jnp.where(eye, xr, 0.0), axis=1, keepdims=True)


def _gate_tiles(graw, bias, row0):
    th = jnp.tanh((graw + bias) / GATE_CAP)
    z = GATE_CAP * th
    row = lax.broadcasted_iota(jnp.int32, graw.shape, 0) + row0
    real = row >= PAD_FRONT
    li = jnp.where(real, z, -jnp.inf)
    lf = jnp.where(real, jnp.minimum(z, 0.0) - jnp.log(1.0 + jnp.exp(-jnp.abs(z))), 0.0)
    return th, z, li, lf, real


def _interleave(gens):
    results = [None] * len(gens)
    live = list(enumerate(gens))
    while live:
        still = []
        for i, gen in live:
            try:
                next(gen)
                still.append((i, gen))
            except StopIteration as stop:
                results[i] = stop.value
        live = still
    return results


def _chunk_common(pm, h, li, lf, cst, nst, mst, tril, eye):
    kraw = pm[:, QK_W + h * DQK:QK_W + (h + 1) * DQK]
    q = (pm[:, h * DQK:(h + 1) * DQK] * QSCALE).astype(BF16)
    yield
    k = kraw.astype(BF16)
    v = pm[:, 2 * QK_W + h * DV:2 * QK_W + (h + 1) * DV].astype(BF16)
    yield
    li_c = li[:, h:h + 1]
    lf_c = lf[:, HEADS + h:HEADS + h + 1]
    li_r = _col_to_row(li_c, eye)
    yield
    lf_r = _col_to_row(lf_c, eye)
    yield
    b_c = jnp.sum(jnp.where(tril, lf_r, 0.0), axis=1, keepdims=True)
    yield
    b_r = _col_to_row(b_c, eye)
    yield
    dmat = jnp.where(tril, b_c - b_r + li_r, -jnp.inf)
    inter = b_c + mst
    yield
    mt = jnp.maximum(inter, jnp.max(dmat, axis=1, keepdims=True))
    yield
    w_inter = jnp.exp(inter - mt)
    p = jnp.exp(dmat - mt)
    yield
    s = _dot(q, k, NT) * p
    yield
    cb = cst.astype(BF16)
    nb = nst.astype(BF16).astype(F32)
    qc = _dot(q, cb, NN)
    yield
    qn = jnp.sum(q.astype(F32) * nb, axis=1, keepdims=True)
    yield
    den = w_inter * qn + jnp.sum(s, axis=1, keepdims=True)
    yield
    dn = jnp.maximum(jnp.abs(den), jnp.exp(-mt))
    b_end = b_c[CHUNK - 1:CHUNK, :]
    decay = b_end - b_c + li_c
    yield
    m_new = jnp.maximum(b_end + mst, jnp.max(decay, axis=0, keepdims=True))
    yield
    w_old = jnp.exp(b_end + mst - m_new)
    w_in = jnp.exp(decay - m_new)
    kw = (w_in * kraw).astype(BF16)
    yield
    return dict(q=q, k=k, v=v, kraw=kraw, mt=mt, w_inter=w_inter, p=p, s=s, cb=cb, nb=nb, qc=qc, qn=qn,
                den=den, dn=dn, m_new=m_new, w_old=w_old, w_in=w_in, kw=kw)


def _chunks_per_step(nc):
    return 1


def _mlstm_fwd(name, pm, bias, nw):
    r = pm.shape[0]
    nc = r // CHUNK
    grp = _chunks_per_step(nc)

    def body(pm_ref, b_ref, nw_ref, hm_ref, ht_ref, cs_ref, ns_ref, ms_ref, c_scr, n_scr, m_scr):
        step = pl.program_id(0)

        @pl.when(step == 0)
        def _():
            c_scr[...] = jnp.zeros_like(c_scr)
            n_scr[...] = jnp.zeros_like(n_scr)
            m_scr[...] = jnp.zeros_like(m_scr)

        rr = lax.broadcasted_iota(jnp.int32, (CHUNK, CHUNK), 0)
        cc = lax.broadcasted_iota(jnp.int32, (CHUNK, CHUNK), 1)
        tril, eye = cc <= rr, cc == rr
        bv, nwv = b_ref[...], nw_ref[...]
        states = [(c_scr[h], n_scr[h], m_scr[h]) for h in range(HEADS)]
        for g in range(grp):
            rows = slice(g * CHUNK, (g + 1) * CHUNK)
            pmv = pm_ref[rows, :]
            _, _, li, lf, _ = _gate_tiles(pmv[:, GATE_COL:GATE_COL + 128], bv, (step * grp + g) * CHUNK)
            def head(h, cst, nst, mst, g=g, rows=rows, pmv=pmv, li=li, lf=lf):
                f = yield from _chunk_common(pmv, h, li, lf, cst, nst, mst, tril, eye)
                num = f["w_inter"] * f["qc"] + _dot(f["s"].astype(BF16), f["v"], NN)
                yield
                hh = num / f["dn"]
                yield
                c_new = f["w_old"] * cst + _dot(f["kw"], f["v"], TN)
                yield
                n_new = f["w_old"] * nst + jnp.sum(
                    f["w_in"].astype(BF16).astype(F32) * f["k"].astype(F32), axis=0, keepdims=True)
                yield
                sl = slice(h * DV, (h + 1) * DV)
                rs = lax.rsqrt(jnp.mean(hh * hh, axis=1, keepdims=True) + EPS)
                yield
                og = pmv[:, 2 * QK_W + MLSTM_W + h * DV:2 * QK_W + MLSTM_W + (h + 1) * DV]
                cs_ref[g, h] = cst
                ns_ref[g, h] = nst
                ms_ref[g, h] = mst
                ht_ref[rows, sl] = hh
                yield
                hm_ref[rows, sl] = (_sigmoid(og) * (hh * rs * nwv[:, sl])).astype(BF16)
                return c_new, n_new, f["m_new"]

            states = _interleave([head(h, *states[h]) for h in range(HEADS)])
        for h, (cst, nst, mst) in enumerate(states):
            c_scr[h] = cst
            n_scr[h] = nst
            m_scr[h] = mst

    return pl.pallas_call(
        body, name=name, grid=(nc // grp,),
        in_specs=[pl.BlockSpec((grp * CHUNK, PM_W), lambda i: (i, 0)), pl.BlockSpec((1, 128), lambda i: (0, 0)),
                  pl.BlockSpec((1, MLSTM_W), lambda i: (0, 0))],
        out_specs=[pl.BlockSpec((grp * CHUNK, MLSTM_W), lambda i: (i, 0)),
                   pl.BlockSpec((grp * CHUNK, MLSTM_W), lambda i: (i, 0)),
                   pl.BlockSpec((grp, HEADS, DQK, DV), lambda i: (i, 0, 0, 0)),
                   pl.BlockSpec((grp, HEADS, 1, DQK), lambda i: (i, 0, 0, 0)),
                   pl.BlockSpec((grp, HEADS, 1, 1), lambda i: (i, 0, 0, 0))],
        out_shape=[jax.ShapeDtypeStruct((r, MLSTM_W + CONV_W), BF16), jax.ShapeDtypeStruct((r, MLSTM_W), F32),
                   jax.ShapeDtypeStruct((nc, HEADS, DQK, DV), F32),
                   jax.ShapeDtypeStruct((nc, HEADS, 1, DQK), F32),
                   jax.ShapeDtypeStruct((nc, HEADS, 1, 1), F32)],
        scratch_shapes=[pltpu.VMEM((HEADS, DQK, DV), F32), pltpu.VMEM((HEADS, 1, DQK), F32),
                        pltpu.VMEM((HEADS, 1, 1), F32)],
        compiler_params=_params(),
    )(pm, bias, nw)


def _mlstm_bwd(name, dcat, pm, ht, cs, ns, ms, bias, nw):
    r = pm.shape[0]
    nc = r // CHUNK
    grp = _chunks_per_step(nc)
    nsteps = nc // grp

    def body(dy_ref, pm_ref, ht_ref, cs_ref, ns_ref, ms_ref, b_ref, nw_ref, dpm_ref, dnw_ref, db_ref,
             dc_scr, dn_scr):
        step = pl.program_id(0)

        @pl.when(step == 0)
        def _():
            dc_scr[...] = jnp.zeros_like(dc_scr)
            dn_scr[...] = jnp.zeros_like(dn_scr)
            dnw_ref[...] = jnp.zeros_like(dnw_ref)
            db_ref[...] = jnp.zeros_like(db_ref)

        rr = lax.broadcasted_iota(jnp.int32, (CHUNK, CHUNK), 0)
        cc = lax.broadcasted_iota(jnp.int32, (CHUNK, CHUNK), 1)
        tril, eye, triu = cc <= rr, cc == rr, cc >= rr
        lane = lax.broadcasted_iota(jnp.int32, (CHUNK, 128), 1)
        rowid = lax.broadcasted_iota(jnp.int32, (CHUNK, 1), 0)
        bv, nwv = b_ref[...], nw_ref[...]
        carried = [(dc_scr[h], dn_scr[h]) for h in range(HEADS)]
        dnw_acc = [jnp.zeros((1, DV), F32) for _ in range(HEADS)]
        db_acc = jnp.zeros((1, 128), F32)
        for g in reversed(range(grp)):
            rows = slice(g * CHUNK, (g + 1) * CHUNK)
            ci = (nsteps - 1 - step) * grp + g
            pmv = pm_ref[rows, :]
            th, z, li, lf, real = _gate_tiles(pmv[:, GATE_COL:GATE_COL + 128], bv, ci * CHUNK)
            heads = _interleave([
                _mlstm_bwd_head(h, pmv, ht_ref[rows, h * DV:(h + 1) * DV], dy_ref[rows, h * DV:(h + 1) * DV], nwv,
                                li, lf, cs_ref[g, h], ns_ref[g, h], ms_ref[g, h], carried[h][0], carried[h][1],
                                tril, eye, triu, lane, rowid, dpm_ref, rows)
                for h in range(HEADS)])
            carried = [(dc_new, dn_new) for _, dc_new, dn_new, _ in heads]
            dgt = heads[0][0] + heads[1][0] + heads[2][0] + heads[3][0]
            dnw_acc = [dnw_acc[h] + heads[h][3] for h in range(HEADS)]
            dact = jnp.where(lane < HEADS, 1.0, 1.0 - _sigmoid(z)) * (1.0 - th * th)
            dgraw = jnp.where(real & (lane < 2 * HEADS), dgt * dact, 0.0)
            dpm_ref[rows, GATE_COL:GATE_COL + 128] = dgraw.astype(BF16)
            db_acc = db_acc + jnp.sum(dgraw, axis=0, keepdims=True)
        for h, (dcn, dnn) in enumerate(carried):
            dc_scr[h] = dcn
            dn_scr[h] = dnn
            dnw_ref[:, h * DV:(h + 1) * DV] += dnw_acc[h]
        db_ref[...] += db_acc

    rev = lambda i: (nsteps - 1 - i, 0)
    rev4 = lambda i: (nsteps - 1 - i, 0, 0, 0)
    return pl.pallas_call(
        body, name=name, grid=(nsteps,),
        in_specs=[pl.BlockSpec((grp * CHUNK, MLSTM_W), rev), pl.BlockSpec((grp * CHUNK, PM_W), rev),
                  pl.BlockSpec((grp * CHUNK, MLSTM_W), rev),
                  pl.BlockSpec((grp, HEADS, DQK, DV), rev4), pl.BlockSpec((grp, HEADS, 1, DQK), rev4),
                  pl.BlockSpec((grp, HEADS, 1, 1), rev4),
                  pl.BlockSpec((1, 128), lambda i: (0, 0)), pl.BlockSpec((1, MLSTM_W), lambda i: (0, 0))],
        out_specs=[pl.BlockSpec((grp * CHUNK, PM_W), rev), pl.BlockSpec((1, MLSTM_W), lambda i: (0, 0)),
                   pl.BlockSpec((1, 128), lambda i: (0, 0))],
        out_shape=[jax.ShapeDtypeStruct((r, PM_W), BF16), jax.ShapeDtypeStruct((1, MLSTM_W), F32),
                   jax.ShapeDtypeStruct((1, 128), F32)],
        scratch_shapes=[pltpu.VMEM((HEADS, DQK, DV), F32), pltpu.VMEM((HEADS, 1, DQK), F32)],
        compiler_params=_params(),
    )(dcat, pm, ht, cs, ns, ms, bias, nw)


def _mlstm_bwd_head(h, pmv, hh, y, nwv, li, lf, cst, nst, mst, dcn, dnn, tril, eye, triu, lane, rowid,
                    dpm_ref, rows):
    f = yield from _chunk_common(pmv, h, li, lf, cst, nst, mst, tril, eye)
    q, k, v, s, p = f["q"], f["k"], f["v"], f["s"], f["p"]
    w_inter, w_in, w_old, dn = f["w_inter"], f["w_in"], f["w_old"], f["dn"]
    osl = slice(2 * QK_W + MLSTM_W + h * DV, 2 * QK_W + MLSTM_W + (h + 1) * DV)
    sg = _sigmoid(pmv[:, osl])
    yield
    rs = lax.rsqrt(jnp.mean(hh * hh, axis=1, keepdims=True) + EPS)
    yield
    nwh = nwv[:, h * DV:(h + 1) * DV]
    dpm_ref[rows, osl] = (y * (hh * rs * nwh) * sg * (1.0 - sg)).astype(BF16)
    yield
    dhn = y * sg
    dnw_h = jnp.sum(dhn * hh * rs, axis=0, keepdims=True)
    yield
    wd = dhn * nwh
    dhh = rs * wd - hh * (rs * rs * rs) * jnp.mean(hh * wd, axis=1, keepdims=True)
    yield
    dnum = dhh / dn
    dd = -jnp.sum(dhh * hh, axis=1, keepdims=True) / dn
    yield
    dden = jnp.where(jnp.abs(f["den"]) > jnp.exp(-f["mt"]), dd * jnp.sign(f["den"]), 0.0)
    dnum_b = dnum.astype(BF16)
    wdn = (w_inter * dnum).astype(BF16)
    wid = (w_inter * dden).astype(BF16).astype(F32)
    yield
    ds = _dot(dnum_b, v, NT) + dden
    yield
    dsp = (ds * p).astype(BF16)
    yield
    dq = _dot(dsp, k, NN) + _dot(wdn, f["cb"], NT) + wid * f["nb"]
    yield
    dk = _dot(dsp, q, TN)
    yield
    dv = _dot(s.astype(BF16), dnum_b, TN)
    yield
    g = ds * s
    g_col = _row_to_col(jnp.sum(g, axis=0, keepdims=True), eye)
    yield
    db = jnp.sum(g, axis=1, keepdims=True) - g_col
    dli = g_col
    yield
    db = db + (jnp.sum(dnum * f["qc"], axis=1, keepdims=True) + dden * f["qn"]) * w_inter
    yield
    dcnb = dcn.astype(BF16)
    dnnb = dnn.astype(BF16).astype(F32)
    dkw = _dot(v, dcnb, NT) + dnnb
    yield
    dk = dk + w_in * dkw
    dv = dv + _dot(f["kw"], dcnb, NN)
    yield
    ddecay = jnp.sum(dkw * f["kraw"], axis=1, keepdims=True) * w_in
    yield
    dw_old = (jnp.sum(jnp.sum(dcn * cst, axis=1, keepdims=True), axis=0, keepdims=True)
              + jnp.sum(dnn * nst, axis=1, keepdims=True))
    yield
    db_end = dw_old * w_old + jnp.sum(ddecay, axis=0, keepdims=True)
    db = db - ddecay + jnp.where(rowid == CHUNK - 1, db_end, 0.0)
    dli = dli + ddecay
    yield
    dc_new = w_old * dcn + _dot(q, wdn, TN)
    yield
    dn_new = w_old * dnn + jnp.sum(wid * q.astype(F32), axis=0, keepdims=True)
    yield
    dlf = jnp.sum(jnp.where(triu, _col_to_row(db, eye), 0.0), axis=1, keepdims=True)
    yield
    gate_part = jnp.where(lane == h, dli, 0.0) + jnp.where(lane == HEADS + h, dlf, 0.0)
    dpm_ref[rows, h * DQK:(h + 1) * DQK] = (dq * QSCALE).astype(BF16)
    yield
    dpm_ref[rows, QK_W + h * DQK:QK_W + (h + 1) * DQK] = dk.astype(BF16)
    yield
    dpm_ref[rows, 2 * QK_W + h * DV:2 * QK_W + (h + 1) * DV] = dv.astype(BF16)
    return gate_part, dc_new, dn_new, dnw_h


def _my_place():
    return lax.axis_index("x"), lax.axis_index("y"), lax.axis_index("c")


def _flip(v, bit):
    return 1 - v if bit else v


def _exchange_small(name, blk, reduce):
    r, c = blk.shape

    def body(x_ref, o_ref, *rest):
        slots = rest[0] if reduce else o_ref
        send_sems, recv_sems = rest[-2], rest[-1]
        x, y, cc = _my_place()
        me = 4 * x + 2 * y + cc
        slots[me] = x_ref[...]
        copies = []
        for k in range(1, N_DEV):
            peer = (_flip(x, k & 4), _flip(y, k & 2), _flip(cc, k & 1))
            cp = pltpu.make_async_remote_copy(
                src_ref=x_ref, dst_ref=slots.at[me], send_sem=send_sems.at[k - 1],
                recv_sem=recv_sems.at[k - 1], device_id=peer, device_id_type=MESH)
            cp.start()
            copies.append(cp)
        for cp in copies:
            cp.wait()
        if reduce:
            acc = slots[0]
            for d in range(1, N_DEV):
                acc = acc + slots[d]
            o_ref[...] = acc

    scratch = ([pltpu.VMEM((N_DEV, r, c), F32)] if reduce else []) + [
        pltpu.SemaphoreType.DMA((N_DEV - 1,)), pltpu.SemaphoreType.DMA((N_DEV - 1,))]
    return pl.pallas_call(
        body, name=name,
        out_shape=jax.ShapeDtypeStruct((r, c) if reduce else (N_DEV, r, c), F32),
        in_specs=[pl.BlockSpec(memory_space=pltpu.VMEM)], out_specs=pl.BlockSpec(memory_space=pltpu.VMEM),
        scratch_shapes=scratch, compiler_params=_params(),
    )(blk)


HBM_SPEC = pl.BlockSpec(memory_space=pltpu.HBM)
SEM_SPEC = pl.BlockSpec(memory_space=pltpu.SEMAPHORE)
ANY_SPEC = pl.BlockSpec(memory_space=pl.ANY)
DATAFLOW = pltpu.SideEffectType.DATAFLOW_SIDE_EFFECTING


def _split_copy(name, arrays, start=None, wait=None, after=None):
    results, token = _split_copies(name, [(arrays, start, wait)], after)
    return results[0][0], results[0][1], token


def _split_copies(name, jobs, after=None):
    operands, in_specs, out_shape, out_specs, aliases = [], [], [], [], {}
    in_at, out_at = [], []
    for arrays, start, wait in jobs:
        in_at.append(len(operands))
        operands += [pltpu.with_memory_space_constraint(a, pltpu.HBM) for a in arrays]
        in_specs += [HBM_SPEC] * len(arrays)
        if wait:
            operands += list(wait[1])
            in_specs += [SEM_SPEC, SEM_SPEC]
    if after is not None:
        operands.append(after)
        in_specs.append(ANY_SPEC)
    for j, (arrays, start, wait) in enumerate(jobs):
        out_at.append(len(out_shape))
        if start:
            out_shape += [pltpu.SemaphoreType.DMA((start[1],)), pltpu.SemaphoreType.DMA((start[1],))]
            out_specs += [SEM_SPEC, SEM_SPEC]
        for i, a in enumerate(arrays):
            aliases[in_at[j] + i] = len(out_shape)
            out_shape.append(pltpu.HBM(a.shape, a.dtype))
            out_specs.append(HBM_SPEC)
    any_start = any(start for _, start, _ in jobs)
    if any_start:
        out_shape.append(jax.ShapeDtypeStruct((8, 128), F32))
        out_specs.append(pl.BlockSpec(memory_space=pltpu.VMEM))
    n_in = len(operands)

    def body(*refs):
        for j, (arrays, start, wait) in enumerate(jobs):
            if wait:
                ins = refs[in_at[j]:in_at[j] + len(arrays)]
                at = in_at[j] + len(arrays)
                for cp in wait[0](ins, refs[at], refs[at + 1]):
                    cp.wait_send()
                    cp.wait_recv()
        for j, (arrays, start, wait) in enumerate(jobs):
            if start:
                ins = refs[in_at[j]:in_at[j] + len(arrays)]
                at = n_in + out_at[j]
                for cp in start[0](ins, refs[at], refs[at + 1]):
                    cp.start()
        if any_start:
            token = refs[n_in + len(out_shape) - 1]
            token[...] = jnp.zeros_like(token)

    outs = pl.pallas_call(
        body, name=name, in_specs=in_specs, out_specs=out_specs, out_shape=out_shape,
        input_output_aliases=aliases, compiler_params=pltpu.CompilerParams(has_side_effects=DATAFLOW),
    )(*operands)
    results = []
    for j, (arrays, start, wait) in enumerate(jobs):
        at = out_at[j]
        sems = (outs[at], outs[at + 1]) if start else None
        at += 2 if start else 0
        results.append((list(outs[at:at + len(arrays)]), sems))
    return results, (outs[-1] if any_start else None)


def _remote(src, dst, send_sems, recv_sems, k, to):
    return pltpu.make_async_remote_copy(src_ref=src, dst_ref=dst, send_sem=send_sems.at[k],
                                        recv_sem=recv_sems.at[k], device_id=to, device_id_type=MESH)


def _slot(px, py, pc):
    return 4 * px + 2 * py + pc


def _gather_first(refs, send_sems, recv_sems):
    x, y, c = _my_place()
    blk = refs[0].at[_slot(x, y, c)]
    targets = [(x, y, 1 - c), (1 - x, y, c), (x, 1 - y, c)]
    return [_remote(blk, blk, send_sems, recv_sems, k, to) for k, to in enumerate(targets)]


def _gather_relay(refs, send_sems, recv_sems):
    x, y, c = _my_place()
    rows = refs[0].shape[1]
    half = rows // 32 * 16
    from_x, from_y = _slot(1 - x, y, c), _slot(x, 1 - y, c)
    upper = refs[0].at[from_x, pl.ds(0, half)]
    lower = refs[0].at[from_y, pl.ds(half, rows - half)]
    return [_remote(upper, upper, send_sems, recv_sems, 0, (x, 1 - y, c)),
            _remote(lower, lower, send_sems, recv_sems, 1, (1 - x, y, c)),
            _remote(refs[0].at[from_x], refs[0].at[from_x], send_sems, recv_sems, 2, (x, y, 1 - c)),
            _remote(refs[0].at[from_y], refs[0].at[from_y], send_sems, recv_sems, 3, (x, y, 1 - c))]


def _gather_last(refs, send_sems, recv_sems):
    x, y, c = _my_place()
    blk = refs[0].at[_slot(1 - x, 1 - y, c)]
    return [_remote(blk, blk, send_sems, recv_sems, 0, (x, y, 1 - c))]


def _scatter_sibling(n):
    def copies(refs, send_sems, recv_sems):
        x, y, c = _my_place()
        return [_remote(refs[a].at[2 * j + 1 - c], refs[n + a].at[j], send_sems, recv_sems, 4 * a + j, (x, y, 1 - c))
                for a in range(n) for j in range(4)]
    return copies


def _scatter_chips(n):
    def copies(refs, send_sems, recv_sems):
        x, y, c = _my_place()
        out = []
        for a in range(n):
            for k in range(1, 4):
                px, py = _flip(x, k & 2), _flip(y, k & 1)
                out.append(_remote(refs[a].at[2 * px + py], refs[n + a].at[2 * x + y], send_sems, recv_sems,
                                   3 * a + k - 1, (px, py, c)))
        return out
    return copies


def _pair_sum(name, core, g, t):
    _, r, c = g.shape
    tr = _tile(r, 512, 8)
    g4 = g.reshape(4, 2, r, c)

    def body(core_ref, g_ref, t_ref, o_ref):
        o_ref[...] = (g_ref[...].astype(F32) + t_ref[...].astype(F32)).astype(BF16)

    return pl.pallas_call(
        body, name=name,
        grid_spec=pltpu.PrefetchScalarGridSpec(
            num_scalar_prefetch=1, grid=(4, r // tr),
            in_specs=[pl.BlockSpec((None, None, tr, c), lambda j, i, core_ref: (j, core_ref[0], i, 0)),
                      pl.BlockSpec((None, tr, c), lambda j, i, core_ref: (j, i, 0))],
            out_specs=pl.BlockSpec((None, tr, c), lambda j, i, core_ref: (j, i, 0))),
        out_shape=jax.ShapeDtypeStruct((4, r, c), BF16), compiler_params=_params(),
    )(core, g4, t)


BF16_TILE_ROWS = 16
IN_STRIDE = IN_SH // BF16_TILE_ROWS * BF16_TILE_ROWS
IN_WIN = IN_STRIDE + BF16_TILE_ROWS
IN_DENSE = IN_STRIDE * (N_DEV - 1) + IN_WIN
assert IN_SH * (N_DEV - 1) - IN_STRIDE * (N_DEV - 1) + IN_SH <= IN_WIN and D_IN <= IN_DENSE


def _window(ref, d):
    return ref.at[pl.ds(pl.multiple_of(IN_STRIDE * d, BF16_TILE_ROWS), IN_WIN)]


def _scatter_sibling_windows(refs, send_sems, recv_sems):
    x, y, c = _my_place()
    return [_remote(_window(refs[0], 2 * j + 1 - c), refs[1].at[j], send_sems, recv_sems, j, (x, y, 1 - c))
            for j in range(4)]


def _pair_sum_windows(name, core, g, t):
    c = g.shape[1]

    def body(core_ref, g_ref, t_ref, o_ref, buf, sem):
        copy = pltpu.make_async_copy(_window(g_ref, 2 * pl.program_id(0) + core_ref[0]), buf, sem)
        copy.start()
        copy.wait()
        o_ref[...] = (buf[...].astype(F32) + t_ref[...].astype(F32)).astype(BF16)

    slot = pl.BlockSpec((None, IN_WIN, c), lambda j, core_ref: (j, 0, 0))
    return pl.pallas_call(
        body, name=name,
        grid_spec=pltpu.PrefetchScalarGridSpec(
            num_scalar_prefetch=1, grid=(4,), in_specs=[ANY_SPEC, slot], out_specs=slot,
            scratch_shapes=[pltpu.VMEM((IN_WIN, c), BF16), pltpu.SemaphoreType.DMA]),
        out_shape=jax.ShapeDtypeStruct((4, IN_WIN, c), BF16), compiler_params=_params(),
    )(core, g, t)


def _adam_math(w, g, m, v):
    m2 = ADAM_B1 * m + (1.0 - ADAM_B1) * g
    v2 = ADAM_B2 * v + (1.0 - ADAM_B2) * (g * g)
    m_hat = m2 / (1.0 - ADAM_B1 ** ADAM_STEP)
    v_hat = v2 / (1.0 - ADAM_B2 ** ADAM_STEP)
    delta = -ADAM_LR * (m_hat / (jnp.sqrt(v_hat) + ADAM_EPS) + ADAM_WD * w)
    return delta, m2, v2


def _adam_sharded(name, chip, w, m, v, grads, row_off=0):
    _, r, c = w.shape
    windows = chip.shape[0] == 2
    tr = r if windows else _tile(r, 256, 8)
    tc = c if tr < r else _tile(c, 256, 128)
    boff = row_off // tr

    def body(chip_ref, w_ref, m_ref, v_ref, p0_ref, q0_ref, p1_ref, q1_ref, g_ref, d_ref, nm_ref, nv_ref):
        mine = chip_ref[0]

        def total(p_ref, q_ref):
            acc = None
            for j in range(4):
                part = jnp.where(mine == j, p_ref[...], q_ref[j]).astype(F32)
                acc = part if acc is None else acc + part
            return acc

        g = jnp.where(pl.program_id(0) == 0, total(p0_ref, q0_ref), total(p1_ref, q1_ref))
        if windows:
            first = chip_ref[1]
            g = pltpu.roll(g, jnp.where(first == 0, 0, r - first), 0)
        delta, m2, v2 = _adam_math(w_ref[...], g, m_ref[...], v_ref[...])
        g_ref[...] = g
        d_ref[...] = delta
        nm_ref[...] = m2
        nv_ref[...] = v2

    def grad_specs(layer):
        at = lambda l, i, j: (jnp.where(l == layer, boff + i, boff), jnp.where(l == layer, j, 0))
        return [pl.BlockSpec((None, tr, tc), lambda l, i, j, chip_ref: (chip_ref[0],) + at(l, i, j)),
                pl.BlockSpec((4, tr, tc), lambda l, i, j, chip_ref: (0,) + at(l, i, j))]

    wspec = pl.BlockSpec((None, tr, tc), lambda l, i, j, chip_ref: (l, i, j))
    sds = jax.ShapeDtypeStruct(w.shape, F32)
    return pl.pallas_call(
        body, name=name,
        grid_spec=pltpu.PrefetchScalarGridSpec(
            num_scalar_prefetch=1, grid=(2, r // tr, c // tc),
            in_specs=[wspec, wspec, wspec] + grad_specs(0) + grad_specs(1), out_specs=[wspec] * 4),
        out_shape=[sds] * 4, compiler_params=_params(),
    )(chip, w, m, v, grads[0][0], grads[0][1], grads[1][0], grads[1][1])


def _adam_small(name, w, m, v, g):
    def body(w_ref, m_ref, v_ref, g_ref, d_ref, nm_ref, nv_ref):
        delta, m2, v2 = _adam_math(w_ref[...], g_ref[...], m_ref[...], v_ref[...])
        d_ref[...] = delta
        nm_ref[...] = m2
        nv_ref[...] = v2

    sds = jax.ShapeDtypeStruct(w.shape, F32)
    vm = pl.BlockSpec(memory_space=pltpu.VMEM)
    return pl.pallas_call(body, name=name, in_specs=[vm] * 4, out_specs=[vm] * 3, out_shape=[sds] * 3,
                          compiler_params=_params())(w, m, v, g)


GATE_END = GATE_COL + 2 * HEADS


def _merge_dw_in(dwm_t, dwc_t):
    full = jnp.concatenate([dwm_t[:GATE_END], dwc_t.reshape(3 * CONV_W, D_MODEL)], axis=0)
    return full.reshape(N_DEV, IN_SH, D_MODEL)


def _pack128(parts):
    flat = jnp.concatenate([p.reshape(-1) for p in parts])
    n = flat.shape[0]
    rows = -(-n // 1024) * 8
    return jnp.pad(flat, (0, rows * 128 - n)).reshape(rows, 128)


def _unpack128(packed, shapes):
    flat = packed.reshape(-1)
    out, at = [], 0
    for s in shapes:
        n = int(np.prod(s))
        out.append(flat[at:at + n].reshape(s))
        at += n
    return out


def kernel(x, meta_tokens, norm_mix_w, w_in, b_gates, conv_w, mlstm_norm_w, w_out, norm_ffn_w, w_gate, w_up, w_down, norm_final_w, loss_target, m_meta_tokens, m_norm_mix_w, m_w_in, m_b_gates, m_conv_w, m_mlstm_norm_w, m_w_out, m_norm_ffn_w, m_w_gate, m_w_up, m_w_down, m_norm_final_w, v_meta_tokens, v_norm_mix_w, v_w_in, v_b_gates, v_conv_w, v_mlstm_norm_w, v_w_out, v_norm_ffn_w, v_w_gate, v_w_up, v_w_down, v_norm_final_w):
    seq = x.shape[1]
    rows = TOK0 + seq
    me = 4 * lax.axis_index("x") + 2 * lax.axis_index("y") + lax.axis_index("c")
    meta_sh = meta_tokens.shape[1]
    conv_sh = conv_w.shape[2]

    w_gate_t, m_w_gate_t, v_w_gate_t = (jnp.transpose(a, (0, 2, 1)) for a in (w_gate, m_w_gate, v_w_gate))
    w_up_t, m_w_up_t, v_w_up_t = (jnp.transpose(a, (0, 2, 1)) for a in (w_up, m_w_up, v_w_up))
    shards = []
    for l in range(DEPTH):
        shards += [jnp.transpose(w_in[l]).astype(BF16), w_out[l].astype(BF16), w_gate_t[l].astype(BF16),
                   w_up_t[l].astype(BF16), w_down[l].astype(BF16)]
    per_layer = ("w_in", "w_out", "w_gate", "w_up", "w_down")
    gather_names = [f"{nm}_{l}" for l in range(DEPTH) for nm in per_layer]
    gather_state = {}

    def gather_step(tag, after, start=None, relay=None, last=None, done=()):
        jobs, idx = [], []
        if relay is not None and relay < len(shards):
            jobs.append((gather_state[relay][0], (_gather_relay, 4), (_gather_first, gather_state[relay][1])))
            idx.append(relay)
        if start is not None and start < len(shards):
            buf = lax.dynamic_update_index_in_dim(lax.empty((N_DEV,) + shards[start].shape, BF16), shards[start], me, 0)
            jobs.append(([buf], (_gather_first, 3), None))
            idx.append(start)
        if last is not None:
            jobs.append((gather_state[last][0], (_gather_last, 1), (_gather_relay, gather_state[last][1])))
            idx.append(last)
        for i in done:
            jobs.append((gather_state[i][0], None, (_gather_last, gather_state[i][1])))
            idx.append(i)
        if not jobs:
            return after, []
        results, tok = _split_copies(f"gather_{tag}", jobs, after)
        for i, res in zip(idx, results):
            gather_state[i] = res
        return (after if tok is None else tok), [gather_state[i][0][0] for i in done]

    bias = [jnp.pad(b_gates[l].reshape(1, 2 * HEADS), ((0, 0), (0, 128 - 2 * HEADS))) for l in range(DEPTH)]
    nmix = [norm_mix_w[l].reshape(1, D_MODEL) for l in range(DEPTH)]
    nffn = [norm_ffn_w[l].reshape(1, D_MODEL) for l in range(DEPTH)]
    nmls = [mlstm_norm_w[l].reshape(1, MLSTM_W) for l in range(DEPTH)]
    weights = [dict() for _ in range(DEPTH)]
    saved = [dict() for _ in range(DEPTH)]

    def layer_fwd(l, h, after):
        w, s = weights[l], saved[l]
        k0 = len(per_layer) * l
        tok, _ = gather_step(f"l{l}_a", after, last=k0)
        _, (g_in,) = gather_step(f"l{l}_b", tok, done=[k0])
        tok, _ = gather_step(f"l{l}_c", g_in, relay=k0 + 1, start=k0 + 3)
        w["win_t"] = g_in.reshape(D_IN, D_MODEL)
        w["wc_t"] = w["win_t"][GATE_END:].reshape(3, CONV_W, D_MODEL)
        s["h0"] = h
        s["hn"] = _rms_fwd(f"norm_mix_{l}", h, nmix[l] + tok[0, 0])
        s["pm"] = _mm_nt(f"proj_mlstm_{l}", s["hn"], w["win_t"], F32, tn=640, tk=D_MODEL, n=PM_W)
        tok, _ = gather_step(f"l{l}_d", s["pm"], relay=k0 + 2, start=k0 + 4)
        tok, _ = gather_step(f"l{l}_d2", tok, last=k0 + 1)
        s["pc"] = _mm_nt_bcols(f"proj_conv_{l}", s["hn"], w["wc_t"], F32, dep=tok)
        hm, s["ht"], s["cs"], s["ns"], s["ms"] = _mlstm_fwd(f"mlstm_fwd_{l}", s["pm"], bias[l] + tok[:1], nmls[l])
        tok, _ = gather_step(f"l{l}_e", hm, relay=k0 + 3, start=k0 + 5)
        tok, _ = gather_step(f"l{l}_e2", tok, last=k0 + 2)
        s["cat"] = _conv_fwd(f"conv_fwd_{l}", s["pc"], conv_rows[l] + tok[0, 0], hm)
        _, (g_out,) = gather_step(f"l{l}_f", s["cat"], done=[k0 + 1])
        w["wo"] = g_out.reshape(D_MODEL, D_MODEL)
        s["h1"], s["hf"] = _proj_res_norm(f"out_proj_{l}", s["cat"], w["wo"], s["h0"], nffn[l])
        tok_g, _ = gather_step(f"l{l}_g", s["h1"], relay=k0 + 4, start=k0 + 6)
        tok, _ = gather_step(f"l{l}_h", tok_g, last=k0 + 3)
        _, (g_gate, g_up) = gather_step(f"l{l}_i", tok, done=[k0 + 2, k0 + 3])
        w["wg_t"] = g_gate.reshape(D_FF, D_MODEL)
        w["wu_t"] = g_up.reshape(D_FF, D_MODEL)
        s["g"], s["u"], s["act"] = _ffn_in(f"ffn_in_{l}", s["hf"], w["wg_t"], w["wu_t"], dep=tok_g)
        tok, _ = gather_step(f"l{l}_j", s["act"], last=k0 + 4)
        _, (g_down,) = gather_step(f"l{l}_k", tok, done=[k0 + 4])
        w["wd"] = g_down.reshape(D_FF, D_MODEL)
        tok, _ = gather_step(f"l{l}_k2", tok, relay=k0 + 5, start=k0 + 7)
        return _mm_nn(f"ffn_out_{l}", s["act"], w["wd"], F32, res=s["h1"], dep=tok)

    tok, _ = gather_step("first", None, start=0)
    zero = tok[0, 0]
    small = jnp.concatenate(
        [meta_tokens + zero, jnp.pad(conv_w.reshape(DEPTH * 3, conv_sh), ((0, 2), (0, meta_sh - conv_sh)))], axis=0)
    slots = _exchange_small("gather_small", small, reduce=False)
    meta_full = jnp.transpose(slots[:, :N_META, :], (1, 0, 2)).reshape(N_META, D_MODEL)
    conv_full = jnp.transpose(slots[:, N_META:N_META + DEPTH * 3, :conv_sh], (1, 0, 2)).reshape(DEPTH, 3, CONV_W)
    conv_rows = [jnp.pad(conv_full[l], ((0, 5), (0, 0))) for l in range(DEPTH)]
    w_in_t, m_w_in_t, v_w_in_t = (jnp.transpose(a + zero, (0, 2, 1)) for a in (w_in, m_w_in, v_w_in))
    tok, w_in_t, m_w_in_t, v_w_in_t, meta_full = lax.optimization_barrier(
        (tok, w_in_t, m_w_in_t, v_w_in_t, meta_full))
    tok, _ = gather_step("pre_a", tok, relay=0)
    tok, _ = gather_step("pre_b", tok, start=1)
    tok, _ = gather_step("pre_c", tok, start=2)
    h = jnp.concatenate([jnp.zeros((PAD_FRONT, D_MODEL), F32), meta_full, x[0]], axis=0)
    h = layer_fwd(0, h, tok)
    h = layer_fwd(1, h, h)

    dh, dh_b, d_final, loss_part = _final_loss("final_loss", h, norm_final_w.reshape(1, D_MODEL), loss_target[0])

    core = lax.axis_index("c").astype(jnp.int32).reshape(1)
    chip = (2 * lax.axis_index("x") + lax.axis_index("y")).astype(jnp.int32).reshape(1)
    scatter_state = {}

    def scatter_begin(nm, grad):
        dense = grad.ndim == 2
        land = lax.empty((4, IN_WIN, grad.shape[1]) if dense else (4,) + grad.shape[1:], BF16)
        copies = _scatter_sibling_windows if dense else _scatter_sibling(1)
        arrs, sems, tok = _split_copy(f"grad_sibling_start_{nm}", [grad, land], start=(copies, 4))
        scatter_state[nm] = (arrs, sems, copies)
        return tok

    def scatter_advance(nm, after):
        arrs, sems, copies = scatter_state[nm]
        arrs, _, _ = _split_copy(f"grad_sibling_done_{nm}", arrs, wait=(copies, sems), after=after)
        pair_sum = _pair_sum_windows if arrs[0].ndim == 2 else _pair_sum
        part = pair_sum(f"grad_pair_sum_{nm}", core, arrs[0], arrs[1])
        arrs, sems, tok = _split_copy(f"grad_chips_start_{nm}", [part, lax.empty(part.shape, BF16)],
                                      start=(_scatter_chips(1), 3))
        scatter_state[nm] = (arrs, sems)
        return tok

    def scattered(nm, after):
        arrs, sems = scatter_state[nm]
        arrs, _, _ = _split_copy(f"grad_chips_done_{nm}", arrs, wait=(_scatter_chips(1), sems), after=after)
        return arrs[0], arrs[1]

    d_mix, d_ffn, d_mls, d_bias, d_conv = ([None] * DEPTH for _ in range(5))

    def layer_bwd(l, dh, dh_b, tok):
        w, s = weights[l], saved[l]
        dg, du = _ffn_act_bwd(f"d_act_{l}", dh_b, w["wd"], s["g"], s["u"], dep=tok)
        dw_down = _mm_tn(f"dw_down_{l}", s["act"], dh_b, BF16, tm=1408, tn=1024)
        tok = scatter_begin(f"w_down_{l}", dw_down.reshape(N_DEV, FF_SH, D_MODEL))
        dhf = _mm_nn(f"d_ffn_gate_{l}", dg, w["wg_t"], F32, dep=tok)
        tok = scatter_advance(f"w_down_{l}", after=dhf)
        dhf = _mm_nn(f"d_ffn_up_{l}", du, w["wu_t"], F32, res=dhf, dep=tok)
        dw_gate = _mm_tn(f"dw_gate_{l}", dg, s["hf"], BF16, tm=1408, tn=1024)
        tok = scatter_begin(f"w_gate_{l}", dw_gate.reshape(N_DEV, FF_SH, D_MODEL))
        dw_up = _mm_tn(f"dw_up_{l}", du, s["hf"], BF16, tm=1408, tn=1024, dep=tok)
        tok = scatter_begin(f"w_up_{l}", dw_up.reshape(N_DEV, FF_SH, D_MODEL))
        dh1, dh1_b, d_ffn[l] = _rms_bwd(f"norm_ffn_bwd_{l}", s["h1"], nffn[l] + tok[0, 0], dhf, dh)
        tok = scatter_advance(f"w_gate_{l}", after=dh1)
        dcat = _mm_nt(f"d_cat_{l}", dh1_b, w["wo"], F32, tk=D_MODEL, dep=tok)
        tok = scatter_advance(f"w_up_{l}", after=dcat)
        dw_out = _mm_tn(f"dw_out_{l}", s["cat"], dh1_b, BF16, tn=1024, dep=tok)
        tok = scatter_begin(f"w_out_{l}", dw_out.reshape(N_DEV, OUT_SH, D_MODEL))
        dpm, d_mls[l], d_bias[l] = _mlstm_bwd(f"mlstm_bwd_{l}", dcat, s["pm"], s["ht"], s["cs"], s["ns"],
                                               s["ms"], bias[l] + tok[:1], nmls[l])
        dpc, d_conv[l] = _conv_bwd(f"conv_bwd_{l}", dcat, s["pc"], conv_rows[l])
        tok = scatter_advance(f"w_out_{l}", after=dpc)
        dwm_t = _mm_tn(f"dw_mlstm_{l}", dpm, s["hn"], BF16, tm=640, tn=1024, dep=tok)
        dwc_t = _mm_tn_acols(f"dw_conv_{l}", dpc, s["hn"], BF16)
        tok = scatter_begin(f"w_in_{l}", _merge_dw_in(dwm_t, dwc_t))
        dhn = _mm_nn_kt(f"d_norm_mlstm_{l}", dpm, w["win_t"], F32, tk=PM_W, dep=tok)
        dhn = _mm_nn_ksum(f"d_norm_conv_{l}", dpc, w["wc_t"], F32, res=dhn)
        tok = scatter_advance(f"w_in_{l}", after=dhn)
        dh, dh_b, d_mix[l] = _rms_bwd(f"norm_mix_bwd_{l}", s["h0"], nmix[l] + tok[0, 0], dhn, dh1)
        return dh, dh_b, tok

    dh, dh_b, tok = layer_bwd(1, dh, dh_b, None)
    dh, dh_b, tok_tail = layer_bwd(0, dh, dh_b, tok)

    pq = {}
    after = dh
    for l in reversed(range(DEPTH)):
        for nm in ("w_down", "w_gate", "w_up", "w_out", "w_in"):
            if (nm, l) != ("w_in", 0):
                pq[nm, l] = scattered(f"{nm}_{l}", after)
                after = pq[nm, l][0]
    untransposed = lambda outs: [jnp.transpose(o, (0, 2, 1)) for o in outs]
    g_out, d_out, nm_out, nv_out = _adam_sharded(
        "adam_w_out", chip, w_out, m_w_out, v_w_out, [pq["w_out", 0], pq["w_out", 1]])
    g_gate, d_gate, nm_gate, nv_gate = untransposed(_adam_sharded(
        "adam_w_gate", chip, w_gate_t, m_w_gate_t, v_w_gate_t, [pq["w_gate", 0], pq["w_gate", 1]]))
    g_up, d_up, nm_up, nv_up = untransposed(_adam_sharded(
        "adam_w_up", chip, w_up_t, m_w_up_t, v_w_up_t, [pq["w_up", 0], pq["w_up", 1]]))
    g_down, d_down, nm_down, nv_down = _adam_sharded(
        "adam_w_down", chip, w_down, m_w_down, v_w_down, [pq["w_down", 0], pq["w_down", 1]])
    pq["w_in", 0] = scattered("w_in_0", nv_down)
    g_in, d_in, nm_in, nv_in = untransposed(_adam_sharded(
        "adam_w_in", chip, w_in_t, m_w_in_t, v_w_in_t, [pq["w_in", 0], pq["w_in", 1]]))

    bg = jnp.concatenate([d_bias[l][0, :2 * HEADS] for l in range(DEPTH)])
    red_in = jnp.concatenate([
        dh[PAD_FRONT:TOK0], d_mix[0], d_mix[1], d_ffn[0], d_ffn[1], d_final,
        jnp.concatenate([d_mls[0], d_mls[1]], axis=1),
        jnp.stack([d_conv[l][:3] for l in range(DEPTH)]).reshape(3, 2 * CONV_W),
        jnp.pad(bg, (0, D_MODEL - bg.shape[0])).reshape(1, D_MODEL),
        jnp.pad(loss_part[:, :1], ((0, 0), (0, D_MODEL - 1))),
        jnp.zeros((5, D_MODEL), F32) + tok_tail[0, 0]], axis=0)
    red = _exchange_small("reduce_small", red_in, reduce=True)
    loss = red[26, 0]
    g_meta = lax.dynamic_slice_in_dim(red[:N_META], me * meta_sh, meta_sh, axis=1)
    g_mix, g_ffn, g_final = red[16:18], red[18:20], red[20]
    g_mls = red[21].reshape(DEPTH, MLSTM_W)
    g_conv = lax.dynamic_slice_in_dim(red[22:25].reshape(DEPTH, 3, CONV_W), me * conv_sh, conv_sh, axis=2)
    g_bias = red[25, :DEPTH * 2 * HEADS].reshape(DEPTH, 2 * HEADS)

    small_w = [meta_tokens, norm_mix_w, b_gates, conv_w, mlstm_norm_w, norm_ffn_w, norm_final_w]
    small_m = [m_meta_tokens, m_norm_mix_w, m_b_gates, m_conv_w, m_mlstm_norm_w, m_norm_ffn_w, m_norm_final_w]
    small_v = [v_meta_tokens, v_norm_mix_w, v_b_gates, v_conv_w, v_mlstm_norm_w, v_norm_ffn_w, v_norm_final_w]
    small_g = [g_meta, g_mix, g_bias, g_conv, g_mls, g_ffn, g_final]
    shapes = [a.shape for a in small_w]
    packed = _adam_small("adam_small", _pack128(small_w), _pack128(small_m), _pack128(small_v), _pack128(small_g))
    (d_meta, d_nmix, d_bg, d_cw, d_nmls, d_nffn, d_nfin), (nm_meta, nm_nmix, nm_bg, nm_cw, nm_nmls, nm_nffn, nm_nfin), \
        (nv_meta, nv_nmix, nv_bg, nv_cw, nv_nmls, nv_nffn, nv_nfin) = (_unpack128(p, shapes) for p in packed)

    grad_x = dh[TOK0:].reshape(1, seq, D_MODEL)
    return (loss, grad_x,
            g_meta, g_mix, g_in, g_bias, g_conv, g_mls, g_out, g_ffn, g_gate, g_up, g_down, g_final,
            d_meta, d_nmix, d_in, d_bg, d_cw, d_nmls, d_out, d_nffn, d_gate, d_up, d_down, d_nfin,
            nm_meta, nm_nmix, nm_in, nm_bg, nm_cw, nm_nmls, nm_out, nm_nffn, nm_gate, nm_up, nm_down, nm_nfin,
            nv_meta, nv_nmix, nv_in, nv_bg, nv_cw, nv_nmls, nv_out, nv_nffn, nv_gate, nv_up, nv_down, nv_nfin)
```

```python
import numpy as np
import jax
import jax.numpy as jnp
from jax import lax
from jax.experimental import pallas as pl
from jax.experimental.pallas import tpu as pltpu

F32 = jnp.float32
BF16 = jnp.bfloat16
MESH = pl.DeviceIdType.MESH

D_MODEL = 2048
DEPTH = 2
N_META = 16
MLSTM_W = 1024
CONV_W = 1024
HEADS = 4
DV = 256
DQK = 128
QK_W = 512
CHUNK = 64
PAD_FRONT = 48
TOK0 = PAD_FRONT + N_META
D_FF = 5632
N_DEV = 8
FF_SH = D_FF // N_DEV
D_IN = 6152
IN_SH = D_IN // N_DEV
OUT_SH = D_MODEL // N_DEV
GATE_COL = 3072
PM_W = GATE_COL + 128
GATE_CAP = 15.0
EPS = 1e-6
QSCALE = DQK ** -0.5

ADAM_LR = 0.001
ADAM_B1 = 0.9
ADAM_B2 = 0.999
ADAM_EPS = 1e-08
ADAM_WD = 0.01
ADAM_STEP = 10

V7X_VMEM_LIMIT = 50 * 1024 * 1024
V7X_MXU_COLS = 256


def _params(**kw):
    return pltpu.CompilerParams(vmem_limit_bytes=V7X_VMEM_LIMIT, **kw)


def _tile(n, target, mult):
    best = None
    for t in range(mult, min(n, target) + 1, mult):
        if n % t == 0:
            best = t
    return best if best is not None else n


def _sigmoid(x):
    return 1.0 / (1.0 + jnp.exp(-x))


NN = ((1,), (0,))
NT = ((1,), (1,))
TN = ((0,), (0,))


def _matmul(name, a, b, out_shape, out_dtype, grid, a_bs, b_bs, o_bs, dims, nk, acc_shape=None,
            res=None, res_bs=None, dep=None):
    has_res = res is not None
    n_in = 2 + has_res + (dep is not None)

    def body(*refs):
        a_ref, b_ref = refs[0], refs[1]
        r_ref = refs[2] if has_res else None
        o_ref = refs[n_in]
        x = lax.dot_general(a_ref[...], b_ref[...], (dims, ((), ())), preferred_element_type=F32)
        if nk == 1:
            if has_res:
                x = x + r_ref[...]
            o_ref[...] = x.astype(o_ref.dtype)
            return
        acc = refs[n_in + 1]
        k = pl.program_id(len(grid) - 1)

        @pl.when(k == 0)
        def _():
            acc[...] = (x + r_ref[...]) if has_res else x

        @pl.when(k > 0)
        def _():
            acc[...] += x

        @pl.when(k == nk - 1)
        def _():
            o_ref[...] = acc[...].astype(o_ref.dtype)

    ins = [a, b] + ([res] if has_res else [])
    specs = [a_bs, b_bs] + ([res_bs] if has_res else [])
    if dep is not None:
        ins.append(dep)
        specs.append(pl.BlockSpec((8, 128), lambda *_: (0, 0)))
    scratch = [pltpu.VMEM(acc_shape, F32)] if nk > 1 else []
    return pl.pallas_call(
        body, name=name, grid=grid, in_specs=specs, out_specs=o_bs,
        out_shape=jax.ShapeDtypeStruct(out_shape, out_dtype), scratch_shapes=scratch,
        compiler_params=_params(),
    )(*ins)


def _mm_nn(name, a, b, out_dtype, res=None, tm=1056, tn=512, dep=None):
    r, k = a.shape
    n = b.shape[1]
    tm, tn = _tile(r, tm, 8), _tile(n, tn, 128)
    return _matmul(name, a, b, (r, n), out_dtype, (r // tm, n // tn, 1),
                   pl.BlockSpec((tm, k), lambda i, j, s: (i, 0)),
                   pl.BlockSpec((k, tn), lambda i, j, s: (0, j)),
                   pl.BlockSpec((tm, tn), lambda i, j, s: (i, j)), NN, 1,
                   res=res, res_bs=pl.BlockSpec((tm, tn), lambda i, j, s: (i, j)), dep=dep)


def _mm_nn_kt(name, a, b, out_dtype, tm=1056, tn=1024, tk=640, dep=None):
    r, k = a.shape
    n = b.shape[1]
    tm, tn, tk = _tile(r, tm, 8), _tile(n, tn, 128), _tile(k, tk, 128)
    nk = k // tk
    return _matmul(name, a, b, (r, n), out_dtype, (r // tm, n // tn, nk),
                   pl.BlockSpec((tm, tk), lambda i, j, s: (i, s)),
                   pl.BlockSpec((tk, tn), lambda i, j, s: (s, j)),
                   pl.BlockSpec((tm, tn), lambda i, j, s: (i, j)), NN, nk, acc_shape=(tm, tn), dep=dep)


def _mm_nn_ksum(name, a3, b3, out_dtype, res=None, tm=1056, tn=1024, dep=None):
    e, r, kb = a3.shape
    n = b3.shape[2]
    tm, tn = _tile(r, tm, 8), _tile(n, tn, 128)
    return _matmul(name, a3, b3, (r, n), out_dtype, (r // tm, n // tn, e),
                   pl.BlockSpec((None, tm, kb), lambda i, j, s: (s, i, 0)),
                   pl.BlockSpec((None, kb, tn), lambda i, j, s: (s, 0, j)),
                   pl.BlockSpec((tm, tn), lambda i, j, s: (i, j)), NN, e, acc_shape=(tm, tn),
                   res=res, res_bs=pl.BlockSpec((tm, tn), lambda i, j, s: (i, j)), dep=dep)


def _mm_nt(name, a, b, out_dtype, res=None, tm=1056, tn=512, tk=640, n=None, dep=None):
    r, k = a.shape
    n = b.shape[0] if n is None else n
    tm, tn, tk = _tile(r, tm, 8), _tile(n, tn, 128), _tile(k, tk, 128)
    nk = k // tk
    return _matmul(name, a, b, (r, n), out_dtype, (r // tm, n // tn, nk),
                   pl.BlockSpec((tm, tk), lambda i, j, s: (i, s)),
                   pl.BlockSpec((tn, tk), lambda i, j, s: (j, s)),
                   pl.BlockSpec((tm, tn), lambda i, j, s: (i, j)), NT, nk, acc_shape=(tm, tn),
                   res=res, res_bs=pl.BlockSpec((tm, tn), lambda i, j, s: (i, j)), dep=dep)


def _mm_nt_bcols(name, a, b3, out_dtype, tm=1056, dep=None):
    r, k = a.shape
    e, n, _ = b3.shape
    tm = _tile(r, tm, 8)
    return _matmul(name, a, b3, (e, r, n), out_dtype, (r // tm, e, 1),
                   pl.BlockSpec((tm, k), lambda i, g, s: (i, 0)),
                   pl.BlockSpec((None, n, k), lambda i, g, s: (g, 0, 0)),
                   pl.BlockSpec((None, tm, n), lambda i, g, s: (g, i, 0)), NT, 1, dep=dep)


def _mm_tn(name, a, b, out_dtype, tm=1024, tn=640, dep=None):
    r, m = a.shape
    n = b.shape[1]
    tm, tn = _tile(m, tm, 128), _tile(n, tn, 128)
    return _matmul(name, a, b, (m, n), out_dtype, (m // tm, n // tn, 1),
                   pl.BlockSpec((r, tm), lambda i, j, s: (0, i)),
                   pl.BlockSpec((r, tn), lambda i, j, s: (0, j)),
                   pl.BlockSpec((tm, tn), lambda i, j, s: (i, j)), TN, 1, dep=dep)


def _mm_tn_acols(name, a3, b, out_dtype, tn=1024, dep=None):
    e, r, m = a3.shape
    n = b.shape[1]
    tn = _tile(n, tn, 128)
    return _matmul(name, a3, b, (e, m, n), out_dtype, (n // tn, e, 1),
                   pl.BlockSpec((None, r, m), lambda j, g, s: (g, 0, 0)),
                   pl.BlockSpec((r, tn), lambda j, g, s: (0, j)),
                   pl.BlockSpec((None, m, tn), lambda j, g, s: (g, 0, j)), TN, 1, dep=dep)


def _norm_proj(name, h, w, b, n, tm=1056, tn=640):
    r, d = h.shape
    tm, tn = _tile(r, tm, 8), _tile(n, tn, 128)

    def body(h_ref, w_ref, b_ref, hn_ref, o_ref):
        @pl.when(pl.program_id(1) == 0)
        def _():
            x = h_ref[...]
            rs = lax.rsqrt(jnp.mean(x * x, axis=1, keepdims=True) + EPS)
            hn_ref[...] = (x * rs * w_ref[...]).astype(BF16)

        o_ref[...] = lax.dot_general(hn_ref[...], b_ref[...], (NT, ((), ())), preferred_element_type=F32)

    row = pl.BlockSpec((tm, d), lambda i, j: (i, 0))
    return pl.pallas_call(
        body, name=name, grid=(r // tm, n // tn),
        in_specs=[row, pl.BlockSpec((1, d), lambda i, j: (0, 0)), pl.BlockSpec((tn, d), lambda i, j: (j, 0))],
        out_specs=[row, pl.BlockSpec((tm, tn), lambda i, j: (i, j))],
        out_shape=[jax.ShapeDtypeStruct((r, d), BF16), jax.ShapeDtypeStruct((r, n), F32)],
        compiler_params=_params(),
    )(h, w, b)


def _proj_res_norm(name, a, b, res, w, tm=528):
    r, k = a.shape
    d = b.shape[1]
    tm = _tile(r, tm, 8)

    def body(a_ref, b_ref, r_ref, w_ref, y_ref, n_ref):
        y = lax.dot_general(a_ref[...], b_ref[...], (NN, ((), ())), preferred_element_type=F32) + r_ref[...]
        y_ref[...] = y
        rs = lax.rsqrt(jnp.mean(y * y, axis=1, keepdims=True) + EPS)
        n_ref[...] = (y * rs * w_ref[...]).astype(BF16)

    row = pl.BlockSpec((tm, d), lambda i: (i, 0))
    return pl.pallas_call(
        body, name=name, grid=(r // tm,),
        in_specs=[pl.BlockSpec((tm, k), lambda i: (i, 0)), pl.BlockSpec((k, d), lambda i: (0, 0)), row,
                  pl.BlockSpec((1, d), lambda i: (0, 0))],
        out_specs=[row, row],
        out_shape=[jax.ShapeDtypeStruct((r, d), F32), jax.ShapeDtypeStruct((r, d), BF16)],
        compiler_params=_params(),
    )(a, b, res, w)


def _rms_bwd(name, x, w, dy, dres):
    r, d = x.shape
    tr = _tile(r, 264, 8)

    def body(x_ref, w_ref, dy_ref, dr_ref, dx_ref, dxb_ref, dw_ref):
        xv = x_ref[...]
        g = dy_ref[...]
        rs = lax.rsqrt(jnp.mean(xv * xv, axis=1, keepdims=True) + EPS)
        wg = g * w_ref[...]
        dx = rs * wg - xv * (rs * rs * rs) * jnp.mean(xv * wg, axis=1, keepdims=True) + dr_ref[...]
        dx_ref[...] = dx
        dxb_ref[...] = dx.astype(BF16)
        part = jnp.sum(g * xv * rs, axis=0, keepdims=True)

        @pl.when(pl.program_id(0) == 0)
        def _():
            dw_ref[...] = part

        @pl.when(pl.program_id(0) > 0)
        def _():
            dw_ref[...] += part

    row = pl.BlockSpec((tr, d), lambda i: (i, 0))
    vec = pl.BlockSpec((1, d), lambda i: (0, 0))
    return pl.pallas_call(
        body, name=name, grid=(r // tr,), in_specs=[row, vec, row, row], out_specs=[row, row, vec],
        out_shape=[jax.ShapeDtypeStruct((r, d), F32), jax.ShapeDtypeStruct((r, d), BF16),
                   jax.ShapeDtypeStruct((1, d), F32)],
        compiler_params=_params(),
    )(x, w, dy, dres)


def _final_loss(name, h, w, target):
    r, d = h.shape
    nb = r // CHUNK

    def body(h_ref, w_ref, t_ref, dh_ref, dhb_ref, dw_ref, ls_ref):
        i = pl.program_id(0)

        @pl.when(i == 0)
        def _():
            dh_ref[...] = jnp.zeros_like(dh_ref)
            dhb_ref[...] = jnp.zeros_like(dhb_ref)
            dw_ref[...] = jnp.zeros_like(dw_ref)
            ls_ref[...] = jnp.zeros_like(ls_ref)

        @pl.when(i > 0)
        def _():
            xv = h_ref[...]
            wv = w_ref[...]
            rs = lax.rsqrt(jnp.mean(xv * xv, axis=1, keepdims=True) + EPS)
            err = xv * rs * wv - t_ref[...]
            sq = jnp.sum(jnp.sum(err * err, axis=1, keepdims=True), axis=0, keepdims=True)
            ls_ref[...] += jnp.broadcast_to(sq * (0.5 / d), ls_ref.shape)
            g = err * (1.0 / d)
            wg = g * wv
            dx = rs * wg - xv * (rs * rs * rs) * jnp.mean(xv * wg, axis=1, keepdims=True)
            dh_ref[...] = dx
            dhb_ref[...] = dx.astype(BF16)
            dw_ref[...] += jnp.sum(g * xv * rs, axis=0, keepdims=True)

    row = pl.BlockSpec((CHUNK, d), lambda i: (i, 0))
    vec = pl.BlockSpec((1, d), lambda i: (0, 0))
    return pl.pallas_call(
        body, name=name, grid=(nb,),
        in_specs=[row, vec, pl.BlockSpec((CHUNK, d), lambda i: (jnp.maximum(i - 1, 0), 0))],
        out_specs=[row, row, vec, pl.BlockSpec((1, 128), lambda i: (0, 0))],
        out_shape=[jax.ShapeDtypeStruct((r, d), F32), jax.ShapeDtypeStruct((r, d), BF16),
                   jax.ShapeDtypeStruct((1, d), F32), jax.ShapeDtypeStruct((1, 128), F32)],
        compiler_params=_params(),
    )(h, w, target)


def _ffn_in(name, hf, wg_t, wu_t, dep=None, tm=1056, tn=512):
    r, d = hf.shape
    f = wg_t.shape[0]
    tm, tn = _tile(r, tm, 8), _tile(f, tn, 128)

    def body(h_ref, wg_ref, wu_ref, *rest):
        g_ref, u_ref, a_ref = rest[-3:]
        x = h_ref[...]
        g = lax.dot_general(x, wg_ref[...], (NT, ((), ())), preferred_element_type=F32)
        u = lax.dot_general(x, wu_ref[...], (NT, ((), ())), preferred_element_type=F32)
        g_ref[...] = g.astype(BF16)
        u_ref[...] = u.astype(BF16)
        a_ref[...] = (g * _sigmoid(g) * u).astype(BF16)

    wspec = pl.BlockSpec((tn, d), lambda i, j: (j, 0))
    ospec = pl.BlockSpec((tm, tn), lambda i, j: (i, j))
    ins, specs = [hf, wg_t, wu_t], [pl.BlockSpec((tm, d), lambda i, j: (i, 0)), wspec, wspec]
    if dep is not None:
        ins.append(dep)
        specs.append(pl.BlockSpec((8, 128), lambda *_: (0, 0)))
    return pl.pallas_call(
        body, name=name, grid=(r // tm, f // tn), in_specs=specs, out_specs=[ospec] * 3,
        out_shape=[jax.ShapeDtypeStruct((r, f), BF16)] * 3, compiler_params=_params(),
    )(*ins)


def _ffn_act_bwd(name, dh, wd, g, u, dep=None, tm=1056, tn=512):
    r, d = dh.shape
    f = wd.shape[0]
    tm, tn = _tile(r, tm, 8), _tile(f, tn, 128)

    def body(dh_ref, wd_ref, g_ref, u_ref, *rest):
        dg_ref, du_ref = rest[-2:]
        tr = _tile(tm, 264, 8)
        for r0 in range(0, tm, tr):
            for c0 in range(0, tn, V7X_MXU_COLS):
                rows, cols = slice(r0, r0 + tr), slice(c0, c0 + V7X_MXU_COLS)
                da = lax.dot_general(dh_ref[rows, :], wd_ref[cols, :], (NT, ((), ())), preferred_element_type=F32)
                gv = g_ref[rows, cols].astype(F32)
                s = _sigmoid(gv)
                t = da * s
                du_ref[rows, cols] = (t * gv).astype(BF16)
                dg_ref[rows, cols] = (t * u_ref[rows, cols].astype(F32) * (1.0 + gv - gv * s)).astype(BF16)

    tile = pl.BlockSpec((tm, tn), lambda i, j: (i, j))
    ins = [dh, wd, g, u]
    specs = [pl.BlockSpec((tm, d), lambda i, j: (i, 0)), pl.BlockSpec((tn, d), lambda i, j: (j, 0)), tile, tile]
    if dep is not None:
        ins.append(dep)
        specs.append(pl.BlockSpec((8, 128), lambda *_: (0, 0)))
    return pl.pallas_call(
        body, name=name, grid=(r // tm, f // tn), in_specs=specs, out_specs=[tile] * 2,
        out_shape=[jax.ShapeDtypeStruct((r, f), BF16)] * 2, compiler_params=_params(),
    )(*ins)


def _shift_down(a, k):
    row = lax.broadcasted_iota(jnp.int32, a.shape, 0)
    return jnp.where(row >= k, pltpu.roll(a, k, 0), 0.0)


def _shift_up(a, k):
    n = a.shape[0]
    row = lax.broadcasted_iota(jnp.int32, a.shape, 0)
    return jnp.where(row < n - k, pltpu.roll(a, n - k, 0), 0.0)


def _conv_fwd(name, pc, cw, cat):
    _, r, w = pc.shape
    nblk = w // 128

    def body(pc_ref, cw_ref, cat_ref, o_ref):
        a = pc_ref[2] * pc_ref[0]
        cwv = cw_ref[...]
        conv = _shift_down(a, 2) * cwv[0:1] + _shift_down(a, 1) * cwv[1:2] + a * cwv[2:3]
        o_ref[...] = (pc_ref[1] * conv).astype(BF16)

    return pl.pallas_call(
        body, name=name, grid=(nblk,),
        in_specs=[pl.BlockSpec((3, r, 128), lambda j: (0, 0, j)), pl.BlockSpec((8, 128), lambda j: (0, j)),
                  pl.BlockSpec(memory_space=pl.ANY)],
        out_specs=pl.BlockSpec((r, 128), lambda j: (0, nblk + j)),
        out_shape=jax.ShapeDtypeStruct(cat.shape, BF16), input_output_aliases={2: 0},
        compiler_params=_params(),
    )(pc, cw, cat)


def _conv_bwd(name, dcat, pc, cw):
    _, r, w = pc.shape
    nblk = w // 128

    def body(dy_ref, pc_ref, cw_ref, dpc_ref, dcw_ref):
        u, gb, gc = pc_ref[0], pc_ref[1], pc_ref[2]
        cwv = cw_ref[...]
        dy = dy_ref[...]
        a = gc * u
        a1, a2 = _shift_down(a, 1), _shift_down(a, 2)
        conv = a2 * cwv[0:1] + a1 * cwv[1:2] + a * cwv[2:3]
        dconv = dy * gb
        da = dconv * cwv[2:3] + _shift_up(dconv, 1) * cwv[1:2] + _shift_up(dconv, 2) * cwv[0:1]
        dpc_ref[0] = (da * gc).astype(BF16)
        dpc_ref[1] = (dy * conv).astype(BF16)
        dpc_ref[2] = (da * u).astype(BF16)
        row = lax.broadcasted_iota(jnp.int32, (8, 128), 0)
        dw0 = jnp.sum(dconv * a2, axis=0, keepdims=True)
        dw1 = jnp.sum(dconv * a1, axis=0, keepdims=True)
        dw2 = jnp.sum(dconv * a, axis=0, keepdims=True)
        dcw_ref[...] = jnp.where(row == 0, dw0, jnp.where(row == 1, dw1, jnp.where(row == 2, dw2, 0.0)))

    return pl.pallas_call(
        body, name=name, grid=(nblk,),
        in_specs=[pl.BlockSpec((r, 128), lambda j: (0, nblk + j)),
                  pl.BlockSpec((3, r, 128), lambda j: (0, 0, j)), pl.BlockSpec((8, 128), lambda j: (0, j))],
        out_specs=[pl.BlockSpec((3, r, 128), lambda j: (0, 0, j)), pl.BlockSpec((8, 128), lambda j: (0, j))],
        out_shape=[jax.ShapeDtypeStruct((3, r, w), BF16), jax.ShapeDtypeStruct((8, w), F32)],
        compiler_params=_params(),
    )(dcat, pc, cw)


def _dot(a, b, dims):
    return lax.dot_general(a, b, (dims, ((), ())), preferred_element_type=F32)


def _col_to_row(xc, eye):
    return jnp.sum(jnp.where(eye, xc, 0.0), axis=0, keepdims=True)


def _row_to_col(xr, eye):
    return jnp.sum(jnp.where(eye, xr, 0.0), axis=1, keepdims=True)


def _gate_tiles(graw, bias, row0):
    th = jnp.tanh((graw + bias) / GATE_CAP)
    z = GATE_CAP * th
    row = lax.broadcasted_iota(jnp.int32, graw.shape, 0) + row0
    real = row >= PAD_FRONT
    li = jnp.where(real, z, -jnp.inf)
    lf = jnp.where(real, jnp.minimum(z, 0.0) - jnp.log(1.0 + jnp.exp(-jnp.abs(z))), 0.0)
    return th, z, li, lf, real


def _interleave(gens):
    results = [None] * len(gens)
    live = list(enumerate(gens))
    while live:
        still = []
        for i, gen in live:
            try:
                next(gen)
                still.append((i, gen))
            except StopIteration as stop:
                results[i] = stop.value
        live = still
    return results


def _chunk_common(pm, h, li, lf, cst, nst, mst, tril, eye):
    kraw = pm[:, QK_W + h * DQK:QK_W + (h + 1) * DQK]
    q = (pm[:, h * DQK:(h + 1) * DQK] * QSCALE).astype(BF16)
    yield
    k = kraw.astype(BF16)
    v = pm[:, 2 * QK_W + h * DV:2 * QK_W + (h + 1) * DV].astype(BF16)
    yield
    li_c = li[:, h:h + 1]
    lf_c = lf[:, HEADS + h:HEADS + h + 1]
    li_r = _col_to_row(li_c, eye)
    yield
    lf_r = _col_to_row(lf_c, eye)
    yield
    b_c = jnp.sum(jnp.where(tril, lf_r, 0.0), axis=1, keepdims=True)
    yield
    b_r = _col_to_row(b_c, eye)
    yield
    dmat = jnp.where(tril, b_c - b_r + li_r, -jnp.inf)
    inter = b_c + mst
    yield
    mt = jnp.maximum(inter, jnp.max(dmat, axis=1, keepdims=True))
    yield
    w_inter = jnp.exp(inter - mt)
    p = jnp.exp(dmat - mt)
    yield
    s = _dot(q, k, NT) * p
    yield
    cb = cst.astype(BF16)
    nb = nst.astype(BF16).astype(F32)
    qc = _dot(q, cb, NN)
    yield
    qn = jnp.sum(q.astype(F32) * nb, axis=1, keepdims=True)
    yield
    den = w_inter * qn + jnp.sum(s, axis=1, keepdims=True)
    yield
    dn = jnp.maximum(jnp.abs(den), jnp.exp(-mt))
    b_end = b_c[CHUNK - 1:CHUNK, :]
    decay = b_end - b_c + li_c
    yield
    m_new = jnp.maximum(b_end + mst, jnp.max(decay, axis=0, keepdims=True))
    yield
    w_old = jnp.exp(b_end + mst - m_new)
    w_in = jnp.exp(decay - m_new)
    kw = (w_in * kraw).astype(BF16)
    yield
    return dict(q=q, k=k, v=v, kraw=kraw, mt=mt, w_inter=w_inter, p=p, s=s, cb=cb, nb=nb, qc=qc, qn=qn,
                den=den, dn=dn, m_new=m_new, w_old=w_old, w_in=w_in, kw=kw)


def _chunks_per_step(nc):
    return 1


def _mlstm_fwd(name, pm, bias, nw):
    r = pm.shape[0]
    nc = r // CHUNK
    grp = _chunks_per_step(nc)

    def body(pm_ref, b_ref, nw_ref, hm_ref, ht_ref, cs_ref, ns_ref, ms_ref, c_scr, n_scr, m_scr):
        step = pl.program_id(0)

        @pl.when(step == 0)
        def _():
            c_scr[...] = jnp.zeros_like(c_scr)
            n_scr[...] = jnp.zeros_like(n_scr)
            m_scr[...] = jnp.zeros_like(m_scr)

        rr = lax.broadcasted_iota(jnp.int32, (CHUNK, CHUNK), 0)
        cc = lax.broadcasted_iota(jnp.int32, (CHUNK, CHUNK), 1)
        tril, eye = cc <= rr, cc == rr
        bv, nwv = b_ref[...], nw_ref[...]
        states = [(c_scr[h], n_scr[h], m_scr[h]) for h in range(HEADS)]
        for g in range(grp):
            rows = slice(g * CHUNK, (g + 1) * CHUNK)
            pmv = pm_ref[rows, :]
            _, _, li, lf, _ = _gate_tiles(pmv[:, GATE_COL:GATE_COL + 128], bv, (step * grp + g) * CHUNK)
            def head(h, cst, nst, mst, g=g, rows=rows, pmv=pmv, li=li, lf=lf):
                f = yield from _chunk_common(pmv, h, li, lf, cst, nst, mst, tril, eye)
                num = f["w_inter"] * f["qc"] + _dot(f["s"].astype(BF16), f["v"], NN)
                yield
                hh = num / f["dn"]
                yield
                c_new = f["w_old"] * cst + _dot(f["kw"], f["v"], TN)
                yield
                n_new = f["w_old"] * nst + jnp.sum(
                    f["w_in"].astype(BF16).astype(F32) * f["k"].astype(F32), axis=0, keepdims=True)
                yield
                sl = slice(h * DV, (h + 1) * DV)
                rs = lax.rsqrt(jnp.mean(hh * hh, axis=1, keepdims=True) + EPS)
                yield
                og = pmv[:, 2 * QK_W + MLSTM_W + h * DV:2 * QK_W + MLSTM_W + (h + 1) * DV]
                cs_ref[g, h] = cst
                ns_ref[g, h] = nst
                ms_ref[g, h] = mst
                ht_ref[rows, sl] = hh
                yield
                hm_ref[rows, sl] = (_sigmoid(og) * (hh * rs * nwv[:, sl])).astype(BF16)
                return c_new, n_new, f["m_new"]

            states = _interleave([head(h, *states[h]) for h in range(HEADS)])
        for h, (cst, nst, mst) in enumerate(states):
            c_scr[h] = cst
            n_scr[h] = nst
            m_scr[h] = mst

    return pl.pallas_call(
        body, name=name, grid=(nc // grp,),
        in_specs=[pl.BlockSpec((grp * CHUNK, PM_W), lambda i: (i, 0)), pl.BlockSpec((1, 128), lambda i: (0, 0)),
                  pl.BlockSpec((1, MLSTM_W), lambda i: (0, 0))],
        out_specs=[pl.BlockSpec((grp * CHUNK, MLSTM_W), lambda i: (i, 0)),
                   pl.BlockSpec((grp * CHUNK, MLSTM_W), lambda i: (i, 0)),
                   pl.BlockSpec((grp, HEADS, DQK, DV), lambda i: (i, 0, 0, 0)),
                   pl.BlockSpec((grp, HEADS, 1, DQK), lambda i: (i, 0, 0, 0)),
                   pl.BlockSpec((grp, HEADS, 1, 1), lambda i: (i, 0, 0, 0))],
        out_shape=[jax.ShapeDtypeStruct((r, MLSTM_W + CONV_W), BF16), jax.ShapeDtypeStruct((r, MLSTM_W), F32),
                   jax.ShapeDtypeStruct((nc, HEADS, DQK, DV), F32),
                   jax.ShapeDtypeStruct((nc, HEADS, 1, DQK), F32),
                   jax.ShapeDtypeStruct((nc, HEADS, 1, 1), F32)],
        scratch_shapes=[pltpu.VMEM((HEADS, DQK, DV), F32), pltpu.VMEM((HEADS, 1, DQK), F32),
                        pltpu.VMEM((HEADS, 1, 1), F32)],
        compiler_params=_params(),
    )(pm, bias, nw)


def _mlstm_bwd(name, dcat, pm, ht, cs, ns, ms, bias, nw):
    r = pm.shape[0]
    nc = r // CHUNK
    grp = _chunks_per_step(nc)
    nsteps = nc // grp

    def body(dy_ref, pm_ref, ht_ref, cs_ref, ns_ref, ms_ref, b_ref, nw_ref, dpm_ref, dnw_ref, db_ref,
             dc_scr, dn_scr):
        step = pl.program_id(0)

        @pl.when(step == 0)
        def _():
            dc_scr[...] = jnp.zeros_like(dc_scr)
            dn_scr[...] = jnp.zeros_like(dn_scr)
            dnw_ref[...] = jnp.zeros_like(dnw_ref)
            db_ref[...] = jnp.zeros_like(db_ref)

        rr = lax.broadcasted_iota(jnp.int32, (CHUNK, CHUNK), 0)
        cc = lax.broadcasted_iota(jnp.int32, (CHUNK, CHUNK), 1)
        tril, eye, triu = cc <= rr, cc == rr, cc >= rr
        lane = lax.broadcasted_iota(jnp.int32, (CHUNK, 128), 1)
        rowid = lax.broadcasted_iota(jnp.int32, (CHUNK, 1), 0)
        bv, nwv = b_ref[...], nw_ref[...]
        carried = [(dc_scr[h], dn_scr[h]) for h in range(HEADS)]
        dnw_acc = [jnp.zeros((1, DV), F32) for _ in range(HEADS)]
        db_acc = jnp.zeros((1, 128), F32)
        for g in reversed(range(grp)):
            rows = slice(g * CHUNK, (g + 1) * CHUNK)
            ci = (nsteps - 1 - step) * grp + g
            pmv = pm_ref[rows, :]
            th, z, li, lf, real = _gate_tiles(pmv[:, GATE_COL:GATE_COL + 128], bv, ci * CHUNK)
            heads = _interleave([
                _mlstm_bwd_head(h, pmv, ht_ref[rows, h * DV:(h + 1) * DV], dy_ref[rows, h * DV:(h + 1) * DV], nwv,
                                li, lf, cs_ref[g, h], ns_ref[g, h], ms_ref[g, h], carried[h][0], carried[h][1],
                                tril, eye, triu, lane, rowid, dpm_ref, rows)
                for h in range(HEADS)])
            carried = [(dc_new, dn_new) for _, dc_new, dn_new, _ in heads]
            dgt = heads[0][0] + heads[1][0] + heads[2][0] + heads[3][0]
            dnw_acc = [dnw_acc[h] + heads[h][3] for h in range(HEADS)]
            dact = jnp.where(lane < HEADS, 1.0, 1.0 - _sigmoid(z)) * (1.0 - th * th)
            dgraw = jnp.where(real & (lane < 2 * HEADS), dgt * dact, 0.0)
            dpm_ref[rows, GATE_COL:GATE_COL + 128] = dgraw.astype(BF16)
            db_acc = db_acc + jnp.sum(dgraw, axis=0, keepdims=True)
        for h, (dcn, dnn) in enumerate(carried):
            dc_scr[h] = dcn
            dn_scr[h] = dnn
            dnw_ref[:, h * DV:(h + 1) * DV] += dnw_acc[h]
        db_ref[...] += db_acc

    rev = lambda i: (nsteps - 1 - i, 0)
    rev4 = lambda i: (nsteps - 1 - i, 0, 0, 0)
    return pl.pallas_call(
        body, name=name, grid=(nsteps,),
        in_specs=[pl.BlockSpec((grp * CHUNK, MLSTM_W), rev), pl.BlockSpec((grp * CHUNK, PM_W), rev),
                  pl.BlockSpec((grp * CHUNK, MLSTM_W), rev),
                  pl.BlockSpec((grp, HEADS, DQK, DV), rev4), pl.BlockSpec((grp, HEADS, 1, DQK), rev4),
                  pl.BlockSpec((grp, HEADS, 1, 1), rev4),
                  pl.BlockSpec((1, 128), lambda i: (0, 0)), pl.BlockSpec((1, MLSTM_W), lambda i: (0, 0))],
        out_specs=[pl.BlockSpec((grp * CHUNK, PM_W), rev), pl.BlockSpec((1, MLSTM_W), lambda i: (0, 0)),
                   pl.BlockSpec((1, 128), lambda i: (0, 0))],
        out_shape=[jax.ShapeDtypeStruct((r, PM_W), BF16), jax.ShapeDtypeStruct((1, MLSTM_W), F32),
                   jax.ShapeDtypeStruct((1, 128), F32)],
        scratch_shapes=[pltpu.VMEM((HEADS, DQK, DV), F32), pltpu.VMEM((HEADS, 1, DQK), F32)],
        compiler_params=_params(),
    )(dcat, pm, ht, cs, ns, ms, bias, nw)


def _mlstm_bwd_head(h, pmv, hh, y, nwv, li, lf, cst, nst, mst, dcn, dnn, tril, eye, triu, lane, rowid,
                    dpm_ref, rows):
    f = yield from _chunk_common(pmv, h, li, lf, cst, nst, mst, tril, eye)
    q, k, v, s, p = f["q"], f["k"], f["v"], f["s"], f["p"]
    w_inter, w_in, w_old, dn = f["w_inter"], f["w_in"], f["w_old"], f["dn"]
    osl = slice(2 * QK_W + MLSTM_W + h * DV, 2 * QK_W + MLSTM_W + (h + 1) * DV)
    sg = _sigmoid(pmv[:, osl])
    yield
    rs = lax.rsqrt(jnp.mean(hh * hh, axis=1, keepdims=True) + EPS)
    yield
    nwh = nwv[:, h * DV:(h + 1) * DV]
    dpm_ref[rows, osl] = (y * (hh * rs * nwh) * sg * (1.0 - sg)).astype(BF16)
    yield
    dhn = y * sg
    dnw_h = jnp.sum(dhn * hh * rs, axis=0, keepdims=True)
    yield
    wd = dhn * nwh
    dhh = rs * wd - hh * (rs * rs * rs) * jnp.mean(hh * wd, axis=1, keepdims=True)
    yield
    dnum = dhh / dn
    dd = -jnp.sum(dhh * hh, axis=1, keepdims=True) / dn
    yield
    dden = jnp.where(jnp.abs(f["den"]) > jnp.exp(-f["mt"]), dd * jnp.sign(f["den"]), 0.0)
    dnum_b = dnum.astype(BF16)
    wdn = (w_inter * dnum).astype(BF16)
    wid = (w_inter * dden).astype(BF16).astype(F32)
    yield
    ds = _dot(dnum_b, v, NT) + dden
    yield
    dsp = (ds * p).astype(BF16)
    yield
    dq = _dot(dsp, k, NN) + _dot(wdn, f["cb"], NT) + wid * f["nb"]
    yield
    dk = _dot(dsp, q, TN)
    yield
    dv = _dot(s.astype(BF16), dnum_b, TN)
    yield
    g = ds * s
    g_col = _row_to_col(jnp.sum(g, axis=0, keepdims=True), eye)
    yield
    db = jnp.sum(g, axis=1, keepdims=True) - g_col
    dli = g_col
    yield
    db = db + (jnp.sum(dnum * f["qc"], axis=1, keepdims=True) + dden * f["qn"]) * w_inter
    yield
    dcnb = dcn.astype(BF16)
    dnnb = dnn.astype(BF16).astype(F32)
    dkw = _dot(v, dcnb, NT) + dnnb
    yield
    dk = dk + w_in * dkw
    dv = dv + _dot(f["kw"], dcnb, NN)
    yield
    ddecay = jnp.sum(dkw * f["kraw"], axis=1, keepdims=True) * w_in
    yield
    dw_old = (jnp.sum(jnp.sum(dcn * cst, axis=1, keepdims=True), axis=0, keepdims=True)
              + jnp.sum(dnn * nst, axis=1, keepdims=True))
    yield
    db_end = dw_old * w_old + jnp.sum(ddecay, axis=0, keepdims=True)
    db = db - ddecay + jnp.where(rowid == CHUNK - 1, db_end, 0.0)
    dli = dli + ddecay
    yield
    dc_new = w_old * dcn + _dot(q, wdn, TN)
    yield
    dn_new = w_old * dnn + jnp.sum(wid * q.astype(F32), axis=0, keepdims=True)
    yield
    dlf = jnp.sum(jnp.where(triu, _col_to_row(db, eye), 0.0), axis=1, keepdims=True)
    yield
    gate_part = jnp.where(lane == h, dli, 0.0) + jnp.where(lane == HEADS + h, dlf, 0.0)
    dpm_ref[rows, h * DQK:(h + 1) * DQK] = (dq * QSCALE).astype(BF16)
    yield
    dpm_ref[rows, QK_W + h * DQK:QK_W + (h + 1) * DQK] = dk.astype(BF16)
    yield
    dpm_ref[rows, 2 * QK_W + h * DV:2 * QK_W + (h + 1) * DV] = dv.astype(BF16)
    return gate_part, dc_new, dn_new, dnw_h


def _my_place():
    return lax.axis_index("x"), lax.axis_index("y"), lax.axis_index("c")


def _flip(v, bit):
    return 1 - v if bit else v


def _exchange_small(name, blk, reduce):
    r, c = blk.shape

    def body(x_ref, o_ref, *rest):
        slots = rest[0] if reduce else o_ref
        send_sems, recv_sems = rest[-2], rest[-1]
        x, y, cc = _my_place()
        me = 4 * x + 2 * y + cc
        slots[me] = x_ref[...]
        copies = []
        for k in range(1, N_DEV):
            peer = (_flip(x, k & 4), _flip(y, k & 2), _flip(cc, k & 1))
            cp = pltpu.make_async_remote_copy(
                src_ref=x_ref, dst_ref=slots.at[me], send_sem=send_sems.at[k - 1],
                recv_sem=recv_sems.at[k - 1], device_id=peer, device_id_type=MESH)
            cp.start()
            copies.append(cp)
        for cp in copies:
            cp.wait()
        if reduce:
            acc = slots[0]
            for d in range(1, N_DEV):
                acc = acc + slots[d]
            o_ref[...] = acc

    scratch = ([pltpu.VMEM((N_DEV, r, c), F32)] if reduce else []) + [
        pltpu.SemaphoreType.DMA((N_DEV - 1,)), pltpu.SemaphoreType.DMA((N_DEV - 1,))]
    return pl.pallas_call(
        body, name=name,
        out_shape=jax.ShapeDtypeStruct((r, c) if reduce else (N_DEV, r, c), F32),
        in_specs=[pl.BlockSpec(memory_space=pltpu.VMEM)], out_specs=pl.BlockSpec(memory_space=pltpu.VMEM),
        scratch_shapes=scratch, compiler_params=_params(),
    )(blk)


HBM_SPEC = pl.BlockSpec(memory_space=pltpu.HBM)
SEM_SPEC = pl.BlockSpec(memory_space=pltpu.SEMAPHORE)
ANY_SPEC = pl.BlockSpec(memory_space=pl.ANY)
DATAFLOW = pltpu.SideEffectType.DATAFLOW_SIDE_EFFECTING


def _split_copy(name, arrays, start=None, wait=None, after=None):
    results, token = _split_copies(name, [(arrays, start, wait)], after)
    return results[0][0], results[0][1], token


def _split_copies(name, jobs, after=None):
    operands, in_specs, out_shape, out_specs, aliases = [], [], [], [], {}
    in_at, out_at = [], []
    for arrays, start, wait in jobs:
        in_at.append(len(operands))
        operands += [pltpu.with_memory_space_constraint(a, pltpu.HBM) for a in arrays]
        in_specs += [HBM_SPEC] * len(arrays)
        if wait:
            operands += list(wait[1])
            in_specs += [SEM_SPEC, SEM_SPEC]
    if after is not None:
        operands.append(after)
        in_specs.append(ANY_SPEC)
    for j, (arrays, start, wait) in enumerate(jobs):
        out_at.append(len(out_shape))
        if start:
            out_shape += [pltpu.SemaphoreType.DMA((start[1],)), pltpu.SemaphoreType.DMA((start[1],))]
            out_specs += [SEM_SPEC, SEM_SPEC]
        for i, a in enumerate(arrays):
            aliases[in_at[j] + i] = len(out_shape)
            out_shape.append(pltpu.HBM(a.shape, a.dtype))
            out_specs.append(HBM_SPEC)
    any_start = any(start for _, start, _ in jobs)
    if any_start:
        out_shape.append(jax.ShapeDtypeStruct((8, 128), F32))
        out_specs.append(pl.BlockSpec(memory_space=pltpu.VMEM))
    n_in = len(operands)

    def body(*refs):
        for j, (arrays, start, wait) in enumerate(jobs):
            if wait:
                ins = refs[in_at[j]:in_at[j] + len(arrays)]
                at = in_at[j] + len(arrays)
                for cp in wait[0](ins, refs[at], refs[at + 1]):
                    cp.wait_send()
                    cp.wait_recv()
        for j, (arrays, start, wait) in enumerate(jobs):
            if start:
                ins = refs[in_at[j]:in_at[j] + len(arrays)]
                at = n_in + out_at[j]
                for cp in start[0](ins, refs[at], refs[at + 1]):
                    cp.start()
        if any_start:
            token = refs[n_in + len(out_shape) - 1]
            token[...] = jnp.zeros_like(token)

    outs = pl.pallas_call(
        body, name=name, in_specs=in_specs, out_specs=out_specs, out_shape=out_shape,
        input_output_aliases=aliases, compiler_params=pltpu.CompilerParams(has_side_effects=DATAFLOW),
    )(*operands)
    results = []
    for j, (arrays, start, wait) in enumerate(jobs):
        at = out_at[j]
        sems = (outs[at], outs[at + 1]) if start else None
        at += 2 if start else 0
        results.append((list(outs[at:at + len(arrays)]), sems))
    return results, (outs[-1] if any_start else None)


def _remote(src, dst, send_sems, recv_sems, k, to):
    return pltpu.make_async_remote_copy(src_ref=src, dst_ref=dst, send_sem=send_sems.at[k],
                                        recv_sem=recv_sems.at[k], device_id=to, device_id_type=MESH)


def _slot(px, py, pc):
    return 4 * px + 2 * py + pc


def _gather_first(refs, send_sems, recv_sems):
    x, y, c = _my_place()
    blk = refs[0].at[_slot(x, y, c)]
    targets = [(x, y, 1 - c), (1 - x, y, c), (x, 1 - y, c)]
    return [_remote(blk, blk, send_sems, recv_sems, k, to) for k, to in enumerate(targets)]


def _gather_relay(refs, send_sems, recv_sems):
    x, y, c = _my_place()
    rows = refs[0].shape[1]
    half = rows // 32 * 16
    from_x, from_y = _slot(1 - x, y, c), _slot(x, 1 - y, c)
    upper = refs[0].at[from_x, pl.ds(0, half)]
    lower = refs[0].at[from_y, pl.ds(half, rows - half)]
    return [_remote(upper, upper, send_sems, recv_sems, 0, (x, 1 - y, c)),
            _remote(lower, lower, send_sems, recv_sems, 1, (1 - x, y, c)),
            _remote(refs[0].at[from_x], refs[0].at[from_x], send_sems, recv_sems, 2, (x, y, 1 - c)),
            _remote(refs[0].at[from_y], refs[0].at[from_y], send_sems, recv_sems, 3, (x, y, 1 - c))]


def _gather_last(refs, send_sems, recv_sems):
    x, y, c = _my_place()
    blk = refs[0].at[_slot(1 - x, 1 - y, c)]
    return [_remote(blk, blk, send_sems, recv_sems, 0, (x, y, 1 - c))]


def _scatter_sibling(n):
    def copies(refs, send_sems, recv_sems):
        x, y, c = _my_place()
        return [_remote(refs[a].at[2 * j + 1 - c], refs[n + a].at[j], send_sems, recv_sems, 4 * a + j, (x, y, 1 - c))
                for a in range(n) for j in range(4)]
    return copies


def _scatter_chips(n):
    def copies(refs, send_sems, recv_sems):
        x, y, c = _my_place()
        out = []
        for a in range(n):
            for k in range(1, 4):
                px, py = _flip(x, k & 2), _flip(y, k & 1)
                out.append(_remote(refs[a].at[2 * px + py], refs[n + a].at[2 * x + y], send_sems, recv_sems,
                                   3 * a + k - 1, (px, py, c)))
        return out
    return copies


def _pair_sum(name, core, g, t):
    _, r, c = g.shape
    tr = _tile(r, 512, 8)
    g4 = g.reshape(4, 2, r, c)

    def body(core_ref, g_ref, t_ref, o_ref):
        o_ref[...] = (g_ref[...].astype(F32) + t_ref[...].astype(F32)).astype(BF16)

    return pl.pallas_call(
        body, name=name,
        grid_spec=pltpu.PrefetchScalarGridSpec(
            num_scalar_prefetch=1, grid=(4, r // tr),
            in_specs=[pl.BlockSpec((None, None, tr, c), lambda j, i, core_ref: (j, core_ref[0], i, 0)),
                      pl.BlockSpec((None, tr, c), lambda j, i, core_ref: (j, i, 0))],
            out_specs=pl.BlockSpec((None, tr, c), lambda j, i, core_ref: (j, i, 0))),
        out_shape=jax.ShapeDtypeStruct((4, r, c), BF16), compiler_params=_params(),
    )(core, g4, t)


def _adam_math(w, g, m, v):
    m2 = ADAM_B1 * m + (1.0 - ADAM_B1) * g
    v2 = ADAM_B2 * v + (1.0 - ADAM_B2) * (g * g)
    m_hat = m2 / (1.0 - ADAM_B1 ** ADAM_STEP)
    v_hat = v2 / (1.0 - ADAM_B2 ** ADAM_STEP)
    delta = -ADAM_LR * (m_hat / (jnp.sqrt(v_hat) + ADAM_EPS) + ADAM_WD * w)
    return delta, m2, v2


def _adam_sharded(name, chip, w, m, v, grads):
    _, r, c = w.shape
    tr = _tile(r, 256, 8)
    tc = c if tr < r else _tile(c, 256, 128)

    def body(chip_ref, w_ref, m_ref, v_ref, p0_ref, q0_ref, p1_ref, q1_ref, g_ref, d_ref, nm_ref, nv_ref):
        mine = chip_ref[0]

        def total(p_ref, q_ref):
            acc = None
            for j in range(4):
                part = jnp.where(mine == j, p_ref[...], q_ref[j]).astype(F32)
                acc = part if acc is None else acc + part
            return acc

        g = jnp.where(pl.program_id(0) == 0, total(p0_ref, q0_ref), total(p1_ref, q1_ref))
        delta, m2, v2 = _adam_math(w_ref[...], g, m_ref[...], v_ref[...])
        g_ref[...] = g
        d_ref[...] = delta
        nm_ref[...] = m2
        nv_ref[...] = v2

    def grad_specs(layer):
        at = lambda l, i, j: (jnp.where(l == layer, i, 0), jnp.where(l == layer, j, 0))
        return [pl.BlockSpec((None, tr, tc), lambda l, i, j, chip_ref: (chip_ref[0],) + at(l, i, j)),
                pl.BlockSpec((4, tr, tc), lambda l, i, j, chip_ref: (0,) + at(l, i, j))]

    wspec = pl.BlockSpec((None, tr, tc), lambda l, i, j, chip_ref: (l, i, j))
    sds = jax.ShapeDtypeStruct(w.shape, F32)
    return pl.pallas_call(
        body, name=name,
        grid_spec=pltpu.PrefetchScalarGridSpec(
            num_scalar_prefetch=1, grid=(2, r // tr, c // tc),
            in_specs=[wspec, wspec, wspec] + grad_specs(0) + grad_specs(1), out_specs=[wspec] * 4),
        out_shape=[sds] * 4, compiler_params=_params(),
    )(chip, w, m, v, grads[0][0], grads[0][1], grads[1][0], grads[1][1])


def _adam_small(name, w, m, v, g):
    def body(w_ref, m_ref, v_ref, g_ref, d_ref, nm_ref, nv_ref):
        delta, m2, v2 = _adam_math(w_ref[...], g_ref[...], m_ref[...], v_ref[...])
        d_ref[...] = delta
        nm_ref[...] = m2
        nv_ref[...] = v2

    sds = jax.ShapeDtypeStruct(w.shape, F32)
    vm = pl.BlockSpec(memory_space=pltpu.VMEM)
    return pl.pallas_call(body, name=name, in_specs=[vm] * 4, out_specs=[vm] * 3, out_shape=[sds] * 3,
                          compiler_params=_params())(w, m, v, g)


GATE_END = GATE_COL + 2 * HEADS


def _merge_dw_in(dwm_t, dwc_t):
    full = jnp.concatenate([dwm_t[:GATE_END], dwc_t.reshape(3 * CONV_W, D_MODEL)], axis=0)
    return full.reshape(N_DEV, IN_SH, D_MODEL)


def _pack128(parts):
    flat = jnp.concatenate([p.reshape(-1) for p in parts])
    n = flat.shape[0]
    rows = -(-n // 1024) * 8
    return jnp.pad(flat, (0, rows * 128 - n)).reshape(rows, 128)


def _unpack128(packed, shapes):
    flat = packed.reshape(-1)
    out, at = [], 0
    for s in shapes:
        n = int(np.prod(s))
        out.append(flat[at:at + n].reshape(s))
        at += n
    return out


def kernel(x, meta_tokens, norm_mix_w, w_in, b_gates, conv_w, mlstm_norm_w, w_out, norm_ffn_w, w_gate, w_up, w_down, norm_final_w, loss_target, m_meta_tokens, m_norm_mix_w, m_w_in, m_b_gates, m_conv_w, m_mlstm_norm_w, m_w_out, m_norm_ffn_w, m_w_gate, m_w_up, m_w_down, m_norm_final_w, v_meta_tokens, v_norm_mix_w, v_w_in, v_b_gates, v_conv_w, v_mlstm_norm_w, v_w_out, v_norm_ffn_w, v_w_gate, v_w_up, v_w_down, v_norm_final_w):
    seq = x.shape[1]
    rows = TOK0 + seq
    me = 4 * lax.axis_index("x") + 2 * lax.axis_index("y") + lax.axis_index("c")
    meta_sh = meta_tokens.shape[1]
    conv_sh = conv_w.shape[2]

    w_gate_t, m_w_gate_t, v_w_gate_t = (jnp.transpose(a, (0, 2, 1)) for a in (w_gate, m_w_gate, v_w_gate))
    w_up_t, m_w_up_t, v_w_up_t = (jnp.transpose(a, (0, 2, 1)) for a in (w_up, m_w_up, v_w_up))
    shards = []
    for l in range(DEPTH):
        shards += [jnp.transpose(w_in[l]).astype(BF16), w_out[l].astype(BF16), w_gate_t[l].astype(BF16),
                   w_up_t[l].astype(BF16), w_down[l].astype(BF16)]
    per_layer = ("w_in", "w_out", "w_gate", "w_up", "w_down")
    gather_state = {}

    def gather_step(tag, after, start=None, relay=None, last=None, done=()):
        jobs, idx = [], []
        if relay is not None and relay < len(shards):
            jobs.append((gather_state[relay][0], (_gather_relay, 4), (_gather_first, gather_state[relay][1])))
            idx.append(relay)
        if start is not None and start < len(shards):
            buf = lax.dynamic_update_index_in_dim(lax.empty((N_DEV,) + shards[start].shape, BF16), shards[start], me, 0)
            jobs.append(([buf], (_gather_first, 3), None))
            idx.append(start)
        if last is not None:
            jobs.append((gather_state[last][0], (_gather_last, 1), (_gather_relay, gather_state[last][1])))
            idx.append(last)
        for i in done:
            jobs.append((gather_state[i][0], None, (_gather_last, gather_state[i][1])))
            idx.append(i)
        if not jobs:
            return after, []
        results, tok = _split_copies(f"gather_{tag}", jobs, after)
        for i, res in zip(idx, results):
            gather_state[i] = res
        return (after if tok is None else tok), [gather_state[i][0][0] for i in done]

    bias = [jnp.pad(b_gates[l].reshape(1, 2 * HEADS), ((0, 0), (0, 128 - 2 * HEADS))) for l in range(DEPTH)]
    nmix = [norm_mix_w[l].reshape(1, D_MODEL) for l in range(DEPTH)]
    nffn = [norm_ffn_w[l].reshape(1, D_MODEL) for l in range(DEPTH)]
    nmls = [mlstm_norm_w[l].reshape(1, MLSTM_W) for l in range(DEPTH)]
    weights = [dict() for _ in range(DEPTH)]
    saved = [dict() for _ in range(DEPTH)]

    def layer_fwd(l, h, after):
        w, s = weights[l], saved[l]
        k0 = len(per_layer) * l
        tok, _ = gather_step(f"l{l}_a", after, last=k0)
        _, (g_in,) = gather_step(f"l{l}_b", tok, done=[k0])
        tok, _ = gather_step(f"l{l}_c", g_in, relay=k0 + 1, start=k0 + 3)
        w["win_t"] = g_in.reshape(D_IN, D_MODEL)
        w["wc_t"] = w["win_t"][GATE_END:].reshape(3, CONV_W, D_MODEL)
        s["h0"] = h
        s["hn"], s["pm"] = _norm_proj(f"proj_mlstm_{l}", h, nmix[l] + tok[0, 0], w["win_t"], PM_W)
        tok, _ = gather_step(f"l{l}_d", s["pm"], relay=k0 + 2, start=k0 + 4)
        tok, _ = gather_step(f"l{l}_d2", tok, last=k0 + 1)
        s["pc"] = _mm_nt_bcols(f"proj_conv_{l}", s["hn"], w["wc_t"], F32, dep=tok)
        hm, s["ht"], s["cs"], s["ns"], s["ms"] = _mlstm_fwd(f"mlstm_fwd_{l}", s["pm"], bias[l] + tok[:1], nmls[l])
        tok, _ = gather_step(f"l{l}_e", hm, relay=k0 + 3, start=k0 + 5)
        tok, _ = gather_step(f"l{l}_e2", tok, last=k0 + 2)
        s["cat"] = _conv_fwd(f"conv_fwd_{l}", s["pc"], conv_rows[l] + tok[0, 0], hm)
        _, (g_out,) = gather_step(f"l{l}_f", s["cat"], done=[k0 + 1])
        w["wo"] = g_out.reshape(D_MODEL, D_MODEL)
        s["h1"], s["hf"] = _proj_res_norm(f"out_proj_{l}", s["cat"], w["wo"], s["h0"], nffn[l])
        tok_g, _ = gather_step(f"l{l}_g", s["h1"], relay=k0 + 4, start=k0 + 6)
        tok, _ = gather_step(f"l{l}_h", tok_g, last=k0 + 3)
        _, (g_gate, g_up) = gather_step(f"l{l}_i", tok, done=[k0 + 2, k0 + 3])
        w["wg_t"] = g_gate.reshape(D_FF, D_MODEL)
        w["wu_t"] = g_up.reshape(D_FF, D_MODEL)
        s["g"], s["u"], s["act"] = _ffn_in(f"ffn_in_{l}", s["hf"], w["wg_t"], w["wu_t"], dep=tok_g)
        tok, _ = gather_step(f"l{l}_j", s["act"], last=k0 + 4)
        _, (g_down,) = gather_step(f"l{l}_k", tok, done=[k0 + 4])
        w["wd"] = g_down.reshape(D_FF, D_MODEL)
        tok, _ = gather_step(f"l{l}_k2", tok, relay=k0 + 5, start=k0 + 7)
        return _mm_nn(f"ffn_out_{l}", s["act"], w["wd"], F32, res=s["h1"], dep=tok)

    tok, _ = gather_step("first", None, start=0)
    zero = tok[0, 0]
    small = jnp.concatenate(
        [meta_tokens + zero, jnp.pad(conv_w.reshape(DEPTH * 3, conv_sh), ((0, 2), (0, meta_sh - conv_sh)))], axis=0)
    slots = _exchange_small("gather_small", small, reduce=False)
    meta_full = jnp.transpose(slots[:, :N_META, :], (1, 0, 2)).reshape(N_META, D_MODEL)
    conv_full = jnp.transpose(slots[:, N_META:N_META + DEPTH * 3, :conv_sh], (1, 0, 2)).reshape(DEPTH, 3, CONV_W)
    conv_rows = [jnp.pad(conv_full[l], ((0, 5), (0, 0))) for l in range(DEPTH)]
    w_in_t, m_w_in_t, v_w_in_t = (jnp.transpose(a + zero, (0, 2, 1)) for a in (w_in, m_w_in, v_w_in))
    tok, w_in_t, m_w_in_t, v_w_in_t, meta_full = lax.optimization_barrier(
        (tok, w_in_t, m_w_in_t, v_w_in_t, meta_full))
    tok, _ = gather_step("pre_a", tok, relay=0)
    tok, _ = gather_step("pre_b", tok, start=1)
    tok, _ = gather_step("pre_c", tok, start=2)
    h = jnp.concatenate([jnp.zeros((PAD_FRONT, D_MODEL), F32), meta_full, x[0]], axis=0)
    h = layer_fwd(0, h, tok)
    h = layer_fwd(1, h, h)

    dh, dh_b, d_final, loss_part = _final_loss("final_loss", h, norm_final_w.reshape(1, D_MODEL), loss_target[0])

    core = lax.axis_index("c").astype(jnp.int32).reshape(1)
    chip = (2 * lax.axis_index("x") + lax.axis_index("y")).astype(jnp.int32).reshape(1)
    scatter_state = {}

    def scatter_begin(nm, grad):
        land = lax.empty((4,) + grad.shape[1:], BF16)
        arrs, sems, tok = _split_copy(f"grad_sibling_start_{nm}", [grad, land], start=(_scatter_sibling(1), 4))
        scatter_state[nm] = (arrs, sems)
        return tok

    def scatter_advance(nm, after):
        arrs, sems = scatter_state[nm]
        arrs, _, _ = _split_copy(f"grad_sibling_done_{nm}", arrs, wait=(_scatter_sibling(1), sems), after=after)
        part = _pair_sum(f"grad_pair_sum_{nm}", core, arrs[0], arrs[1])
        arrs, sems, tok = _split_copy(f"grad_chips_start_{nm}", [part, lax.empty(part.shape, BF16)],
                                      start=(_scatter_chips(1), 3))
        scatter_state[nm] = (arrs, sems)
        return tok

    def scattered(nm, after):
        arrs, sems = scatter_state[nm]
        arrs, _, _ = _split_copy(f"grad_chips_done_{nm}", arrs, wait=(_scatter_chips(1), sems), after=after)
        return arrs[0], arrs[1]

    d_mix, d_ffn, d_mls, d_bias, d_conv = ([None] * DEPTH for _ in range(5))

    def layer_bwd(l, dh, dh_b, tok):
        w, s = weights[l], saved[l]
        dg, du = _ffn_act_bwd(f"d_act_{l}", dh_b, w["wd"], s["g"], s["u"], dep=tok)
        dw_down = _mm_tn(f"dw_down_{l}", s["act"], dh_b, BF16, tm=1408, tn=1024)
        tok = scatter_begin(f"w_down_{l}", dw_down.reshape(N_DEV, FF_SH, D_MODEL))
        dhf = _mm_nn(f"d_ffn_gate_{l}", dg, w["wg_t"], F32, dep=tok)
        tok = scatter_advance(f"w_down_{l}", after=dhf)
        dhf = _mm_nn(f"d_ffn_up_{l}", du, w["wu_t"], F32, res=dhf, dep=tok)
        dw_gate = _mm_tn(f"dw_gate_{l}", dg, s["hf"], BF16, tm=1408, tn=1024)
        tok = scatter_begin(f"w_gate_{l}", dw_gate.reshape(N_DEV, FF_SH, D_MODEL))
        dw_up = _mm_tn(f"dw_up_{l}", du, s["hf"], BF16, tm=1408, tn=1024, dep=tok)
        tok = scatter_begin(f"w_up_{l}", dw_up.reshape(N_DEV, FF_SH, D_MODEL))
        dh1, dh1_b, d_ffn[l] = _rms_bwd(f"norm_ffn_bwd_{l}", s["h1"], nffn[l] + tok[0, 0], dhf, dh)
        tok = scatter_advance(f"w_gate_{l}", after=dh1)
        dcat = _mm_nt(f"d_cat_{l}", dh1_b, w["wo"], F32, tk=D_MODEL, dep=tok)
        tok = scatter_advance(f"w_up_{l}", after=dcat)
        dw_out = _mm_tn(f"dw_out_{l}", s["cat"], dh1_b, BF16, tn=1024, dep=tok)
        tok = scatter_begin(f"w_out_{l}", dw_out.reshape(N_DEV, OUT_SH, D_MODEL))
        dpm, d_mls[l], d_bias[l] = _mlstm_bwd(f"mlstm_bwd_{l}", dcat, s["pm"], s["ht"], s["cs"], s["ns"],
                                               s["ms"], bias[l] + tok[:1], nmls[l])
        dpc, d_conv[l] = _conv_bwd(f"conv_bwd_{l}", dcat, s["pc"], conv_rows[l])
        tok = scatter_advance(f"w_out_{l}", after=dpc)
        dwm_t = _mm_tn(f"dw_mlstm_{l}", dpm, s["hn"], BF16, tm=640, tn=1024, dep=tok)
        dwc_t = _mm_tn_acols(f"dw_conv_{l}", dpc, s["hn"], BF16)
        tok = scatter_begin(f"w_in_{l}", _merge_dw_in(dwm_t, dwc_t))
        dhn = _mm_nn_kt(f"d_norm_mlstm_{l}", dpm, w["win_t"], F32, tk=PM_W, dep=tok)
        dhn = _mm_nn_ksum(f"d_norm_conv_{l}", dpc, w["wc_t"], F32, res=dhn)
        tok = scatter_advance(f"w_in_{l}", after=dhn)
        dh, dh_b, d_mix[l] = _rms_bwd(f"norm_mix_bwd_{l}", s["h0"], nmix[l] + tok[0, 0], dhn, dh1)
        return dh, dh_b, tok

    dh, dh_b, tok = layer_bwd(1, dh, dh_b, None)
    dh, dh_b, tok_tail = layer_bwd(0, dh, dh_b, tok)

    pq = {}
    after = dh
    for l in reversed(range(DEPTH)):
        for nm in ("w_down", "w_gate", "w_up", "w_out", "w_in"):
            if (nm, l) != ("w_in", 0):
                pq[nm, l] = scattered(f"{nm}_{l}", after)
                after = pq[nm, l][0]
    untransposed = lambda outs: [jnp.transpose(o, (0, 2, 1)) for o in outs]
    g_out, d_out, nm_out, nv_out = _adam_sharded(
        "adam_w_out", chip, w_out, m_w_out, v_w_out, [pq["w_out", 0], pq["w_out", 1]])
    g_gate, d_gate, nm_gate, nv_gate = untransposed(_adam_sharded(
        "adam_w_gate", chip, w_gate_t, m_w_gate_t, v_w_gate_t, [pq["w_gate", 0], pq["w_gate", 1]]))
    g_up, d_up, nm_up, nv_up = untransposed(_adam_sharded(
        "adam_w_up", chip, w_up_t, m_w_up_t, v_w_up_t, [pq["w_up", 0], pq["w_up", 1]]))
    g_down, d_down, nm_down, nv_down = _adam_sharded(
        "adam_w_down", chip, w_down, m_w_down, v_w_down, [pq["w_down", 0], pq["w_down", 1]])
    pq["w_in", 0] = scattered("w_in_0", nv_down)
    g_in, d_in, nm_in, nv_in = untransposed(_adam_sharded(
        "adam_w_in", chip, w_in_t, m_w_in_t, v_w_in_t, [pq["w_in", 0], pq["w_in", 1]]))

    bg = jnp.concatenate([d_bias[l][0, :2 * HEADS] for l in range(DEPTH)])
    red_in = jnp.concatenate([
        dh[PAD_FRONT:TOK0], d_mix[0], d_mix[1], d_ffn[0], d_ffn[1], d_final,
        jnp.concatenate([d_mls[0], d_mls[1]], axis=1),
        jnp.stack([d_conv[l][:3] for l in range(DEPTH)]).reshape(3, 2 * CONV_W),
        jnp.pad(bg, (0, D_MODEL - bg.shape[0])).reshape(1, D_MODEL),
        jnp.pad(loss_part[:, :1], ((0, 0), (0, D_MODEL - 1))),
        jnp.zeros((5, D_MODEL), F32) + tok_tail[0, 0]], axis=0)
    red = _exchange_small("reduce_small", red_in, reduce=True)
    loss = red[26, 0]
    g_meta = lax.dynamic_slice_in_dim(red[:N_META], me * meta_sh, meta_sh, axis=1)
    g_mix, g_ffn, g_final = red[16:18], red[18:20], red[20]
    g_mls = red[21].reshape(DEPTH, MLSTM_W)
    g_conv = lax.dynamic_slice_in_dim(red[22:25].reshape(DEPTH, 3, CONV_W), me * conv_sh, conv_sh, axis=2)
    g_bias = red[25, :DEPTH * 2 * HEADS].reshape(DEPTH, 2 * HEADS)

    small_w = [meta_tokens, norm_mix_w, b_gates, conv_w, mlstm_norm_w, norm_ffn_w, norm_final_w]
    small_m = [m_meta_tokens, m_norm_mix_w, m_b_gates, m_conv_w, m_mlstm_norm_w, m_norm_ffn_w, m_norm_final_w]
    small_v = [v_meta_tokens, v_norm_mix_w, v_b_gates, v_conv_w, v_mlstm_norm_w, v_norm_ffn_w, v_norm_final_w]
    small_g = [g_meta, g_mix, g_bias, g_conv, g_mls, g_ffn, g_final]
    shapes = [a.shape for a in small_w]
    packed = _adam_small("adam_small", _pack128(small_w), _pack128(small_m), _pack128(small_v), _pack128(small_g))
    (d_meta, d_nmix, d_bg, d_cw, d_nmls, d_nffn, d_nfin), (nm_meta, nm_nmix, nm_bg, nm_cw, nm_nmls, nm_nffn, nm_nfin), \
        (nv_meta, nv_nmix, nv_bg, nv_cw, nv_nmls, nv_nffn, nv_nfin) = (_unpack128(p, shapes) for p in packed)

    grad_x = dh[TOK0:].reshape(1, seq, D_MODEL)
    return (loss, grad_x,
            g_meta, g_mix, g_in, g_bias, g_conv, g_mls, g_out, g_ffn, g_gate, g_up, g_down, g_final,
            d_meta, d_nmix, d_in, d_bg, d_cw, d_nmls, d_out, d_nffn, d_gate, d_up, d_down, d_nfin,
            nm_meta, nm_nmix, nm_in, nm_bg, nm_cw, nm_nmls, nm_out, nm_nffn, nm_gate, nm_up, nm_down, nm_nfin,
            nv_meta, nv_nmix, nv_in, nv_bg, nv_cw, nv_nmls, nv_out, nv_nffn, nv_gate, nv_up, nv_down, nv_nfin)
```

```python
import numpy as np
import jax
import jax.numpy as jnp
from jax import lax
from jax.experimental import pallas as pl
from jax.experimental.pallas import tpu as pltpu

F32 = jnp.float32
BF16 = jnp.bfloat16
MESH = pl.DeviceIdType.MESH

D_MODEL = 2048
DEPTH = 2
N_META = 16
MLSTM_W = 1024
CONV_W = 1024
HEADS = 4
DV = 256
DQK = 128
QK_W = 512
CHUNK = 64
PAD_FRONT = 48
TOK0 = PAD_FRONT + N_META
D_FF = 5632
N_DEV = 8
FF_SH = D_FF // N_DEV
D_IN = 6152
IN_SH = D_IN // N_DEV
OUT_SH = D_MODEL // N_DEV
GATE_COL = 3072
PM_W = GATE_COL + 128
GATE_CAP = 15.0
EPS = 1e-6
QSCALE = DQK ** -0.5

ADAM_LR = 0.001
ADAM_B1 = 0.9
ADAM_B2 = 0.999
ADAM_EPS = 1e-08
ADAM_WD = 0.01
ADAM_STEP = 10

V7X_VMEM_LIMIT = 50 * 1024 * 1024
V7X_MXU_COLS = 256


def _params(**kw):
    return pltpu.CompilerParams(vmem_limit_bytes=V7X_VMEM_LIMIT, **kw)


def _tile(n, target, mult):
    best = None
    for t in range(mult, min(n, target) + 1, mult):
        if n % t == 0:
            best = t
    return best if best is not None else n


def _sigmoid(x):
    return 1.0 / (1.0 + jnp.exp(-x))


NN = ((1,), (0,))
NT = ((1,), (1,))
TN = ((0,), (0,))


def _matmul(name, a, b, out_shape, out_dtype, grid, a_bs, b_bs, o_bs, dims, nk, acc_shape=None,
            res=None, res_bs=None, dep=None):
    has_res = res is not None
    n_in = 2 + has_res + (dep is not None)

    def body(*refs):
        a_ref, b_ref = refs[0], refs[1]
        r_ref = refs[2] if has_res else None
        o_ref = refs[n_in]
        x = lax.dot_general(a_ref[...], b_ref[...], (dims, ((), ())), preferred_element_type=F32)
        if nk == 1:
            if has_res:
                x = x + r_ref[...]
            o_ref[...] = x.astype(o_ref.dtype)
            return
        acc = refs[n_in + 1]
        k = pl.program_id(len(grid) - 1)

        @pl.when(k == 0)
        def _():
            acc[...] = (x + r_ref[...]) if has_res else x

        @pl.when(k > 0)
        def _():
            acc[...] += x

        @pl.when(k == nk - 1)
        def _():
            o_ref[...] = acc[...].astype(o_ref.dtype)

    ins = [a, b] + ([res] if has_res else [])
    specs = [a_bs, b_bs] + ([res_bs] if has_res else [])
    if dep is not None:
        ins.append(dep)
        specs.append(pl.BlockSpec((8, 128), lambda *_: (0, 0)))
    scratch = [pltpu.VMEM(acc_shape, F32)] if nk > 1 else []
    return pl.pallas_call(
        body, name=name, grid=grid, in_specs=specs, out_specs=o_bs,
        out_shape=jax.ShapeDtypeStruct(out_shape, out_dtype), scratch_shapes=scratch,
        compiler_params=_params(),
    )(*ins)


def _mm_nn(name, a, b, out_dtype, res=None, tm=1056, tn=512, dep=None):
    r, k = a.shape
    n = b.shape[1]
    tm, tn = _tile(r, tm, 8), _tile(n, tn, 128)
    return _matmul(name, a, b, (r, n), out_dtype, (r // tm, n // tn, 1),
                   pl.BlockSpec((tm, k), lambda i, j, s: (i, 0)),
                   pl.BlockSpec((k, tn), lambda i, j, s: (0, j)),
                   pl.BlockSpec((tm, tn), lambda i, j, s: (i, j)), NN, 1,
                   res=res, res_bs=pl.BlockSpec((tm, tn), lambda i, j, s: (i, j)), dep=dep)


def _mm_nn_kt(name, a, b, out_dtype, tm=1056, tn=1024, tk=640, dep=None):
    r, k = a.shape
    n = b.shape[1]
    tm, tn, tk = _tile(r, tm, 8), _tile(n, tn, 128), _tile(k, tk, 128)
    nk = k // tk
    return _matmul(name, a, b, (r, n), out_dtype, (r // tm, n // tn, nk),
                   pl.BlockSpec((tm, tk), lambda i, j, s: (i, s)),
                   pl.BlockSpec((tk, tn), lambda i, j, s: (s, j)),
                   pl.BlockSpec((tm, tn), lambda i, j, s: (i, j)), NN, nk, acc_shape=(tm, tn), dep=dep)


def _mm_nn_stack(name, a3, b3, res, tm=1056, tn=512):
    e, r, kb = a3.shape
    n = b3.shape[2]
    tm, tn = _tile(r, tm, 8), _tile(n, tn, 128)

    def body(a_ref, b_ref, r_ref, o_ref):
        acc = r_ref[...]
        for s in range(e):
            acc = acc + lax.dot_general(a_ref[s], b_ref[s], (NN, ((), ())), preferred_element_type=F32)
        o_ref[...] = acc

    tile = pl.BlockSpec((tm, tn), lambda i, j: (i, j))
    return pl.pallas_call(
        body, name=name, grid=(r // tm, n // tn),
        in_specs=[pl.BlockSpec((e, tm, kb), lambda i, j: (0, i, 0)),
                  pl.BlockSpec((e, kb, tn), lambda i, j: (0, 0, j)), tile],
        out_specs=tile, out_shape=jax.ShapeDtypeStruct((r, n), F32), compiler_params=_params(),
    )(a3, b3, res)


def _mm_nt(name, a, b, out_dtype, res=None, tm=1056, tn=512, tk=640, n=None, dep=None):
    r, k = a.shape
    n = b.shape[0] if n is None else n
    tm, tn, tk = _tile(r, tm, 8), _tile(n, tn, 128), _tile(k, tk, 128)
    nk = k // tk
    return _matmul(name, a, b, (r, n), out_dtype, (r // tm, n // tn, nk),
                   pl.BlockSpec((tm, tk), lambda i, j, s: (i, s)),
                   pl.BlockSpec((tn, tk), lambda i, j, s: (j, s)),
                   pl.BlockSpec((tm, tn), lambda i, j, s: (i, j)), NT, nk, acc_shape=(tm, tn),
                   res=res, res_bs=pl.BlockSpec((tm, tn), lambda i, j, s: (i, j)), dep=dep)


def _mm_nt_bcols(name, a, b3, out_dtype, tm=1056, dep=None):
    r, k = a.shape
    e, n, _ = b3.shape
    tm = _tile(r, tm, 8)
    return _matmul(name, a, b3, (e, r, n), out_dtype, (r // tm, e, 1),
                   pl.BlockSpec((tm, k), lambda i, g, s: (i, 0)),
                   pl.BlockSpec((None, n, k), lambda i, g, s: (g, 0, 0)),
                   pl.BlockSpec((None, tm, n), lambda i, g, s: (g, i, 0)), NT, 1, dep=dep)


def _mm_tn(name, a, b, out_dtype, tm=1024, tn=640, dep=None):
    r, m = a.shape
    n = b.shape[1]
    tm, tn = _tile(m, tm, 128), _tile(n, tn, 128)
    return _matmul(name, a, b, (m, n), out_dtype, (m // tm, n // tn, 1),
                   pl.BlockSpec((r, tm), lambda i, j, s: (0, i)),
                   pl.BlockSpec((r, tn), lambda i, j, s: (0, j)),
                   pl.BlockSpec((tm, tn), lambda i, j, s: (i, j)), TN, 1, dep=dep)


def _mm_tn_acols(name, a3, b, out_dtype, tn=1024, dep=None):
    e, r, m = a3.shape
    n = b.shape[1]
    tn = _tile(n, tn, 128)
    return _matmul(name, a3, b, (e, m, n), out_dtype, (n // tn, e, 1),
                   pl.BlockSpec((None, r, m), lambda j, g, s: (g, 0, 0)),
                   pl.BlockSpec((r, tn), lambda j, g, s: (0, j)),
                   pl.BlockSpec((None, m, tn), lambda j, g, s: (g, 0, j)), TN, 1, dep=dep)


def _norm_proj(name, h, w, b, n, tm=1056, tn=640):
    r, d = h.shape
    tm, tn = _tile(r, tm, 8), _tile(n, tn, 128)

    def body(h_ref, w_ref, b_ref, hn_ref, o_ref):
        @pl.when(pl.program_id(1) == 0)
        def _():
            x = h_ref[...]
            rs = lax.rsqrt(jnp.mean(x * x, axis=1, keepdims=True) + EPS)
            hn_ref[...] = (x * rs * w_ref[...]).astype(BF16)

        o_ref[...] = lax.dot_general(hn_ref[...], b_ref[...], (NT, ((), ())), preferred_element_type=F32)

    row = pl.BlockSpec((tm, d), lambda i, j: (i, 0))
    return pl.pallas_call(
        body, name=name, grid=(r // tm, n // tn),
        in_specs=[row, pl.BlockSpec((1, d), lambda i, j: (0, 0)), pl.BlockSpec((tn, d), lambda i, j: (j, 0))],
        out_specs=[row, pl.BlockSpec((tm, tn), lambda i, j: (i, j))],
        out_shape=[jax.ShapeDtypeStruct((r, d), BF16), jax.ShapeDtypeStruct((r, n), F32)],
        compiler_params=_params(),
    )(h, w, b)


def _proj_res_norm(name, a, b, res, w, tm=528):
    r, k = a.shape
    d = b.shape[1]
    tm = _tile(r, tm, 8)

    def body(a_ref, b_ref, r_ref, w_ref, y_ref, n_ref):
        y = lax.dot_general(a_ref[...], b_ref[...], (NN, ((), ())), preferred_element_type=F32) + r_ref[...]
        y_ref[...] = y
        rs = lax.rsqrt(jnp.mean(y * y, axis=1, keepdims=True) + EPS)
        n_ref[...] = (y * rs * w_ref[...]).astype(BF16)

    row = pl.BlockSpec((tm, d), lambda i: (i, 0))
    return pl.pallas_call(
        body, name=name, grid=(r // tm,),
        in_specs=[pl.BlockSpec((tm, k), lambda i: (i, 0)), pl.BlockSpec((k, d), lambda i: (0, 0)), row,
                  pl.BlockSpec((1, d), lambda i: (0, 0))],
        out_specs=[row, row],
        out_shape=[jax.ShapeDtypeStruct((r, d), F32), jax.ShapeDtypeStruct((r, d), BF16)],
        compiler_params=_params(),
    )(a, b, res, w)


def _rms_bwd(name, x, w, dy, dres):
    r, d = x.shape
    tr = _tile(r, 264, 8)

    def body(x_ref, w_ref, dy_ref, dr_ref, dx_ref, dxb_ref, dw_ref):
        xv = x_ref[...]
        g = dy_ref[...]
        rs = lax.rsqrt(jnp.mean(xv * xv, axis=1, keepdims=True) + EPS)
        wg = g * w_ref[...]
        dx = rs * wg - xv * (rs * rs * rs) * jnp.mean(xv * wg, axis=1, keepdims=True) + dr_ref[...]
        dx_ref[...] = dx
        dxb_ref[...] = dx.astype(BF16)
        part = jnp.sum(g * xv * rs, axis=0, keepdims=True)

        @pl.when(pl.program_id(0) == 0)
        def _():
            dw_ref[...] = part

        @pl.when(pl.program_id(0) > 0)
        def _():
            dw_ref[...] += part

    row = pl.BlockSpec((tr, d), lambda i: (i, 0))
    vec = pl.BlockSpec((1, d), lambda i: (0, 0))
    return pl.pallas_call(
        body, name=name, grid=(r // tr,), in_specs=[row, vec, row, row], out_specs=[row, row, vec],
        out_shape=[jax.ShapeDtypeStruct((r, d), F32), jax.ShapeDtypeStruct((r, d), BF16),
                   jax.ShapeDtypeStruct((1, d), F32)],
        compiler_params=_params(),
    )(x, w, dy, dres)


def _final_loss(name, h, w, target):
    r, d = h.shape
    nb = r // CHUNK

    def body(h_ref, w_ref, t_ref, dh_ref, dhb_ref, dw_ref, ls_ref):
        i = pl.program_id(0)

        @pl.when(i == 0)
        def _():
            dh_ref[...] = jnp.zeros_like(dh_ref)
            dhb_ref[...] = jnp.zeros_like(dhb_ref)
            dw_ref[...] = jnp.zeros_like(dw_ref)
            ls_ref[...] = jnp.zeros_like(ls_ref)

        @pl.when(i > 0)
        def _():
            xv = h_ref[...]
            wv = w_ref[...]
            rs = lax.rsqrt(jnp.mean(xv * xv, axis=1, keepdims=True) + EPS)
            err = xv * rs * wv - t_ref[...]
            sq = jnp.sum(jnp.sum(err * err, axis=1, keepdims=True), axis=0, keepdims=True)
            ls_ref[...] += jnp.broadcast_to(sq * (0.5 / d), ls_ref.shape)
            g = err * (1.0 / d)
            wg = g * wv
            dx = rs * wg - xv * (rs * rs * rs) * jnp.mean(xv * wg, axis=1, keepdims=True)
            dh_ref[...] = dx
            dhb_ref[...] = dx.astype(BF16)
            dw_ref[...] += jnp.sum(g * xv * rs, axis=0, keepdims=True)

    row = pl.BlockSpec((CHUNK, d), lambda i: (i, 0))
    vec = pl.BlockSpec((1, d), lambda i: (0, 0))
    return pl.pallas_call(
        body, name=name, grid=(nb,),
        in_specs=[row, vec, pl.BlockSpec((CHUNK, d), lambda i: (jnp.maximum(i - 1, 0), 0))],
        out_specs=[row, row, vec, pl.BlockSpec((1, 128), lambda i: (0, 0))],
        out_shape=[jax.ShapeDtypeStruct((r, d), F32), jax.ShapeDtypeStruct((r, d), BF16),
                   jax.ShapeDtypeStruct((1, d), F32), jax.ShapeDtypeStruct((1, 128), F32)],
        compiler_params=_params(),
    )(h, w, target)


def _ffn_in(name, hf, wg_t, wu_t, dep=None, tm=1056, tn=512):
    r, d = hf.shape
    f = wg_t.shape[0]
    tm, tn = _tile(r, tm, 8), _tile(f, tn, 128)

    def body(h_ref, wg_ref, wu_ref, *rest):
        g_ref, u_ref, a_ref = rest[-3:]
        x = h_ref[...]
        g = lax.dot_general(x, wg_ref[...], (NT, ((), ())), preferred_element_type=F32)
        u = lax.dot_general(x, wu_ref[...], (NT, ((), ())), preferred_element_type=F32)
        g_ref[...] = g.astype(BF16)
        u_ref[...] = u.astype(BF16)
        a_ref[...] = (g * _sigmoid(g) * u).astype(BF16)

    wspec = pl.BlockSpec((tn, d), lambda i, j: (j, 0))
    ospec = pl.BlockSpec((tm, tn), lambda i, j: (i, j))
    ins, specs = [hf, wg_t, wu_t], [pl.BlockSpec((tm, d), lambda i, j: (i, 0)), wspec, wspec]
    if dep is not None:
        ins.append(dep)
        specs.append(pl.BlockSpec((8, 128), lambda *_: (0, 0)))
    return pl.pallas_call(
        body, name=name, grid=(r // tm, f // tn), in_specs=specs, out_specs=[ospec] * 3,
        out_shape=[jax.ShapeDtypeStruct((r, f), BF16)] * 3, compiler_params=_params(),
    )(*ins)


def _ffn_act_bwd(name, dh, wd, g, u, dep=None, tm=1056, tn=512):
    r, d = dh.shape
    f = wd.shape[0]
    tm, tn = _tile(r, tm, 8), _tile(f, tn, 128)

    def body(dh_ref, wd_ref, g_ref, u_ref, *rest):
        dg_ref, du_ref = rest[-2:]
        tr = _tile(tm, 264, 8)
        for r0 in range(0, tm, tr):
            for c0 in range(0, tn, V7X_MXU_COLS):
                rows, cols = slice(r0, r0 + tr), slice(c0, c0 + V7X_MXU_COLS)
                da = lax.dot_general(dh_ref[rows, :], wd_ref[cols, :], (NT, ((), ())), preferred_element_type=F32)
                gv = g_ref[rows, cols].astype(F32)
                s = _sigmoid(gv)
                t = da * s
                du_ref[rows, cols] = (t * gv).astype(BF16)
                dg_ref[rows, cols] = (t * u_ref[rows, cols].astype(F32) * (1.0 + gv - gv * s)).astype(BF16)

    tile = pl.BlockSpec((tm, tn), lambda i, j: (i, j))
    ins = [dh, wd, g, u]
    specs = [pl.BlockSpec((tm, d), lambda i, j: (i, 0)), pl.BlockSpec((tn, d), lambda i, j: (j, 0)), tile, tile]
    if dep is not None:
        ins.append(dep)
        specs.append(pl.BlockSpec((8, 128), lambda *_: (0, 0)))
    return pl.pallas_call(
        body, name=name, grid=(r // tm, f // tn), in_specs=specs, out_specs=[tile] * 2,
        out_shape=[jax.ShapeDtypeStruct((r, f), BF16)] * 2, compiler_params=_params(),
    )(*ins)


def _shift_down(a, k):
    row = lax.broadcasted_iota(jnp.int32, a.shape, 0)
    return jnp.where(row >= k, pltpu.roll(a, k, 0), 0.0)


def _shift_up(a, k):
    n = a.shape[0]
    row = lax.broadcasted_iota(jnp.int32, a.shape, 0)
    return jnp.where(row < n - k, pltpu.roll(a, n - k, 0), 0.0)


def _conv_fwd(name, pc, cw, cat):
    _, r, w = pc.shape
    nblk = w // 128

    def body(pc_ref, cw_ref, cat_ref, o_ref):
        a = pc_ref[2] * pc_ref[0]
        cwv = cw_ref[...]
        conv = _shift_down(a, 2) * cwv[0:1] + _shift_down(a, 1) * cwv[1:2] + a * cwv[2:3]
        o_ref[...] = (pc_ref[1] * conv).astype(BF16)

    return pl.pallas_call(
        body, name=name, grid=(nblk,),
        in_specs=[pl.BlockSpec((3, r, 128), lambda j: (0, 0, j)), pl.BlockSpec((8, 128), lambda j: (0, j)),
                  pl.BlockSpec(memory_space=pl.ANY)],
        out_specs=pl.BlockSpec((r, 128), lambda j: (0, nblk + j)),
        out_shape=jax.ShapeDtypeStruct(cat.shape, BF16), input_output_aliases={2: 0},
        compiler_params=_params(),
    )(pc, cw, cat)


def _conv_bwd(name, dcat, pc, cw):
    _, r, w = pc.shape
    nblk = w // 128

    def body(dy_ref, pc_ref, cw_ref, dpc_ref, dcw_ref):
        u, gb, gc = pc_ref[0], pc_ref[1], pc_ref[2]
        cwv = cw_ref[...]
        dy = dy_ref[...]
        a = gc * u
        a1, a2 = _shift_down(a, 1), _shift_down(a, 2)
        conv = a2 * cwv[0:1] + a1 * cwv[1:2] + a * cwv[2:3]
        dconv = dy * gb
        da = dconv * cwv[2:3] + _shift_up(dconv, 1) * cwv[1:2] + _shift_up(dconv, 2) * cwv[0:1]
        dpc_ref[0] = (da * gc).astype(BF16)
        dpc_ref[1] = (dy * conv).astype(BF16)
        dpc_ref[2] = (da * u).astype(BF16)
        row = lax.broadcasted_iota(jnp.int32, (8, 128), 0)
        dw0 = jnp.sum(dconv * a2, axis=0, keepdims=True)
        dw1 = jnp.sum(dconv * a1, axis=0, keepdims=True)
        dw2 = jnp.sum(dconv * a, axis=0, keepdims=True)
        dcw_ref[...] = jnp.where(row == 0, dw0, jnp.where(row == 1, dw1, jnp.where(row == 2, dw2, 0.0)))

    return pl.pallas_call(
        body, name=name, grid=(nblk,),
        in_specs=[pl.BlockSpec((r, 128), lambda j: (0, nblk + j)),
                  pl.BlockSpec((3, r, 128), lambda j: (0, 0, j)), pl.BlockSpec((8, 128), lambda j: (0, j))],
        out_specs=[pl.BlockSpec((3, r, 128), lambda j: (0, 0, j)), pl.BlockSpec((8, 128), lambda j: (0, j))],
        out_shape=[jax.ShapeDtypeStruct((3, r, w), BF16), jax.ShapeDtypeStruct((8, w), F32)],
        compiler_params=_params(),
    )(dcat, pc, cw)


def _dot(a, b, dims):
    return lax.dot_general(a, b, (dims, ((), ())), preferred_element_type=F32)


def _col_to_row(xc, eye):
    return jnp.sum(jnp.where(eye, xc, 0.0), axis=0, keepdims=True)


def _row_to_col(xr, eye):
    return jnp.sum(jnp.where(eye, xr, 0.0), axis=1, keepdims=True)


def _gate_tiles(graw, bias, row0):
    th = jnp.tanh((graw + bias) / GATE_CAP)
    z = GATE_CAP * th
    row = lax.broadcasted_iota(jnp.int32, graw.shape, 0) + row0
    real = row >= PAD_FRONT
    li = jnp.where(real, z, -jnp.inf)
    lf = jnp.where(real, jnp.minimum(z, 0.0) - jnp.log(1.0 + jnp.exp(-jnp.abs(z))), 0.0)
    return th, z, li, lf, real


def _interleave(gens):
    results = [None] * len(gens)
    live = list(enumerate(gens))
    while live:
        still = []
        for i, gen in live:
            try:
                next(gen)
                still.append((i, gen))
            except StopIteration as stop:
                results[i] = stop.value
        live = still
    return results


def _chunk_common(pm, h, li, lf, cst, nst, mst, tril, eye):
    kraw = pm[:, QK_W + h * DQK:QK_W + (h + 1) * DQK]
    q = (pm[:, h * DQK:(h + 1) * DQK] * QSCALE).astype(BF16)
    yield
    k = kraw.astype(BF16)
    v = pm[:, 2 * QK_W + h * DV:2 * QK_W + (h + 1) * DV].astype(BF16)
    yield
    li_c = li[:, h:h + 1]
    lf_c = lf[:, HEADS + h:HEADS + h + 1]
    li_r = _col_to_row(li_c, eye)
    yield
    lf_r = _col_to_row(lf_c, eye)
    yield
    b_c = jnp.sum(jnp.where(tril, lf_r, 0.0), axis=1, keepdims=True)
    yield
    b_r = _col_to_row(b_c, eye)
    yield
    dmat = jnp.where(tril, b_c - b_r + li_r, -jnp.inf)
    inter = b_c + mst
    yield
    mt = jnp.maximum(inter, jnp.max(dmat, axis=1, keepdims=True))
    yield
    w_inter = jnp.exp(inter - mt)
    p = jnp.exp(dmat - mt)
    yield
    s = _dot(q, k, NT) * p
    yield
    cb = cst.astype(BF16)
    nb = nst.astype(BF16).astype(F32)
    qc = _dot(q, cb, NN)
    yield
    qn = jnp.sum(q.astype(F32) * nb, axis=1, keepdims=True)
    yield
    den = w_inter * qn + jnp.sum(s, axis=1, keepdims=True)
    yield
    dn = jnp.maximum(jnp.abs(den), jnp.exp(-mt))
    b_end = b_c[CHUNK - 1:CHUNK, :]
    decay = b_end - b_c + li_c
    yield
    m_new = jnp.maximum(b_end + mst, jnp.max(decay, axis=0, keepdims=True))
    yield
    w_old = jnp.exp(b_end + mst - m_new)
    w_in = jnp.exp(decay - m_new)
    kw = (w_in * kraw).astype(BF16)
    yield
    return dict(q=q, k=k, v=v, kraw=kraw, mt=mt, w_inter=w_inter, p=p, s=s, cb=cb, nb=nb, qc=qc, qn=qn,
                den=den, dn=dn, m_new=m_new, w_old=w_old, w_in=w_in, kw=kw)


def _chunks_per_step(nc):
    return 1


def _mlstm_fwd(name, pm, bias, nw):
    r = pm.shape[0]
    nc = r // CHUNK
    grp = _chunks_per_step(nc)

    def body(pm_ref, b_ref, nw_ref, hm_ref, ht_ref, cs_ref, ns_ref, ms_ref, c_scr, n_scr, m_scr):
        step = pl.program_id(0)

        @pl.when(step == 0)
        def _():
            c_scr[...] = jnp.zeros_like(c_scr)
            n_scr[...] = jnp.zeros_like(n_scr)
            m_scr[...] = jnp.zeros_like(m_scr)

        rr = lax.broadcasted_iota(jnp.int32, (CHUNK, CHUNK), 0)
        cc = lax.broadcasted_iota(jnp.int32, (CHUNK, CHUNK), 1)
        tril, eye = cc <= rr, cc == rr
        bv, nwv = b_ref[...], nw_ref[...]
        states = [(c_scr[h], n_scr[h], m_scr[h]) for h in range(HEADS)]
        for g in range(grp):
            rows = slice(g * CHUNK, (g + 1) * CHUNK)
            pmv = pm_ref[rows, :]
            _, _, li, lf, _ = _gate_tiles(pmv[:, GATE_COL:GATE_COL + 128], bv, (step * grp + g) * CHUNK)
            def head(h, cst, nst, mst, g=g, rows=rows, pmv=pmv, li=li, lf=lf):
                f = yield from _chunk_common(pmv, h, li, lf, cst, nst, mst, tril, eye)
                num = f["w_inter"] * f["qc"] + _dot(f["s"].astype(BF16), f["v"], NN)
                yield
                hh = num / f["dn"]
                yield
                c_new = f["w_old"] * cst + _dot(f["kw"], f["v"], TN)
                yield
                n_new = f["w_old"] * nst + jnp.sum(
                    f["w_in"].astype(BF16).astype(F32) * f["k"].astype(F32), axis=0, keepdims=True)
                yield
                sl = slice(h * DV, (h + 1) * DV)
                rs = lax.rsqrt(jnp.mean(hh * hh, axis=1, keepdims=True) + EPS)
                yield
                og = pmv[:, 2 * QK_W + MLSTM_W + h * DV:2 * QK_W + MLSTM_W + (h + 1) * DV]
                cs_ref[g, h] = cst
                ns_ref[g, h] = nst
                ms_ref[g, h] = mst
                ht_ref[rows, sl] = hh
                yield
                hm_ref[rows, sl] = (_sigmoid(og) * (hh * rs * nwv[:, sl])).astype(BF16)
                return c_new, n_new, f["m_new"]

            states = _interleave([head(h, *states[h]) for h in range(HEADS)])
        for h, (cst, nst, mst) in enumerate(states):
            c_scr[h] = cst
            n_scr[h] = nst
            m_scr[h] = mst

    return pl.pallas_call(
        body, name=name, grid=(nc // grp,),
        in_specs=[pl.BlockSpec((grp * CHUNK, PM_W), lambda i: (i, 0)), pl.BlockSpec((1, 128), lambda i: (0, 0)),
                  pl.BlockSpec((1, MLSTM_W), lambda i: (0, 0))],
        out_specs=[pl.BlockSpec((grp * CHUNK, MLSTM_W), lambda i: (i, 0)),
                   pl.BlockSpec((grp * CHUNK, MLSTM_W), lambda i: (i, 0)),
                   pl.BlockSpec((grp, HEADS, DQK, DV), lambda i: (i, 0, 0, 0)),
                   pl.BlockSpec((grp, HEADS, 1, DQK), lambda i: (i, 0, 0, 0)),
                   pl.BlockSpec((grp, HEADS, 1, 1), lambda i: (i, 0, 0, 0))],
        out_shape=[jax.ShapeDtypeStruct((r, MLSTM_W + CONV_W), BF16), jax.ShapeDtypeStruct((r, MLSTM_W), F32),
                   jax.ShapeDtypeStruct((nc, HEADS, DQK, DV), F32),
                   jax.ShapeDtypeStruct((nc, HEADS, 1, DQK), F32),
                   jax.ShapeDtypeStruct((nc, HEADS, 1, 1), F32)],
        scratch_shapes=[pltpu.VMEM((HEADS, DQK, DV), F32), pltpu.VMEM((HEADS, 1, DQK), F32),
                        pltpu.VMEM((HEADS, 1, 1), F32)],
        compiler_params=_params(),
    )(pm, bias, nw)


def _mlstm_bwd(name, dcat, pm, ht, cs, ns, ms, bias, nw):
    r = pm.shape[0]
    nc = r // CHUNK
    grp = _chunks_per_step(nc)
    nsteps = nc // grp

    def body(dy_ref, pm_ref, ht_ref, cs_ref, ns_ref, ms_ref, b_ref, nw_ref, dpm_ref, dnw_ref, db_ref,
             dc_scr, dn_scr):
        step = pl.program_id(0)

        @pl.when(step == 0)
        def _():
            dc_scr[...] = jnp.zeros_like(dc_scr)
            dn_scr[...] = jnp.zeros_like(dn_scr)
            dnw_ref[...] = jnp.zeros_like(dnw_ref)
            db_ref[...] = jnp.zeros_like(db_ref)

        rr = lax.broadcasted_iota(jnp.int32, (CHUNK, CHUNK), 0)
        cc = lax.broadcasted_iota(jnp.int32, (CHUNK, CHUNK), 1)
        tril, eye, triu = cc <= rr, cc == rr, cc >= rr
        lane = lax.broadcasted_iota(jnp.int32, (CHUNK, 128), 1)
        rowid = lax.broadcasted_iota(jnp.int32, (CHUNK, 1), 0)
        bv, nwv = b_ref[...], nw_ref[...]
        carried = [(dc_scr[h], dn_scr[h]) for h in range(HEADS)]
        dnw_acc = [jnp.zeros((1, DV), F32) for _ in range(HEADS)]
        db_acc = jnp.zeros((1, 128), F32)
        for g in reversed(range(grp)):
            rows = slice(g * CHUNK, (g + 1) * CHUNK)
            ci = (nsteps - 1 - step) * grp + g
            pmv = pm_ref[rows, :]
            th, z, li, lf, real = _gate_tiles(pmv[:, GATE_COL:GATE_COL + 128], bv, ci * CHUNK)
            heads = _interleave([
                _mlstm_bwd_head(h, pmv, ht_ref[rows, h * DV:(h + 1) * DV], dy_ref[rows, h * DV:(h + 1) * DV], nwv,
                                li, lf, cs_ref[g, h], ns_ref[g, h], ms_ref[g, h], carried[h][0], carried[h][1],
                                tril, eye, triu, lane, rowid, dpm_ref, rows)
                for h in range(HEADS)])
            carried = [(dc_new, dn_new) for _, dc_new, dn_new, _ in heads]
            dgt = heads[0][0] + heads[1][0] + heads[2][0] + heads[3][0]
            dnw_acc = [dnw_acc[h] + heads[h][3] for h in range(HEADS)]
            dact = jnp.where(lane < HEADS, 1.0, 1.0 - _sigmoid(z)) * (1.0 - th * th)
            dgraw = jnp.where(real & (lane < 2 * HEADS), dgt * dact, 0.0)
            dpm_ref[rows, GATE_COL:GATE_COL + 128] = dgraw.astype(BF16)
            db_acc = db_acc + jnp.sum(dgraw, axis=0, keepdims=True)
        for h, (dcn, dnn) in enumerate(carried):
            dc_scr[h] = dcn
            dn_scr[h] = dnn
            dnw_ref[:, h * DV:(h + 1) * DV] += dnw_acc[h]
        db_ref[...] += db_acc

    rev = lambda i: (nsteps - 1 - i, 0)
    rev4 = lambda i: (nsteps - 1 - i, 0, 0, 0)
    return pl.pallas_call(
        body, name=name, grid=(nsteps,),
        in_specs=[pl.BlockSpec((grp * CHUNK, MLSTM_W), rev), pl.BlockSpec((grp * CHUNK, PM_W), rev),
                  pl.BlockSpec((grp * CHUNK, MLSTM_W), rev),
                  pl.BlockSpec((grp, HEADS, DQK, DV), rev4), pl.BlockSpec((grp, HEADS, 1, DQK), rev4),
                  pl.BlockSpec((grp, HEADS, 1, 1), rev4),
                  pl.BlockSpec((1, 128), lambda i: (0, 0)), pl.BlockSpec((1, MLSTM_W), lambda i: (0, 0))],
        out_specs=[pl.BlockSpec((grp * CHUNK, PM_W), rev), pl.BlockSpec((1, MLSTM_W), lambda i: (0, 0)),
                   pl.BlockSpec((1, 128), lambda i: (0, 0))],
        out_shape=[jax.ShapeDtypeStruct((r, PM_W), BF16), jax.ShapeDtypeStruct((1, MLSTM_W), F32),
                   jax.ShapeDtypeStruct((1, 128), F32)],
        scratch_shapes=[pltpu.VMEM((HEADS, DQK, DV), F32), pltpu.VMEM((HEADS, 1, DQK), F32)],
        compiler_params=_params(),
    )(dcat, pm, ht, cs, ns, ms, bias, nw)


def _mlstm_bwd_head(h, pmv, hh, y, nwv, li, lf, cst, nst, mst, dcn, dnn, tril, eye, triu, lane, rowid,
                    dpm_ref, rows):
    f = yield from _chunk_common(pmv, h, li, lf, cst, nst, mst, tril, eye)
    q, k, v, s, p = f["q"], f["k"], f["v"], f["s"], f["p"]
    w_inter, w_in, w_old, dn = f["w_inter"], f["w_in"], f["w_old"], f["dn"]
    osl = slice(2 * QK_W + MLSTM_W + h * DV, 2 * QK_W + MLSTM_W + (h + 1) * DV)
    sg = _sigmoid(pmv[:, osl])
    yield
    rs = lax.rsqrt(jnp.mean(hh * hh, axis=1, keepdims=True) + EPS)
    yield
    nwh = nwv[:, h * DV:(h + 1) * DV]
    dpm_ref[rows, osl] = (y * (hh * rs * nwh) * sg * (1.0 - sg)).astype(BF16)
    yield
    dhn = y * sg
    dnw_h = jnp.sum(dhn * hh * rs, axis=0, keepdims=True)
    yield
    wd = dhn * nwh
    dhh = rs * wd - hh * (rs * rs * rs) * jnp.mean(hh * wd, axis=1, keepdims=True)
    yield
    dnum = dhh / dn
    dd = -jnp.sum(dhh * hh, axis=1, keepdims=True) / dn
    yield
    dden = jnp.where(jnp.abs(f["den"]) > jnp.exp(-f["mt"]), dd * jnp.sign(f["den"]), 0.0)
    dnum_b = dnum.astype(BF16)
    wdn = (w_inter * dnum).astype(BF16)
    wid = (w_inter * dden).astype(BF16).astype(F32)
    yield
    ds = _dot(dnum_b, v, NT) + dden
    yield
    dsp = (ds * p).astype(BF16)
    yield
    dq = _dot(dsp, k, NN) + _dot(wdn, f["cb"], NT) + wid * f["nb"]
    yield
    dk = _dot(dsp, q, TN)
    yield
    dv = _dot(s.astype(BF16), dnum_b, TN)
    yield
    g = ds * s
    g_col = _row_to_col(jnp.sum(g, axis=0, keepdims=True), eye)
    yield
    db = jnp.sum(g, axis=1, keepdims=True) - g_col
    dli = g_col
    yield
    db = db + (jnp.sum(dnum * f["qc"], axis=1, keepdims=True) + dden * f["qn"]) * w_inter
    yield
    dcnb = dcn.astype(BF16)
    dnnb = dnn.astype(BF16).astype(F32)
    dkw = _dot(v, dcnb, NT) + dnnb
    yield
    dk = dk + w_in * dkw
    dv = dv + _dot(f["kw"], dcnb, NN)
    yield
    ddecay = jnp.sum(dkw * f["kraw"], axis=1, keepdims=True) * w_in
    yield
    dw_old = (jnp.sum(jnp.sum(dcn * cst, axis=1, keepdims=True), axis=0, keepdims=True)
              + jnp.sum(dnn * nst, axis=1, keepdims=True))
    yield
    db_end = dw_old * w_old + jnp.sum(ddecay, axis=0, keepdims=True)
    db = db - ddecay + jnp.where(rowid == CHUNK - 1, db_end, 0.0)
    dli = dli + ddecay
    yield
    dc_new = w_old * dcn + _dot(q, wdn, TN)
    yield
    dn_new = w_old * dnn + jnp.sum(wid * q.astype(F32), axis=0, keepdims=True)
    yield
    dlf = jnp.sum(jnp.where(triu, _col_to_row(db, eye), 0.0), axis=1, keepdims=True)
    yield
    gate_part = jnp.where(lane == h, dli, 0.0) + jnp.where(lane == HEADS + h, dlf, 0.0)
    dpm_ref[rows, h * DQK:(h + 1) * DQK] = (dq * QSCALE).astype(BF16)
    yield
    dpm_ref[rows, QK_W + h * DQK:QK_W + (h + 1) * DQK] = dk.astype(BF16)
    yield
    dpm_ref[rows, 2 * QK_W + h * DV:2 * QK_W + (h + 1) * DV] = dv.astype(BF16)
    return gate_part, dc_new, dn_new, dnw_h


def _my_place():
    return lax.axis_index("x"), lax.axis_index("y"), lax.axis_index("c")


def _flip(v, bit):
    return 1 - v if bit else v


def _exchange_small(name, blk, reduce):
    r, c = blk.shape

    def body(x_ref, o_ref, *rest):
        slots = rest[0] if reduce else o_ref
        send_sems, recv_sems = rest[-2], rest[-1]
        x, y, cc = _my_place()
        me = 4 * x + 2 * y + cc
        slots[me] = x_ref[...]
        copies = []
        for k in range(1, N_DEV):
            peer = (_flip(x, k & 4), _flip(y, k & 2), _flip(cc, k & 1))
            cp = pltpu.make_async_remote_copy(
                src_ref=x_ref, dst_ref=slots.at[me], send_sem=send_sems.at[k - 1],
                recv_sem=recv_sems.at[k - 1], device_id=peer, device_id_type=MESH)
            cp.start()
            copies.append(cp)
        for cp in copies:
            cp.wait()
        if reduce:
            acc = slots[0]
            for d in range(1, N_DEV):
                acc = acc + slots[d]
            o_ref[...] = acc

    scratch = ([pltpu.VMEM((N_DEV, r, c), F32)] if reduce else []) + [
        pltpu.SemaphoreType.DMA((N_DEV - 1,)), pltpu.SemaphoreType.DMA((N_DEV - 1,))]
    return pl.pallas_call(
        body, name=name,
        out_shape=jax.ShapeDtypeStruct((r, c) if reduce else (N_DEV, r, c), F32),
        in_specs=[pl.BlockSpec(memory_space=pltpu.VMEM)], out_specs=pl.BlockSpec(memory_space=pltpu.VMEM),
        scratch_shapes=scratch, compiler_params=_params(),
    )(blk)


HBM_SPEC = pl.BlockSpec(memory_space=pltpu.HBM)
SEM_SPEC = pl.BlockSpec(memory_space=pltpu.SEMAPHORE)
ANY_SPEC = pl.BlockSpec(memory_space=pl.ANY)
DATAFLOW = pltpu.SideEffectType.DATAFLOW_SIDE_EFFECTING


def _split_copy(name, arrays, start=None, wait=None, after=None):
    results, token = _split_copies(name, [(arrays, start, wait)], after)
    return results[0][0], results[0][1], token


def _split_copies(name, jobs, after=None):
    operands, in_specs, out_shape, out_specs, aliases = [], [], [], [], {}
    in_at, out_at = [], []
    for arrays, start, wait in jobs:
        in_at.append(len(operands))
        operands += [pltpu.with_memory_space_constraint(a, pltpu.HBM) for a in arrays]
        in_specs += [HBM_SPEC] * len(arrays)
        if wait:
            operands += list(wait[1])
            in_specs += [SEM_SPEC, SEM_SPEC]
    if after is not None:
        operands.append(after)
        in_specs.append(ANY_SPEC)
    for j, (arrays, start, wait) in enumerate(jobs):
        out_at.append(len(out_shape))
        if start:
            out_shape += [pltpu.SemaphoreType.DMA((start[1],)), pltpu.SemaphoreType.DMA((start[1],))]
            out_specs += [SEM_SPEC, SEM_SPEC]
        for i, a in enumerate(arrays):
            aliases[in_at[j] + i] = len(out_shape)
            out_shape.append(pltpu.HBM(a.shape, a.dtype))
            out_specs.append(HBM_SPEC)
    any_start = any(start for _, start, _ in jobs)
    if any_start:
        out_shape.append(jax.ShapeDtypeStruct((8, 128), F32))
        out_specs.append(pl.BlockSpec(memory_space=pltpu.VMEM))
    n_in = len(operands)

    def body(*refs):
        for j, (arrays, start, wait) in enumerate(jobs):
            if wait:
                ins = refs[in_at[j]:in_at[j] + len(arrays)]
                at = in_at[j] + len(arrays)
                for cp in wait[0](ins, refs[at], refs[at + 1]):
                    cp.wait_send()
                    cp.wait_recv()
        for j, (arrays, start, wait) in enumerate(jobs):
            if start:
                ins = refs[in_at[j]:in_at[j] + len(arrays)]
                at = n_in + out_at[j]
                for cp in start[0](ins, refs[at], refs[at + 1]):
                    cp.start()
        if any_start:
            token = refs[n_in + len(out_shape) - 1]
            token[...] = jnp.zeros_like(token)

    outs = pl.pallas_call(
        body, name=name, in_specs=in_specs, out_specs=out_specs, out_shape=out_shape,
        input_output_aliases=aliases, compiler_params=pltpu.CompilerParams(has_side_effects=DATAFLOW),
    )(*operands)
    results = []
    for j, (arrays, start, wait) in enumerate(jobs):
        at = out_at[j]
        sems = (outs[at], outs[at + 1]) if start else None
        at += 2 if start else 0
        results.append((list(outs[at:at + len(arrays)]), sems))
    return results, (outs[-1] if any_start else None)


def _remote(src, dst, send_sems, recv_sems, k, to):
    return pltpu.make_async_remote_copy(src_ref=src, dst_ref=dst, send_sem=send_sems.at[k],
                                        recv_sem=recv_sems.at[k], device_id=to, device_id_type=MESH)


def _slot(px, py, pc):
    return 4 * px + 2 * py + pc


def _gather_first(refs, send_sems, recv_sems):
    x, y, c = _my_place()
    blk = refs[0].at[_slot(x, y, c)]
    targets = [(x, y, 1 - c), (1 - x, y, c), (x, 1 - y, c)]
    return [_remote(blk, blk, send_sems, recv_sems, k, to) for k, to in enumerate(targets)]


def _gather_relay(refs, send_sems, recv_sems):
    x, y, c = _my_place()
    rows = refs[0].shape[1]
    half = rows // 32 * 16
    from_x, from_y = _slot(1 - x, y, c), _slot(x, 1 - y, c)
    upper = refs[0].at[from_x, pl.ds(0, half)]
    lower = refs[0].at[from_y, pl.ds(half, rows - half)]
    return [_remote(upper, upper, send_sems, recv_sems, 0, (x, 1 - y, c)),
            _remote(lower, lower, send_sems, recv_sems, 1, (1 - x, y, c)),
            _remote(refs[0].at[from_x], refs[0].at[from_x], send_sems, recv_sems, 2, (x, y, 1 - c)),
            _remote(refs[0].at[from_y], refs[0].at[from_y], send_sems, recv_sems, 3, (x, y, 1 - c))]


def _gather_last(refs, send_sems, recv_sems):
    x, y, c = _my_place()
    blk = refs[0].at[_slot(1 - x, 1 - y, c)]
    return [_remote(blk, blk, send_sems, recv_sems, 0, (x, y, 1 - c))]


def _scatter_sibling(n):
    def copies(refs, send_sems, recv_sems):
        x, y, c = _my_place()
        return [_remote(refs[a].at[2 * j + 1 - c], refs[n + a].at[j], send_sems, recv_sems, 4 * a + j, (x, y, 1 - c))
                for a in range(n) for j in range(4)]
    return copies


def _scatter_chips(n):
    def copies(refs, send_sems, recv_sems):
        x, y, c = _my_place()
        out = []
        for a in range(n):
            for k in range(1, 4):
                px, py = _flip(x, k & 2), _flip(y, k & 1)
                out.append(_remote(refs[a].at[2 * px + py], refs[n + a].at[2 * x + y], send_sems, recv_sems,
                                   3 * a + k - 1, (px, py, c)))
        return out
    return copies


def _pair_sum(name, core, g, t):
    _, r, c = g.shape
    tr = _tile(r, 512, 8)
    g4 = g.reshape(4, 2, r, c)

    def body(core_ref, g_ref, t_ref, o_ref):
        o_ref[...] = (g_ref[...].astype(F32) + t_ref[...].astype(F32)).astype(BF16)

    return pl.pallas_call(
        body, name=name,
        grid_spec=pltpu.PrefetchScalarGridSpec(
            num_scalar_prefetch=1, grid=(4, r // tr),
            in_specs=[pl.BlockSpec((None, None, tr, c), lambda j, i, core_ref: (j, core_ref[0], i, 0)),
                      pl.BlockSpec((None, tr, c), lambda j, i, core_ref: (j, i, 0))],
            out_specs=pl.BlockSpec((None, tr, c), lambda j, i, core_ref: (j, i, 0))),
        out_shape=jax.ShapeDtypeStruct((4, r, c), BF16), compiler_params=_params(),
    )(core, g4, t)


def _adam_math(w, g, m, v):
    m2 = ADAM_B1 * m + (1.0 - ADAM_B1) * g
    v2 = ADAM_B2 * v + (1.0 - ADAM_B2) * (g * g)
    m_hat = m2 / (1.0 - ADAM_B1 ** ADAM_STEP)
    v_hat = v2 / (1.0 - ADAM_B2 ** ADAM_STEP)
    delta = -ADAM_LR * (m_hat / (jnp.sqrt(v_hat) + ADAM_EPS) + ADAM_WD * w)
    return delta, m2, v2


def _adam_sharded(name, chip, w, m, v, grads):
    _, r, c = w.shape
    tr = _tile(r, 256, 8)
    tc = c if tr < r else _tile(c, 256, 128)

    def body(chip_ref, w_ref, m_ref, v_ref, p0_ref, q0_ref, p1_ref, q1_ref, g_ref, d_ref, nm_ref, nv_ref):
        mine = chip_ref[0]

        def total(p_ref, q_ref):
            acc = None
            for j in range(4):
                part = jnp.where(mine == j, p_ref[...], q_ref[j]).astype(F32)
                acc = part if acc is None else acc + part
            return acc

        g = jnp.where(pl.program_id(0) == 0, total(p0_ref, q0_ref), total(p1_ref, q1_ref))
        delta, m2, v2 = _adam_math(w_ref[...], g, m_ref[...], v_ref[...])
        g_ref[...] = g
        d_ref[...] = delta
        nm_ref[...] = m2
        nv_ref[...] = v2

    def grad_specs(layer):
        at = lambda l, i, j: (jnp.where(l == layer, i, 0), jnp.where(l == layer, j, 0))
        return [pl.BlockSpec((None, tr, tc), lambda l, i, j, chip_ref: (chip_ref[0],) + at(l, i, j)),
                pl.BlockSpec((4, tr, tc), lambda l, i, j, chip_ref: (0,) + at(l, i, j))]

    wspec = pl.BlockSpec((None, tr, tc), lambda l, i, j, chip_ref: (l, i, j))
    sds = jax.ShapeDtypeStruct(w.shape, F32)
    return pl.pallas_call(
        body, name=name,
        grid_spec=pltpu.PrefetchScalarGridSpec(
            num_scalar_prefetch=1, grid=(2, r // tr, c // tc),
            in_specs=[wspec, wspec, wspec] + grad_specs(0) + grad_specs(1), out_specs=[wspec] * 4),
        out_shape=[sds] * 4, compiler_params=_params(),
    )(chip, w, m, v, grads[0][0], grads[0][1], grads[1][0], grads[1][1])


def _adam_small(name, w, m, v, g):
    def body(w_ref, m_ref, v_ref, g_ref, d_ref, nm_ref, nv_ref):
        delta, m2, v2 = _adam_math(w_ref[...], g_ref[...], m_ref[...], v_ref[...])
        d_ref[...] = delta
        nm_ref[...] = m2
        nv_ref[...] = v2

    sds = jax.ShapeDtypeStruct(w.shape, F32)
    vm = pl.BlockSpec(memory_space=pltpu.VMEM)
    return pl.pallas_call(body, name=name, in_specs=[vm] * 4, out_specs=[vm] * 3, out_shape=[sds] * 3,
                          compiler_params=_params())(w, m, v, g)


GATE_END = GATE_COL + 2 * HEADS


def _split_w_in(gathered):
    win_t = gathered.reshape(D_IN, D_MODEL)
    return win_t, win_t[GATE_END:].reshape(3, CONV_W, D_MODEL)


def _merge_dw_in(dwm_t, dwc_t):
    full = jnp.concatenate([dwm_t[:GATE_END], dwc_t.reshape(3 * CONV_W, D_MODEL)], axis=0)
    return full.reshape(N_DEV, IN_SH, D_MODEL)


def _pack128(parts):
    flat = jnp.concatenate([p.reshape(-1) for p in parts])
    n = flat.shape[0]
    rows = -(-n // 1024) * 8
    return jnp.pad(flat, (0, rows * 128 - n)).reshape(rows, 128)


def _unpack128(packed, shapes):
    flat = packed.reshape(-1)
    out, at = [], 0
    for s in shapes:
        n = int(np.prod(s))
        out.append(flat[at:at + n].reshape(s))
        at += n
    return out


def kernel(x, meta_tokens, norm_mix_w, w_in, b_gates, conv_w, mlstm_norm_w, w_out, norm_ffn_w, w_gate, w_up, w_down, norm_final_w, loss_target, m_meta_tokens, m_norm_mix_w, m_w_in, m_b_gates, m_conv_w, m_mlstm_norm_w, m_w_out, m_norm_ffn_w, m_w_gate, m_w_up, m_w_down, m_norm_final_w, v_meta_tokens, v_norm_mix_w, v_w_in, v_b_gates, v_conv_w, v_mlstm_norm_w, v_w_out, v_norm_ffn_w, v_w_gate, v_w_up, v_w_down, v_norm_final_w):
    seq = x.shape[1]
    rows = TOK0 + seq
    me = 4 * lax.axis_index("x") + 2 * lax.axis_index("y") + lax.axis_index("c")
    meta_sh = meta_tokens.shape[1]
    conv_sh = conv_w.shape[2]

    w_gate_t, m_w_gate_t, v_w_gate_t = (jnp.transpose(a, (0, 2, 1)) for a in (w_gate, m_w_gate, v_w_gate))
    w_up_t, m_w_up_t, v_w_up_t = (jnp.transpose(a, (0, 2, 1)) for a in (w_up, m_w_up, v_w_up))
    shards = []
    for l in range(DEPTH):
        shards += [jnp.transpose(w_in[l]).astype(BF16), w_out[l].astype(BF16), w_gate_t[l].astype(BF16),
                   w_up_t[l].astype(BF16), w_down[l].astype(BF16)]
    per_layer = ("w_in", "w_out", "w_gate", "w_up", "w_down")
    gather_state = {}

    def gather_step(tag, after, start=None, relay=None, last=None, done=()):
        jobs, idx = [], []
        if relay is not None and relay < len(shards):
            jobs.append((gather_state[relay][0], (_gather_relay, 4), (_gather_first, gather_state[relay][1])))
            idx.append(relay)
        if start is not None and start < len(shards):
            buf = lax.dynamic_update_index_in_dim(lax.empty((N_DEV,) + shards[start].shape, BF16), shards[start], me, 0)
            jobs.append(([buf], (_gather_first, 3), None))
            idx.append(start)
        if last is not None:
            jobs.append((gather_state[last][0], (_gather_last, 1), (_gather_relay, gather_state[last][1])))
            idx.append(last)
        for i in done:
            jobs.append((gather_state[i][0], None, (_gather_last, gather_state[i][1])))
            idx.append(i)
        if not jobs:
            return after, []
        results, tok = _split_copies(f"gather_{tag}", jobs, after)
        for i, res in zip(idx, results):
            gather_state[i] = res
        return (after if tok is None else tok), [gather_state[i][0][0] for i in done]

    bias = [jnp.pad(b_gates[l].reshape(1, 2 * HEADS), ((0, 0), (0, 128 - 2 * HEADS))) for l in range(DEPTH)]
    nmix = [norm_mix_w[l].reshape(1, D_MODEL) for l in range(DEPTH)]
    nffn = [norm_ffn_w[l].reshape(1, D_MODEL) for l in range(DEPTH)]
    nmls = [mlstm_norm_w[l].reshape(1, MLSTM_W) for l in range(DEPTH)]
    weights = [dict() for _ in range(DEPTH)]
    saved = [dict() for _ in range(DEPTH)]

    def layer_fwd(l, h, after):
        w, s = weights[l], saved[l]
        k0 = len(per_layer) * l
        tok, _ = gather_step(f"l{l}_a", after, last=k0)
        _, (g_in,) = gather_step(f"l{l}_b", tok, done=[k0])
        tok, _ = gather_step(f"l{l}_c", g_in, relay=k0 + 1, start=k0 + 3)
        w["win_t"], w["wc_t"] = _split_w_in(g_in)
        s["h0"] = h
        s["hn"], s["pm"] = _norm_proj(f"proj_mlstm_{l}", h, nmix[l] + tok[0, 0], w["win_t"], PM_W)
        tok, _ = gather_step(f"l{l}_d", s["pm"], relay=k0 + 2, start=k0 + 4)
        tok, _ = gather_step(f"l{l}_d2", tok, last=k0 + 1)
        s["pc"] = _mm_nt_bcols(f"proj_conv_{l}", s["hn"], w["wc_t"], F32, dep=tok)
        hm, s["ht"], s["cs"], s["ns"], s["ms"] = _mlstm_fwd(f"mlstm_fwd_{l}", s["pm"], bias[l] + tok[:1], nmls[l])
        tok, _ = gather_step(f"l{l}_e", hm, relay=k0 + 3, start=k0 + 5)
        tok, _ = gather_step(f"l{l}_e2", tok, last=k0 + 2)
        s["cat"] = _conv_fwd(f"conv_fwd_{l}", s["pc"], conv_rows[l] + tok[0, 0], hm)
        _, (g_out,) = gather_step(f"l{l}_f", s["cat"], done=[k0 + 1])
        w["wo"] = g_out.reshape(D_MODEL, D_MODEL)
        s["h1"], s["hf"] = _proj_res_norm(f"out_proj_{l}", s["cat"], w["wo"], s["h0"], nffn[l])
        tok_g, _ = gather_step(f"l{l}_g", s["h1"], relay=k0 + 4, start=k0 + 6)
        tok, _ = gather_step(f"l{l}_h", tok_g, last=k0 + 3)
        _, (g_gate, g_up) = gather_step(f"l{l}_i", tok, done=[k0 + 2, k0 + 3])
        w["wg_t"] = g_gate.reshape(D_FF, D_MODEL)
        w["wu_t"] = g_up.reshape(D_FF, D_MODEL)
        s["g"], s["u"], s["act"] = _ffn_in(f"ffn_in_{l}", s["hf"], w["wg_t"], w["wu_t"], dep=tok_g)
        tok, _ = gather_step(f"l{l}_j", s["act"], last=k0 + 4)
        _, (g_down,) = gather_step(f"l{l}_k", tok, done=[k0 + 4])
        w["wd"] = g_down.reshape(D_FF, D_MODEL)
        tok, _ = gather_step(f"l{l}_k2", tok, relay=k0 + 5, start=k0 + 7)
        return _mm_nn(f"ffn_out_{l}", s["act"], w["wd"], F32, res=s["h1"], dep=tok)

    tok, _ = gather_step("first", None, start=0)
    zero = tok[0, 0]
    small = jnp.concatenate(
        [meta_tokens + zero, jnp.pad(conv_w.reshape(DEPTH * 3, conv_sh), ((0, 2), (0, meta_sh - conv_sh)))], axis=0)
    slots = _exchange_small("gather_small", small, reduce=False)
    meta_full = jnp.transpose(slots[:, :N_META, :], (1, 0, 2)).reshape(N_META, D_MODEL)
    conv_full = jnp.transpose(slots[:, N_META:N_META + DEPTH * 3, :conv_sh], (1, 0, 2)).reshape(DEPTH, 3, CONV_W)
    conv_rows = [jnp.pad(conv_full[l], ((0, 5), (0, 0))) for l in range(DEPTH)]
    w_in_t, m_w_in_t, v_w_in_t = (jnp.transpose(a + zero, (0, 2, 1)) for a in (w_in, m_w_in, v_w_in))
    tok, w_in_t, m_w_in_t, v_w_in_t, meta_full = lax.optimization_barrier(
        (tok, w_in_t, m_w_in_t, v_w_in_t, meta_full))
    tok, _ = gather_step("pre_a", tok, relay=0)
    tok, _ = gather_step("pre_b", tok, start=1)
    tok, _ = gather_step("pre_c", tok, start=2)
    h = jnp.concatenate([jnp.zeros((PAD_FRONT, D_MODEL), F32), meta_full, x[0]], axis=0)
    h = layer_fwd(0, h, tok)
    h = layer_fwd(1, h, h)

    dh, dh_b, d_final, loss_part = _final_loss("final_loss", h, norm_final_w.reshape(1, D_MODEL), loss_target[0])

    core = lax.axis_index("c").astype(jnp.int32).reshape(1)
    chip = (2 * lax.axis_index("x") + lax.axis_index("y")).astype(jnp.int32).reshape(1)
    scatter_state = {}

    def scatter_begin(nm, grad):
        land = lax.empty((4,) + grad.shape[1:], BF16)
        arrs, sems, tok = _split_copy(f"grad_sibling_start_{nm}", [grad, land], start=(_scatter_sibling(1), 4))
        scatter_state[nm] = (arrs, sems)
        return tok

    def scatter_advance(nm, after):
        arrs, sems = scatter_state[nm]
        arrs, _, _ = _split_copy(f"grad_sibling_done_{nm}", arrs, wait=(_scatter_sibling(1), sems), after=after)
        part = _pair_sum(f"grad_pair_sum_{nm}", core, arrs[0], arrs[1])
        arrs, sems, tok = _split_copy(f"grad_chips_start_{nm}", [part, lax.empty(part.shape, BF16)],
                                      start=(_scatter_chips(1), 3))
        scatter_state[nm] = (arrs, sems)
        return tok

    def scattered(nm, after):
        arrs, sems = scatter_state[nm]
        arrs, _, _ = _split_copy(f"grad_chips_done_{nm}", arrs, wait=(_scatter_chips(1), sems), after=after)
        return arrs[0], arrs[1]

    d_mix, d_ffn, d_mls, d_bias, d_conv = ([None] * DEPTH for _ in range(5))

    def layer_bwd(l, dh, dh_b, tok):
        w, s = weights[l], saved[l]
        dg, du = _ffn_act_bwd(f"d_act_{l}", dh_b, w["wd"], s["g"], s["u"], dep=tok)
        dw_down = _mm_tn(f"dw_down_{l}", s["act"], dh_b, BF16, tm=1408, tn=1024)
        tok = scatter_begin(f"w_down_{l}", dw_down.reshape(N_DEV, FF_SH, D_MODEL))
        dhf = _mm_nn(f"d_ffn_gate_{l}", dg, w["wg_t"], F32, dep=tok)
        tok = scatter_advance(f"w_down_{l}", after=dhf)
        dhf = _mm_nn(f"d_ffn_up_{l}", du, w["wu_t"], F32, res=dhf, dep=tok)
        dw_gate = _mm_tn(f"dw_gate_{l}", dg, s["hf"], BF16, tm=1408, tn=1024)
        tok = scatter_begin(f"w_gate_{l}", dw_gate.reshape(N_DEV, FF_SH, D_MODEL))
        dw_up = _mm_tn(f"dw_up_{l}", du, s["hf"], BF16, tm=1408, tn=1024, dep=tok)
        tok = scatter_begin(f"w_up_{l}", dw_up.reshape(N_DEV, FF_SH, D_MODEL))
        dh1, dh1_b, d_ffn[l] = _rms_bwd(f"norm_ffn_bwd_{l}", s["h1"], nffn[l] + tok[0, 0], dhf, dh)
        tok = scatter_advance(f"w_gate_{l}", after=dh1)
        dcat = _mm_nt(f"d_cat_{l}", dh1_b, w["wo"], F32, tk=D_MODEL, dep=tok)
        tok = scatter_advance(f"w_up_{l}", after=dcat)
        dw_out = _mm_tn(f"dw_out_{l}", s["cat"], dh1_b, BF16, tn=1024, dep=tok)
        tok = scatter_begin(f"w_out_{l}", dw_out.reshape(N_DEV, OUT_SH, D_MODEL))
        dpm, d_mls[l], d_bias[l] = _mlstm_bwd(f"mlstm_bwd_{l}", dcat, s["pm"], s["ht"], s["cs"], s["ns"],
                                               s["ms"], bias[l] + tok[:1], nmls[l])
        dpc, d_conv[l] = _conv_bwd(f"conv_bwd_{l}", dcat, s["pc"], conv_rows[l])
        tok = scatter_advance(f"w_out_{l}", after=dpc)
        dwm_t = _mm_tn(f"dw_mlstm_{l}", dpm, s["hn"], BF16, tm=640, tn=1024, dep=tok)
        dwc_t = _mm_tn_acols(f"dw_conv_{l}", dpc, s["hn"], BF16)
        tok = scatter_begin(f"w_in_{l}", _merge_dw_in(dwm_t, dwc_t))
        dhn = _mm_nn_kt(f"d_norm_mlstm_{l}", dpm, w["win_t"], F32, tk=PM_W, dep=tok)
        dhn = _mm_nn_stack(f"d_norm_conv_{l}", dpc, w["wc_t"], dhn)
        tok = scatter_advance(f"w_in_{l}", after=dhn)
        dh, dh_b, d_mix[l] = _rms_bwd(f"norm_mix_bwd_{l}", s["h0"], nmix[l] + tok[0, 0], dhn, dh1)
        return dh, dh_b, tok

    dh, dh_b, tok = layer_bwd(1, dh, dh_b, None)
    dh, dh_b, tok_tail = layer_bwd(0, dh, dh_b, tok)

    pq = {}
    after = dh
    for l in reversed(range(DEPTH)):
        for nm in ("w_down", "w_gate", "w_up", "w_out", "w_in"):
            if (nm, l) != ("w_in", 0):
                pq[nm, l] = scattered(f"{nm}_{l}", after)
                after = pq[nm, l][0]
    untransposed = lambda outs: [jnp.transpose(o, (0, 2, 1)) for o in outs]
    g_out, d_out, nm_out, nv_out = _adam_sharded(
        "adam_w_out", chip, w_out, m_w_out, v_w_out, [pq["w_out", 0], pq["w_out", 1]])
    g_gate, d_gate, nm_gate, nv_gate = untransposed(_adam_sharded(
        "adam_w_gate", chip, w_gate_t, m_w_gate_t, v_w_gate_t, [pq["w_gate", 0], pq["w_gate", 1]]))
    g_up, d_up, nm_up, nv_up = untransposed(_adam_sharded(
        "adam_w_up", chip, w_up_t, m_w_up_t, v_w_up_t, [pq["w_up", 0], pq["w_up", 1]]))
    g_down, d_down, nm_down, nv_down = _adam_sharded(
        "adam_w_down", chip, w_down, m_w_down, v_w_down, [pq["w_down", 0], pq["w_down", 1]])
    pq["w_in", 0] = scattered("w_in_0", nv_down)
    g_in, d_in, nm_in, nv_in = untransposed(_adam_sharded(
        "adam_w_in", chip, w_in_t, m_w_in_t, v_w_in_t, [pq["w_in", 0], pq["w_in", 1]]))

    bg = jnp.concatenate([d_bias[l][0, :2 * HEADS] for l in range(DEPTH)])
    red_in = jnp.concatenate([
        dh[PAD_FRONT:TOK0], d_mix[0], d_mix[1], d_ffn[0], d_ffn[1], d_final,
        jnp.concatenate([d_mls[0], d_mls[1]], axis=1),
        jnp.stack([d_conv[l][:3] for l in range(DEPTH)]).reshape(3, 2 * CONV_W),
        jnp.pad(bg, (0, D_MODEL - bg.shape[0])).reshape(1, D_MODEL),
        jnp.pad(loss_part[:, :1], ((0, 0), (0, D_MODEL - 1))),
        jnp.zeros((5, D_MODEL), F32) + tok_tail[0, 0]], axis=0)
    red = _exchange_small("reduce_small", red_in, reduce=True)
    loss = red[26, 0]
    g_meta = lax.dynamic_slice_in_dim(red[:N_META], me * meta_sh, meta_sh, axis=1)
    g_mix, g_ffn, g_final = red[16:18], red[18:20], red[20]
    g_mls = red[21].reshape(DEPTH, MLSTM_W)
    g_conv = lax.dynamic_slice_in_dim(red[22:25].reshape(DEPTH, 3, CONV_W), me * conv_sh, conv_sh, axis=2)
    g_bias = red[25, :DEPTH * 2 * HEADS].reshape(DEPTH, 2 * HEADS)

    small_w = [meta_tokens, norm_mix_w, b_gates, conv_w, mlstm_norm_w, norm_ffn_w, norm_final_w]
    small_m = [m_meta_tokens, m_norm_mix_w, m_b_gates, m_conv_w, m_mlstm_norm_w, m_norm_ffn_w, m_norm_final_w]
    small_v = [v_meta_tokens, v_norm_mix_w, v_b_gates, v_conv_w, v_mlstm_norm_w, v_norm_ffn_w, v_norm_final_w]
    small_g = [g_meta, g_mix, g_bias, g_conv, g_mls, g_ffn, g_final]
    shapes = [a.shape for a in small_w]
    packed = _adam_small("adam_small", _pack128(small_w), _pack128(small_m), _pack128(small_v), _pack128(small_g))
    (d_meta, d_nmix, d_bg, d_cw, d_nmls, d_nffn, d_nfin), (nm_meta, nm_nmix, nm_bg, nm_cw, nm_nmls, nm_nffn, nm_nfin), \
        (nv_meta, nv_nmix, nv_bg, nv_cw, nv_nmls, nv_nffn, nv_nfin) = (_unpack128(p, shapes) for p in packed)

    grad_x = dh[TOK0:].reshape(1, seq, D_MODEL)
    return (loss, grad_x,
            g_meta, g_mix, g_in, g_bias, g_conv, g_mls, g_out, g_ffn, g_gate, g_up, g_down, g_final,
            d_meta, d_nmix, d_in, d_bg, d_cw, d_nmls, d_out, d_nffn, d_gate, d_up, d_down, d_nfin,
            nm_meta, nm_nmix, nm_in, nm_bg, nm_cw, nm_nmls, nm_out, nm_nffn, nm_gate, nm_up, nm_down, nm_nfin,
            nv_meta, nv_nmix, nv_in, nv_bg, nv_cw, nv_nmls, nv_out, nv_nffn, nv_gate, nv_up, nv_down, nv_nfin)
```

```python
import numpy as np
import jax
import jax.numpy as jnp
from jax import lax
from jax.experimental import pallas as pl
from jax.experimental.pallas import tpu as pltpu

F32 = jnp.float32
BF16 = jnp.bfloat16
MESH = pl.DeviceIdType.MESH

D_MODEL = 2048
DEPTH = 2
N_META = 16
MLSTM_W = 1024
CONV_W = 1024
HEADS = 4
DV = 256
DQK = 128
QK_W = 512
CHUNK = 64
PAD_FRONT = 48
TOK0 = PAD_FRONT + N_META
D_FF = 5632
N_DEV = 8
FF_SH = D_FF // N_DEV
D_IN = 6152
IN_SH = D_IN // N_DEV
OUT_SH = D_MODEL // N_DEV
GATE_COL = 3072
PM_W = GATE_COL + 128
GATE_CAP = 15.0
EPS = 1e-6
QSCALE = DQK ** -0.5

ADAM_LR = 0.001
ADAM_B1 = 0.9
ADAM_B2 = 0.999
ADAM_EPS = 1e-08
ADAM_WD = 0.01
ADAM_STEP = 10

V7X_VMEM_LIMIT = 50 * 1024 * 1024
V7X_MXU_COLS = 256


def _params(**kw):
    return pltpu.CompilerParams(vmem_limit_bytes=V7X_VMEM_LIMIT, **kw)


def _tile(n, target, mult):
    best = None
    for t in range(mult, min(n, target) + 1, mult):
        if n % t == 0:
            best = t
    return best if best is not None else n


def _sigmoid(x):
    return 1.0 / (1.0 + jnp.exp(-x))


NN = ((1,), (0,))
NT = ((1,), (1,))
TN = ((0,), (0,))


def _matmul(name, a, b, out_shape, out_dtype, grid, a_bs, b_bs, o_bs, dims, nk, acc_shape=None,
            res=None, res_bs=None, dep=None):
    has_res = res is not None
    n_in = 2 + has_res + (dep is not None)

    def body(*refs):
        a_ref, b_ref = refs[0], refs[1]
        r_ref = refs[2] if has_res else None
        o_ref = refs[n_in]
        x = lax.dot_general(a_ref[...], b_ref[...], (dims, ((), ())), preferred_element_type=F32)
        if nk == 1:
            if has_res:
                x = x + r_ref[...]
            o_ref[...] = x.astype(o_ref.dtype)
            return
        acc = refs[n_in + 1]
        k = pl.program_id(len(grid) - 1)

        @pl.when(k == 0)
        def _():
            acc[...] = (x + r_ref[...]) if has_res else x

        @pl.when(k > 0)
        def _():
            acc[...] += x

        @pl.when(k == nk - 1)
        def _():
            o_ref[...] = acc[...].astype(o_ref.dtype)

    ins = [a, b] + ([res] if has_res else [])
    specs = [a_bs, b_bs] + ([res_bs] if has_res else [])
    if dep is not None:
        ins.append(dep)
        specs.append(pl.BlockSpec((8, 128), lambda *_: (0, 0)))
    scratch = [pltpu.VMEM(acc_shape, F32)] if nk > 1 else []
    return pl.pallas_call(
        body, name=name, grid=grid, in_specs=specs, out_specs=o_bs,
        out_shape=jax.ShapeDtypeStruct(out_shape, out_dtype), scratch_shapes=scratch,
        compiler_params=_params(),
    )(*ins)


def _mm_nn(name, a, b, out_dtype, res=None, tm=1056, tn=512, dep=None):
    r, k = a.shape
    n = b.shape[1]
    tm, tn = _tile(r, tm, 8), _tile(n, tn, 128)
    return _matmul(name, a, b, (r, n), out_dtype, (r // tm, n // tn, 1),
                   pl.BlockSpec((tm, k), lambda i, j, s: (i, 0)),
                   pl.BlockSpec((k, tn), lambda i, j, s: (0, j)),
                   pl.BlockSpec((tm, tn), lambda i, j, s: (i, j)), NN, 1,
                   res=res, res_bs=pl.BlockSpec((tm, tn), lambda i, j, s: (i, j)), dep=dep)


def _mm_nn_two(name, a, b, a3, b3, dep=None, tm=1056, tn=512):
    r, k = a.shape
    e, _, kb = a3.shape
    n = b.shape[1]
    tm, tn = _tile(r, tm, 8), _tile(n, tn, 128)

    def body(a_ref, b_ref, a3_ref, b3_ref, *rest):
        acc = lax.dot_general(a_ref[...], b_ref[...], (NN, ((), ())), preferred_element_type=F32)
        for s in range(e):
            acc = acc + lax.dot_general(a3_ref[s], b3_ref[s], (NN, ((), ())), preferred_element_type=F32)
        rest[-1][...] = acc

    ins = [a, b, a3, b3]
    specs = [pl.BlockSpec((tm, k), lambda i, j: (i, 0)), pl.BlockSpec((k, tn), lambda i, j: (0, j)),
             pl.BlockSpec((e, tm, kb), lambda i, j: (0, i, 0)), pl.BlockSpec((e, kb, tn), lambda i, j: (0, 0, j))]
    if dep is not None:
        ins.append(dep)
        specs.append(pl.BlockSpec((8, 128), lambda *_: (0, 0)))
    return pl.pallas_call(
        body, name=name, grid=(r // tm, n // tn), in_specs=specs,
        out_specs=pl.BlockSpec((tm, tn), lambda i, j: (i, j)),
        out_shape=jax.ShapeDtypeStruct((r, n), F32), compiler_params=_params(),
    )(*ins)


def _mm_nt(name, a, b, out_dtype, res=None, tm=1056, tn=512, tk=640, n=None, dep=None):
    r, k = a.shape
    n = b.shape[0] if n is None else n
    tm, tn, tk = _tile(r, tm, 8), _tile(n, tn, 128), _tile(k, tk, 128)
    nk = k // tk
    return _matmul(name, a, b, (r, n), out_dtype, (r // tm, n // tn, nk),
                   pl.BlockSpec((tm, tk), lambda i, j, s: (i, s)),
                   pl.BlockSpec((tn, tk), lambda i, j, s: (j, s)),
                   pl.BlockSpec((tm, tn), lambda i, j, s: (i, j)), NT, nk, acc_shape=(tm, tn),
                   res=res, res_bs=pl.BlockSpec((tm, tn), lambda i, j, s: (i, j)), dep=dep)


def _mm_nt_bcols(name, a, b3, out_dtype, tm=1056, dep=None):
    r, k = a.shape
    e, n, _ = b3.shape
    tm = _tile(r, tm, 8)
    return _matmul(name, a, b3, (e, r, n), out_dtype, (r // tm, e, 1),
                   pl.BlockSpec((tm, k), lambda i, g, s: (i, 0)),
                   pl.BlockSpec((None, n, k), lambda i, g, s: (g, 0, 0)),
                   pl.BlockSpec((None, tm, n), lambda i, g, s: (g, i, 0)), NT, 1, dep=dep)


def _mm_tn(name, a, b, out_dtype, tm=1024, tn=640, dep=None):
    r, m = a.shape
    n = b.shape[1]
    tm, tn = _tile(m, tm, 128), _tile(n, tn, 128)
    return _matmul(name, a, b, (m, n), out_dtype, (m // tm, n // tn, 1),
                   pl.BlockSpec((r, tm), lambda i, j, s: (0, i)),
                   pl.BlockSpec((r, tn), lambda i, j, s: (0, j)),
                   pl.BlockSpec((tm, tn), lambda i, j, s: (i, j)), TN, 1, dep=dep)


def _mm_tn_acols(name, a3, b, out_dtype, tn=1024, dep=None):
    e, r, m = a3.shape
    n = b.shape[1]
    tn = _tile(n, tn, 128)
    return _matmul(name, a3, b, (e, m, n), out_dtype, (n // tn, e, 1),
                   pl.BlockSpec((None, r, m), lambda j, g, s: (g, 0, 0)),
                   pl.BlockSpec((r, tn), lambda j, g, s: (0, j)),
                   pl.BlockSpec((None, m, tn), lambda j, g, s: (g, 0, j)), TN, 1, dep=dep)


def _norm_proj(name, h, w, b, n, tm=1056, tn=640):
    r, d = h.shape
    tm, tn = _tile(r, tm, 8), _tile(n, tn, 128)

    def body(h_ref, w_ref, b_ref, hn_ref, o_ref):
        @pl.when(pl.program_id(1) == 0)
        def _():
            x = h_ref[...]
            rs = lax.rsqrt(jnp.mean(x * x, axis=1, keepdims=True) + EPS)
            hn_ref[...] = (x * rs * w_ref[...]).astype(BF16)

        o_ref[...] = lax.dot_general(hn_ref[...], b_ref[...], (NT, ((), ())), preferred_element_type=F32)

    row = pl.BlockSpec((tm, d), lambda i, j: (i, 0))
    return pl.pallas_call(
        body, name=name, grid=(r // tm, n // tn),
        in_specs=[row, pl.BlockSpec((1, d), lambda i, j: (0, 0)), pl.BlockSpec((tn, d), lambda i, j: (j, 0))],
        out_specs=[row, pl.BlockSpec((tm, tn), lambda i, j: (i, j))],
        out_shape=[jax.ShapeDtypeStruct((r, d), BF16), jax.ShapeDtypeStruct((r, n), F32)],
        compiler_params=_params(),
    )(h, w, b)


def _proj_res_norm(name, a, b, res, w, tm=528):
    r, k = a.shape
    d = b.shape[1]
    tm = _tile(r, tm, 8)

    def body(a_ref, b_ref, r_ref, w_ref, y_ref, n_ref):
        y = lax.dot_general(a_ref[...], b_ref[...], (NN, ((), ())), preferred_element_type=F32) + r_ref[...]
        y_ref[...] = y
        rs = lax.rsqrt(jnp.mean(y * y, axis=1, keepdims=True) + EPS)
        n_ref[...] = (y * rs * w_ref[...]).astype(BF16)

    row = pl.BlockSpec((tm, d), lambda i: (i, 0))
    return pl.pallas_call(
        body, name=name, grid=(r // tm,),
        in_specs=[pl.BlockSpec((tm, k), lambda i: (i, 0)), pl.BlockSpec((k, d), lambda i: (0, 0)), row,
                  pl.BlockSpec((1, d), lambda i: (0, 0))],
        out_specs=[row, row],
        out_shape=[jax.ShapeDtypeStruct((r, d), F32), jax.ShapeDtypeStruct((r, d), BF16)],
        compiler_params=_params(),
    )(a, b, res, w)


def _rms_bwd(name, x, w, dy, dres):
    r, d = x.shape
    tr = _tile(r, 264, 8)

    def body(x_ref, w_ref, dy_ref, dr_ref, dx_ref, dxb_ref, dw_ref):
        xv = x_ref[...]
        g = dy_ref[...]
        rs = lax.rsqrt(jnp.mean(xv * xv, axis=1, keepdims=True) + EPS)
        wg = g * w_ref[...]
        dx = rs * wg - xv * (rs * rs * rs) * jnp.mean(xv * wg, axis=1, keepdims=True) + dr_ref[...]
        dx_ref[...] = dx
        dxb_ref[...] = dx.astype(BF16)
        part = jnp.sum(g * xv * rs, axis=0, keepdims=True)

        @pl.when(pl.program_id(0) == 0)
        def _():
            dw_ref[...] = part

        @pl.when(pl.program_id(0) > 0)
        def _():
            dw_ref[...] += part

    row = pl.BlockSpec((tr, d), lambda i: (i, 0))
    vec = pl.BlockSpec((1, d), lambda i: (0, 0))
    return pl.pallas_call(
        body, name=name, grid=(r // tr,), in_specs=[row, vec, row, row], out_specs=[row, row, vec],
        out_shape=[jax.ShapeDtypeStruct((r, d), F32), jax.ShapeDtypeStruct((r, d), BF16),
                   jax.ShapeDtypeStruct((1, d), F32)],
        compiler_params=_params(),
    )(x, w, dy, dres)


def _final_loss(name, h, w, target):
    r, d = h.shape
    nb = r // CHUNK

    def body(h_ref, w_ref, t_ref, dh_ref, dhb_ref, dw_ref, ls_ref):
        i = pl.program_id(0)

        @pl.when(i == 0)
        def _():
            dh_ref[...] = jnp.zeros_like(dh_ref)
            dhb_ref[...] = jnp.zeros_like(dhb_ref)
            dw_ref[...] = jnp.zeros_like(dw_ref)
            ls_ref[...] = jnp.zeros_like(ls_ref)

        @pl.when(i > 0)
        def _():
            xv = h_ref[...]
            wv = w_ref[...]
            rs = lax.rsqrt(jnp.mean(xv * xv, axis=1, keepdims=True) + EPS)
            err = xv * rs * wv - t_ref[...]
            sq = jnp.sum(jnp.sum(err * err, axis=1, keepdims=True), axis=0, keepdims=True)
            ls_ref[...] += jnp.broadcast_to(sq * (0.5 / d), ls_ref.shape)
            g = err * (1.0 / d)
            wg = g * wv
            dx = rs * wg - xv * (rs * rs * rs) * jnp.mean(xv * wg, axis=1, keepdims=True)
            dh_ref[...] = dx
            dhb_ref[...] = dx.astype(BF16)
            dw_ref[...] += jnp.sum(g * xv * rs, axis=0, keepdims=True)

    row = pl.BlockSpec((CHUNK, d), lambda i: (i, 0))
    vec = pl.BlockSpec((1, d), lambda i: (0, 0))
    return pl.pallas_call(
        body, name=name, grid=(nb,),
        in_specs=[row, vec, pl.BlockSpec((CHUNK, d), lambda i: (jnp.maximum(i - 1, 0), 0))],
        out_specs=[row, row, vec, pl.BlockSpec((1, 128), lambda i: (0, 0))],
        out_shape=[jax.ShapeDtypeStruct((r, d), F32), jax.ShapeDtypeStruct((r, d), BF16),
                   jax.ShapeDtypeStruct((1, d), F32), jax.ShapeDtypeStruct((1, 128), F32)],
        compiler_params=_params(),
    )(h, w, target)


def _ffn_in(name, hf, wg_t, wu_t, dep=None, tm=1056, tn=512):
    r, d = hf.shape
    f = wg_t.shape[0]
    tm, tn = _tile(r, tm, 8), _tile(f, tn, 128)

    def body(h_ref, wg_ref, wu_ref, *rest):
        g_ref, u_ref, a_ref = rest[-3:]
        x = h_ref[...]
        g = lax.dot_general(x, wg_ref[...], (NT, ((), ())), preferred_element_type=F32)
        u = lax.dot_general(x, wu_ref[...], (NT, ((), ())), preferred_element_type=F32)
        g_ref[...] = g.astype(BF16)
        u_ref[...] = u.astype(BF16)
        a_ref[...] = (g * _sigmoid(g) * u).astype(BF16)

    wspec = pl.BlockSpec((tn, d), lambda i, j: (j, 0))
    ospec = pl.BlockSpec((tm, tn), lambda i, j: (i, j))
    ins, specs = [hf, wg_t, wu_t], [pl.BlockSpec((tm, d), lambda i, j: (i, 0)), wspec, wspec]
    if dep is not None:
        ins.append(dep)
        specs.append(pl.BlockSpec((8, 128), lambda *_: (0, 0)))
    return pl.pallas_call(
        body, name=name, grid=(r // tm, f // tn), in_specs=specs, out_specs=[ospec] * 3,
        out_shape=[jax.ShapeDtypeStruct((r, f), BF16)] * 3, compiler_params=_params(),
    )(*ins)


def _ffn_act_bwd(name, dh, wd, g, u, dep=None, tm=1056, tn=512):
    r, d = dh.shape
    f = wd.shape[0]
    tm, tn = _tile(r, tm, 8), _tile(f, tn, 128)

    def body(dh_ref, wd_ref, g_ref, u_ref, *rest):
        dg_ref, du_ref = rest[-2:]
        tr = _tile(tm, 264, 8)
        for r0 in range(0, tm, tr):
            for c0 in range(0, tn, V7X_MXU_COLS):
                rows, cols = slice(r0, r0 + tr), slice(c0, c0 + V7X_MXU_COLS)
                da = lax.dot_general(dh_ref[rows, :], wd_ref[cols, :], (NT, ((), ())), preferred_element_type=F32)
                gv = g_ref[rows, cols].astype(F32)
                s = _sigmoid(gv)
                t = da * s
                du_ref[rows, cols] = (t * gv).astype(BF16)
                dg_ref[rows, cols] = (t * u_ref[rows, cols].astype(F32) * (1.0 + gv - gv * s)).astype(BF16)

    tile = pl.BlockSpec((tm, tn), lambda i, j: (i, j))
    ins = [dh, wd, g, u]
    specs = [pl.BlockSpec((tm, d), lambda i, j: (i, 0)), pl.BlockSpec((tn, d), lambda i, j: (j, 0)), tile, tile]
    if dep is not None:
        ins.append(dep)
        specs.append(pl.BlockSpec((8, 128), lambda *_: (0, 0)))
    return pl.pallas_call(
        body, name=name, grid=(r // tm, f // tn), in_specs=specs, out_specs=[tile] * 2,
        out_shape=[jax.ShapeDtypeStruct((r, f), BF16)] * 2, compiler_params=_params(),
    )(*ins)


def _shift_down(a, k):
    row = lax.broadcasted_iota(jnp.int32, a.shape, 0)
    return jnp.where(row >= k, pltpu.roll(a, k, 0), 0.0)


def _shift_up(a, k):
    n = a.shape[0]
    row = lax.broadcasted_iota(jnp.int32, a.shape, 0)
    return jnp.where(row < n - k, pltpu.roll(a, n - k, 0), 0.0)


def _conv_fwd(name, pc, cw, cat):
    _, r, w = pc.shape
    nblk = w // 128

    def body(pc_ref, cw_ref, cat_ref, o_ref):
        a = pc_ref[2] * pc_ref[0]
        cwv = cw_ref[...]
        conv = _shift_down(a, 2) * cwv[0:1] + _shift_down(a, 1) * cwv[1:2] + a * cwv[2:3]
        o_ref[...] = (pc_ref[1] * conv).astype(BF16)

    return pl.pallas_call(
        body, name=name, grid=(nblk,),
        in_specs=[pl.BlockSpec((3, r, 128), lambda j: (0, 0, j)), pl.BlockSpec((8, 128), lambda j: (0, j)),
                  pl.BlockSpec(memory_space=pl.ANY)],
        out_specs=pl.BlockSpec((r, 128), lambda j: (0, nblk + j)),
        out_shape=jax.ShapeDtypeStruct(cat.shape, BF16), input_output_aliases={2: 0},
        compiler_params=_params(),
    )(pc, cw, cat)


def _conv_bwd(name, dcat, pc, cw):
    _, r, w = pc.shape
    nblk = w // 128

    def body(dy_ref, pc_ref, cw_ref, dpc_ref, dcw_ref):
        u, gb, gc = pc_ref[0], pc_ref[1], pc_ref[2]
        cwv = cw_ref[...]
        dy = dy_ref[...]
        a = gc * u
        a1, a2 = _shift_down(a, 1), _shift_down(a, 2)
        conv = a2 * cwv[0:1] + a1 * cwv[1:2] + a * cwv[2:3]
        dconv = dy * gb
        da = dconv * cwv[2:3] + _shift_up(dconv, 1) * cwv[1:2] + _shift_up(dconv, 2) * cwv[0:1]
        dpc_ref[0] = (da * gc).astype(BF16)
        dpc_ref[1] = (dy * conv).astype(BF16)
        dpc_ref[2] = (da * u).astype(BF16)
        row = lax.broadcasted_iota(jnp.int32, (8, 128), 0)
        dw0 = jnp.sum(dconv * a2, axis=0, keepdims=True)
        dw1 = jnp.sum(dconv * a1, axis=0, keepdims=True)
        dw2 = jnp.sum(dconv * a, axis=0, keepdims=True)
        dcw_ref[...] = jnp.where(row == 0, dw0, jnp.where(row == 1, dw1, jnp.where(row == 2, dw2, 0.0)))

    return pl.pallas_call(
        body, name=name, grid=(nblk,),
        in_specs=[pl.BlockSpec((r, 128), lambda j: (0, nblk + j)),
                  pl.BlockSpec((3, r, 128), lambda j: (0, 0, j)), pl.BlockSpec((8, 128), lambda j: (0, j))],
        out_specs=[pl.BlockSpec((3, r, 128), lambda j: (0, 0, j)), pl.BlockSpec((8, 128), lambda j: (0, j))],
        out_shape=[jax.ShapeDtypeStruct((3, r, w), BF16), jax.ShapeDtypeStruct((8, w), F32)],
        compiler_params=_params(),
    )(dcat, pc, cw)


def _dot(a, b, dims):
    return lax.dot_general(a, b, (dims, ((), ())), preferred_element_type=F32)


def _col_to_row(xc, eye):
    return jnp.sum(jnp.where(eye, xc, 0.0), axis=0, keepdims=True)


def _row_to_col(xr, eye):
    return jnp.sum(jnp.where(eye, xr, 0.0), axis=1, keepdims=True)


def _gate_tiles(graw, bias, row0):
    th = jnp.tanh((graw + bias) / GATE_CAP)
    z = GATE_CAP * th
    row = lax.broadcasted_iota(jnp.int32, graw.shape, 0) + row0
    real = row >= PAD_FRONT
    li = jnp.where(real, z, -jnp.inf)
    lf = jnp.where(real, jnp.minimum(z, 0.0) - jnp.log(1.0 + jnp.exp(-jnp.abs(z))), 0.0)
    return th, z, li, lf, real


def _interleave(gens):
    results = [None] * len(gens)
    live = list(enumerate(gens))
    while live:
        still = []
        for i, gen in live:
            try:
                next(gen)
                still.append((i, gen))
            except StopIteration as stop:
                results[i] = stop.value
        live = still
    return results


def _chunk_common(pm, h, li, lf, cst, nst, mst, tril, eye):
    kraw = pm[:, QK_W + h * DQK:QK_W + (h + 1) * DQK]
    q = (pm[:, h * DQK:(h + 1) * DQK] * QSCALE).astype(BF16)
    yield
    k = kraw.astype(BF16)
    v = pm[:, 2 * QK_W + h * DV:2 * QK_W + (h + 1) * DV].astype(BF16)
    yield
    li_c = li[:, h:h + 1]
    lf_c = lf[:, HEADS + h:HEADS + h + 1]
    li_r = _col_to_row(li_c, eye)
    yield
    lf_r = _col_to_row(lf_c, eye)
    yield
    b_c = jnp.sum(jnp.where(tril, lf_r, 0.0), axis=1, keepdims=True)
    yield
    b_r = _col_to_row(b_c, eye)
    yield
    dmat = jnp.where(tril, b_c - b_r + li_r, -jnp.inf)
    inter = b_c + mst
    yield
    mt = jnp.maximum(inter, jnp.max(dmat, axis=1, keepdims=True))
    yield
    w_inter = jnp.exp(inter - mt)
    p = jnp.exp(dmat - mt)
    yield
    s = _dot(q, k, NT) * p
    yield
    cb = cst.astype(BF16)
    nb = nst.astype(BF16).astype(F32)
    qc = _dot(q, cb, NN)
    yield
    qn = jnp.sum(q.astype(F32) * nb, axis=1, keepdims=True)
    yield
    den = w_inter * qn + jnp.sum(s, axis=1, keepdims=True)
    yield
    dn = jnp.maximum(jnp.abs(den), jnp.exp(-mt))
    b_end = b_c[CHUNK - 1:CHUNK, :]
    decay = b_end - b_c + li_c
    yield
    m_new = jnp.maximum(b_end + mst, jnp.max(decay, axis=0, keepdims=True))
    yield
    w_old = jnp.exp(b_end + mst - m_new)
    w_in = jnp.exp(decay - m_new)
    kw = (w_in * kraw).astype(BF16)
    yield
    return dict(q=q, k=k, v=v, kraw=kraw, mt=mt, w_inter=w_inter, p=p, s=s, cb=cb, nb=nb, qc=qc, qn=qn,
                den=den, dn=dn, m_new=m_new, w_old=w_old, w_in=w_in, kw=kw)


def _chunks_per_step(nc):
    return 1


def _mlstm_fwd(name, pm, bias, nw):
    r = pm.shape[0]
    nc = r // CHUNK
    grp = _chunks_per_step(nc)

    def body(pm_ref, b_ref, nw_ref, hm_ref, ht_ref, cs_ref, ns_ref, ms_ref, c_scr, n_scr, m_scr):
        step = pl.program_id(0)

        @pl.when(step == 0)
        def _():
            c_scr[...] = jnp.zeros_like(c_scr)
            n_scr[...] = jnp.zeros_like(n_scr)
            m_scr[...] = jnp.zeros_like(m_scr)

        rr = lax.broadcasted_iota(jnp.int32, (CHUNK, CHUNK), 0)
        cc = lax.broadcasted_iota(jnp.int32, (CHUNK, CHUNK), 1)
        tril, eye = cc <= rr, cc == rr
        bv, nwv = b_ref[...], nw_ref[...]
        states = [(c_scr[h], n_scr[h], m_scr[h]) for h in range(HEADS)]
        for g in range(grp):
            rows = slice(g * CHUNK, (g + 1) * CHUNK)
            pmv = pm_ref[rows, :]
            _, _, li, lf, _ = _gate_tiles(pmv[:, GATE_COL:GATE_COL + 128], bv, (step * grp + g) * CHUNK)
            def head(h, cst, nst, mst, g=g, rows=rows, pmv=pmv, li=li, lf=lf):
                f = yield from _chunk_common(pmv, h, li, lf, cst, nst, mst, tril, eye)
                num = f["w_inter"] * f["qc"] + _dot(f["s"].astype(BF16), f["v"], NN)
                yield
                hh = num / f["dn"]
                yield
                c_new = f["w_old"] * cst + _dot(f["kw"], f["v"], TN)
                yield
                n_new = f["w_old"] * nst + jnp.sum(
                    f["w_in"].astype(BF16).astype(F32) * f["k"].astype(F32), axis=0, keepdims=True)
                yield
                sl = slice(h * DV, (h + 1) * DV)
                rs = lax.rsqrt(jnp.mean(hh * hh, axis=1, keepdims=True) + EPS)
                yield
                og = pmv[:, 2 * QK_W + MLSTM_W + h * DV:2 * QK_W + MLSTM_W + (h + 1) * DV]
                cs_ref[g, h] = cst
                ns_ref[g, h] = nst
                ms_ref[g, h] = mst
                ht_ref[rows, sl] = hh
                yield
                hm_ref[rows, sl] = (_sigmoid(og) * (hh * rs * nwv[:, sl])).astype(BF16)
                return c_new, n_new, f["m_new"]

            states = _interleave([head(h, *states[h]) for h in range(HEADS)])
        for h, (cst, nst, mst) in enumerate(states):
            c_scr[h] = cst
            n_scr[h] = nst
            m_scr[h] = mst

    return pl.pallas_call(
        body, name=name, grid=(nc // grp,),
        in_specs=[pl.BlockSpec((grp * CHUNK, PM_W), lambda i: (i, 0)), pl.BlockSpec((1, 128), lambda i: (0, 0)),
                  pl.BlockSpec((1, MLSTM_W), lambda i: (0, 0))],
        out_specs=[pl.BlockSpec((grp * CHUNK, MLSTM_W), lambda i: (i, 0)),
                   pl.BlockSpec((grp * CHUNK, MLSTM_W), lambda i: (i, 0)),
                   pl.BlockSpec((grp, HEADS, DQK, DV), lambda i: (i, 0, 0, 0)),
                   pl.BlockSpec((grp, HEADS, 1, DQK), lambda i: (i, 0, 0, 0)),
                   pl.BlockSpec((grp, HEADS, 1, 1), lambda i: (i, 0, 0, 0))],
        out_shape=[jax.ShapeDtypeStruct((r, MLSTM_W + CONV_W), BF16), jax.ShapeDtypeStruct((r, MLSTM_W), F32),
                   jax.ShapeDtypeStruct((nc, HEADS, DQK, DV), F32),
                   jax.ShapeDtypeStruct((nc, HEADS, 1, DQK), F32),
                   jax.ShapeDtypeStruct((nc, HEADS, 1, 1), F32)],
        scratch_shapes=[pltpu.VMEM((HEADS, DQK, DV), F32), pltpu.VMEM((HEADS, 1, DQK), F32),
                        pltpu.VMEM((HEADS, 1, 1), F32)],
        compiler_params=_params(),
    )(pm, bias, nw)


def _mlstm_bwd(name, dcat, pm, ht, cs, ns, ms, bias, nw):
    r = pm.shape[0]
    nc = r // CHUNK
    grp = _chunks_per_step(nc)
    nsteps = nc // grp

    def body(dy_ref, pm_ref, ht_ref, cs_ref, ns_ref, ms_ref, b_ref, nw_ref, dpm_ref, dnw_ref, db_ref,
             dc_scr, dn_scr):
        step = pl.program_id(0)

        @pl.when(step == 0)
        def _():
            dc_scr[...] = jnp.zeros_like(dc_scr)
            dn_scr[...] = jnp.zeros_like(dn_scr)
            dnw_ref[...] = jnp.zeros_like(dnw_ref)
            db_ref[...] = jnp.zeros_like(db_ref)

        rr = lax.broadcasted_iota(jnp.int32, (CHUNK, CHUNK), 0)
        cc = lax.broadcasted_iota(jnp.int32, (CHUNK, CHUNK), 1)
        tril, eye, triu = cc <= rr, cc == rr, cc >= rr
        lane = lax.broadcasted_iota(jnp.int32, (CHUNK, 128), 1)
        rowid = lax.broadcasted_iota(jnp.int32, (CHUNK, 1), 0)
        bv, nwv = b_ref[...], nw_ref[...]
        carried = [(dc_scr[h], dn_scr[h]) for h in range(HEADS)]
        dnw_acc = [jnp.zeros((1, DV), F32) for _ in range(HEADS)]
        db_acc = jnp.zeros((1, 128), F32)
        for g in reversed(range(grp)):
            rows = slice(g * CHUNK, (g + 1) * CHUNK)
            ci = (nsteps - 1 - step) * grp + g
            pmv = pm_ref[rows, :]
            th, z, li, lf, real = _gate_tiles(pmv[:, GATE_COL:GATE_COL + 128], bv, ci * CHUNK)
            heads = _interleave([
                _mlstm_bwd_head(h, pmv, ht_ref[rows, h * DV:(h + 1) * DV], dy_ref[rows, h * DV:(h + 1) * DV], nwv,
                                li, lf, cs_ref[g, h], ns_ref[g, h], ms_ref[g, h], carried[h][0], carried[h][1],
                                tril, eye, triu, lane, rowid, dpm_ref, rows)
                for h in range(HEADS)])
            carried = [(dc_new, dn_new) for _, dc_new, dn_new, _ in heads]
            dgt = heads[0][0] + heads[1][0] + heads[2][0] + heads[3][0]
            dnw_acc = [dnw_acc[h] + heads[h][3] for h in range(HEADS)]
            dact = jnp.where(lane < HEADS, 1.0, 1.0 - _sigmoid(z)) * (1.0 - th * th)
            dgraw = jnp.where(real & (lane < 2 * HEADS), dgt * dact, 0.0)
            dpm_ref[rows, GATE_COL:GATE_COL + 128] = dgraw.astype(BF16)
            db_acc = db_acc + jnp.sum(dgraw, axis=0, keepdims=True)
        for h, (dcn, dnn) in enumerate(carried):
            dc_scr[h] = dcn
            dn_scr[h] = dnn
            dnw_ref[:, h * DV:(h + 1) * DV] += dnw_acc[h]
        db_ref[...] += db_acc

    rev = lambda i: (nsteps - 1 - i, 0)
    rev4 = lambda i: (nsteps - 1 - i, 0, 0, 0)
    return pl.pallas_call(
        body, name=name, grid=(nsteps,),
        in_specs=[pl.BlockSpec((grp * CHUNK, MLSTM_W), rev), pl.BlockSpec((grp * CHUNK, PM_W), rev),
                  pl.BlockSpec((grp * CHUNK, MLSTM_W), rev),
                  pl.BlockSpec((grp, HEADS, DQK, DV), rev4), pl.BlockSpec((grp, HEADS, 1, DQK), rev4),
                  pl.BlockSpec((grp, HEADS, 1, 1), rev4),
                  pl.BlockSpec((1, 128), lambda i: (0, 0)), pl.BlockSpec((1, MLSTM_W), lambda i: (0, 0))],
        out_specs=[pl.BlockSpec((grp * CHUNK, PM_W), rev), pl.BlockSpec((1, MLSTM_W), lambda i: (0, 0)),
                   pl.BlockSpec((1, 128), lambda i: (0, 0))],
        out_shape=[jax.ShapeDtypeStruct((r, PM_W), BF16), jax.ShapeDtypeStruct((1, MLSTM_W), F32),
                   jax.ShapeDtypeStruct((1, 128), F32)],
        scratch_shapes=[pltpu.VMEM((HEADS, DQK, DV), F32), pltpu.VMEM((HEADS, 1, DQK), F32)],
        compiler_params=_params(),
    )(dcat, pm, ht, cs, ns, ms, bias, nw)


def _mlstm_bwd_head(h, pmv, hh, y, nwv, li, lf, cst, nst, mst, dcn, dnn, tril, eye, triu, lane, rowid,
                    dpm_ref, rows):
    f = yield from _chunk_common(pmv, h, li, lf, cst, nst, mst, tril, eye)
    q, k, v, s, p = f["q"], f["k"], f["v"], f["s"], f["p"]
    w_inter, w_in, w_old, dn = f["w_inter"], f["w_in"], f["w_old"], f["dn"]
    osl = slice(2 * QK_W + MLSTM_W + h * DV, 2 * QK_W + MLSTM_W + (h + 1) * DV)
    sg = _sigmoid(pmv[:, osl])
    yield
    rs = lax.rsqrt(jnp.mean(hh * hh, axis=1, keepdims=True) + EPS)
    yield
    nwh = nwv[:, h * DV:(h + 1) * DV]
    dpm_ref[rows, osl] = (y * (hh * rs * nwh) * sg * (1.0 - sg)).astype(BF16)
    yield
    dhn = y * sg
    dnw_h = jnp.sum(dhn * hh * rs, axis=0, keepdims=True)
    yield
    wd = dhn * nwh
    dhh = rs * wd - hh * (rs * rs * rs) * jnp.mean(hh * wd, axis=1, keepdims=True)
    yield
    dnum = dhh / dn
    dd = -jnp.sum(dhh * hh, axis=1, keepdims=True) / dn
    yield
    dden = jnp.where(jnp.abs(f["den"]) > jnp.exp(-f["mt"]), dd * jnp.sign(f["den"]), 0.0)
    dnum_b = dnum.astype(BF16)
    wdn = (w_inter * dnum).astype(BF16)
    wid = (w_inter * dden).astype(BF16).astype(F32)
    yield
    ds = _dot(dnum_b, v, NT) + dden
    yield
    dsp = (ds * p).astype(BF16)
    yield
    dq = _dot(dsp, k, NN) + _dot(wdn, f["cb"], NT) + wid * f["nb"]
    yield
    dk = _dot(dsp, q, TN)
    yield
    dv = _dot(s.astype(BF16), dnum_b, TN)
    yield
    g = ds * s
    g_col = _row_to_col(jnp.sum(g, axis=0, keepdims=True), eye)
    yield
    db = jnp.sum(g, axis=1, keepdims=True) - g_col
    dli = g_col
    yield
    db = db + (jnp.sum(dnum * f["qc"], axis=1, keepdims=True) + dden * f["qn"]) * w_inter
    yield
    dcnb = dcn.astype(BF16)
    dnnb = dnn.astype(BF16).astype(F32)
    dkw = _dot(v, dcnb, NT) + dnnb
    yield
    dk = dk + w_in * dkw
    dv = dv + _dot(f["kw"], dcnb, NN)
    yield
    ddecay = jnp.sum(dkw * f["kraw"], axis=1, keepdims=True) * w_in
    yield
    dw_old = (jnp.sum(jnp.sum(dcn * cst, axis=1, keepdims=True), axis=0, keepdims=True)
              + jnp.sum(dnn * nst, axis=1, keepdims=True))
    yield
    db_end = dw_old * w_old + jnp.sum(ddecay, axis=0, keepdims=True)
    db = db - ddecay + jnp.where(rowid == CHUNK - 1, db_end, 0.0)
    dli = dli + ddecay
    yield
    dc_new = w_old * dcn + _dot(q, wdn, TN)
    yield
    dn_new = w_old * dnn + jnp.sum(wid * q.astype(F32), axis=0, keepdims=True)
    yield
    dlf = jnp.sum(jnp.where(triu, _col_to_row(db, eye), 0.0), axis=1, keepdims=True)
    yield
    gate_part = jnp.where(lane == h, dli, 0.0) + jnp.where(lane == HEADS + h, dlf, 0.0)
    dpm_ref[rows, h * DQK:(h + 1) * DQK] = (dq * QSCALE).astype(BF16)
    yield
    dpm_ref[rows, QK_W + h * DQK:QK_W + (h + 1) * DQK] = dk.astype(BF16)
    yield
    dpm_ref[rows, 2 * QK_W + h * DV:2 * QK_W + (h + 1) * DV] = dv.astype(BF16)
    return gate_part, dc_new, dn_new, dnw_h


def _my_place():
    return lax.axis_index("x"), lax.axis_index("y"), lax.axis_index("c")


def _flip(v, bit):
    return 1 - v if bit else v


def _exchange_small(name, blk, reduce):
    r, c = blk.shape

    def body(x_ref, o_ref, *rest):
        slots = rest[0] if reduce else o_ref
        send_sems, recv_sems = rest[-2], rest[-1]
        x, y, cc = _my_place()
        me = 4 * x + 2 * y + cc
        slots[me] = x_ref[...]
        copies = []
        for k in range(1, N_DEV):
            peer = (_flip(x, k & 4), _flip(y, k & 2), _flip(cc, k & 1))
            cp = pltpu.make_async_remote_copy(
                src_ref=x_ref, dst_ref=slots.at[me], send_sem=send_sems.at[k - 1],
                recv_sem=recv_sems.at[k - 1], device_id=peer, device_id_type=MESH)
            cp.start()
            copies.append(cp)
        for cp in copies:
            cp.wait()
        if reduce:
            acc = slots[0]
            for d in range(1, N_DEV):
                acc = acc + slots[d]
            o_ref[...] = acc

    scratch = ([pltpu.VMEM((N_DEV, r, c), F32)] if reduce else []) + [
        pltpu.SemaphoreType.DMA((N_DEV - 1,)), pltpu.SemaphoreType.DMA((N_DEV - 1,))]
    return pl.pallas_call(
        body, name=name,
        out_shape=jax.ShapeDtypeStruct((r, c) if reduce else (N_DEV, r, c), F32),
        in_specs=[pl.BlockSpec(memory_space=pltpu.VMEM)], out_specs=pl.BlockSpec(memory_space=pltpu.VMEM),
        scratch_shapes=scratch, compiler_params=_params(),
    )(blk)


HBM_SPEC = pl.BlockSpec(memory_space=pltpu.HBM)
SEM_SPEC = pl.BlockSpec(memory_space=pltpu.SEMAPHORE)
ANY_SPEC = pl.BlockSpec(memory_space=pl.ANY)
DATAFLOW = pltpu.SideEffectType.DATAFLOW_SIDE_EFFECTING


def _split_copy(name, arrays, start=None, wait=None, after=None):
    results, token = _split_copies(name, [(arrays, start, wait)], after)
    return results[0][0], results[0][1], token


def _split_copies(name, jobs, after=None):
    operands, in_specs, out_shape, out_specs, aliases = [], [], [], [], {}
    in_at, out_at = [], []
    for arrays, start, wait in jobs:
        in_at.append(len(operands))
        operands += [pltpu.with_memory_space_constraint(a, pltpu.HBM) for a in arrays]
        in_specs += [HBM_SPEC] * len(arrays)
        if wait:
            operands += list(wait[1])
            in_specs += [SEM_SPEC, SEM_SPEC]
    if after is not None:
        operands.append(after)
        in_specs.append(ANY_SPEC)
    for j, (arrays, start, wait) in enumerate(jobs):
        out_at.append(len(out_shape))
        if start:
            out_shape += [pltpu.SemaphoreType.DMA((start[1],)), pltpu.SemaphoreType.DMA((start[1],))]
            out_specs += [SEM_SPEC, SEM_SPEC]
        for i, a in enumerate(arrays):
            aliases[in_at[j] + i] = len(out_shape)
            out_shape.append(pltpu.HBM(a.shape, a.dtype))
            out_specs.append(HBM_SPEC)
    any_start = any(start for _, start, _ in jobs)
    if any_start:
        out_shape.append(jax.ShapeDtypeStruct((8, 128), F32))
        out_specs.append(pl.BlockSpec(memory_space=pltpu.VMEM))
    n_in = len(operands)

    def body(*refs):
        for j, (arrays, start, wait) in enumerate(jobs):
            if wait:
                ins = refs[in_at[j]:in_at[j] + len(arrays)]
                at = in_at[j] + len(arrays)
                for cp in wait[0](ins, refs[at], refs[at + 1]):
                    cp.wait_send()
                    cp.wait_recv()
        for j, (arrays, start, wait) in enumerate(jobs):
            if start:
                ins = refs[in_at[j]:in_at[j] + len(arrays)]
                at = n_in + out_at[j]
                for cp in start[0](ins, refs[at], refs[at + 1]):
                    cp.start()
        if any_start:
            token = refs[n_in + len(out_shape) - 1]
            token[...] = jnp.zeros_like(token)

    outs = pl.pallas_call(
        body, name=name, in_specs=in_specs, out_specs=out_specs, out_shape=out_shape,
        input_output_aliases=aliases, compiler_params=pltpu.CompilerParams(has_side_effects=DATAFLOW),
    )(*operands)
    results = []
    for j, (arrays, start, wait) in enumerate(jobs):
        at = out_at[j]
        sems = (outs[at], outs[at + 1]) if start else None
        at += 2 if start else 0
        results.append((list(outs[at:at + len(arrays)]), sems))
    return results, (outs[-1] if any_start else None)


def _remote(src, dst, send_sems, recv_sems, k, to):
    return pltpu.make_async_remote_copy(src_ref=src, dst_ref=dst, send_sem=send_sems.at[k],
                                        recv_sem=recv_sems.at[k], device_id=to, device_id_type=MESH)


def _slot(px, py, pc):
    return 4 * px + 2 * py + pc


def _gather_first(refs, send_sems, recv_sems):
    x, y, c = _my_place()
    blk = refs[0].at[_slot(x, y, c)]
    targets = [(x, y, 1 - c), (1 - x, y, c), (x, 1 - y, c)]
    return [_remote(blk, blk, send_sems, recv_sems, k, to) for k, to in enumerate(targets)]


def _gather_relay(refs, send_sems, recv_sems):
    x, y, c = _my_place()
    rows = refs[0].shape[1]
    half = rows // 32 * 16
    from_x, from_y = _slot(1 - x, y, c), _slot(x, 1 - y, c)
    upper = refs[0].at[from_x, pl.ds(0, half)]
    lower = refs[0].at[from_y, pl.ds(half, rows - half)]
    return [_remote(upper, upper, send_sems, recv_sems, 0, (x, 1 - y, c)),
            _remote(lower, lower, send_sems, recv_sems, 1, (1 - x, y, c)),
            _remote(refs[0].at[from_x], refs[0].at[from_x], send_sems, recv_sems, 2, (x, y, 1 - c)),
            _remote(refs[0].at[from_y], refs[0].at[from_y], send_sems, recv_sems, 3, (x, y, 1 - c))]


def _gather_last(refs, send_sems, recv_sems):
    x, y, c = _my_place()
    blk = refs[0].at[_slot(1 - x, 1 - y, c)]
    return [_remote(blk, blk, send_sems, recv_sems, 0, (x, y, 1 - c))]


def _scatter_sibling(n):
    def copies(refs, send_sems, recv_sems):
        x, y, c = _my_place()
        return [_remote(refs[a].at[2 * j + 1 - c], refs[n + a].at[j], send_sems, recv_sems, 4 * a + j, (x, y, 1 - c))
                for a in range(n) for j in range(4)]
    return copies


def _scatter_chips(n):
    def copies(refs, send_sems, recv_sems):
        x, y, c = _my_place()
        out = []
        for a in range(n):
            for k in range(1, 4):
                px, py = _flip(x, k & 2), _flip(y, k & 1)
                out.append(_remote(refs[a].at[2 * px + py], refs[n + a].at[2 * x + y], send_sems, recv_sems,
                                   3 * a + k - 1, (px, py, c)))
        return out
    return copies


def _pair_sum(name, core, g, t):
    _, r, c = g.shape
    tr = _tile(r, 512, 8)
    g4 = g.reshape(4, 2, r, c)

    def body(core_ref, g_ref, t_ref, o_ref):
        o_ref[...] = (g_ref[...].astype(F32) + t_ref[...].astype(F32)).astype(BF16)

    return pl.pallas_call(
        body, name=name,
        grid_spec=pltpu.PrefetchScalarGridSpec(
            num_scalar_prefetch=1, grid=(4, r // tr),
            in_specs=[pl.BlockSpec((None, None, tr, c), lambda j, i, core_ref: (j, core_ref[0], i, 0)),
                      pl.BlockSpec((None, tr, c), lambda j, i, core_ref: (j, i, 0))],
            out_specs=pl.BlockSpec((None, tr, c), lambda j, i, core_ref: (j, i, 0))),
        out_shape=jax.ShapeDtypeStruct((4, r, c), BF16), compiler_params=_params(),
    )(core, g4, t)


def _adam_math(w, g, m, v):
    m2 = ADAM_B1 * m + (1.0 - ADAM_B1) * g
    v2 = ADAM_B2 * v + (1.0 - ADAM_B2) * (g * g)
    m_hat = m2 / (1.0 - ADAM_B1 ** ADAM_STEP)
    v_hat = v2 / (1.0 - ADAM_B2 ** ADAM_STEP)
    delta = -ADAM_LR * (m_hat / (jnp.sqrt(v_hat) + ADAM_EPS) + ADAM_WD * w)
    return delta, m2, v2


def _adam_sharded(name, chip, w, m, v, grads):
    _, r, c = w.shape
    tr = _tile(r, 256, 8)
    tc = c if tr < r else _tile(c, 256, 128)

    def body(chip_ref, w_ref, m_ref, v_ref, p0_ref, q0_ref, p1_ref, q1_ref, g_ref, d_ref, nm_ref, nv_ref):
        mine = chip_ref[0]

        def total(p_ref, q_ref):
            acc = None
            for j in range(4):
                part = jnp.where(mine == j, p_ref[...], q_ref[j]).astype(F32)
                acc = part if acc is None else acc + part
            return acc

        g = jnp.where(pl.program_id(0) == 0, total(p0_ref, q0_ref), total(p1_ref, q1_ref))
        delta, m2, v2 = _adam_math(w_ref[...], g, m_ref[...], v_ref[...])
        g_ref[...] = g
        d_ref[...] = delta
        nm_ref[...] = m2
        nv_ref[...] = v2

    def grad_specs(layer):
        at = lambda l, i, j: (jnp.where(l == layer, i, 0), jnp.where(l == layer, j, 0))
        return [pl.BlockSpec((None, tr, tc), lambda l, i, j, chip_ref: (chip_ref[0],) + at(l, i, j)),
                pl.BlockSpec((4, tr, tc), lambda l, i, j, chip_ref: (0,) + at(l, i, j))]

    wspec = pl.BlockSpec((None, tr, tc), lambda l, i, j, chip_ref: (l, i, j))
    sds = jax.ShapeDtypeStruct(w.shape, F32)
    return pl.pallas_call(
        body, name=name,
        grid_spec=pltpu.PrefetchScalarGridSpec(
            num_scalar_prefetch=1, grid=(2, r // tr, c // tc),
            in_specs=[wspec, wspec, wspec] + grad_specs(0) + grad_specs(1), out_specs=[wspec] * 4),
        out_shape=[sds] * 4, compiler_params=_params(),
    )(chip, w, m, v, grads[0][0], grads[0][1], grads[1][0], grads[1][1])


def _adam_small(name, w, m, v, g):
    def body(w_ref, m_ref, v_ref, g_ref, d_ref, nm_ref, nv_ref):
        delta, m2, v2 = _adam_math(w_ref[...], g_ref[...], m_ref[...], v_ref[...])
        d_ref[...] = delta
        nm_ref[...] = m2
        nv_ref[...] = v2

    sds = jax.ShapeDtypeStruct(w.shape, F32)
    vm = pl.BlockSpec(memory_space=pltpu.VMEM)
    return pl.pallas_call(body, name=name, in_specs=[vm] * 4, out_specs=[vm] * 3, out_shape=[sds] * 3,
                          compiler_params=_params())(w, m, v, g)


GATE_END = GATE_COL + 2 * HEADS


def _split_w_in(gathered):
    win_t = gathered.reshape(D_IN, D_MODEL)
    return win_t, win_t[GATE_END:].reshape(3, CONV_W, D_MODEL)


def _merge_dw_in(dwm_t, dwc_t):
    full = jnp.concatenate([dwm_t[:GATE_END], dwc_t.reshape(3 * CONV_W, D_MODEL)], axis=0)
    return full.reshape(N_DEV, IN_SH, D_MODEL)


def _pack128(parts):
    flat = jnp.concatenate([p.reshape(-1) for p in parts])
    n = flat.shape[0]
    rows = -(-n // 1024) * 8
    return jnp.pad(flat, (0, rows * 128 - n)).reshape(rows, 128)


def _unpack128(packed, shapes):
    flat = packed.reshape(-1)
    out, at = [], 0
    for s in shapes:
        n = int(np.prod(s))
        out.append(flat[at:at + n].reshape(s))
        at += n
    return out


def kernel(x, meta_tokens, norm_mix_w, w_in, b_gates, conv_w, mlstm_norm_w, w_out, norm_ffn_w, w_gate, w_up, w_down, norm_final_w, loss_target, m_meta_tokens, m_norm_mix_w, m_w_in, m_b_gates, m_conv_w, m_mlstm_norm_w, m_w_out, m_norm_ffn_w, m_w_gate, m_w_up, m_w_down, m_norm_final_w, v_meta_tokens, v_norm_mix_w, v_w_in, v_b_gates, v_conv_w, v_mlstm_norm_w, v_w_out, v_norm_ffn_w, v_w_gate, v_w_up, v_w_down, v_norm_final_w):
    seq = x.shape[1]
    rows = TOK0 + seq
    me = 4 * lax.axis_index("x") + 2 * lax.axis_index("y") + lax.axis_index("c")
    meta_sh = meta_tokens.shape[1]
    conv_sh = conv_w.shape[2]

    w_gate_t, m_w_gate_t, v_w_gate_t = (jnp.transpose(a, (0, 2, 1)) for a in (w_gate, m_w_gate, v_w_gate))
    w_up_t, m_w_up_t, v_w_up_t = (jnp.transpose(a, (0, 2, 1)) for a in (w_up, m_w_up, v_w_up))
    shards = []
    for l in range(DEPTH):
        shards += [jnp.transpose(w_in[l]).astype(BF16), w_out[l].astype(BF16), w_gate_t[l].astype(BF16),
                   w_up_t[l].astype(BF16), w_down[l].astype(BF16)]
    per_layer = ("w_in", "w_out", "w_gate", "w_up", "w_down")
    gather_state = {}

    def gather_step(tag, after, start=None, relay=None, last=None, done=()):
        jobs, idx = [], []
        if relay is not None and relay < len(shards):
            jobs.append((gather_state[relay][0], (_gather_relay, 4), (_gather_first, gather_state[relay][1])))
            idx.append(relay)
        if start is not None and start < len(shards):
            buf = lax.dynamic_update_index_in_dim(lax.empty((N_DEV,) + shards[start].shape, BF16), shards[start], me, 0)
            jobs.append(([buf], (_gather_first, 3), None))
            idx.append(start)
        if last is not None:
            jobs.append((gather_state[last][0], (_gather_last, 1), (_gather_relay, gather_state[last][1])))
            idx.append(last)
        for i in done:
            jobs.append((gather_state[i][0], None, (_gather_last, gather_state[i][1])))
            idx.append(i)
        if not jobs:
            return after, []
        results, tok = _split_copies(f"gather_{tag}", jobs, after)
        for i, res in zip(idx, results):
            gather_state[i] = res
        return (after if tok is None else tok), [gather_state[i][0][0] for i in done]

    bias = [jnp.pad(b_gates[l].reshape(1, 2 * HEADS), ((0, 0), (0, 128 - 2 * HEADS))) for l in range(DEPTH)]
    nmix = [norm_mix_w[l].reshape(1, D_MODEL) for l in range(DEPTH)]
    nffn = [norm_ffn_w[l].reshape(1, D_MODEL) for l in range(DEPTH)]
    nmls = [mlstm_norm_w[l].reshape(1, MLSTM_W) for l in range(DEPTH)]
    weights = [dict() for _ in range(DEPTH)]
    saved = [dict() for _ in range(DEPTH)]

    def layer_fwd(l, h, after):
        w, s = weights[l], saved[l]
        k0 = len(per_layer) * l
        tok, _ = gather_step(f"l{l}_a", after, last=k0)
        _, (g_in,) = gather_step(f"l{l}_b", tok, done=[k0])
        tok, _ = gather_step(f"l{l}_c", g_in, relay=k0 + 1, start=k0 + 3)
        w["win_t"], w["wc_t"] = _split_w_in(g_in)
        s["h0"] = h
        s["hn"], s["pm"] = _norm_proj(f"proj_mlstm_{l}", h, nmix[l] + tok[0, 0], w["win_t"], PM_W)
        tok, _ = gather_step(f"l{l}_d", s["pm"], relay=k0 + 2, start=k0 + 4)
        tok, _ = gather_step(f"l{l}_d2", tok, last=k0 + 1)
        s["pc"] = _mm_nt_bcols(f"proj_conv_{l}", s["hn"], w["wc_t"], F32, dep=tok)
        hm, s["ht"], s["cs"], s["ns"], s["ms"] = _mlstm_fwd(f"mlstm_fwd_{l}", s["pm"], bias[l] + tok[:1], nmls[l])
        tok, _ = gather_step(f"l{l}_e", hm, relay=k0 + 3, start=k0 + 5)
        tok, _ = gather_step(f"l{l}_e2", tok, last=k0 + 2)
        s["cat"] = _conv_fwd(f"conv_fwd_{l}", s["pc"], conv_rows[l] + tok[0, 0], hm)
        _, (g_out,) = gather_step(f"l{l}_f", s["cat"], done=[k0 + 1])
        w["wo"] = g_out.reshape(D_MODEL, D_MODEL)
        s["h1"], s["hf"] = _proj_res_norm(f"out_proj_{l}", s["cat"], w["wo"], s["h0"], nffn[l])
        tok_g, _ = gather_step(f"l{l}_g", s["h1"], relay=k0 + 4, start=k0 + 6)
        tok, _ = gather_step(f"l{l}_h", tok_g, last=k0 + 3)
        _, (g_gate, g_up) = gather_step(f"l{l}_i", tok, done=[k0 + 2, k0 + 3])
        w["wg_t"] = g_gate.reshape(D_FF, D_MODEL)
        w["wu_t"] = g_up.reshape(D_FF, D_MODEL)
        s["g"], s["u"], s["act"] = _ffn_in(f"ffn_in_{l}", s["hf"], w["wg_t"], w["wu_t"], dep=tok_g)
        tok, _ = gather_step(f"l{l}_j", s["act"], last=k0 + 4)
        _, (g_down,) = gather_step(f"l{l}_k", tok, done=[k0 + 4])
        w["wd"] = g_down.reshape(D_FF, D_MODEL)
        tok, _ = gather_step(f"l{l}_k2", tok, relay=k0 + 5, start=k0 + 7)
        return _mm_nn(f"ffn_out_{l}", s["act"], w["wd"], F32, res=s["h1"], dep=tok)

    tok, _ = gather_step("first", None, start=0)
    zero = tok[0, 0]
    small = jnp.concatenate(
        [meta_tokens + zero, jnp.pad(conv_w.reshape(DEPTH * 3, conv_sh), ((0, 2), (0, meta_sh - conv_sh)))], axis=0)
    slots = _exchange_small("gather_small", small, reduce=False)
    meta_full = jnp.transpose(slots[:, :N_META, :], (1, 0, 2)).reshape(N_META, D_MODEL)
    conv_full = jnp.transpose(slots[:, N_META:N_META + DEPTH * 3, :conv_sh], (1, 0, 2)).reshape(DEPTH, 3, CONV_W)
    conv_rows = [jnp.pad(conv_full[l], ((0, 5), (0, 0))) for l in range(DEPTH)]
    w_in_t, m_w_in_t, v_w_in_t = (jnp.transpose(a + zero, (0, 2, 1)) for a in (w_in, m_w_in, v_w_in))
    tok, w_in_t, m_w_in_t, v_w_in_t, meta_full = lax.optimization_barrier(
        (tok, w_in_t, m_w_in_t, v_w_in_t, meta_full))
    tok, _ = gather_step("pre_a", tok, relay=0)
    tok, _ = gather_step("pre_b", tok, start=1)
    tok, _ = gather_step("pre_c", tok, start=2)
    h = jnp.concatenate([jnp.zeros((PAD_FRONT, D_MODEL), F32), meta_full, x[0]], axis=0)
    h = layer_fwd(0, h, tok)
    h = layer_fwd(1, h, h)

    dh, dh_b, d_final, loss_part = _final_loss("final_loss", h, norm_final_w.reshape(1, D_MODEL), loss_target[0])

    core = lax.axis_index("c").astype(jnp.int32).reshape(1)
    chip = (2 * lax.axis_index("x") + lax.axis_index("y")).astype(jnp.int32).reshape(1)
    scatter_state = {}

    def scatter_begin(nm, grad):
        land = lax.empty((4,) + grad.shape[1:], BF16)
        arrs, sems, tok = _split_copy(f"grad_sibling_start_{nm}", [grad, land], start=(_scatter_sibling(1), 4))
        scatter_state[nm] = (arrs, sems)
        return tok

    def scatter_advance(nm, after):
        arrs, sems = scatter_state[nm]
        arrs, _, _ = _split_copy(f"grad_sibling_done_{nm}", arrs, wait=(_scatter_sibling(1), sems), after=after)
        part = _pair_sum(f"grad_pair_sum_{nm}", core, arrs[0], arrs[1])
        arrs, sems, tok = _split_copy(f"grad_chips_start_{nm}", [part, lax.empty(part.shape, BF16)],
                                      start=(_scatter_chips(1), 3))
        scatter_state[nm] = (arrs, sems)
        return tok

    def scattered(nm, after):
        arrs, sems = scatter_state[nm]
        arrs, _, _ = _split_copy(f"grad_chips_done_{nm}", arrs, wait=(_scatter_chips(1), sems), after=after)
        return arrs[0], arrs[1]

    d_mix, d_ffn, d_mls, d_bias, d_conv = ([None] * DEPTH for _ in range(5))

    def layer_bwd(l, dh, dh_b, tok):
        w, s = weights[l], saved[l]
        dg, du = _ffn_act_bwd(f"d_act_{l}", dh_b, w["wd"], s["g"], s["u"], dep=tok)
        dw_down = _mm_tn(f"dw_down_{l}", s["act"], dh_b, BF16, tm=1408, tn=1024)
        tok = scatter_begin(f"w_down_{l}", dw_down.reshape(N_DEV, FF_SH, D_MODEL))
        dhf = _mm_nn(f"d_ffn_gate_{l}", dg, w["wg_t"], F32, dep=tok)
        tok = scatter_advance(f"w_down_{l}", after=dhf)
        dhf = _mm_nn(f"d_ffn_up_{l}", du, w["wu_t"], F32, res=dhf, dep=tok)
        dw_gate = _mm_tn(f"dw_gate_{l}", dg, s["hf"], BF16, tm=1408, tn=1024)
        tok = scatter_begin(f"w_gate_{l}", dw_gate.reshape(N_DEV, FF_SH, D_MODEL))
        dw_up = _mm_tn(f"dw_up_{l}", du, s["hf"], BF16, tm=1408, tn=1024, dep=tok)
        tok = scatter_begin(f"w_up_{l}", dw_up.reshape(N_DEV, FF_SH, D_MODEL))
        dh1, dh1_b, d_ffn[l] = _rms_bwd(f"norm_ffn_bwd_{l}", s["h1"], nffn[l] + tok[0, 0], dhf, dh)
        tok = scatter_advance(f"w_gate_{l}", after=dh1)
        dcat = _mm_nt(f"d_cat_{l}", dh1_b, w["wo"], F32, tk=D_MODEL, dep=tok)
        tok = scatter_advance(f"w_up_{l}", after=dcat)
        dw_out = _mm_tn(f"dw_out_{l}", s["cat"], dh1_b, BF16, tn=1024, dep=tok)
        tok = scatter_begin(f"w_out_{l}", dw_out.reshape(N_DEV, OUT_SH, D_MODEL))
        dpm, d_mls[l], d_bias[l] = _mlstm_bwd(f"mlstm_bwd_{l}", dcat, s["pm"], s["ht"], s["cs"], s["ns"],
                                               s["ms"], bias[l] + tok[:1], nmls[l])
        dpc, d_conv[l] = _conv_bwd(f"conv_bwd_{l}", dcat, s["pc"], conv_rows[l])
        tok = scatter_advance(f"w_out_{l}", after=dpc)
        dwm_t = _mm_tn(f"dw_mlstm_{l}", dpm, s["hn"], BF16, tm=640, tn=1024, dep=tok)
        dwc_t = _mm_tn_acols(f"dw_conv_{l}", dpc, s["hn"], BF16)
        tok = scatter_begin(f"w_in_{l}", _merge_dw_in(dwm_t, dwc_t))
        dhn = _mm_nn_two(f"d_norm_{l}", dpm, w["win_t"], dpc, w["wc_t"], dep=tok)
        tok = scatter_advance(f"w_in_{l}", after=dhn)
        dh, dh_b, d_mix[l] = _rms_bwd(f"norm_mix_bwd_{l}", s["h0"], nmix[l] + tok[0, 0], dhn, dh1)
        return dh, dh_b, tok

    dh, dh_b, tok = layer_bwd(1, dh, dh_b, None)
    dh, dh_b, tok_tail = layer_bwd(0, dh, dh_b, tok)

    pq = {}
    after = dh
    for l in reversed(range(DEPTH)):
        for nm in ("w_down", "w_gate", "w_up", "w_out", "w_in"):
            if (nm, l) != ("w_in", 0):
                pq[nm, l] = scattered(f"{nm}_{l}", after)
                after = pq[nm, l][0]
    untransposed = lambda outs: [jnp.transpose(o, (0, 2, 1)) for o in outs]
    g_out, d_out, nm_out, nv_out = _adam_sharded(
        "adam_w_out", chip, w_out, m_w_out, v_w_out, [pq["w_out", 0], pq["w_out", 1]])
    g_gate, d_gate, nm_gate, nv_gate = untransposed(_adam_sharded(
        "adam_w_gate", chip, w_gate_t, m_w_gate_t, v_w_gate_t, [pq["w_gate", 0], pq["w_gate", 1]]))
    g_up, d_up, nm_up, nv_up = untransposed(_adam_sharded(
        "adam_w_up", chip, w_up_t, m_w_up_t, v_w_up_t, [pq["w_up", 0], pq["w_up", 1]]))
    g_down, d_down, nm_down, nv_down = _adam_sharded(
        "adam_w_down", chip, w_down, m_w_down, v_w_down, [pq["w_down", 0], pq["w_down", 1]])
    pq["w_in", 0] = scattered("w_in_0", nv_down)
    g_in, d_in, nm_in, nv_in = untransposed(_adam_sharded(
        "adam_w_in", chip, w_in_t, m_w_in_t, v_w_in_t, [pq["w_in", 0], pq["w_in", 1]]))

    bg = jnp.concatenate([d_bias[l][0, :2 * HEADS] for l in range(DEPTH)])
    red_in = jnp.concatenate([
        dh[PAD_FRONT:TOK0], d_mix[0], d_mix[1], d_ffn[0], d_ffn[1], d_final,
        jnp.concatenate([d_mls[0], d_mls[1]], axis=1),
        jnp.stack([d_conv[l][:3] for l in range(DEPTH)]).reshape(3, 2 * CONV_W),
        jnp.pad(bg, (0, D_MODEL - bg.shape[0])).reshape(1, D_MODEL),
        jnp.pad(loss_part[:, :1], ((0, 0), (0, D_MODEL - 1))),
        jnp.zeros((5, D_MODEL), F32) + tok_tail[0, 0]], axis=0)
    red = _exchange_small("reduce_small", red_in, reduce=True)
    loss = red[26, 0]
    g_meta = lax.dynamic_slice_in_dim(red[:N_META], me * meta_sh, meta_sh, axis=1)
    g_mix, g_ffn, g_final = red[16:18], red[18:20], red[20]
    g_mls = red[21].reshape(DEPTH, MLSTM_W)
    g_conv = lax.dynamic_slice_in_dim(red[22:25].reshape(DEPTH, 3, CONV_W), me * conv_sh, conv_sh, axis=2)
    g_bias = red[25, :DEPTH * 2 * HEADS].reshape(DEPTH, 2 * HEADS)

    small_w = [meta_tokens, norm_mix_w, b_gates, conv_w, mlstm_norm_w, norm_ffn_w, norm_final_w]
    small_m = [m_meta_tokens, m_norm_mix_w, m_b_gates, m_conv_w, m_mlstm_norm_w, m_norm_ffn_w, m_norm_final_w]
    small_v = [v_meta_tokens, v_norm_mix_w, v_b_gates, v_conv_w, v_mlstm_norm_w, v_norm_ffn_w, v_norm_final_w]
    small_g = [g_meta, g_mix, g_bias, g_conv, g_mls, g_ffn, g_final]
    shapes = [a.shape for a in small_w]
    packed = _adam_small("adam_small", _pack128(small_w), _pack128(small_m), _pack128(small_v), _pack128(small_g))
    (d_meta, d_nmix, d_bg, d_cw, d_nmls, d_nffn, d_nfin), (nm_meta, nm_nmix, nm_bg, nm_cw, nm_nmls, nm_nffn, nm_nfin), \
        (nv_meta, nv_nmix, nv_bg, nv_cw, nv_nmls, nv_nffn, nv_nfin) = (_unpack128(p, shapes) for p in packed)

    grad_x = dh[TOK0:].reshape(1, seq, D_MODEL)
    return (loss, grad_x,
            g_meta, g_mix, g_in, g_bias, g_conv, g_mls, g_out, g_ffn, g_gate, g_up, g_down, g_final,
            d_meta, d_nmix, d_in, d_bg, d_cw, d_nmls, d_out, d_nffn, d_gate, d_up, d_down, d_nfin,
            nm_meta, nm_nmix, nm_in, nm_bg, nm_cw, nm_nmls, nm_out, nm_nffn, nm_gate, nm_up, nm_down, nm_nfin,
            nv_meta, nv_nmix, nv_in, nv_bg, nv_cw, nv_nmls, nv_out, nv_nffn, nv_gate, nv_up, nv_down, nv_nfin)
```

```python
import numpy as np
import jax
import jax.numpy as jnp
from jax import lax
from jax.experimental import pallas as pl
from jax.experimental.pallas import tpu as pltpu

F32 = jnp.float32
BF16 = jnp.bfloat16
MESH = pl.DeviceIdType.MESH

D_MODEL = 2048
DEPTH = 2
N_META = 16
MLSTM_W = 1024
CONV_W = 1024
HEADS = 4
DV = 256
DQK = 128
QK_W = 512
CHUNK = 64
PAD_FRONT = 48
TOK0 = PAD_FRONT + N_META
D_FF = 5632
N_DEV = 8
FF_SH = D_FF // N_DEV
D_IN = 6152
IN_SH = D_IN // N_DEV
OUT_SH = D_MODEL // N_DEV
GATE_COL = 3072
PM_W = GATE_COL + 128
GATE_CAP = 15.0
EPS = 1e-6
QSCALE = DQK ** -0.5

ADAM_LR = 0.001
ADAM_B1 = 0.9
ADAM_B2 = 0.999
ADAM_EPS = 1e-08
ADAM_WD = 0.01
ADAM_STEP = 10

V7X_VMEM_LIMIT = 50 * 1024 * 1024
V7X_MXU_COLS = 256


def _params(**kw):
    return pltpu.CompilerParams(vmem_limit_bytes=V7X_VMEM_LIMIT, **kw)


def _tile(n, target, mult):
    best = None
    for t in range(mult, min(n, target) + 1, mult):
        if n % t == 0:
            best = t
    return best if best is not None else n


def _sigmoid(x):
    return 1.0 / (1.0 + jnp.exp(-x))


NN = ((1,), (0,))
NT = ((1,), (1,))
TN = ((0,), (0,))


def _matmul(name, a, b, out_shape, out_dtype, grid, a_bs, b_bs, o_bs, dims, nk, acc_shape=None,
            res=None, res_bs=None, dep=None):
    has_res = res is not None
    n_in = 2 + has_res + (dep is not None)

    def body(*refs):
        a_ref, b_ref = refs[0], refs[1]
        r_ref = refs[2] if has_res else None
        o_ref = refs[n_in]
        x = lax.dot_general(a_ref[...], b_ref[...], (dims, ((), ())), preferred_element_type=F32)
        if nk == 1:
            if has_res:
                x = x + r_ref[...]
            o_ref[...] = x.astype(o_ref.dtype)
            return
        acc = refs[n_in + 1]
        k = pl.program_id(len(grid) - 1)

        @pl.when(k == 0)
        def _():
            acc[...] = (x + r_ref[...]) if has_res else x

        @pl.when(k > 0)
        def _():
            acc[...] += x

        @pl.when(k == nk - 1)
        def _():
            o_ref[...] = acc[...].astype(o_ref.dtype)

    ins = [a, b] + ([res] if has_res else [])
    specs = [a_bs, b_bs] + ([res_bs] if has_res else [])
    if dep is not None:
        ins.append(dep)
        specs.append(pl.BlockSpec((8, 128), lambda *_: (0, 0)))
    scratch = [pltpu.VMEM(acc_shape, F32)] if nk > 1 else []
    return pl.pallas_call(
        body, name=name, grid=grid, in_specs=specs, out_specs=o_bs,
        out_shape=jax.ShapeDtypeStruct(out_shape, out_dtype), scratch_shapes=scratch,
        compiler_params=_params(),
    )(*ins)


def _mm_nn(name, a, b, out_dtype, res=None, tm=1056, tn=512, dep=None):
    r, k = a.shape
    n = b.shape[1]
    tm, tn = _tile(r, tm, 8), _tile(n, tn, 128)
    return _matmul(name, a, b, (r, n), out_dtype, (r // tm, n // tn, 1),
                   pl.BlockSpec((tm, k), lambda i, j, s: (i, 0)),
                   pl.BlockSpec((k, tn), lambda i, j, s: (0, j)),
                   pl.BlockSpec((tm, tn), lambda i, j, s: (i, j)), NN, 1,
                   res=res, res_bs=pl.BlockSpec((tm, tn), lambda i, j, s: (i, j)), dep=dep)


def _mm_nn_two(name, a, b, a3, b3, dep=None, tm=1056, tn=512):
    r, k = a.shape
    e, _, kb = a3.shape
    n = b.shape[1]
    tm, tn = _tile(r, tm, 8), _tile(n, tn, 128)

    def body(a_ref, b_ref, a3_ref, b3_ref, *rest):
        acc = lax.dot_general(a_ref[...], b_ref[...], (NN, ((), ())), preferred_element_type=F32)
        for s in range(e):
            acc = acc + lax.dot_general(a3_ref[s], b3_ref[s], (NN, ((), ())), preferred_element_type=F32)
        rest[-1][...] = acc

    ins = [a, b, a3, b3]
    specs = [pl.BlockSpec((tm, k), lambda i, j: (i, 0)), pl.BlockSpec((k, tn), lambda i, j: (0, j)),
             pl.BlockSpec((e, tm, kb), lambda i, j: (0, i, 0)), pl.BlockSpec((e, kb, tn), lambda i, j: (0, 0, j))]
    if dep is not None:
        ins.append(dep)
        specs.append(pl.BlockSpec((8, 128), lambda *_: (0, 0)))
    return pl.pallas_call(
        body, name=name, grid=(r // tm, n // tn), in_specs=specs,
        out_specs=pl.BlockSpec((tm, tn), lambda i, j: (i, j)),
        out_shape=jax.ShapeDtypeStruct((r, n), F32), compiler_params=_params(),
    )(*ins)


def _mm_nt(name, a, b, out_dtype, res=None, tm=1056, tn=512, tk=640, n=None, dep=None):
    r, k = a.shape
    n = b.shape[0] if n is None else n
    tm, tn, tk = _tile(r, tm, 8), _tile(n, tn, 128), _tile(k, tk, 128)
    nk = k // tk
    return _matmul(name, a, b, (r, n), out_dtype, (r // tm, n // tn, nk),
                   pl.BlockSpec((tm, tk), lambda i, j, s: (i, s)),
                   pl.BlockSpec((tn, tk), lambda i, j, s: (j, s)),
                   pl.BlockSpec((tm, tn), lambda i, j, s: (i, j)), NT, nk, acc_shape=(tm, tn),
                   res=res, res_bs=pl.BlockSpec((tm, tn), lambda i, j, s: (i, j)), dep=dep)


def _mm_nt_bcols(name, a, b3, out_dtype, tm=1056, dep=None):
    r, k = a.shape
    e, n, _ = b3.shape
    tm = _tile(r, tm, 8)
    return _matmul(name, a, b3, (e, r, n), out_dtype, (r // tm, e, 1),
                   pl.BlockSpec((tm, k), lambda i, g, s: (i, 0)),
                   pl.BlockSpec((None, n, k), lambda i, g, s: (g, 0, 0)),
                   pl.BlockSpec((None, tm, n), lambda i, g, s: (g, i, 0)), NT, 1, dep=dep)


def _mm_tn(name, a, b, out_dtype, tm=1024, tn=640, dep=None):
    r, m = a.shape
    n = b.shape[1]
    tm, tn = _tile(m, tm, 128), _tile(n, tn, 128)
    return _matmul(name, a, b, (m, n), out_dtype, (m // tm, n // tn, 1),
                   pl.BlockSpec((r, tm), lambda i, j, s: (0, i)),
                   pl.BlockSpec((r, tn), lambda i, j, s: (0, j)),
                   pl.BlockSpec((tm, tn), lambda i, j, s: (i, j)), TN, 1, dep=dep)


def _mm_tn_acols(name, a3, b, out_dtype, tn=1024, dep=None):
    e, r, m = a3.shape
    n = b.shape[1]
    tn = _tile(n, tn, 128)
    return _matmul(name, a3, b, (e, m, n), out_dtype, (n // tn, e, 1),
                   pl.BlockSpec((None, r, m), lambda j, g, s: (g, 0, 0)),
                   pl.BlockSpec((r, tn), lambda j, g, s: (0, j)),
                   pl.BlockSpec((None, m, tn), lambda j, g, s: (g, 0, j)), TN, 1, dep=dep)


def _norm_proj(name, h, w, b, n, tm=1056, tn=640):
    r, d = h.shape
    tm, tn = _tile(r, tm, 8), _tile(n, tn, 128)

    def body(h_ref, w_ref, b_ref, hn_ref, o_ref):
        @pl.when(pl.program_id(1) == 0)
        def _():
            x = h_ref[...]
            rs = lax.rsqrt(jnp.mean(x * x, axis=1, keepdims=True) + EPS)
            hn_ref[...] = (x * rs * w_ref[...]).astype(BF16)

        o_ref[...] = lax.dot_general(hn_ref[...], b_ref[...], (NT, ((), ())), preferred_element_type=F32)

    row = pl.BlockSpec((tm, d), lambda i, j: (i, 0))
    return pl.pallas_call(
        body, name=name, grid=(r // tm, n // tn),
        in_specs=[row, pl.BlockSpec((1, d), lambda i, j: (0, 0)), pl.BlockSpec((tn, d), lambda i, j: (j, 0))],
        out_specs=[row, pl.BlockSpec((tm, tn), lambda i, j: (i, j))],
        out_shape=[jax.ShapeDtypeStruct((r, d), BF16), jax.ShapeDtypeStruct((r, n), F32)],
        compiler_params=_params(),
    )(h, w, b)


def _proj_res_norm(name, a, b, res, w, tm=528):
    r, k = a.shape
    d = b.shape[1]
    tm = _tile(r, tm, 8)

    def body(a_ref, b_ref, r_ref, w_ref, y_ref, n_ref):
        y = lax.dot_general(a_ref[...], b_ref[...], (NN, ((), ())), preferred_element_type=F32) + r_ref[...]
        y_ref[...] = y
        rs = lax.rsqrt(jnp.mean(y * y, axis=1, keepdims=True) + EPS)
        n_ref[...] = (y * rs * w_ref[...]).astype(BF16)

    row = pl.BlockSpec((tm, d), lambda i: (i, 0))
    return pl.pallas_call(
        body, name=name, grid=(r // tm,),
        in_specs=[pl.BlockSpec((tm, k), lambda i: (i, 0)), pl.BlockSpec((k, d), lambda i: (0, 0)), row,
                  pl.BlockSpec((1, d), lambda i: (0, 0))],
        out_specs=[row, row],
        out_shape=[jax.ShapeDtypeStruct((r, d), F32), jax.ShapeDtypeStruct((r, d), BF16)],
        compiler_params=_params(),
    )(a, b, res, w)


def _rms_bwd(name, x, w, dy, dres):
    r, d = x.shape
    tr = _tile(r, 264, 8)

    def body(x_ref, w_ref, dy_ref, dr_ref, dx_ref, dxb_ref, dw_ref):
        xv = x_ref[...]
        g = dy_ref[...]
        rs = lax.rsqrt(jnp.mean(xv * xv, axis=1, keepdims=True) + EPS)
        wg = g * w_ref[...]
        dx = rs * wg - xv * (rs * rs * rs) * jnp.mean(xv * wg, axis=1, keepdims=True) + dr_ref[...]
        dx_ref[...] = dx
        dxb_ref[...] = dx.astype(BF16)
        part = jnp.sum(g * xv * rs, axis=0, keepdims=True)

        @pl.when(pl.program_id(0) == 0)
        def _():
            dw_ref[...] = part

        @pl.when(pl.program_id(0) > 0)
        def _():
            dw_ref[...] += part

    row = pl.BlockSpec((tr, d), lambda i: (i, 0))
    vec = pl.BlockSpec((1, d), lambda i: (0, 0))
    return pl.pallas_call(
        body, name=name, grid=(r // tr,), in_specs=[row, vec, row, row], out_specs=[row, row, vec],
        out_shape=[jax.ShapeDtypeStruct((r, d), F32), jax.ShapeDtypeStruct((r, d), BF16),
                   jax.ShapeDtypeStruct((1, d), F32)],
        compiler_params=_params(),
    )(x, w, dy, dres)


def _final_loss(name, h, w, target):
    r, d = h.shape
    nb = r // CHUNK

    def body(h_ref, w_ref, t_ref, dh_ref, dhb_ref, dw_ref, ls_ref):
        i = pl.program_id(0)

        @pl.when(i == 0)
        def _():
            dh_ref[...] = jnp.zeros_like(dh_ref)
            dhb_ref[...] = jnp.zeros_like(dhb_ref)
            dw_ref[...] = jnp.zeros_like(dw_ref)
            ls_ref[...] = jnp.zeros_like(ls_ref)

        @pl.when(i > 0)
        def _():
            xv = h_ref[...]
            wv = w_ref[...]
            rs = lax.rsqrt(jnp.mean(xv * xv, axis=1, keepdims=True) + EPS)
            err = xv * rs * wv - t_ref[...]
            sq = jnp.sum(jnp.sum(err * err, axis=1, keepdims=True), axis=0, keepdims=True)
            ls_ref[...] += jnp.broadcast_to(sq * (0.5 / d), ls_ref.shape)
            g = err * (1.0 / d)
            wg = g * wv
            dx = rs * wg - xv * (rs * rs * rs) * jnp.mean(xv * wg, axis=1, keepdims=True)
            dh_ref[...] = dx
            dhb_ref[...] = dx.astype(BF16)
            dw_ref[...] += jnp.sum(g * xv * rs, axis=0, keepdims=True)

    row = pl.BlockSpec((CHUNK, d), lambda i: (i, 0))
    vec = pl.BlockSpec((1, d), lambda i: (0, 0))
    return pl.pallas_call(
        body, name=name, grid=(nb,),
        in_specs=[row, vec, pl.BlockSpec((CHUNK, d), lambda i: (jnp.maximum(i - 1, 0), 0))],
        out_specs=[row, row, vec, pl.BlockSpec((1, 128), lambda i: (0, 0))],
        out_shape=[jax.ShapeDtypeStruct((r, d), F32), jax.ShapeDtypeStruct((r, d), BF16),
                   jax.ShapeDtypeStruct((1, d), F32), jax.ShapeDtypeStruct((1, 128), F32)],
        compiler_params=_params(),
    )(h, w, target)


def _ffn_in(name, hf, wg_t, wu_t, dep=None, tm=1056, tn=512):
    r, d = hf.shape
    f = wg_t.shape[0]
    tm, tn = _tile(r, tm, 8), _tile(f, tn, 128)

    def body(h_ref, wg_ref, wu_ref, *rest):
        g_ref, u_ref, a_ref = rest[-3:]
        x = h_ref[...]
        g = lax.dot_general(x, wg_ref[...], (NT, ((), ())), preferred_element_type=F32)
        u = lax.dot_general(x, wu_ref[...], (NT, ((), ())), preferred_element_type=F32)
        g_ref[...] = g.astype(BF16)
        u_ref[...] = u.astype(BF16)
        a_ref[...] = (g * _sigmoid(g) * u).astype(BF16)

    wspec = pl.BlockSpec((tn, d), lambda i, j: (j, 0))
    ospec = pl.BlockSpec((tm, tn), lambda i, j: (i, j))
    ins, specs = [hf, wg_t, wu_t], [pl.BlockSpec((tm, d), lambda i, j: (i, 0)), wspec, wspec]
    if dep is not None:
        ins.append(dep)
        specs.append(pl.BlockSpec((8, 128), lambda *_: (0, 0)))
    return pl.pallas_call(
        body, name=name, grid=(r // tm, f // tn), in_specs=specs, out_specs=[ospec] * 3,
        out_shape=[jax.ShapeDtypeStruct((r, f), BF16)] * 3, compiler_params=_params(),
    )(*ins)


def _ffn_act_bwd(name, dh, wd, g, u, dep=None, tm=1056, tn=512):
    r, d = dh.shape
    f = wd.shape[0]
    tm, tn = _tile(r, tm, 8), _tile(f, tn, 128)

    def body(dh_ref, wd_ref, g_ref, u_ref, *rest):
        dg_ref, du_ref = rest[-2:]
        tr = _tile(tm, 264, 8)
        for r0 in range(0, tm, tr):
            for c0 in range(0, tn, V7X_MXU_COLS):
                rows, cols = slice(r0, r0 + tr), slice(c0, c0 + V7X_MXU_COLS)
                da = lax.dot_general(dh_ref[rows, :], wd_ref[cols, :], (NT, ((), ())), preferred_element_type=F32)
                gv = g_ref[rows, cols]
                s = _sigmoid(gv)
                t = da.astype(BF16) * s
                du_ref[rows, cols] = t * gv
                dg_ref[rows, cols] = t * u_ref[rows, cols] * (1.0 + gv - gv * s)

    tile = pl.BlockSpec((tm, tn), lambda i, j: (i, j))
    ins = [dh, wd, g, u]
    specs = [pl.BlockSpec((tm, d), lambda i, j: (i, 0)), pl.BlockSpec((tn, d), lambda i, j: (j, 0)), tile, tile]
    if dep is not None:
        ins.append(dep)
        specs.append(pl.BlockSpec((8, 128), lambda *_: (0, 0)))
    return pl.pallas_call(
        body, name=name, grid=(r // tm, f // tn), in_specs=specs, out_specs=[tile] * 2,
        out_shape=[jax.ShapeDtypeStruct((r, f), BF16)] * 2, compiler_params=_params(),
    )(*ins)


def _shift_down(a, k):
    row = lax.broadcasted_iota(jnp.int32, a.shape, 0)
    return jnp.where(row >= k, pltpu.roll(a, k, 0), 0.0)


def _shift_up(a, k):
    n = a.shape[0]
    row = lax.broadcasted_iota(jnp.int32, a.shape, 0)
    return jnp.where(row < n - k, pltpu.roll(a, n - k, 0), 0.0)


def _conv_fwd(name, pc, cw, cat):
    _, r, w = pc.shape
    nblk = w // 128

    def body(pc_ref, cw_ref, cat_ref, o_ref):
        a = pc_ref[2] * pc_ref[0]
        cwv = cw_ref[...]
        conv = _shift_down(a, 2) * cwv[0:1] + _shift_down(a, 1) * cwv[1:2] + a * cwv[2:3]
        o_ref[...] = (pc_ref[1] * conv).astype(BF16)

    return pl.pallas_call(
        body, name=name, grid=(nblk,),
        in_specs=[pl.BlockSpec((3, r, 128), lambda j: (0, 0, j)), pl.BlockSpec((8, 128), lambda j: (0, j)),
                  pl.BlockSpec(memory_space=pl.ANY)],
        out_specs=pl.BlockSpec((r, 128), lambda j: (0, nblk + j)),
        out_shape=jax.ShapeDtypeStruct(cat.shape, BF16), input_output_aliases={2: 0},
        compiler_params=_params(),
    )(pc, cw, cat)


def _conv_bwd(name, dcat, pc, cw):
    _, r, w = pc.shape
    nblk = w // 128

    def body(dy_ref, pc_ref, cw_ref, dpc_ref, dcw_ref):
        u, gb, gc = pc_ref[0], pc_ref[1], pc_ref[2]
        cwv = cw_ref[...]
        dy = dy_ref[...]
        a = gc * u
        a1, a2 = _shift_down(a, 1), _shift_down(a, 2)
        conv = a2 * cwv[0:1] + a1 * cwv[1:2] + a * cwv[2:3]
        dconv = dy * gb
        da = dconv * cwv[2:3] + _shift_up(dconv, 1) * cwv[1:2] + _shift_up(dconv, 2) * cwv[0:1]
        dpc_ref[0] = (da * gc).astype(BF16)
        dpc_ref[1] = (dy * conv).astype(BF16)
        dpc_ref[2] = (da * u).astype(BF16)
        row = lax.broadcasted_iota(jnp.int32, (8, 128), 0)
        dw0 = jnp.sum(dconv * a2, axis=0, keepdims=True)
        dw1 = jnp.sum(dconv * a1, axis=0, keepdims=True)
        dw2 = jnp.sum(dconv * a, axis=0, keepdims=True)
        dcw_ref[...] = jnp.where(row == 0, dw0, jnp.where(row == 1, dw1, jnp.where(row == 2, dw2, 0.0)))

    return pl.pallas_call(
        body, name=name, grid=(nblk,),
        in_specs=[pl.BlockSpec((r, 128), lambda j: (0, nblk + j)),
                  pl.BlockSpec((3, r, 128), lambda j: (0, 0, j)), pl.BlockSpec((8, 128), lambda j: (0, j))],
        out_specs=[pl.BlockSpec((3, r, 128), lambda j: (0, 0, j)), pl.BlockSpec((8, 128), lambda j: (0, j))],
        out_shape=[jax.ShapeDtypeStruct((3, r, w), BF16), jax.ShapeDtypeStruct((8, w), F32)],
        compiler_params=_params(),
    )(dcat, pc, cw)


def _dot(a, b, dims):
    return lax.dot_general(a, b, (dims, ((), ())), preferred_element_type=F32)


def _col_to_row(xc, eye):
    return jnp.sum(jnp.where(eye, xc, 0.0), axis=0, keepdims=True)


def _row_to_col(xr, eye):
    return jnp.sum(jnp.where(eye, xr, 0.0), axis=1, keepdims=True)


def _gate_tiles(graw, bias, row0):
    th = jnp.tanh((graw + bias) / GATE_CAP)
    z = GATE_CAP * th
    row = lax.broadcasted_iota(jnp.int32, graw.shape, 0) + row0
    real = row >= PAD_FRONT
    li = jnp.where(real, z, -jnp.inf)
    lf = jnp.where(real, jnp.minimum(z, 0.0) - jnp.log(1.0 + jnp.exp(-jnp.abs(z))), 0.0)
    return th, z, li, lf, real


def _interleave(gens):
    results = [None] * len(gens)
    live = list(enumerate(gens))
    while live:
        still = []
        for i, gen in live:
            try:
                next(gen)
                still.append((i, gen))
            except StopIteration as stop:
                results[i] = stop.value
        live = still
    return results


def _chunk_common(pm, h, li, lf, cst, nst, mst, tril, eye):
    kraw = pm[:, QK_W + h * DQK:QK_W + (h + 1) * DQK]
    q = (pm[:, h * DQK:(h + 1) * DQK] * QSCALE).astype(BF16)
    yield
    k = kraw.astype(BF16)
    v = pm[:, 2 * QK_W + h * DV:2 * QK_W + (h + 1) * DV].astype(BF16)
    yield
    li_c = li[:, h:h + 1]
    lf_c = lf[:, HEADS + h:HEADS + h + 1]
    li_r = _col_to_row(li_c, eye)
    yield
    lf_r = _col_to_row(lf_c, eye)
    yield
    b_c = jnp.sum(jnp.where(tril, lf_r, 0.0), axis=1, keepdims=True)
    yield
    b_r = _col_to_row(b_c, eye)
    yield
    dmat = jnp.where(tril, b_c - b_r + li_r, -jnp.inf)
    inter = b_c + mst
    yield
    mt = jnp.maximum(inter, jnp.max(dmat, axis=1, keepdims=True))
    yield
    w_inter = jnp.exp(inter - mt)
    p = jnp.exp(dmat - mt)
    yield
    s = _dot(q, k, NT) * p
    yield
    cb = cst.astype(BF16)
    nb = nst.astype(BF16).astype(F32)
    qc = _dot(q, cb, NN)
    yield
    qn = jnp.sum(q.astype(F32) * nb, axis=1, keepdims=True)
    yield
    den = w_inter * qn + jnp.sum(s, axis=1, keepdims=True)
    yield
    dn = jnp.maximum(jnp.abs(den), jnp.exp(-mt))
    b_end = b_c[CHUNK - 1:CHUNK, :]
    decay = b_end - b_c + li_c
    yield
    m_new = jnp.maximum(b_end + mst, jnp.max(decay, axis=0, keepdims=True))
    yield
    w_old = jnp.exp(b_end + mst - m_new)
    w_in = jnp.exp(decay - m_new)
    kw = (w_in * kraw).astype(BF16)
    yield
    return dict(q=q, k=k, v=v, kraw=kraw, mt=mt, w_inter=w_inter, p=p, s=s, cb=cb, nb=nb, qc=qc, qn=qn,
                den=den, dn=dn, m_new=m_new, w_old=w_old, w_in=w_in, kw=kw)


def _chunks_per_step(nc):
    return 1


def _mlstm_fwd(name, pm, bias, nw):
    r = pm.shape[0]
    nc = r // CHUNK
    grp = _chunks_per_step(nc)

    def body(pm_ref, b_ref, nw_ref, hm_ref, ht_ref, cs_ref, ns_ref, ms_ref, c_scr, n_scr, m_scr):
        step = pl.program_id(0)

        @pl.when(step == 0)
        def _():
            c_scr[...] = jnp.zeros_like(c_scr)
            n_scr[...] = jnp.zeros_like(n_scr)
            m_scr[...] = jnp.zeros_like(m_scr)

        rr = lax.broadcasted_iota(jnp.int32, (CHUNK, CHUNK), 0)
        cc = lax.broadcasted_iota(jnp.int32, (CHUNK, CHUNK), 1)
        tril, eye = cc <= rr, cc == rr
        bv, nwv = b_ref[...], nw_ref[...]
        states = [(c_scr[h], n_scr[h], m_scr[h]) for h in range(HEADS)]
        for g in range(grp):
            rows = slice(g * CHUNK, (g + 1) * CHUNK)
            pmv = pm_ref[rows, :]
            _, _, li, lf, _ = _gate_tiles(pmv[:, GATE_COL:GATE_COL + 128], bv, (step * grp + g) * CHUNK)
            def head(h, cst, nst, mst, g=g, rows=rows, pmv=pmv, li=li, lf=lf):
                f = yield from _chunk_common(pmv, h, li, lf, cst, nst, mst, tril, eye)
                num = f["w_inter"] * f["qc"] + _dot(f["s"].astype(BF16), f["v"], NN)
                yield
                hh = num / f["dn"]
                yield
                c_new = f["w_old"] * cst + _dot(f["kw"], f["v"], TN)
                yield
                n_new = f["w_old"] * nst + jnp.sum(
                    f["w_in"].astype(BF16).astype(F32) * f["k"].astype(F32), axis=0, keepdims=True)
                yield
                sl = slice(h * DV, (h + 1) * DV)
                rs = lax.rsqrt(jnp.mean(hh * hh, axis=1, keepdims=True) + EPS)
                yield
                og = pmv[:, 2 * QK_W + MLSTM_W + h * DV:2 * QK_W + MLSTM_W + (h + 1) * DV]
                cs_ref[g, h] = cst
                ns_ref[g, h] = nst
                ms_ref[g, h] = mst
                ht_ref[rows, sl] = hh
                yield
                hm_ref[rows, sl] = (_sigmoid(og) * (hh * rs * nwv[:, sl])).astype(BF16)
                return c_new, n_new, f["m_new"]

            states = _interleave([head(h, *states[h]) for h in range(HEADS)])
        for h, (cst, nst, mst) in enumerate(states):
            c_scr[h] = cst
            n_scr[h] = nst
            m_scr[h] = mst

    return pl.pallas_call(
        body, name=name, grid=(nc // grp,),
        in_specs=[pl.BlockSpec((grp * CHUNK, PM_W), lambda i: (i, 0)), pl.BlockSpec((1, 128), lambda i: (0, 0)),
                  pl.BlockSpec((1, MLSTM_W), lambda i: (0, 0))],
        out_specs=[pl.BlockSpec((grp * CHUNK, MLSTM_W), lambda i: (i, 0)),
                   pl.BlockSpec((grp * CHUNK, MLSTM_W), lambda i: (i, 0)),
                   pl.BlockSpec((grp, HEADS, DQK, DV), lambda i: (i, 0, 0, 0)),
                   pl.BlockSpec((grp, HEADS, 1, DQK), lambda i: (i, 0, 0, 0)),
                   pl.BlockSpec((grp, HEADS, 1, 1), lambda i: (i, 0, 0, 0))],
        out_shape=[jax.ShapeDtypeStruct((r, MLSTM_W + CONV_W), BF16), jax.ShapeDtypeStruct((r, MLSTM_W), F32),
                   jax.ShapeDtypeStruct((nc, HEADS, DQK, DV), F32),
                   jax.ShapeDtypeStruct((nc, HEADS, 1, DQK), F32),
                   jax.ShapeDtypeStruct((nc, HEADS, 1, 1), F32)],
        scratch_shapes=[pltpu.VMEM((HEADS, DQK, DV), F32), pltpu.VMEM((HEADS, 1, DQK), F32),
                        pltpu.VMEM((HEADS, 1, 1), F32)],
        compiler_params=_params(),
    )(pm, bias, nw)


def _mlstm_bwd(name, dcat, pm, ht, cs, ns, ms, bias, nw):
    r = pm.shape[0]
    nc = r // CHUNK
    grp = _chunks_per_step(nc)
    nsteps = nc // grp

    def body(dy_ref, pm_ref, ht_ref, cs_ref, ns_ref, ms_ref, b_ref, nw_ref, dpm_ref, dnw_ref, db_ref,
             dc_scr, dn_scr):
        step = pl.program_id(0)

        @pl.when(step == 0)
        def _():
            dc_scr[...] = jnp.zeros_like(dc_scr)
            dn_scr[...] = jnp.zeros_like(dn_scr)
            dnw_ref[...] = jnp.zeros_like(dnw_ref)
            db_ref[...] = jnp.zeros_like(db_ref)

        rr = lax.broadcasted_iota(jnp.int32, (CHUNK, CHUNK), 0)
        cc = lax.broadcasted_iota(jnp.int32, (CHUNK, CHUNK), 1)
        tril, eye, triu = cc <= rr, cc == rr, cc >= rr
        lane = lax.broadcasted_iota(jnp.int32, (CHUNK, 128), 1)
        rowid = lax.broadcasted_iota(jnp.int32, (CHUNK, 1), 0)
        bv, nwv = b_ref[...], nw_ref[...]
        carried = [(dc_scr[h], dn_scr[h]) for h in range(HEADS)]
        dnw_acc = [jnp.zeros((1, DV), F32) for _ in range(HEADS)]
        db_acc = jnp.zeros((1, 128), F32)
        for g in reversed(range(grp)):
            rows = slice(g * CHUNK, (g + 1) * CHUNK)
            ci = (nsteps - 1 - step) * grp + g
            pmv = pm_ref[rows, :]
            th, z, li, lf, real = _gate_tiles(pmv[:, GATE_COL:GATE_COL + 128], bv, ci * CHUNK)
            heads = _interleave([
                _mlstm_bwd_head(h, pmv, ht_ref[rows, h * DV:(h + 1) * DV], dy_ref[rows, h * DV:(h + 1) * DV], nwv,
                                li, lf, cs_ref[g, h], ns_ref[g, h], ms_ref[g, h], carried[h][0], carried[h][1],
                                tril, eye, triu, lane, rowid, dpm_ref, rows)
                for h in range(HEADS)])
            carried = [(dc_new, dn_new) for _, dc_new, dn_new, _ in heads]
            dgt = heads[0][0] + heads[1][0] + heads[2][0] + heads[3][0]
            dnw_acc = [dnw_acc[h] + heads[h][3] for h in range(HEADS)]
            dact = jnp.where(lane < HEADS, 1.0, 1.0 - _sigmoid(z)) * (1.0 - th * th)
            dgraw = jnp.where(real & (lane < 2 * HEADS), dgt * dact, 0.0)
            dpm_ref[rows, GATE_COL:GATE_COL + 128] = dgraw.astype(BF16)
            db_acc = db_acc + jnp.sum(dgraw, axis=0, keepdims=True)
        for h, (dcn, dnn) in enumerate(carried):
            dc_scr[h] = dcn
            dn_scr[h] = dnn
            dnw_ref[:, h * DV:(h + 1) * DV] += dnw_acc[h]
        db_ref[...] += db_acc

    rev = lambda i: (nsteps - 1 - i, 0)
    rev4 = lambda i: (nsteps - 1 - i, 0, 0, 0)
    return pl.pallas_call(
        body, name=name, grid=(nsteps,),
        in_specs=[pl.BlockSpec((grp * CHUNK, MLSTM_W), rev), pl.BlockSpec((grp * CHUNK, PM_W), rev),
                  pl.BlockSpec((grp * CHUNK, MLSTM_W), rev),
                  pl.BlockSpec((grp, HEADS, DQK, DV), rev4), pl.BlockSpec((grp, HEADS, 1, DQK), rev4),
                  pl.BlockSpec((grp, HEADS, 1, 1), rev4),
                  pl.BlockSpec((1, 128), lambda i: (0, 0)), pl.BlockSpec((1, MLSTM_W), lambda i: (0, 0))],
        out_specs=[pl.BlockSpec((grp * CHUNK, PM_W), rev), pl.BlockSpec((1, MLSTM_W), lambda i: (0, 0)),
                   pl.BlockSpec((1, 128), lambda i: (0, 0))],
        out_shape=[jax.ShapeDtypeStruct((r, PM_W), BF16), jax.ShapeDtypeStruct((1, MLSTM_W), F32),
                   jax.ShapeDtypeStruct((1, 128), F32)],
        scratch_shapes=[pltpu.VMEM((HEADS, DQK, DV), F32), pltpu.VMEM((HEADS, 1, DQK), F32)],
        compiler_params=_params(),
    )(dcat, pm, ht, cs, ns, ms, bias, nw)


def _mlstm_bwd_head(h, pmv, hh, y, nwv, li, lf, cst, nst, mst, dcn, dnn, tril, eye, triu, lane, rowid,
                    dpm_ref, rows):
    f = yield from _chunk_common(pmv, h, li, lf, cst, nst, mst, tril, eye)
    q, k, v, s, p = f["q"], f["k"], f["v"], f["s"], f["p"]
    w_inter, w_in, w_old, dn = f["w_inter"], f["w_in"], f["w_old"], f["dn"]
    osl = slice(2 * QK_W + MLSTM_W + h * DV, 2 * QK_W + MLSTM_W + (h + 1) * DV)
    sg = _sigmoid(pmv[:, osl])
    yield
    rs = lax.rsqrt(jnp.mean(hh * hh, axis=1, keepdims=True) + EPS)
    yield
    nwh = nwv[:, h * DV:(h + 1) * DV]
    dpm_ref[rows, osl] = (y * (hh * rs * nwh) * sg * (1.0 - sg)).astype(BF16)
    yield
    dhn = y * sg
    dnw_h = jnp.sum(dhn * hh * rs, axis=0, keepdims=True)
    yield
    wd = dhn * nwh
    dhh = rs * wd - hh * (rs * rs * rs) * jnp.mean(hh * wd, axis=1, keepdims=True)
    yield
    dnum = dhh / dn
    dd = -jnp.sum(dhh * hh, axis=1, keepdims=True) / dn
    yield
    dden = jnp.where(jnp.abs(f["den"]) > jnp.exp(-f["mt"]), dd * jnp.sign(f["den"]), 0.0)
    dnum_b = dnum.astype(BF16)
    wdn = (w_inter * dnum).astype(BF16)
    wid = (w_inter * dden).astype(BF16).astype(F32)
    yield
    ds = _dot(dnum_b, v, NT) + dden
    yield
    dsp = (ds * p).astype(BF16)
    yield
    dq = _dot(dsp, k, NN) + _dot(wdn, f["cb"], NT) + wid * f["nb"]
    yield
    dk = _dot(dsp, q, TN)
    yield
    dv = _dot(s.astype(BF16), dnum_b, TN)
    yield
    g = ds * s
    g_col = _row_to_col(jnp.sum(g, axis=0, keepdims=True), eye)
    yield
    db = jnp.sum(g, axis=1, keepdims=True) - g_col
    dli = g_col
    yield
    db = db + (jnp.sum(dnum * f["qc"], axis=1, keepdims=True) + dden * f["qn"]) * w_inter
    yield
    dcnb = dcn.astype(BF16)
    dnnb = dnn.astype(BF16).astype(F32)
    dkw = _dot(v, dcnb, NT) + dnnb
    yield
    dk = dk + w_in * dkw
    dv = dv + _dot(f["kw"], dcnb, NN)
    yield
    ddecay = jnp.sum(dkw * f["kraw"], axis=1, keepdims=True) * w_in
    yield
    dw_old = (jnp.sum(jnp.sum(dcn * cst, axis=1, keepdims=True), axis=0, keepdims=True)
              + jnp.sum(dnn * nst, axis=1, keepdims=True))
    yield
    db_end = dw_old * w_old + jnp.sum(ddecay, axis=0, keepdims=True)
    db = db - ddecay + jnp.where(rowid == CHUNK - 1, db_end, 0.0)
    dli = dli + ddecay
    yield
    dc_new = w_old * dcn + _dot(q, wdn, TN)
    yield
    dn_new = w_old * dnn + jnp.sum(wid * q.astype(F32), axis=0, keepdims=True)
    yield
    dlf = jnp.sum(jnp.where(triu, _col_to_row(db, eye), 0.0), axis=1, keepdims=True)
    yield
    gate_part = jnp.where(lane == h, dli, 0.0) + jnp.where(lane == HEADS + h, dlf, 0.0)
    dpm_ref[rows, h * DQK:(h + 1) * DQK] = (dq * QSCALE).astype(BF16)
    yield
    dpm_ref[rows, QK_W + h * DQK:QK_W + (h + 1) * DQK] = dk.astype(BF16)
    yield
    dpm_ref[rows, 2 * QK_W + h * DV:2 * QK_W + (h + 1) * DV] = dv.astype(BF16)
    return gate_part, dc_new, dn_new, dnw_h


def _my_place():
    return lax.axis_index("x"), lax.axis_index("y"), lax.axis_index("c")


def _flip(v, bit):
    return 1 - v if bit else v


def _exchange_small(name, blk, reduce):
    r, c = blk.shape

    def body(x_ref, o_ref, *rest):
        slots = rest[0] if reduce else o_ref
        send_sems, recv_sems = rest[-2], rest[-1]
        x, y, cc = _my_place()
        me = 4 * x + 2 * y + cc
        slots[me] = x_ref[...]
        copies = []
        for k in range(1, N_DEV):
            peer = (_flip(x, k & 4), _flip(y, k & 2), _flip(cc, k & 1))
            cp = pltpu.make_async_remote_copy(
                src_ref=x_ref, dst_ref=slots.at[me], send_sem=send_sems.at[k - 1],
                recv_sem=recv_sems.at[k - 1], device_id=peer, device_id_type=MESH)
            cp.start()
            copies.append(cp)
        for cp in copies:
            cp.wait()
        if reduce:
            acc = slots[0]
            for d in range(1, N_DEV):
                acc = acc + slots[d]
            o_ref[...] = acc

    scratch = ([pltpu.VMEM((N_DEV, r, c), F32)] if reduce else []) + [
        pltpu.SemaphoreType.DMA((N_DEV - 1,)), pltpu.SemaphoreType.DMA((N_DEV - 1,))]
    return pl.pallas_call(
        body, name=name,
        out_shape=jax.ShapeDtypeStruct((r, c) if reduce else (N_DEV, r, c), F32),
        in_specs=[pl.BlockSpec(memory_space=pltpu.VMEM)], out_specs=pl.BlockSpec(memory_space=pltpu.VMEM),
        scratch_shapes=scratch, compiler_params=_params(),
    )(blk)


HBM_SPEC = pl.BlockSpec(memory_space=pltpu.HBM)
SEM_SPEC = pl.BlockSpec(memory_space=pltpu.SEMAPHORE)
ANY_SPEC = pl.BlockSpec(memory_space=pl.ANY)
DATAFLOW = pltpu.SideEffectType.DATAFLOW_SIDE_EFFECTING


def _split_copy(name, arrays, start=None, wait=None, after=None):
    results, token = _split_copies(name, [(arrays, start, wait)], after)
    return results[0][0], results[0][1], token


def _split_copies(name, jobs, after=None):
    operands, in_specs, out_shape, out_specs, aliases = [], [], [], [], {}
    in_at, out_at = [], []
    for arrays, start, wait in jobs:
        in_at.append(len(operands))
        operands += [pltpu.with_memory_space_constraint(a, pltpu.HBM) for a in arrays]
        in_specs += [HBM_SPEC] * len(arrays)
        if wait:
            operands += list(wait[1])
            in_specs += [SEM_SPEC, SEM_SPEC]
    if after is not None:
        operands.append(after)
        in_specs.append(ANY_SPEC)
    for j, (arrays, start, wait) in enumerate(jobs):
        out_at.append(len(out_shape))
        if start:
            out_shape += [pltpu.SemaphoreType.DMA((start[1],)), pltpu.SemaphoreType.DMA((start[1],))]
            out_specs += [SEM_SPEC, SEM_SPEC]
        for i, a in enumerate(arrays):
            aliases[in_at[j] + i] = len(out_shape)
            out_shape.append(pltpu.HBM(a.shape, a.dtype))
            out_specs.append(HBM_SPEC)
    any_start = any(start for _, start, _ in jobs)
    if any_start:
        out_shape.append(jax.ShapeDtypeStruct((8, 128), F32))
        out_specs.append(pl.BlockSpec(memory_space=pltpu.VMEM))
    n_in = len(operands)

    def body(*refs):
        for j, (arrays, start, wait) in enumerate(jobs):
            if wait:
                ins = refs[in_at[j]:in_at[j] + len(arrays)]
                at = in_at[j] + len(arrays)
                for cp in wait[0](ins, refs[at], refs[at + 1]):
                    cp.wait_send()
                    cp.wait_recv()
        for j, (arrays, start, wait) in enumerate(jobs):
            if start:
                ins = refs[in_at[j]:in_at[j] + len(arrays)]
                at = n_in + out_at[j]
                for cp in start[0](ins, refs[at], refs[at + 1]):
                    cp.start()
        if any_start:
            token = refs[n_in + len(out_shape) - 1]
            token[...] = jnp.zeros_like(token)

    outs = pl.pallas_call(
        body, name=name, in_specs=in_specs, out_specs=out_specs, out_shape=out_shape,
        input_output_aliases=aliases, compiler_params=pltpu.CompilerParams(has_side_effects=DATAFLOW),
    )(*operands)
    results = []
    for j, (arrays, start, wait) in enumerate(jobs):
        at = out_at[j]
        sems = (outs[at], outs[at + 1]) if start else None
        at += 2 if start else 0
        results.append((list(outs[at:at + len(arrays)]), sems))
    return results, (outs[-1] if any_start else None)


def _remote(src, dst, send_sems, recv_sems, k, to):
    return pltpu.make_async_remote_copy(src_ref=src, dst_ref=dst, send_sem=send_sems.at[k],
                                        recv_sem=recv_sems.at[k], device_id=to, device_id_type=MESH)


def _slot(px, py, pc):
    return 4 * px + 2 * py + pc


def _gather_first(refs, send_sems, recv_sems):
    x, y, c = _my_place()
    blk = refs[0].at[_slot(x, y, c)]
    targets = [(x, y, 1 - c), (1 - x, y, c), (x, 1 - y, c)]
    return [_remote(blk, blk, send_sems, recv_sems, k, to) for k, to in enumerate(targets)]


def _gather_relay(refs, send_sems, recv_sems):
    x, y, c = _my_place()
    rows = refs[0].shape[1]
    half = rows // 32 * 16
    from_x, from_y = _slot(1 - x, y, c), _slot(x, 1 - y, c)
    upper = refs[0].at[from_x, pl.ds(0, half)]
    lower = refs[0].at[from_y, pl.ds(half, rows - half)]
    return [_remote(upper, upper, send_sems, recv_sems, 0, (x, 1 - y, c)),
            _remote(lower, lower, send_sems, recv_sems, 1, (1 - x, y, c)),
            _remote(refs[0].at[from_x], refs[0].at[from_x], send_sems, recv_sems, 2, (x, y, 1 - c)),
            _remote(refs[0].at[from_y], refs[0].at[from_y], send_sems, recv_sems, 3, (x, y, 1 - c))]


def _gather_last(refs, send_sems, recv_sems):
    x, y, c = _my_place()
    blk = refs[0].at[_slot(1 - x, 1 - y, c)]
    return [_remote(blk, blk, send_sems, recv_sems, 0, (x, y, 1 - c))]


def _scatter_sibling(n):
    def copies(refs, send_sems, recv_sems):
        x, y, c = _my_place()
        return [_remote(refs[a].at[2 * j + 1 - c], refs[n + a].at[j], send_sems, recv_sems, 4 * a + j, (x, y, 1 - c))
                for a in range(n) for j in range(4)]
    return copies


def _scatter_chips(n):
    def copies(refs, send_sems, recv_sems):
        x, y, c = _my_place()
        out = []
        for a in range(n):
            for k in range(1, 4):
                px, py = _flip(x, k & 2), _flip(y, k & 1)
                out.append(_remote(refs[a].at[2 * px + py], refs[n + a].at[2 * x + y], send_sems, recv_sems,
                                   3 * a + k - 1, (px, py, c)))
        return out
    return copies


def _pair_sum(name, core, g, t):
    _, r, c = g.shape
    tr = _tile(r, 512, 8)
    g4 = g.reshape(4, 2, r, c)

    def body(core_ref, g_ref, t_ref, o_ref):
        o_ref[...] = (g_ref[...].astype(F32) + t_ref[...].astype(F32)).astype(BF16)

    return pl.pallas_call(
        body, name=name,
        grid_spec=pltpu.PrefetchScalarGridSpec(
            num_scalar_prefetch=1, grid=(4, r // tr),
            in_specs=[pl.BlockSpec((None, None, tr, c), lambda j, i, core_ref: (j, core_ref[0], i, 0)),
                      pl.BlockSpec((None, tr, c), lambda j, i, core_ref: (j, i, 0))],
            out_specs=pl.BlockSpec((None, tr, c), lambda j, i, core_ref: (j, i, 0))),
        out_shape=jax.ShapeDtypeStruct((4, r, c), BF16), compiler_params=_params(),
    )(core, g4, t)


def _adam_math(w, g, m, v):
    m2 = ADAM_B1 * m + (1.0 - ADAM_B1) * g
    v2 = ADAM_B2 * v + (1.0 - ADAM_B2) * (g * g)
    m_hat = m2 / (1.0 - ADAM_B1 ** ADAM_STEP)
    v_hat = v2 / (1.0 - ADAM_B2 ** ADAM_STEP)
    delta = -ADAM_LR * (m_hat / (jnp.sqrt(v_hat) + ADAM_EPS) + ADAM_WD * w)
    return delta, m2, v2


def _adam_sharded(name, chip, w, m, v, grads):
    _, r, c = w.shape
    tr = _tile(r, 256, 8)
    tc = c if tr < r else _tile(c, 256, 128)

    def body(chip_ref, w_ref, m_ref, v_ref, p0_ref, q0_ref, p1_ref, q1_ref, g_ref, d_ref, nm_ref, nv_ref):
        mine = chip_ref[0]

        def total(p_ref, q_ref):
            acc = None
            for j in range(4):
                part = jnp.where(mine == j, p_ref[...], q_ref[j]).astype(F32)
                acc = part if acc is None else acc + part
            return acc

        g = jnp.where(pl.program_id(0) == 0, total(p0_ref, q0_ref), total(p1_ref, q1_ref))
        delta, m2, v2 = _adam_math(w_ref[...], g, m_ref[...], v_ref[...])
        g_ref[...] = g
        d_ref[...] = delta
        nm_ref[...] = m2
        nv_ref[...] = v2

    def grad_specs(layer):
        at = lambda l, i, j: (jnp.where(l == layer, i, 0), jnp.where(l == layer, j, 0))
        return [pl.BlockSpec((None, tr, tc), lambda l, i, j, chip_ref: (chip_ref[0],) + at(l, i, j)),
                pl.BlockSpec((4, tr, tc), lambda l, i, j, chip_ref: (0,) + at(l, i, j))]

    wspec = pl.BlockSpec((None, tr, tc), lambda l, i, j, chip_ref: (l, i, j))
    sds = jax.ShapeDtypeStruct(w.shape, F32)
    return pl.pallas_call(
        body, name=name,
        grid_spec=pltpu.PrefetchScalarGridSpec(
            num_scalar_prefetch=1, grid=(2, r // tr, c // tc),
            in_specs=[wspec, wspec, wspec] + grad_specs(0) + grad_specs(1), out_specs=[wspec] * 4),
        out_shape=[sds] * 4, compiler_params=_params(),
    )(chip, w, m, v, grads[0][0], grads[0][1], grads[1][0], grads[1][1])


def _adam_small(name, w, m, v, g):
    def body(w_ref, m_ref, v_ref, g_ref, d_ref, nm_ref, nv_ref):
        delta, m2, v2 = _adam_math(w_ref[...], g_ref[...], m_ref[...], v_ref[...])
        d_ref[...] = delta
        nm_ref[...] = m2
        nv_ref[...] = v2

    sds = jax.ShapeDtypeStruct(w.shape, F32)
    vm = pl.BlockSpec(memory_space=pltpu.VMEM)
    return pl.pallas_call(body, name=name, in_specs=[vm] * 4, out_specs=[vm] * 3, out_shape=[sds] * 3,
                          compiler_params=_params())(w, m, v, g)


GATE_END = GATE_COL + 2 * HEADS


def _split_w_in(gathered):
    win_t = gathered.reshape(D_IN, D_MODEL)
    return win_t, win_t[GATE_END:].reshape(3, CONV_W, D_MODEL)


def _merge_dw_in(dwm_t, dwc_t):
    full = jnp.concatenate([dwm_t[:GATE_END], dwc_t.reshape(3 * CONV_W, D_MODEL)], axis=0)
    return full.reshape(N_DEV, IN_SH, D_MODEL)


def _pack128(parts):
    flat = jnp.concatenate([p.reshape(-1) for p in parts])
    n = flat.shape[0]
    rows = -(-n // 1024) * 8
    return jnp.pad(flat, (0, rows * 128 - n)).reshape(rows, 128)


def _unpack128(packed, shapes):
    flat = packed.reshape(-1)
    out, at = [], 0
    for s in shapes:
        n = int(np.prod(s))
        out.append(flat[at:at + n].reshape(s))
        at += n
    return out


def kernel(x, meta_tokens, norm_mix_w, w_in, b_gates, conv_w, mlstm_norm_w, w_out, norm_ffn_w, w_gate, w_up, w_down, norm_final_w, loss_target, m_meta_tokens, m_norm_mix_w, m_w_in, m_b_gates, m_conv_w, m_mlstm_norm_w, m_w_out, m_norm_ffn_w, m_w_gate, m_w_up, m_w_down, m_norm_final_w, v_meta_tokens, v_norm_mix_w, v_w_in, v_b_gates, v_conv_w, v_mlstm_norm_w, v_w_out, v_norm_ffn_w, v_w_gate, v_w_up, v_w_down, v_norm_final_w):
    seq = x.shape[1]
    rows = TOK0 + seq
    me = 4 * lax.axis_index("x") + 2 * lax.axis_index("y") + lax.axis_index("c")
    meta_sh = meta_tokens.shape[1]
    conv_sh = conv_w.shape[2]

    w_gate_t, m_w_gate_t, v_w_gate_t = (jnp.transpose(a, (0, 2, 1)) for a in (w_gate, m_w_gate, v_w_gate))
    w_up_t, m_w_up_t, v_w_up_t = (jnp.transpose(a, (0, 2, 1)) for a in (w_up, m_w_up, v_w_up))
    shards = []
    for l in range(DEPTH):
        shards += [jnp.transpose(w_in[l]).astype(BF16), w_out[l].astype(BF16), w_gate_t[l].astype(BF16),
                   w_up_t[l].astype(BF16), w_down[l].astype(BF16)]
    per_layer = ("w_in", "w_out", "w_gate", "w_up", "w_down")
    gather_state = {}

    def gather_step(tag, after, start=None, relay=None, last=None, done=()):
        jobs, idx = [], []
        if relay is not None and relay < len(shards):
            jobs.append((gather_state[relay][0], (_gather_relay, 4), (_gather_first, gather_state[relay][1])))
            idx.append(relay)
        if start is not None and start < len(shards):
            buf = lax.dynamic_update_index_in_dim(lax.empty((N_DEV,) + shards[start].shape, BF16), shards[start], me, 0)
            jobs.append(([buf], (_gather_first, 3), None))
            idx.append(start)
        if last is not None:
            jobs.append((gather_state[last][0], (_gather_last, 1), (_gather_relay, gather_state[last][1])))
            idx.append(last)
        for i in done:
            jobs.append((gather_state[i][0], None, (_gather_last, gather_state[i][1])))
            idx.append(i)
        if not jobs:
            return after, []
        results, tok = _split_copies(f"gather_{tag}", jobs, after)
        for i, res in zip(idx, results):
            gather_state[i] = res
        return (after if tok is None else tok), [gather_state[i][0][0] for i in done]

    bias = [jnp.pad(b_gates[l].reshape(1, 2 * HEADS), ((0, 0), (0, 128 - 2 * HEADS))) for l in range(DEPTH)]
    nmix = [norm_mix_w[l].reshape(1, D_MODEL) for l in range(DEPTH)]
    nffn = [norm_ffn_w[l].reshape(1, D_MODEL) for l in range(DEPTH)]
    nmls = [mlstm_norm_w[l].reshape(1, MLSTM_W) for l in range(DEPTH)]
    weights = [dict() for _ in range(DEPTH)]
    saved = [dict() for _ in range(DEPTH)]

    def layer_fwd(l, h, after):
        w, s = weights[l], saved[l]
        k0 = len(per_layer) * l
        tok, _ = gather_step(f"l{l}_a", after, last=k0)
        _, (g_in,) = gather_step(f"l{l}_b", tok, done=[k0])
        tok, _ = gather_step(f"l{l}_c", g_in, relay=k0 + 1, start=k0 + 3)
        w["win_t"], w["wc_t"] = _split_w_in(g_in)
        s["h0"] = h
        s["hn"], s["pm"] = _norm_proj(f"proj_mlstm_{l}", h, nmix[l] + tok[0, 0], w["win_t"], PM_W)
        tok, _ = gather_step(f"l{l}_d", s["pm"], relay=k0 + 2, start=k0 + 4)
        tok, _ = gather_step(f"l{l}_d2", tok, last=k0 + 1)
        s["pc"] = _mm_nt_bcols(f"proj_conv_{l}", s["hn"], w["wc_t"], F32, dep=tok)
        hm, s["ht"], s["cs"], s["ns"], s["ms"] = _mlstm_fwd(f"mlstm_fwd_{l}", s["pm"], bias[l] + tok[:1], nmls[l])
        tok, _ = gather_step(f"l{l}_e", hm, relay=k0 + 3, start=k0 + 5)
        tok, _ = gather_step(f"l{l}_e2", tok, last=k0 + 2)
        s["cat"] = _conv_fwd(f"conv_fwd_{l}", s["pc"], conv_rows[l] + tok[0, 0], hm)
        _, (g_out,) = gather_step(f"l{l}_f", s["cat"], done=[k0 + 1])
        w["wo"] = g_out.reshape(D_MODEL, D_MODEL)
        s["h1"], s["hf"] = _proj_res_norm(f"out_proj_{l}", s["cat"], w["wo"], s["h0"], nffn[l])
        tok_g, _ = gather_step(f"l{l}_g", s["h1"], relay=k0 + 4, start=k0 + 6)
        tok, _ = gather_step(f"l{l}_h", tok_g, last=k0 + 3)
        _, (g_gate, g_up) = gather_step(f"l{l}_i", tok, done=[k0 + 2, k0 + 3])
        w["wg_t"] = g_gate.reshape(D_FF, D_MODEL)
        w["wu_t"] = g_up.reshape(D_FF, D_MODEL)
        s["g"], s["u"], s["act"] = _ffn_in(f"ffn_in_{l}", s["hf"], w["wg_t"], w["wu_t"], dep=tok_g)
        tok, _ = gather_step(f"l{l}_j", s["act"], last=k0 + 4)
        _, (g_down,) = gather_step(f"l{l}_k", tok, done=[k0 + 4])
        w["wd"] = g_down.reshape(D_FF, D_MODEL)
        tok, _ = gather_step(f"l{l}_k2", tok, relay=k0 + 5, start=k0 + 7)
        return _mm_nn(f"ffn_out_{l}", s["act"], w["wd"], F32, res=s["h1"], dep=tok)

    tok, _ = gather_step("first", None, start=0)
    zero = tok[0, 0]
    small = jnp.concatenate(
        [meta_tokens + zero, jnp.pad(conv_w.reshape(DEPTH * 3, conv_sh), ((0, 2), (0, meta_sh - conv_sh)))], axis=0)
    slots = _exchange_small("gather_small", small, reduce=False)
    meta_full = jnp.transpose(slots[:, :N_META, :], (1, 0, 2)).reshape(N_META, D_MODEL)
    conv_full = jnp.transpose(slots[:, N_META:N_META + DEPTH * 3, :conv_sh], (1, 0, 2)).reshape(DEPTH, 3, CONV_W)
    conv_rows = [jnp.pad(conv_full[l], ((0, 5), (0, 0))) for l in range(DEPTH)]
    w_in_t, m_w_in_t, v_w_in_t = (jnp.transpose(a + zero, (0, 2, 1)) for a in (w_in, m_w_in, v_w_in))
    tok, w_in_t, m_w_in_t, v_w_in_t, meta_full = lax.optimization_barrier(
        (tok, w_in_t, m_w_in_t, v_w_in_t, meta_full))
    tok, _ = gather_step("pre_a", tok, relay=0)
    tok, _ = gather_step("pre_b", tok, start=1)
    tok, _ = gather_step("pre_c", tok, start=2)
    h = jnp.concatenate([jnp.zeros((PAD_FRONT, D_MODEL), F32), meta_full, x[0]], axis=0)
    h = layer_fwd(0, h, tok)
    h = layer_fwd(1, h, h)

    dh, dh_b, d_final, loss_part = _final_loss("final_loss", h, norm_final_w.reshape(1, D_MODEL), loss_target[0])

    core = lax.axis_index("c").astype(jnp.int32).reshape(1)
    chip = (2 * lax.axis_index("x") + lax.axis_index("y")).astype(jnp.int32).reshape(1)
    scatter_state = {}

    def scatter_begin(nm, grad):
        land = lax.empty((4,) + grad.shape[1:], BF16)
        arrs, sems, tok = _split_copy(f"grad_sibling_start_{nm}", [grad, land], start=(_scatter_sibling(1), 4))
        scatter_state[nm] = (arrs, sems)
        return tok

    def scatter_advance(nm, after):
        arrs, sems = scatter_state[nm]
        arrs, _, _ = _split_copy(f"grad_sibling_done_{nm}", arrs, wait=(_scatter_sibling(1), sems), after=after)
        part = _pair_sum(f"grad_pair_sum_{nm}", core, arrs[0], arrs[1])
        arrs, sems, tok = _split_copy(f"grad_chips_start_{nm}", [part, lax.empty(part.shape, BF16)],
                                      start=(_scatter_chips(1), 3))
        scatter_state[nm] = (arrs, sems)
        return tok

    def scattered(nm, after):
        arrs, sems = scatter_state[nm]
        arrs, _, _ = _split_copy(f"grad_chips_done_{nm}", arrs, wait=(_scatter_chips(1), sems), after=after)
        return arrs[0], arrs[1]

    d_mix, d_ffn, d_mls, d_bias, d_conv = ([None] * DEPTH for _ in range(5))

    def layer_bwd(l, dh, dh_b, tok):
        w, s = weights[l], saved[l]
        dg, du = _ffn_act_bwd(f"d_act_{l}", dh_b, w["wd"], s["g"], s["u"], dep=tok)
        dw_down = _mm_tn(f"dw_down_{l}", s["act"], dh_b, BF16, tm=1408, tn=1024)
        tok = scatter_begin(f"w_down_{l}", dw_down.reshape(N_DEV, FF_SH, D_MODEL))
        dhf = _mm_nn(f"d_ffn_gate_{l}", dg, w["wg_t"], F32, dep=tok)
        tok = scatter_advance(f"w_down_{l}", after=dhf)
        dhf = _mm_nn(f"d_ffn_up_{l}", du, w["wu_t"], F32, res=dhf, dep=tok)
        dw_gate = _mm_tn(f"dw_gate_{l}", dg, s["hf"], BF16, tm=1408, tn=1024)
        tok = scatter_begin(f"w_gate_{l}", dw_gate.reshape(N_DEV, FF_SH, D_MODEL))
        dw_up = _mm_tn(f"dw_up_{l}", du, s["hf"], BF16, tm=1408, tn=1024, dep=tok)
        tok = scatter_begin(f"w_up_{l}", dw_up.reshape(N_DEV, FF_SH, D_MODEL))
        dh1, dh1_b, d_ffn[l] = _rms_bwd(f"norm_ffn_bwd_{l}", s["h1"], nffn[l] + tok[0, 0], dhf, dh)
        tok = scatter_advance(f"w_gate_{l}", after=dh1)
        dcat = _mm_nt(f"d_cat_{l}", dh1_b, w["wo"], F32, tk=D_MODEL, dep=tok)
        tok = scatter_advance(f"w_up_{l}", after=dcat)
        dw_out = _mm_tn(f"dw_out_{l}", s["cat"], dh1_b, BF16, tn=1024, dep=tok)
        tok = scatter_begin(f"w_out_{l}", dw_out.reshape(N_DEV, OUT_SH, D_MODEL))
        dpm, d_mls[l], d_bias[l] = _mlstm_bwd(f"mlstm_bwd_{l}", dcat, s["pm"], s["ht"], s["cs"], s["ns"],
                                               s["ms"], bias[l] + tok[:1], nmls[l])
        dpc, d_conv[l] = _conv_bwd(f"conv_bwd_{l}", dcat, s["pc"], conv_rows[l])
        tok = scatter_advance(f"w_out_{l}", after=dpc)
        dwm_t = _mm_tn(f"dw_mlstm_{l}", dpm, s["hn"], BF16, tm=640, tn=1024, dep=tok)
        dwc_t = _mm_tn_acols(f"dw_conv_{l}", dpc, s["hn"], BF16)
        tok = scatter_begin(f"w_in_{l}", _merge_dw_in(dwm_t, dwc_t))
        dhn = _mm_nn_two(f"d_norm_{l}", dpm, w["win_t"], dpc, w["wc_t"], dep=tok)
        tok = scatter_advance(f"w_in_{l}", after=dhn)
        dh, dh_b, d_mix[l] = _rms_bwd(f"norm_mix_bwd_{l}", s["h0"], nmix[l] + tok[0, 0], dhn, dh1)
        return dh, dh_b, tok

    dh, dh_b, tok = layer_bwd(1, dh, dh_b, None)
    dh, dh_b, tok_tail = layer_bwd(0, dh, dh_b, tok)

    pq = {}
    after = dh
    for l in reversed(range(DEPTH)):
        for nm in ("w_down", "w_gate", "w_up", "w_out", "w_in"):
            if (nm, l) != ("w_in", 0):
                pq[nm, l] = scattered(f"{nm}_{l}", after)
                after = pq[nm, l][0]
    untransposed = lambda outs: [jnp.transpose(o, (0, 2, 1)) for o in outs]
    g_out, d_out, nm_out, nv_out = _adam_sharded(
        "adam_w_out", chip, w_out, m_w_out, v_w_out, [pq["w_out", 0], pq["w_out", 1]])
    g_gate, d_gate, nm_gate, nv_gate = untransposed(_adam_sharded(
        "adam_w_gate", chip, w_gate_t, m_w_gate_t, v_w_gate_t, [pq["w_gate", 0], pq["w_gate", 1]]))
    g_up, d_up, nm_up, nv_up = untransposed(_adam_sharded(
        "adam_w_up", chip, w_up_t, m_w_up_t, v_w_up_t, [pq["w_up", 0], pq["w_up", 1]]))
    g_down, d_down, nm_down, nv_down = _adam_sharded(
        "adam_w_down", chip, w_down, m_w_down, v_w_down, [pq["w_down", 0], pq["w_down", 1]])
    pq["w_in", 0] = scattered("w_in_0", nv_down)
    g_in, d_in, nm_in, nv_in = untransposed(_adam_sharded(
        "adam_w_in", chip, w_in_t, m_w_in_t, v_w_in_t, [pq["w_in", 0], pq["w_in", 1]]))

    bg = jnp.concatenate([d_bias[l][0, :2 * HEADS] for l in range(DEPTH)])
    red_in = jnp.concatenate([
        dh[PAD_FRONT:TOK0], d_mix[0], d_mix[1], d_ffn[0], d_ffn[1], d_final,
        jnp.concatenate([d_mls[0], d_mls[1]], axis=1),
        jnp.stack([d_conv[l][:3] for l in range(DEPTH)]).reshape(3, 2 * CONV_W),
        jnp.pad(bg, (0, D_MODEL - bg.shape[0])).reshape(1, D_MODEL),
        jnp.pad(loss_part[:, :1], ((0, 0), (0, D_MODEL - 1))),
        jnp.zeros((5, D_MODEL), F32) + tok_tail[0, 0]], axis=0)
    red = _exchange_small("reduce_small", red_in, reduce=True)
    loss = red[26, 0]
    g_meta = lax.dynamic_slice_in_dim(red[:N_META], me * meta_sh, meta_sh, axis=1)
    g_mix, g_ffn, g_final = red[16:18], red[18:20], red[20]
    g_mls = red[21].reshape(DEPTH, MLSTM_W)
    g_conv = lax.dynamic_slice_in_dim(red[22:25].reshape(DEPTH, 3, CONV_W), me * conv_sh, conv_sh, axis=2)
    g_bias = red[25, :DEPTH * 2 * HEADS].reshape(DEPTH, 2 * HEADS)

    small_w = [meta_tokens, norm_mix_w, b_gates, conv_w, mlstm_norm_w, norm_ffn_w, norm_final_w]
    small_m = [m_meta_tokens, m_norm_mix_w, m_b_gates, m_conv_w, m_mlstm_norm_w, m_norm_ffn_w, m_norm_final_w]
    small_v = [v_meta_tokens, v_norm_mix_w, v_b_gates, v_conv_w, v_mlstm_norm_w, v_norm_ffn_w, v_norm_final_w]
    small_g = [g_meta, g_mix, g_bias, g_conv, g_mls, g_ffn, g_final]
    shapes = [a.shape for a in small_w]
    packed = _adam_small("adam_small", _pack128(small_w), _pack128(small_m), _pack128(small_v), _pack128(small_g))
    (d_meta, d_nmix, d_bg, d_cw, d_nmls, d_nffn, d_nfin), (nm_meta, nm_nmix, nm_bg, nm_cw, nm_nmls, nm_nffn, nm_nfin), \
        (nv_meta, nv_nmix, nv_bg, nv_cw, nv_nmls, nv_nffn, nv_nfin) = (_unpack128(p, shapes) for p in packed)

    grad_x = dh[TOK0:].reshape(1, seq, D_MODEL)
    return (loss, grad_x,
            g_meta, g_mix, g_in, g_bias, g_conv, g_mls, g_out, g_ffn, g_gate, g_up, g_down, g_final,
            d_meta, d_nmix, d_in, d_bg, d_cw, d_nmls, d_out, d_nffn, d_gate, d_up, d_down, d_nfin,
            nm_meta, nm_nmix, nm_in, nm_bg, nm_cw, nm_nmls, nm_out, nm_nffn, nm_gate, nm_up, nm_down, nm_nfin,
            nv_meta, nv_nmix, nv_in, nv_bg, nv_cw, nv_nmls, nv_out, nv_nffn, nv_gate, nv_up, nv_down, nv_nfin)
```

```python
import numpy as np
import jax
import jax.numpy as jnp
from jax import lax
from jax.experimental import pallas as pl
from jax.experimental.pallas import tpu as pltpu

F32 = jnp.float32
BF16 = jnp.bfloat16
MESH = pl.DeviceIdType.MESH

D_MODEL = 2048
DEPTH = 2
N_META = 16
MLSTM_W = 1024
CONV_W = 1024
HEADS = 4
DV = 256
DQK = 128
QK_W = 512
CHUNK = 64
PAD_FRONT = 48
TOK0 = PAD_FRONT + N_META
D_FF = 5632
N_DEV = 8
FF_SH = D_FF // N_DEV
D_IN = 6152
IN_SH = D_IN // N_DEV
OUT_SH = D_MODEL // N_DEV
GATE_COL = 3072
PM_W = GATE_COL + 128
GATE_CAP = 15.0
EPS = 1e-6
QSCALE = DQK ** -0.5

ADAM_LR = 0.001
ADAM_B1 = 0.9
ADAM_B2 = 0.999
ADAM_EPS = 1e-08
ADAM_WD = 0.01
ADAM_STEP = 10

V7X_VMEM_LIMIT = 50 * 1024 * 1024
V7X_MXU_COLS = 256


def _params(**kw):
    return pltpu.CompilerParams(vmem_limit_bytes=V7X_VMEM_LIMIT, **kw)


def _tile(n, target, mult):
    best = None
    for t in range(mult, min(n, target) + 1, mult):
        if n % t == 0:
            best = t
    return best if best is not None else n


def _sigmoid(x):
    return 1.0 / (1.0 + jnp.exp(-x))


NN = ((1,), (0,))
NT = ((1,), (1,))
TN = ((0,), (0,))


def _matmul(name, a, b, out_shape, out_dtype, grid, a_bs, b_bs, o_bs, dims, nk, acc_shape=None,
            res=None, res_bs=None, dep=None):
    has_res = res is not None
    n_in = 2 + has_res + (dep is not None)

    def body(*refs):
        a_ref, b_ref = refs[0], refs[1]
        r_ref = refs[2] if has_res else None
        o_ref = refs[n_in]
        x = lax.dot_general(a_ref[...], b_ref[...], (dims, ((), ())), preferred_element_type=F32)
        if nk == 1:
            if has_res:
                x = x + r_ref[...]
            o_ref[...] = x.astype(o_ref.dtype)
            return
        acc = refs[n_in + 1]
        k = pl.program_id(len(grid) - 1)

        @pl.when(k == 0)
        def _():
            acc[...] = (x + r_ref[...]) if has_res else x

        @pl.when(k > 0)
        def _():
            acc[...] += x

        @pl.when(k == nk - 1)
        def _():
            o_ref[...] = acc[...].astype(o_ref.dtype)

    ins = [a, b] + ([res] if has_res else [])
    specs = [a_bs, b_bs] + ([res_bs] if has_res else [])
    if dep is not None:
        ins.append(dep)
        specs.append(pl.BlockSpec((8, 128), lambda *_: (0, 0)))
    scratch = [pltpu.VMEM(acc_shape, F32)] if nk > 1 else []
    return pl.pallas_call(
        body, name=name, grid=grid, in_specs=specs, out_specs=o_bs,
        out_shape=jax.ShapeDtypeStruct(out_shape, out_dtype), scratch_shapes=scratch,
        compiler_params=_params(),
    )(*ins)


def _mm_nn(name, a, b, out_dtype, res=None, tm=1056, tn=512, dep=None):
    r, k = a.shape
    n = b.shape[1]
    tm, tn = _tile(r, tm, 8), _tile(n, tn, 128)
    return _matmul(name, a, b, (r, n), out_dtype, (r // tm, n // tn, 1),
                   pl.BlockSpec((tm, k), lambda i, j, s: (i, 0)),
                   pl.BlockSpec((k, tn), lambda i, j, s: (0, j)),
                   pl.BlockSpec((tm, tn), lambda i, j, s: (i, j)), NN, 1,
                   res=res, res_bs=pl.BlockSpec((tm, tn), lambda i, j, s: (i, j)), dep=dep)


def _mm_nn_two(name, a, b, a3, b3, dep=None, tm=1056, tn=512):
    r, k = a.shape
    e, _, kb = a3.shape
    n = b.shape[1]
    tm, tn = _tile(r, tm, 8), _tile(n, tn, 128)

    def body(a_ref, b_ref, a3_ref, b3_ref, *rest):
        acc = lax.dot_general(a_ref[...], b_ref[...], (NN, ((), ())), preferred_element_type=F32)
        for s in range(e):
            acc = acc + lax.dot_general(a3_ref[s], b3_ref[s], (NN, ((), ())), preferred_element_type=F32)
        rest[-1][...] = acc

    ins = [a, b, a3, b3]
    specs = [pl.BlockSpec((tm, k), lambda i, j: (i, 0)), pl.BlockSpec((k, tn), lambda i, j: (0, j)),
             pl.BlockSpec((e, tm, kb), lambda i, j: (0, i, 0)), pl.BlockSpec((e, kb, tn), lambda i, j: (0, 0, j))]
    if dep is not None:
        ins.append(dep)
        specs.append(pl.BlockSpec((8, 128), lambda *_: (0, 0)))
    return pl.pallas_call(
        body, name=name, grid=(r // tm, n // tn), in_specs=specs,
        out_specs=pl.BlockSpec((tm, tn), lambda i, j: (i, j)),
        out_shape=jax.ShapeDtypeStruct((r, n), F32), compiler_params=_params(),
    )(*ins)


def _mm_nt(name, a, b, out_dtype, res=None, tm=1056, tn=512, tk=640, n=None, dep=None):
    r, k = a.shape
    n = b.shape[0] if n is None else n
    tm, tn, tk = _tile(r, tm, 8), _tile(n, tn, 128), _tile(k, tk, 128)
    nk = k // tk
    return _matmul(name, a, b, (r, n), out_dtype, (r // tm, n // tn, nk),
                   pl.BlockSpec((tm, tk), lambda i, j, s: (i, s)),
                   pl.BlockSpec((tn, tk), lambda i, j, s: (j, s)),
                   pl.BlockSpec((tm, tn), lambda i, j, s: (i, j)), NT, nk, acc_shape=(tm, tn),
                   res=res, res_bs=pl.BlockSpec((tm, tn), lambda i, j, s: (i, j)), dep=dep)


def _mm_nt_bcols(name, a, b3, out_dtype, tm=1056, dep=None):
    r, k = a.shape
    e, n, _ = b3.shape
    tm = _tile(r, tm, 8)
    return _matmul(name, a, b3, (e, r, n), out_dtype, (r // tm, e, 1),
                   pl.BlockSpec((tm, k), lambda i, g, s: (i, 0)),
                   pl.BlockSpec((None, n, k), lambda i, g, s: (g, 0, 0)),
                   pl.BlockSpec((None, tm, n), lambda i, g, s: (g, i, 0)), NT, 1, dep=dep)


def _mm_tn(name, a, b, out_dtype, tm=1024, tn=640, dep=None):
    r, m = a.shape
    n = b.shape[1]
    tm, tn = _tile(m, tm, 128), _tile(n, tn, 128)
    return _matmul(name, a, b, (m, n), out_dtype, (m // tm, n // tn, 1),
                   pl.BlockSpec((r, tm), lambda i, j, s: (0, i)),
                   pl.BlockSpec((r, tn), lambda i, j, s: (0, j)),
                   pl.BlockSpec((tm, tn), lambda i, j, s: (i, j)), TN, 1, dep=dep)


def _mm_tn_acols(name, a3, b, out_dtype, tn=1024, dep=None):
    e, r, m = a3.shape
    n = b.shape[1]
    tn = _tile(n, tn, 128)
    return _matmul(name, a3, b, (e, m, n), out_dtype, (n // tn, e, 1),
                   pl.BlockSpec((None, r, m), lambda j, g, s: (g, 0, 0)),
                   pl.BlockSpec((r, tn), lambda j, g, s: (0, j)),
                   pl.BlockSpec((None, m, tn), lambda j, g, s: (g, 0, j)), TN, 1, dep=dep)


def _norm_proj(name, h, w, b, n, tm=1056, tn=640):
    r, d = h.shape
    tm, tn = _tile(r, tm, 8), _tile(n, tn, 128)

    def body(h_ref, w_ref, b_ref, hn_ref, o_ref):
        @pl.when(pl.program_id(1) == 0)
        def _():
            x = h_ref[...]
            rs = lax.rsqrt(jnp.mean(x * x, axis=1, keepdims=True) + EPS)
            hn_ref[...] = (x * rs * w_ref[...]).astype(BF16)

        o_ref[...] = lax.dot_general(hn_ref[...], b_ref[...], (NT, ((), ())), preferred_element_type=F32)

    row = pl.BlockSpec((tm, d), lambda i, j: (i, 0))
    return pl.pallas_call(
        body, name=name, grid=(r // tm, n // tn),
        in_specs=[row, pl.BlockSpec((1, d), lambda i, j: (0, 0)), pl.BlockSpec((tn, d), lambda i, j: (j, 0))],
        out_specs=[row, pl.BlockSpec((tm, tn), lambda i, j: (i, j))],
        out_shape=[jax.ShapeDtypeStruct((r, d), BF16), jax.ShapeDtypeStruct((r, n), F32)],
        compiler_params=_params(),
    )(h, w, b)


def _proj_res_norm(name, a, b, res, w, tm=528):
    r, k = a.shape
    d = b.shape[1]
    tm = _tile(r, tm, 8)

    def body(a_ref, b_ref, r_ref, w_ref, y_ref, n_ref):
        y = lax.dot_general(a_ref[...], b_ref[...], (NN, ((), ())), preferred_element_type=F32) + r_ref[...]
        y_ref[...] = y
        rs = lax.rsqrt(jnp.mean(y * y, axis=1, keepdims=True) + EPS)
        n_ref[...] = (y * rs * w_ref[...]).astype(BF16)

    row = pl.BlockSpec((tm, d), lambda i: (i, 0))
    return pl.pallas_call(
        body, name=name, grid=(r // tm,),
        in_specs=[pl.BlockSpec((tm, k), lambda i: (i, 0)), pl.BlockSpec((k, d), lambda i: (0, 0)), row,
                  pl.BlockSpec((1, d), lambda i: (0, 0))],
        out_specs=[row, row],
        out_shape=[jax.ShapeDtypeStruct((r, d), F32), jax.ShapeDtypeStruct((r, d), BF16)],
        compiler_params=_params(),
    )(a, b, res, w)


def _rms_bwd(name, x, w, dy, dres):
    r, d = x.shape
    tr = _tile(r, 264, 8)

    def body(x_ref, w_ref, dy_ref, dr_ref, dx_ref, dxb_ref, dw_ref):
        xv = x_ref[...]
        g = dy_ref[...]
        rs = lax.rsqrt(jnp.mean(xv * xv, axis=1, keepdims=True) + EPS)
        wg = g * w_ref[...]
        dx = rs * wg - xv * (rs * rs * rs) * jnp.mean(xv * wg, axis=1, keepdims=True) + dr_ref[...]
        dx_ref[...] = dx
        dxb_ref[...] = dx.astype(BF16)
        part = jnp.sum(g * xv * rs, axis=0, keepdims=True)

        @pl.when(pl.program_id(0) == 0)
        def _():
            dw_ref[...] = part

        @pl.when(pl.program_id(0) > 0)
        def _():
            dw_ref[...] += part

    row = pl.BlockSpec((tr, d), lambda i: (i, 0))
    vec = pl.BlockSpec((1, d), lambda i: (0, 0))
    return pl.pallas_call(
        body, name=name, grid=(r // tr,), in_specs=[row, vec, row, row], out_specs=[row, row, vec],
        out_shape=[jax.ShapeDtypeStruct((r, d), F32), jax.ShapeDtypeStruct((r, d), BF16),
                   jax.ShapeDtypeStruct((1, d), F32)],
        compiler_params=_params(),
    )(x, w, dy, dres)


def _final_loss(name, h, w, target):
    r, d = h.shape
    nb = r // CHUNK

    def body(h_ref, w_ref, t_ref, dh_ref, dhb_ref, dw_ref, ls_ref):
        i = pl.program_id(0)

        @pl.when(i == 0)
        def _():
            dh_ref[...] = jnp.zeros_like(dh_ref)
            dhb_ref[...] = jnp.zeros_like(dhb_ref)
            dw_ref[...] = jnp.zeros_like(dw_ref)
            ls_ref[...] = jnp.zeros_like(ls_ref)

        @pl.when(i > 0)
        def _():
            xv = h_ref[...]
            wv = w_ref[...]
            rs = lax.rsqrt(jnp.mean(xv * xv, axis=1, keepdims=True) + EPS)
            err = xv * rs * wv - t_ref[...]
            sq = jnp.sum(jnp.sum(err * err, axis=1, keepdims=True), axis=0, keepdims=True)
            ls_ref[...] += jnp.broadcast_to(sq * (0.5 / d), ls_ref.shape)
            g = err * (1.0 / d)
            wg = g * wv
            dx = rs * wg - xv * (rs * rs * rs) * jnp.mean(xv * wg, axis=1, keepdims=True)
            dh_ref[...] = dx
            dhb_ref[...] = dx.astype(BF16)
            dw_ref[...] += jnp.sum(g * xv * rs, axis=0, keepdims=True)

    row = pl.BlockSpec((CHUNK, d), lambda i: (i, 0))
    vec = pl.BlockSpec((1, d), lambda i: (0, 0))
    return pl.pallas_call(
        body, name=name, grid=(nb,),
        in_specs=[row, vec, pl.BlockSpec((CHUNK, d), lambda i: (jnp.maximum(i - 1, 0), 0))],
        out_specs=[row, row, vec, pl.BlockSpec((1, 128), lambda i: (0, 0))],
        out_shape=[jax.ShapeDtypeStruct((r, d), F32), jax.ShapeDtypeStruct((r, d), BF16),
                   jax.ShapeDtypeStruct((1, d), F32), jax.ShapeDtypeStruct((1, 128), F32)],
        compiler_params=_params(),
    )(h, w, target)


def _ffn_in(name, hf, wg_t, wu_t, dep=None, tm=1056, tn=512):
    r, d = hf.shape
    f = wg_t.shape[0]
    tm, tn = _tile(r, tm, 8), _tile(f, tn, 128)

    def body(h_ref, wg_ref, wu_ref, *rest):
        g_ref, u_ref, a_ref = rest[-3:]
        x = h_ref[...]
        g = lax.dot_general(x, wg_ref[...], (NT, ((), ())), preferred_element_type=F32)
        u = lax.dot_general(x, wu_ref[...], (NT, ((), ())), preferred_element_type=F32)
        g_ref[...] = g.astype(BF16)
        u_ref[...] = u.astype(BF16)
        a_ref[...] = (g * _sigmoid(g) * u).astype(BF16)

    wspec = pl.BlockSpec((tn, d), lambda i, j: (j, 0))
    ospec = pl.BlockSpec((tm, tn), lambda i, j: (i, j))
    ins, specs = [hf, wg_t, wu_t], [pl.BlockSpec((tm, d), lambda i, j: (i, 0)), wspec, wspec]
    if dep is not None:
        ins.append(dep)
        specs.append(pl.BlockSpec((8, 128), lambda *_: (0, 0)))
    return pl.pallas_call(
        body, name=name, grid=(r // tm, f // tn), in_specs=specs, out_specs=[ospec] * 3,
        out_shape=[jax.ShapeDtypeStruct((r, f), BF16)] * 3, compiler_params=_params(),
    )(*ins)


def _ffn_act_bwd(name, dh, wd, g, u, dep=None, tm=1056, tn=512):
    r, d = dh.shape
    f = wd.shape[0]
    tm, tn = _tile(r, tm, 8), _tile(f, tn, 128)

    def body(dh_ref, wd_ref, g_ref, u_ref, *rest):
        dg_ref, du_ref = rest[-2:]
        tr = _tile(tm, 264, 8)
        for r0 in range(0, tm, tr):
            for c0 in range(0, tn, V7X_MXU_COLS):
                rows, cols = slice(r0, r0 + tr), slice(c0, c0 + V7X_MXU_COLS)
                da = lax.dot_general(dh_ref[rows, :], wd_ref[cols, :], (NT, ((), ())), preferred_element_type=F32)
                gv = g_ref[rows, cols]
                s = _sigmoid(gv)
                t = da.astype(BF16) * s
                du_ref[rows, cols] = t * gv
                dg_ref[rows, cols] = t * u_ref[rows, cols] * (1.0 + gv - gv * s)

    tile = pl.BlockSpec((tm, tn), lambda i, j: (i, j))
    ins = [dh, wd, g, u]
    specs = [pl.BlockSpec((tm, d), lambda i, j: (i, 0)), pl.BlockSpec((tn, d), lambda i, j: (j, 0)), tile, tile]
    if dep is not None:
        ins.append(dep)
        specs.append(pl.BlockSpec((8, 128), lambda *_: (0, 0)))
    return pl.pallas_call(
        body, name=name, grid=(r // tm, f // tn), in_specs=specs, out_specs=[tile] * 2,
        out_shape=[jax.ShapeDtypeStruct((r, f), BF16)] * 2, compiler_params=_params(),
    )(*ins)


def _shift_down(a, k):
    row = lax.broadcasted_iota(jnp.int32, a.shape, 0)
    return jnp.where(row >= k, pltpu.roll(a, k, 0), 0.0)


def _shift_up(a, k):
    n = a.shape[0]
    row = lax.broadcasted_iota(jnp.int32, a.shape, 0)
    return jnp.where(row < n - k, pltpu.roll(a, n - k, 0), 0.0)


def _conv_fwd(name, pc, cw, cat):
    _, r, w = pc.shape
    nblk = w // 128

    def body(pc_ref, cw_ref, cat_ref, o_ref):
        a = pc_ref[2] * pc_ref[0]
        cwv = cw_ref[...]
        conv = _shift_down(a, 2) * cwv[0:1] + _shift_down(a, 1) * cwv[1:2] + a * cwv[2:3]
        o_ref[...] = (pc_ref[1] * conv).astype(BF16)

    return pl.pallas_call(
        body, name=name, grid=(nblk,),
        in_specs=[pl.BlockSpec((3, r, 128), lambda j: (0, 0, j)), pl.BlockSpec((8, 128), lambda j: (0, j)),
                  pl.BlockSpec(memory_space=pl.ANY)],
        out_specs=pl.BlockSpec((r, 128), lambda j: (0, nblk + j)),
        out_shape=jax.ShapeDtypeStruct(cat.shape, BF16), input_output_aliases={2: 0},
        compiler_params=_params(),
    )(pc, cw, cat)


def _conv_bwd(name, dcat, pc, cw):
    _, r, w = pc.shape
    nblk = w // 128

    def body(dy_ref, pc_ref, cw_ref, dpc_ref, dcw_ref):
        u, gb, gc = pc_ref[0], pc_ref[1], pc_ref[2]
        cwv = cw_ref[...]
        dy = dy_ref[...]
        a = gc * u
        a1, a2 = _shift_down(a, 1), _shift_down(a, 2)
        conv = a2 * cwv[0:1] + a1 * cwv[1:2] + a * cwv[2:3]
        dconv = dy * gb
        da = dconv * cwv[2:3] + _shift_up(dconv, 1) * cwv[1:2] + _shift_up(dconv, 2) * cwv[0:1]
        dpc_ref[0] = (da * gc).astype(BF16)
        dpc_ref[1] = (dy * conv).astype(BF16)
        dpc_ref[2] = (da * u).astype(BF16)
        row = lax.broadcasted_iota(jnp.int32, (8, 128), 0)
        dw0 = jnp.sum(dconv * a2, axis=0, keepdims=True)
        dw1 = jnp.sum(dconv * a1, axis=0, keepdims=True)
        dw2 = jnp.sum(dconv * a, axis=0, keepdims=True)
        dcw_ref[...] = jnp.where(row == 0, dw0, jnp.where(row == 1, dw1, jnp.where(row == 2, dw2, 0.0)))

    return pl.pallas_call(
        body, name=name, grid=(nblk,),
        in_specs=[pl.BlockSpec((r, 128), lambda j: (0, nblk + j)),
                  pl.BlockSpec((3, r, 128), lambda j: (0, 0, j)), pl.BlockSpec((8, 128), lambda j: (0, j))],
        out_specs=[pl.BlockSpec((3, r, 128), lambda j: (0, 0, j)), pl.BlockSpec((8, 128), lambda j: (0, j))],
        out_shape=[jax.ShapeDtypeStruct((3, r, w), BF16), jax.ShapeDtypeStruct((8, w), F32)],
        compiler_params=_params(),
    )(dcat, pc, cw)


def _dot(a, b, dims):
    return lax.dot_general(a, b, (dims, ((), ())), preferred_element_type=F32)


def _col_to_row(xc, eye):
    return jnp.sum(jnp.where(eye, xc, 0.0), axis=0, keepdims=True)


def _row_to_col(xr, eye):
    return jnp.sum(jnp.where(eye, xr, 0.0), axis=1, keepdims=True)


def _gate_tiles(graw, bias, row0):
    th = jnp.tanh((graw + bias) / GATE_CAP)
    z = GATE_CAP * th
    row = lax.broadcasted_iota(jnp.int32, graw.shape, 0) + row0
    real = row >= PAD_FRONT
    li = jnp.where(real, z, -jnp.inf)
    lf = jnp.where(real, jnp.minimum(z, 0.0) - jnp.log(1.0 + jnp.exp(-jnp.abs(z))), 0.0)
    return th, z, li, lf, real


def _interleave(gens):
    results = [None] * len(gens)
    live = list(enumerate(gens))
    while live:
        still = []
        for i, gen in live:
            try:
                next(gen)
                still.append((i, gen))
            except StopIteration as stop:
                results[i] = stop.value
        live = still
    return results


def _chunk_common(pm, h, li, lf, cst, nst, mst, tril, eye):
    kraw = pm[:, QK_W + h * DQK:QK_W + (h + 1) * DQK]
    q = (pm[:, h * DQK:(h + 1) * DQK] * QSCALE).astype(BF16)
    yield
    k = kraw.astype(BF16)
    v = pm[:, 2 * QK_W + h * DV:2 * QK_W + (h + 1) * DV].astype(BF16)
    yield
    li_c = li[:, h:h + 1]
    lf_c = lf[:, HEADS + h:HEADS + h + 1]
    li_r = _col_to_row(li_c, eye)
    yield
    lf_r = _col_to_row(lf_c, eye)
    yield
    b_c = jnp.sum(jnp.where(tril, lf_r, 0.0), axis=1, keepdims=True)
    yield
    b_r = _col_to_row(b_c, eye)
    yield
    dmat = jnp.where(tril, b_c - b_r + li_r, -jnp.inf)
    inter = b_c + mst
    yield
    mt = jnp.maximum(inter, jnp.max(dmat, axis=1, keepdims=True))
    yield
    w_inter = jnp.exp(inter - mt)
    p = jnp.exp(dmat - mt)
    yield
    s = _dot(q, k, NT) * p
    yield
    cb = cst.astype(BF16)
    nb = nst.astype(BF16).astype(F32)
    qc = _dot(q, cb, NN)
    yield
    qn = jnp.sum(q.astype(F32) * nb, axis=1, keepdims=True)
    yield
    den = w_inter * qn + jnp.sum(s, axis=1, keepdims=True)
    yield
    dn = jnp.maximum(jnp.abs(den), jnp.exp(-mt))
    b_end = b_c[CHUNK - 1:CHUNK, :]
    decay = b_end - b_c + li_c
    yield
    m_new = jnp.maximum(b_end + mst, jnp.max(decay, axis=0, keepdims=True))
    yield
    w_old = jnp.exp(b_end + mst - m_new)
    w_in = jnp.exp(decay - m_new)
    kw = (w_in * kraw).astype(BF16)
    yield
    return dict(q=q, k=k, v=v, kraw=kraw, mt=mt, w_inter=w_inter, p=p, s=s, cb=cb, nb=nb, qc=qc, qn=qn,
                den=den, dn=dn, m_new=m_new, w_old=w_old, w_in=w_in, kw=kw)


def _chunks_per_step(nc, want):
    return want if nc % want == 0 else 1


def _mlstm_fwd(name, pm, bias, nw):
    r = pm.shape[0]
    nc = r // CHUNK
    grp = _chunks_per_step(nc, 3)

    def body(pm_ref, b_ref, nw_ref, hm_ref, ht_ref, cs_ref, ns_ref, ms_ref, c_scr, n_scr, m_scr):
        step = pl.program_id(0)

        @pl.when(step == 0)
        def _():
            c_scr[...] = jnp.zeros_like(c_scr)
            n_scr[...] = jnp.zeros_like(n_scr)
            m_scr[...] = jnp.zeros_like(m_scr)

        rr = lax.broadcasted_iota(jnp.int32, (CHUNK, CHUNK), 0)
        cc = lax.broadcasted_iota(jnp.int32, (CHUNK, CHUNK), 1)
        tril, eye = cc <= rr, cc == rr
        bv, nwv = b_ref[...], nw_ref[...]
        states = [(c_scr[h], n_scr[h], m_scr[h]) for h in range(HEADS)]
        for g in range(grp):
            rows = slice(g * CHUNK, (g + 1) * CHUNK)
            pmv = pm_ref[rows, :]
            _, _, li, lf, _ = _gate_tiles(pmv[:, GATE_COL:GATE_COL + 128], bv, (step * grp + g) * CHUNK)
            def head(h, cst, nst, mst, g=g, rows=rows, pmv=pmv, li=li, lf=lf):
                f = yield from _chunk_common(pmv, h, li, lf, cst, nst, mst, tril, eye)
                num = f["w_inter"] * f["qc"] + _dot(f["s"].astype(BF16), f["v"], NN)
                yield
                hh = num / f["dn"]
                yield
                c_new = f["w_old"] * cst + _dot(f["kw"], f["v"], TN)
                yield
                n_new = f["w_old"] * nst + jnp.sum(
                    f["w_in"].astype(BF16).astype(F32) * f["k"].astype(F32), axis=0, keepdims=True)
                yield
                sl = slice(h * DV, (h + 1) * DV)
                rs = lax.rsqrt(jnp.mean(hh * hh, axis=1, keepdims=True) + EPS)
                yield
                og = pmv[:, 2 * QK_W + MLSTM_W + h * DV:2 * QK_W + MLSTM_W + (h + 1) * DV]
                cs_ref[g, h] = cst
                ns_ref[g, h] = nst
                ms_ref[g, h] = mst
                ht_ref[rows, sl] = hh
                yield
                hm_ref[rows, sl] = (_sigmoid(og) * (hh * rs * nwv[:, sl])).astype(BF16)
                return c_new, n_new, f["m_new"]

            states = _interleave([head(h, *states[h]) for h in range(HEADS)])
        for h, (cst, nst, mst) in enumerate(states):
            c_scr[h] = cst
            n_scr[h] = nst
            m_scr[h] = mst

    return pl.pallas_call(
        body, name=name, grid=(nc // grp,),
        in_specs=[pl.BlockSpec((grp * CHUNK, PM_W), lambda i: (i, 0)), pl.BlockSpec((1, 128), lambda i: (0, 0)),
                  pl.BlockSpec((1, MLSTM_W), lambda i: (0, 0))],
        out_specs=[pl.BlockSpec((grp * CHUNK, MLSTM_W), lambda i: (i, 0)),
                   pl.BlockSpec((grp * CHUNK, MLSTM_W), lambda i: (i, 0)),
                   pl.BlockSpec((grp, HEADS, DQK, DV), lambda i: (i, 0, 0, 0)),
                   pl.BlockSpec((grp, HEADS, 1, DQK), lambda i: (i, 0, 0, 0)),
                   pl.BlockSpec((grp, HEADS, 1, 1), lambda i: (i, 0, 0, 0))],
        out_shape=[jax.ShapeDtypeStruct((r, MLSTM_W + CONV_W), BF16), jax.ShapeDtypeStruct((r, MLSTM_W), F32),
                   jax.ShapeDtypeStruct((nc, HEADS, DQK, DV), F32),
                   jax.ShapeDtypeStruct((nc, HEADS, 1, DQK), F32),
                   jax.ShapeDtypeStruct((nc, HEADS, 1, 1), F32)],
        scratch_shapes=[pltpu.VMEM((HEADS, DQK, DV), F32), pltpu.VMEM((HEADS, 1, DQK), F32),
                        pltpu.VMEM((HEADS, 1, 1), F32)],
        compiler_params=_params(),
    )(pm, bias, nw)


def _mlstm_bwd(name, dcat, pm, ht, cs, ns, ms, bias, nw):
    r = pm.shape[0]
    nc = r // CHUNK
    grp = _chunks_per_step(nc, 1)
    nsteps = nc // grp

    def body(dy_ref, pm_ref, ht_ref, cs_ref, ns_ref, ms_ref, b_ref, nw_ref, dpm_ref, dnw_ref, db_ref,
             dc_scr, dn_scr):
        step = pl.program_id(0)

        @pl.when(step == 0)
        def _():
            dc_scr[...] = jnp.zeros_like(dc_scr)
            dn_scr[...] = jnp.zeros_like(dn_scr)
            dnw_ref[...] = jnp.zeros_like(dnw_ref)
            db_ref[...] = jnp.zeros_like(db_ref)

        rr = lax.broadcasted_iota(jnp.int32, (CHUNK, CHUNK), 0)
        cc = lax.broadcasted_iota(jnp.int32, (CHUNK, CHUNK), 1)
        tril, eye, triu = cc <= rr, cc == rr, cc >= rr
        lane = lax.broadcasted_iota(jnp.int32, (CHUNK, 128), 1)
        rowid = lax.broadcasted_iota(jnp.int32, (CHUNK, 1), 0)
        bv, nwv = b_ref[...], nw_ref[...]
        carried = [(dc_scr[h], dn_scr[h]) for h in range(HEADS)]
        dnw_acc = [jnp.zeros((1, DV), F32) for _ in range(HEADS)]
        db_acc = jnp.zeros((1, 128), F32)
        for g in reversed(range(grp)):
            rows = slice(g * CHUNK, (g + 1) * CHUNK)
            ci = (nsteps - 1 - step) * grp + g
            pmv = pm_ref[rows, :]
            th, z, li, lf, real = _gate_tiles(pmv[:, GATE_COL:GATE_COL + 128], bv, ci * CHUNK)
            heads = _interleave([
                _mlstm_bwd_head(h, pmv, ht_ref[rows, h * DV:(h + 1) * DV], dy_ref[rows, h * DV:(h + 1) * DV], nwv,
                                li, lf, cs_ref[g, h], ns_ref[g, h], ms_ref[g, h], carried[h][0], carried[h][1],
                                tril, eye, triu, lane, rowid, dpm_ref, rows)
                for h in range(HEADS)])
            carried = [(dc_new, dn_new) for _, dc_new, dn_new, _ in heads]
            dgt = heads[0][0] + heads[1][0] + heads[2][0] + heads[3][0]
            dnw_acc = [dnw_acc[h] + heads[h][3] for h in range(HEADS)]
            dact = jnp.where(lane < HEADS, 1.0, 1.0 - _sigmoid(z)) * (1.0 - th * th)
            dgraw = jnp.where(real & (lane < 2 * HEADS), dgt * dact, 0.0)
            dpm_ref[rows, GATE_COL:GATE_COL + 128] = dgraw.astype(BF16)
            db_acc = db_acc + jnp.sum(dgraw, axis=0, keepdims=True)
        for h, (dcn, dnn) in enumerate(carried):
            dc_scr[h] = dcn
            dn_scr[h] = dnn
            dnw_ref[:, h * DV:(h + 1) * DV] += dnw_acc[h]
        db_ref[...] += db_acc

    rev = lambda i: (nsteps - 1 - i, 0)
    rev4 = lambda i: (nsteps - 1 - i, 0, 0, 0)
    return pl.pallas_call(
        body, name=name, grid=(nsteps,),
        in_specs=[pl.BlockSpec((grp * CHUNK, MLSTM_W), rev), pl.BlockSpec((grp * CHUNK, PM_W), rev),
                  pl.BlockSpec((grp * CHUNK, MLSTM_W), rev),
                  pl.BlockSpec((grp, HEADS, DQK, DV), rev4), pl.BlockSpec((grp, HEADS, 1, DQK), rev4),
                  pl.BlockSpec((grp, HEADS, 1, 1), rev4),
                  pl.BlockSpec((1, 128), lambda i: (0, 0)), pl.BlockSpec((1, MLSTM_W), lambda i: (0, 0))],
        out_specs=[pl.BlockSpec((grp * CHUNK, PM_W), rev), pl.BlockSpec((1, MLSTM_W), lambda i: (0, 0)),
                   pl.BlockSpec((1, 128), lambda i: (0, 0))],
        out_shape=[jax.ShapeDtypeStruct((r, PM_W), BF16), jax.ShapeDtypeStruct((1, MLSTM_W), F32),
                   jax.ShapeDtypeStruct((1, 128), F32)],
        scratch_shapes=[pltpu.VMEM((HEADS, DQK, DV), F32), pltpu.VMEM((HEADS, 1, DQK), F32)],
        compiler_params=_params(),
    )(dcat, pm, ht, cs, ns, ms, bias, nw)


def _mlstm_bwd_head(h, pmv, hh, y, nwv, li, lf, cst, nst, mst, dcn, dnn, tril, eye, triu, lane, rowid,
                    dpm_ref, rows):
    f = yield from _chunk_common(pmv, h, li, lf, cst, nst, mst, tril, eye)
    q, k, v, s, p = f["q"], f["k"], f["v"], f["s"], f["p"]
    w_inter, w_in, w_old, dn = f["w_inter"], f["w_in"], f["w_old"], f["dn"]
    osl = slice(2 * QK_W + MLSTM_W + h * DV, 2 * QK_W + MLSTM_W + (h + 1) * DV)
    sg = _sigmoid(pmv[:, osl])
    yield
    rs = lax.rsqrt(jnp.mean(hh * hh, axis=1, keepdims=True) + EPS)
    yield
    nwh = nwv[:, h * DV:(h + 1) * DV]
    dpm_ref[rows, osl] = (y * (hh * rs * nwh) * sg * (1.0 - sg)).astype(BF16)
    yield
    dhn = y * sg
    dnw_h = jnp.sum(dhn * hh * rs, axis=0, keepdims=True)
    yield
    wd = dhn * nwh
    dhh = rs * wd - hh * (rs * rs * rs) * jnp.mean(hh * wd, axis=1, keepdims=True)
    yield
    dnum = dhh / dn
    dd = -jnp.sum(dhh * hh, axis=1, keepdims=True) / dn
    yield
    dden = jnp.where(jnp.abs(f["den"]) > jnp.exp(-f["mt"]), dd * jnp.sign(f["den"]), 0.0)
    dnum_b = dnum.astype(BF16)
    wdn = (w_inter * dnum).astype(BF16)
    wid = (w_inter * dden).astype(BF16).astype(F32)
    yield
    ds = _dot(dnum_b, v, NT) + dden
    yield
    dsp = (ds * p).astype(BF16)
    yield
    dq = _dot(dsp, k, NN) + _dot(wdn, f["cb"], NT) + wid * f["nb"]
    yield
    dk = _dot(dsp, q, TN)
    yield
    dv = _dot(s.astype(BF16), dnum_b, TN)
    yield
    g = ds * s
    g_col = _row_to_col(jnp.sum(g, axis=0, keepdims=True), eye)
    yield
    db = jnp.sum(g, axis=1, keepdims=True) - g_col
    dli = g_col
    yield
    db = db + (jnp.sum(dnum * f["qc"], axis=1, keepdims=True) + dden * f["qn"]) * w_inter
    yield
    dcnb = dcn.astype(BF16)
    dnnb = dnn.astype(BF16).astype(F32)
    dkw = _dot(v, dcnb, NT) + dnnb
    yield
    dk = dk + w_in * dkw
    dv = dv + _dot(f["kw"], dcnb, NN)
    yield
    ddecay = jnp.sum(dkw * f["kraw"], axis=1, keepdims=True) * w_in
    yield
    dw_old = (jnp.sum(jnp.sum(dcn * cst, axis=1, keepdims=True), axis=0, keepdims=True)
              + jnp.sum(dnn * nst, axis=1, keepdims=True))
    yield
    db_end = dw_old * w_old + jnp.sum(ddecay, axis=0, keepdims=True)
    db = db - ddecay + jnp.where(rowid == CHUNK - 1, db_end, 0.0)
    dli = dli + ddecay
    yield
    dc_new = w_old * dcn + _dot(q, wdn, TN)
    yield
    dn_new = w_old * dnn + jnp.sum(wid * q.astype(F32), axis=0, keepdims=True)
    yield
    dlf = jnp.sum(jnp.where(triu, _col_to_row(db, eye), 0.0), axis=1, keepdims=True)
    yield
    gate_part = jnp.where(lane == h, dli, 0.0) + jnp.where(lane == HEADS + h, dlf, 0.0)
    dpm_ref[rows, h * DQK:(h + 1) * DQK] = (dq * QSCALE).astype(BF16)
    yield
    dpm_ref[rows, QK_W + h * DQK:QK_W + (h + 1) * DQK] = dk.astype(BF16)
    yield
    dpm_ref[rows, 2 * QK_W + h * DV:2 * QK_W + (h + 1) * DV] = dv.astype(BF16)
    return gate_part, dc_new, dn_new, dnw_h


def _my_place():
    return lax.axis_index("x"), lax.axis_index("y"), lax.axis_index("c")


def _flip(v, bit):
    return 1 - v if bit else v


def _exchange_small(name, blk, reduce):
    r, c = blk.shape

    def body(x_ref, o_ref, *rest):
        slots = rest[0] if reduce else o_ref
        send_sems, recv_sems = rest[-2], rest[-1]
        x, y, cc = _my_place()
        me = 4 * x + 2 * y + cc
        slots[me] = x_ref[...]
        copies = []
        for k in range(1, N_DEV):
            peer = (_flip(x, k & 4), _flip(y, k & 2), _flip(cc, k & 1))
            cp = pltpu.make_async_remote_copy(
                src_ref=x_ref, dst_ref=slots.at[me], send_sem=send_sems.at[k - 1],
                recv_sem=recv_sems.at[k - 1], device_id=peer, device_id_type=MESH)
            cp.start()
            copies.append(cp)
        for cp in copies:
            cp.wait()
        if reduce:
            acc = slots[0]
            for d in range(1, N_DEV):
                acc = acc + slots[d]
            o_ref[...] = acc

    scratch = ([pltpu.VMEM((N_DEV, r, c), F32)] if reduce else []) + [
        pltpu.SemaphoreType.DMA((N_DEV - 1,)), pltpu.SemaphoreType.DMA((N_DEV - 1,))]
    return pl.pallas_call(
        body, name=name,
        out_shape=jax.ShapeDtypeStruct((r, c) if reduce else (N_DEV, r, c), F32),
        in_specs=[pl.BlockSpec(memory_space=pltpu.VMEM)], out_specs=pl.BlockSpec(memory_space=pltpu.VMEM),
        scratch_shapes=scratch, compiler_params=_params(),
    )(blk)


HBM_SPEC = pl.BlockSpec(memory_space=pltpu.HBM)
SEM_SPEC = pl.BlockSpec(memory_space=pltpu.SEMAPHORE)
ANY_SPEC = pl.BlockSpec(memory_space=pl.ANY)
DATAFLOW = pltpu.SideEffectType.DATAFLOW_SIDE_EFFECTING


def _split_copy(name, arrays, start=None, wait=None, after=None):
    results, token = _split_copies(name, [(arrays, start, wait)], after)
    return results[0][0], results[0][1], token


def _split_copies(name, jobs, after=None):
    operands, in_specs, out_shape, out_specs, aliases = [], [], [], [], {}
    in_at, out_at = [], []
    for arrays, start, wait in jobs:
        in_at.append(len(operands))
        operands += [pltpu.with_memory_space_constraint(a, pltpu.HBM) for a in arrays]
        in_specs += [HBM_SPEC] * len(arrays)
        if wait:
            operands += list(wait[1])
            in_specs += [SEM_SPEC, SEM_SPEC]
    if after is not None:
        operands.append(after)
        in_specs.append(ANY_SPEC)
    for j, (arrays, start, wait) in enumerate(jobs):
        out_at.append(len(out_shape))
        if start:
            out_shape += [pltpu.SemaphoreType.DMA((start[1],)), pltpu.SemaphoreType.DMA((start[1],))]
            out_specs += [SEM_SPEC, SEM_SPEC]
        for i, a in enumerate(arrays):
            aliases[in_at[j] + i] = len(out_shape)
            out_shape.append(pltpu.HBM(a.shape, a.dtype))
            out_specs.append(HBM_SPEC)
    any_start = any(start for _, start, _ in jobs)
    if any_start:
        out_shape.append(jax.ShapeDtypeStruct((8, 128), F32))
        out_specs.append(pl.BlockSpec(memory_space=pltpu.VMEM))
    n_in = len(operands)

    def body(*refs):
        for j, (arrays, start, wait) in enumerate(jobs):
            if wait:
                ins = refs[in_at[j]:in_at[j] + len(arrays)]
                at = in_at[j] + len(arrays)
                for cp in wait[0](ins, refs[at], refs[at + 1]):
                    cp.wait_send()
                    cp.wait_recv()
        for j, (arrays, start, wait) in enumerate(jobs):
            if start:
                ins = refs[in_at[j]:in_at[j] + len(arrays)]
                at = n_in + out_at[j]
                for cp in start[0](ins, refs[at], refs[at + 1]):
                    cp.start()
        if any_start:
            token = refs[n_in + len(out_shape) - 1]
            token[...] = jnp.zeros_like(token)

    outs = pl.pallas_call(
        body, name=name, in_specs=in_specs, out_specs=out_specs, out_shape=out_shape,
        input_output_aliases=aliases, compiler_params=pltpu.CompilerParams(has_side_effects=DATAFLOW),
    )(*operands)
    results = []
    for j, (arrays, start, wait) in enumerate(jobs):
        at = out_at[j]
        sems = (outs[at], outs[at + 1]) if start else None
        at += 2 if start else 0
        results.append((list(outs[at:at + len(arrays)]), sems))
    return results, (outs[-1] if any_start else None)


def _remote(src, dst, send_sems, recv_sems, k, to):
    return pltpu.make_async_remote_copy(src_ref=src, dst_ref=dst, send_sem=send_sems.at[k],
                                        recv_sem=recv_sems.at[k], device_id=to, device_id_type=MESH)


def _slot(px, py, pc):
    return 4 * px + 2 * py + pc


def _gather_first(refs, send_sems, recv_sems):
    x, y, c = _my_place()
    blk = refs[0].at[_slot(x, y, c)]
    targets = [(x, y, 1 - c), (1 - x, y, c), (x, 1 - y, c)]
    return [_remote(blk, blk, send_sems, recv_sems, k, to) for k, to in enumerate(targets)]


def _gather_relay(refs, send_sems, recv_sems):
    x, y, c = _my_place()
    rows = refs[0].shape[1]
    half = rows // 32 * 16
    from_x, from_y = _slot(1 - x, y, c), _slot(x, 1 - y, c)
    upper = refs[0].at[from_x, pl.ds(0, half)]
    lower = refs[0].at[from_y, pl.ds(half, rows - half)]
    return [_remote(upper, upper, send_sems, recv_sems, 0, (x, 1 - y, c)),
            _remote(lower, lower, send_sems, recv_sems, 1, (1 - x, y, c)),
            _remote(refs[0].at[from_x], refs[0].at[from_x], send_sems, recv_sems, 2, (x, y, 1 - c)),
            _remote(refs[0].at[from_y], refs[0].at[from_y], send_sems, recv_sems, 3, (x, y, 1 - c))]


def _gather_last(refs, send_sems, recv_sems):
    x, y, c = _my_place()
    blk = refs[0].at[_slot(1 - x, 1 - y, c)]
    return [_remote(blk, blk, send_sems, recv_sems, 0, (x, y, 1 - c))]


def _scatter_sibling(n):
    def copies(refs, send_sems, recv_sems):
        x, y, c = _my_place()
        return [_remote(refs[a].at[2 * j + 1 - c], refs[n + a].at[j], send_sems, recv_sems, 4 * a + j, (x, y, 1 - c))
                for a in range(n) for j in range(4)]
    return copies


def _scatter_chips(n):
    def copies(refs, send_sems, recv_sems):
        x, y, c = _my_place()
        out = []
        for a in range(n):
            for k in range(1, 4):
                px, py = _flip(x, k & 2), _flip(y, k & 1)
                out.append(_remote(refs[a].at[2 * px + py], refs[n + a].at[2 * x + y], send_sems, recv_sems,
                                   3 * a + k - 1, (px, py, c)))
        return out
    return copies


def _pair_sum(name, core, g, t):
    _, r, c = g.shape
    tr = _tile(r, 512, 8)
    g4 = g.reshape(4, 2, r, c)

    def body(core_ref, g_ref, t_ref, o_ref):
        o_ref[...] = (g_ref[...].astype(F32) + t_ref[...].astype(F32)).astype(BF16)

    return pl.pallas_call(
        body, name=name,
        grid_spec=pltpu.PrefetchScalarGridSpec(
            num_scalar_prefetch=1, grid=(4, r // tr),
            in_specs=[pl.BlockSpec((None, None, tr, c), lambda j, i, core_ref: (j, core_ref[0], i, 0)),
                      pl.BlockSpec((None, tr, c), lambda j, i, core_ref: (j, i, 0))],
            out_specs=pl.BlockSpec((None, tr, c), lambda j, i, core_ref: (j, i, 0))),
        out_shape=jax.ShapeDtypeStruct((4, r, c), BF16), compiler_params=_params(),
    )(core, g4, t)


def _adam_math(w, g, m, v):
    m2 = ADAM_B1 * m + (1.0 - ADAM_B1) * g
    v2 = ADAM_B2 * v + (1.0 - ADAM_B2) * (g * g)
    m_hat = m2 / (1.0 - ADAM_B1 ** ADAM_STEP)
    v_hat = v2 / (1.0 - ADAM_B2 ** ADAM_STEP)
    delta = -ADAM_LR * (m_hat / (jnp.sqrt(v_hat) + ADAM_EPS) + ADAM_WD * w)
    return delta, m2, v2


def _adam_sharded(name, chip, w, m, v, grads):
    _, r, c = w.shape
    tr = _tile(r, 256, 8)
    tc = c if tr < r else _tile(c, 256, 128)

    def body(chip_ref, w_ref, m_ref, v_ref, p0_ref, q0_ref, p1_ref, q1_ref, g_ref, d_ref, nm_ref, nv_ref):
        mine = chip_ref[0]

        def total(p_ref, q_ref):
            acc = None
            for j in range(4):
                part = jnp.where(mine == j, p_ref[...], q_ref[j]).astype(F32)
                acc = part if acc is None else acc + part
            return acc

        g = jnp.where(pl.program_id(0) == 0, total(p0_ref, q0_ref), total(p1_ref, q1_ref))
        delta, m2, v2 = _adam_math(w_ref[...], g, m_ref[...], v_ref[...])
        g_ref[...] = g
        d_ref[...] = delta
        nm_ref[...] = m2
        nv_ref[...] = v2

    def grad_specs(layer):
        at = lambda l, i, j: (jnp.where(l == layer, i, 0), jnp.where(l == layer, j, 0))
        return [pl.BlockSpec((None, tr, tc), lambda l, i, j, chip_ref: (chip_ref[0],) + at(l, i, j)),
                pl.BlockSpec((4, tr, tc), lambda l, i, j, chip_ref: (0,) + at(l, i, j))]

    wspec = pl.BlockSpec((None, tr, tc), lambda l, i, j, chip_ref: (l, i, j))
    sds = jax.ShapeDtypeStruct(w.shape, F32)
    return pl.pallas_call(
        body, name=name,
        grid_spec=pltpu.PrefetchScalarGridSpec(
            num_scalar_prefetch=1, grid=(2, r // tr, c // tc),
            in_specs=[wspec, wspec, wspec] + grad_specs(0) + grad_specs(1), out_specs=[wspec] * 4),
        out_shape=[sds] * 4, compiler_params=_params(),
    )(chip, w, m, v, grads[0][0], grads[0][1], grads[1][0], grads[1][1])


def _adam_layer(name, chip, layer, w, m, v, grad, other=None):
    _, r, c = w.shape
    tr = _tile(r, 256, 8)
    tc = c if tr < r else _tile(c, 256, 128)

    def body(chip_ref, w_ref, m_ref, v_ref, p_ref, q_ref, *rest):
        g_ref, d_ref, nm_ref, nv_ref = rest[-4:]
        mine = chip_ref[0]
        g = None
        for j in range(4):
            part = jnp.where(mine == j, p_ref[...], q_ref[j]).astype(F32)
            g = part if g is None else g + part
        delta, m2, v2 = _adam_math(w_ref[...], g, m_ref[...], v_ref[...])
        g_ref[...] = g
        d_ref[...] = delta
        nm_ref[...] = m2
        nv_ref[...] = v2

    wspec = pl.BlockSpec((None, tr, tc), lambda i, j, chip_ref: (layer, i, j))
    in_specs = [wspec, wspec, wspec,
                pl.BlockSpec((None, tr, tc), lambda i, j, chip_ref: (chip_ref[0], i, j)),
                pl.BlockSpec((4, tr, tc), lambda i, j, chip_ref: (0, i, j))]
    ins = [chip, w, m, v, grad[0], grad[1]]
    aliases = {}
    if other is not None:
        aliases = {len(ins) + k: k for k in range(4)}
        ins += list(other)
        in_specs += [pl.BlockSpec(memory_space=pl.ANY)] * 4
    sds = jax.ShapeDtypeStruct(w.shape, F32)
    return pl.pallas_call(
        body, name=name,
        grid_spec=pltpu.PrefetchScalarGridSpec(
            num_scalar_prefetch=1, grid=(r // tr, c // tc), in_specs=in_specs, out_specs=[wspec] * 4),
        out_shape=[sds] * 4, input_output_aliases=aliases, compiler_params=_params(),
    )(*ins)


def _adam_small(name, w, m, v, g):
    def body(w_ref, m_ref, v_ref, g_ref, d_ref, nm_ref, nv_ref):
        delta, m2, v2 = _adam_math(w_ref[...], g_ref[...], m_ref[...], v_ref[...])
        d_ref[...] = delta
        nm_ref[...] = m2
        nv_ref[...] = v2

    sds = jax.ShapeDtypeStruct(w.shape, F32)
    vm = pl.BlockSpec(memory_space=pltpu.VMEM)
    return pl.pallas_call(body, name=name, in_specs=[vm] * 4, out_specs=[vm] * 3, out_shape=[sds] * 3,
                          compiler_params=_params())(w, m, v, g)


GATE_END = GATE_COL + 2 * HEADS


def _split_w_in(gathered):
    win_t = gathered.reshape(D_IN, D_MODEL)
    return win_t, win_t[GATE_END:].reshape(3, CONV_W, D_MODEL)


def _merge_dw_in(dwm_t, dwc_t):
    full = jnp.concatenate([dwm_t[:GATE_END], dwc_t.reshape(3 * CONV_W, D_MODEL)], axis=0)
    return full.reshape(N_DEV, IN_SH, D_MODEL)


def _pack128(parts):
    flat = jnp.concatenate([p.reshape(-1) for p in parts])
    n = flat.shape[0]
    rows = -(-n // 1024) * 8
    return jnp.pad(flat, (0, rows * 128 - n)).reshape(rows, 128)


def _unpack128(packed, shapes):
    flat = packed.reshape(-1)
    out, at = [], 0
    for s in shapes:
        n = int(np.prod(s))
        out.append(flat[at:at + n].reshape(s))
        at += n
    return out


def kernel(x, meta_tokens, norm_mix_w, w_in, b_gates, conv_w, mlstm_norm_w, w_out, norm_ffn_w, w_gate, w_up, w_down, norm_final_w, loss_target, m_meta_tokens, m_norm_mix_w, m_w_in, m_b_gates, m_conv_w, m_mlstm_norm_w, m_w_out, m_norm_ffn_w, m_w_gate, m_w_up, m_w_down, m_norm_final_w, v_meta_tokens, v_norm_mix_w, v_w_in, v_b_gates, v_conv_w, v_mlstm_norm_w, v_w_out, v_norm_ffn_w, v_w_gate, v_w_up, v_w_down, v_norm_final_w):
    seq = x.shape[1]
    rows = TOK0 + seq
    me = 4 * lax.axis_index("x") + 2 * lax.axis_index("y") + lax.axis_index("c")
    meta_sh = meta_tokens.shape[1]
    conv_sh = conv_w.shape[2]

    w_gate_t, m_w_gate_t, v_w_gate_t = (jnp.transpose(a, (0, 2, 1)) for a in (w_gate, m_w_gate, v_w_gate))
    w_up_t, m_w_up_t, v_w_up_t = (jnp.transpose(a, (0, 2, 1)) for a in (w_up, m_w_up, v_w_up))
    shards = []
    for l in range(DEPTH):
        shards += [jnp.transpose(w_in[l]).astype(BF16), w_out[l].astype(BF16), w_gate_t[l].astype(BF16),
                   w_up_t[l].astype(BF16), w_down[l].astype(BF16)]
    per_layer = ("w_in", "w_out", "w_gate", "w_up", "w_down")
    gather_state = {}

    def gather_step(tag, after, start=None, relay=None, last=None, done=()):
        jobs, idx = [], []
        if relay is not None and relay < len(shards):
            jobs.append((gather_state[relay][0], (_gather_relay, 4), (_gather_first, gather_state[relay][1])))
            idx.append(relay)
        if start is not None and start < len(shards):
            buf = lax.dynamic_update_index_in_dim(lax.empty((N_DEV,) + shards[start].shape, BF16), shards[start], me, 0)
            jobs.append(([buf], (_gather_first, 3), None))
            idx.append(start)
        if last is not None:
            jobs.append((gather_state[last][0], (_gather_last, 1), (_gather_relay, gather_state[last][1])))
            idx.append(last)
        for i in done:
            jobs.append((gather_state[i][0], None, (_gather_last, gather_state[i][1])))
            idx.append(i)
        if not jobs:
            return after, []
        results, tok = _split_copies(f"gather_{tag}", jobs, after)
        for i, res in zip(idx, results):
            gather_state[i] = res
        return (after if tok is None else tok), [gather_state[i][0][0] for i in done]

    bias = [jnp.pad(b_gates[l].reshape(1, 2 * HEADS), ((0, 0), (0, 128 - 2 * HEADS))) for l in range(DEPTH)]
    nmix = [norm_mix_w[l].reshape(1, D_MODEL) for l in range(DEPTH)]
    nffn = [norm_ffn_w[l].reshape(1, D_MODEL) for l in range(DEPTH)]
    nmls = [mlstm_norm_w[l].reshape(1, MLSTM_W) for l in range(DEPTH)]
    weights = [dict() for _ in range(DEPTH)]
    saved = [dict() for _ in range(DEPTH)]

    def layer_fwd(l, h, after):
        w, s = weights[l], saved[l]
        k0 = len(per_layer) * l
        tok, _ = gather_step(f"l{l}_a", after, last=k0)
        _, (g_in,) = gather_step(f"l{l}_b", tok, done=[k0])
        tok, _ = gather_step(f"l{l}_c", g_in, relay=k0 + 1, start=k0 + 3)
        w["win_t"], w["wc_t"] = _split_w_in(g_in)
        s["h0"] = h
        s["hn"], s["pm"] = _norm_proj(f"proj_mlstm_{l}", h, nmix[l] + tok[0, 0], w["win_t"], PM_W)
        tok, _ = gather_step(f"l{l}_d", s["pm"], relay=k0 + 2, start=k0 + 4)
        tok, _ = gather_step(f"l{l}_d2", tok, last=k0 + 1)
        s["pc"] = _mm_nt_bcols(f"proj_conv_{l}", s["hn"], w["wc_t"], F32, dep=tok)
        hm, s["ht"], s["cs"], s["ns"], s["ms"] = _mlstm_fwd(f"mlstm_fwd_{l}", s["pm"], bias[l] + tok[:1], nmls[l])
        tok, _ = gather_step(f"l{l}_e", hm, relay=k0 + 3, start=k0 + 5)
        tok, _ = gather_step(f"l{l}_e2", tok, last=k0 + 2)
        s["cat"] = _conv_fwd(f"conv_fwd_{l}", s["pc"], conv_rows[l] + tok[0, 0], hm)
        _, (g_out,) = gather_step(f"l{l}_f", s["cat"], done=[k0 + 1])
        w["wo"] = g_out.reshape(D_MODEL, D_MODEL)
        s["h1"], s["hf"] = _proj_res_norm(f"out_proj_{l}", s["cat"], w["wo"], s["h0"], nffn[l])
        tok_g, _ = gather_step(f"l{l}_g", s["h1"], relay=k0 + 4, start=k0 + 6)
        tok, _ = gather_step(f"l{l}_h", tok_g, last=k0 + 3)
        _, (g_gate, g_up) = gather_step(f"l{l}_i", tok, done=[k0 + 2, k0 + 3])
        w["wg_t"] = g_gate.reshape(D_FF, D_MODEL)
        w["wu_t"] = g_up.reshape(D_FF, D_MODEL)
        s["g"], s["u"], s["act"] = _ffn_in(f"ffn_in_{l}", s["hf"], w["wg_t"], w["wu_t"], dep=tok_g)
        tok, _ = gather_step(f"l{l}_j", s["act"], last=k0 + 4)
        _, (g_down,) = gather_step(f"l{l}_k", tok, done=[k0 + 4])
        w["wd"] = g_down.reshape(D_FF, D_MODEL)
        tok, _ = gather_step(f"l{l}_k2", tok, relay=k0 + 5, start=k0 + 7)
        return _mm_nn(f"ffn_out_{l}", s["act"], w["wd"], F32, res=s["h1"], dep=tok)

    tok, _ = gather_step("first", None, start=0)
    zero = tok[0, 0]
    small = jnp.concatenate(
        [meta_tokens + zero, jnp.pad(conv_w.reshape(DEPTH * 3, conv_sh), ((0, 2), (0, meta_sh - conv_sh)))], axis=0)
    slots = _exchange_small("gather_small", small, reduce=False)
    meta_full = jnp.transpose(slots[:, :N_META, :], (1, 0, 2)).reshape(N_META, D_MODEL)
    conv_full = jnp.transpose(slots[:, N_META:N_META + DEPTH * 3, :conv_sh], (1, 0, 2)).reshape(DEPTH, 3, CONV_W)
    conv_rows = [jnp.pad(conv_full[l], ((0, 5), (0, 0))) for l in range(DEPTH)]
    w_in_t, m_w_in_t, v_w_in_t = (jnp.transpose(a + zero, (0, 2, 1)) for a in (w_in, m_w_in, v_w_in))
    tok, w_in_t, m_w_in_t, v_w_in_t, meta_full = lax.optimization_barrier(
        (tok, w_in_t, m_w_in_t, v_w_in_t, meta_full))
    tok, _ = gather_step("pre_a", tok, relay=0)
    tok, _ = gather_step("pre_b", tok, start=1)
    tok, _ = gather_step("pre_c", tok, start=2)
    h = jnp.concatenate([jnp.zeros((PAD_FRONT, D_MODEL), F32), meta_full, x[0]], axis=0)
    h = layer_fwd(0, h, tok)
    h = layer_fwd(1, h, h)

    dh, dh_b, d_final, loss_part = _final_loss("final_loss", h, norm_final_w.reshape(1, D_MODEL), loss_target[0])

    core = lax.axis_index("c").astype(jnp.int32).reshape(1)
    chip = (2 * lax.axis_index("x") + lax.axis_index("y")).astype(jnp.int32).reshape(1)
    scatter_state = {}

    def scatter_begin(nm, grad):
        land = lax.empty((4,) + grad.shape[1:], BF16)
        arrs, sems, tok = _split_copy(f"grad_sibling_start_{nm}", [grad, land], start=(_scatter_sibling(1), 4))
        scatter_state[nm] = (arrs, sems)
        return tok

    def scatter_advance(nm, after):
        arrs, sems = scatter_state[nm]
        arrs, _, _ = _split_copy(f"grad_sibling_done_{nm}", arrs, wait=(_scatter_sibling(1), sems), after=after)
        part = _pair_sum(f"grad_pair_sum_{nm}", core, arrs[0], arrs[1])
        arrs, sems, tok = _split_copy(f"grad_chips_start_{nm}", [part, lax.empty(part.shape, BF16)],
                                      start=(_scatter_chips(1), 3))
        scatter_state[nm] = (arrs, sems)
        return tok

    def scattered(nm, after):
        arrs, sems = scatter_state[nm]
        arrs, _, _ = _split_copy(f"grad_chips_done_{nm}", arrs, wait=(_scatter_chips(1), sems), after=after)
        return arrs[0], arrs[1]

    d_mix, d_ffn, d_mls, d_bias, d_conv = ([None] * DEPTH for _ in range(5))

    def layer_bwd(l, dh, dh_b, tok):
        w, s = weights[l], saved[l]
        dg, du = _ffn_act_bwd(f"d_act_{l}", dh_b, w["wd"], s["g"], s["u"], dep=tok)
        dw_down = _mm_tn(f"dw_down_{l}", s["act"], dh_b, BF16, tm=1408, tn=1024)
        tok = scatter_begin(f"w_down_{l}", dw_down.reshape(N_DEV, FF_SH, D_MODEL))
        dhf = _mm_nn(f"d_ffn_gate_{l}", dg, w["wg_t"], F32, dep=tok)
        tok = scatter_advance(f"w_down_{l}", after=dhf)
        dhf = _mm_nn(f"d_ffn_up_{l}", du, w["wu_t"], F32, res=dhf, dep=tok)
        dw_gate = _mm_tn(f"dw_gate_{l}", dg, s["hf"], BF16, tm=1408, tn=1024)
        tok = scatter_begin(f"w_gate_{l}", dw_gate.reshape(N_DEV, FF_SH, D_MODEL))
        dw_up = _mm_tn(f"dw_up_{l}", du, s["hf"], BF16, tm=1408, tn=1024, dep=tok)
        tok = scatter_begin(f"w_up_{l}", dw_up.reshape(N_DEV, FF_SH, D_MODEL))
        dh1, dh1_b, d_ffn[l] = _rms_bwd(f"norm_ffn_bwd_{l}", s["h1"], nffn[l] + tok[0, 0], dhf, dh)
        tok = scatter_advance(f"w_gate_{l}", after=dh1)
        dcat = _mm_nt(f"d_cat_{l}", dh1_b, w["wo"], F32, tk=D_MODEL, dep=tok)
        tok = scatter_advance(f"w_up_{l}", after=dcat)
        dw_out = _mm_tn(f"dw_out_{l}", s["cat"], dh1_b, BF16, tn=1024, dep=tok)
        tok = scatter_begin(f"w_out_{l}", dw_out.reshape(N_DEV, OUT_SH, D_MODEL))
        dpm, d_mls[l], d_bias[l] = _mlstm_bwd(f"mlstm_bwd_{l}", dcat, s["pm"], s["ht"], s["cs"], s["ns"],
                                               s["ms"], bias[l] + tok[:1], nmls[l])
        dpc, d_conv[l] = _conv_bwd(f"conv_bwd_{l}", dcat, s["pc"], conv_rows[l])
        tok = scatter_advance(f"w_out_{l}", after=dpc)
        dwm_t = _mm_tn(f"dw_mlstm_{l}", dpm, s["hn"], BF16, tm=640, tn=1024, dep=tok)
        dwc_t = _mm_tn_acols(f"dw_conv_{l}", dpc, s["hn"], BF16)
        tok = scatter_begin(f"w_in_{l}", _merge_dw_in(dwm_t, dwc_t))
        dhn = _mm_nn_two(f"d_norm_{l}", dpm, w["win_t"], dpc, w["wc_t"], dep=tok)
        tok = scatter_advance(f"w_in_{l}", after=dhn)
        dh, dh_b, d_mix[l] = _rms_bwd(f"norm_mix_bwd_{l}", s["h0"], nmix[l] + tok[0, 0], dhn, dh1)
        return dh, dh_b, tok

    dh, dh_b, tok = layer_bwd(1, dh, dh_b, None)
    dh, dh_b, tok_tail = layer_bwd(0, dh, dh_b, tok)

    pq = {}
    after = dh
    for l in reversed(range(DEPTH)):
        for nm in ("w_down", "w_gate", "w_up", "w_out", "w_in"):
            if (nm, l) != ("w_in", 0):
                pq[nm, l] = scattered(f"{nm}_{l}", after)
                after = pq[nm, l][0]
    untransposed = lambda outs: [jnp.transpose(o, (0, 2, 1)) for o in outs]
    g_out, d_out, nm_out, nv_out = _adam_sharded(
        "adam_w_out", chip, w_out, m_w_out, v_w_out, [pq["w_out", 0], pq["w_out", 1]])
    g_gate, d_gate, nm_gate, nv_gate = untransposed(_adam_sharded(
        "adam_w_gate", chip, w_gate_t, m_w_gate_t, v_w_gate_t, [pq["w_gate", 0], pq["w_gate", 1]]))
    g_up, d_up, nm_up, nv_up = untransposed(_adam_sharded(
        "adam_w_up", chip, w_up_t, m_w_up_t, v_w_up_t, [pq["w_up", 0], pq["w_up", 1]]))
    g_down, d_down, nm_down, nv_down = _adam_sharded(
        "adam_w_down", chip, w_down, m_w_down, v_w_down, [pq["w_down", 0], pq["w_down", 1]])
    w_in_1 = _adam_layer("adam_w_in_1", chip, 1, w_in_t, m_w_in_t, v_w_in_t, pq["w_in", 1])
    pq["w_in", 0] = scattered("w_in_0", nv_down[0, :8, :128] + w_in_1[3][1, :8, :128])
    g_in, d_in, nm_in, nv_in = untransposed(_adam_layer(
        "adam_w_in_0", chip, 0, w_in_t, m_w_in_t, v_w_in_t, pq["w_in", 0], other=w_in_1))

    bg = jnp.concatenate([d_bias[l][0, :2 * HEADS] for l in range(DEPTH)])
    red_in = jnp.concatenate([
        dh[PAD_FRONT:TOK0], d_mix[0], d_mix[1], d_ffn[0], d_ffn[1], d_final,
        jnp.concatenate([d_mls[0], d_mls[1]], axis=1),
        jnp.stack([d_conv[l][:3] for l in range(DEPTH)]).reshape(3, 2 * CONV_W),
        jnp.pad(bg, (0, D_MODEL - bg.shape[0])).reshape(1, D_MODEL),
        jnp.pad(loss_part[:, :1], ((0, 0), (0, D_MODEL - 1))),
        jnp.zeros((5, D_MODEL), F32) + tok_tail[0, 0]], axis=0)
    red = _exchange_small("reduce_small", red_in, reduce=True)
    loss = red[26, 0]
    g_meta = lax.dynamic_slice_in_dim(red[:N_META], me * meta_sh, meta_sh, axis=1)
    g_mix, g_ffn, g_final = red[16:18], red[18:20], red[20]
    g_mls = red[21].reshape(DEPTH, MLSTM_W)
    g_conv = lax.dynamic_slice_in_dim(red[22:25].reshape(DEPTH, 3, CONV_W), me * conv_sh, conv_sh, axis=2)
    g_bias = red[25, :DEPTH * 2 * HEADS].reshape(DEPTH, 2 * HEADS)

    small_w = [meta_tokens, norm_mix_w, b_gates, conv_w, mlstm_norm_w, norm_ffn_w, norm_final_w]
    small_m = [m_meta_tokens, m_norm_mix_w, m_b_gates, m_conv_w, m_mlstm_norm_w, m_norm_ffn_w, m_norm_final_w]
    small_v = [v_meta_tokens, v_norm_mix_w, v_b_gates, v_conv_w, v_mlstm_norm_w, v_norm_ffn_w, v_norm_final_w]
    small_g = [g_meta, g_mix, g_bias, g_conv, g_mls, g_ffn, g_final]
    shapes = [a.shape for a in small_w]
    packed = _adam_small("adam_small", _pack128(small_w), _pack128(small_m), _pack128(small_v), _pack128(small_g))
    (d_meta, d_nmix, d_bg, d_cw, d_nmls, d_nffn, d_nfin), (nm_meta, nm_nmix, nm_bg, nm_cw, nm_nmls, nm_nffn, nm_nfin), \
        (nv_meta, nv_nmix, nv_bg, nv_cw, nv_nmls, nv_nffn, nv_nfin) = (_unpack128(p, shapes) for p in packed)

    grad_x = dh[TOK0:].reshape(1, seq, D_MODEL)
    return (loss, grad_x,
            g_meta, g_mix, g_in, g_bias, g_conv, g_mls, g_out, g_ffn, g_gate, g_up, g_down, g_final,
            d_meta, d_nmix, d_in, d_bg, d_cw, d_nmls, d_out, d_nffn, d_gate, d_up, d_down, d_nfin,
            nm_meta, nm_nmix, nm_in, nm_bg, nm_cw, nm_nmls, nm_out, nm_nffn, nm_gate, nm_up, nm_down, nm_nfin,
            nv_meta, nv_nmix, nv_in, nv_bg, nv_cw, nv_nmls, nv_out, nv_nffn, nv_gate, nv_up, nv_down, nv_nfin)
```

```python
import numpy as np
import jax
import jax.numpy as jnp
from jax import lax
from jax.experimental import pallas as pl
from jax.experimental.pallas import tpu as pltpu

F32 = jnp.float32
BF16 = jnp.bfloat16
MESH = pl.DeviceIdType.MESH

D_MODEL = 2048
DEPTH = 2
N_META = 16
MLSTM_W = 1024
CONV_W = 1024
HEADS = 4
DV = 256
DQK = 128
QK_W = 512
CHUNK = 64
PAD_FRONT = 48
TOK0 = PAD_FRONT + N_META
D_FF = 5632
N_DEV = 8
FF_SH = D_FF // N_DEV
D_IN = 6152
IN_SH = D_IN // N_DEV
OUT_SH = D_MODEL // N_DEV
GATE_COL = 3072
PM_W = GATE_COL + 128
GATE_CAP = 15.0
EPS = 1e-6
QSCALE = DQK ** -0.5

ADAM_LR = 0.001
ADAM_B1 = 0.9
ADAM_B2 = 0.999
ADAM_EPS = 1e-08
ADAM_WD = 0.01
ADAM_STEP = 10

V7X_VMEM_LIMIT = 50 * 1024 * 1024
V7X_MXU_COLS = 256


def _params(**kw):
    return pltpu.CompilerParams(vmem_limit_bytes=V7X_VMEM_LIMIT, **kw)


def _tile(n, target, mult):
    best = None
    for t in range(mult, min(n, target) + 1, mult):
        if n % t == 0:
            best = t
    return best if best is not None else n


def _sigmoid(x):
    return 1.0 / (1.0 + jnp.exp(-x))


NN = ((1,), (0,))
NT = ((1,), (1,))
TN = ((0,), (0,))


def _matmul(name, a, b, out_shape, out_dtype, grid, a_bs, b_bs, o_bs, dims, nk, acc_shape=None,
            res=None, res_bs=None, dep=None):
    has_res = res is not None
    n_in = 2 + has_res + (dep is not None)

    def body(*refs):
        a_ref, b_ref = refs[0], refs[1]
        r_ref = refs[2] if has_res else None
        o_ref = refs[n_in]
        x = lax.dot_general(a_ref[...], b_ref[...], (dims, ((), ())), preferred_element_type=F32)
        if nk == 1:
            if has_res:
                x = x + r_ref[...]
            o_ref[...] = x.astype(o_ref.dtype)
            return
        acc = refs[n_in + 1]
        k = pl.program_id(len(grid) - 1)

        @pl.when(k == 0)
        def _():
            acc[...] = (x + r_ref[...]) if has_res else x

        @pl.when(k > 0)
        def _():
            acc[...] += x

        @pl.when(k == nk - 1)
        def _():
            o_ref[...] = acc[...].astype(o_ref.dtype)

    ins = [a, b] + ([res] if has_res else [])
    specs = [a_bs, b_bs] + ([res_bs] if has_res else [])
    if dep is not None:
        ins.append(dep)
        specs.append(pl.BlockSpec((8, 128), lambda *_: (0, 0)))
    scratch = [pltpu.VMEM(acc_shape, F32)] if nk > 1 else []
    return pl.pallas_call(
        body, name=name, grid=grid, in_specs=specs, out_specs=o_bs,
        out_shape=jax.ShapeDtypeStruct(out_shape, out_dtype), scratch_shapes=scratch,
        compiler_params=_params(),
    )(*ins)


def _mm_nn(name, a, b, out_dtype, res=None, tm=1056, tn=512, dep=None):
    r, k = a.shape
    n = b.shape[1]
    tm, tn = _tile(r, tm, 8), _tile(n, tn, 128)
    return _matmul(name, a, b, (r, n), out_dtype, (r // tm, n // tn, 1),
                   pl.BlockSpec((tm, k), lambda i, j, s: (i, 0)),
                   pl.BlockSpec((k, tn), lambda i, j, s: (0, j)),
                   pl.BlockSpec((tm, tn), lambda i, j, s: (i, j)), NN, 1,
                   res=res, res_bs=pl.BlockSpec((tm, tn), lambda i, j, s: (i, j)), dep=dep)


def _mm_nn_two(name, a, b, a3, b3, dep=None, tm=1056, tn=512):
    r, k = a.shape
    e, _, kb = a3.shape
    n = b.shape[1]
    tm, tn = _tile(r, tm, 8), _tile(n, tn, 128)

    def body(a_ref, b_ref, a3_ref, b3_ref, *rest):
        acc = lax.dot_general(a_ref[...], b_ref[...], (NN, ((), ())), preferred_element_type=F32)
        for s in range(e):
            acc = acc + lax.dot_general(a3_ref[s], b3_ref[s], (NN, ((), ())), preferred_element_type=F32)
        rest[-1][...] = acc

    ins = [a, b, a3, b3]
    specs = [pl.BlockSpec((tm, k), lambda i, j: (i, 0)), pl.BlockSpec((k, tn), lambda i, j: (0, j)),
             pl.BlockSpec((e, tm, kb), lambda i, j: (0, i, 0)), pl.BlockSpec((e, kb, tn), lambda i, j: (0, 0, j))]
    if dep is not None:
        ins.append(dep)
        specs.append(pl.BlockSpec((8, 128), lambda *_: (0, 0)))
    return pl.pallas_call(
        body, name=name, grid=(r // tm, n // tn), in_specs=specs,
        out_specs=pl.BlockSpec((tm, tn), lambda i, j: (i, j)),
        out_shape=jax.ShapeDtypeStruct((r, n), F32), compiler_params=_params(),
    )(*ins)


def _mm_nt(name, a, b, out_dtype, res=None, tm=1056, tn=512, tk=640, n=None, dep=None):
    r, k = a.shape
    n = b.shape[0] if n is None else n
    tm, tn, tk = _tile(r, tm, 8), _tile(n, tn, 128), _tile(k, tk, 128)
    nk = k // tk
    return _matmul(name, a, b, (r, n), out_dtype, (r // tm, n // tn, nk),
                   pl.BlockSpec((tm, tk), lambda i, j, s: (i, s)),
                   pl.BlockSpec((tn, tk), lambda i, j, s: (j, s)),
                   pl.BlockSpec((tm, tn), lambda i, j, s: (i, j)), NT, nk, acc_shape=(tm, tn),
                   res=res, res_bs=pl.BlockSpec((tm, tn), lambda i, j, s: (i, j)), dep=dep)


def _mm_nt_bcols(name, a, b3, out_dtype, tm=1056, dep=None):
    r, k = a.shape
    e, n, _ = b3.shape
    tm = _tile(r, tm, 8)
    return _matmul(name, a, b3, (e, r, n), out_dtype, (r // tm, e, 1),
                   pl.BlockSpec((tm, k), lambda i, g, s: (i, 0)),
                   pl.BlockSpec((None, n, k), lambda i, g, s: (g, 0, 0)),
                   pl.BlockSpec((None, tm, n), lambda i, g, s: (g, i, 0)), NT, 1, dep=dep)


def _mm_tn(name, a, b, out_dtype, tm=1024, tn=640, dep=None):
    r, m = a.shape
    n = b.shape[1]
    tm, tn = _tile(m, tm, 128), _tile(n, tn, 128)
    return _matmul(name, a, b, (m, n), out_dtype, (m // tm, n // tn, 1),
                   pl.BlockSpec((r, tm), lambda i, j, s: (0, i)),
                   pl.BlockSpec((r, tn), lambda i, j, s: (0, j)),
                   pl.BlockSpec((tm, tn), lambda i, j, s: (i, j)), TN, 1, dep=dep)


def _mm_tn_acols(name, a3, b, out_dtype, tn=1024, dep=None):
    e, r, m = a3.shape
    n = b.shape[1]
    tn = _tile(n, tn, 128)
    return _matmul(name, a3, b, (e, m, n), out_dtype, (n // tn, e, 1),
                   pl.BlockSpec((None, r, m), lambda j, g, s: (g, 0, 0)),
                   pl.BlockSpec((r, tn), lambda j, g, s: (0, j)),
                   pl.BlockSpec((None, m, tn), lambda j, g, s: (g, 0, j)), TN, 1, dep=dep)


def _norm_proj(name, h, w, b, n, tm=1056, tn=640):
    r, d = h.shape
    tm, tn = _tile(r, tm, 8), _tile(n, tn, 128)

    def body(h_ref, w_ref, b_ref, hn_ref, o_ref):
        @pl.when(pl.program_id(1) == 0)
        def _():
            x = h_ref[...]
            rs = lax.rsqrt(jnp.mean(x * x, axis=1, keepdims=True) + EPS)
            hn_ref[...] = (x * rs * w_ref[...]).astype(BF16)

        o_ref[...] = lax.dot_general(hn_ref[...], b_ref[...], (NT, ((), ())), preferred_element_type=F32)

    row = pl.BlockSpec((tm, d), lambda i, j: (i, 0))
    return pl.pallas_call(
        body, name=name, grid=(r // tm, n // tn),
        in_specs=[row, pl.BlockSpec((1, d), lambda i, j: (0, 0)), pl.BlockSpec((tn, d), lambda i, j: (j, 0))],
        out_specs=[row, pl.BlockSpec((tm, tn), lambda i, j: (i, j))],
        out_shape=[jax.ShapeDtypeStruct((r, d), BF16), jax.ShapeDtypeStruct((r, n), F32)],
        compiler_params=_params(),
    )(h, w, b)


def _proj_res_norm(name, a, b, res, w, tm=528):
    r, k = a.shape
    d = b.shape[1]
    tm = _tile(r, tm, 8)

    def body(a_ref, b_ref, r_ref, w_ref, y_ref, n_ref):
        y = lax.dot_general(a_ref[...], b_ref[...], (NN, ((), ())), preferred_element_type=F32) + r_ref[...]
        y_ref[...] = y
        rs = lax.rsqrt(jnp.mean(y * y, axis=1, keepdims=True) + EPS)
        n_ref[...] = (y * rs * w_ref[...]).astype(BF16)

    row = pl.BlockSpec((tm, d), lambda i: (i, 0))
    return pl.pallas_call(
        body, name=name, grid=(r // tm,),
        in_specs=[pl.BlockSpec((tm, k), lambda i: (i, 0)), pl.BlockSpec((k, d), lambda i: (0, 0)), row,
                  pl.BlockSpec((1, d), lambda i: (0, 0))],
        out_specs=[row, row],
        out_shape=[jax.ShapeDtypeStruct((r, d), F32), jax.ShapeDtypeStruct((r, d), BF16)],
        compiler_params=_params(),
    )(a, b, res, w)


def _rms_bwd(name, x, w, dy, dres):
    r, d = x.shape
    tr = _tile(r, 264, 8)

    def body(x_ref, w_ref, dy_ref, dr_ref, dx_ref, dxb_ref, dw_ref):
        xv = x_ref[...]
        g = dy_ref[...]
        rs = lax.rsqrt(jnp.mean(xv * xv, axis=1, keepdims=True) + EPS)
        wg = g * w_ref[...]
        dx = rs * wg - xv * (rs * rs * rs) * jnp.mean(xv * wg, axis=1, keepdims=True) + dr_ref[...]
        dx_ref[...] = dx
        dxb_ref[...] = dx.astype(BF16)
        part = jnp.sum(g * xv * rs, axis=0, keepdims=True)

        @pl.when(pl.program_id(0) == 0)
        def _():
            dw_ref[...] = part

        @pl.when(pl.program_id(0) > 0)
        def _():
            dw_ref[...] += part

    row = pl.BlockSpec((tr, d), lambda i: (i, 0))
    vec = pl.BlockSpec((1, d), lambda i: (0, 0))
    return pl.pallas_call(
        body, name=name, grid=(r // tr,), in_specs=[row, vec, row, row], out_specs=[row, row, vec],
        out_shape=[jax.ShapeDtypeStruct((r, d), F32), jax.ShapeDtypeStruct((r, d), BF16),
                   jax.ShapeDtypeStruct((1, d), F32)],
        compiler_params=_params(),
    )(x, w, dy, dres)


def _final_loss(name, h, w, target):
    r, d = h.shape
    tr = _tile(r, 264, 8)

    def body(h_ref, w_ref, t_ref, dh_ref, dhb_ref, dw_ref, ls_ref):
        i = pl.program_id(0)

        @pl.when(i == 0)
        def _():
            dw_ref[...] = jnp.zeros_like(dw_ref)
            ls_ref[...] = jnp.zeros_like(ls_ref)

        xv = h_ref[...]
        wv = w_ref[...]
        rs = lax.rsqrt(jnp.mean(xv * xv, axis=1, keepdims=True) + EPS)
        counted = lax.broadcasted_iota(jnp.int32, (tr, 1), 0) + i * tr >= TOK0
        err = jnp.where(counted, xv * rs * wv - t_ref[...], 0.0)
        sq = jnp.sum(jnp.sum(err * err, axis=1, keepdims=True), axis=0, keepdims=True)
        ls_ref[...] += jnp.broadcast_to(sq * (0.5 / d), ls_ref.shape)
        g = err * (1.0 / d)
        wg = g * wv
        dx = rs * wg - xv * (rs * rs * rs) * jnp.mean(xv * wg, axis=1, keepdims=True)
        dh_ref[...] = dx
        dhb_ref[...] = dx.astype(BF16)
        dw_ref[...] += jnp.sum(g * xv * rs, axis=0, keepdims=True)

    row = pl.BlockSpec((tr, d), lambda i: (i, 0))
    vec = pl.BlockSpec((1, d), lambda i: (0, 0))
    return pl.pallas_call(
        body, name=name, grid=(r // tr,),
        in_specs=[row, vec, row],
        out_specs=[row, row, vec, pl.BlockSpec((1, 128), lambda i: (0, 0))],
        out_shape=[jax.ShapeDtypeStruct((r, d), F32), jax.ShapeDtypeStruct((r, d), BF16),
                   jax.ShapeDtypeStruct((1, d), F32), jax.ShapeDtypeStruct((1, 128), F32)],
        compiler_params=_params(),
    )(h, w, target)


def _ffn_in(name, hf, wg_t, wu_t, dep=None, tm=1056, tn=512):
    r, d = hf.shape
    f = wg_t.shape[0]
    tm, tn = _tile(r, tm, 8), _tile(f, tn, 128)

    def body(h_ref, wg_ref, wu_ref, *rest):
        g_ref, u_ref, a_ref = rest[-3:]
        x = h_ref[...]
        g = lax.dot_general(x, wg_ref[...], (NT, ((), ())), preferred_element_type=F32)
        u = lax.dot_general(x, wu_ref[...], (NT, ((), ())), preferred_element_type=F32)
        g_ref[...] = g.astype(BF16)
        u_ref[...] = u.astype(BF16)
        a_ref[...] = (g * _sigmoid(g) * u).astype(BF16)

    wspec = pl.BlockSpec((tn, d), lambda i, j: (j, 0))
    ospec = pl.BlockSpec((tm, tn), lambda i, j: (i, j))
    ins, specs = [hf, wg_t, wu_t], [pl.BlockSpec((tm, d), lambda i, j: (i, 0)), wspec, wspec]
    if dep is not None:
        ins.append(dep)
        specs.append(pl.BlockSpec((8, 128), lambda *_: (0, 0)))
    return pl.pallas_call(
        body, name=name, grid=(r // tm, f // tn), in_specs=specs, out_specs=[ospec] * 3,
        out_shape=[jax.ShapeDtypeStruct((r, f), BF16)] * 3, compiler_params=_params(),
    )(*ins)


def _ffn_act_bwd(name, dh, wd, g, u, dep=None, tm=1056, tn=512):
    r, d = dh.shape
    f = wd.shape[0]
    tm, tn = _tile(r, tm, 8), _tile(f, tn, 128)

    def body(dh_ref, wd_ref, g_ref, u_ref, *rest):
        dg_ref, du_ref = rest[-2:]
        tr = _tile(tm, 264, 8)
        for r0 in range(0, tm, tr):
            for c0 in range(0, tn, V7X_MXU_COLS):
                rows, cols = slice(r0, r0 + tr), slice(c0, c0 + V7X_MXU_COLS)
                da = lax.dot_general(dh_ref[rows, :], wd_ref[cols, :], (NT, ((), ())), preferred_element_type=F32)
                gv = g_ref[rows, cols]
                s = _sigmoid(gv)
                t = da.astype(BF16) * s
                du_ref[rows, cols] = t * gv
                dg_ref[rows, cols] = t * u_ref[rows, cols] * (1.0 + gv - gv * s)

    tile = pl.BlockSpec((tm, tn), lambda i, j: (i, j))
    ins = [dh, wd, g, u]
    specs = [pl.BlockSpec((tm, d), lambda i, j: (i, 0)), pl.BlockSpec((tn, d), lambda i, j: (j, 0)), tile, tile]
    if dep is not None:
        ins.append(dep)
        specs.append(pl.BlockSpec((8, 128), lambda *_: (0, 0)))
    return pl.pallas_call(
        body, name=name, grid=(r // tm, f // tn), in_specs=specs, out_specs=[tile] * 2,
        out_shape=[jax.ShapeDtypeStruct((r, f), BF16)] * 2, compiler_params=_params(),
    )(*ins)


def _shift_down(a, k):
    row = lax.broadcasted_iota(jnp.int32, a.shape, 0)
    return jnp.where(row >= k, pltpu.roll(a, k, 0), 0.0)


def _shift_up(a, k):
    n = a.shape[0]
    row = lax.broadcasted_iota(jnp.int32, a.shape, 0)
    return jnp.where(row < n - k, pltpu.roll(a, n - k, 0), 0.0)


def _conv_fwd(name, pc, cw, cat):
    _, r, w = pc.shape
    nblk = w // 128

    def body(pc_ref, cw_ref, cat_ref, o_ref):
        a = pc_ref[2] * pc_ref[0]
        cwv = cw_ref[...]
        conv = _shift_down(a, 2) * cwv[0:1] + _shift_down(a, 1) * cwv[1:2] + a * cwv[2:3]
        o_ref[...] = (pc_ref[1] * conv).astype(BF16)

    return pl.pallas_call(
        body, name=name, grid=(nblk,),
        in_specs=[pl.BlockSpec((3, r, 128), lambda j: (0, 0, j)), pl.BlockSpec((8, 128), lambda j: (0, j)),
                  pl.BlockSpec(memory_space=pl.ANY)],
        out_specs=pl.BlockSpec((r, 128), lambda j: (0, nblk + j)),
        out_shape=jax.ShapeDtypeStruct(cat.shape, BF16), input_output_aliases={2: 0},
        compiler_params=_params(),
    )(pc, cw, cat)


def _conv_bwd(name, dcat, pc, cw):
    _, r, w = pc.shape
    nblk = w // 128

    def body(dy_ref, pc_ref, cw_ref, dpc_ref, dcw_ref):
        u, gb, gc = pc_ref[0], pc_ref[1], pc_ref[2]
        cwv = cw_ref[...]
        dy = dy_ref[...]
        a = gc * u
        a1, a2 = _shift_down(a, 1), _shift_down(a, 2)
        conv = a2 * cwv[0:1] + a1 * cwv[1:2] + a * cwv[2:3]
        dconv = dy * gb
        da = dconv * cwv[2:3] + _shift_up(dconv, 1) * cwv[1:2] + _shift_up(dconv, 2) * cwv[0:1]
        dpc_ref[0] = (da * gc).astype(BF16)
        dpc_ref[1] = (dy * conv).astype(BF16)
        dpc_ref[2] = (da * u).astype(BF16)
        row = lax.broadcasted_iota(jnp.int32, (8, 128), 0)
        dw0 = jnp.sum(dconv * a2, axis=0, keepdims=True)
        dw1 = jnp.sum(dconv * a1, axis=0, keepdims=True)
        dw2 = jnp.sum(dconv * a, axis=0, keepdims=True)
        dcw_ref[...] = jnp.where(row == 0, dw0, jnp.where(row == 1, dw1, jnp.where(row == 2, dw2, 0.0)))

    return pl.pallas_call(
        body, name=name, grid=(nblk,),
        in_specs=[pl.BlockSpec((r, 128), lambda j: (0, nblk + j)),
                  pl.BlockSpec((3, r, 128), lambda j: (0, 0, j)), pl.BlockSpec((8, 128), lambda j: (0, j))],
        out_specs=[pl.BlockSpec((3, r, 128), lambda j: (0, 0, j)), pl.BlockSpec((8, 128), lambda j: (0, j))],
        out_shape=[jax.ShapeDtypeStruct((3, r, w), BF16), jax.ShapeDtypeStruct((8, w), F32)],
        compiler_params=_params(),
    )(dcat, pc, cw)


def _dot(a, b, dims):
    return lax.dot_general(a, b, (dims, ((), ())), preferred_element_type=F32)


def _col_to_row(xc, eye):
    return jnp.sum(jnp.where(eye, xc, 0.0), axis=0, keepdims=True)


def _row_to_col(xr, eye):
    return jnp.sum(jnp.where(eye, xr, 0.0), axis=1, keepdims=True)


def _gate_tiles(graw, bias, row0):
    th = jnp.tanh((graw + bias) / GATE_CAP)
    z = GATE_CAP * th
    row = lax.broadcasted_iota(jnp.int32, graw.shape, 0) + row0
    real = row >= PAD_FRONT
    li = jnp.where(real, z, -jnp.inf)
    lf = jnp.where(real, jnp.minimum(z, 0.0) - jnp.log(1.0 + jnp.exp(-jnp.abs(z))), 0.0)
    return th, z, li, lf, real


def _interleave(gens):
    results = [None] * len(gens)
    live = list(enumerate(gens))
    while live:
        still = []
        for i, gen in live:
            try:
                next(gen)
                still.append((i, gen))
            except StopIteration as stop:
                results[i] = stop.value
        live = still
    return results


def _chunk_common(pm, h, li, lf, cst, nst, mst, tril, eye):
    kraw = pm[:, QK_W + h * DQK:QK_W + (h + 1) * DQK]
    q = (pm[:, h * DQK:(h + 1) * DQK] * QSCALE).astype(BF16)
    yield
    k = kraw.astype(BF16)
    v = pm[:, 2 * QK_W + h * DV:2 * QK_W + (h + 1) * DV].astype(BF16)
    yield
    li_c = li[:, h:h + 1]
    lf_c = lf[:, HEADS + h:HEADS + h + 1]
    li_r = _col_to_row(li_c, eye)
    yield
    lf_r = _col_to_row(lf_c, eye)
    yield
    b_c = jnp.sum(jnp.where(tril, lf_r, 0.0), axis=1, keepdims=True)
    yield
    b_r = _col_to_row(b_c, eye)
    yield
    dmat = jnp.where(tril, b_c - b_r + li_r, -jnp.inf)
    inter = b_c + mst
    yield
    mt = jnp.maximum(inter, jnp.max(dmat, axis=1, keepdims=True))
    yield
    w_inter = jnp.exp(inter - mt)
    p = jnp.exp(dmat - mt)
    yield
    s = _dot(q, k, NT) * p
    yield
    cb = cst.astype(BF16)
    nb = nst.astype(BF16).astype(F32)
    qc = _dot(q, cb, NN)
    yield
    qn = jnp.sum(q.astype(F32) * nb, axis=1, keepdims=True)
    yield
    den = w_inter * qn + jnp.sum(s, axis=1, keepdims=True)
    yield
    dn = jnp.maximum(jnp.abs(den), jnp.exp(-mt))
    b_end = b_c[CHUNK - 1:CHUNK, :]
    decay = b_end - b_c + li_c
    yield
    m_new = jnp.maximum(b_end + mst, jnp.max(decay, axis=0, keepdims=True))
    yield
    w_old = jnp.exp(b_end + mst - m_new)
    w_in = jnp.exp(decay - m_new)
    kw = (w_in * kraw).astype(BF16)
    yield
    return dict(q=q, k=k, v=v, kraw=kraw, mt=mt, w_inter=w_inter, p=p, s=s, cb=cb, nb=nb, qc=qc, qn=qn,
                den=den, dn=dn, m_new=m_new, w_old=w_old, w_in=w_in, kw=kw)


def _chunks_per_step(nc, want):
    return want if nc % want == 0 else 1


def _mlstm_fwd(name, pm, bias, nw):
    r = pm.shape[0]
    nc = r // CHUNK
    grp = _chunks_per_step(nc, 3)

    def body(pm_ref, b_ref, nw_ref, hm_ref, ht_ref, cs_ref, ns_ref, ms_ref, c_scr, n_scr, m_scr):
        step = pl.program_id(0)

        @pl.when(step == 0)
        def _():
            c_scr[...] = jnp.zeros_like(c_scr)
            n_scr[...] = jnp.zeros_like(n_scr)
            m_scr[...] = jnp.zeros_like(m_scr)

        rr = lax.broadcasted_iota(jnp.int32, (CHUNK, CHUNK), 0)
        cc = lax.broadcasted_iota(jnp.int32, (CHUNK, CHUNK), 1)
        tril, eye = cc <= rr, cc == rr
        bv, nwv = b_ref[...], nw_ref[...]
        states = [(c_scr[h], n_scr[h], m_scr[h]) for h in range(HEADS)]
        for g in range(grp):
            rows = slice(g * CHUNK, (g + 1) * CHUNK)
            pmv = pm_ref[rows, :]
            _, _, li, lf, _ = _gate_tiles(pmv[:, GATE_COL:GATE_COL + 128], bv, (step * grp + g) * CHUNK)
            def head(h, cst, nst, mst, g=g, rows=rows, pmv=pmv, li=li, lf=lf):
                f = yield from _chunk_common(pmv, h, li, lf, cst, nst, mst, tril, eye)
                num = f["w_inter"] * f["qc"] + _dot(f["s"].astype(BF16), f["v"], NN)
                yield
                hh = num / f["dn"]
                yield
                c_new = f["w_old"] * cst + _dot(f["kw"], f["v"], TN)
                yield
                n_new = f["w_old"] * nst + jnp.sum(
                    f["w_in"].astype(BF16).astype(F32) * f["k"].astype(F32), axis=0, keepdims=True)
                yield
                sl = slice(h * DV, (h + 1) * DV)
                rs = lax.rsqrt(jnp.mean(hh * hh, axis=1, keepdims=True) + EPS)
                yield
                og = pmv[:, 2 * QK_W + MLSTM_W + h * DV:2 * QK_W + MLSTM_W + (h + 1) * DV]
                cs_ref[g, h] = cst
                ns_ref[g, h] = nst
                ms_ref[g, h] = mst
                ht_ref[rows, sl] = hh
                yield
                hm_ref[rows, sl] = (_sigmoid(og) * (hh * rs * nwv[:, sl])).astype(BF16)
                return c_new, n_new, f["m_new"]

            states = _interleave([head(h, *states[h]) for h in range(HEADS)])
        for h, (cst, nst, mst) in enumerate(states):
            c_scr[h] = cst
            n_scr[h] = nst
            m_scr[h] = mst

    return pl.pallas_call(
        body, name=name, grid=(nc // grp,),
        in_specs=[pl.BlockSpec((grp * CHUNK, PM_W), lambda i: (i, 0)), pl.BlockSpec((1, 128), lambda i: (0, 0)),
                  pl.BlockSpec((1, MLSTM_W), lambda i: (0, 0))],
        out_specs=[pl.BlockSpec((grp * CHUNK, MLSTM_W), lambda i: (i, 0)),
                   pl.BlockSpec((grp * CHUNK, MLSTM_W), lambda i: (i, 0)),
                   pl.BlockSpec((grp, HEADS, DQK, DV), lambda i: (i, 0, 0, 0)),
                   pl.BlockSpec((grp, HEADS, 1, DQK), lambda i: (i, 0, 0, 0)),
                   pl.BlockSpec((grp, HEADS, 1, 1), lambda i: (i, 0, 0, 0))],
        out_shape=[jax.ShapeDtypeStruct((r, MLSTM_W + CONV_W), BF16), jax.ShapeDtypeStruct((r, MLSTM_W), F32),
                   jax.ShapeDtypeStruct((nc, HEADS, DQK, DV), F32),
                   jax.ShapeDtypeStruct((nc, HEADS, 1, DQK), F32),
                   jax.ShapeDtypeStruct((nc, HEADS, 1, 1), F32)],
        scratch_shapes=[pltpu.VMEM((HEADS, DQK, DV), F32), pltpu.VMEM((HEADS, 1, DQK), F32),
                        pltpu.VMEM((HEADS, 1, 1), F32)],
        compiler_params=_params(),
    )(pm, bias, nw)


def _mlstm_bwd(name, dcat, pm, ht, cs, ns, ms, bias, nw):
    r = pm.shape[0]
    nc = r // CHUNK
    grp = _chunks_per_step(nc, 1)
    nsteps = nc // grp

    def body(dy_ref, pm_ref, ht_ref, cs_ref, ns_ref, ms_ref, b_ref, nw_ref, dpm_ref, dnw_ref, db_ref,
             dc_scr, dn_scr):
        step = pl.program_id(0)

        @pl.when(step == 0)
        def _():
            dc_scr[...] = jnp.zeros_like(dc_scr)
            dn_scr[...] = jnp.zeros_like(dn_scr)
            dnw_ref[...] = jnp.zeros_like(dnw_ref)
            db_ref[...] = jnp.zeros_like(db_ref)

        rr = lax.broadcasted_iota(jnp.int32, (CHUNK, CHUNK), 0)
        cc = lax.broadcasted_iota(jnp.int32, (CHUNK, CHUNK), 1)
        tril, eye, triu = cc <= rr, cc == rr, cc >= rr
        lane = lax.broadcasted_iota(jnp.int32, (CHUNK, 128), 1)
        rowid = lax.broadcasted_iota(jnp.int32, (CHUNK, 1), 0)
        bv, nwv = b_ref[...], nw_ref[...]
        carried = [(dc_scr[h], dn_scr[h]) for h in range(HEADS)]
        dnw_acc = [jnp.zeros((1, DV), F32) for _ in range(HEADS)]
        db_acc = jnp.zeros((1, 128), F32)
        for g in reversed(range(grp)):
            rows = slice(g * CHUNK, (g + 1) * CHUNK)
            ci = (nsteps - 1 - step) * grp + g
            pmv = pm_ref[rows, :]
            th, z, li, lf, real = _gate_tiles(pmv[:, GATE_COL:GATE_COL + 128], bv, ci * CHUNK)
            heads = _interleave([
                _mlstm_bwd_head(h, pmv, ht_ref[rows, h * DV:(h + 1) * DV], dy_ref[rows, h * DV:(h + 1) * DV], nwv,
                                li, lf, cs_ref[g, h], ns_ref[g, h], ms_ref[g, h], carried[h][0], carried[h][1],
                                tril, eye, triu, lane, rowid, dpm_ref, rows)
                for h in range(HEADS)])
            carried = [(dc_new, dn_new) for _, dc_new, dn_new, _ in heads]
            dgt = heads[0][0] + heads[1][0] + heads[2][0] + heads[3][0]
            dnw_acc = [dnw_acc[h] + heads[h][3] for h in range(HEADS)]
            dact = jnp.where(lane < HEADS, 1.0, 1.0 - _sigmoid(z)) * (1.0 - th * th)
            dgraw = jnp.where(real & (lane < 2 * HEADS), dgt * dact, 0.0)
            dpm_ref[rows, GATE_COL:GATE_COL + 128] = dgraw.astype(BF16)
            db_acc = db_acc + jnp.sum(dgraw, axis=0, keepdims=True)
        for h, (dcn, dnn) in enumerate(carried):
            dc_scr[h] = dcn
            dn_scr[h] = dnn
            dnw_ref[:, h * DV:(h + 1) * DV] += dnw_acc[h]
        db_ref[...] += db_acc

    rev = lambda i: (nsteps - 1 - i, 0)
    rev4 = lambda i: (nsteps - 1 - i, 0, 0, 0)
    return pl.pallas_call(
        body, name=name, grid=(nsteps,),
        in_specs=[pl.BlockSpec((grp * CHUNK, MLSTM_W), rev), pl.BlockSpec((grp * CHUNK, PM_W), rev),
                  pl.BlockSpec((grp * CHUNK, MLSTM_W), rev),
                  pl.BlockSpec((grp, HEADS, DQK, DV), rev4), pl.BlockSpec((grp, HEADS, 1, DQK), rev4),
                  pl.BlockSpec((grp, HEADS, 1, 1), rev4),
                  pl.BlockSpec((1, 128), lambda i: (0, 0)), pl.BlockSpec((1, MLSTM_W), lambda i: (0, 0))],
        out_specs=[pl.BlockSpec((grp * CHUNK, PM_W), rev), pl.BlockSpec((1, MLSTM_W), lambda i: (0, 0)),
                   pl.BlockSpec((1, 128), lambda i: (0, 0))],
        out_shape=[jax.ShapeDtypeStruct((r, PM_W), BF16), jax.ShapeDtypeStruct((1, MLSTM_W), F32),
                   jax.ShapeDtypeStruct((1, 128), F32)],
        scratch_shapes=[pltpu.VMEM((HEADS, DQK, DV), F32), pltpu.VMEM((HEADS, 1, DQK), F32)],
        compiler_params=_params(),
    )(dcat, pm, ht, cs, ns, ms, bias, nw)


def _mlstm_bwd_head(h, pmv, hh, y, nwv, li, lf, cst, nst, mst, dcn, dnn, tril, eye, triu, lane, rowid,
                    dpm_ref, rows):
    f = yield from _chunk_common(pmv, h, li, lf, cst, nst, mst, tril, eye)
    q, k, v, s, p = f["q"], f["k"], f["v"], f["s"], f["p"]
    w_inter, w_in, w_old, dn = f["w_inter"], f["w_in"], f["w_old"], f["dn"]
    osl = slice(2 * QK_W + MLSTM_W + h * DV, 2 * QK_W + MLSTM_W + (h + 1) * DV)
    sg = _sigmoid(pmv[:, osl])
    yield
    rs = lax.rsqrt(jnp.mean(hh * hh, axis=1, keepdims=True) + EPS)
    yield
    nwh = nwv[:, h * DV:(h + 1) * DV]
    dpm_ref[rows, osl] = (y * (hh * rs * nwh) * sg * (1.0 - sg)).astype(BF16)
    yield
    dhn = y * sg
    dnw_h = jnp.sum(dhn * hh * rs, axis=0, keepdims=True)
    yield
    wd = dhn * nwh
    dhh = rs * wd - hh * (rs * rs * rs) * jnp.mean(hh * wd, axis=1, keepdims=True)
    yield
    dnum = dhh / dn
    dd = -jnp.sum(dhh * hh, axis=1, keepdims=True) / dn
    yield
    dden = jnp.where(jnp.abs(f["den"]) > jnp.exp(-f["mt"]), dd * jnp.sign(f["den"]), 0.0)
    dnum_b = dnum.astype(BF16)
    wdn = (w_inter * dnum).astype(BF16)
    wid = (w_inter * dden).astype(BF16).astype(F32)
    yield
    ds = _dot(dnum_b, v, NT) + dden
    yield
    dsp = (ds * p).astype(BF16)
    yield
    dq = _dot(dsp, k, NN) + _dot(wdn, f["cb"], NT) + wid * f["nb"]
    yield
    dk = _dot(dsp, q, TN)
    yield
    dv = _dot(s.astype(BF16), dnum_b, TN)
    yield
    g = ds * s
    g_col = _row_to_col(jnp.sum(g, axis=0, keepdims=True), eye)
    yield
    db = jnp.sum(g, axis=1, keepdims=True) - g_col
    dli = g_col
    yield
    db = db + (jnp.sum(dnum * f["qc"], axis=1, keepdims=True) + dden * f["qn"]) * w_inter
    yield
    dcnb = dcn.astype(BF16)
    dnnb = dnn.astype(BF16).astype(F32)
    dkw = _dot(v, dcnb, NT) + dnnb
    yield
    dk = dk + w_in * dkw
    dv = dv + _dot(f["kw"], dcnb, NN)
    yield
    ddecay = jnp.sum(dkw * f["kraw"], axis=1, keepdims=True) * w_in
    yield
    dw_old = (jnp.sum(jnp.sum(dcn * cst, axis=1, keepdims=True), axis=0, keepdims=True)
              + jnp.sum(dnn * nst, axis=1, keepdims=True))
    yield
    db_end = dw_old * w_old + jnp.sum(ddecay, axis=0, keepdims=True)
    db = db - ddecay + jnp.where(rowid == CHUNK - 1, db_end, 0.0)
    dli = dli + ddecay
    yield
    dc_new = w_old * dcn + _dot(q, wdn, TN)
    yield
    dn_new = w_old * dnn + jnp.sum(wid * q.astype(F32), axis=0, keepdims=True)
    yield
    dlf = jnp.sum(jnp.where(triu, _col_to_row(db, eye), 0.0), axis=1, keepdims=True)
    yield
    gate_part = jnp.where(lane == h, dli, 0.0) + jnp.where(lane == HEADS + h, dlf, 0.0)
    dpm_ref[rows, h * DQK:(h + 1) * DQK] = (dq * QSCALE).astype(BF16)
    yield
    dpm_ref[rows, QK_W + h * DQK:QK_W + (h + 1) * DQK] = dk.astype(BF16)
    yield
    dpm_ref[rows, 2 * QK_W + h * DV:2 * QK_W + (h + 1) * DV] = dv.astype(BF16)
    return gate_part, dc_new, dn_new, dnw_h


def _my_place():
    return lax.axis_index("x"), lax.axis_index("y"), lax.axis_index("c")


def _flip(v, bit):
    return 1 - v if bit else v


def _exchange_small(name, blk, reduce):
    r, c = blk.shape

    def body(x_ref, o_ref, *rest):
        slots = rest[0] if reduce else o_ref
        send_sems, recv_sems = rest[-2], rest[-1]
        x, y, cc = _my_place()
        me = 4 * x + 2 * y + cc
        slots[me] = x_ref[...]
        copies = []
        for k in range(1, N_DEV):
            peer = (_flip(x, k & 4), _flip(y, k & 2), _flip(cc, k & 1))
            cp = pltpu.make_async_remote_copy(
                src_ref=x_ref, dst_ref=slots.at[me], send_sem=send_sems.at[k - 1],
                recv_sem=recv_sems.at[k - 1], device_id=peer, device_id_type=MESH)
            cp.start()
            copies.append(cp)
        for cp in copies:
            cp.wait()
        if reduce:
            acc = slots[0]
            for d in range(1, N_DEV):
                acc = acc + slots[d]
            o_ref[...] = acc

    scratch = ([pltpu.VMEM((N_DEV, r, c), F32)] if reduce else []) + [
        pltpu.SemaphoreType.DMA((N_DEV - 1,)), pltpu.SemaphoreType.DMA((N_DEV - 1,))]
    return pl.pallas_call(
        body, name=name,
        out_shape=jax.ShapeDtypeStruct((r, c) if reduce else (N_DEV, r, c), F32),
        in_specs=[pl.BlockSpec(memory_space=pltpu.VMEM)], out_specs=pl.BlockSpec(memory_space=pltpu.VMEM),
        scratch_shapes=scratch, compiler_params=_params(),
    )(blk)


HBM_SPEC = pl.BlockSpec(memory_space=pltpu.HBM)
SEM_SPEC = pl.BlockSpec(memory_space=pltpu.SEMAPHORE)
ANY_SPEC = pl.BlockSpec(memory_space=pl.ANY)
DATAFLOW = pltpu.SideEffectType.DATAFLOW_SIDE_EFFECTING


def _split_copy(name, arrays, start=None, wait=None, after=None):
    results, token = _split_copies(name, [(arrays, start, wait)], after)
    return results[0][0], results[0][1], token


def _split_copies(name, jobs, after=None):
    operands, in_specs, out_shape, out_specs, aliases = [], [], [], [], {}
    in_at, out_at = [], []
    for arrays, start, wait in jobs:
        in_at.append(len(operands))
        operands += [pltpu.with_memory_space_constraint(a, pltpu.HBM) for a in arrays]
        in_specs += [HBM_SPEC] * len(arrays)
        if wait:
            operands += list(wait[1])
            in_specs += [SEM_SPEC, SEM_SPEC]
    if after is not None:
        operands.append(after)
        in_specs.append(ANY_SPEC)
    for j, (arrays, start, wait) in enumerate(jobs):
        out_at.append(len(out_shape))
        if start:
            out_shape += [pltpu.SemaphoreType.DMA((start[1],)), pltpu.SemaphoreType.DMA((start[1],))]
            out_specs += [SEM_SPEC, SEM_SPEC]
        for i, a in enumerate(arrays):
            aliases[in_at[j] + i] = len(out_shape)
            out_shape.append(pltpu.HBM(a.shape, a.dtype))
            out_specs.append(HBM_SPEC)
    any_start = any(start for _, start, _ in jobs)
    if any_start:
        out_shape.append(jax.ShapeDtypeStruct((8, 128), F32))
        out_specs.append(pl.BlockSpec(memory_space=pltpu.VMEM))
    n_in = len(operands)

    def body(*refs):
        for j, (arrays, start, wait) in enumerate(jobs):
            if wait:
                ins = refs[in_at[j]:in_at[j] + len(arrays)]
                at = in_at[j] + len(arrays)
                for cp in wait[0](ins, refs[at], refs[at + 1]):
                    cp.wait_send()
                    cp.wait_recv()
        for j, (arrays, start, wait) in enumerate(jobs):
            if start:
                ins = refs[in_at[j]:in_at[j] + len(arrays)]
                at = n_in + out_at[j]
                for cp in start[0](ins, refs[at], refs[at + 1]):
                    cp.start()
        if any_start:
            token = refs[n_in + len(out_shape) - 1]
            token[...] = jnp.zeros_like(token)

    outs = pl.pallas_call(
        body, name=name, in_specs=in_specs, out_specs=out_specs, out_shape=out_shape,
        input_output_aliases=aliases, compiler_params=pltpu.CompilerParams(has_side_effects=DATAFLOW),
    )(*operands)
    results = []
    for j, (arrays, start, wait) in enumerate(jobs):
        at = out_at[j]
        sems = (outs[at], outs[at + 1]) if start else None
        at += 2 if start else 0
        results.append((list(outs[at:at + len(arrays)]), sems))
    return results, (outs[-1] if any_start else None)


def _remote(src, dst, send_sems, recv_sems, k, to):
    return pltpu.make_async_remote_copy(src_ref=src, dst_ref=dst, send_sem=send_sems.at[k],
                                        recv_sem=recv_sems.at[k], device_id=to, device_id_type=MESH)


def _slot(px, py, pc):
    return 4 * px + 2 * py + pc


def _gather_first(refs, send_sems, recv_sems):
    x, y, c = _my_place()
    blk = refs[0].at[_slot(x, y, c)]
    targets = [(x, y, 1 - c), (1 - x, y, c), (x, 1 - y, c)]
    return [_remote(blk, blk, send_sems, recv_sems, k, to) for k, to in enumerate(targets)]


def _gather_relay(refs, send_sems, recv_sems):
    x, y, c = _my_place()
    rows = refs[0].shape[1]
    half = rows // 32 * 16
    from_x, from_y = _slot(1 - x, y, c), _slot(x, 1 - y, c)
    upper = refs[0].at[from_x, pl.ds(0, half)]
    lower = refs[0].at[from_y, pl.ds(half, rows - half)]
    return [_remote(upper, upper, send_sems, recv_sems, 0, (x, 1 - y, c)),
            _remote(lower, lower, send_sems, recv_sems, 1, (1 - x, y, c)),
            _remote(refs[0].at[from_x], refs[0].at[from_x], send_sems, recv_sems, 2, (x, y, 1 - c)),
            _remote(refs[0].at[from_y], refs[0].at[from_y], send_sems, recv_sems, 3, (x, y, 1 - c))]


def _gather_last(refs, send_sems, recv_sems):
    x, y, c = _my_place()
    blk = refs[0].at[_slot(1 - x, 1 - y, c)]
    return [_remote(blk, blk, send_sems, recv_sems, 0, (x, y, 1 - c))]


def _scatter_sibling(n):
    def copies(refs, send_sems, recv_sems):
        x, y, c = _my_place()
        return [_remote(refs[a].at[2 * j + 1 - c], refs[n + a].at[j], send_sems, recv_sems, 4 * a + j, (x, y, 1 - c))
                for a in range(n) for j in range(4)]
    return copies


def _scatter_chips(n):
    def copies(refs, send_sems, recv_sems):
        x, y, c = _my_place()
        out = []
        for a in range(n):
            for k in range(1, 4):
                px, py = _flip(x, k & 2), _flip(y, k & 1)
                out.append(_remote(refs[a].at[2 * px + py], refs[n + a].at[2 * x + y], send_sems, recv_sems,
                                   3 * a + k - 1, (px, py, c)))
        return out
    return copies


def _pair_sum(name, core, g, t):
    _, r, c = g.shape
    tr = _tile(r, 512, 8)
    g4 = g.reshape(4, 2, r, c)

    def body(core_ref, g_ref, t_ref, o_ref):
        o_ref[...] = (g_ref[...].astype(F32) + t_ref[...].astype(F32)).astype(BF16)

    return pl.pallas_call(
        body, name=name,
        grid_spec=pltpu.PrefetchScalarGridSpec(
            num_scalar_prefetch=1, grid=(4, r // tr),
            in_specs=[pl.BlockSpec((None, None, tr, c), lambda j, i, core_ref: (j, core_ref[0], i, 0)),
                      pl.BlockSpec((None, tr, c), lambda j, i, core_ref: (j, i, 0))],
            out_specs=pl.BlockSpec((None, tr, c), lambda j, i, core_ref: (j, i, 0))),
        out_shape=jax.ShapeDtypeStruct((4, r, c), BF16), compiler_params=_params(),
    )(core, g4, t)


def _adam_math(w, g, m, v):
    m2 = ADAM_B1 * m + (1.0 - ADAM_B1) * g
    v2 = ADAM_B2 * v + (1.0 - ADAM_B2) * (g * g)
    m_hat = m2 / (1.0 - ADAM_B1 ** ADAM_STEP)
    v_hat = v2 / (1.0 - ADAM_B2 ** ADAM_STEP)
    delta = -ADAM_LR * (m_hat / (jnp.sqrt(v_hat) + ADAM_EPS) + ADAM_WD * w)
    return delta, m2, v2


def _adam_sharded(name, chip, w, m, v, grads):
    _, r, c = w.shape
    tr = _tile(r, 256, 8)
    tc = c if tr < r else _tile(c, 256, 128)

    def body(chip_ref, w_ref, m_ref, v_ref, p0_ref, q0_ref, p1_ref, q1_ref, g_ref, d_ref, nm_ref, nv_ref):
        mine = chip_ref[0]

        def total(p_ref, q_ref):
            acc = None
            for j in range(4):
                part = jnp.where(mine == j, p_ref[...], q_ref[j]).astype(F32)
                acc = part if acc is None else acc + part
            return acc

        g = jnp.where(pl.program_id(0) == 0, total(p0_ref, q0_ref), total(p1_ref, q1_ref))
        delta, m2, v2 = _adam_math(w_ref[...], g, m_ref[...], v_ref[...])
        g_ref[...] = g
        d_ref[...] = delta
        nm_ref[...] = m2
        nv_ref[...] = v2

    def grad_specs(layer):
        at = lambda l, i, j: (jnp.where(l == layer, i, 0), jnp.where(l == layer, j, 0))
        return [pl.BlockSpec((None, tr, tc), lambda l, i, j, chip_ref: (chip_ref[0],) + at(l, i, j)),
                pl.BlockSpec((4, tr, tc), lambda l, i, j, chip_ref: (0,) + at(l, i, j))]

    wspec = pl.BlockSpec((None, tr, tc), lambda l, i, j, chip_ref: (l, i, j))
    sds = jax.ShapeDtypeStruct(w.shape, F32)
    return pl.pallas_call(
        body, name=name,
        grid_spec=pltpu.PrefetchScalarGridSpec(
            num_scalar_prefetch=1, grid=(2, r // tr, c // tc),
            in_specs=[wspec, wspec, wspec] + grad_specs(0) + grad_specs(1), out_specs=[wspec] * 4),
        out_shape=[sds] * 4, compiler_params=_params(),
    )(chip, w, m, v, grads[0][0], grads[0][1], grads[1][0], grads[1][1])


def _adam_layer(name, chip, layer, w, m, v, grad, other=None):
    _, r, c = w.shape
    tr = _tile(r, 256, 8)
    tc = c if tr < r else _tile(c, 256, 128)

    def body(chip_ref, w_ref, m_ref, v_ref, p_ref, q_ref, *rest):
        g_ref, d_ref, nm_ref, nv_ref = rest[-4:]
        mine = chip_ref[0]
        g = None
        for j in range(4):
            part = jnp.where(mine == j, p_ref[...], q_ref[j]).astype(F32)
            g = part if g is None else g + part
        delta, m2, v2 = _adam_math(w_ref[...], g, m_ref[...], v_ref[...])
        g_ref[...] = g
        d_ref[...] = delta
        nm_ref[...] = m2
        nv_ref[...] = v2

    wspec = pl.BlockSpec((None, tr, tc), lambda i, j, chip_ref: (layer, i, j))
    in_specs = [wspec, wspec, wspec,
                pl.BlockSpec((None, tr, tc), lambda i, j, chip_ref: (chip_ref[0], i, j)),
                pl.BlockSpec((4, tr, tc), lambda i, j, chip_ref: (0, i, j))]
    ins = [chip, w, m, v, grad[0], grad[1]]
    aliases = {}
    if other is not None:
        aliases = {len(ins) + k: k for k in range(4)}
        ins += list(other)
        in_specs += [pl.BlockSpec(memory_space=pl.ANY)] * 4
    sds = jax.ShapeDtypeStruct(w.shape, F32)
    return pl.pallas_call(
        body, name=name,
        grid_spec=pltpu.PrefetchScalarGridSpec(
            num_scalar_prefetch=1, grid=(r // tr, c // tc), in_specs=in_specs, out_specs=[wspec] * 4),
        out_shape=[sds] * 4, input_output_aliases=aliases, compiler_params=_params(),
    )(*ins)


def _adam_small(name, w, m, v, g):
    def body(w_ref, m_ref, v_ref, g_ref, d_ref, nm_ref, nv_ref):
        delta, m2, v2 = _adam_math(w_ref[...], g_ref[...], m_ref[...], v_ref[...])
        d_ref[...] = delta
        nm_ref[...] = m2
        nv_ref[...] = v2

    sds = jax.ShapeDtypeStruct(w.shape, F32)
    vm = pl.BlockSpec(memory_space=pltpu.VMEM)
    return pl.pallas_call(body, name=name, in_specs=[vm] * 4, out_specs=[vm] * 3, out_shape=[sds] * 3,
                          compiler_params=_params())(w, m, v, g)


GATE_END = GATE_COL + 2 * HEADS


def _split_w_in(gathered):
    win_t = gathered.reshape(D_IN, D_MODEL)
    return win_t, win_t[GATE_END:].reshape(3, CONV_W, D_MODEL)


def _merge_dw_in(dwm_t, dwc_t):
    full = jnp.concatenate([dwm_t[:GATE_END], dwc_t.reshape(3 * CONV_W, D_MODEL)], axis=0)
    return full.reshape(N_DEV, IN_SH, D_MODEL)


def _pack128(parts):
    flat = jnp.concatenate([p.reshape(-1) for p in parts])
    n = flat.shape[0]
    rows = -(-n // 1024) * 8
    return jnp.pad(flat, (0, rows * 128 - n)).reshape(rows, 128)


def _unpack128(packed, shapes):
    flat = packed.reshape(-1)
    out, at = [], 0
    for s in shapes:
        n = int(np.prod(s))
        out.append(flat[at:at + n].reshape(s))
        at += n
    return out


def kernel(x, meta_tokens, norm_mix_w, w_in, b_gates, conv_w, mlstm_norm_w, w_out, norm_ffn_w, w_gate, w_up, w_down, norm_final_w, loss_target, m_meta_tokens, m_norm_mix_w, m_w_in, m_b_gates, m_conv_w, m_mlstm_norm_w, m_w_out, m_norm_ffn_w, m_w_gate, m_w_up, m_w_down, m_norm_final_w, v_meta_tokens, v_norm_mix_w, v_w_in, v_b_gates, v_conv_w, v_mlstm_norm_w, v_w_out, v_norm_ffn_w, v_w_gate, v_w_up, v_w_down, v_norm_final_w):
    seq = x.shape[1]
    rows = TOK0 + seq
    me = 4 * lax.axis_index("x") + 2 * lax.axis_index("y") + lax.axis_index("c")
    meta_sh = meta_tokens.shape[1]
    conv_sh = conv_w.shape[2]

    w_gate_t, m_w_gate_t, v_w_gate_t = (jnp.transpose(a, (0, 2, 1)) for a in (w_gate, m_w_gate, v_w_gate))
    w_up_t, m_w_up_t, v_w_up_t = (jnp.transpose(a, (0, 2, 1)) for a in (w_up, m_w_up, v_w_up))
    shards = []
    for l in range(DEPTH):
        shards += [jnp.transpose(w_in[l]).astype(BF16), w_out[l].astype(BF16), w_gate_t[l].astype(BF16),
                   w_up_t[l].astype(BF16), w_down[l].astype(BF16)]
    per_layer = ("w_in", "w_out", "w_gate", "w_up", "w_down")
    gather_state = {}

    def gather_step(tag, after, start=None, relay=None, last=None, done=()):
        jobs, idx = [], []
        if relay is not None and relay < len(shards):
            jobs.append((gather_state[relay][0], (_gather_relay, 4), (_gather_first, gather_state[relay][1])))
            idx.append(relay)
        if start is not None and start < len(shards):
            buf = lax.dynamic_update_index_in_dim(lax.empty((N_DEV,) + shards[start].shape, BF16), shards[start], me, 0)
            jobs.append(([buf], (_gather_first, 3), None))
            idx.append(start)
        if last is not None:
            jobs.append((gather_state[last][0], (_gather_last, 1), (_gather_relay, gather_state[last][1])))
            idx.append(last)
        for i in done:
            jobs.append((gather_state[i][0], None, (_gather_last, gather_state[i][1])))
            idx.append(i)
        if not jobs:
            return after, []
        results, tok = _split_copies(f"gather_{tag}", jobs, after)
        for i, res in zip(idx, results):
            gather_state[i] = res
        return (after if tok is None else tok), [gather_state[i][0][0] for i in done]

    bias = [jnp.pad(b_gates[l].reshape(1, 2 * HEADS), ((0, 0), (0, 128 - 2 * HEADS))) for l in range(DEPTH)]
    nmix = [norm_mix_w[l].reshape(1, D_MODEL) for l in range(DEPTH)]
    nffn = [norm_ffn_w[l].reshape(1, D_MODEL) for l in range(DEPTH)]
    nmls = [mlstm_norm_w[l].reshape(1, MLSTM_W) for l in range(DEPTH)]
    weights = [dict() for _ in range(DEPTH)]
    saved = [dict() for _ in range(DEPTH)]

    def layer_fwd(l, h, after):
        w, s = weights[l], saved[l]
        k0 = len(per_layer) * l
        tok, _ = gather_step(f"l{l}_a", after, last=k0)
        _, (g_in,) = gather_step(f"l{l}_b", tok, done=[k0])
        tok, _ = gather_step(f"l{l}_c", g_in, relay=k0 + 1, start=k0 + 3)
        w["win_t"], w["wc_t"] = _split_w_in(g_in)
        s["h0"] = h
        s["hn"], s["pm"] = _norm_proj(f"proj_mlstm_{l}", h, nmix[l] + tok[0, 0], w["win_t"], PM_W)
        tok, _ = gather_step(f"l{l}_d", s["pm"], relay=k0 + 2, start=k0 + 4)
        tok, _ = gather_step(f"l{l}_d2", tok, last=k0 + 1)
        s["pc"] = _mm_nt_bcols(f"proj_conv_{l}", s["hn"], w["wc_t"], F32, dep=tok)
        hm, s["ht"], s["cs"], s["ns"], s["ms"] = _mlstm_fwd(f"mlstm_fwd_{l}", s["pm"], bias[l] + tok[:1], nmls[l])
        tok, _ = gather_step(f"l{l}_e", hm, relay=k0 + 3, start=k0 + 5)
        tok, _ = gather_step(f"l{l}_e2", tok, last=k0 + 2)
        s["cat"] = _conv_fwd(f"conv_fwd_{l}", s["pc"], conv_rows[l] + tok[0, 0], hm)
        _, (g_out,) = gather_step(f"l{l}_f", s["cat"], done=[k0 + 1])
        w["wo"] = g_out.reshape(D_MODEL, D_MODEL)
        s["h1"], s["hf"] = _proj_res_norm(f"out_proj_{l}", s["cat"], w["wo"], s["h0"], nffn[l])
        tok_g, _ = gather_step(f"l{l}_g", s["h1"], relay=k0 + 4, start=k0 + 6)
        tok, _ = gather_step(f"l{l}_h", tok_g, last=k0 + 3)
        _, (g_gate, g_up) = gather_step(f"l{l}_i", tok, done=[k0 + 2, k0 + 3])
        w["wg_t"] = g_gate.reshape(D_FF, D_MODEL)
        w["wu_t"] = g_up.reshape(D_FF, D_MODEL)
        s["g"], s["u"], s["act"] = _ffn_in(f"ffn_in_{l}", s["hf"], w["wg_t"], w["wu_t"], dep=tok_g)
        tok, _ = gather_step(f"l{l}_j", s["act"], last=k0 + 4)
        _, (g_down,) = gather_step(f"l{l}_k", tok, done=[k0 + 4])
        w["wd"] = g_down.reshape(D_FF, D_MODEL)
        tok, _ = gather_step(f"l{l}_k2", tok, relay=k0 + 5, start=k0 + 7)
        return _mm_nn(f"ffn_out_{l}", s["act"], w["wd"], F32, res=s["h1"], dep=tok)

    tok, _ = gather_step("first", None, start=0)
    zero = tok[0, 0]
    small = jnp.concatenate(
        [meta_tokens + zero, jnp.pad(conv_w.reshape(DEPTH * 3, conv_sh), ((0, 2), (0, meta_sh - conv_sh)))], axis=0)
    slots = _exchange_small("gather_small", small, reduce=False)
    meta_full = jnp.transpose(slots[:, :N_META, :], (1, 0, 2)).reshape(N_META, D_MODEL)
    conv_full = jnp.transpose(slots[:, N_META:N_META + DEPTH * 3, :conv_sh], (1, 0, 2)).reshape(DEPTH, 3, CONV_W)
    conv_rows = [jnp.pad(conv_full[l], ((0, 5), (0, 0))) for l in range(DEPTH)]
    w_in_t, m_w_in_t, v_w_in_t = (jnp.transpose(a + zero, (0, 2, 1)) for a in (w_in, m_w_in, v_w_in))
    target_rows = jnp.pad(loss_target[0] + zero, ((TOK0, 0), (0, 0)))
    tok, w_in_t, m_w_in_t, v_w_in_t, meta_full, target_rows = lax.optimization_barrier(
        (tok, w_in_t, m_w_in_t, v_w_in_t, meta_full, target_rows))
    tok, _ = gather_step("pre_a", tok, relay=0)
    tok, _ = gather_step("pre_b", tok, start=1)
    tok, _ = gather_step("pre_c", tok, start=2)
    h = jnp.concatenate([jnp.zeros((PAD_FRONT, D_MODEL), F32), meta_full, x[0]], axis=0)
    h = layer_fwd(0, h, tok)
    h = layer_fwd(1, h, h)

    dh, dh_b, d_final, loss_part = _final_loss("final_loss", h, norm_final_w.reshape(1, D_MODEL), target_rows)

    core = lax.axis_index("c").astype(jnp.int32).reshape(1)
    chip = (2 * lax.axis_index("x") + lax.axis_index("y")).astype(jnp.int32).reshape(1)
    scatter_state = {}

    def scatter_begin(nm, grad):
        land = lax.empty((4,) + grad.shape[1:], BF16)
        arrs, sems, tok = _split_copy(f"grad_sibling_start_{nm}", [grad, land], start=(_scatter_sibling(1), 4))
        scatter_state[nm] = (arrs, sems)
        return tok

    def scatter_advance(nm, after):
        arrs, sems = scatter_state[nm]
        arrs, _, _ = _split_copy(f"grad_sibling_done_{nm}", arrs, wait=(_scatter_sibling(1), sems), after=after)
        part = _pair_sum(f"grad_pair_sum_{nm}", core, arrs[0], arrs[1])
        arrs, sems, tok = _split_copy(f"grad_chips_start_{nm}", [part, lax.empty(part.shape, BF16)],
                                      start=(_scatter_chips(1), 3))
        scatter_state[nm] = (arrs, sems)
        return tok

    def scattered(nm, after):
        arrs, sems = scatter_state[nm]
        arrs, _, _ = _split_copy(f"grad_chips_done_{nm}", arrs, wait=(_scatter_chips(1), sems), after=after)
        return arrs[0], arrs[1]

    d_mix, d_ffn, d_mls, d_bias, d_conv = ([None] * DEPTH for _ in range(5))

    def layer_bwd(l, dh, dh_b, tok):
        w, s = weights[l], saved[l]
        dg, du = _ffn_act_bwd(f"d_act_{l}", dh_b, w["wd"], s["g"], s["u"], dep=tok)
        dw_down = _mm_tn(f"dw_down_{l}", s["act"], dh_b, BF16, tm=1408, tn=1024)
        tok = scatter_begin(f"w_down_{l}", dw_down.reshape(N_DEV, FF_SH, D_MODEL))
        dhf = _mm_nn(f"d_ffn_gate_{l}", dg, w["wg_t"], F32, dep=tok)
        tok = scatter_advance(f"w_down_{l}", after=dhf)
        dhf = _mm_nn(f"d_ffn_up_{l}", du, w["wu_t"], F32, res=dhf, dep=tok)
        dw_gate = _mm_tn(f"dw_gate_{l}", dg, s["hf"], BF16, tm=1408, tn=1024)
        tok = scatter_begin(f"w_gate_{l}", dw_gate.reshape(N_DEV, FF_SH, D_MODEL))
        dw_up = _mm_tn(f"dw_up_{l}", du, s["hf"], BF16, tm=1408, tn=1024, dep=tok)
        tok = scatter_begin(f"w_up_{l}", dw_up.reshape(N_DEV, FF_SH, D_MODEL))
        dh1, dh1_b, d_ffn[l] = _rms_bwd(f"norm_ffn_bwd_{l}", s["h1"], nffn[l] + tok[0, 0], dhf, dh)
        tok = scatter_advance(f"w_gate_{l}", after=dh1)
        dcat = _mm_nt(f"d_cat_{l}", dh1_b, w["wo"], F32, tk=D_MODEL, dep=tok)
        tok = scatter_advance(f"w_up_{l}", after=dcat)
        dw_out = _mm_tn(f"dw_out_{l}", s["cat"], dh1_b, BF16, tn=1024, dep=tok)
        tok = scatter_begin(f"w_out_{l}", dw_out.reshape(N_DEV, OUT_SH, D_MODEL))
        dpm, d_mls[l], d_bias[l] = _mlstm_bwd(f"mlstm_bwd_{l}", dcat, s["pm"], s["ht"], s["cs"], s["ns"],
                                               s["ms"], bias[l] + tok[:1], nmls[l])
        dpc, d_conv[l] = _conv_bwd(f"conv_bwd_{l}", dcat, s["pc"], conv_rows[l])
        tok = scatter_advance(f"w_out_{l}", after=dpc)
        dwm_t = _mm_tn(f"dw_mlstm_{l}", dpm, s["hn"], BF16, tm=640, tn=1024, dep=tok)
        dwc_t = _mm_tn_acols(f"dw_conv_{l}", dpc, s["hn"], BF16)
        tok = scatter_begin(f"w_in_{l}", _merge_dw_in(dwm_t, dwc_t))
        dhn = _mm_nn_two(f"d_norm_{l}", dpm, w["win_t"], dpc, w["wc_t"], dep=tok)
        tok = scatter_advance(f"w_in_{l}", after=dhn)
        dh, dh_b, d_mix[l] = _rms_bwd(f"norm_mix_bwd_{l}", s["h0"], nmix[l] + tok[0, 0], dhn, dh1)
        return dh, dh_b, tok

    dh, dh_b, tok = layer_bwd(1, dh, dh_b, None)
    dh, dh_b, tok_tail = layer_bwd(0, dh, dh_b, tok)

    pq = {}
    after = dh
    for l in reversed(range(DEPTH)):
        for nm in ("w_down", "w_gate", "w_up", "w_out", "w_in"):
            if (nm, l) != ("w_in", 0):
                pq[nm, l] = scattered(f"{nm}_{l}", after)
                after = pq[nm, l][0]
    untransposed = lambda outs: [jnp.transpose(o, (0, 2, 1)) for o in outs]
    g_out, d_out, nm_out, nv_out = _adam_sharded(
        "adam_w_out", chip, w_out, m_w_out, v_w_out, [pq["w_out", 0], pq["w_out", 1]])
    g_gate, d_gate, nm_gate, nv_gate = untransposed(_adam_sharded(
        "adam_w_gate", chip, w_gate_t, m_w_gate_t, v_w_gate_t, [pq["w_gate", 0], pq["w_gate", 1]]))
    g_up, d_up, nm_up, nv_up = untransposed(_adam_sharded(
        "adam_w_up", chip, w_up_t, m_w_up_t, v_w_up_t, [pq["w_up", 0], pq["w_up", 1]]))
    g_down, d_down, nm_down, nv_down = _adam_sharded(
        "adam_w_down", chip, w_down, m_w_down, v_w_down, [pq["w_down", 0], pq["w_down", 1]])
    w_in_1 = _adam_layer("adam_w_in_1", chip, 1, w_in_t, m_w_in_t, v_w_in_t, pq["w_in", 1])
    pq["w_in", 0] = scattered("w_in_0", nv_down[0, :8, :128] + w_in_1[3][1, :8, :128])
    g_in, d_in, nm_in, nv_in = untransposed(_adam_layer(
        "adam_w_in_0", chip, 0, w_in_t, m_w_in_t, v_w_in_t, pq["w_in", 0], other=w_in_1))

    bg = jnp.concatenate([d_bias[l][0, :2 * HEADS] for l in range(DEPTH)])
    red_in = jnp.concatenate([
        dh[PAD_FRONT:TOK0], d_mix[0], d_mix[1], d_ffn[0], d_ffn[1], d_final,
        jnp.concatenate([d_mls[0], d_mls[1]], axis=1),
        jnp.stack([d_conv[l][:3] for l in range(DEPTH)]).reshape(3, 2 * CONV_W),
        jnp.pad(bg, (0, D_MODEL - bg.shape[0])).reshape(1, D_MODEL),
        jnp.pad(loss_part[:, :1], ((0, 0), (0, D_MODEL - 1))),
        jnp.zeros((5, D_MODEL), F32) + tok_tail[0, 0]], axis=0)
    red = _exchange_small("reduce_small", red_in, reduce=True)
    loss = red[26, 0]
    g_meta = lax.dynamic_slice_in_dim(red[:N_META], me * meta_sh, meta_sh, axis=1)
    g_mix, g_ffn, g_final = red[16:18], red[18:20], red[20]
    g_mls = red[21].reshape(DEPTH, MLSTM_W)
    g_conv = lax.dynamic_slice_in_dim(red[22:25].reshape(DEPTH, 3, CONV_W), me * conv_sh, conv_sh, axis=2)
    g_bias = red[25, :DEPTH * 2 * HEADS].reshape(DEPTH, 2 * HEADS)

    small_w = [meta_tokens, norm_mix_w, b_gates, conv_w, mlstm_norm_w, norm_ffn_w, norm_final_w]
    small_m = [m_meta_tokens, m_norm_mix_w, m_b_gates, m_conv_w, m_mlstm_norm_w, m_norm_ffn_w, m_norm_final_w]
    small_v = [v_meta_tokens, v_norm_mix_w, v_b_gates, v_conv_w, v_mlstm_norm_w, v_norm_ffn_w, v_norm_final_w]
    small_g = [g_meta, g_mix, g_bias, g_conv, g_mls, g_ffn, g_final]
    shapes = [a.shape for a in small_w]
    packed = _adam_small("adam_small", _pack128(small_w), _pack128(small_m), _pack128(small_v), _pack128(small_g))
    (d_meta, d_nmix, d_bg, d_cw, d_nmls, d_nffn, d_nfin), (nm_meta, nm_nmix, nm_bg, nm_cw, nm_nmls, nm_nffn, nm_nfin), \
        (nv_meta, nv_nmix, nv_bg, nv_cw, nv_nmls, nv_nffn, nv_nfin) = (_unpack128(p, shapes) for p in packed)

    grad_x = dh[TOK0:].reshape(1, seq, D_MODEL)
    return (loss, grad_x,
            g_meta, g_mix, g_in, g_bias, g_conv, g_mls, g_out, g_ffn, g_gate, g_up, g_down, g_final,
            d_meta, d_nmix, d_in, d_bg, d_cw, d_nmls, d_out, d_nffn, d_gate, d_up, d_down, d_nfin,
            nm_meta, nm_nmix, nm_in, nm_bg, nm_cw, nm_nmls, nm_out, nm_nffn, nm_gate, nm_up, nm_down, nm_nfin,
            nv_meta, nv_nmix, nv_in, nv_bg, nv_cw, nv_nmls, nv_out, nv_nffn, nv_gate, nv_up, nv_down, nv_nfin)
```

```python
import numpy as np
import jax
import jax.numpy as jnp
from jax import lax
from jax.experimental import pallas as pl
from jax.experimental.pallas import tpu as pltpu

F32 = jnp.float32
BF16 = jnp.bfloat16
MESH = pl.DeviceIdType.MESH

D_MODEL = 2048
DEPTH = 2
N_META = 16
MLSTM_W = 1024
CONV_W = 1024
HEADS = 4
DV = 256
DQK = 128
QK_W = 512
CHUNK = 64
PAD_FRONT = 48
TOK0 = PAD_FRONT + N_META
D_FF = 5632
N_DEV = 8
FF_SH = D_FF // N_DEV
D_IN = 6152
IN_SH = D_IN // N_DEV
OUT_SH = D_MODEL // N_DEV
GATE_COL = 3072
PM_W = GATE_COL + 128
GATE_CAP = 15.0
EPS = 1e-6
QSCALE = DQK ** -0.5

ADAM_LR = 0.001
ADAM_B1 = 0.9
ADAM_B2 = 0.999
ADAM_EPS = 1e-08
ADAM_WD = 0.01
ADAM_STEP = 10

V7X_VMEM_LIMIT = 50 * 1024 * 1024
V7X_MXU_COLS = 256


def _params(**kw):
    return pltpu.CompilerParams(vmem_limit_bytes=V7X_VMEM_LIMIT, **kw)


def _tile(n, target, mult):
    best = None
    for t in range(mult, min(n, target) + 1, mult):
        if n % t == 0:
            best = t
    return best if best is not None else n


def _sigmoid(x):
    return 1.0 / (1.0 + jnp.exp(-x))


NN = ((1,), (0,))
NT = ((1,), (1,))
TN = ((0,), (0,))


def _matmul(name, a, b, out_shape, out_dtype, grid, a_bs, b_bs, o_bs, dims, nk, acc_shape=None,
            res=None, res_bs=None, dep=None):
    has_res = res is not None
    n_in = 2 + has_res + (dep is not None)

    def body(*refs):
        a_ref, b_ref = refs[0], refs[1]
        r_ref = refs[2] if has_res else None
        o_ref = refs[n_in]
        x = lax.dot_general(a_ref[...], b_ref[...], (dims, ((), ())), preferred_element_type=F32)
        if nk == 1:
            if has_res:
                x = x + r_ref[...]
            o_ref[...] = x.astype(o_ref.dtype)
            return
        acc = refs[n_in + 1]
        k = pl.program_id(len(grid) - 1)

        @pl.when(k == 0)
        def _():
            acc[...] = (x + r_ref[...]) if has_res else x

        @pl.when(k > 0)
        def _():
            acc[...] += x

        @pl.when(k == nk - 1)
        def _():
            o_ref[...] = acc[...].astype(o_ref.dtype)

    ins = [a, b] + ([res] if has_res else [])
    specs = [a_bs, b_bs] + ([res_bs] if has_res else [])
    if dep is not None:
        ins.append(dep)
        specs.append(pl.BlockSpec((8, 128), lambda *_: (0, 0)))
    scratch = [pltpu.VMEM(acc_shape, F32)] if nk > 1 else []
    return pl.pallas_call(
        body, name=name, grid=grid, in_specs=specs, out_specs=o_bs,
        out_shape=jax.ShapeDtypeStruct(out_shape, out_dtype), scratch_shapes=scratch,
        compiler_params=_params(),
    )(*ins)


def _mm_nn(name, a, b, out_dtype, res=None, tm=1056, tn=512, dep=None):
    r, k = a.shape
    n = b.shape[1]
    tm, tn = _tile(r, tm, 8), _tile(n, tn, 128)
    return _matmul(name, a, b, (r, n), out_dtype, (r // tm, n // tn, 1),
                   pl.BlockSpec((tm, k), lambda i, j, s: (i, 0)),
                   pl.BlockSpec((k, tn), lambda i, j, s: (0, j)),
                   pl.BlockSpec((tm, tn), lambda i, j, s: (i, j)), NN, 1,
                   res=res, res_bs=pl.BlockSpec((tm, tn), lambda i, j, s: (i, j)), dep=dep)


def _mm_nn_two(name, a, b, a3, b3, dep=None, tm=1056, tn=512):
    r, k = a.shape
    e, _, kb = a3.shape
    n = b.shape[1]
    tm, tn = _tile(r, tm, 8), _tile(n, tn, 128)

    def body(a_ref, b_ref, a3_ref, b3_ref, *rest):
        acc = lax.dot_general(a_ref[...], b_ref[...], (NN, ((), ())), preferred_element_type=F32)
        for s in range(e):
            acc = acc + lax.dot_general(a3_ref[s], b3_ref[s], (NN, ((), ())), preferred_element_type=F32)
        rest[-1][...] = acc

    ins = [a, b, a3, b3]
    specs = [pl.BlockSpec((tm, k), lambda i, j: (i, 0)), pl.BlockSpec((k, tn), lambda i, j: (0, j)),
             pl.BlockSpec((e, tm, kb), lambda i, j: (0, i, 0)), pl.BlockSpec((e, kb, tn), lambda i, j: (0, 0, j))]
    if dep is not None:
        ins.append(dep)
        specs.append(pl.BlockSpec((8, 128), lambda *_: (0, 0)))
    return pl.pallas_call(
        body, name=name, grid=(r // tm, n // tn), in_specs=specs,
        out_specs=pl.BlockSpec((tm, tn), lambda i, j: (i, j)),
        out_shape=jax.ShapeDtypeStruct((r, n), F32), compiler_params=_params(),
    )(*ins)


def _mm_nt(name, a, b, out_dtype, res=None, tm=1056, tn=512, tk=640, n=None, dep=None):
    r, k = a.shape
    n = b.shape[0] if n is None else n
    tm, tn, tk = _tile(r, tm, 8), _tile(n, tn, 128), _tile(k, tk, 128)
    nk = k // tk
    return _matmul(name, a, b, (r, n), out_dtype, (r // tm, n // tn, nk),
                   pl.BlockSpec((tm, tk), lambda i, j, s: (i, s)),
                   pl.BlockSpec((tn, tk), lambda i, j, s: (j, s)),
                   pl.BlockSpec((tm, tn), lambda i, j, s: (i, j)), NT, nk, acc_shape=(tm, tn),
                   res=res, res_bs=pl.BlockSpec((tm, tn), lambda i, j, s: (i, j)), dep=dep)


def _mm_nt_bcols(name, a, b3, out_dtype, tm=1056, dep=None):
    r, k = a.shape
    e, n, _ = b3.shape
    tm = _tile(r, tm, 8)
    return _matmul(name, a, b3, (e, r, n), out_dtype, (r // tm, e, 1),
                   pl.BlockSpec((tm, k), lambda i, g, s: (i, 0)),
                   pl.BlockSpec((None, n, k), lambda i, g, s: (g, 0, 0)),
                   pl.BlockSpec((None, tm, n), lambda i, g, s: (g, i, 0)), NT, 1, dep=dep)


def _mm_tn(name, a, b, out_dtype, tm=1024, tn=640, dep=None):
    r, m = a.shape
    n = b.shape[1]
    tm, tn = _tile(m, tm, 128), _tile(n, tn, 128)
    return _matmul(name, a, b, (m, n), out_dtype, (m // tm, n // tn, 1),
                   pl.BlockSpec((r, tm), lambda i, j, s: (0, i)),
                   pl.BlockSpec((r, tn), lambda i, j, s: (0, j)),
                   pl.BlockSpec((tm, tn), lambda i, j, s: (i, j)), TN, 1, dep=dep)


def _mm_tn_acols(name, a3, b, out_dtype, tn=1024, dep=None):
    e, r, m = a3.shape
    n = b.shape[1]
    tn = _tile(n, tn, 128)
    return _matmul(name, a3, b, (e, m, n), out_dtype, (n // tn, e, 1),
                   pl.BlockSpec((None, r, m), lambda j, g, s: (g, 0, 0)),
                   pl.BlockSpec((r, tn), lambda j, g, s: (0, j)),
                   pl.BlockSpec((None, m, tn), lambda j, g, s: (g, 0, j)), TN, 1, dep=dep)


def _norm_proj(name, h, w, b, n, tm=1056, tn=640):
    r, d = h.shape
    tm, tn = _tile(r, tm, 8), _tile(n, tn, 128)

    def body(h_ref, w_ref, b_ref, hn_ref, o_ref):
        @pl.when(pl.program_id(1) == 0)
        def _():
            x = h_ref[...]
            rs = lax.rsqrt(jnp.mean(x * x, axis=1, keepdims=True) + EPS)
            hn_ref[...] = (x * rs * w_ref[...]).astype(BF16)

        o_ref[...] = lax.dot_general(hn_ref[...], b_ref[...], (NT, ((), ())), preferred_element_type=F32)

    row = pl.BlockSpec((tm, d), lambda i, j: (i, 0))
    return pl.pallas_call(
        body, name=name, grid=(r // tm, n // tn),
        in_specs=[row, pl.BlockSpec((1, d), lambda i, j: (0, 0)), pl.BlockSpec((tn, d), lambda i, j: (j, 0))],
        out_specs=[row, pl.BlockSpec((tm, tn), lambda i, j: (i, j))],
        out_shape=[jax.ShapeDtypeStruct((r, d), BF16), jax.ShapeDtypeStruct((r, n), F32)],
        compiler_params=_params(),
    )(h, w, b)


def _proj_res_norm(name, a, b, res, w, tm=528):
    r, k = a.shape
    d = b.shape[1]
    tm = _tile(r, tm, 8)

    def body(a_ref, b_ref, r_ref, w_ref, y_ref, n_ref):
        y = lax.dot_general(a_ref[...], b_ref[...], (NN, ((), ())), preferred_element_type=F32) + r_ref[...]
        y_ref[...] = y
        rs = lax.rsqrt(jnp.mean(y * y, axis=1, keepdims=True) + EPS)
        n_ref[...] = (y * rs * w_ref[...]).astype(BF16)

    row = pl.BlockSpec((tm, d), lambda i: (i, 0))
    return pl.pallas_call(
        body, name=name, grid=(r // tm,),
        in_specs=[pl.BlockSpec((tm, k), lambda i: (i, 0)), pl.BlockSpec((k, d), lambda i: (0, 0)), row,
                  pl.BlockSpec((1, d), lambda i: (0, 0))],
        out_specs=[row, row],
        out_shape=[jax.ShapeDtypeStruct((r, d), F32), jax.ShapeDtypeStruct((r, d), BF16)],
        compiler_params=_params(),
    )(a, b, res, w)


def _rms_bwd(name, x, w, dy, dres):
    r, d = x.shape
    tr = _tile(r, 264, 8)

    def body(x_ref, w_ref, dy_ref, dr_ref, dx_ref, dxb_ref, dw_ref):
        xv = x_ref[...]
        g = dy_ref[...]
        rs = lax.rsqrt(jnp.mean(xv * xv, axis=1, keepdims=True) + EPS)
        wg = g * w_ref[...]
        dx = rs * wg - xv * (rs * rs * rs) * jnp.mean(xv * wg, axis=1, keepdims=True) + dr_ref[...]
        dx_ref[...] = dx
        dxb_ref[...] = dx.astype(BF16)
        part = jnp.sum(g * xv * rs, axis=0, keepdims=True)

        @pl.when(pl.program_id(0) == 0)
        def _():
            dw_ref[...] = part

        @pl.when(pl.program_id(0) > 0)
        def _():
            dw_ref[...] += part

    row = pl.BlockSpec((tr, d), lambda i: (i, 0))
    vec = pl.BlockSpec((1, d), lambda i: (0, 0))
    return pl.pallas_call(
        body, name=name, grid=(r // tr,), in_specs=[row, vec, row, row], out_specs=[row, row, vec],
        out_shape=[jax.ShapeDtypeStruct((r, d), F32), jax.ShapeDtypeStruct((r, d), BF16),
                   jax.ShapeDtypeStruct((1, d), F32)],
        compiler_params=_params(),
    )(x, w, dy, dres)


def _final_loss(name, h, w, target):
    r, d = h.shape
    nb = r // CHUNK

    def body(h_ref, w_ref, t_ref, dh_ref, dhb_ref, dw_ref, ls_ref):
        i = pl.program_id(0)

        @pl.when(i == 0)
        def _():
            dh_ref[...] = jnp.zeros_like(dh_ref)
            dhb_ref[...] = jnp.zeros_like(dhb_ref)
            dw_ref[...] = jnp.zeros_like(dw_ref)
            ls_ref[...] = jnp.zeros_like(ls_ref)

        @pl.when(i > 0)
        def _():
            xv = h_ref[...]
            wv = w_ref[...]
            rs = lax.rsqrt(jnp.mean(xv * xv, axis=1, keepdims=True) + EPS)
            err = xv * rs * wv - t_ref[...]
            sq = jnp.sum(jnp.sum(err * err, axis=1, keepdims=True), axis=0, keepdims=True)
            ls_ref[...] += jnp.broadcast_to(sq * (0.5 / d), ls_ref.shape)
            g = err * (1.0 / d)
            wg = g * wv
            dx = rs * wg - xv * (rs * rs * rs) * jnp.mean(xv * wg, axis=1, keepdims=True)
            dh_ref[...] = dx
            dhb_ref[...] = dx.astype(BF16)
            dw_ref[...] += jnp.sum(g * xv * rs, axis=0, keepdims=True)

    row = pl.BlockSpec((CHUNK, d), lambda i: (i, 0))
    vec = pl.BlockSpec((1, d), lambda i: (0, 0))
    return pl.pallas_call(
        body, name=name, grid=(nb,),
        in_specs=[row, vec, pl.BlockSpec((CHUNK, d), lambda i: (jnp.maximum(i - 1, 0), 0))],
        out_specs=[row, row, vec, pl.BlockSpec((1, 128), lambda i: (0, 0))],
        out_shape=[jax.ShapeDtypeStruct((r, d), F32), jax.ShapeDtypeStruct((r, d), BF16),
                   jax.ShapeDtypeStruct((1, d), F32), jax.ShapeDtypeStruct((1, 128), F32)],
        compiler_params=_params(),
    )(h, w, target)


def _ffn_in(name, hf, wg_t, wu_t, dep=None, tm=1056, tn=512):
    r, d = hf.shape
    f = wg_t.shape[0]
    tm, tn = _tile(r, tm, 8), _tile(f, tn, 128)

    def body(h_ref, wg_ref, wu_ref, *rest):
        g_ref, u_ref, a_ref = rest[-3:]
        x = h_ref[...]
        g = lax.dot_general(x, wg_ref[...], (NT, ((), ())), preferred_element_type=F32)
        u = lax.dot_general(x, wu_ref[...], (NT, ((), ())), preferred_element_type=F32)
        g_ref[...] = g.astype(BF16)
        u_ref[...] = u.astype(BF16)
        a_ref[...] = (g * _sigmoid(g) * u).astype(BF16)

    wspec = pl.BlockSpec((tn, d), lambda i, j: (j, 0))
    ospec = pl.BlockSpec((tm, tn), lambda i, j: (i, j))
    ins, specs = [hf, wg_t, wu_t], [pl.BlockSpec((tm, d), lambda i, j: (i, 0)), wspec, wspec]
    if dep is not None:
        ins.append(dep)
        specs.append(pl.BlockSpec((8, 128), lambda *_: (0, 0)))
    return pl.pallas_call(
        body, name=name, grid=(r // tm, f // tn), in_specs=specs, out_specs=[ospec] * 3,
        out_shape=[jax.ShapeDtypeStruct((r, f), BF16)] * 3, compiler_params=_params(),
    )(*ins)


def _ffn_act_bwd(name, dh, wd, g, u, dep=None, tm=1056, tn=512):
    r, d = dh.shape
    f = wd.shape[0]
    tm, tn = _tile(r, tm, 8), _tile(f, tn, 128)

    def body(dh_ref, wd_ref, g_ref, u_ref, *rest):
        dg_ref, du_ref = rest[-2:]
        tr = _tile(tm, 264, 8)
        for r0 in range(0, tm, tr):
            for c0 in range(0, tn, V7X_MXU_COLS):
                rows, cols = slice(r0, r0 + tr), slice(c0, c0 + V7X_MXU_COLS)
                da = lax.dot_general(dh_ref[rows, :], wd_ref[cols, :], (NT, ((), ())), preferred_element_type=F32)
                gv = g_ref[rows, cols]
                s = _sigmoid(gv)
                t = da.astype(BF16) * s
                du_ref[rows, cols] = t * gv
                dg_ref[rows, cols] = t * u_ref[rows, cols] * (1.0 + gv - gv * s)

    tile = pl.BlockSpec((tm, tn), lambda i, j: (i, j))
    ins = [dh, wd, g, u]
    specs = [pl.BlockSpec((tm, d), lambda i, j: (i, 0)), pl.BlockSpec((tn, d), lambda i, j: (j, 0)), tile, tile]
    if dep is not None:
        ins.append(dep)
        specs.append(pl.BlockSpec((8, 128), lambda *_: (0, 0)))
    return pl.pallas_call(
        body, name=name, grid=(r // tm, f // tn), in_specs=specs, out_specs=[tile] * 2,
        out_shape=[jax.ShapeDtypeStruct((r, f), BF16)] * 2, compiler_params=_params(),
    )(*ins)


def _shift_down(a, k):
    row = lax.broadcasted_iota(jnp.int32, a.shape, 0)
    return jnp.where(row >= k, pltpu.roll(a, k, 0), 0.0)


def _shift_up(a, k):
    n = a.shape[0]
    row = lax.broadcasted_iota(jnp.int32, a.shape, 0)
    return jnp.where(row < n - k, pltpu.roll(a, n - k, 0), 0.0)


def _conv_fwd(name, pc, cw, cat):
    _, r, w = pc.shape
    nblk = w // 128

    def body(pc_ref, cw_ref, cat_ref, o_ref):
        a = pc_ref[2] * pc_ref[0]
        cwv = cw_ref[...]
        conv = _shift_down(a, 2) * cwv[0:1] + _shift_down(a, 1) * cwv[1:2] + a * cwv[2:3]
        o_ref[...] = (pc_ref[1] * conv).astype(BF16)

    return pl.pallas_call(
        body, name=name, grid=(nblk,),
        in_specs=[pl.BlockSpec((3, r, 128), lambda j: (0, 0, j)), pl.BlockSpec((8, 128), lambda j: (0, j)),
                  pl.BlockSpec(memory_space=pl.ANY)],
        out_specs=pl.BlockSpec((r, 128), lambda j: (0, nblk + j)),
        out_shape=jax.ShapeDtypeStruct(cat.shape, BF16), input_output_aliases={2: 0},
        compiler_params=_params(),
    )(pc, cw, cat)


def _conv_bwd(name, dcat, pc, cw):
    _, r, w = pc.shape
    nblk = w // 128

    def body(dy_ref, pc_ref, cw_ref, dpc_ref, dcw_ref):
        u, gb, gc = pc_ref[0], pc_ref[1], pc_ref[2]
        cwv = cw_ref[...]
        dy = dy_ref[...]
        a = gc * u
        a1, a2 = _shift_down(a, 1), _shift_down(a, 2)
        conv = a2 * cwv[0:1] + a1 * cwv[1:2] + a * cwv[2:3]
        dconv = dy * gb
        da = dconv * cwv[2:3] + _shift_up(dconv, 1) * cwv[1:2] + _shift_up(dconv, 2) * cwv[0:1]
        dpc_ref[0] = (da * gc).astype(BF16)
        dpc_ref[1] = (dy * conv).astype(BF16)
        dpc_ref[2] = (da * u).astype(BF16)
        row = lax.broadcasted_iota(jnp.int32, (8, 128), 0)
        dw0 = jnp.sum(dconv * a2, axis=0, keepdims=True)
        dw1 = jnp.sum(dconv * a1, axis=0, keepdims=True)
        dw2 = jnp.sum(dconv * a, axis=0, keepdims=True)
        dcw_ref[...] = jnp.where(row == 0, dw0, jnp.where(row == 1, dw1, jnp.where(row == 2, dw2, 0.0)))

    return pl.pallas_call(
        body, name=name, grid=(nblk,),
        in_specs=[pl.BlockSpec((r, 128), lambda j: (0, nblk + j)),
                  pl.BlockSpec((3, r, 128), lambda j: (0, 0, j)), pl.BlockSpec((8, 128), lambda j: (0, j))],
        out_specs=[pl.BlockSpec((3, r, 128), lambda j: (0, 0, j)), pl.BlockSpec((8, 128), lambda j: (0, j))],
        out_shape=[jax.ShapeDtypeStruct((3, r, w), BF16), jax.ShapeDtypeStruct((8, w), F32)],
        compiler_params=_params(),
    )(dcat, pc, cw)


def _dot(a, b, dims):
    return lax.dot_general(a, b, (dims, ((), ())), preferred_element_type=F32)


def _col_to_row(xc, eye):
    return jnp.sum(jnp.where(eye, xc, 0.0), axis=0, keepdims=True)


def _row_to_col(xr, eye):
    return jnp.sum(jnp.where(eye, xr, 0.0), axis=1, keepdims=True)


def _gate_tiles(graw, bias, row0):
    th = jnp.tanh((graw + bias) / GATE_CAP)
    z = GATE_CAP * th
    row = lax.broadcasted_iota(jnp.int32, graw.shape, 0) + row0
    real = row >= PAD_FRONT
    li = jnp.where(real, z, -jnp.inf)
    lf = jnp.where(real, jnp.minimum(z, 0.0) - jnp.log(1.0 + jnp.exp(-jnp.abs(z))), 0.0)
    return th, z, li, lf, real


def _interleave(gens):
    results = [None] * len(gens)
    live = list(enumerate(gens))
    while live:
        still = []
        for i, gen in live:
            try:
                next(gen)
                still.append((i, gen))
            except StopIteration as stop:
                results[i] = stop.value
        live = still
    return results


def _chunk_common(pm, h, li, lf, cst, nst, mst, tril, eye):
    kraw = pm[:, QK_W + h * DQK:QK_W + (h + 1) * DQK]
    q = (pm[:, h * DQK:(h + 1) * DQK] * QSCALE).astype(BF16)
    yield
    k = kraw.astype(BF16)
    v = pm[:, 2 * QK_W + h * DV:2 * QK_W + (h + 1) * DV].astype(BF16)
    yield
    li_c = li[:, h:h + 1]
    lf_c = lf[:, HEADS + h:HEADS + h + 1]
    li_r = _col_to_row(li_c, eye)
    yield
    lf_r = _col_to_row(lf_c, eye)
    yield
    b_c = jnp.sum(jnp.where(tril, lf_r, 0.0), axis=1, keepdims=True)
    yield
    b_r = _col_to_row(b_c, eye)
    yield
    dmat = jnp.where(tril, b_c - b_r + li_r, -jnp.inf)
    inter = b_c + mst
    yield
    mt = jnp.maximum(inter, jnp.max(dmat, axis=1, keepdims=True))
    yield
    w_inter = jnp.exp(inter - mt)
    p = jnp.exp(dmat - mt)
    yield
    s = _dot(q, k, NT) * p
    yield
    cb = cst.astype(BF16)
    nb = nst.astype(BF16).astype(F32)
    qc = _dot(q, cb, NN)
    yield
    qn = jnp.sum(q.astype(F32) * nb, axis=1, keepdims=True)
    yield
    den = w_inter * qn + jnp.sum(s, axis=1, keepdims=True)
    yield
    dn = jnp.maximum(jnp.abs(den), jnp.exp(-mt))
    b_end = b_c[CHUNK - 1:CHUNK, :]
    decay = b_end - b_c + li_c
    yield
    m_new = jnp.maximum(b_end + mst, jnp.max(decay, axis=0, keepdims=True))
    yield
    w_old = jnp.exp(b_end + mst - m_new)
    w_in = jnp.exp(decay - m_new)
    kw = (w_in * kraw).astype(BF16)
    yield
    return dict(q=q, k=k, v=v, kraw=kraw, mt=mt, w_inter=w_inter, p=p, s=s, cb=cb, nb=nb, qc=qc, qn=qn,
                den=den, dn=dn, m_new=m_new, w_old=w_old, w_in=w_in, kw=kw)


def _chunks_per_step(nc, want):
    return want if nc % want == 0 else 1


def _mlstm_fwd(name, pm, bias, nw):
    r = pm.shape[0]
    nc = r // CHUNK
    grp = _chunks_per_step(nc, 3)

    def body(pm_ref, b_ref, nw_ref, hm_ref, ht_ref, cs_ref, ns_ref, ms_ref, c_scr, n_scr, m_scr):
        step = pl.program_id(0)

        @pl.when(step == 0)
        def _():
            c_scr[...] = jnp.zeros_like(c_scr)
            n_scr[...] = jnp.zeros_like(n_scr)
            m_scr[...] = jnp.zeros_like(m_scr)

        rr = lax.broadcasted_iota(jnp.int32, (CHUNK, CHUNK), 0)
        cc = lax.broadcasted_iota(jnp.int32, (CHUNK, CHUNK), 1)
        tril, eye = cc <= rr, cc == rr
        bv, nwv = b_ref[...], nw_ref[...]
        states = [(c_scr[h], n_scr[h], m_scr[h]) for h in range(HEADS)]
        for g in range(grp):
            rows = slice(g * CHUNK, (g + 1) * CHUNK)
            pmv = pm_ref[rows, :]
            _, _, li, lf, _ = _gate_tiles(pmv[:, GATE_COL:GATE_COL + 128], bv, (step * grp + g) * CHUNK)
            def head(h, cst, nst, mst, g=g, rows=rows, pmv=pmv, li=li, lf=lf):
                f = yield from _chunk_common(pmv, h, li, lf, cst, nst, mst, tril, eye)
                num = f["w_inter"] * f["qc"] + _dot(f["s"].astype(BF16), f["v"], NN)
                yield
                hh = num / f["dn"]
                yield
                c_new = f["w_old"] * cst + _dot(f["kw"], f["v"], TN)
                yield
                n_new = f["w_old"] * nst + jnp.sum(
                    f["w_in"].astype(BF16).astype(F32) * f["k"].astype(F32), axis=0, keepdims=True)
                yield
                sl = slice(h * DV, (h + 1) * DV)
                rs = lax.rsqrt(jnp.mean(hh * hh, axis=1, keepdims=True) + EPS)
                yield
                og = pmv[:, 2 * QK_W + MLSTM_W + h * DV:2 * QK_W + MLSTM_W + (h + 1) * DV]
                cs_ref[g, h] = cst
                ns_ref[g, h] = nst
                ms_ref[g, h] = mst
                ht_ref[rows, sl] = hh
                yield
                hm_ref[rows, sl] = (_sigmoid(og) * (hh * rs * nwv[:, sl])).astype(BF16)
                return c_new, n_new, f["m_new"]

            states = _interleave([head(h, *states[h]) for h in range(HEADS)])
        for h, (cst, nst, mst) in enumerate(states):
            c_scr[h] = cst
            n_scr[h] = nst
            m_scr[h] = mst

    return pl.pallas_call(
        body, name=name, grid=(nc // grp,),
        in_specs=[pl.BlockSpec((grp * CHUNK, PM_W), lambda i: (i, 0)), pl.BlockSpec((1, 128), lambda i: (0, 0)),
                  pl.BlockSpec((1, MLSTM_W), lambda i: (0, 0))],
        out_specs=[pl.BlockSpec((grp * CHUNK, MLSTM_W), lambda i: (i, 0)),
                   pl.BlockSpec((grp * CHUNK, MLSTM_W), lambda i: (i, 0)),
                   pl.BlockSpec((grp, HEADS, DQK, DV), lambda i: (i, 0, 0, 0)),
                   pl.BlockSpec((grp, HEADS, 1, DQK), lambda i: (i, 0, 0, 0)),
                   pl.BlockSpec((grp, HEADS, 1, 1), lambda i: (i, 0, 0, 0))],
        out_shape=[jax.ShapeDtypeStruct((r, MLSTM_W + CONV_W), BF16), jax.ShapeDtypeStruct((r, MLSTM_W), F32),
                   jax.ShapeDtypeStruct((nc, HEADS, DQK, DV), F32),
                   jax.ShapeDtypeStruct((nc, HEADS, 1, DQK), F32),
                   jax.ShapeDtypeStruct((nc, HEADS, 1, 1), F32)],
        scratch_shapes=[pltpu.VMEM((HEADS, DQK, DV), F32), pltpu.VMEM((HEADS, 1, DQK), F32),
                        pltpu.VMEM((HEADS, 1, 1), F32)],
        compiler_params=_params(),
    )(pm, bias, nw)


def _mlstm_bwd(name, dcat, pm, ht, cs, ns, ms, bias, nw):
    r = pm.shape[0]
    nc = r // CHUNK
    grp = _chunks_per_step(nc, 1)
    nsteps = nc // grp

    def body(dy_ref, pm_ref, ht_ref, cs_ref, ns_ref, ms_ref, b_ref, nw_ref, dpm_ref, dnw_ref, db_ref,
             dc_scr, dn_scr):
        step = pl.program_id(0)

        @pl.when(step == 0)
        def _():
            dc_scr[...] = jnp.zeros_like(dc_scr)
            dn_scr[...] = jnp.zeros_like(dn_scr)
            dnw_ref[...] = jnp.zeros_like(dnw_ref)
            db_ref[...] = jnp.zeros_like(db_ref)

        rr = lax.broadcasted_iota(jnp.int32, (CHUNK, CHUNK), 0)
        cc = lax.broadcasted_iota(jnp.int32, (CHUNK, CHUNK), 1)
        tril, eye, triu = cc <= rr, cc == rr, cc >= rr
        lane = lax.broadcasted_iota(jnp.int32, (CHUNK, 128), 1)
        rowid = lax.broadcasted_iota(jnp.int32, (CHUNK, 1), 0)
        bv, nwv = b_ref[...], nw_ref[...]
        carried = [(dc_scr[h], dn_scr[h]) for h in range(HEADS)]
        dnw_acc = [jnp.zeros((1, DV), F32) for _ in range(HEADS)]
        db_acc = jnp.zeros((1, 128), F32)
        for g in reversed(range(grp)):
            rows = slice(g * CHUNK, (g + 1) * CHUNK)
            ci = (nsteps - 1 - step) * grp + g
            pmv = pm_ref[rows, :]
            th, z, li, lf, real = _gate_tiles(pmv[:, GATE_COL:GATE_COL + 128], bv, ci * CHUNK)
            heads = _interleave([
                _mlstm_bwd_head(h, pmv, ht_ref[rows, h * DV:(h + 1) * DV], dy_ref[rows, h * DV:(h + 1) * DV], nwv,
                                li, lf, cs_ref[g, h], ns_ref[g, h], ms_ref[g, h], carried[h][0], carried[h][1],
                                tril, eye, triu, lane, rowid, dpm_ref, rows)
                for h in range(HEADS)])
            carried = [(dc_new, dn_new) for _, dc_new, dn_new, _ in heads]
            dgt = heads[0][0] + heads[1][0] + heads[2][0] + heads[3][0]
            dnw_acc = [dnw_acc[h] + heads[h][3] for h in range(HEADS)]
            dact = jnp.where(lane < HEADS, 1.0, 1.0 - _sigmoid(z)) * (1.0 - th * th)
            dgraw = jnp.where(real & (lane < 2 * HEADS), dgt * dact, 0.0)
            dpm_ref[rows, GATE_COL:GATE_COL + 128] = dgraw.astype(BF16)
            db_acc = db_acc + jnp.sum(dgraw, axis=0, keepdims=True)
        for h, (dcn, dnn) in enumerate(carried):
            dc_scr[h] = dcn
            dn_scr[h] = dnn
            dnw_ref[:, h * DV:(h + 1) * DV] += dnw_acc[h]
        db_ref[...] += db_acc

    rev = lambda i: (nsteps - 1 - i, 0)
    rev4 = lambda i: (nsteps - 1 - i, 0, 0, 0)
    return pl.pallas_call(
        body, name=name, grid=(nsteps,),
        in_specs=[pl.BlockSpec((grp * CHUNK, MLSTM_W), rev), pl.BlockSpec((grp * CHUNK, PM_W), rev),
                  pl.BlockSpec((grp * CHUNK, MLSTM_W), rev),
                  pl.BlockSpec((grp, HEADS, DQK, DV), rev4), pl.BlockSpec((grp, HEADS, 1, DQK), rev4),
                  pl.BlockSpec((grp, HEADS, 1, 1), rev4),
                  pl.BlockSpec((1, 128), lambda i: (0, 0)), pl.BlockSpec((1, MLSTM_W), lambda i: (0, 0))],
        out_specs=[pl.BlockSpec((grp * CHUNK, PM_W), rev), pl.BlockSpec((1, MLSTM_W), lambda i: (0, 0)),
                   pl.BlockSpec((1, 128), lambda i: (0, 0))],
        out_shape=[jax.ShapeDtypeStruct((r, PM_W), BF16), jax.ShapeDtypeStruct((1, MLSTM_W), F32),
                   jax.ShapeDtypeStruct((1, 128), F32)],
        scratch_shapes=[pltpu.VMEM((HEADS, DQK, DV), F32), pltpu.VMEM((HEADS, 1, DQK), F32)],
        compiler_params=_params(),
    )(dcat, pm, ht, cs, ns, ms, bias, nw)


def _mlstm_bwd_head(h, pmv, hh, y, nwv, li, lf, cst, nst, mst, dcn, dnn, tril, eye, triu, lane, rowid,
                    dpm_ref, rows):
    f = yield from _chunk_common(pmv, h, li, lf, cst, nst, mst, tril, eye)
    q, k, v, s, p = f["q"], f["k"], f["v"], f["s"], f["p"]
    w_inter, w_in, w_old, dn = f["w_inter"], f["w_in"], f["w_old"], f["dn"]
    osl = slice(2 * QK_W + MLSTM_W + h * DV, 2 * QK_W + MLSTM_W + (h + 1) * DV)
    sg = _sigmoid(pmv[:, osl])
    yield
    rs = lax.rsqrt(jnp.mean(hh * hh, axis=1, keepdims=True) + EPS)
    yield
    nwh = nwv[:, h * DV:(h + 1) * DV]
    dpm_ref[rows, osl] = (y * (hh * rs * nwh) * sg * (1.0 - sg)).astype(BF16)
    yield
    dhn = y * sg
    dnw_h = jnp.sum(dhn * hh * rs, axis=0, keepdims=True)
    yield
    wd = dhn * nwh
    dhh = rs * wd - hh * (rs * rs * rs) * jnp.mean(hh * wd, axis=1, keepdims=True)
    yield
    dnum = dhh / dn
    dd = -jnp.sum(dhh * hh, axis=1, keepdims=True) / dn
    yield
    dden = jnp.where(jnp.abs(f["den"]) > jnp.exp(-f["mt"]), dd * jnp.sign(f["den"]), 0.0)
    dnum_b = dnum.astype(BF16)
    wdn = (w_inter * dnum).astype(BF16)
    wid = (w_inter * dden).astype(BF16).astype(F32)
    yield
    ds = _dot(dnum_b, v, NT) + dden
    yield
    dsp = (ds * p).astype(BF16)
    yield
    dq = _dot(dsp, k, NN) + _dot(wdn, f["cb"], NT) + wid * f["nb"]
    yield
    dk = _dot(dsp, q, TN)
    yield
    dv = _dot(s.astype(BF16), dnum_b, TN)
    yield
    g = ds * s
    g_col = _row_to_col(jnp.sum(g, axis=0, keepdims=True), eye)
    yield
    db = jnp.sum(g, axis=1, keepdims=True) - g_col
    dli = g_col
    yield
    db = db + (jnp.sum(dnum * f["qc"], axis=1, keepdims=True) + dden * f["qn"]) * w_inter
    yield
    dcnb = dcn.astype(BF16)
    dnnb = dnn.astype(BF16).astype(F32)
    dkw = _dot(v, dcnb, NT) + dnnb
    yield
    dk = dk + w_in * dkw
    dv = dv + _dot(f["kw"], dcnb, NN)
    yield
    ddecay = jnp.sum(dkw * f["kraw"], axis=1, keepdims=True) * w_in
    yield
    dw_old = (jnp.sum(jnp.sum(dcn * cst, axis=1, keepdims=True), axis=0, keepdims=True)
              + jnp.sum(dnn * nst, axis=1, keepdims=True))
    yield
    db_end = dw_old * w_old + jnp.sum(ddecay, axis=0, keepdims=True)
    db = db - ddecay + jnp.where(rowid == CHUNK - 1, db_end, 0.0)
    dli = dli + ddecay
    yield
    dc_new = w_old * dcn + _dot(q, wdn, TN)
    yield
    dn_new = w_old * dnn + jnp.sum(wid * q.astype(F32), axis=0, keepdims=True)
    yield
    dlf = jnp.sum(jnp.where(triu, _col_to_row(db, eye), 0.0), axis=1, keepdims=True)
    yield
    gate_part = jnp.where(lane == h, dli, 0.0) + jnp.where(lane == HEADS + h, dlf, 0.0)
    dpm_ref[rows, h * DQK:(h + 1) * DQK] = (dq * QSCALE).astype(BF16)
    yield
    dpm_ref[rows, QK_W + h * DQK:QK_W + (h + 1) * DQK] = dk.astype(BF16)
    yield
    dpm_ref[rows, 2 * QK_W + h * DV:2 * QK_W + (h + 1) * DV] = dv.astype(BF16)
    return gate_part, dc_new, dn_new, dnw_h


def _my_place():
    return lax.axis_index("x"), lax.axis_index("y"), lax.axis_index("c")


def _flip(v, bit):
    return 1 - v if bit else v


def _exchange_small(name, blk, reduce):
    r, c = blk.shape

    def body(x_ref, o_ref, *rest):
        slots = rest[0] if reduce else o_ref
        send_sems, recv_sems = rest[-2], rest[-1]
        x, y, cc = _my_place()
        me = 4 * x + 2 * y + cc
        slots[me] = x_ref[...]
        copies = []
        for k in range(1, N_DEV):
            peer = (_flip(x, k & 4), _flip(y, k & 2), _flip(cc, k & 1))
            cp = pltpu.make_async_remote_copy(
                src_ref=x_ref, dst_ref=slots.at[me], send_sem=send_sems.at[k - 1],
                recv_sem=recv_sems.at[k - 1], device_id=peer, device_id_type=MESH)
            cp.start()
            copies.append(cp)
        for cp in copies:
            cp.wait()
        if reduce:
            acc = slots[0]
            for d in range(1, N_DEV):
                acc = acc + slots[d]
            o_ref[...] = acc

    scratch = ([pltpu.VMEM((N_DEV, r, c), F32)] if reduce else []) + [
        pltpu.SemaphoreType.DMA((N_DEV - 1,)), pltpu.SemaphoreType.DMA((N_DEV - 1,))]
    return pl.pallas_call(
        body, name=name,
        out_shape=jax.ShapeDtypeStruct((r, c) if reduce else (N_DEV, r, c), F32),
        in_specs=[pl.BlockSpec(memory_space=pltpu.VMEM)], out_specs=pl.BlockSpec(memory_space=pltpu.VMEM),
        scratch_shapes=scratch, compiler_params=_params(),
    )(blk)


HBM_SPEC = pl.BlockSpec(memory_space=pltpu.HBM)
SEM_SPEC = pl.BlockSpec(memory_space=pltpu.SEMAPHORE)
ANY_SPEC = pl.BlockSpec(memory_space=pl.ANY)
DATAFLOW = pltpu.SideEffectType.DATAFLOW_SIDE_EFFECTING


def _split_copy(name, arrays, start=None, wait=None, after=None):
    results, token = _split_copies(name, [(arrays, start, wait)], after)
    return results[0][0], results[0][1], token


def _split_copies(name, jobs, after=None):
    operands, in_specs, out_shape, out_specs, aliases = [], [], [], [], {}
    in_at, out_at = [], []
    for arrays, start, wait in jobs:
        in_at.append(len(operands))
        operands += [pltpu.with_memory_space_constraint(a, pltpu.HBM) for a in arrays]
        in_specs += [HBM_SPEC] * len(arrays)
        if wait:
            operands += list(wait[1])
            in_specs += [SEM_SPEC, SEM_SPEC]
    if after is not None:
        operands.append(after)
        in_specs.append(ANY_SPEC)
    for j, (arrays, start, wait) in enumerate(jobs):
        out_at.append(len(out_shape))
        if start:
            out_shape += [pltpu.SemaphoreType.DMA((start[1],)), pltpu.SemaphoreType.DMA((start[1],))]
            out_specs += [SEM_SPEC, SEM_SPEC]
        for i, a in enumerate(arrays):
            aliases[in_at[j] + i] = len(out_shape)
            out_shape.append(pltpu.HBM(a.shape, a.dtype))
            out_specs.append(HBM_SPEC)
    any_start = any(start for _, start, _ in jobs)
    if any_start:
        out_shape.append(jax.ShapeDtypeStruct((8, 128), F32))
        out_specs.append(pl.BlockSpec(memory_space=pltpu.VMEM))
    n_in = len(operands)

    def body(*refs):
        for j, (arrays, start, wait) in enumerate(jobs):
            if wait:
                ins = refs[in_at[j]:in_at[j] + len(arrays)]
                at = in_at[j] + len(arrays)
                for cp in wait[0](ins, refs[at], refs[at + 1]):
                    cp.wait_send()
                    cp.wait_recv()
        for j, (arrays, start, wait) in enumerate(jobs):
            if start:
                ins = refs[in_at[j]:in_at[j] + len(arrays)]
                at = n_in + out_at[j]
                for cp in start[0](ins, refs[at], refs[at + 1]):
                    cp.start()
        if any_start:
            token = refs[n_in + len(out_shape) - 1]
            token[...] = jnp.zeros_like(token)

    outs = pl.pallas_call(
        body, name=name, in_specs=in_specs, out_specs=out_specs, out_shape=out_shape,
        input_output_aliases=aliases, compiler_params=pltpu.CompilerParams(has_side_effects=DATAFLOW),
    )(*operands)
    results = []
    for j, (arrays, start, wait) in enumerate(jobs):
        at = out_at[j]
        sems = (outs[at], outs[at + 1]) if start else None
        at += 2 if start else 0
        results.append((list(outs[at:at + len(arrays)]), sems))
    return results, (outs[-1] if any_start else None)


def _remote(src, dst, send_sems, recv_sems, k, to):
    return pltpu.make_async_remote_copy(src_ref=src, dst_ref=dst, send_sem=send_sems.at[k],
                                        recv_sem=recv_sems.at[k], device_id=to, device_id_type=MESH)


def _slot(px, py, pc):
    return 4 * px + 2 * py + pc


def _gather_first(refs, send_sems, recv_sems):
    x, y, c = _my_place()
    blk = refs[0].at[_slot(x, y, c)]
    targets = [(x, y, 1 - c), (1 - x, y, c), (x, 1 - y, c)]
    return [_remote(blk, blk, send_sems, recv_sems, k, to) for k, to in enumerate(targets)]


def _gather_relay(refs, send_sems, recv_sems):
    x, y, c = _my_place()
    rows = refs[0].shape[1]
    half = rows // 32 * 16
    from_x, from_y = _slot(1 - x, y, c), _slot(x, 1 - y, c)
    upper = refs[0].at[from_x, pl.ds(0, half)]
    lower = refs[0].at[from_y, pl.ds(half, rows - half)]
    return [_remote(upper, upper, send_sems, recv_sems, 0, (x, 1 - y, c)),
            _remote(lower, lower, send_sems, recv_sems, 1, (1 - x, y, c)),
            _remote(refs[0].at[from_x], refs[0].at[from_x], send_sems, recv_sems, 2, (x, y, 1 - c)),
            _remote(refs[0].at[from_y], refs[0].at[from_y], send_sems, recv_sems, 3, (x, y, 1 - c))]


def _gather_last(refs, send_sems, recv_sems):
    x, y, c = _my_place()
    blk = refs[0].at[_slot(1 - x, 1 - y, c)]
    return [_remote(blk, blk, send_sems, recv_sems, 0, (x, y, 1 - c))]


def _scatter_sibling(n):
    def copies(refs, send_sems, recv_sems):
        x, y, c = _my_place()
        return [_remote(refs[a].at[2 * j + 1 - c], refs[n + a].at[j], send_sems, recv_sems, 4 * a + j, (x, y, 1 - c))
                for a in range(n) for j in range(4)]
    return copies


def _scatter_chips(n):
    def copies(refs, send_sems, recv_sems):
        x, y, c = _my_place()
        out = []
        for a in range(n):
            for k in range(1, 4):
                px, py = _flip(x, k & 2), _flip(y, k & 1)
                out.append(_remote(refs[a].at[2 * px + py], refs[n + a].at[2 * x + y], send_sems, recv_sems,
                                   3 * a + k - 1, (px, py, c)))
        return out
    return copies


def _pair_sum(name, core, g, t):
    _, r, c = g.shape
    tr = _tile(r, 512, 8)
    g4 = g.reshape(4, 2, r, c)

    def body(core_ref, g_ref, t_ref, o_ref):
        o_ref[...] = (g_ref[...].astype(F32) + t_ref[...].astype(F32)).astype(BF16)

    return pl.pallas_call(
        body, name=name,
        grid_spec=pltpu.PrefetchScalarGridSpec(
            num_scalar_prefetch=1, grid=(4, r // tr),
            in_specs=[pl.BlockSpec((None, None, tr, c), lambda j, i, core_ref: (j, core_ref[0], i, 0)),
                      pl.BlockSpec((None, tr, c), lambda j, i, core_ref: (j, i, 0))],
            out_specs=pl.BlockSpec((None, tr, c), lambda j, i, core_ref: (j, i, 0))),
        out_shape=jax.ShapeDtypeStruct((4, r, c), BF16), compiler_params=_params(),
    )(core, g4, t)


def _adam_math(w, g, m, v):
    m2 = ADAM_B1 * m + (1.0 - ADAM_B1) * g
    v2 = ADAM_B2 * v + (1.0 - ADAM_B2) * (g * g)
    m_hat = m2 / (1.0 - ADAM_B1 ** ADAM_STEP)
    v_hat = v2 / (1.0 - ADAM_B2 ** ADAM_STEP)
    delta = -ADAM_LR * (m_hat / (jnp.sqrt(v_hat) + ADAM_EPS) + ADAM_WD * w)
    return delta, m2, v2


def _adam_sharded(name, chip, w, m, v, grads):
    _, r, c = w.shape
    tr = _tile(r, 256, 8)
    tc = c if tr < r else _tile(c, 256, 128)

    def body(chip_ref, w_ref, m_ref, v_ref, p0_ref, q0_ref, p1_ref, q1_ref, g_ref, d_ref, nm_ref, nv_ref):
        mine = chip_ref[0]

        def total(p_ref, q_ref):
            acc = None
            for j in range(4):
                part = jnp.where(mine == j, p_ref[...], q_ref[j]).astype(F32)
                acc = part if acc is None else acc + part
            return acc

        g = jnp.where(pl.program_id(0) == 0, total(p0_ref, q0_ref), total(p1_ref, q1_ref))
        delta, m2, v2 = _adam_math(w_ref[...], g, m_ref[...], v_ref[...])
        g_ref[...] = g
        d_ref[...] = delta
        nm_ref[...] = m2
        nv_ref[...] = v2

    def grad_specs(layer):
        at = lambda l, i, j: (jnp.where(l == layer, i, 0), jnp.where(l == layer, j, 0))
        return [pl.BlockSpec((None, tr, tc), lambda l, i, j, chip_ref: (chip_ref[0],) + at(l, i, j)),
                pl.BlockSpec((4, tr, tc), lambda l, i, j, chip_ref: (0,) + at(l, i, j))]

    wspec = pl.BlockSpec((None, tr, tc), lambda l, i, j, chip_ref: (l, i, j))
    sds = jax.ShapeDtypeStruct(w.shape, F32)
    return pl.pallas_call(
        body, name=name,
        grid_spec=pltpu.PrefetchScalarGridSpec(
            num_scalar_prefetch=1, grid=(2, r // tr, c // tc),
            in_specs=[wspec, wspec, wspec] + grad_specs(0) + grad_specs(1), out_specs=[wspec] * 4),
        out_shape=[sds] * 4, compiler_params=_params(),
    )(chip, w, m, v, grads[0][0], grads[0][1], grads[1][0], grads[1][1])


def _adam_layer(name, chip, layer, w, m, v, grad, other=None):
    _, r, c = w.shape
    tr = _tile(r, 256, 8)
    tc = c if tr < r else _tile(c, 256, 128)

    def body(chip_ref, w_ref, m_ref, v_ref, p_ref, q_ref, *rest):
        g_ref, d_ref, nm_ref, nv_ref = rest[-4:]
        mine = chip_ref[0]
        g = None
        for j in range(4):
            part = jnp.where(mine == j, p_ref[...], q_ref[j]).astype(F32)
            g = part if g is None else g + part
        delta, m2, v2 = _adam_math(w_ref[...], g, m_ref[...], v_ref[...])
        g_ref[...] = g
        d_ref[...] = delta
        nm_ref[...] = m2
        nv_ref[...] = v2

    wspec = pl.BlockSpec((None, tr, tc), lambda i, j, chip_ref: (layer, i, j))
    in_specs = [wspec, wspec, wspec,
                pl.BlockSpec((None, tr, tc), lambda i, j, chip_ref: (chip_ref[0], i, j)),
                pl.BlockSpec((4, tr, tc), lambda i, j, chip_ref: (0, i, j))]
    ins = [chip, w, m, v, grad[0], grad[1]]
    aliases = {}
    if other is not None:
        aliases = {len(ins) + k: k for k in range(4)}
        ins += list(other)
        in_specs += [pl.BlockSpec(memory_space=pl.ANY)] * 4
    sds = jax.ShapeDtypeStruct(w.shape, F32)
    return pl.pallas_call(
        body, name=name,
        grid_spec=pltpu.PrefetchScalarGridSpec(
            num_scalar_prefetch=1, grid=(r // tr, c // tc), in_specs=in_specs, out_specs=[wspec] * 4),
        out_shape=[sds] * 4, input_output_aliases=aliases, compiler_params=_params(),
    )(*ins)


def _adam_small(name, w, m, v, g):
    def body(w_ref, m_ref, v_ref, g_ref, d_ref, nm_ref, nv_ref):
        delta, m2, v2 = _adam_math(w_ref[...], g_ref[...], m_ref[...], v_ref[...])
        d_ref[...] = delta
        nm_ref[...] = m2
        nv_ref[...] = v2

    sds = jax.ShapeDtypeStruct(w.shape, F32)
    vm = pl.BlockSpec(memory_space=pltpu.VMEM)
    return pl.pallas_call(body, name=name, in_specs=[vm] * 4, out_specs=[vm] * 3, out_shape=[sds] * 3,
                          compiler_params=_params())(w, m, v, g)


GATE_END = GATE_COL + 2 * HEADS


def _split_w_in(gathered):
    win_t = gathered.reshape(D_IN, D_MODEL)
    return win_t, win_t[GATE_END:].reshape(3, CONV_W, D_MODEL)


def _merge_dw_in(dwm_t, dwc_t):
    full = jnp.concatenate([dwm_t[:GATE_END], dwc_t.reshape(3 * CONV_W, D_MODEL)], axis=0)
    return full.reshape(N_DEV, IN_SH, D_MODEL)


def _pack128(parts):
    flat = jnp.concatenate([p.reshape(-1) for p in parts])
    n = flat.shape[0]
    rows = -(-n // 1024) * 8
    return jnp.pad(flat, (0, rows * 128 - n)).reshape(rows, 128)


def _unpack128(packed, shapes):
    flat = packed.reshape(-1)
    out, at = [], 0
    for s in shapes:
        n = int(np.prod(s))
        out.append(flat[at:at + n].reshape(s))
        at += n
    return out


def kernel(x, meta_tokens, norm_mix_w, w_in, b_gates, conv_w, mlstm_norm_w, w_out, norm_ffn_w, w_gate, w_up, w_down, norm_final_w, loss_target, m_meta_tokens, m_norm_mix_w, m_w_in, m_b_gates, m_conv_w, m_mlstm_norm_w, m_w_out, m_norm_ffn_w, m_w_gate, m_w_up, m_w_down, m_norm_final_w, v_meta_tokens, v_norm_mix_w, v_w_in, v_b_gates, v_conv_w, v_mlstm_norm_w, v_w_out, v_norm_ffn_w, v_w_gate, v_w_up, v_w_down, v_norm_final_w):
    seq = x.shape[1]
    rows = TOK0 + seq
    me = 4 * lax.axis_index("x") + 2 * lax.axis_index("y") + lax.axis_index("c")
    meta_sh = meta_tokens.shape[1]
    conv_sh = conv_w.shape[2]

    w_gate_t, m_w_gate_t, v_w_gate_t = (jnp.transpose(a, (0, 2, 1)) for a in (w_gate, m_w_gate, v_w_gate))
    w_up_t, m_w_up_t, v_w_up_t = (jnp.transpose(a, (0, 2, 1)) for a in (w_up, m_w_up, v_w_up))
    shards = []
    for l in range(DEPTH):
        shards += [jnp.transpose(w_in[l]).astype(BF16), w_out[l].astype(BF16), w_gate_t[l].astype(BF16),
                   w_up_t[l].astype(BF16), w_down[l].astype(BF16)]
    per_layer = ("w_in", "w_out", "w_gate", "w_up", "w_down")
    gather_state = {}

    def gather_step(tag, after, start=None, relay=None, last=None, done=()):
        jobs, idx = [], []
        if relay is not None and relay < len(shards):
            jobs.append((gather_state[relay][0], (_gather_relay, 4), (_gather_first, gather_state[relay][1])))
            idx.append(relay)
        if start is not None and start < len(shards):
            buf = lax.dynamic_update_index_in_dim(lax.empty((N_DEV,) + shards[start].shape, BF16), shards[start], me, 0)
            jobs.append(([buf], (_gather_first, 3), None))
            idx.append(start)
        if last is not None:
            jobs.append((gather_state[last][0], (_gather_last, 1), (_gather_relay, gather_state[last][1])))
            idx.append(last)
        for i in done:
            jobs.append((gather_state[i][0], None, (_gather_last, gather_state[i][1])))
            idx.append(i)
        if not jobs:
            return after, []
        results, tok = _split_copies(f"gather_{tag}", jobs, after)
        for i, res in zip(idx, results):
            gather_state[i] = res
        return (after if tok is None else tok), [gather_state[i][0][0] for i in done]

    bias = [jnp.pad(b_gates[l].reshape(1, 2 * HEADS), ((0, 0), (0, 128 - 2 * HEADS))) for l in range(DEPTH)]
    nmix = [norm_mix_w[l].reshape(1, D_MODEL) for l in range(DEPTH)]
    nffn = [norm_ffn_w[l].reshape(1, D_MODEL) for l in range(DEPTH)]
    nmls = [mlstm_norm_w[l].reshape(1, MLSTM_W) for l in range(DEPTH)]
    weights = [dict() for _ in range(DEPTH)]
    saved = [dict() for _ in range(DEPTH)]

    def layer_fwd(l, h, after):
        w, s = weights[l], saved[l]
        k0 = len(per_layer) * l
        tok, _ = gather_step(f"l{l}_a", after, last=k0)
        _, (g_in,) = gather_step(f"l{l}_b", tok, done=[k0])
        tok, _ = gather_step(f"l{l}_c", g_in, relay=k0 + 1, start=k0 + 3)
        w["win_t"], w["wc_t"] = _split_w_in(g_in)
        s["h0"] = h
        s["hn"], s["pm"] = _norm_proj(f"proj_mlstm_{l}", h, nmix[l] + tok[0, 0], w["win_t"], PM_W)
        tok, _ = gather_step(f"l{l}_d", s["pm"], relay=k0 + 2, start=k0 + 4)
        tok, _ = gather_step(f"l{l}_d2", tok, last=k0 + 1)
        s["pc"] = _mm_nt_bcols(f"proj_conv_{l}", s["hn"], w["wc_t"], F32, dep=tok)
        hm, s["ht"], s["cs"], s["ns"], s["ms"] = _mlstm_fwd(f"mlstm_fwd_{l}", s["pm"], bias[l] + tok[:1], nmls[l])
        tok, _ = gather_step(f"l{l}_e", hm, relay=k0 + 3, start=k0 + 5)
        tok, _ = gather_step(f"l{l}_e2", tok, last=k0 + 2)
        s["cat"] = _conv_fwd(f"conv_fwd_{l}", s["pc"], conv_rows[l] + tok[0, 0], hm)
        _, (g_out,) = gather_step(f"l{l}_f", s["cat"], done=[k0 + 1])
        w["wo"] = g_out.reshape(D_MODEL, D_MODEL)
        s["h1"], s["hf"] = _proj_res_norm(f"out_proj_{l}", s["cat"], w["wo"], s["h0"], nffn[l])
        tok_g, _ = gather_step(f"l{l}_g", s["h1"], relay=k0 + 4, start=k0 + 6)
        tok, _ = gather_step(f"l{l}_h", tok_g, last=k0 + 3)
        _, (g_gate, g_up) = gather_step(f"l{l}_i", tok, done=[k0 + 2, k0 + 3])
        w["wg_t"] = g_gate.reshape(D_FF, D_MODEL)
        w["wu_t"] = g_up.reshape(D_FF, D_MODEL)
        s["g"], s["u"], s["act"] = _ffn_in(f"ffn_in_{l}", s["hf"], w["wg_t"], w["wu_t"], dep=tok_g)
        tok, _ = gather_step(f"l{l}_j", s["act"], last=k0 + 4)
        _, (g_down,) = gather_step(f"l{l}_k", tok, done=[k0 + 4])
        w["wd"] = g_down.reshape(D_FF, D_MODEL)
        tok, _ = gather_step(f"l{l}_k2", tok, relay=k0 + 5, start=k0 + 7)
        return _mm_nn(f"ffn_out_{l}", s["act"], w["wd"], F32, res=s["h1"], dep=tok)

    tok, _ = gather_step("first", None, start=0)
    zero = tok[0, 0]
    small = jnp.concatenate(
        [meta_tokens + zero, jnp.pad(conv_w.reshape(DEPTH * 3, conv_sh), ((0, 2), (0, meta_sh - conv_sh)))], axis=0)
    slots = _exchange_small("gather_small", small, reduce=False)
    meta_full = jnp.transpose(slots[:, :N_META, :], (1, 0, 2)).reshape(N_META, D_MODEL)
    conv_full = jnp.transpose(slots[:, N_META:N_META + DEPTH * 3, :conv_sh], (1, 0, 2)).reshape(DEPTH, 3, CONV_W)
    conv_rows = [jnp.pad(conv_full[l], ((0, 5), (0, 0))) for l in range(DEPTH)]
    w_in_t, m_w_in_t, v_w_in_t = (jnp.transpose(a + zero, (0, 2, 1)) for a in (w_in, m_w_in, v_w_in))
    tok, w_in_t, m_w_in_t, v_w_in_t, meta_full = lax.optimization_barrier(
        (tok, w_in_t, m_w_in_t, v_w_in_t, meta_full))
    tok, _ = gather_step("pre_a", tok, relay=0)
    tok, _ = gather_step("pre_b", tok, start=1)
    tok, _ = gather_step("pre_c", tok, start=2)
    h = jnp.concatenate([jnp.zeros((PAD_FRONT, D_MODEL), F32), meta_full, x[0]], axis=0)
    h = layer_fwd(0, h, tok)
    h = layer_fwd(1, h, h)

    dh, dh_b, d_final, loss_part = _final_loss("final_loss", h, norm_final_w.reshape(1, D_MODEL), loss_target[0])

    core = lax.axis_index("c").astype(jnp.int32).reshape(1)
    chip = (2 * lax.axis_index("x") + lax.axis_index("y")).astype(jnp.int32).reshape(1)
    scatter_state = {}

    def scatter_begin(nm, grad):
        land = lax.empty((4,) + grad.shape[1:], BF16)
        arrs, sems, tok = _split_copy(f"grad_sibling_start_{nm}", [grad, land], start=(_scatter_sibling(1), 4))
        scatter_state[nm] = (arrs, sems)
        return tok

    def scatter_advance(nm, after):
        arrs, sems = scatter_state[nm]
        arrs, _, _ = _split_copy(f"grad_sibling_done_{nm}", arrs, wait=(_scatter_sibling(1), sems), after=after)
        part = _pair_sum(f"grad_pair_sum_{nm}", core, arrs[0], arrs[1])
        arrs, sems, tok = _split_copy(f"grad_chips_start_{nm}", [part, lax.empty(part.shape, BF16)],
                                      start=(_scatter_chips(1), 3))
        scatter_state[nm] = (arrs, sems)
        return tok

    def scattered(nm, after):
        arrs, sems = scatter_state[nm]
        arrs, _, _ = _split_copy(f"grad_chips_done_{nm}", arrs, wait=(_scatter_chips(1), sems), after=after)
        return arrs[0], arrs[1]

    d_mix, d_ffn, d_mls, d_bias, d_conv = ([None] * DEPTH for _ in range(5))

    def layer_bwd(l, dh, dh_b, tok):
        w, s = weights[l], saved[l]
        dg, du = _ffn_act_bwd(f"d_act_{l}", dh_b, w["wd"], s["g"], s["u"], dep=tok)
        dw_down = _mm_tn(f"dw_down_{l}", s["act"], dh_b, BF16, tm=1408, tn=1024)
        tok = scatter_begin(f"w_down_{l}", dw_down.reshape(N_DEV, FF_SH, D_MODEL))
        dhf = _mm_nn(f"d_ffn_gate_{l}", dg, w["wg_t"], F32, dep=tok)
        tok = scatter_advance(f"w_down_{l}", after=dhf)
        dhf = _mm_nn(f"d_ffn_up_{l}", du, w["wu_t"], F32, res=dhf, dep=tok)
        dw_gate = _mm_tn(f"dw_gate_{l}", dg, s["hf"], BF16, tm=1408, tn=1024)
        tok = scatter_begin(f"w_gate_{l}", dw_gate.reshape(N_DEV, FF_SH, D_MODEL))
        dw_up = _mm_tn(f"dw_up_{l}", du, s["hf"], BF16, tm=1408, tn=1024, dep=tok)
        tok = scatter_begin(f"w_up_{l}", dw_up.reshape(N_DEV, FF_SH, D_MODEL))
        dh1, dh1_b, d_ffn[l] = _rms_bwd(f"norm_ffn_bwd_{l}", s["h1"], nffn[l] + tok[0, 0], dhf, dh)
        tok = scatter_advance(f"w_gate_{l}", after=dh1)
        dcat = _mm_nt(f"d_cat_{l}", dh1_b, w["wo"], F32, tk=D_MODEL, dep=tok)
        tok = scatter_advance(f"w_up_{l}", after=dcat)
        dw_out = _mm_tn(f"dw_out_{l}", s["cat"], dh1_b, BF16, tn=1024, dep=tok)
        tok = scatter_begin(f"w_out_{l}", dw_out.reshape(N_DEV, OUT_SH, D_MODEL))
        dpm, d_mls[l], d_bias[l] = _mlstm_bwd(f"mlstm_bwd_{l}", dcat, s["pm"], s["ht"], s["cs"], s["ns"],
                                               s["ms"], bias[l] + tok[:1], nmls[l])
        dpc, d_conv[l] = _conv_bwd(f"conv_bwd_{l}", dcat, s["pc"], conv_rows[l])
        tok = scatter_advance(f"w_out_{l}", after=dpc)
        dwm_t = _mm_tn(f"dw_mlstm_{l}", dpm, s["hn"], BF16, tm=640, tn=1024, dep=tok)
        dwc_t = _mm_tn_acols(f"dw_conv_{l}", dpc, s["hn"], BF16)
        tok = scatter_begin(f"w_in_{l}", _merge_dw_in(dwm_t, dwc_t))
        dhn = _mm_nn_two(f"d_norm_{l}", dpm, w["win_t"], dpc, w["wc_t"], dep=tok)
        tok = scatter_advance(f"w_in_{l}", after=dhn)
        dh, dh_b, d_mix[l] = _rms_bwd(f"norm_mix_bwd_{l}", s["h0"], nmix[l] + tok[0, 0], dhn, dh1)
        return dh, dh_b, tok

    dh, dh_b, tok = layer_bwd(1, dh, dh_b, None)
    dh, dh_b, tok_tail = layer_bwd(0, dh, dh_b, tok)

    early = [(nm, l) for l in reversed(range(DEPTH)) for nm in ("w_down", "w_gate", "w_up", "w_out", "w_in")
             if (nm, l) != ("w_in", 0)]
    results, _ = _split_copies(
        "grad_chips_done_early",
        [(scatter_state[f"{nm}_{l}"][0], None, (_scatter_chips(1), scatter_state[f"{nm}_{l}"][1])) for nm, l in early],
        after=dh)
    pq = {key: (arrs[0], arrs[1]) for key, (arrs, _) in zip(early, results)}
    untransposed = lambda outs: [jnp.transpose(o, (0, 2, 1)) for o in outs]
    g_out, d_out, nm_out, nv_out = _adam_sharded(
        "adam_w_out", chip, w_out, m_w_out, v_w_out, [pq["w_out", 0], pq["w_out", 1]])
    g_gate, d_gate, nm_gate, nv_gate = untransposed(_adam_sharded(
        "adam_w_gate", chip, w_gate_t, m_w_gate_t, v_w_gate_t, [pq["w_gate", 0], pq["w_gate", 1]]))
    g_up, d_up, nm_up, nv_up = untransposed(_adam_sharded(
        "adam_w_up", chip, w_up_t, m_w_up_t, v_w_up_t, [pq["w_up", 0], pq["w_up", 1]]))
    g_down, d_down, nm_down, nv_down = _adam_sharded(
        "adam_w_down", chip, w_down, m_w_down, v_w_down, [pq["w_down", 0], pq["w_down", 1]])
    w_in_1 = _adam_layer("adam_w_in_1", chip, 1, w_in_t, m_w_in_t, v_w_in_t, pq["w_in", 1])
    pq["w_in", 0] = scattered("w_in_0", nv_down[0, :8, :128] + w_in_1[3][1, :8, :128])
    g_in, d_in, nm_in, nv_in = untransposed(_adam_layer(
        "adam_w_in_0", chip, 0, w_in_t, m_w_in_t, v_w_in_t, pq["w_in", 0], other=w_in_1))

    bg = jnp.concatenate([d_bias[l][0, :2 * HEADS] for l in range(DEPTH)])
    red_in = jnp.concatenate([
        dh[PAD_FRONT:TOK0], d_mix[0], d_mix[1], d_ffn[0], d_ffn[1], d_final,
        jnp.concatenate([d_mls[0], d_mls[1]], axis=1),
        jnp.stack([d_conv[l][:3] for l in range(DEPTH)]).reshape(3, 2 * CONV_W),
        jnp.pad(bg, (0, D_MODEL - bg.shape[0])).reshape(1, D_MODEL),
        jnp.pad(loss_part[:, :1], ((0, 0), (0, D_MODEL - 1))),
        jnp.zeros((5, D_MODEL), F32) + tok_tail[0, 0]], axis=0)
    red = _exchange_small("reduce_small", red_in, reduce=True)
    loss = red[26, 0]
    g_meta = lax.dynamic_slice_in_dim(red[:N_META], me * meta_sh, meta_sh, axis=1)
    g_mix, g_ffn, g_final = red[16:18], red[18:20], red[20]
    g_mls = red[21].reshape(DEPTH, MLSTM_W)
    g_conv = lax.dynamic_slice_in_dim(red[22:25].reshape(DEPTH, 3, CONV_W), me * conv_sh, conv_sh, axis=2)
    g_bias = red[25, :DEPTH * 2 * HEADS].reshape(DEPTH, 2 * HEADS)

    small_w = [meta_tokens, norm_mix_w, b_gates, conv_w, mlstm_norm_w, norm_ffn_w, norm_final_w]
    small_m = [m_meta_tokens, m_norm_mix_w, m_b_gates, m_conv_w, m_mlstm_norm_w, m_norm_ffn_w, m_norm_final_w]
    small_v = [v_meta_tokens, v_norm_mix_w, v_b_gates, v_conv_w, v_mlstm_norm_w, v_norm_ffn_w, v_norm_final_w]
    small_g = [g_meta, g_mix, g_bias, g_conv, g_mls, g_ffn, g_final]
    shapes = [a.shape for a in small_w]
    packed = _adam_small("adam_small", _pack128(small_w), _pack128(small_m), _pack128(small_v), _pack128(small_g))
    (d_meta, d_nmix, d_bg, d_cw, d_nmls, d_nffn, d_nfin), (nm_meta, nm_nmix, nm_bg, nm_cw, nm_nmls, nm_nffn, nm_nfin), \
        (nv_meta, nv_nmix, nv_bg, nv_cw, nv_nmls, nv_nffn, nv_nfin) = (_unpack128(p, shapes) for p in packed)

    grad_x = dh[TOK0:].reshape(1, seq, D_MODEL)
    return (loss, grad_x,
            g_meta, g_mix, g_in, g_bias, g_conv, g_mls, g_out, g_ffn, g_gate, g_up, g_down, g_final,
            d_meta, d_nmix, d_in, d_bg, d_cw, d_nmls, d_out, d_nffn, d_gate, d_up, d_down, d_nfin,
            nm_meta, nm_nmix, nm_in, nm_bg, nm_cw, nm_nmls, nm_out, nm_nffn, nm_gate, nm_up, nm_down, nm_nfin,
            nv_meta, nv_nmix, nv_in, nv_bg, nv_cw, nv_nmls, nv_out, nv_nffn, nv_gate, nv_up, nv_down, nv_nfin)
```
